```python
import math
import numpy as np
import jax
import jax.numpy as jnp
from jax import lax

D_MODEL = 1024
BATCH = 32
SEQ = 2048
DEPTH = 2

GRID_W = 64
CTX_LEN = 256
EPS = 1e-6
D_MIX = D_MODEL
W_POOL = D_MIX // 4
W_CONV = D_MIX // 4
W_NA = D_MIX // 4
W_SSD = D_MIX - W_POOL - W_CONV - W_NA
POOL_WINDOWS = (2, 4, 8, 16)
POOL_GROUP = W_POOL // len(POOL_WINDOWS)
CONV_K = 3
NA_HEADS = 4
NA_HEAD_DIM = W_NA // NA_HEADS
WIN_R = 8
WIN_C = 16
COL_BLOCK = 16
COL_BAND = 32
SSD_HEAD_DIM = 64
SSD_HEADS = W_SSD // SSD_HEAD_DIM
SSD_GROUPS = 2
SSD_STATE = 128
SSD_CONV_K = 3
SSD_CHUNK = 128
SSD_XBC = W_SSD + 2 * SSD_GROUPS * SSD_STATE
D_IN_PROJ = W_POOL + 3 * W_CONV + 3 * W_NA + W_SSD + SSD_XBC + 2 * SSD_HEADS
PROJ_SPLITS = (W_POOL, W_POOL + 3 * W_CONV, W_POOL + 3 * W_CONV + 3 * W_NA)
D_FF = 2816
N_EXPERTS = 8
TOP_K = 2
D_FF_EXPERT = 1408

kernel_name = 'hybrid_pool_conv_na_ssd_moe_dit'


def rmsnorm(x, g):
    xf = x.astype(jnp.float32)
    y = xf * lax.rsqrt(jnp.mean(xf * xf, axis=-1, keepdims=True) + EPS)
    return (y * g.astype(jnp.float32)).astype(x.dtype)


def dwconv_centered(u, w, b=None):
    k = w.shape[0]
    L = u.shape[1]
    p = k // 2
    up = jnp.pad(u, ((0, 0), (p, p), (0, 0)))
    y = up[:, 0:L] * w[0]
    for i in range(1, k):
        y = y + up[:, i:i + L] * w[i]
    return y if b is None else y + b


def pool_mix(u, w_grp, scale):
    bn, L, _ = u.shape
    s = jnp.pad(jnp.cumsum(u.astype(jnp.float32), axis=1), ((0, 0), (1, 0), (0, 0)))
    t = np.arange(L)
    outs = []
    for g, win in enumerate(POOL_WINDOWS):
        lo = np.clip(t - win // 2, 0, L)
        hi = np.clip(t - win // 2 + win, 0, L)
        sg = s[..., g * POOL_GROUP:(g + 1) * POOL_GROUP]
        cnt = jnp.asarray((hi - lo)[:, None], jnp.float32)
        mean = (sg[:, hi] - sg[:, lo]) / cnt
        outs.append(mean.astype(u.dtype) - u[..., g * POOL_GROUP:(g + 1) * POOL_GROUP])
    p = jnp.stack(outs, axis=2)
    y = jnp.einsum('blgc,gcd->blgd', p, w_grp).reshape(bn, L, W_POOL)
    return y * scale


def short_conv_mix(pr, w_conv):
    h, bg, cg = jnp.split(pr, 3, axis=-1)
    return bg * dwconv_centered(cg * h, w_conv)


def na_geometry():
    n_cb = GRID_W // COL_BLOCK
    c = np.arange(GRID_W).reshape(n_cb, COL_BLOCK)
    sc = np.clip(c - WIN_C // 2, 0, GRID_W - WIN_C)
    band0 = np.clip(np.arange(n_cb) * COL_BLOCK - WIN_C // 2, 0, GRID_W - COL_BAND)
    band_cols = band0[:, None] + np.arange(COL_BAND)
    kc = band_cols[:, None, :]
    col_mask = (kc >= sc[..., None]) & (kc < sc[..., None] + WIN_C)
    dc_idx = np.clip(kc - c[..., None] + WIN_C - 1, 0, 2 * WIN_C - 2)
    return band_cols, col_mask, dc_idx


def na_attention(q, k, v, k_ctx, v_ctx, rpb):
    bn, S, H, hd = q.shape
    rows = S // GRID_W
    kr = min(WIN_R, rows)
    band_cols, col_mask, dc_idx = na_geometry()
    n_cb = band_cols.shape[0]
    qg = q.reshape(bn, rows, n_cb, COL_BLOCK, H, hd)
    kg = k.reshape(bn, rows, GRID_W, H, hd)
    vg = v.reshape(bn, rows, GRID_W, H, hd)
    sm = 1.0 / math.sqrt(hd)
    mask = col_mask[:, :, None, :]

    def row_fn(r):
        sr = jnp.clip(r - kr // 2, 0, rows - kr)
        q_r = lax.dynamic_index_in_dim(qg, r, axis=1, keepdims=False)
        k_r = lax.dynamic_slice_in_dim(kg, sr, kr, axis=1)[:, :, band_cols]
        v_r = lax.dynamic_slice_in_dim(vg, sr, kr, axis=1)[:, :, band_cols]
        s_win = jnp.einsum('bjqhd,bijkhd->bhjqik', q_r, k_r).astype(jnp.float32) * sm
        ri = sr + jnp.arange(kr) - r + WIN_R - 1
        bias = jnp.take(rpb[:, ri], dc_idx, axis=2).transpose(0, 2, 3, 1, 4)
        s_win = jnp.where(mask, s_win + bias.astype(jnp.float32), -jnp.inf)
        s_win = s_win.reshape(bn, H, n_cb, COL_BLOCK, kr * COL_BAND)
        s_ctx = jnp.einsum('bjqhd,bkhd->bhjqk', q_r, k_ctx).astype(jnp.float32) * sm
        p = jax.nn.softmax(jnp.concatenate([s_win, s_ctx], axis=-1), axis=-1).astype(q.dtype)
        p_win = p[..., :kr * COL_BAND].reshape(bn, H, n_cb, COL_BLOCK, kr, COL_BAND)
        p_ctx = p[..., kr * COL_BAND:]
        return (jnp.einsum('bhjqik,bijkhd->bjqhd', p_win, v_r)
                + jnp.einsum('bhjqk,bkhd->bjqhd', p_ctx, v_ctx))

    o = lax.map(row_fn, jnp.arange(rows))
    return jnp.moveaxis(o, 0, 1).reshape(bn, S, H * hd)


def ctx_attention(q, k, v):
    bn, L, H, hd = q.shape
    s = jnp.einsum('bqhd,bkhd->bhqk', q, k).astype(jnp.float32) * (1.0 / math.sqrt(hd))
    p = jax.nn.softmax(s, axis=-1).astype(q.dtype)
    return jnp.einsum('bhqk,bkhd->bqhd', p, v).reshape(bn, L, H * hd)


def ssd_inputs(pr, conv_w, conv_b):
    bn, L, _ = pr.shape
    z, xbc, dt_raw = jnp.split(pr, [W_SSD, W_SSD + SSD_XBC], axis=-1)
    xbc = jax.nn.silu(dwconv_centered(xbc, conv_w, conv_b))
    xs, bs, cs = jnp.split(xbc, [W_SSD, W_SSD + SSD_GROUPS * SSD_STATE], axis=-1)
    xs = xs.reshape(bn, L, SSD_HEADS, SSD_HEAD_DIM)
    bs = bs.reshape(bn, L, SSD_GROUPS, SSD_STATE)
    cs = cs.reshape(bn, L, SSD_GROUPS, SSD_STATE)
    return z, xs, bs, cs, dt_raw


def ssd_chunked(x, dt, a, bmat, cmat, h0, need_y):
    bn, L, H, P = x.shape
    T = SSD_CHUNK
    nc = L // T
    f32 = jnp.float32
    rep = H // bmat.shape[2]
    bh = jnp.repeat(bmat.astype(f32), rep, axis=2).reshape(bn, nc, T, H, -1)
    ch = jnp.repeat(cmat.astype(f32), rep, axis=2).reshape(bn, nc, T, H, -1)
    xdt = (x.astype(f32) * dt[..., None]).reshape(bn, nc, T, H, P)
    la = (dt * a).reshape(bn, nc, T, H).transpose(0, 3, 1, 2)
    acum = jnp.cumsum(la, axis=-1)
    states = jnp.einsum('bclhn,bhcl,bclhp->bchpn', bh, jnp.exp(acum[..., -1:] - acum), xdt)
    states = jnp.concatenate([h0[:, None].astype(f32), states], axis=1)
    tot = jnp.pad(jnp.cumsum(acum[..., -1], axis=-1), ((0, 0), (0, 0), (1, 0)))
    tril_c = np.tril(np.ones((nc + 1, nc + 1), dtype=bool))
    decay_chunk = jnp.exp(jnp.where(tril_c, tot[..., :, None] - tot[..., None, :], -jnp.inf))
    states = jnp.einsum('bhzc,bchpn->bzhpn', decay_chunk, states)
    h_final = states[:, -1]
    if not need_y:
        return None, h_final
    tril_t = np.tril(np.ones((T, T), dtype=bool))
    lmat = jnp.exp(jnp.where(tril_t, acum[..., :, None] - acum[..., None, :], -jnp.inf))
    scores = jnp.einsum('bclhn,bcshn->bhcls', ch, bh) * lmat
    y = (jnp.einsum('bhcls,bcshp->bclhp', scores, xdt)
         + jnp.einsum('bclhn,bchpn,bhcl->bclhp', ch, states[:, :-1], jnp.exp(acum)))
    return y.reshape(bn, L, H, P), h_final


def rev(t, d):
    return t if d == 0 else jnp.flip(t, axis=1)


def ssd_mix(pr, pr_c, conv_w, conv_b, dt_bias, a_log, d_skip, norm_g, ctx_out):
    z, xs, bs, cs, dtr = ssd_inputs(pr, conv_w, conv_b)
    zc, xsc, bsc, csc, dtrc = ssd_inputs(pr_c, conv_w, conv_b)
    h0 = jnp.zeros((pr.shape[0], SSD_HEADS, SSD_HEAD_DIM, SSD_STATE), jnp.float32)
    y = xs.astype(jnp.float32) * d_skip.astype(jnp.float32)[:, None]
    yc = xsc.astype(jnp.float32) * d_skip.astype(jnp.float32)[:, None] if ctx_out else None
    for d in range(2):
        a = -jnp.exp(a_log[d].astype(jnp.float32))
        hs = slice(d * SSD_HEADS, (d + 1) * SSD_HEADS)
        dt_l = jax.nn.softplus(dtr[..., hs].astype(jnp.float32) + dt_bias[d].astype(jnp.float32))
        dt_c = jax.nn.softplus(dtrc[..., hs].astype(jnp.float32) + dt_bias[d].astype(jnp.float32))
        y_cd, h_c = ssd_chunked(rev(xsc, d), rev(dt_c, d), a, rev(bsc, d), rev(csc, d), h0, ctx_out)
        y_ld, _ = ssd_chunked(rev(xs, d), rev(dt_l, d), a, rev(bs, d), rev(cs, d), h_c, True)
        y = y + rev(y_ld, d)
        if ctx_out:
            yc = yc + rev(y_cd, d)
    o = rmsnorm(y.reshape(z.shape).astype(z.dtype) * jax.nn.silu(z), norm_g)
    if not ctx_out:
        return o, None
    oc = rmsnorm(yc.reshape(zc.shape).astype(zc.dtype) * jax.nn.silu(zc), norm_g)
    return o, oc


def mixer(pr, pr_c, pool_w, pool_scale, conv_w, rpb, s_conv_w, s_conv_b, dt_bias, a_log, d_skip,
          s_norm_g, ctx_out):
    bn, S, _ = pr.shape
    Lc = pr_c.shape[1]
    p_pool, p_conv, p_na, p_ssd = jnp.split(pr, PROJ_SPLITS, axis=-1)
    c_pool, c_conv, c_na, c_ssd = jnp.split(pr_c, PROJ_SPLITS, axis=-1)
    q, k, v = [t.reshape(bn, S, NA_HEADS, NA_HEAD_DIM) for t in jnp.split(p_na, 3, axis=-1)]
    qc, kc, vc = [t.reshape(bn, Lc, NA_HEADS, NA_HEAD_DIM) for t in jnp.split(c_na, 3, axis=-1)]
    o_ssd, oc_ssd = ssd_mix(p_ssd, c_ssd, s_conv_w, s_conv_b, dt_bias, a_log, d_skip, s_norm_g, ctx_out)
    o = jnp.concatenate([pool_mix(p_pool, pool_w, pool_scale),
                         short_conv_mix(p_conv, conv_w),
                         na_attention(q, k, v, kc, vc, rpb),
                         o_ssd], axis=-1)
    if not ctx_out:
        return o, None
    oc = jnp.concatenate([pool_mix(c_pool, pool_w, pool_scale),
                          short_conv_mix(c_conv, conv_w),
                          ctx_attention(qc, kc, vc),
                          oc_ssd], axis=-1)
    return o, oc


def swiglu(h, w_gu, w_down):
    g, u = jnp.split(h @ w_gu, 2, axis=-1)
    return (jax.nn.silu(g) * u) @ w_down


def moe_swiglu(h, w_router, w_gu, w_down):
    logits = (h @ w_router).astype(jnp.float32)
    top_v, top_i = lax.top_k(logits, TOP_K)
    gates = jax.nn.softmax(top_v, axis=-1)
    out = jnp.zeros_like(h)
    for e in range(N_EXPERTS):
        w_e = jnp.sum(jnp.where(top_i == e, gates, 0.0), axis=-1).astype(h.dtype)
        out = out + w_e[..., None] * swiglu(h, w_gu[e], w_down[e])
    return out


def setup_inputs(seed: int = 0) -> dict:
    key = jax.random.key(seed)
    keys = jax.random.split(key, 32)
    n_dense = (DEPTH + 1) // 2
    n_moe = DEPTH // 2

    def nrm(i, shape, s):
        return jax.random.normal(keys[i], shape, jnp.float32) * s

    dt0 = jnp.exp(jax.random.uniform(keys[17], (DEPTH, 2, SSD_HEADS), jnp.float32,
                                     minval=math.log(1e-3), maxval=math.log(1e-1)))
    return {
        'x': nrm(0, (BATCH, SEQ, D_MODEL), 1.0),
        'c': nrm(1, (BATCH, D_MODEL), 1.0),
        'ctx': nrm(2, (BATCH, CTX_LEN, D_MODEL), 1.0),
        'c_ctx': nrm(3, (D_MODEL,), 1.0),
        'w_ada': nrm(4, (DEPTH, D_MODEL, 6 * D_MODEL), D_MODEL ** -0.5),
        'b_ada': nrm(5, (DEPTH, 6 * D_MODEL), 0.02),
        'g_mix': 1.0 + nrm(6, (DEPTH, D_MODEL), 0.05),
        'g_ffn': 1.0 + nrm(7, (DEPTH, D_MODEL), 0.05),
        'w_in': nrm(8, (DEPTH, D_MODEL, D_IN_PROJ), D_MODEL ** -0.5),
        'w_out': nrm(9, (DEPTH, D_MIX, D_MODEL), D_MIX ** -0.5),
        'pool_w': nrm(10, (DEPTH, len(POOL_WINDOWS), POOL_GROUP, POOL_GROUP), POOL_GROUP ** -0.5),
        'pool_scale': 1.0 + nrm(11, (DEPTH, W_POOL), 0.1),
        'conv_w': nrm(12, (DEPTH, CONV_K, W_CONV), CONV_K ** -0.5),
        'na_rpb': nrm(13, (DEPTH, NA_HEADS, 2 * WIN_R - 1, 2 * WIN_C - 1), 0.1),
        'ssd_conv_w': nrm(14, (DEPTH, SSD_CONV_K, SSD_XBC), SSD_CONV_K ** -0.5),
        'ssd_conv_b': nrm(15, (DEPTH, SSD_XBC), 0.01),
        'ssd_dt_bias': dt0 + jnp.log(-jnp.expm1(-dt0)),
        'ssd_a_log': jnp.log(jax.random.uniform(keys[16], (DEPTH, 2, SSD_HEADS), jnp.float32,
                                                minval=1.0, maxval=16.0)),
        'ssd_d': 1.0 + nrm(18, (DEPTH, SSD_HEADS), 0.1),
        'ssd_norm_g': 1.0 + nrm(19, (DEPTH, W_SSD), 0.05),
        'ffn_w_gu': nrm(20, (n_dense, D_MODEL, 2 * D_FF), D_MODEL ** -0.5),
        'ffn_w_down': nrm(21, (n_dense, D_FF, D_MODEL), D_FF ** -0.5),
        'moe_router': nrm(22, (n_moe, D_MODEL, N_EXPERTS), D_MODEL ** -0.5),
        'moe_w_gu': nrm(23, (n_moe, N_EXPERTS, D_MODEL, 2 * D_FF_EXPERT), D_MODEL ** -0.5),
        'moe_w_down': nrm(24, (n_moe, N_EXPERTS, D_FF_EXPERT, D_MODEL), D_FF_EXPERT ** -0.5),
        'g_final': 1.0 + nrm(25, (D_MODEL,), 0.05),
    }


def reference(x, c, ctx, c_ctx, w_ada, b_ada, g_mix, g_ffn, w_in, w_out, pool_w, pool_scale, conv_w,
              na_rpb, ssd_conv_w, ssd_conv_b, ssd_dt_bias, ssd_a_log, ssd_d, ssd_norm_g, ffn_w_gu,
              ffn_w_down, moe_router, moe_w_gu, moe_w_down, g_final):
    s_c = jax.nn.silu(c)
    s_cc = jax.nn.silu(c_ctx)
    xc = ctx
    for l in range(DEPTH):
        ctx_out = l < DEPTH - 1
        sh1, sc1, gt1, sh2, sc2, gt2 = [m[:, None, :] for m in
                                        jnp.split(s_c @ w_ada[l] + b_ada[l], 6, axis=-1)]
        sh1c, sc1c, gt1c, sh2c, sc2c, gt2c = jnp.split(s_cc @ w_ada[l] + b_ada[l], 6, axis=-1)
        h = rmsnorm(x, g_mix[l]) * (1.0 + sc1) + sh1
        hc = rmsnorm(xc, g_mix[l]) * (1.0 + sc1c) + sh1c
        o, oc = mixer(h @ w_in[l], hc @ w_in[l], pool_w[l], pool_scale[l], conv_w[l], na_rpb[l],
                      ssd_conv_w[l], ssd_conv_b[l], ssd_dt_bias[l], ssd_a_log[l], ssd_d[l],
                      ssd_norm_g[l], ctx_out)
        x = x + gt1 * (o @ w_out[l])
        j = l // 2
        h = rmsnorm(x, g_ffn[l]) * (1.0 + sc2) + sh2
        if l % 2 == 0:
            x = x + gt2 * swiglu(h, ffn_w_gu[j], ffn_w_down[j])
        else:
            x = x + gt2 * moe_swiglu(h, moe_router[j], moe_w_gu[j], moe_w_down[j])
        if ctx_out:
            xc = xc + gt1c * (oc @ w_out[l])
            hc = rmsnorm(xc, g_ffn[l]) * (1.0 + sc2c) + sh2c
            if l % 2 == 0:
                xc = xc + gt2c * swiglu(hc, ffn_w_gu[j], ffn_w_down[j])
            else:
                xc = xc + gt2c * moe_swiglu(hc, moe_router[j], moe_w_gu[j], moe_w_down[j])
    return rmsnorm(x, g_final)
```

```python
import functools
import math

import numpy as np
import jax
import jax.numpy as jnp
from jax import lax
from jax.experimental import pallas as pl
from jax.experimental.pallas import tpu as pltpu

D_MODEL = 1024
DEPTH = 2
GRID_W = 64
EPS = 1e-6
W_POOL = 256
W_CONV = 256
W_NA = 256
W_SSD = 256
POOL_WINDOWS = (2, 4, 8, 16)
POOL_GROUP = W_POOL // len(POOL_WINDOWS)
NA_HEADS = 4
NA_HEAD_DIM = W_NA // NA_HEADS
WIN_R = 8
WIN_C = 16
COL_BLOCK = 16
COL_BAND = 32
SSD_HEAD_DIM = 64
SSD_HEADS = W_SSD // SSD_HEAD_DIM
SSD_GROUPS = 2
SSD_STATE = 128
SSD_CHUNK = 128
SSD_XBC = W_SSD + 2 * SSD_GROUPS * SSD_STATE
D_IN_PROJ = W_POOL + 3 * W_CONV + 3 * W_NA + W_SSD + SSD_XBC + 2 * SSD_HEADS
PROJ_SPLITS = (W_POOL, W_POOL + 3 * W_CONV, W_POOL + 3 * W_CONV + 3 * W_NA)
D_FF = 2816
N_EXPERTS = 8
TOP_K = 2
D_FF_EXPERT = 1408

LANES = 128
D_IN_PAD = -(-D_IN_PROJ // LANES) * LANES
VMEM_LIMIT = 56 * 1024 * 1024
BF16 = jnp.bfloat16
F32 = jnp.float32


def _params(*sem):
    return pltpu.CompilerParams(dimension_semantics=sem, vmem_limit_bytes=VMEM_LIMIT)


def _norm_mod(x, g, scale, shift):
    ms = jnp.mean(x * x, axis=-1, keepdims=True)
    return (x * lax.rsqrt(ms + EPS) * g) * (1.0 + scale) + shift


def _mod_map(tiles_per_batch):
    if tiles_per_batch is None:
        return lambda i, *_: (0, 0, 0)
    return lambda i, *_: (i // tiles_per_batch, 0, 0)


def _ada_kernel(c_ref, w_ref, b_ref, o_ref):
    c = c_ref[...]
    s = c * jax.nn.sigmoid(c)
    o_ref[...] = jnp.dot(s.astype(BF16), w_ref[...], preferred_element_type=F32) + b_ref[...]


def ada_modulation(c_rows, w, b):
    r, d = c_rows.shape
    n = w.shape[1]
    tn = 1536
    return pl.pallas_call(
        _ada_kernel,
        out_shape=jax.ShapeDtypeStruct((r, n), F32),
        grid=(n // tn,),
        in_specs=[pl.BlockSpec((r, d), lambda j: (0, 0)),
                  pl.BlockSpec((d, tn), lambda j: (0, j)),
                  pl.BlockSpec((1, tn), lambda j: (0, j))],
        out_specs=pl.BlockSpec((r, tn), lambda j: (0, j)),
        compiler_params=_params("arbitrary"),
        name="ada_modulation",
    )(c_rows, w, b)


def _in_proj_kernel(x_ref, g_ref, mod_ref, w_ref, o_ref):
    h = _norm_mod(x_ref[...], g_ref[...], mod_ref[0, 1:2, :], mod_ref[0, 0:1, :])
    o_ref[...] = jnp.dot(h.astype(BF16), w_ref[...], preferred_element_type=F32).astype(o_ref.dtype)


def in_proj(x2, g, mod, w, tiles_per_batch, tm, out_dtype=F32):
    m, d = x2.shape
    n = w.shape[1]
    return pl.pallas_call(
        _in_proj_kernel,
        out_shape=jax.ShapeDtypeStruct((m, n), out_dtype),
        grid=(m // tm,),
        in_specs=[pl.BlockSpec((tm, d), lambda i: (i, 0)),
                  pl.BlockSpec((1, d), lambda i: (0, 0)),
                  pl.BlockSpec((1, 6, d), _mod_map(tiles_per_batch)),
                  pl.BlockSpec((d, n), lambda i: (0, 0))],
        out_specs=pl.BlockSpec((tm, n), lambda i: (i, 0)),
        compiler_params=_params("parallel"),
        name="in_proj",
    )(x2, g, mod, w)


def _out_proj_kernel(x_ref, o_ref, mod_ref, w_ref, y_ref):
    y = jnp.dot(o_ref[...].astype(BF16), w_ref[...], preferred_element_type=F32)
    y_ref[...] = x_ref[...] + mod_ref[0, 2:3, :] * y


def out_proj(x2, o2, mod, w, tiles_per_batch, tm):
    m, d = x2.shape
    k = o2.shape[1]
    return pl.pallas_call(
        _out_proj_kernel,
        out_shape=jax.ShapeDtypeStruct((m, d), F32),
        grid=(m // tm,),
        in_specs=[pl.BlockSpec((tm, d), lambda i: (i, 0)),
                  pl.BlockSpec((tm, k), lambda i: (i, 0)),
                  pl.BlockSpec((1, 6, d), _mod_map(tiles_per_batch)),
                  pl.BlockSpec((k, d), lambda i: (0, 0))],
        out_specs=pl.BlockSpec((tm, d), lambda i: (i, 0)),
        compiler_params=_params("parallel"),
        name="out_proj",
    )(x2, o2, mod, w)


def _ffn_kernel(x_ref, g_ref, mod_ref, wg_ref, wu_ref, wd_ref, y_ref, h_ref, acc_ref):
    f = pl.program_id(1)

    @pl.when(f == 0)
    def _():
        h = _norm_mod(x_ref[...], g_ref[...], mod_ref[0, 4:5, :], mod_ref[0, 3:4, :])
        h_ref[...] = h.astype(BF16)
        acc_ref[...] = jnp.zeros_like(acc_ref)

    h = h_ref[...]
    gg = jnp.dot(h, wg_ref[...], preferred_element_type=F32)
    uu = jnp.dot(h, wu_ref[...], preferred_element_type=F32)
    a = (gg * jax.nn.sigmoid(gg) * uu).astype(BF16)
    acc_ref[...] += jnp.dot(a, wd_ref[...], preferred_element_type=F32)

    @pl.when(f == pl.num_programs(1) - 1)
    def _():
        y_ref[...] = x_ref[...] + mod_ref[0, 5:6, :] * acc_ref[...]


def ffn_dense(x2, g, mod, w_gu, w_down, tiles_per_batch, tm, tf):
    m, d = x2.shape
    ff = w_down.shape[0]
    nf = ff // tf
    return pl.pallas_call(
        _ffn_kernel,
        out_shape=jax.ShapeDtypeStruct((m, d), F32),
        grid=(m // tm, nf),
        in_specs=[pl.BlockSpec((tm, d), lambda i, f: (i, 0)),
                  pl.BlockSpec((1, d), lambda i, f: (0, 0)),
                  pl.BlockSpec((1, 6, d), _mod_map(tiles_per_batch)),
                  pl.BlockSpec((d, tf), lambda i, f: (0, f)),
                  pl.BlockSpec((d, tf), lambda i, f: (0, nf + f)),
                  pl.BlockSpec((tf, d), lambda i, f: (f, 0))],
        out_specs=pl.BlockSpec((tm, d), lambda i, f: (i, 0)),
        scratch_shapes=[pltpu.VMEM((tm, d), BF16), pltpu.VMEM((tm, d), F32)],
        compiler_params=_params("parallel", "arbitrary"),
        name="ffn_dense",
    )(x2, g, mod, w_gu, w_gu, w_down)


def _router_kernel(x_ref, g_ref, mod_ref, wr_ref, h_ref, gate_ref):
    h = _norm_mod(x_ref[...], g_ref[...], mod_ref[0, 4:5, :], mod_ref[0, 3:4, :]).astype(BF16)
    h_ref[...] = h
    logits = jnp.dot(h, wr_ref[...], preferred_element_type=F32)
    lane = lax.broadcasted_iota(jnp.int32, logits.shape, 1)
    neg = jnp.float32(-jnp.inf)
    logits = jnp.where(lane < N_EXPERTS, logits, neg)
    m1 = jnp.max(logits, axis=-1, keepdims=True)
    i1 = jnp.min(jnp.where(logits == m1, lane, LANES), axis=-1, keepdims=True)
    rest = jnp.where(lane == i1, neg, logits)
    m2 = jnp.max(rest, axis=-1, keepdims=True)
    i2 = jnp.min(jnp.where(rest == m2, lane, LANES), axis=-1, keepdims=True)
    e2 = jnp.exp(m2 - m1)
    g1 = 1.0 / (1.0 + e2)
    g2 = e2 / (1.0 + e2)
    gate_ref[...] = jnp.where(lane == i1, g1, jnp.where(lane == i2, g2, -1.0))


def moe_router(x2, g, mod, w_router, tiles_per_batch, tm):
    m, d = x2.shape
    return pl.pallas_call(
        _router_kernel,
        out_shape=(jax.ShapeDtypeStruct((m, d), BF16), jax.ShapeDtypeStruct((m, LANES), F32)),
        grid=(m // tm,),
        in_specs=[pl.BlockSpec((tm, d), lambda i: (i, 0)),
                  pl.BlockSpec((1, d), lambda i: (0, 0)),
                  pl.BlockSpec((1, 6, d), _mod_map(tiles_per_batch)),
                  pl.BlockSpec((d, LANES), lambda i: (0, 0))],
        out_specs=(pl.BlockSpec((tm, d), lambda i: (i, 0)),
                   pl.BlockSpec((tm, LANES), lambda i: (i, 0))),
        compiler_params=_params("parallel"),
        name="moe_router",
    )(x2, g, mod, w_router)


def _moe_kernel(te_ref, nt_ref, xg_ref, gate_ref, wg_ref, wu_ref, wd_ref, y_ref):
    i = pl.program_id(0)

    @pl.when(i < nt_ref[0])
    def _():
        h = xg_ref[...]
        gg = jnp.dot(h, wg_ref[0], preferred_element_type=F32)
        uu = jnp.dot(h, wu_ref[0], preferred_element_type=F32)
        a = (gg * jax.nn.sigmoid(gg) * uu).astype(BF16)
        y = jnp.dot(a, wd_ref[0], preferred_element_type=F32)
        y_ref[...] = (gate_ref[...] * y).astype(y_ref.dtype)

    @pl.when(i >= nt_ref[0])
    def _():
        y_ref[...] = jnp.zeros_like(y_ref)


def moe_grouped(tile_expert, n_tiles_used, xg, row_gate, w_gu, w_down, tm):
    p, d = xg.shape
    fe = w_down.shape[1]
    grid_spec = pltpu.PrefetchScalarGridSpec(
        num_scalar_prefetch=2,
        grid=(p // tm,),
        in_specs=[pl.BlockSpec((tm, d), lambda i, te, nt: (i, 0)),
                  pl.BlockSpec((tm, 1), lambda i, te, nt: (i, 0)),
                  pl.BlockSpec((1, d, fe), lambda i, te, nt: (te[i], 0, 0)),
                  pl.BlockSpec((1, d, fe), lambda i, te, nt: (te[i], 0, 1)),
                  pl.BlockSpec((1, fe, d), lambda i, te, nt: (te[i], 0, 0))],
        out_specs=pl.BlockSpec((tm, d), lambda i, te, nt: (i, 0)),
    )
    return pl.pallas_call(
        _moe_kernel,
        out_shape=jax.ShapeDtypeStruct((p, d), F32),
        grid_spec=grid_spec,
        compiler_params=_params("arbitrary"),
        name="moe_grouped",
    )(tile_expert, n_tiles_used, xg, row_gate, w_gu, w_gu, w_down)


def moe_block(x2, g, mod, w_router, w_gu, w_down, tiles_per_batch, tm_tok, tm_grp):
    m, d = x2.shape
    h, gate_dense = moe_router(x2, g, mod, w_router, tiles_per_batch, tm_tok)
    gates = gate_dense[:, :N_EXPERTS]
    sel = gates >= 0.0
    rank = jnp.cumsum(sel.astype(jnp.int32), axis=0) - 1
    cnt = jnp.sum(sel.astype(jnp.int32), axis=0)
    gsize = ((cnt + tm_grp - 1) // tm_grp) * tm_grp
    gend = jnp.cumsum(gsize)
    gstart = gend - gsize
    p_max = m * TOP_K + N_EXPERTS * tm_grp
    n_tiles = p_max // tm_grp
    dest = jnp.where(sel, gstart[None, :] + rank, p_max)
    d_lo = jnp.min(dest, axis=1)
    d_hi = jnp.max(jnp.where(sel, dest, -1), axis=1)
    g_lo = jnp.take_along_axis(gates, jnp.argmin(dest, axis=1)[:, None], axis=1)[:, 0]
    g_hi = jnp.sum(jnp.where(sel, gates, 0.0), axis=1) - g_lo
    tok = jnp.arange(m, dtype=jnp.int32)
    dests = jnp.concatenate([d_lo, d_hi])
    src = jnp.zeros((p_max,), jnp.int32).at[dests].set(jnp.concatenate([tok, tok]), unique_indices=True)
    row_gate = jnp.zeros((p_max,), F32).at[dests].set(jnp.concatenate([g_lo, g_hi]), unique_indices=True)
    tile_start = jnp.arange(n_tiles, dtype=jnp.int32) * tm_grp
    tile_expert = jnp.minimum(jnp.sum((tile_start[:, None] >= gend[None, :]).astype(jnp.int32), axis=1),
                              N_EXPERTS - 1).astype(jnp.int32)
    n_used = (gend[-1] // tm_grp).astype(jnp.int32).reshape(1)
    xg = jnp.take(h, src, axis=0)
    yg = moe_grouped(tile_expert, n_used, xg, row_gate[:, None], w_gu, w_down, tm_grp)
    return jnp.take(yg, d_lo, axis=0) + jnp.take(yg, d_hi, axis=0)


def _residual_kernel(x_ref, y_ref, mod_ref, o_ref):
    o_ref[...] = x_ref[...] + mod_ref[0, 5:6, :] * y_ref[...]


def _residual_norm_kernel(x_ref, y_ref, mod_ref, g_ref, o_ref):
    x = x_ref[...] + mod_ref[0, 5:6, :] * y_ref[...]
    ms = jnp.mean(x * x, axis=-1, keepdims=True)
    o_ref[...] = x * lax.rsqrt(ms + EPS) * g_ref[...]


def residual_gate(x2, y2, mod, tiles_per_batch, tm, g_final=None):
    m, d = x2.shape
    row = pl.BlockSpec((tm, d), lambda i: (i, 0))
    specs = [row, row, pl.BlockSpec((1, 6, d), _mod_map(tiles_per_batch))]
    args = [x2, y2, mod]
    body = _residual_kernel
    if g_final is not None:
        specs.append(pl.BlockSpec((1, d), lambda i: (0, 0)))
        args.append(g_final)
        body = _residual_norm_kernel
    return pl.pallas_call(
        body,
        out_shape=jax.ShapeDtypeStruct((m, d), F32),
        grid=(m // tm,),
        in_specs=specs,
        out_specs=row,
        compiler_params=_params("parallel"),
        name="residual_gate",
    )(*args)


def _final_norm_kernel(x_ref, g_ref, o_ref):
    x = x_ref[...]
    ms = jnp.mean(x * x, axis=-1, keepdims=True)
    o_ref[...] = x * lax.rsqrt(ms + EPS) * g_ref[...]


def final_norm(x2, g, tm):
    m, d = x2.shape
    return pl.pallas_call(
        _final_norm_kernel,
        out_shape=jax.ShapeDtypeStruct((m, d), F32),
        grid=(m // tm,),
        in_specs=[pl.BlockSpec((tm, d), lambda i: (i, 0)), pl.BlockSpec((1, d), lambda i: (0, 0))],
        out_specs=pl.BlockSpec((tm, d), lambda i: (i, 0)),
        compiler_params=_params("parallel"),
        name="final_norm",
    )(x2, g)


def _rmsnorm(x, g):
    xf = x.astype(F32)
    return xf * lax.rsqrt(jnp.mean(xf * xf, axis=-1, keepdims=True) + EPS) * g


def _dwconv_centered(u, w, b=None):
    k = w.shape[0]
    L = u.shape[1]
    p = k // 2
    up = jnp.pad(u, ((0, 0), (p, p), (0, 0)))
    y = up[:, 0:L] * w[0]
    for i in range(1, k):
        y = y + up[:, i:i + L] * w[i]
    return y if b is None else y + b


def _pool_mix(u, w_grp, scale):
    bn, L, _ = u.shape
    s = jnp.pad(jnp.cumsum(u.astype(F32), axis=1), ((0, 0), (1, 0), (0, 0)))
    t = np.arange(L)
    outs = []
    for g, win in enumerate(POOL_WINDOWS):
        lo = np.clip(t - win // 2, 0, L)
        hi = np.clip(t - win // 2 + win, 0, L)
        sg = s[..., g * POOL_GROUP:(g + 1) * POOL_GROUP]
        cnt = jnp.asarray((hi - lo)[:, None], F32)
        mean = (sg[:, hi] - sg[:, lo]) / cnt
        outs.append(mean - u[..., g * POOL_GROUP:(g + 1) * POOL_GROUP])
    p = jnp.stack(outs, axis=2)
    y = jnp.einsum('blgc,gcd->blgd', p, w_grp).reshape(bn, L, W_POOL)
    return y * scale


def _short_conv_mix(pr, w_conv):
    h, bg, cg = jnp.split(pr, 3, axis=-1)
    return bg * _dwconv_centered(cg * h, w_conv)


def _na_geometry():
    n_cb = GRID_W // COL_BLOCK
    c = np.arange(GRID_W).reshape(n_cb, COL_BLOCK)
    sc = np.clip(c - WIN_C // 2, 0, GRID_W - WIN_C)
    band0 = np.clip(np.arange(n_cb) * COL_BLOCK - WIN_C // 2, 0, GRID_W - COL_BAND)
    band_cols = band0[:, None] + np.arange(COL_BAND)
    kc = band_cols[:, None, :]
    col_mask = (kc >= sc[..., None]) & (kc < sc[..., None] + WIN_C)
    dc_idx = np.clip(kc - c[..., None] + WIN_C - 1, 0, 2 * WIN_C - 2)
    return band_cols, col_mask, dc_idx


def _na_attention(q, k, v, k_ctx, v_ctx, rpb):
    bn, S, H, hd = q.shape
    rows = S // GRID_W
    kr = min(WIN_R, rows)
    band_cols, col_mask, dc_idx = _na_geometry()
    n_cb = band_cols.shape[0]
    qg = q.reshape(bn, rows, n_cb, COL_BLOCK, H, hd)
    kg = k.reshape(bn, rows, GRID_W, H, hd)
    vg = v.reshape(bn, rows, GRID_W, H, hd)
    sm = 1.0 / math.sqrt(hd)
    mask = col_mask[:, :, None, :]

    def row_fn(r):
        sr = jnp.clip(r - kr // 2, 0, rows - kr)
        q_r = lax.dynamic_index_in_dim(qg, r, axis=1, keepdims=False)
        k_r = lax.dynamic_slice_in_dim(kg, sr, kr, axis=1)[:, :, band_cols]
        v_r = lax.dynamic_slice_in_dim(vg, sr, kr, axis=1)[:, :, band_cols]
        s_win = jnp.einsum('bjqhd,bijkhd->bhjqik', q_r, k_r).astype(F32) * sm
        ri = sr + jnp.arange(kr) - r + WIN_R - 1
        bias = jnp.take(rpb[:, ri], dc_idx, axis=2).transpose(0, 2, 3, 1, 4)
        s_win = jnp.where(mask, s_win + bias.astype(F32), -jnp.inf)
        s_win = s_win.reshape(bn, H, n_cb, COL_BLOCK, kr * COL_BAND)
        s_ctx = jnp.einsum('bjqhd,bkhd->bhjqk', q_r, k_ctx).astype(F32) * sm
        p = jax.nn.softmax(jnp.concatenate([s_win, s_ctx], axis=-1), axis=-1)
        p_win = p[..., :kr * COL_BAND].reshape(bn, H, n_cb, COL_BLOCK, kr, COL_BAND)
        p_ctx = p[..., kr * COL_BAND:]
        return (jnp.einsum('bhjqik,bijkhd->bjqhd', p_win, v_r)
                + jnp.einsum('bhjqk,bkhd->bjqhd', p_ctx, v_ctx))

    o = lax.map(row_fn, jnp.arange(rows))
    return jnp.moveaxis(o, 0, 1).reshape(bn, S, H * hd)


def _ctx_attention(q, k, v):
    bn, L, H, hd = q.shape
    s = jnp.einsum('bqhd,bkhd->bhqk', q, k).astype(F32) * (1.0 / math.sqrt(hd))
    p = jax.nn.softmax(s, axis=-1)
    return jnp.einsum('bhqk,bkhd->bqhd', p, v).reshape(bn, L, H * hd)


def _ssd_inputs(pr, conv_w, conv_b):
    bn, L, _ = pr.shape
    z, xbc, dt_raw = jnp.split(pr, [W_SSD, W_SSD + SSD_XBC], axis=-1)
    xbc = jax.nn.silu(_dwconv_centered(xbc, conv_w, conv_b))
    xs, bs, cs = jnp.split(xbc, [W_SSD, W_SSD + SSD_GROUPS * SSD_STATE], axis=-1)
    xs = xs.reshape(bn, L, SSD_HEADS, SSD_HEAD_DIM)
    bs = bs.reshape(bn, L, SSD_GROUPS, SSD_STATE)
    cs = cs.reshape(bn, L, SSD_GROUPS, SSD_STATE)
    return z, xs, bs, cs, dt_raw


def _ssd_chunked(x, dt, a, bmat, cmat, h0, need_y):
    bn, L, H, P = x.shape
    T = SSD_CHUNK
    nc = L // T
    rep = H // bmat.shape[2]
    bh = jnp.repeat(bmat.astype(F32), rep, axis=2).reshape(bn, nc, T, H, -1)
    ch = jnp.repeat(cmat.astype(F32), rep, axis=2).reshape(bn, nc, T, H, -1)
    xdt = (x.astype(F32) * dt[..., None]).reshape(bn, nc, T, H, P)
    la = (dt * a).reshape(bn, nc, T, H).transpose(0, 3, 1, 2)
    acum = jnp.cumsum(la, axis=-1)
    states = jnp.einsum('bclhn,bhcl,bclhp->bchpn', bh, jnp.exp(acum[..., -1:] - acum), xdt)
    states = jnp.concatenate([h0[:, None].astype(F32), states], axis=1)
    tot = jnp.pad(jnp.cumsum(acum[..., -1], axis=-1), ((0, 0), (0, 0), (1, 0)))
    tril_c = np.tril(np.ones((nc + 1, nc + 1), dtype=bool))
    decay_chunk = jnp.exp(jnp.where(tril_c, tot[..., :, None] - tot[..., None, :], -jnp.inf))
    states = jnp.einsum('bhzc,bchpn->bzhpn', decay_chunk, states)
    h_final = states[:, -1]
    if not need_y:
        return None, h_final
    tril_t = np.tril(np.ones((T, T), dtype=bool))
    lmat = jnp.exp(jnp.where(tril_t, acum[..., :, None] - acum[..., None, :], -jnp.inf))
    scores = jnp.einsum('bclhn,bcshn->bhcls', ch, bh) * lmat
    y = (jnp.einsum('bhcls,bcshp->bclhp', scores, xdt)
         + jnp.einsum('bclhn,bchpn,bhcl->bclhp', ch, states[:, :-1], jnp.exp(acum)))
    return y.reshape(bn, L, H, P), h_final


def _rev(t, d):
    return t if d == 0 else jnp.flip(t, axis=1)


def _ssd_mix(pr, pr_c, conv_w, conv_b, dt_bias, a_log, d_skip, norm_g, ctx_out):
    z, xs, bs, cs, dtr = _ssd_inputs(pr, conv_w, conv_b)
    zc, xsc, bsc, csc, dtrc = _ssd_inputs(pr_c, conv_w, conv_b)
    h0 = jnp.zeros((pr.shape[0], SSD_HEADS, SSD_HEAD_DIM, SSD_STATE), F32)
    y = xs * d_skip[:, None]
    yc = xsc * d_skip[:, None] if ctx_out else None
    for d in range(2):
        a = -jnp.exp(a_log[d])
        hs = slice(d * SSD_HEADS, (d + 1) * SSD_HEADS)
        dt_l = jax.nn.softplus(dtr[..., hs] + dt_bias[d])
        dt_c = jax.nn.softplus(dtrc[..., hs] + dt_bias[d])
        y_cd, h_c = _ssd_chunked(_rev(xsc, d), _rev(dt_c, d), a, _rev(bsc, d), _rev(csc, d), h0, ctx_out)
        y_ld, _ = _ssd_chunked(_rev(xs, d), _rev(dt_l, d), a, _rev(bs, d), _rev(cs, d), h_c, True)
        y = y + _rev(y_ld, d)
        if ctx_out:
            yc = yc + _rev(y_cd, d)
    o = _rmsnorm(y.reshape(z.shape) * jax.nn.silu(z), norm_g)
    if not ctx_out:
        return o, None
    oc = _rmsnorm(yc.reshape(zc.shape) * jax.nn.silu(zc), norm_g)
    return o, oc


def _mixer(pr, pr_c, pool_w, pool_scale, conv_w, rpb, s_conv_w, s_conv_b, dt_bias, a_log, d_skip,
           s_norm_g, ctx_out):
    bn, S, _ = pr.shape
    Lc = pr_c.shape[1]
    pr = pr[..., :D_IN_PROJ]
    pr_c = pr_c[..., :D_IN_PROJ]
    p_pool, p_conv, p_na, p_ssd = jnp.split(pr, PROJ_SPLITS, axis=-1)
    c_pool, c_conv, c_na, c_ssd = jnp.split(pr_c, PROJ_SPLITS, axis=-1)
    q, k, v = [t.reshape(bn, S, NA_HEADS, NA_HEAD_DIM) for t in jnp.split(p_na, 3, axis=-1)]
    qc, kc, vc = [t.reshape(bn, Lc, NA_HEADS, NA_HEAD_DIM) for t in jnp.split(c_na, 3, axis=-1)]
    o_ssd, oc_ssd = _ssd_mix(p_ssd, c_ssd, s_conv_w, s_conv_b, dt_bias, a_log, d_skip, s_norm_g, ctx_out)
    o = jnp.concatenate([_pool_mix(p_pool, pool_w, pool_scale),
                         _short_conv_mix(p_conv, conv_w),
                         _na_attention(q, k, v, kc, vc, rpb),
                         o_ssd], axis=-1)
    if not ctx_out:
        return o, None
    oc = jnp.concatenate([_pool_mix(c_pool, pool_w, pool_scale),
                          _short_conv_mix(c_conv, conv_w),
                          _ctx_attention(qc, kc, vc),
                          oc_ssd], axis=-1)
    return o, oc


TM = 512
TM_CTX = 256
TF = 1408
TM_GRP = 512


def kernel(x, c, ctx, c_ctx, w_ada, b_ada, g_mix, g_ffn, w_in, w_out, pool_w, pool_scale, conv_w, na_rpb,
           ssd_conv_w, ssd_conv_b, ssd_dt_bias, ssd_a_log, ssd_d, ssd_norm_g, ffn_w_gu, ffn_w_down,
           moe_router, moe_w_gu, moe_w_down, g_final):
    bn, S, d = x.shape
    Lc = ctx.shape[1]
    m, mc = bn * S, bn * Lc
    tpb = S // TM
    x2 = x.reshape(m, d)
    xc2 = ctx.reshape(mc, d)
    c_rows = jnp.concatenate([c, c_ctx[None, :], jnp.zeros((7, d), F32)], axis=0)
    for l in range(DEPTH):
        ctx_out = l < DEPTH - 1
        last = l == DEPTH - 1
        ada = ada_modulation(c_rows, w_ada[l].astype(BF16), b_ada[l][None, :])
        mod = ada[:bn].reshape(bn, 6, d)
        mod_c = ada[bn:bn + 1].reshape(1, 6, d)
        w_in_l = jnp.pad(w_in[l], ((0, 0), (0, D_IN_PAD - D_IN_PROJ))).astype(BF16)
        g_m = g_mix[l][None, :]
        g_f = g_ffn[l][None, :]
        pr = in_proj(x2, g_m, mod, w_in_l, tpb, TM)
        pr_c = in_proj(xc2, g_m, mod_c, w_in_l, None, TM_CTX)
        o, oc = _mixer(pr.reshape(bn, S, -1), pr_c.reshape(bn, Lc, -1), pool_w[l], pool_scale[l], conv_w[l],
                       na_rpb[l], ssd_conv_w[l], ssd_conv_b[l], ssd_dt_bias[l], ssd_a_log[l], ssd_d[l],
                       ssd_norm_g[l], ctx_out)
        w_out_l = w_out[l].astype(BF16)
        x2 = out_proj(x2, o.reshape(m, -1), mod, w_out_l, tpb, TM)
        if ctx_out:
            xc2 = out_proj(xc2, oc.reshape(mc, -1), mod_c, w_out_l, None, TM_CTX)
        j = l // 2
        if l % 2 == 0:
            w_gu = ffn_w_gu[j].astype(BF16)
            w_dn = ffn_w_down[j].astype(BF16)
            x2 = ffn_dense(x2, g_f, mod, w_gu, w_dn, tpb, TM, TF)
            if ctx_out:
                xc2 = ffn_dense(xc2, g_f, mod_c, w_gu, w_dn, None, TM_CTX, TF)
        else:
            w_r = jnp.pad(moe_router[j], ((0, 0), (0, LANES - N_EXPERTS))).astype(BF16)
            w_gu = moe_w_gu[j].astype(BF16)
            w_dn = moe_w_down[j].astype(BF16)
            y = moe_block(x2, g_f, mod, w_r, w_gu, w_dn, tpb, TM, TM_GRP)
            x2 = residual_gate(x2, y, mod, tpb, TM, g_final[None, :] if last else None)
            if ctx_out:
                yc = moe_block(xc2, g_f, mod_c, w_r, w_gu, w_dn, None, TM_CTX, TM_GRP)
                xc2 = residual_gate(xc2, yc, mod_c, None, TM_CTX)
            if last:
                return x2.reshape(bn, S, d)
    return final_norm(x2, g_final[None, :], TM).reshape(bn, S, d)
```

```python
import functools
import math

import numpy as np
import jax
import jax.numpy as jnp
from jax import lax
from jax.experimental import pallas as pl
from jax.experimental.pallas import tpu as pltpu

D_MODEL = 1024
DEPTH = 2
GRID_W = 64
EPS = 1e-6
W_POOL = 256
W_CONV = 256
W_NA = 256
W_SSD = 256
POOL_WINDOWS = (2, 4, 8, 16)
POOL_GROUP = W_POOL // len(POOL_WINDOWS)
NA_HEADS = 4
NA_HEAD_DIM = W_NA // NA_HEADS
WIN_R = 8
WIN_C = 16
COL_BLOCK = 16
COL_BAND = 32
SSD_HEAD_DIM = 64
SSD_HEADS = W_SSD // SSD_HEAD_DIM
SSD_GROUPS = 2
SSD_STATE = 128
SSD_CHUNK = 128
SSD_XBC = W_SSD + 2 * SSD_GROUPS * SSD_STATE
D_IN_PROJ = W_POOL + 3 * W_CONV + 3 * W_NA + W_SSD + SSD_XBC + 2 * SSD_HEADS
PROJ_SPLITS = (W_POOL, W_POOL + 3 * W_CONV, W_POOL + 3 * W_CONV + 3 * W_NA)
D_FF = 2816
N_EXPERTS = 8
TOP_K = 2
D_FF_EXPERT = 1408

LANES = 128
D_IN_PAD = -(-D_IN_PROJ // LANES) * LANES
VMEM_LIMIT = 56 * 1024 * 1024
BF16 = jnp.bfloat16
F32 = jnp.float32


def _params(*sem):
    return pltpu.CompilerParams(dimension_semantics=sem, vmem_limit_bytes=VMEM_LIMIT)


def _norm_mod(x, g, scale, shift):
    ms = jnp.mean(x * x, axis=-1, keepdims=True)
    return (x * lax.rsqrt(ms + EPS) * g) * (1.0 + scale) + shift


def _mod_map(tiles_per_batch):
    if tiles_per_batch is None:
        return lambda i, *_: (0, 0, 0)
    return lambda i, *_: (i // tiles_per_batch, 0, 0)


def _ada_kernel(c_ref, w_ref, b_ref, o_ref):
    c = c_ref[...]
    s = c * jax.nn.sigmoid(c)
    o_ref[...] = jnp.dot(s.astype(BF16), w_ref[...], preferred_element_type=F32) + b_ref[...]


def ada_modulation(c_rows, w, b):
    r, d = c_rows.shape
    n = w.shape[1]
    tn = 1536
    return pl.pallas_call(
        _ada_kernel,
        out_shape=jax.ShapeDtypeStruct((r, n), F32),
        grid=(n // tn,),
        in_specs=[pl.BlockSpec((r, d), lambda j: (0, 0)),
                  pl.BlockSpec((d, tn), lambda j: (0, j)),
                  pl.BlockSpec((1, tn), lambda j: (0, j))],
        out_specs=pl.BlockSpec((r, tn), lambda j: (0, j)),
        compiler_params=_params("arbitrary"),
        name="ada_modulation",
    )(c_rows, w, b)


def _in_proj_kernel(x_ref, g_ref, mod_ref, w_ref, o_ref):
    h = _norm_mod(x_ref[...], g_ref[...], mod_ref[0, 1:2, :], mod_ref[0, 0:1, :])
    o_ref[...] = jnp.dot(h.astype(BF16), w_ref[...], preferred_element_type=F32).astype(o_ref.dtype)


def in_proj(x2, g, mod, w, tiles_per_batch, tm, out_dtype=F32):
    m, d = x2.shape
    n = w.shape[1]
    return pl.pallas_call(
        _in_proj_kernel,
        out_shape=jax.ShapeDtypeStruct((m, n), out_dtype),
        grid=(m // tm,),
        in_specs=[pl.BlockSpec((tm, d), lambda i: (i, 0)),
                  pl.BlockSpec((1, d), lambda i: (0, 0)),
                  pl.BlockSpec((1, 6, d), _mod_map(tiles_per_batch)),
                  pl.BlockSpec((d, n), lambda i: (0, 0))],
        out_specs=pl.BlockSpec((tm, n), lambda i: (i, 0)),
        compiler_params=_params("parallel"),
        name="in_proj",
    )(x2, g, mod, w)


def _out_proj_kernel(x_ref, o_ref, mod_ref, w_ref, y_ref):
    y = jnp.dot(o_ref[...].astype(BF16), w_ref[...], preferred_element_type=F32)
    y_ref[...] = x_ref[...] + mod_ref[0, 2:3, :] * y


def out_proj(x2, o2, mod, w, tiles_per_batch, tm):
    m, d = x2.shape
    k = o2.shape[1]
    return pl.pallas_call(
        _out_proj_kernel,
        out_shape=jax.ShapeDtypeStruct((m, d), F32),
        grid=(m // tm,),
        in_specs=[pl.BlockSpec((tm, d), lambda i: (i, 0)),
                  pl.BlockSpec((tm, k), lambda i: (i, 0)),
                  pl.BlockSpec((1, 6, d), _mod_map(tiles_per_batch)),
                  pl.BlockSpec((k, d), lambda i: (0, 0))],
        out_specs=pl.BlockSpec((tm, d), lambda i: (i, 0)),
        compiler_params=_params("parallel"),
        name="out_proj",
    )(x2, o2, mod, w)


def _ffn_kernel(x_ref, g_ref, mod_ref, wg_ref, wu_ref, wd_ref, y_ref, h_ref, acc_ref):
    f = pl.program_id(1)

    @pl.when(f == 0)
    def _():
        h = _norm_mod(x_ref[...], g_ref[...], mod_ref[0, 4:5, :], mod_ref[0, 3:4, :])
        h_ref[...] = h.astype(BF16)
        acc_ref[...] = jnp.zeros_like(acc_ref)

    h = h_ref[...]
    gg = jnp.dot(h, wg_ref[...], preferred_element_type=F32)
    uu = jnp.dot(h, wu_ref[...], preferred_element_type=F32)
    a = (gg * jax.nn.sigmoid(gg) * uu).astype(BF16)
    acc_ref[...] += jnp.dot(a, wd_ref[...], preferred_element_type=F32)

    @pl.when(f == pl.num_programs(1) - 1)
    def _():
        y_ref[...] = x_ref[...] + mod_ref[0, 5:6, :] * acc_ref[...]


def ffn_dense(x2, g, mod, w_gu, w_down, tiles_per_batch, tm, tf):
    m, d = x2.shape
    ff = w_down.shape[0]
    nf = ff // tf
    return pl.pallas_call(
        _ffn_kernel,
        out_shape=jax.ShapeDtypeStruct((m, d), F32),
        grid=(m // tm, nf),
        in_specs=[pl.BlockSpec((tm, d), lambda i, f: (i, 0)),
                  pl.BlockSpec((1, d), lambda i, f: (0, 0)),
                  pl.BlockSpec((1, 6, d), _mod_map(tiles_per_batch)),
                  pl.BlockSpec((d, tf), lambda i, f: (0, f)),
                  pl.BlockSpec((d, tf), lambda i, f: (0, nf + f)),
                  pl.BlockSpec((tf, d), lambda i, f: (f, 0))],
        out_specs=pl.BlockSpec((tm, d), lambda i, f: (i, 0)),
        scratch_shapes=[pltpu.VMEM((tm, d), BF16), pltpu.VMEM((tm, d), F32)],
        compiler_params=_params("parallel", "arbitrary"),
        name="ffn_dense",
    )(x2, g, mod, w_gu, w_gu, w_down)


def _router_kernel(x_ref, g_ref, mod_ref, wr_ref, h_ref, gate_ref):
    h = _norm_mod(x_ref[...], g_ref[...], mod_ref[0, 4:5, :], mod_ref[0, 3:4, :]).astype(BF16)
    h_ref[...] = h
    logits = jnp.dot(h, wr_ref[...], preferred_element_type=F32)
    lane = lax.broadcasted_iota(jnp.int32, logits.shape, 1)
    neg = jnp.float32(-jnp.inf)
    logits = jnp.where(lane < N_EXPERTS, logits, neg)
    m1 = jnp.max(logits, axis=-1, keepdims=True)
    i1 = jnp.min(jnp.where(logits == m1, lane, LANES), axis=-1, keepdims=True)
    rest = jnp.where(lane == i1, neg, logits)
    m2 = jnp.max(rest, axis=-1, keepdims=True)
    i2 = jnp.min(jnp.where(rest == m2, lane, LANES), axis=-1, keepdims=True)
    e2 = jnp.exp(m2 - m1)
    g1 = 1.0 / (1.0 + e2)
    g2 = e2 / (1.0 + e2)
    gate_ref[...] = jnp.where(lane == i1, g1, jnp.where(lane == i2, g2, -1.0))


def moe_router(x2, g, mod, w_router, tiles_per_batch, tm):
    m, d = x2.shape
    return pl.pallas_call(
        _router_kernel,
        out_shape=(jax.ShapeDtypeStruct((m, d), BF16), jax.ShapeDtypeStruct((m, LANES), F32)),
        grid=(m // tm,),
        in_specs=[pl.BlockSpec((tm, d), lambda i: (i, 0)),
                  pl.BlockSpec((1, d), lambda i: (0, 0)),
                  pl.BlockSpec((1, 6, d), _mod_map(tiles_per_batch)),
                  pl.BlockSpec((d, LANES), lambda i: (0, 0))],
        out_specs=(pl.BlockSpec((tm, d), lambda i: (i, 0)),
                   pl.BlockSpec((tm, LANES), lambda i: (i, 0))),
        compiler_params=_params("parallel"),
        name="moe_router",
    )(x2, g, mod, w_router)


def _moe_kernel(te_ref, nt_ref, xg_ref, gate_ref, wg_ref, wu_ref, wd_ref, y_ref):
    i = pl.program_id(0)

    @pl.when(i < nt_ref[0])
    def _():
        h = xg_ref[...]
        gg = jnp.dot(h, wg_ref[0], preferred_element_type=F32)
        uu = jnp.dot(h, wu_ref[0], preferred_element_type=F32)
        a = (gg * jax.nn.sigmoid(gg) * uu).astype(BF16)
        y = jnp.dot(a, wd_ref[0], preferred_element_type=F32)
        y_ref[...] = (gate_ref[...] * y).astype(y_ref.dtype)

    @pl.when(i >= nt_ref[0])
    def _():
        y_ref[...] = jnp.zeros_like(y_ref)


def moe_grouped(tile_expert, n_tiles_used, xg, row_gate, w_gu, w_down, tm):
    p, d = xg.shape
    fe = w_down.shape[1]
    grid_spec = pltpu.PrefetchScalarGridSpec(
        num_scalar_prefetch=2,
        grid=(p // tm,),
        in_specs=[pl.BlockSpec((tm, d), lambda i, te, nt: (i, 0)),
                  pl.BlockSpec((tm, 1), lambda i, te, nt: (i, 0)),
                  pl.BlockSpec((1, d, fe), lambda i, te, nt: (te[i], 0, 0)),
                  pl.BlockSpec((1, d, fe), lambda i, te, nt: (te[i], 0, 1)),
                  pl.BlockSpec((1, fe, d), lambda i, te, nt: (te[i], 0, 0))],
        out_specs=pl.BlockSpec((tm, d), lambda i, te, nt: (i, 0)),
    )
    return pl.pallas_call(
        _moe_kernel,
        out_shape=jax.ShapeDtypeStruct((p, d), F32),
        grid_spec=grid_spec,
        compiler_params=_params("arbitrary"),
        name="moe_grouped",
    )(tile_expert, n_tiles_used, xg, row_gate, w_gu, w_gu, w_down)


def moe_block(x2, g, mod, w_router, w_gu, w_down, tiles_per_batch, tm_tok, tm_grp):
    m, d = x2.shape
    h, gate_dense = moe_router(x2, g, mod, w_router, tiles_per_batch, tm_tok)
    gates = gate_dense[:, :N_EXPERTS]
    sel = gates >= 0.0
    rank = jnp.cumsum(sel.astype(jnp.int32), axis=0) - 1
    cnt = jnp.sum(sel.astype(jnp.int32), axis=0)
    gsize = ((cnt + tm_grp - 1) // tm_grp) * tm_grp
    gend = jnp.cumsum(gsize)
    gstart = gend - gsize
    p_max = m * TOP_K + N_EXPERTS * tm_grp
    n_tiles = p_max // tm_grp
    dest = jnp.where(sel, gstart[None, :] + rank, p_max)
    d_lo = jnp.min(dest, axis=1)
    d_hi = jnp.max(jnp.where(sel, dest, -1), axis=1)
    g_lo = jnp.take_along_axis(gates, jnp.argmin(dest, axis=1)[:, None], axis=1)[:, 0]
    g_hi = jnp.sum(jnp.where(sel, gates, 0.0), axis=1) - g_lo
    tok = jnp.arange(m, dtype=jnp.int32)
    dests = jnp.concatenate([d_lo, d_hi])
    src = jnp.zeros((p_max,), jnp.int32).at[dests].set(jnp.concatenate([tok, tok]), unique_indices=True)
    row_gate = jnp.zeros((p_max,), F32).at[dests].set(jnp.concatenate([g_lo, g_hi]), unique_indices=True)
    tile_start = jnp.arange(n_tiles, dtype=jnp.int32) * tm_grp
    tile_expert = jnp.minimum(jnp.sum((tile_start[:, None] >= gend[None, :]).astype(jnp.int32), axis=1),
                              N_EXPERTS - 1).astype(jnp.int32)
    n_used = (gend[-1] // tm_grp).astype(jnp.int32).reshape(1)
    xg = jnp.take(h, src, axis=0)
    yg = moe_grouped(tile_expert, n_used, xg, row_gate[:, None], w_gu, w_down, tm_grp)
    return jnp.take(yg, d_lo, axis=0) + jnp.take(yg, d_hi, axis=0)


def _residual_kernel(x_ref, y_ref, mod_ref, o_ref):
    o_ref[...] = x_ref[...] + mod_ref[0, 5:6, :] * y_ref[...]


def _residual_norm_kernel(x_ref, y_ref, mod_ref, g_ref, o_ref):
    x = x_ref[...] + mod_ref[0, 5:6, :] * y_ref[...]
    ms = jnp.mean(x * x, axis=-1, keepdims=True)
    o_ref[...] = x * lax.rsqrt(ms + EPS) * g_ref[...]


def residual_gate(x2, y2, mod, tiles_per_batch, tm, g_final=None):
    m, d = x2.shape
    row = pl.BlockSpec((tm, d), lambda i: (i, 0))
    specs = [row, row, pl.BlockSpec((1, 6, d), _mod_map(tiles_per_batch))]
    args = [x2, y2, mod]
    body = _residual_kernel
    if g_final is not None:
        specs.append(pl.BlockSpec((1, d), lambda i: (0, 0)))
        args.append(g_final)
        body = _residual_norm_kernel
    return pl.pallas_call(
        body,
        out_shape=jax.ShapeDtypeStruct((m, d), F32),
        grid=(m // tm,),
        in_specs=specs,
        out_specs=row,
        compiler_params=_params("parallel"),
        name="residual_gate",
    )(*args)


def _final_norm_kernel(x_ref, g_ref, o_ref):
    x = x_ref[...]
    ms = jnp.mean(x * x, axis=-1, keepdims=True)
    o_ref[...] = x * lax.rsqrt(ms + EPS) * g_ref[...]


def final_norm(x2, g, tm):
    m, d = x2.shape
    return pl.pallas_call(
        _final_norm_kernel,
        out_shape=jax.ShapeDtypeStruct((m, d), F32),
        grid=(m // tm,),
        in_specs=[pl.BlockSpec((tm, d), lambda i: (i, 0)), pl.BlockSpec((1, d), lambda i: (0, 0))],
        out_specs=pl.BlockSpec((tm, d), lambda i: (i, 0)),
        compiler_params=_params("parallel"),
        name="final_norm",
    )(x2, g)


NA_QROWS = 4
NA_KROWS = NA_QROWS + WIN_R
Q_COL, K_COL, V_COL = 4, 5, 6


def _na_key_start(j, rows):
    return np.clip(j * NA_QROWS - WIN_R // 2, 0, rows - NA_KROWS)


def _na_bias_index(rows):
    nblk = rows // NA_QROWS
    pats = []
    for j in range(nblk):
        start = _na_key_start(j, rows)
        r = j * NA_QROWS + np.arange(NA_QROWS)
        sr = np.clip(r - WIN_R // 2, 0, rows - WIN_R)
        kr = start + np.arange(NA_KROWS)
        rvalid = (kr[None, :] >= sr[:, None]) & (kr[None, :] < sr[:, None] + WIN_R)
        ri = np.clip(kr[None, :] - r[:, None] + WIN_R - 1, 0, 2 * WIN_R - 2)
        c = np.arange(GRID_W)
        sc = np.clip(c - WIN_C // 2, 0, GRID_W - WIN_C)
        cvalid = (c[None, :] >= sc[:, None]) & (c[None, :] < sc[:, None] + WIN_C)
        ci = np.clip(c[None, :] - c[:, None] + WIN_C - 1, 0, 2 * WIN_C - 2)
        shape = (NA_QROWS, GRID_W, NA_KROWS, GRID_W)
        valid = np.broadcast_to(rvalid[:, None, :, None] & cvalid[None, :, None, :], shape)
        ridx = np.broadcast_to(ri[:, None, :, None], shape)
        cidx = np.broadcast_to(ci[None, :, None, :], shape)
        n_q, n_k = NA_QROWS * GRID_W, NA_KROWS * GRID_W
        pats.append((ridx.reshape(n_q, n_k), cidx.reshape(n_q, n_k), valid.reshape(n_q, n_k)))
    for j in range(2, nblk - 1):
        assert all(np.array_equal(a, b) for a, b in zip(pats[1], pats[j]))
    sel = [pats[0], pats[1], pats[nblk - 1]]
    return tuple(np.stack([p[i] for p in sel]) for i in range(3))


def na_bias_table(rpb, rows):
    ridx, cidx, valid = _na_bias_index(rows)
    b = rpb[:, ridx, cidx]
    b = jnp.where(valid[None], b, -jnp.inf)
    return jnp.transpose(b, (1, 0, 2, 3)).astype(F32)


def _attend_heads(q, key_sets, bias_fn):
    n, w = q.shape
    lane = lax.broadcasted_iota(jnp.int32, (1, w), 1)
    out = jnp.zeros((n, w), F32)
    for h in range(NA_HEADS):
        mh = (lane >= h * NA_HEAD_DIM) & (lane < (h + 1) * NA_HEAD_DIM)
        qh = jnp.where(mh, q, 0.0).astype(BF16)
        scores = []
        for i, (k, _) in enumerate(key_sets):
            s = lax.dot_general(qh, k, (((1,), (1,)), ((), ())), preferred_element_type=F32)
            b = bias_fn(i, h)
            scores.append(s if b is None else s + b)
        m = scores[0].max(axis=-1, keepdims=True)
        for s in scores[1:]:
            m = jnp.maximum(m, s.max(axis=-1, keepdims=True))
        denom = jnp.zeros((n, 1), F32)
        acc = jnp.zeros((n, w), F32)
        for s, (_, v) in zip(scores, key_sets):
            p = jnp.exp(s - m)
            denom = denom + p.sum(axis=-1, keepdims=True)
            acc = acc + jnp.dot(p.astype(BF16), v, preferred_element_type=F32)
        out = out + jnp.where(mh, acc / denom, 0.0)
    return out


def _na_kernel(q_ref, k_ref, v_ref, kc_ref, vc_ref, bias_ref, o_ref, *, rows):
    j = pl.program_id(1)
    start = jnp.clip(j * NA_QROWS - WIN_R // 2, 0, rows - NA_KROWS)
    t0 = pl.multiple_of(start * GRID_W, GRID_W)
    nk = NA_KROWS * GRID_W
    kw = k_ref[0, pl.ds(t0, nk), :].astype(BF16)
    vw = v_ref[0, pl.ds(t0, nk), :].astype(BF16)
    kc = kc_ref[0].astype(BF16)
    vc = vc_ref[0].astype(BF16)
    q = q_ref[0] * (1.0 / math.sqrt(NA_HEAD_DIM))
    o_ref[0] = _attend_heads(q, [(kw, vw), (kc, vc)], lambda i, h: bias_ref[0, h] if i == 0 else None)


def na_attention(pr3, prc3, bias):
    bn, S, _ = pr3.shape
    Lc = prc3.shape[1]
    rows = S // GRID_W
    nblk = rows // NA_QROWS
    nq = NA_QROWS * GRID_W

    def pat(b, j):
        return (jnp.where(j == 0, 0, jnp.where(j == nblk - 1, 2, 1)), 0, 0, 0)

    return pl.pallas_call(
        functools.partial(_na_kernel, rows=rows),
        out_shape=jax.ShapeDtypeStruct((bn, S, W_NA), F32),
        grid=(bn, nblk),
        in_specs=[pl.BlockSpec((1, nq, W_NA), lambda b, j: (b, j, Q_COL)),
                  pl.BlockSpec((1, S, W_NA), lambda b, j: (b, 0, K_COL)),
                  pl.BlockSpec((1, S, W_NA), lambda b, j: (b, 0, V_COL)),
                  pl.BlockSpec((1, Lc, W_NA), lambda b, j: (b, 0, K_COL)),
                  pl.BlockSpec((1, Lc, W_NA), lambda b, j: (b, 0, V_COL)),
                  pl.BlockSpec((1,) + bias.shape[1:], pat)],
        out_specs=pl.BlockSpec((1, nq, W_NA), lambda b, j: (b, j, 0)),
        compiler_params=_params("parallel", "arbitrary"),
        name="na_attention",
    )(pr3, pr3, pr3, prc3, prc3, bias)


def _ctx_attn_kernel(q_ref, k_ref, v_ref, o_ref):
    q = q_ref[0] * (1.0 / math.sqrt(NA_HEAD_DIM))
    o_ref[0] = _attend_heads(q, [(k_ref[0].astype(BF16), v_ref[0].astype(BF16))], lambda i, h: None)


def ctx_attention(prc3):
    bn, Lc, _ = prc3.shape
    return pl.pallas_call(
        _ctx_attn_kernel,
        out_shape=jax.ShapeDtypeStruct((bn, Lc, W_NA), F32),
        grid=(bn,),
        in_specs=[pl.BlockSpec((1, Lc, W_NA), lambda b: (b, 0, Q_COL)),
                  pl.BlockSpec((1, Lc, W_NA), lambda b: (b, 0, K_COL)),
                  pl.BlockSpec((1, Lc, W_NA), lambda b: (b, 0, V_COL))],
        out_specs=pl.BlockSpec((1, Lc, W_NA), lambda b: (b, 0, 0)),
        compiler_params=_params("parallel"),
        name="ctx_attention",
    )(prc3, prc3, prc3)


def _rmsnorm(x, g):
    xf = x.astype(F32)
    return xf * lax.rsqrt(jnp.mean(xf * xf, axis=-1, keepdims=True) + EPS) * g


def _dwconv_centered(u, w, b=None):
    k = w.shape[0]
    L = u.shape[1]
    p = k // 2
    up = jnp.pad(u, ((0, 0), (p, p), (0, 0)))
    y = up[:, 0:L] * w[0]
    for i in range(1, k):
        y = y + up[:, i:i + L] * w[i]
    return y if b is None else y + b


def _pool_mix(u, w_grp, scale):
    bn, L, _ = u.shape
    s = jnp.pad(jnp.cumsum(u.astype(F32), axis=1), ((0, 0), (1, 0), (0, 0)))
    t = np.arange(L)
    outs = []
    for g, win in enumerate(POOL_WINDOWS):
        lo = np.clip(t - win // 2, 0, L)
        hi = np.clip(t - win // 2 + win, 0, L)
        sg = s[..., g * POOL_GROUP:(g + 1) * POOL_GROUP]
        cnt = jnp.asarray((hi - lo)[:, None], F32)
        mean = (sg[:, hi] - sg[:, lo]) / cnt
        outs.append(mean - u[..., g * POOL_GROUP:(g + 1) * POOL_GROUP])
    p = jnp.stack(outs, axis=2)
    y = jnp.einsum('blgc,gcd->blgd', p, w_grp).reshape(bn, L, W_POOL)
    return y * scale


def _short_conv_mix(pr, w_conv):
    h, bg, cg = jnp.split(pr, 3, axis=-1)
    return bg * _dwconv_centered(cg * h, w_conv)


def _ssd_inputs(pr, conv_w, conv_b):
    bn, L, _ = pr.shape
    z, xbc, dt_raw = jnp.split(pr, [W_SSD, W_SSD + SSD_XBC], axis=-1)
    xbc = jax.nn.silu(_dwconv_centered(xbc, conv_w, conv_b))
    xs, bs, cs = jnp.split(xbc, [W_SSD, W_SSD + SSD_GROUPS * SSD_STATE], axis=-1)
    xs = xs.reshape(bn, L, SSD_HEADS, SSD_HEAD_DIM)
    bs = bs.reshape(bn, L, SSD_GROUPS, SSD_STATE)
    cs = cs.reshape(bn, L, SSD_GROUPS, SSD_STATE)
    return z, xs, bs, cs, dt_raw


def _ssd_chunked(x, dt, a, bmat, cmat, h0, need_y):
    bn, L, H, P = x.shape
    T = SSD_CHUNK
    nc = L // T
    rep = H // bmat.shape[2]
    bh = jnp.repeat(bmat.astype(F32), rep, axis=2).reshape(bn, nc, T, H, -1)
    ch = jnp.repeat(cmat.astype(F32), rep, axis=2).reshape(bn, nc, T, H, -1)
    xdt = (x.astype(F32) * dt[..., None]).reshape(bn, nc, T, H, P)
    la = (dt * a).reshape(bn, nc, T, H).transpose(0, 3, 1, 2)
    acum = jnp.cumsum(la, axis=-1)
    states = jnp.einsum('bclhn,bhcl,bclhp->bchpn', bh, jnp.exp(acum[..., -1:] - acum), xdt)
    states = jnp.concatenate([h0[:, None].astype(F32), states], axis=1)
    tot = jnp.pad(jnp.cumsum(acum[..., -1], axis=-1), ((0, 0), (0, 0), (1, 0)))
    tril_c = np.tril(np.ones((nc + 1, nc + 1), dtype=bool))
    decay_chunk = jnp.exp(jnp.where(tril_c, tot[..., :, None] - tot[..., None, :], -jnp.inf))
    states = jnp.einsum('bhzc,bchpn->bzhpn', decay_chunk, states)
    h_final = states[:, -1]
    if not need_y:
        return None, h_final
    tril_t = np.tril(np.ones((T, T), dtype=bool))
    lmat = jnp.exp(jnp.where(tril_t, acum[..., :, None] - acum[..., None, :], -jnp.inf))
    scores = jnp.einsum('bclhn,bcshn->bhcls', ch, bh) * lmat
    y = (jnp.einsum('bhcls,bcshp->bclhp', scores, xdt)
         + jnp.einsum('bclhn,bchpn,bhcl->bclhp', ch, states[:, :-1], jnp.exp(acum)))
    return y.reshape(bn, L, H, P), h_final


def _rev(t, d):
    return t if d == 0 else jnp.flip(t, axis=1)


def _ssd_mix(pr, pr_c, conv_w, conv_b, dt_bias, a_log, d_skip, norm_g, ctx_out):
    z, xs, bs, cs, dtr = _ssd_inputs(pr, conv_w, conv_b)
    zc, xsc, bsc, csc, dtrc = _ssd_inputs(pr_c, conv_w, conv_b)
    h0 = jnp.zeros((pr.shape[0], SSD_HEADS, SSD_HEAD_DIM, SSD_STATE), F32)
    y = xs * d_skip[:, None]
    yc = xsc * d_skip[:, None] if ctx_out else None
    for d in range(2):
        a = -jnp.exp(a_log[d])
        hs = slice(d * SSD_HEADS, (d + 1) * SSD_HEADS)
        dt_l = jax.nn.softplus(dtr[..., hs] + dt_bias[d])
        dt_c = jax.nn.softplus(dtrc[..., hs] + dt_bias[d])
        y_cd, h_c = _ssd_chunked(_rev(xsc, d), _rev(dt_c, d), a, _rev(bsc, d), _rev(csc, d), h0, ctx_out)
        y_ld, _ = _ssd_chunked(_rev(xs, d), _rev(dt_l, d), a, _rev(bs, d), _rev(cs, d), h_c, True)
        y = y + _rev(y_ld, d)
        if ctx_out:
            yc = yc + _rev(y_cd, d)
    o = _rmsnorm(y.reshape(z.shape) * jax.nn.silu(z), norm_g)
    if not ctx_out:
        return o, None
    oc = _rmsnorm(yc.reshape(zc.shape) * jax.nn.silu(zc), norm_g)
    return o, oc


def _mixer(pr, pr_c, pool_w, pool_scale, conv_w, rpb, s_conv_w, s_conv_b, dt_bias, a_log, d_skip,
           s_norm_g, ctx_out):
    bn, S, _ = pr.shape
    o_na = na_attention(pr, pr_c, na_bias_table(rpb, S // GRID_W))
    oc_na = ctx_attention(pr_c) if ctx_out else None
    pr = pr[..., :D_IN_PROJ]
    pr_c = pr_c[..., :D_IN_PROJ]
    p_pool, p_conv, _, p_ssd = jnp.split(pr, PROJ_SPLITS, axis=-1)
    c_pool, c_conv, _, c_ssd = jnp.split(pr_c, PROJ_SPLITS, axis=-1)
    o_ssd, oc_ssd = _ssd_mix(p_ssd, c_ssd, s_conv_w, s_conv_b, dt_bias, a_log, d_skip, s_norm_g, ctx_out)
    o = jnp.concatenate([_pool_mix(p_pool, pool_w, pool_scale),
                         _short_conv_mix(p_conv, conv_w),
                         o_na,
                         o_ssd], axis=-1)
    if not ctx_out:
        return o, None
    oc = jnp.concatenate([_pool_mix(c_pool, pool_w, pool_scale),
                          _short_conv_mix(c_conv, conv_w),
                          oc_na,
                          oc_ssd], axis=-1)
    return o, oc


TM = 512
TM_CTX = 256
TF = 1408
TM_GRP = 512


def kernel(x, c, ctx, c_ctx, w_ada, b_ada, g_mix, g_ffn, w_in, w_out, pool_w, pool_scale, conv_w, na_rpb,
           ssd_conv_w, ssd_conv_b, ssd_dt_bias, ssd_a_log, ssd_d, ssd_norm_g, ffn_w_gu, ffn_w_down,
           moe_router, moe_w_gu, moe_w_down, g_final):
    bn, S, d = x.shape
    Lc = ctx.shape[1]
    m, mc = bn * S, bn * Lc
    tpb = S // TM
    x2 = x.reshape(m, d)
    xc2 = ctx.reshape(mc, d)
    c_rows = jnp.concatenate([c, c_ctx[None, :], jnp.zeros((7, d), F32)], axis=0)
    for l in range(DEPTH):
        ctx_out = l < DEPTH - 1
        last = l == DEPTH - 1
        ada = ada_modulation(c_rows, w_ada[l].astype(BF16), b_ada[l][None, :])
        mod = ada[:bn].reshape(bn, 6, d)
        mod_c = ada[bn:bn + 1].reshape(1, 6, d)
        w_in_l = jnp.pad(w_in[l], ((0, 0), (0, D_IN_PAD - D_IN_PROJ))).astype(BF16)
        g_m = g_mix[l][None, :]
        g_f = g_ffn[l][None, :]
        pr = in_proj(x2, g_m, mod, w_in_l, tpb, TM)
        pr_c = in_proj(xc2, g_m, mod_c, w_in_l, None, TM_CTX)
        o, oc = _mixer(pr.reshape(bn, S, -1), pr_c.reshape(bn, Lc, -1), pool_w[l], pool_scale[l], conv_w[l],
                       na_rpb[l], ssd_conv_w[l], ssd_conv_b[l], ssd_dt_bias[l], ssd_a_log[l], ssd_d[l],
                       ssd_norm_g[l], ctx_out)
        w_out_l = w_out[l].astype(BF16)
        x2 = out_proj(x2, o.reshape(m, -1), mod, w_out_l, tpb, TM)
        if ctx_out:
            xc2 = out_proj(xc2, oc.reshape(mc, -1), mod_c, w_out_l, None, TM_CTX)
        j = l // 2
        if l % 2 == 0:
            w_gu = ffn_w_gu[j].astype(BF16)
            w_dn = ffn_w_down[j].astype(BF16)
            x2 = ffn_dense(x2, g_f, mod, w_gu, w_dn, tpb, TM, TF)
            if ctx_out:
                xc2 = ffn_dense(xc2, g_f, mod_c, w_gu, w_dn, None, TM_CTX, TF)
        else:
            w_r = jnp.pad(moe_router[j], ((0, 0), (0, LANES - N_EXPERTS))).astype(BF16)
            w_gu = moe_w_gu[j].astype(BF16)
            w_dn = moe_w_down[j].astype(BF16)
            y = moe_block(x2, g_f, mod, w_r, w_gu, w_dn, tpb, TM, TM_GRP)
            x2 = residual_gate(x2, y, mod, tpb, TM, g_final[None, :] if last else None)
            if ctx_out:
                yc = moe_block(xc2, g_f, mod_c, w_r, w_gu, w_dn, None, TM_CTX, TM_GRP)
                xc2 = residual_gate(xc2, yc, mod_c, None, TM_CTX)
            if last:
                return x2.reshape(bn, S, d)
    return final_norm(x2, g_final[None, :], TM).reshape(bn, S, d)
```

```python
import functools
import math

import numpy as np
import jax
import jax.numpy as jnp
from jax import lax
from jax.experimental import pallas as pl
from jax.experimental.pallas import tpu as pltpu

D_MODEL = 1024
DEPTH = 2
GRID_W = 64
EPS = 1e-6
W_POOL = 256
W_CONV = 256
W_NA = 256
W_SSD = 256
POOL_WINDOWS = (2, 4, 8, 16)
POOL_GROUP = W_POOL // len(POOL_WINDOWS)
NA_HEADS = 4
NA_HEAD_DIM = W_NA // NA_HEADS
WIN_R = 8
WIN_C = 16
COL_BLOCK = 16
COL_BAND = 32
SSD_HEAD_DIM = 64
SSD_HEADS = W_SSD // SSD_HEAD_DIM
SSD_GROUPS = 2
SSD_STATE = 128
SSD_CHUNK = 128
SSD_XBC = W_SSD + 2 * SSD_GROUPS * SSD_STATE
D_IN_PROJ = W_POOL + 3 * W_CONV + 3 * W_NA + W_SSD + SSD_XBC + 2 * SSD_HEADS
PROJ_SPLITS = (W_POOL, W_POOL + 3 * W_CONV, W_POOL + 3 * W_CONV + 3 * W_NA)
D_FF = 2816
N_EXPERTS = 8
TOP_K = 2
D_FF_EXPERT = 1408

LANES = 128
D_IN_PAD = -(-D_IN_PROJ // LANES) * LANES
VMEM_LIMIT = 56 * 1024 * 1024
BF16 = jnp.bfloat16
F32 = jnp.float32


def _params(*sem):
    return pltpu.CompilerParams(dimension_semantics=sem, vmem_limit_bytes=VMEM_LIMIT)


def _norm_mod(x, g, scale, shift):
    ms = jnp.mean(x * x, axis=-1, keepdims=True)
    return (x * lax.rsqrt(ms + EPS) * g) * (1.0 + scale) + shift


def _mod_map(tiles_per_batch):
    if tiles_per_batch is None:
        return lambda i, *_: (0, 0, 0)
    return lambda i, *_: (i // tiles_per_batch, 0, 0)


def _ada_kernel(c_ref, w_ref, b_ref, o_ref):
    c = c_ref[...]
    s = c * jax.nn.sigmoid(c)
    o_ref[...] = jnp.dot(s.astype(BF16), w_ref[...], preferred_element_type=F32) + b_ref[...]


def ada_modulation(c_rows, w, b):
    r, d = c_rows.shape
    n = w.shape[1]
    tn = 1536
    return pl.pallas_call(
        _ada_kernel,
        out_shape=jax.ShapeDtypeStruct((r, n), F32),
        grid=(n // tn,),
        in_specs=[pl.BlockSpec((r, d), lambda j: (0, 0)),
                  pl.BlockSpec((d, tn), lambda j: (0, j)),
                  pl.BlockSpec((1, tn), lambda j: (0, j))],
        out_specs=pl.BlockSpec((r, tn), lambda j: (0, j)),
        compiler_params=_params("arbitrary"),
        name="ada_modulation",
    )(c_rows, w, b)


def _in_proj_kernel(x_ref, g_ref, mod_ref, w_ref, o_ref):
    h = _norm_mod(x_ref[...], g_ref[...], mod_ref[0, 1:2, :], mod_ref[0, 0:1, :])
    o_ref[...] = jnp.dot(h.astype(BF16), w_ref[...], preferred_element_type=F32).astype(o_ref.dtype)


def in_proj(x2, g, mod, w, tiles_per_batch, tm, out_dtype=F32):
    m, d = x2.shape
    n = w.shape[1]
    return pl.pallas_call(
        _in_proj_kernel,
        out_shape=jax.ShapeDtypeStruct((m, n), out_dtype),
        grid=(m // tm,),
        in_specs=[pl.BlockSpec((tm, d), lambda i: (i, 0)),
                  pl.BlockSpec((1, d), lambda i: (0, 0)),
                  pl.BlockSpec((1, 6, d), _mod_map(tiles_per_batch)),
                  pl.BlockSpec((d, n), lambda i: (0, 0))],
        out_specs=pl.BlockSpec((tm, n), lambda i: (i, 0)),
        compiler_params=_params("parallel"),
        name="in_proj",
    )(x2, g, mod, w)


def _out_proj_kernel(x_ref, o_ref, mod_ref, w_ref, y_ref):
    y = jnp.dot(o_ref[...].astype(BF16), w_ref[...], preferred_element_type=F32)
    y_ref[...] = x_ref[...] + mod_ref[0, 2:3, :] * y


def out_proj(x2, o2, mod, w, tiles_per_batch, tm):
    m, d = x2.shape
    k = o2.shape[1]
    return pl.pallas_call(
        _out_proj_kernel,
        out_shape=jax.ShapeDtypeStruct((m, d), F32),
        grid=(m // tm,),
        in_specs=[pl.BlockSpec((tm, d), lambda i: (i, 0)),
                  pl.BlockSpec((tm, k), lambda i: (i, 0)),
                  pl.BlockSpec((1, 6, d), _mod_map(tiles_per_batch)),
                  pl.BlockSpec((k, d), lambda i: (0, 0))],
        out_specs=pl.BlockSpec((tm, d), lambda i: (i, 0)),
        compiler_params=_params("parallel"),
        name="out_proj",
    )(x2, o2, mod, w)


def _ffn_kernel(x_ref, g_ref, mod_ref, wg_ref, wu_ref, wd_ref, y_ref, h_ref, acc_ref):
    f = pl.program_id(1)

    @pl.when(f == 0)
    def _():
        h = _norm_mod(x_ref[...], g_ref[...], mod_ref[0, 4:5, :], mod_ref[0, 3:4, :])
        h_ref[...] = h.astype(BF16)
        acc_ref[...] = jnp.zeros_like(acc_ref)

    h = h_ref[...]
    gg = jnp.dot(h, wg_ref[...], preferred_element_type=F32)
    uu = jnp.dot(h, wu_ref[...], preferred_element_type=F32)
    a = (gg * jax.nn.sigmoid(gg) * uu).astype(BF16)
    acc_ref[...] += jnp.dot(a, wd_ref[...], preferred_element_type=F32)

    @pl.when(f == pl.num_programs(1) - 1)
    def _():
        y_ref[...] = x_ref[...] + mod_ref[0, 5:6, :] * acc_ref[...]


def ffn_dense(x2, g, mod, w_gu, w_down, tiles_per_batch, tm, tf):
    m, d = x2.shape
    ff = w_down.shape[0]
    nf = ff // tf
    return pl.pallas_call(
        _ffn_kernel,
        out_shape=jax.ShapeDtypeStruct((m, d), F32),
        grid=(m // tm, nf),
        in_specs=[pl.BlockSpec((tm, d), lambda i, f: (i, 0)),
                  pl.BlockSpec((1, d), lambda i, f: (0, 0)),
                  pl.BlockSpec((1, 6, d), _mod_map(tiles_per_batch)),
                  pl.BlockSpec((d, tf), lambda i, f: (0, f)),
                  pl.BlockSpec((d, tf), lambda i, f: (0, nf + f)),
                  pl.BlockSpec((tf, d), lambda i, f: (f, 0))],
        out_specs=pl.BlockSpec((tm, d), lambda i, f: (i, 0)),
        scratch_shapes=[pltpu.VMEM((tm, d), BF16), pltpu.VMEM((tm, d), F32)],
        compiler_params=_params("parallel", "arbitrary"),
        name="ffn_dense",
    )(x2, g, mod, w_gu, w_gu, w_down)


def _router_kernel(x_ref, g_ref, mod_ref, wr_ref, h_ref, gate_ref):
    h = _norm_mod(x_ref[...], g_ref[...], mod_ref[0, 4:5, :], mod_ref[0, 3:4, :]).astype(BF16)
    h_ref[...] = h
    logits = jnp.dot(h, wr_ref[...], preferred_element_type=F32)
    lane = lax.broadcasted_iota(jnp.int32, logits.shape, 1)
    neg = jnp.float32(-jnp.inf)
    logits = jnp.where(lane < N_EXPERTS, logits, neg)
    m1 = jnp.max(logits, axis=-1, keepdims=True)
    i1 = jnp.min(jnp.where(logits == m1, lane, LANES), axis=-1, keepdims=True)
    rest = jnp.where(lane == i1, neg, logits)
    m2 = jnp.max(rest, axis=-1, keepdims=True)
    i2 = jnp.min(jnp.where(rest == m2, lane, LANES), axis=-1, keepdims=True)
    e2 = jnp.exp(m2 - m1)
    g1 = 1.0 / (1.0 + e2)
    g2 = e2 / (1.0 + e2)
    gate_ref[...] = jnp.where(lane == i1, g1, jnp.where(lane == i2, g2, -1.0))


def moe_router(x2, g, mod, w_router, tiles_per_batch, tm):
    m, d = x2.shape
    return pl.pallas_call(
        _router_kernel,
        out_shape=(jax.ShapeDtypeStruct((m, d), BF16), jax.ShapeDtypeStruct((m, LANES), F32)),
        grid=(m // tm,),
        in_specs=[pl.BlockSpec((tm, d), lambda i: (i, 0)),
                  pl.BlockSpec((1, d), lambda i: (0, 0)),
                  pl.BlockSpec((1, 6, d), _mod_map(tiles_per_batch)),
                  pl.BlockSpec((d, LANES), lambda i: (0, 0))],
        out_specs=(pl.BlockSpec((tm, d), lambda i: (i, 0)),
                   pl.BlockSpec((tm, LANES), lambda i: (i, 0))),
        compiler_params=_params("parallel"),
        name="moe_router",
    )(x2, g, mod, w_router)


def _moe_kernel(te_ref, nt_ref, xg_ref, gate_ref, wg_ref, wu_ref, wd_ref, y_ref):
    i = pl.program_id(0)

    @pl.when(i < nt_ref[0])
    def _():
        h = xg_ref[...]
        gg = jnp.dot(h, wg_ref[0], preferred_element_type=F32)
        uu = jnp.dot(h, wu_ref[0], preferred_element_type=F32)
        a = (gg * jax.nn.sigmoid(gg) * uu).astype(BF16)
        y = jnp.dot(a, wd_ref[0], preferred_element_type=F32)
        y_ref[...] = (gate_ref[...] * y).astype(y_ref.dtype)

    @pl.when(i >= nt_ref[0])
    def _():
        y_ref[...] = jnp.zeros_like(y_ref)


def moe_grouped(tile_expert, n_tiles_used, xg, row_gate, w_gu, w_down, tm):
    p, d = xg.shape
    fe = w_down.shape[1]
    grid_spec = pltpu.PrefetchScalarGridSpec(
        num_scalar_prefetch=2,
        grid=(p // tm,),
        in_specs=[pl.BlockSpec((tm, d), lambda i, te, nt: (i, 0)),
                  pl.BlockSpec((tm, 1), lambda i, te, nt: (i, 0)),
                  pl.BlockSpec((1, d, fe), lambda i, te, nt: (te[i], 0, 0)),
                  pl.BlockSpec((1, d, fe), lambda i, te, nt: (te[i], 0, 1)),
                  pl.BlockSpec((1, fe, d), lambda i, te, nt: (te[i], 0, 0))],
        out_specs=pl.BlockSpec((tm, d), lambda i, te, nt: (i, 0)),
    )
    return pl.pallas_call(
        _moe_kernel,
        out_shape=jax.ShapeDtypeStruct((p, d), F32),
        grid_spec=grid_spec,
        compiler_params=_params("arbitrary"),
        name="moe_grouped",
    )(tile_expert, n_tiles_used, xg, row_gate, w_gu, w_gu, w_down)


def moe_block(x2, g, mod, w_router, w_gu, w_down, tiles_per_batch, tm_tok, tm_grp):
    m, d = x2.shape
    h, gate_dense = moe_router(x2, g, mod, w_router, tiles_per_batch, tm_tok)
    gates = gate_dense[:, :N_EXPERTS]
    sel = gates >= 0.0
    rank = jnp.cumsum(sel.astype(jnp.int32), axis=0) - 1
    cnt = jnp.sum(sel.astype(jnp.int32), axis=0)
    gsize = ((cnt + tm_grp - 1) // tm_grp) * tm_grp
    gend = jnp.cumsum(gsize)
    gstart = gend - gsize
    p_max = m * TOP_K + N_EXPERTS * tm_grp
    n_tiles = p_max // tm_grp
    dest = jnp.where(sel, gstart[None, :] + rank, p_max)
    d_lo = jnp.min(dest, axis=1)
    d_hi = jnp.max(jnp.where(sel, dest, -1), axis=1)
    g_lo = jnp.take_along_axis(gates, jnp.argmin(dest, axis=1)[:, None], axis=1)[:, 0]
    g_hi = jnp.sum(jnp.where(sel, gates, 0.0), axis=1) - g_lo
    tok = jnp.arange(m, dtype=jnp.int32)
    dests = jnp.concatenate([d_lo, d_hi])
    src = jnp.zeros((p_max,), jnp.int32).at[dests].set(jnp.concatenate([tok, tok]), unique_indices=True)
    row_gate = jnp.zeros((p_max,), F32).at[dests].set(jnp.concatenate([g_lo, g_hi]), unique_indices=True)
    tile_start = jnp.arange(n_tiles, dtype=jnp.int32) * tm_grp
    tile_expert = jnp.minimum(jnp.sum((tile_start[:, None] >= gend[None, :]).astype(jnp.int32), axis=1),
                              N_EXPERTS - 1).astype(jnp.int32)
    n_used = (gend[-1] // tm_grp).astype(jnp.int32).reshape(1)
    xg = jnp.take(h, src, axis=0)
    yg = moe_grouped(tile_expert, n_used, xg, row_gate[:, None], w_gu, w_down, tm_grp)
    return jnp.take(yg, d_lo, axis=0) + jnp.take(yg, d_hi, axis=0)


def _residual_kernel(x_ref, y_ref, mod_ref, o_ref):
    o_ref[...] = x_ref[...] + mod_ref[0, 5:6, :] * y_ref[...]


def _residual_norm_kernel(x_ref, y_ref, mod_ref, g_ref, o_ref):
    x = x_ref[...] + mod_ref[0, 5:6, :] * y_ref[...]
    ms = jnp.mean(x * x, axis=-1, keepdims=True)
    o_ref[...] = x * lax.rsqrt(ms + EPS) * g_ref[...]


def residual_gate(x2, y2, mod, tiles_per_batch, tm, g_final=None):
    m, d = x2.shape
    row = pl.BlockSpec((tm, d), lambda i: (i, 0))
    specs = [row, row, pl.BlockSpec((1, 6, d), _mod_map(tiles_per_batch))]
    args = [x2, y2, mod]
    body = _residual_kernel
    if g_final is not None:
        specs.append(pl.BlockSpec((1, d), lambda i: (0, 0)))
        args.append(g_final)
        body = _residual_norm_kernel
    return pl.pallas_call(
        body,
        out_shape=jax.ShapeDtypeStruct((m, d), F32),
        grid=(m // tm,),
        in_specs=specs,
        out_specs=row,
        compiler_params=_params("parallel"),
        name="residual_gate",
    )(*args)


def _final_norm_kernel(x_ref, g_ref, o_ref):
    x = x_ref[...]
    ms = jnp.mean(x * x, axis=-1, keepdims=True)
    o_ref[...] = x * lax.rsqrt(ms + EPS) * g_ref[...]


def final_norm(x2, g, tm):
    m, d = x2.shape
    return pl.pallas_call(
        _final_norm_kernel,
        out_shape=jax.ShapeDtypeStruct((m, d), F32),
        grid=(m // tm,),
        in_specs=[pl.BlockSpec((tm, d), lambda i: (i, 0)), pl.BlockSpec((1, d), lambda i: (0, 0))],
        out_specs=pl.BlockSpec((tm, d), lambda i: (i, 0)),
        compiler_params=_params("parallel"),
        name="final_norm",
    )(x2, g)


NA_QROWS = 4
NA_KROWS = NA_QROWS + WIN_R
Q_COL, K_COL, V_COL = 4, 5, 6


def _na_key_start(j, rows):
    return np.clip(j * NA_QROWS - WIN_R // 2, 0, rows - NA_KROWS)


def _na_bias_index(rows):
    nblk = rows // NA_QROWS
    pats = []
    for j in range(nblk):
        start = _na_key_start(j, rows)
        r = j * NA_QROWS + np.arange(NA_QROWS)
        sr = np.clip(r - WIN_R // 2, 0, rows - WIN_R)
        kr = start + np.arange(NA_KROWS)
        rvalid = (kr[None, :] >= sr[:, None]) & (kr[None, :] < sr[:, None] + WIN_R)
        ri = np.clip(kr[None, :] - r[:, None] + WIN_R - 1, 0, 2 * WIN_R - 2)
        pats.append((ri, rvalid))
    for j in range(2, nblk - 1):
        assert all(np.array_equal(a, b) for a, b in zip(pats[1], pats[j]))
    sel = [pats[0], pats[1], pats[nblk - 1]]
    ri = np.stack([p[0] for p in sel])
    rvalid = np.stack([p[1] for p in sel])
    c = np.arange(GRID_W)
    sc = np.clip(c - WIN_C // 2, 0, GRID_W - WIN_C)
    cvalid = (c[None, :] >= sc[:, None]) & (c[None, :] < sc[:, None] + WIN_C)
    ci = np.clip(c[None, :] - c[:, None] + WIN_C - 1, 0, 2 * WIN_C - 2)
    r_onehot = (ri[..., None] == np.arange(2 * WIN_R - 1)).astype(np.float32)
    c_onehot = (ci[..., None] == np.arange(2 * WIN_C - 1)).astype(np.float32)
    valid = rvalid[:, :, None, :, None] & cvalid[None, None, :, None, :]
    return r_onehot, c_onehot, valid


def na_bias_table(rpb, rows):
    r_onehot, c_onehot, valid = _na_bias_index(rows)
    exact = lax.Precision.HIGHEST
    toep = jnp.einsum('hrd,qkd->hrqk', rpb, c_onehot, precision=exact)
    b = jnp.einsum('pirR,hRqk->phiqrk', r_onehot, toep, precision=exact)
    b = jnp.where(valid[:, None], b, -jnp.inf)
    return b.reshape(3, NA_HEADS, NA_QROWS * GRID_W, NA_KROWS * GRID_W).astype(F32)


def _attend_heads(q, key_sets, bias_fn):
    n, w = q.shape
    lane = lax.broadcasted_iota(jnp.int32, (1, w), 1)
    out = jnp.zeros((n, w), F32)
    for h in range(NA_HEADS):
        mh = (lane >= h * NA_HEAD_DIM) & (lane < (h + 1) * NA_HEAD_DIM)
        qh = jnp.where(mh, q, 0.0).astype(BF16)
        scores = []
        for i, (k, _) in enumerate(key_sets):
            s = lax.dot_general(qh, k, (((1,), (1,)), ((), ())), preferred_element_type=F32)
            b = bias_fn(i, h)
            scores.append(s if b is None else s + b)
        m = scores[0].max(axis=-1, keepdims=True)
        for s in scores[1:]:
            m = jnp.maximum(m, s.max(axis=-1, keepdims=True))
        denom = jnp.zeros((n, 1), F32)
        acc = jnp.zeros((n, w), F32)
        for s, (_, v) in zip(scores, key_sets):
            p = jnp.exp(s - m)
            denom = denom + p.sum(axis=-1, keepdims=True)
            acc = acc + jnp.dot(p.astype(BF16), v, preferred_element_type=F32)
        out = out + jnp.where(mh, acc / denom, 0.0)
    return out


def _na_kernel(q_ref, k_ref, v_ref, kc_ref, vc_ref, bias_ref, o_ref, *, rows):
    j = pl.program_id(1)
    start = jnp.clip(j * NA_QROWS - WIN_R // 2, 0, rows - NA_KROWS)
    t0 = pl.multiple_of(start * GRID_W, GRID_W)
    nk = NA_KROWS * GRID_W
    kw = k_ref[0, pl.ds(t0, nk), :].astype(BF16)
    vw = v_ref[0, pl.ds(t0, nk), :].astype(BF16)
    kc = kc_ref[0].astype(BF16)
    vc = vc_ref[0].astype(BF16)
    q = q_ref[0] * (1.0 / math.sqrt(NA_HEAD_DIM))
    o_ref[0] = _attend_heads(q, [(kw, vw), (kc, vc)], lambda i, h: bias_ref[0, h] if i == 0 else None)


def na_attention(pr3, prc3, bias):
    bn, S, _ = pr3.shape
    Lc = prc3.shape[1]
    rows = S // GRID_W
    nblk = rows // NA_QROWS
    nq = NA_QROWS * GRID_W

    def pat(b, j):
        return (jnp.where(j == 0, 0, jnp.where(j == nblk - 1, 2, 1)), 0, 0, 0)

    return pl.pallas_call(
        functools.partial(_na_kernel, rows=rows),
        out_shape=jax.ShapeDtypeStruct((bn, S, W_NA), F32),
        grid=(bn, nblk),
        in_specs=[pl.BlockSpec((1, nq, W_NA), lambda b, j: (b, j, Q_COL)),
                  pl.BlockSpec((1, S, W_NA), lambda b, j: (b, 0, K_COL)),
                  pl.BlockSpec((1, S, W_NA), lambda b, j: (b, 0, V_COL)),
                  pl.BlockSpec((1, Lc, W_NA), lambda b, j: (b, 0, K_COL)),
                  pl.BlockSpec((1, Lc, W_NA), lambda b, j: (b, 0, V_COL)),
                  pl.BlockSpec((1,) + bias.shape[1:], pat)],
        out_specs=pl.BlockSpec((1, nq, W_NA), lambda b, j: (b, j, 0)),
        compiler_params=_params("parallel", "arbitrary"),
        name="na_attention",
    )(pr3, pr3, pr3, prc3, prc3, bias)


def _ctx_attn_kernel(q_ref, k_ref, v_ref, o_ref):
    q = q_ref[0] * (1.0 / math.sqrt(NA_HEAD_DIM))
    o_ref[0] = _attend_heads(q, [(k_ref[0].astype(BF16), v_ref[0].astype(BF16))], lambda i, h: None)


def ctx_attention(prc3):
    bn, Lc, _ = prc3.shape
    return pl.pallas_call(
        _ctx_attn_kernel,
        out_shape=jax.ShapeDtypeStruct((bn, Lc, W_NA), F32),
        grid=(bn,),
        in_specs=[pl.BlockSpec((1, Lc, W_NA), lambda b: (b, 0, Q_COL)),
                  pl.BlockSpec((1, Lc, W_NA), lambda b: (b, 0, K_COL)),
                  pl.BlockSpec((1, Lc, W_NA), lambda b: (b, 0, V_COL))],
        out_specs=pl.BlockSpec((1, Lc, W_NA), lambda b: (b, 0, 0)),
        compiler_params=_params("parallel"),
        name="ctx_attention",
    )(prc3, prc3, prc3)


def _rmsnorm(x, g):
    xf = x.astype(F32)
    return xf * lax.rsqrt(jnp.mean(xf * xf, axis=-1, keepdims=True) + EPS) * g


def _dwconv_centered(u, w, b=None):
    k = w.shape[0]
    L = u.shape[1]
    p = k // 2
    up = jnp.pad(u, ((0, 0), (p, p), (0, 0)))
    y = up[:, 0:L] * w[0]
    for i in range(1, k):
        y = y + up[:, i:i + L] * w[i]
    return y if b is None else y + b


def _pool_mix(u, w_grp, scale):
    bn, L, _ = u.shape
    s = jnp.pad(jnp.cumsum(u.astype(F32), axis=1), ((0, 0), (1, 0), (0, 0)))
    t = np.arange(L)
    outs = []
    for g, win in enumerate(POOL_WINDOWS):
        lo = np.clip(t - win // 2, 0, L)
        hi = np.clip(t - win // 2 + win, 0, L)
        sg = s[..., g * POOL_GROUP:(g + 1) * POOL_GROUP]
        cnt = jnp.asarray((hi - lo)[:, None], F32)
        mean = (sg[:, hi] - sg[:, lo]) / cnt
        outs.append(mean - u[..., g * POOL_GROUP:(g + 1) * POOL_GROUP])
    p = jnp.stack(outs, axis=2)
    y = jnp.einsum('blgc,gcd->blgd', p, w_grp).reshape(bn, L, W_POOL)
    return y * scale


def _short_conv_mix(pr, w_conv):
    h, bg, cg = jnp.split(pr, 3, axis=-1)
    return bg * _dwconv_centered(cg * h, w_conv)


def _ssd_inputs(pr, conv_w, conv_b):
    bn, L, _ = pr.shape
    z, xbc, dt_raw = jnp.split(pr, [W_SSD, W_SSD + SSD_XBC], axis=-1)
    xbc = jax.nn.silu(_dwconv_centered(xbc, conv_w, conv_b))
    xs, bs, cs = jnp.split(xbc, [W_SSD, W_SSD + SSD_GROUPS * SSD_STATE], axis=-1)
    xs = xs.reshape(bn, L, SSD_HEADS, SSD_HEAD_DIM)
    bs = bs.reshape(bn, L, SSD_GROUPS, SSD_STATE)
    cs = cs.reshape(bn, L, SSD_GROUPS, SSD_STATE)
    return z, xs, bs, cs, dt_raw


def _ssd_chunked(x, dt, a, bmat, cmat, h0, need_y):
    bn, L, H, P = x.shape
    T = SSD_CHUNK
    nc = L // T
    rep = H // bmat.shape[2]
    bh = jnp.repeat(bmat.astype(F32), rep, axis=2).reshape(bn, nc, T, H, -1)
    ch = jnp.repeat(cmat.astype(F32), rep, axis=2).reshape(bn, nc, T, H, -1)
    xdt = (x.astype(F32) * dt[..., None]).reshape(bn, nc, T, H, P)
    la = (dt * a).reshape(bn, nc, T, H).transpose(0, 3, 1, 2)
    acum = jnp.cumsum(la, axis=-1)
    states = jnp.einsum('bclhn,bhcl,bclhp->bchpn', bh, jnp.exp(acum[..., -1:] - acum), xdt)
    states = jnp.concatenate([h0[:, None].astype(F32), states], axis=1)
    tot = jnp.pad(jnp.cumsum(acum[..., -1], axis=-1), ((0, 0), (0, 0), (1, 0)))
    tril_c = np.tril(np.ones((nc + 1, nc + 1), dtype=bool))
    decay_chunk = jnp.exp(jnp.where(tril_c, tot[..., :, None] - tot[..., None, :], -jnp.inf))
    states = jnp.einsum('bhzc,bchpn->bzhpn', decay_chunk, states)
    h_final = states[:, -1]
    if not need_y:
        return None, h_final
    tril_t = np.tril(np.ones((T, T), dtype=bool))
    lmat = jnp.exp(jnp.where(tril_t, acum[..., :, None] - acum[..., None, :], -jnp.inf))
    scores = jnp.einsum('bclhn,bcshn->bhcls', ch, bh) * lmat
    y = (jnp.einsum('bhcls,bcshp->bclhp', scores, xdt)
         + jnp.einsum('bclhn,bchpn,bhcl->bclhp', ch, states[:, :-1], jnp.exp(acum)))
    return y.reshape(bn, L, H, P), h_final


def _rev(t, d):
    return t if d == 0 else jnp.flip(t, axis=1)


def _ssd_mix(pr, pr_c, conv_w, conv_b, dt_bias, a_log, d_skip, norm_g, ctx_out):
    z, xs, bs, cs, dtr = _ssd_inputs(pr, conv_w, conv_b)
    zc, xsc, bsc, csc, dtrc = _ssd_inputs(pr_c, conv_w, conv_b)
    h0 = jnp.zeros((pr.shape[0], SSD_HEADS, SSD_HEAD_DIM, SSD_STATE), F32)
    y = xs * d_skip[:, None]
    yc = xsc * d_skip[:, None] if ctx_out else None
    for d in range(2):
        a = -jnp.exp(a_log[d])
        hs = slice(d * SSD_HEADS, (d + 1) * SSD_HEADS)
        dt_l = jax.nn.softplus(dtr[..., hs] + dt_bias[d])
        dt_c = jax.nn.softplus(dtrc[..., hs] + dt_bias[d])
        y_cd, h_c = _ssd_chunked(_rev(xsc, d), _rev(dt_c, d), a, _rev(bsc, d), _rev(csc, d), h0, ctx_out)
        y_ld, _ = _ssd_chunked(_rev(xs, d), _rev(dt_l, d), a, _rev(bs, d), _rev(cs, d), h_c, True)
        y = y + _rev(y_ld, d)
        if ctx_out:
            yc = yc + _rev(y_cd, d)
    o = _rmsnorm(y.reshape(z.shape) * jax.nn.silu(z), norm_g)
    if not ctx_out:
        return o, None
    oc = _rmsnorm(yc.reshape(zc.shape) * jax.nn.silu(zc), norm_g)
    return o, oc


def _mixer(pr, pr_c, pool_w, pool_scale, conv_w, rpb, s_conv_w, s_conv_b, dt_bias, a_log, d_skip,
           s_norm_g, ctx_out):
    bn, S, _ = pr.shape
    o_na = na_attention(pr, pr_c, na_bias_table(rpb, S // GRID_W))
    oc_na = ctx_attention(pr_c) if ctx_out else None
    pr = pr[..., :D_IN_PROJ]
    pr_c = pr_c[..., :D_IN_PROJ]
    p_pool, p_conv, _, p_ssd = jnp.split(pr, PROJ_SPLITS, axis=-1)
    c_pool, c_conv, _, c_ssd = jnp.split(pr_c, PROJ_SPLITS, axis=-1)
    o_ssd, oc_ssd = _ssd_mix(p_ssd, c_ssd, s_conv_w, s_conv_b, dt_bias, a_log, d_skip, s_norm_g, ctx_out)
    o = jnp.concatenate([_pool_mix(p_pool, pool_w, pool_scale),
                         _short_conv_mix(p_conv, conv_w),
                         o_na,
                         o_ssd], axis=-1)
    if not ctx_out:
        return o, None
    oc = jnp.concatenate([_pool_mix(c_pool, pool_w, pool_scale),
                          _short_conv_mix(c_conv, conv_w),
                          oc_na,
                          oc_ssd], axis=-1)
    return o, oc


TM = 512
TM_CTX = 256
TF = 1408
TM_GRP = 512


def kernel(x, c, ctx, c_ctx, w_ada, b_ada, g_mix, g_ffn, w_in, w_out, pool_w, pool_scale, conv_w, na_rpb,
           ssd_conv_w, ssd_conv_b, ssd_dt_bias, ssd_a_log, ssd_d, ssd_norm_g, ffn_w_gu, ffn_w_down,
           moe_router, moe_w_gu, moe_w_down, g_final):
    bn, S, d = x.shape
    Lc = ctx.shape[1]
    m, mc = bn * S, bn * Lc
    tpb = S // TM
    x2 = x.reshape(m, d)
    xc2 = ctx.reshape(mc, d)
    c_rows = jnp.concatenate([c, c_ctx[None, :], jnp.zeros((7, d), F32)], axis=0)
    for l in range(DEPTH):
        ctx_out = l < DEPTH - 1
        last = l == DEPTH - 1
        ada = ada_modulation(c_rows, w_ada[l].astype(BF16), b_ada[l][None, :])
        mod = ada[:bn].reshape(bn, 6, d)
        mod_c = ada[bn:bn + 1].reshape(1, 6, d)
        w_in_l = jnp.pad(w_in[l], ((0, 0), (0, D_IN_PAD - D_IN_PROJ))).astype(BF16)
        g_m = g_mix[l][None, :]
        g_f = g_ffn[l][None, :]
        pr = in_proj(x2, g_m, mod, w_in_l, tpb, TM)
        pr_c = in_proj(xc2, g_m, mod_c, w_in_l, None, TM_CTX)
        o, oc = _mixer(pr.reshape(bn, S, -1), pr_c.reshape(bn, Lc, -1), pool_w[l], pool_scale[l], conv_w[l],
                       na_rpb[l], ssd_conv_w[l], ssd_conv_b[l], ssd_dt_bias[l], ssd_a_log[l], ssd_d[l],
                       ssd_norm_g[l], ctx_out)
        w_out_l = w_out[l].astype(BF16)
        x2 = out_proj(x2, o.reshape(m, -1), mod, w_out_l, tpb, TM)
        if ctx_out:
            xc2 = out_proj(xc2, oc.reshape(mc, -1), mod_c, w_out_l, None, TM_CTX)
        j = l // 2
        if l % 2 == 0:
            w_gu = ffn_w_gu[j].astype(BF16)
            w_dn = ffn_w_down[j].astype(BF16)
            x2 = ffn_dense(x2, g_f, mod, w_gu, w_dn, tpb, TM, TF)
            if ctx_out:
                xc2 = ffn_dense(xc2, g_f, mod_c, w_gu, w_dn, None, TM_CTX, TF)
        else:
            w_r = jnp.pad(moe_router[j], ((0, 0), (0, LANES - N_EXPERTS))).astype(BF16)
            w_gu = moe_w_gu[j].astype(BF16)
            w_dn = moe_w_down[j].astype(BF16)
            y = moe_block(x2, g_f, mod, w_r, w_gu, w_dn, tpb, TM, TM_GRP)
            x2 = residual_gate(x2, y, mod, tpb, TM, g_final[None, :] if last else None)
            if ctx_out:
                yc = moe_block(xc2, g_f, mod_c, w_r, w_gu, w_dn, None, TM_CTX, TM_GRP)
                xc2 = residual_gate(xc2, yc, mod_c, None, TM_CTX)
            if last:
                return x2.reshape(bn, S, d)
    return final_norm(x2, g_final[None, :], TM).reshape(bn, S, d)
```

```python
import functools
import math

import numpy as np
import jax
import jax.numpy as jnp
from jax import lax
from jax.experimental import pallas as pl
from jax.experimental.pallas import tpu as pltpu

D_MODEL = 1024
DEPTH = 2
GRID_W = 64
EPS = 1e-6
W_POOL = 256
W_CONV = 256
W_NA = 256
W_SSD = 256
POOL_WINDOWS = (2, 4, 8, 16)
POOL_GROUP = W_POOL // len(POOL_WINDOWS)
NA_HEADS = 4
NA_HEAD_DIM = W_NA // NA_HEADS
WIN_R = 8
WIN_C = 16
COL_BLOCK = 16
COL_BAND = 32
SSD_HEAD_DIM = 64
SSD_HEADS = W_SSD // SSD_HEAD_DIM
SSD_GROUPS = 2
SSD_STATE = 128
SSD_CHUNK = 128
SSD_XBC = W_SSD + 2 * SSD_GROUPS * SSD_STATE
D_IN_PROJ = W_POOL + 3 * W_CONV + 3 * W_NA + W_SSD + SSD_XBC + 2 * SSD_HEADS
PROJ_SPLITS = (W_POOL, W_POOL + 3 * W_CONV, W_POOL + 3 * W_CONV + 3 * W_NA)
D_FF = 2816
N_EXPERTS = 8
TOP_K = 2
D_FF_EXPERT = 1408

LANES = 128
D_IN_PAD = -(-D_IN_PROJ // LANES) * LANES
VMEM_LIMIT = 56 * 1024 * 1024
BF16 = jnp.bfloat16
F32 = jnp.float32


def _params(*sem):
    return pltpu.CompilerParams(dimension_semantics=sem, vmem_limit_bytes=VMEM_LIMIT)


def _norm_mod(x, g, scale, shift):
    ms = jnp.mean(x * x, axis=-1, keepdims=True)
    return (x * lax.rsqrt(ms + EPS) * g) * (1.0 + scale) + shift


def _mod_map(tiles_per_batch):
    if tiles_per_batch is None:
        return lambda i, *_: (0, 0, 0)
    return lambda i, *_: (i // tiles_per_batch, 0, 0)


def _ada_kernel(c_ref, w_ref, b_ref, o_ref):
    c = c_ref[...]
    s = c * jax.nn.sigmoid(c)
    o_ref[...] = jnp.dot(s.astype(BF16), w_ref[...], preferred_element_type=F32) + b_ref[...]


def ada_modulation(c_rows, w, b):
    r, d = c_rows.shape
    n = w.shape[1]
    tn = 1536
    return pl.pallas_call(
        _ada_kernel,
        out_shape=jax.ShapeDtypeStruct((r, n), F32),
        grid=(n // tn,),
        in_specs=[pl.BlockSpec((r, d), lambda j: (0, 0)),
                  pl.BlockSpec((d, tn), lambda j: (0, j)),
                  pl.BlockSpec((1, tn), lambda j: (0, j))],
        out_specs=pl.BlockSpec((r, tn), lambda j: (0, j)),
        compiler_params=_params("arbitrary"),
        name="ada_modulation",
    )(c_rows, w, b)


def _in_proj_kernel(x_ref, g_ref, mod_ref, w_ref, o_ref):
    h = _norm_mod(x_ref[...], g_ref[...], mod_ref[0, 1:2, :], mod_ref[0, 0:1, :])
    o_ref[...] = jnp.dot(h.astype(BF16), w_ref[...], preferred_element_type=F32).astype(o_ref.dtype)


def in_proj(x2, g, mod, w, tiles_per_batch, tm, out_dtype=F32):
    m, d = x2.shape
    n = w.shape[1]
    return pl.pallas_call(
        _in_proj_kernel,
        out_shape=jax.ShapeDtypeStruct((m, n), out_dtype),
        grid=(m // tm,),
        in_specs=[pl.BlockSpec((tm, d), lambda i: (i, 0)),
                  pl.BlockSpec((1, d), lambda i: (0, 0)),
                  pl.BlockSpec((1, 6, d), _mod_map(tiles_per_batch)),
                  pl.BlockSpec((d, n), lambda i: (0, 0))],
        out_specs=pl.BlockSpec((tm, n), lambda i: (i, 0)),
        compiler_params=_params("parallel"),
        name="in_proj",
    )(x2, g, mod, w)


def _out_proj_kernel(x_ref, mod_ref, w_ref, *refs):
    part_refs, y_ref = refs[:-1], refs[-1]
    y = jnp.zeros(y_ref.shape, F32)
    k0 = 0
    for p_ref in part_refs:
        k = p_ref.shape[1]
        y = y + jnp.dot(p_ref[...].astype(BF16), w_ref[k0:k0 + k, :], preferred_element_type=F32)
        k0 += k
    y_ref[...] = x_ref[...] + mod_ref[0, 2:3, :] * y


def out_proj(x2, parts, mod, w, tiles_per_batch, tm):
    m, d = x2.shape
    return pl.pallas_call(
        _out_proj_kernel,
        out_shape=jax.ShapeDtypeStruct((m, d), F32),
        grid=(m // tm,),
        in_specs=[pl.BlockSpec((tm, d), lambda i: (i, 0)),
                  pl.BlockSpec((1, 6, d), _mod_map(tiles_per_batch)),
                  pl.BlockSpec(w.shape, lambda i: (0, 0))]
                 + [pl.BlockSpec((tm, p.shape[1]), lambda i: (i, 0)) for p in parts],
        out_specs=pl.BlockSpec((tm, d), lambda i: (i, 0)),
        compiler_params=_params("parallel"),
        name="out_proj",
    )(x2, mod, w, *parts)


def _ffn_kernel(x_ref, g_ref, mod_ref, wg_ref, wu_ref, wd_ref, y_ref, h_ref, acc_ref):
    f = pl.program_id(1)

    @pl.when(f == 0)
    def _():
        h = _norm_mod(x_ref[...], g_ref[...], mod_ref[0, 4:5, :], mod_ref[0, 3:4, :])
        h_ref[...] = h.astype(BF16)
        acc_ref[...] = jnp.zeros_like(acc_ref)

    h = h_ref[...]
    gg = jnp.dot(h, wg_ref[...], preferred_element_type=F32)
    uu = jnp.dot(h, wu_ref[...], preferred_element_type=F32)
    a = (gg * jax.nn.sigmoid(gg) * uu).astype(BF16)
    acc_ref[...] += jnp.dot(a, wd_ref[...], preferred_element_type=F32)

    @pl.when(f == pl.num_programs(1) - 1)
    def _():
        y_ref[...] = x_ref[...] + mod_ref[0, 5:6, :] * acc_ref[...]


def ffn_dense(x2, g, mod, w_gu, w_down, tiles_per_batch, tm, tf):
    m, d = x2.shape
    ff = w_down.shape[0]
    nf = ff // tf
    return pl.pallas_call(
        _ffn_kernel,
        out_shape=jax.ShapeDtypeStruct((m, d), F32),
        grid=(m // tm, nf),
        in_specs=[pl.BlockSpec((tm, d), lambda i, f: (i, 0)),
                  pl.BlockSpec((1, d), lambda i, f: (0, 0)),
                  pl.BlockSpec((1, 6, d), _mod_map(tiles_per_batch)),
                  pl.BlockSpec((d, tf), lambda i, f: (0, f)),
                  pl.BlockSpec((d, tf), lambda i, f: (0, nf + f)),
                  pl.BlockSpec((tf, d), lambda i, f: (f, 0))],
        out_specs=pl.BlockSpec((tm, d), lambda i, f: (i, 0)),
        scratch_shapes=[pltpu.VMEM((tm, d), BF16), pltpu.VMEM((tm, d), F32)],
        compiler_params=_params("parallel", "arbitrary"),
        name="ffn_dense",
    )(x2, g, mod, w_gu, w_gu, w_down)


def _router_kernel(x_ref, g_ref, mod_ref, wr_ref, h_ref, gate_ref):
    h = _norm_mod(x_ref[...], g_ref[...], mod_ref[0, 4:5, :], mod_ref[0, 3:4, :]).astype(BF16)
    h_ref[...] = h
    logits = jnp.dot(h, wr_ref[...], preferred_element_type=F32)
    lane = lax.broadcasted_iota(jnp.int32, logits.shape, 1)
    neg = jnp.float32(-jnp.inf)
    logits = jnp.where(lane < N_EXPERTS, logits, neg)
    m1 = jnp.max(logits, axis=-1, keepdims=True)
    i1 = jnp.min(jnp.where(logits == m1, lane, LANES), axis=-1, keepdims=True)
    rest = jnp.where(lane == i1, neg, logits)
    m2 = jnp.max(rest, axis=-1, keepdims=True)
    i2 = jnp.min(jnp.where(rest == m2, lane, LANES), axis=-1, keepdims=True)
    e2 = jnp.exp(m2 - m1)
    g1 = 1.0 / (1.0 + e2)
    g2 = e2 / (1.0 + e2)
    gate_ref[...] = jnp.where(lane == i1, g1, jnp.where(lane == i2, g2, -1.0))


def moe_router(x2, g, mod, w_router, tiles_per_batch, tm):
    m, d = x2.shape
    return pl.pallas_call(
        _router_kernel,
        out_shape=(jax.ShapeDtypeStruct((m, d), BF16), jax.ShapeDtypeStruct((m, LANES), F32)),
        grid=(m // tm,),
        in_specs=[pl.BlockSpec((tm, d), lambda i: (i, 0)),
                  pl.BlockSpec((1, d), lambda i: (0, 0)),
                  pl.BlockSpec((1, 6, d), _mod_map(tiles_per_batch)),
                  pl.BlockSpec((d, LANES), lambda i: (0, 0))],
        out_specs=(pl.BlockSpec((tm, d), lambda i: (i, 0)),
                   pl.BlockSpec((tm, LANES), lambda i: (i, 0))),
        compiler_params=_params("parallel"),
        name="moe_router",
    )(x2, g, mod, w_router)


def _moe_kernel(te_ref, nt_ref, xg_ref, gate_ref, wg_ref, wu_ref, wd_ref, y_ref):
    i = pl.program_id(0)

    @pl.when(i < nt_ref[0])
    def _():
        h = xg_ref[...]
        gg = jnp.dot(h, wg_ref[0], preferred_element_type=F32)
        uu = jnp.dot(h, wu_ref[0], preferred_element_type=F32)
        a = (gg * jax.nn.sigmoid(gg) * uu).astype(BF16)
        y = jnp.dot(a, wd_ref[0], preferred_element_type=F32)
        y_ref[...] = (gate_ref[...] * y).astype(y_ref.dtype)

    @pl.when(i >= nt_ref[0])
    def _():
        y_ref[...] = jnp.zeros_like(y_ref)


def moe_grouped(tile_expert, n_tiles_used, xg, row_gate, w_gu, w_down, tm):
    p, d = xg.shape
    fe = w_down.shape[1]
    grid_spec = pltpu.PrefetchScalarGridSpec(
        num_scalar_prefetch=2,
        grid=(p // tm,),
        in_specs=[pl.BlockSpec((tm, d), lambda i, te, nt: (i, 0)),
                  pl.BlockSpec((tm, 1), lambda i, te, nt: (i, 0)),
                  pl.BlockSpec((1, d, fe), lambda i, te, nt: (te[i], 0, 0)),
                  pl.BlockSpec((1, d, fe), lambda i, te, nt: (te[i], 0, 1)),
                  pl.BlockSpec((1, fe, d), lambda i, te, nt: (te[i], 0, 0))],
        out_specs=pl.BlockSpec((tm, d), lambda i, te, nt: (i, 0)),
    )
    return pl.pallas_call(
        _moe_kernel,
        out_shape=jax.ShapeDtypeStruct((p, d), F32),
        grid_spec=grid_spec,
        compiler_params=_params("arbitrary"),
        name="moe_grouped",
    )(tile_expert, n_tiles_used, xg, row_gate, w_gu, w_gu, w_down)


def moe_block(x2, g, mod, w_router, w_gu, w_down, tiles_per_batch, tm_tok, tm_grp):
    m, d = x2.shape
    h, gate_dense = moe_router(x2, g, mod, w_router, tiles_per_batch, tm_tok)
    gates = gate_dense[:, :N_EXPERTS]
    sel = gates >= 0.0
    rank = jnp.cumsum(sel.astype(jnp.int32), axis=0) - 1
    cnt = jnp.sum(sel.astype(jnp.int32), axis=0)
    gsize = ((cnt + tm_grp - 1) // tm_grp) * tm_grp
    gend = jnp.cumsum(gsize)
    gstart = gend - gsize
    p_max = m * TOP_K + N_EXPERTS * tm_grp
    n_tiles = p_max // tm_grp
    dest = jnp.where(sel, gstart[None, :] + rank, p_max)
    d_lo = jnp.min(dest, axis=1)
    d_hi = jnp.max(jnp.where(sel, dest, -1), axis=1)
    g_lo = jnp.take_along_axis(gates, jnp.argmin(dest, axis=1)[:, None], axis=1)[:, 0]
    g_hi = jnp.sum(jnp.where(sel, gates, 0.0), axis=1) - g_lo
    tok = jnp.arange(m, dtype=jnp.int32)
    dests = jnp.concatenate([d_lo, d_hi])
    src = jnp.zeros((p_max,), jnp.int32).at[dests].set(jnp.concatenate([tok, tok]), unique_indices=True)
    row_gate = jnp.zeros((p_max,), F32).at[dests].set(jnp.concatenate([g_lo, g_hi]), unique_indices=True)
    tile_start = jnp.arange(n_tiles, dtype=jnp.int32) * tm_grp
    tile_expert = jnp.minimum(jnp.sum((tile_start[:, None] >= gend[None, :]).astype(jnp.int32), axis=1),
                              N_EXPERTS - 1).astype(jnp.int32)
    n_used = (gend[-1] // tm_grp).astype(jnp.int32).reshape(1)
    xg = jnp.take(h, src, axis=0)
    yg = moe_grouped(tile_expert, n_used, xg, row_gate[:, None], w_gu, w_down, tm_grp)
    return jnp.take(yg, d_lo, axis=0) + jnp.take(yg, d_hi, axis=0)


def _residual_kernel(x_ref, y_ref, mod_ref, o_ref):
    o_ref[...] = x_ref[...] + mod_ref[0, 5:6, :] * y_ref[...]


def _residual_norm_kernel(x_ref, y_ref, mod_ref, g_ref, o_ref):
    x = x_ref[...] + mod_ref[0, 5:6, :] * y_ref[...]
    ms = jnp.mean(x * x, axis=-1, keepdims=True)
    o_ref[...] = x * lax.rsqrt(ms + EPS) * g_ref[...]


def residual_gate(x2, y2, mod, tiles_per_batch, tm, g_final=None):
    m, d = x2.shape
    row = pl.BlockSpec((tm, d), lambda i: (i, 0))
    specs = [row, row, pl.BlockSpec((1, 6, d), _mod_map(tiles_per_batch))]
    args = [x2, y2, mod]
    body = _residual_kernel
    if g_final is not None:
        specs.append(pl.BlockSpec((1, d), lambda i: (0, 0)))
        args.append(g_final)
        body = _residual_norm_kernel
    return pl.pallas_call(
        body,
        out_shape=jax.ShapeDtypeStruct((m, d), F32),
        grid=(m // tm,),
        in_specs=specs,
        out_specs=row,
        compiler_params=_params("parallel"),
        name="residual_gate",
    )(*args)


def _final_norm_kernel(x_ref, g_ref, o_ref):
    x = x_ref[...]
    ms = jnp.mean(x * x, axis=-1, keepdims=True)
    o_ref[...] = x * lax.rsqrt(ms + EPS) * g_ref[...]


def final_norm(x2, g, tm):
    m, d = x2.shape
    return pl.pallas_call(
        _final_norm_kernel,
        out_shape=jax.ShapeDtypeStruct((m, d), F32),
        grid=(m // tm,),
        in_specs=[pl.BlockSpec((tm, d), lambda i: (i, 0)), pl.BlockSpec((1, d), lambda i: (0, 0))],
        out_specs=pl.BlockSpec((tm, d), lambda i: (i, 0)),
        compiler_params=_params("parallel"),
        name="final_norm",
    )(x2, g)


NA_QROWS = 4
NA_KROWS = NA_QROWS + WIN_R
Q_COL, K_COL, V_COL = 4, 5, 6


def _na_key_start(j, rows):
    return np.clip(j * NA_QROWS - WIN_R // 2, 0, rows - NA_KROWS)


def _na_bias_index(rows):
    nblk = rows // NA_QROWS
    pats = []
    for j in range(nblk):
        start = _na_key_start(j, rows)
        r = j * NA_QROWS + np.arange(NA_QROWS)
        sr = np.clip(r - WIN_R // 2, 0, rows - WIN_R)
        kr = start + np.arange(NA_KROWS)
        rvalid = (kr[None, :] >= sr[:, None]) & (kr[None, :] < sr[:, None] + WIN_R)
        ri = np.clip(kr[None, :] - r[:, None] + WIN_R - 1, 0, 2 * WIN_R - 2)
        pats.append((ri, rvalid))
    for j in range(2, nblk - 1):
        assert all(np.array_equal(a, b) for a, b in zip(pats[1], pats[j]))
    sel = [pats[0], pats[1], pats[nblk - 1]]
    ri = np.stack([p[0] for p in sel])
    rvalid = np.stack([p[1] for p in sel])
    c = np.arange(GRID_W)
    sc = np.clip(c - WIN_C // 2, 0, GRID_W - WIN_C)
    cvalid = (c[None, :] >= sc[:, None]) & (c[None, :] < sc[:, None] + WIN_C)
    ci = np.clip(c[None, :] - c[:, None] + WIN_C - 1, 0, 2 * WIN_C - 2)
    r_onehot = (ri[..., None] == np.arange(2 * WIN_R - 1)).astype(np.float32)
    c_onehot = (ci[..., None] == np.arange(2 * WIN_C - 1)).astype(np.float32)
    valid = rvalid[:, :, None, :, None] & cvalid[None, None, :, None, :]
    return r_onehot, c_onehot, valid


def na_bias_table(rpb, rows):
    r_onehot, c_onehot, valid = _na_bias_index(rows)
    exact = lax.Precision.HIGHEST
    toep = jnp.einsum('hrd,qkd->hrqk', rpb, c_onehot, precision=exact)
    b = jnp.einsum('pirR,hRqk->phiqrk', r_onehot, toep, precision=exact)
    b = jnp.where(valid[:, None], b, -jnp.inf)
    return b.reshape(3, NA_HEADS, NA_QROWS * GRID_W, NA_KROWS * GRID_W).astype(F32)


def _attend_heads(q, key_sets, bias_fn):
    n, w = q.shape
    lane = lax.broadcasted_iota(jnp.int32, (1, w), 1)
    out = jnp.zeros((n, w), F32)
    for h in range(NA_HEADS):
        mh = (lane >= h * NA_HEAD_DIM) & (lane < (h + 1) * NA_HEAD_DIM)
        qh = jnp.where(mh, q, 0.0).astype(BF16)
        scores = []
        for i, (k, _) in enumerate(key_sets):
            s = lax.dot_general(qh, k, (((1,), (1,)), ((), ())), preferred_element_type=F32)
            b = bias_fn(i, h)
            scores.append(s if b is None else s + b)
        m = scores[0].max(axis=-1, keepdims=True)
        for s in scores[1:]:
            m = jnp.maximum(m, s.max(axis=-1, keepdims=True))
        denom = jnp.zeros((n, 1), F32)
        acc = jnp.zeros((n, w), F32)
        for s, (_, v) in zip(scores, key_sets):
            p = jnp.exp(s - m)
            denom = denom + p.sum(axis=-1, keepdims=True)
            acc = acc + jnp.dot(p.astype(BF16), v, preferred_element_type=F32)
        out = out + jnp.where(mh, acc / denom, 0.0)
    return out


def _na_kernel(q_ref, k_ref, v_ref, kc_ref, vc_ref, bias_ref, o_ref, *, rows):
    j = pl.program_id(1)
    start = jnp.clip(j * NA_QROWS - WIN_R // 2, 0, rows - NA_KROWS)
    t0 = pl.multiple_of(start * GRID_W, GRID_W)
    nk = NA_KROWS * GRID_W
    kw = k_ref[0, pl.ds(t0, nk), :].astype(BF16)
    vw = v_ref[0, pl.ds(t0, nk), :].astype(BF16)
    kc = kc_ref[0].astype(BF16)
    vc = vc_ref[0].astype(BF16)
    q = q_ref[0] * (1.0 / math.sqrt(NA_HEAD_DIM))
    o_ref[0] = _attend_heads(q, [(kw, vw), (kc, vc)], lambda i, h: bias_ref[0, h] if i == 0 else None)


def na_attention(pr3, prc3, bias):
    bn, S, _ = pr3.shape
    Lc = prc3.shape[1]
    rows = S // GRID_W
    nblk = rows // NA_QROWS
    nq = NA_QROWS * GRID_W

    def pat(b, j):
        return (jnp.where(j == 0, 0, jnp.where(j == nblk - 1, 2, 1)), 0, 0, 0)

    return pl.pallas_call(
        functools.partial(_na_kernel, rows=rows),
        out_shape=jax.ShapeDtypeStruct((bn, S, W_NA), F32),
        grid=(bn, nblk),
        in_specs=[pl.BlockSpec((1, nq, W_NA), lambda b, j: (b, j, Q_COL)),
                  pl.BlockSpec((1, S, W_NA), lambda b, j: (b, 0, K_COL)),
                  pl.BlockSpec((1, S, W_NA), lambda b, j: (b, 0, V_COL)),
                  pl.BlockSpec((1, Lc, W_NA), lambda b, j: (b, 0, K_COL)),
                  pl.BlockSpec((1, Lc, W_NA), lambda b, j: (b, 0, V_COL)),
                  pl.BlockSpec((1,) + bias.shape[1:], pat)],
        out_specs=pl.BlockSpec((1, nq, W_NA), lambda b, j: (b, j, 0)),
        compiler_params=_params("parallel", "arbitrary"),
        name="na_attention",
    )(pr3, pr3, pr3, prc3, prc3, bias)


def _ctx_attn_kernel(q_ref, k_ref, v_ref, o_ref):
    q = q_ref[0] * (1.0 / math.sqrt(NA_HEAD_DIM))
    o_ref[0] = _attend_heads(q, [(k_ref[0].astype(BF16), v_ref[0].astype(BF16))], lambda i, h: None)


def ctx_attention(prc3):
    bn, Lc, _ = prc3.shape
    return pl.pallas_call(
        _ctx_attn_kernel,
        out_shape=jax.ShapeDtypeStruct((bn, Lc, W_NA), F32),
        grid=(bn,),
        in_specs=[pl.BlockSpec((1, Lc, W_NA), lambda b: (b, 0, Q_COL)),
                  pl.BlockSpec((1, Lc, W_NA), lambda b: (b, 0, K_COL)),
                  pl.BlockSpec((1, Lc, W_NA), lambda b: (b, 0, V_COL))],
        out_specs=pl.BlockSpec((1, Lc, W_NA), lambda b: (b, 0, 0)),
        compiler_params=_params("parallel"),
        name="ctx_attention",
    )(prc3, prc3, prc3)


POOL_COL, CONV_H_COL, CONV_B_COL, CONV_C_COL = 0, 1, 2, 3


def _shift_rows(x, k, row):
    n = x.shape[0]
    y = pltpu.roll(x, k % n, 0)
    return jnp.where(row < k, 0.0, y) if k > 0 else jnp.where(row >= n + k, 0.0, y)


def _local_mix_kernel(u_ref, h_ref, bg_ref, cg_ref, pw_ref, ps_ref, cw_ref, op_ref, oc_ref, *, seq):
    row = lax.broadcasted_iota(jnp.int32, (seq, 1), 0)
    lane = lax.broadcasted_iota(jnp.int32, (1, W_POOL), 1)
    u = u_ref[0]
    trailing, leading = {1: u}, {1: u}
    for w in (1, 2, 4):
        trailing[2 * w] = trailing[w] + _shift_rows(trailing[w], w, row)
        leading[2 * w] = leading[w] + _shift_rows(leading[w], -w, row)
    rowf = row.astype(F32)
    win_sum = jnp.zeros_like(u)
    cnt = jnp.zeros_like(u)
    for g, win in enumerate(POOL_WINDOWS):
        half = win // 2
        mg = (lane >= g * POOL_GROUP) & (lane < (g + 1) * POOL_GROUP)
        s = _shift_rows(trailing[half], 1, row) + leading[half]
        n = jnp.minimum(rowf + half, float(seq)) - jnp.maximum(rowf - half, 0.0)
        win_sum = jnp.where(mg, s, win_sum)
        cnt = jnp.where(mg, n, cnt)
    p = win_sum / cnt - u
    op_ref[0] = jnp.dot(p.astype(BF16), pw_ref[...], preferred_element_type=F32) * ps_ref[...]
    v = cg_ref[0] * h_ref[0]
    conv = (_shift_rows(v, 1, row) * cw_ref[0:1, :] + v * cw_ref[1:2, :] + _shift_rows(v, -1, row) * cw_ref[2:3, :])
    oc_ref[0] = bg_ref[0] * conv


def local_mix(pr3, pool_w, pool_scale, conv_w):
    bn, seq, _ = pr3.shape
    pw = jax.scipy.linalg.block_diag(*[pool_w[g] for g in range(len(POOL_WINDOWS))]).astype(BF16)
    blk = lambda col: pl.BlockSpec((1, seq, W_POOL), lambda b: (b, 0, col))
    full = lambda a: pl.BlockSpec(a.shape, lambda b: (0,) * a.ndim)
    args = (pw, pool_scale[None, :], conv_w)
    return pl.pallas_call(
        functools.partial(_local_mix_kernel, seq=seq),
        out_shape=(jax.ShapeDtypeStruct((bn, seq, W_POOL), F32), jax.ShapeDtypeStruct((bn, seq, W_CONV), F32)),
        grid=(bn,),
        in_specs=[blk(POOL_COL), blk(CONV_H_COL), blk(CONV_B_COL), blk(CONV_C_COL)] + [full(a) for a in args],
        out_specs=(pl.BlockSpec((1, seq, W_POOL), lambda b: (b, 0, 0)),
                   pl.BlockSpec((1, seq, W_CONV), lambda b: (b, 0, 0))),
        compiler_params=_params("parallel"),
        name="local_mix",
    )(pr3, pr3, pr3, pr3, *args)


Z_COL, XS_COL, BS_COL, CS_COL = 7, 8, 9, 10
DT_COL = 22
HEADS_PER_GROUP = SSD_HEADS // SSD_GROUPS


def _expand_heads(v, d):
    lane = lax.broadcasted_iota(jnp.int32, (1, W_SSD), 1)
    out = jnp.zeros((v.shape[0], W_SSD), F32)
    for h in range(SSD_HEADS):
        k = d * SSD_HEADS + h
        mh = (lane >= h * SSD_HEAD_DIM) & (lane < (h + 1) * SSD_HEAD_DIM)
        out = jnp.where(mh, v[:, k:k + 1], out)
    return out


def _ssd_chunk(c, d, act_ref, dt_ref, la_ref, y_ref, st_ref, need_y):
    T = SSD_CHUNK
    r0 = pl.multiple_of(c * T, T)
    xs = act_ref[pl.ds(r0, T), 0:W_SSD]
    bm = act_ref[pl.ds(r0, T), W_SSD:2 * W_SSD]
    cm = act_ref[pl.ds(r0, T), 2 * W_SSD:3 * W_SSD]
    dt = dt_ref[pl.ds(r0, T), :]
    la = la_ref[pl.ds(r0, T), :]
    li = lax.broadcasted_iota(jnp.int32, (T, T), 0)
    si = lax.broadcasted_iota(jnp.int32, (T, T), 1)
    mask = (si <= li) if d == 0 else (si >= li)
    acum = jnp.dot(mask.astype(F32), la, precision=lax.Precision.HIGHEST, preferred_element_type=F32)
    tot = acum[T - 1:T, :] if d == 0 else acum[0:1, :]
    acum_e = _expand_heads(acum, d)
    tot_e = _expand_heads(tot, d)
    xdt = xs * _expand_heads(dt, d)
    xw = (xdt * jnp.exp(tot_e - acum_e)).astype(BF16)
    st = st_ref[d]
    bt = bm.T.astype(BF16)
    if need_y:
        acum_t = acum.T
        xdt_b = xdt.astype(BF16)
        lane = lax.broadcasted_iota(jnp.int32, (1, W_SSD), 1)
        y = jnp.zeros((T, W_SSD), F32)
        y_state = []
    new_st = []
    gw = HEADS_PER_GROUP * SSD_HEAD_DIM
    for g in range(SSD_GROUPS):
        bg = bm[:, g * SSD_STATE:(g + 1) * SSD_STATE].astype(BF16)
        cg = cm[:, g * SSD_STATE:(g + 1) * SSD_STATE].astype(BF16)
        st_g = st[:, g * gw:(g + 1) * gw]
        if need_y:
            cb = lax.dot_general(cg, bg, (((1,), (1,)), ((), ())), preferred_element_type=F32)
            for h in range(g * HEADS_PER_GROUP, (g + 1) * HEADS_PER_GROUP):
                k = d * SSD_HEADS + h
                decay = jnp.exp(jnp.where(mask, acum[:, k:k + 1] - acum_t[k:k + 1, :], -jnp.inf))
                mh = (lane >= h * SSD_HEAD_DIM) & (lane < (h + 1) * SSD_HEAD_DIM)
                y = y + jnp.where(mh, jnp.dot((cb * decay).astype(BF16), xdt_b, preferred_element_type=F32), 0.0)
            y_state.append(jnp.dot(cg, st_g.astype(BF16), preferred_element_type=F32))
        upd = jnp.dot(bt[g * SSD_STATE:(g + 1) * SSD_STATE, :], xw[:, g * gw:(g + 1) * gw],
                      preferred_element_type=F32)
        new_st.append(jnp.exp(tot_e[:, g * gw:(g + 1) * gw]) * st_g + upd)
    st_ref[d] = jnp.concatenate(new_st, axis=1)
    if need_y:
        y_ref[pl.ds(r0, T), :] += y + jnp.concatenate(y_state, axis=1) * jnp.exp(acum_e)


def _ssd_kernel(z_ref, xs_ref, bs_ref, cs_ref, dtr_ref, cw_ref, cb_ref, dtb_ref, alog_ref, dsk_ref, ng_ref, h0_ref,
                *refs, seq, need_y):
    if need_y:
        o_ref, hT_ref, act_ref, dt_ref, la_ref, st_ref, y_ref = refs
    else:
        hT_ref, act_ref, dt_ref, la_ref, st_ref = refs
        y_ref = None
    nc = seq // SSD_CHUNK
    row = lax.broadcasted_iota(jnp.int32, (seq, 1), 0)
    for gi, ref in enumerate((xs_ref, bs_ref, cs_ref)):
        sl = slice(gi * W_SSD, (gi + 1) * W_SSD)
        x = ref[0]
        cv = (_shift_rows(x, 1, row) * cw_ref[0:1, sl] + x * cw_ref[1:2, sl]
              + _shift_rows(x, -1, row) * cw_ref[2:3, sl] + cb_ref[:, sl])
        act_ref[:, sl] = cv * jax.nn.sigmoid(cv)
    dtv = dtr_ref[0] + dtb_ref[...]
    dt = jnp.maximum(dtv, 0.0) + jnp.log1p(jnp.exp(-jnp.abs(dtv)))
    dt_ref[...] = dt
    la_ref[...] = dt * (-jnp.exp(alog_ref[...]))
    st_ref[...] = h0_ref[0]
    if need_y:
        y_ref[...] = act_ref[:, 0:W_SSD] * dsk_ref[...]

    def body(i, carry):
        _ssd_chunk(i, 0, act_ref, dt_ref, la_ref, y_ref, st_ref, need_y)
        _ssd_chunk(nc - 1 - i, 1, act_ref, dt_ref, la_ref, y_ref, st_ref, need_y)
        return carry

    lax.fori_loop(0, nc, body, 0)
    hT_ref[0] = st_ref[...]
    if need_y:
        z = z_ref[0]
        yz = y_ref[...] * (z * jax.nn.sigmoid(z))
        ms = jnp.mean(yz * yz, axis=-1, keepdims=True)
        o_ref[0] = yz * lax.rsqrt(ms + EPS) * ng_ref[...]


def ssd_scan(pr3, h0, conv_w, conv_b, dt_bias, a_log, d_skip, norm_g, need_y):
    bn, seq, _ = pr3.shape
    pad = LANES - 2 * SSD_HEADS
    dtb = jnp.pad(dt_bias.reshape(1, -1), ((0, 0), (0, pad)))
    alog = jnp.pad(a_log.reshape(1, -1), ((0, 0), (0, pad)))
    dsk = jnp.repeat(d_skip, SSD_HEAD_DIM)[None, :]
    blk = lambda col, w=W_SSD: pl.BlockSpec((1, seq, w), lambda b: (b, 0, col))
    full = lambda a: pl.BlockSpec(a.shape, lambda b: (0,) * a.ndim)
    st_shape = (1, 2, SSD_STATE, W_SSD)
    st_spec = pl.BlockSpec(st_shape, lambda b: (b, 0, 0, 0))
    args = (conv_w, conv_b[None, :], dtb, alog, dsk, norm_g[None, :])
    out_shape = [jax.ShapeDtypeStruct((bn,) + st_shape[1:], F32)]
    out_specs = [st_spec]
    scratch = [pltpu.VMEM((seq, 3 * W_SSD), F32), pltpu.VMEM((seq, LANES), F32), pltpu.VMEM((seq, LANES), F32),
               pltpu.VMEM(st_shape[1:], F32)]
    if need_y:
        out_shape.insert(0, jax.ShapeDtypeStruct((bn, seq, W_SSD), F32))
        out_specs.insert(0, pl.BlockSpec((1, seq, W_SSD), lambda b: (b, 0, 0)))
        scratch.append(pltpu.VMEM((seq, W_SSD), F32))
    res = pl.pallas_call(
        functools.partial(_ssd_kernel, seq=seq, need_y=need_y),
        out_shape=out_shape,
        grid=(bn,),
        in_specs=[blk(Z_COL), blk(XS_COL), blk(BS_COL), blk(CS_COL), blk(DT_COL, LANES)]
                 + [full(a) for a in args] + [st_spec],
        out_specs=out_specs,
        scratch_shapes=scratch,
        compiler_params=_params("parallel"),
        name="ssd_scan",
    )(pr3, pr3, pr3, pr3, pr3, *args, h0)
    return (res[0], res[1]) if need_y else (None, res[0])


def ssd_mix(pr3, prc3, conv_w, conv_b, dt_bias, a_log, d_skip, norm_g, ctx_out):
    bn = pr3.shape[0]
    h0 = jnp.zeros((bn, 2, SSD_STATE, W_SSD), F32)
    oc, hc = ssd_scan(prc3, h0, conv_w, conv_b, dt_bias, a_log, d_skip, norm_g, ctx_out)
    o, _ = ssd_scan(pr3, hc, conv_w, conv_b, dt_bias, a_log, d_skip, norm_g, True)
    return o, oc


def mixer(pr3, prc3, pool_w, pool_scale, conv_w, rpb, s_conv_w, s_conv_b, dt_bias, a_log, d_skip, s_norm_g,
          ctx_out):
    bn, S, _ = pr3.shape
    o_ssd, oc_ssd = ssd_mix(pr3, prc3, s_conv_w, s_conv_b, dt_bias, a_log, d_skip, s_norm_g, ctx_out)
    o = [*local_mix(pr3, pool_w, pool_scale, conv_w),
         na_attention(pr3, prc3, na_bias_table(rpb, S // GRID_W)), o_ssd]
    if not ctx_out:
        return o, None
    oc = [*local_mix(prc3, pool_w, pool_scale, conv_w), ctx_attention(prc3), oc_ssd]
    return o, oc


TM = 512
TM_CTX = 256
TF = 1408
TM_GRP = 512


def kernel(x, c, ctx, c_ctx, w_ada, b_ada, g_mix, g_ffn, w_in, w_out, pool_w, pool_scale, conv_w, na_rpb,
           ssd_conv_w, ssd_conv_b, ssd_dt_bias, ssd_a_log, ssd_d, ssd_norm_g, ffn_w_gu, ffn_w_down,
           moe_router, moe_w_gu, moe_w_down, g_final):
    bn, S, d = x.shape
    Lc = ctx.shape[1]
    m, mc = bn * S, bn * Lc
    tpb = S // TM
    x2 = x.reshape(m, d)
    xc2 = ctx.reshape(mc, d)
    c_rows = jnp.concatenate([c, c_ctx[None, :], jnp.zeros((7, d), F32)], axis=0)
    for l in range(DEPTH):
        ctx_out = l < DEPTH - 1
        last = l == DEPTH - 1
        ada = ada_modulation(c_rows, w_ada[l].astype(BF16), b_ada[l][None, :])
        mod = ada[:bn].reshape(bn, 6, d)
        mod_c = ada[bn:bn + 1].reshape(1, 6, d)
        w_in_l = jnp.pad(w_in[l], ((0, 0), (0, D_IN_PAD - D_IN_PROJ))).astype(BF16)
        g_m = g_mix[l][None, :]
        g_f = g_ffn[l][None, :]
        pr = in_proj(x2, g_m, mod, w_in_l, tpb, TM)
        pr_c = in_proj(xc2, g_m, mod_c, w_in_l, None, TM_CTX)
        o, oc = mixer(pr.reshape(bn, S, -1), pr_c.reshape(bn, Lc, -1), pool_w[l], pool_scale[l], conv_w[l],
                      na_rpb[l], ssd_conv_w[l], ssd_conv_b[l], ssd_dt_bias[l], ssd_a_log[l], ssd_d[l],
                      ssd_norm_g[l], ctx_out)
        w_out_l = w_out[l].astype(BF16)
        x2 = out_proj(x2, [p.reshape(m, -1) for p in o], mod, w_out_l, tpb, TM)
        if ctx_out:
            xc2 = out_proj(xc2, [p.reshape(mc, -1) for p in oc], mod_c, w_out_l, None, TM_CTX)
        j = l // 2
        if l % 2 == 0:
            w_gu = ffn_w_gu[j].astype(BF16)
            w_dn = ffn_w_down[j].astype(BF16)
            x2 = ffn_dense(x2, g_f, mod, w_gu, w_dn, tpb, TM, TF)
            if ctx_out:
                xc2 = ffn_dense(xc2, g_f, mod_c, w_gu, w_dn, None, TM_CTX, TF)
        else:
            w_r = jnp.pad(moe_router[j], ((0, 0), (0, LANES - N_EXPERTS))).astype(BF16)
            w_gu = moe_w_gu[j].astype(BF16)
            w_dn = moe_w_down[j].astype(BF16)
            y = moe_block(x2, g_f, mod, w_r, w_gu, w_dn, tpb, TM, TM_GRP)
            x2 = residual_gate(x2, y, mod, tpb, TM, g_final[None, :] if last else None)
            if ctx_out:
                yc = moe_block(xc2, g_f, mod_c, w_r, w_gu, w_dn, None, TM_CTX, TM_GRP)
                xc2 = residual_gate(xc2, yc, mod_c, None, TM_CTX)
            if last:
                return x2.reshape(bn, S, d)
    return final_norm(x2, g_final[None, :], TM).reshape(bn, S, d)
```

```python
import functools
import math

import numpy as np
import jax
import jax.numpy as jnp
from jax import lax
from jax.experimental import pallas as pl
from jax.experimental.pallas import tpu as pltpu

D_MODEL = 1024
DEPTH = 2
GRID_W = 64
EPS = 1e-6
W_POOL = 256
W_CONV = 256
W_NA = 256
W_SSD = 256
POOL_WINDOWS = (2, 4, 8, 16)
POOL_GROUP = W_POOL // len(POOL_WINDOWS)
NA_HEADS = 4
NA_HEAD_DIM = W_NA // NA_HEADS
WIN_R = 8
WIN_C = 16
COL_BLOCK = 16
COL_BAND = 32
SSD_HEAD_DIM = 64
SSD_HEADS = W_SSD // SSD_HEAD_DIM
SSD_GROUPS = 2
SSD_STATE = 128
SSD_CHUNK = 128
SSD_XBC = W_SSD + 2 * SSD_GROUPS * SSD_STATE
D_IN_PROJ = W_POOL + 3 * W_CONV + 3 * W_NA + W_SSD + SSD_XBC + 2 * SSD_HEADS
PROJ_SPLITS = (W_POOL, W_POOL + 3 * W_CONV, W_POOL + 3 * W_CONV + 3 * W_NA)
D_FF = 2816
N_EXPERTS = 8
TOP_K = 2
D_FF_EXPERT = 1408

LANES = 128
D_IN_PAD = -(-D_IN_PROJ // LANES) * LANES
VMEM_LIMIT = 56 * 1024 * 1024
BF16 = jnp.bfloat16
F32 = jnp.float32


def _params(*sem):
    return pltpu.CompilerParams(dimension_semantics=sem, vmem_limit_bytes=VMEM_LIMIT)


def _norm_mod(x, g, scale, shift):
    ms = jnp.mean(x * x, axis=-1, keepdims=True)
    return (x * lax.rsqrt(ms + EPS) * g) * (1.0 + scale) + shift


def _mod_map(tiles_per_batch):
    if tiles_per_batch is None:
        return lambda i, *_: (0, 0, 0)
    return lambda i, *_: (i // tiles_per_batch, 0, 0)


def _ada_kernel(c_ref, w_ref, b_ref, o_ref):
    c = c_ref[...]
    s = c * jax.nn.sigmoid(c)
    o_ref[...] = jnp.dot(s.astype(BF16), w_ref[...], preferred_element_type=F32) + b_ref[...]


def ada_modulation(c_rows, w, b):
    r, d = c_rows.shape
    n = w.shape[1]
    tn = 1536
    return pl.pallas_call(
        _ada_kernel,
        out_shape=jax.ShapeDtypeStruct((r, n), F32),
        grid=(n // tn,),
        in_specs=[pl.BlockSpec((r, d), lambda j: (0, 0)),
                  pl.BlockSpec((d, tn), lambda j: (0, j)),
                  pl.BlockSpec((1, tn), lambda j: (0, j))],
        out_specs=pl.BlockSpec((r, tn), lambda j: (0, j)),
        compiler_params=_params("arbitrary"),
        name="ada_modulation",
    )(c_rows, w, b)


def _in_proj_kernel(x_ref, g_ref, mod_ref, w_ref, o_ref):
    h = _norm_mod(x_ref[...], g_ref[...], mod_ref[0, 1:2, :], mod_ref[0, 0:1, :])
    o_ref[...] = jnp.dot(h.astype(BF16), w_ref[...], preferred_element_type=F32).astype(o_ref.dtype)


def in_proj(x2, g, mod, w, tiles_per_batch, tm, out_dtype=F32):
    m, d = x2.shape
    n = w.shape[1]
    return pl.pallas_call(
        _in_proj_kernel,
        out_shape=jax.ShapeDtypeStruct((m, n), out_dtype),
        grid=(m // tm,),
        in_specs=[pl.BlockSpec((tm, d), lambda i: (i, 0)),
                  pl.BlockSpec((1, d), lambda i: (0, 0)),
                  pl.BlockSpec((1, 6, d), _mod_map(tiles_per_batch)),
                  pl.BlockSpec((d, n), lambda i: (0, 0))],
        out_specs=pl.BlockSpec((tm, n), lambda i: (i, 0)),
        compiler_params=_params("parallel"),
        name="in_proj",
    )(x2, g, mod, w)


def _out_proj_kernel(x_ref, mod_ref, w_ref, *refs):
    part_refs, y_ref = refs[:-1], refs[-1]
    y = jnp.zeros(y_ref.shape, F32)
    k0 = 0
    for p_ref in part_refs:
        k = p_ref.shape[1]
        y = y + jnp.dot(p_ref[...].astype(BF16), w_ref[k0:k0 + k, :], preferred_element_type=F32)
        k0 += k
    y_ref[...] = x_ref[...] + mod_ref[0, 2:3, :] * y


def out_proj(x2, parts, mod, w, tiles_per_batch, tm):
    m, d = x2.shape
    return pl.pallas_call(
        _out_proj_kernel,
        out_shape=jax.ShapeDtypeStruct((m, d), F32),
        grid=(m // tm,),
        in_specs=[pl.BlockSpec((tm, d), lambda i: (i, 0)),
                  pl.BlockSpec((1, 6, d), _mod_map(tiles_per_batch)),
                  pl.BlockSpec(w.shape, lambda i: (0, 0))]
                 + [pl.BlockSpec((tm, p.shape[1]), lambda i: (i, 0)) for p in parts],
        out_specs=pl.BlockSpec((tm, d), lambda i: (i, 0)),
        compiler_params=_params("parallel"),
        name="out_proj",
    )(x2, mod, w, *parts)


def _ffn_kernel(x_ref, g_ref, mod_ref, wg_ref, wu_ref, wd_ref, y_ref, h_ref, acc_ref):
    f = pl.program_id(1)

    @pl.when(f == 0)
    def _():
        h = _norm_mod(x_ref[...], g_ref[...], mod_ref[0, 4:5, :], mod_ref[0, 3:4, :])
        h_ref[...] = h.astype(BF16)
        acc_ref[...] = jnp.zeros_like(acc_ref)

    h = h_ref[...]
    gg = jnp.dot(h, wg_ref[...], preferred_element_type=F32)
    uu = jnp.dot(h, wu_ref[...], preferred_element_type=F32)
    a = (gg * jax.nn.sigmoid(gg) * uu).astype(BF16)
    acc_ref[...] += jnp.dot(a, wd_ref[...], preferred_element_type=F32)

    @pl.when(f == pl.num_programs(1) - 1)
    def _():
        y_ref[...] = x_ref[...] + mod_ref[0, 5:6, :] * acc_ref[...]


def ffn_dense(x2, g, mod, w_gu, w_down, tiles_per_batch, tm, tf):
    m, d = x2.shape
    ff = w_down.shape[0]
    nf = ff // tf
    return pl.pallas_call(
        _ffn_kernel,
        out_shape=jax.ShapeDtypeStruct((m, d), F32),
        grid=(m // tm, nf),
        in_specs=[pl.BlockSpec((tm, d), lambda i, f: (i, 0)),
                  pl.BlockSpec((1, d), lambda i, f: (0, 0)),
                  pl.BlockSpec((1, 6, d), _mod_map(tiles_per_batch)),
                  pl.BlockSpec((d, tf), lambda i, f: (0, f)),
                  pl.BlockSpec((d, tf), lambda i, f: (0, nf + f)),
                  pl.BlockSpec((tf, d), lambda i, f: (f, 0))],
        out_specs=pl.BlockSpec((tm, d), lambda i, f: (i, 0)),
        scratch_shapes=[pltpu.VMEM((tm, d), BF16), pltpu.VMEM((tm, d), F32)],
        compiler_params=_params("parallel", "arbitrary"),
        name="ffn_dense",
    )(x2, g, mod, w_gu, w_gu, w_down)


ROUTE_E1, ROUTE_E2, ROUTE_R1, ROUTE_R2, ROUTE_G1, ROUTE_G2 = range(6)


def _router_kernel(x_ref, g_ref, mod_ref, wr_ref, h_ref, route_ref, cnt_ref, base_ref):
    @pl.when(pl.program_id(0) == 0)
    def _():
        base_ref[...] = jnp.zeros_like(base_ref)

    h = _norm_mod(x_ref[...], g_ref[...], mod_ref[0, 4:5, :], mod_ref[0, 3:4, :]).astype(BF16)
    h_ref[...] = h
    tm = h.shape[0]
    logits = jnp.dot(h, wr_ref[...], preferred_element_type=F32)
    lane = lax.broadcasted_iota(jnp.int32, logits.shape, 1)
    neg = jnp.float32(-jnp.inf)
    logits = jnp.where(lane < N_EXPERTS, logits, neg)
    m1 = jnp.max(logits, axis=-1, keepdims=True)
    i1 = jnp.min(jnp.where(logits == m1, lane, LANES), axis=-1, keepdims=True)
    rest = jnp.where(lane == i1, neg, logits)
    m2 = jnp.max(rest, axis=-1, keepdims=True)
    i2 = jnp.min(jnp.where(rest == m2, lane, LANES), axis=-1, keepdims=True)
    e2 = jnp.exp(m2 - m1)
    g1 = 1.0 / (1.0 + e2)
    g2 = e2 / (1.0 + e2)
    sel1, sel2 = lane == i1, lane == i2
    sel = jnp.where(sel1 | sel2, 1.0, 0.0)
    li = lax.broadcasted_iota(jnp.int32, (tm, tm), 0)
    si = lax.broadcasted_iota(jnp.int32, (tm, tm), 1)
    before = jnp.where(si < li, 1.0, 0.0).astype(BF16)
    rank = jnp.dot(before, sel.astype(BF16), preferred_element_type=F32) + base_ref[...]
    r1 = jnp.sum(jnp.where(sel1, rank, 0.0), axis=-1, keepdims=True)
    r2 = jnp.sum(jnp.where(sel2, rank, 0.0), axis=-1, keepdims=True)
    base_ref[...] += jnp.sum(sel, axis=0, keepdims=True)
    cnt_ref[...] = base_ref[...]
    fields = (i1.astype(F32), i2.astype(F32), r1, r2, g1, g2)
    route = jnp.zeros(logits.shape, F32)
    for k, v in enumerate(fields):
        route = jnp.where(lane == k, v, route)
    route_ref[...] = route


def moe_router(x2, g, mod, w_router, tiles_per_batch, tm):
    m, d = x2.shape
    return pl.pallas_call(
        _router_kernel,
        out_shape=(jax.ShapeDtypeStruct((m, d), BF16), jax.ShapeDtypeStruct((m, LANES), F32),
                   jax.ShapeDtypeStruct((1, LANES), F32)),
        grid=(m // tm,),
        in_specs=[pl.BlockSpec((tm, d), lambda i: (i, 0)),
                  pl.BlockSpec((1, d), lambda i: (0, 0)),
                  pl.BlockSpec((1, 6, d), _mod_map(tiles_per_batch)),
                  pl.BlockSpec((d, LANES), lambda i: (0, 0))],
        out_specs=(pl.BlockSpec((tm, d), lambda i: (i, 0)),
                   pl.BlockSpec((tm, LANES), lambda i: (i, 0)),
                   pl.BlockSpec((1, LANES), lambda i: (0, 0))),
        scratch_shapes=[pltpu.VMEM((1, LANES), F32)],
        compiler_params=_params("arbitrary"),
        name="moe_router",
    )(x2, g, mod, w_router)


def _moe_kernel(te_ref, nt_ref, xg_ref, wg_ref, wu_ref, wd_ref, y_ref):
    i = pl.program_id(0)

    @pl.when(i < nt_ref[0])
    def _():
        h = xg_ref[...]
        gg = jnp.dot(h, wg_ref[0], preferred_element_type=F32)
        uu = jnp.dot(h, wu_ref[0], preferred_element_type=F32)
        a = (gg * jax.nn.sigmoid(gg) * uu).astype(BF16)
        y_ref[...] = jnp.dot(a, wd_ref[0], preferred_element_type=F32).astype(y_ref.dtype)

    @pl.when(i >= nt_ref[0])
    def _():
        y_ref[...] = jnp.zeros_like(y_ref)


def moe_grouped(tile_expert, n_tiles_used, xg, w_gu, w_down, tm):
    p, d = xg.shape
    fe = w_down.shape[1]
    grid_spec = pltpu.PrefetchScalarGridSpec(
        num_scalar_prefetch=2,
        grid=(p // tm,),
        in_specs=[pl.BlockSpec((tm, d), lambda i, te, nt: (i, 0)),
                  pl.BlockSpec((1, d, fe), lambda i, te, nt: (te[i], 0, 0)),
                  pl.BlockSpec((1, d, fe), lambda i, te, nt: (te[i], 0, 1)),
                  pl.BlockSpec((1, fe, d), lambda i, te, nt: (te[i], 0, 0))],
        out_specs=pl.BlockSpec((tm, d), lambda i, te, nt: (i, 0)),
    )
    return pl.pallas_call(
        _moe_kernel,
        out_shape=jax.ShapeDtypeStruct((p, d), BF16),
        grid_spec=grid_spec,
        compiler_params=_params("arbitrary"),
        name="moe_grouped",
    )(tile_expert, n_tiles_used, xg, w_gu, w_gu, w_down)


def _moe_combine_kernel(x_ref, ya_ref, yb_ref, route_ref, mod_ref, *refs):
    o_ref = refs[-1]
    r = route_ref[...]
    y = (r[:, ROUTE_G1:ROUTE_G1 + 1] * ya_ref[...].astype(F32) + r[:, ROUTE_G2:ROUTE_G2 + 1] * yb_ref[...].astype(F32))
    x = x_ref[...] + mod_ref[0, 5:6, :] * y
    if len(refs) == 2:
        ms = jnp.mean(x * x, axis=-1, keepdims=True)
        x = x * lax.rsqrt(ms + EPS) * refs[0][...]
    o_ref[...] = x


def moe_combine(x2, ya, yb, route, mod, tiles_per_batch, tm, g_final=None):
    m, d = x2.shape
    row = pl.BlockSpec((tm, d), lambda i: (i, 0))
    specs = [row, row, row, pl.BlockSpec((tm, LANES), lambda i: (i, 0)),
             pl.BlockSpec((1, 6, d), _mod_map(tiles_per_batch))]
    args = [x2, ya, yb, route, mod]
    if g_final is not None:
        specs.append(pl.BlockSpec((1, d), lambda i: (0, 0)))
        args.append(g_final)
    return pl.pallas_call(
        _moe_combine_kernel,
        out_shape=jax.ShapeDtypeStruct((m, d), F32),
        grid=(m // tm,),
        in_specs=specs,
        out_specs=row,
        compiler_params=_params("parallel"),
        name="moe_combine",
    )(*args)


def moe_block(x2, g, mod, w_router, w_gu, w_down, tiles_per_batch, tm_tok, tm_grp, g_final=None):
    m, d = x2.shape
    h, route, cnt = moe_router(x2, g, mod, w_router, tiles_per_batch, tm_tok)
    cnt = cnt[0, :N_EXPERTS].astype(jnp.int32)
    gsize = ((cnt + tm_grp - 1) // tm_grp) * tm_grp
    gend = jnp.cumsum(gsize)
    gstart = gend - gsize
    p_max = m * TOP_K + N_EXPERTS * tm_grp
    n_tiles = p_max // tm_grp
    ri = route[:, :ROUTE_G1].astype(jnp.int32)
    onehot = lambda e: (e[:, None] == jnp.arange(N_EXPERTS)[None, :]).astype(jnp.int32)
    d1 = jnp.sum(onehot(ri[:, ROUTE_E1]) * gstart[None, :], axis=1) + ri[:, ROUTE_R1]
    d2 = jnp.sum(onehot(ri[:, ROUTE_E2]) * gstart[None, :], axis=1) + ri[:, ROUTE_R2]
    tok = jnp.arange(m, dtype=jnp.int32)
    src = jnp.zeros((p_max,), jnp.int32).at[jnp.concatenate([d1, d2])].set(jnp.concatenate([tok, tok]),
                                                                          unique_indices=True)
    tile_start = jnp.arange(n_tiles, dtype=jnp.int32) * tm_grp
    tile_expert = jnp.minimum(jnp.sum((tile_start[:, None] >= gend[None, :]).astype(jnp.int32), axis=1),
                              N_EXPERTS - 1).astype(jnp.int32)
    n_used = (gend[-1] // tm_grp).astype(jnp.int32).reshape(1)
    xg = jnp.take(h, src, axis=0)
    yg = moe_grouped(tile_expert, n_used, xg, w_gu, w_down, tm_grp)
    return moe_combine(x2, jnp.take(yg, d1, axis=0), jnp.take(yg, d2, axis=0), route, mod, tiles_per_batch,
                       tm_tok, g_final)


def _final_norm_kernel(x_ref, g_ref, o_ref):
    x = x_ref[...]
    ms = jnp.mean(x * x, axis=-1, keepdims=True)
    o_ref[...] = x * lax.rsqrt(ms + EPS) * g_ref[...]


def final_norm(x2, g, tm):
    m, d = x2.shape
    return pl.pallas_call(
        _final_norm_kernel,
        out_shape=jax.ShapeDtypeStruct((m, d), F32),
        grid=(m // tm,),
        in_specs=[pl.BlockSpec((tm, d), lambda i: (i, 0)), pl.BlockSpec((1, d), lambda i: (0, 0))],
        out_specs=pl.BlockSpec((tm, d), lambda i: (i, 0)),
        compiler_params=_params("parallel"),
        name="final_norm",
    )(x2, g)


NA_QROWS = 4
NA_KROWS = NA_QROWS + WIN_R
Q_COL, K_COL, V_COL = 4, 5, 6


def _na_key_start(j, rows):
    return np.clip(j * NA_QROWS - WIN_R // 2, 0, rows - NA_KROWS)


def _na_bias_index(rows):
    nblk = rows // NA_QROWS
    pats = []
    for j in range(nblk):
        start = _na_key_start(j, rows)
        r = j * NA_QROWS + np.arange(NA_QROWS)
        sr = np.clip(r - WIN_R // 2, 0, rows - WIN_R)
        kr = start + np.arange(NA_KROWS)
        rvalid = (kr[None, :] >= sr[:, None]) & (kr[None, :] < sr[:, None] + WIN_R)
        ri = np.clip(kr[None, :] - r[:, None] + WIN_R - 1, 0, 2 * WIN_R - 2)
        pats.append((ri, rvalid))
    for j in range(2, nblk - 1):
        assert all(np.array_equal(a, b) for a, b in zip(pats[1], pats[j]))
    sel = [pats[0], pats[1], pats[nblk - 1]]
    ri = np.stack([p[0] for p in sel])
    rvalid = np.stack([p[1] for p in sel])
    c = np.arange(GRID_W)
    sc = np.clip(c - WIN_C // 2, 0, GRID_W - WIN_C)
    cvalid = (c[None, :] >= sc[:, None]) & (c[None, :] < sc[:, None] + WIN_C)
    ci = np.clip(c[None, :] - c[:, None] + WIN_C - 1, 0, 2 * WIN_C - 2)
    r_onehot = (ri[..., None] == np.arange(2 * WIN_R - 1)).astype(np.float32)
    c_onehot = (ci[..., None] == np.arange(2 * WIN_C - 1)).astype(np.float32)
    valid = rvalid[:, :, None, :, None] & cvalid[None, None, :, None, :]
    return r_onehot, c_onehot, valid


def na_bias_table(rpb, rows):
    r_onehot, c_onehot, valid = _na_bias_index(rows)
    exact = lax.Precision.HIGHEST
    toep = jnp.einsum('hrd,qkd->hrqk', rpb, c_onehot, precision=exact)
    b = jnp.einsum('pirR,hRqk->phiqrk', r_onehot, toep, precision=exact)
    b = jnp.where(valid[:, None], b, -jnp.inf)
    return b.reshape(3, NA_HEADS, NA_QROWS * GRID_W, NA_KROWS * GRID_W).astype(F32)


def _attend_heads(q, key_sets, bias_fn):
    n, w = q.shape
    lane = lax.broadcasted_iota(jnp.int32, (1, w), 1)
    out = jnp.zeros((n, w), F32)
    for h in range(NA_HEADS):
        mh = (lane >= h * NA_HEAD_DIM) & (lane < (h + 1) * NA_HEAD_DIM)
        qh = jnp.where(mh, q, 0.0).astype(BF16)
        scores = []
        for i, (k, _) in enumerate(key_sets):
            s = lax.dot_general(qh, k, (((1,), (1,)), ((), ())), preferred_element_type=F32)
            b = bias_fn(i, h)
            scores.append(s if b is None else s + b)
        m = scores[0].max(axis=-1, keepdims=True)
        for s in scores[1:]:
            m = jnp.maximum(m, s.max(axis=-1, keepdims=True))
        denom = jnp.zeros((n, 1), F32)
        acc = jnp.zeros((n, w), F32)
        for s, (_, v) in zip(scores, key_sets):
            p = jnp.exp(s - m)
            denom = denom + p.sum(axis=-1, keepdims=True)
            acc = acc + jnp.dot(p.astype(BF16), v, preferred_element_type=F32)
        out = out + jnp.where(mh, acc / denom, 0.0)
    return out


def _na_kernel(q_ref, k_ref, v_ref, kc_ref, vc_ref, bias_ref, o_ref, *, rows):
    j = pl.program_id(1)
    start = jnp.clip(j * NA_QROWS - WIN_R // 2, 0, rows - NA_KROWS)
    t0 = pl.multiple_of(start * GRID_W, GRID_W)
    nk = NA_KROWS * GRID_W
    kw = k_ref[0, pl.ds(t0, nk), :].astype(BF16)
    vw = v_ref[0, pl.ds(t0, nk), :].astype(BF16)
    kc = kc_ref[0].astype(BF16)
    vc = vc_ref[0].astype(BF16)
    q = q_ref[0] * (1.0 / math.sqrt(NA_HEAD_DIM))
    o_ref[0] = _attend_heads(q, [(kw, vw), (kc, vc)], lambda i, h: bias_ref[0, h] if i == 0 else None)


def na_attention(pr3, prc3, bias):
    bn, S, _ = pr3.shape
    Lc = prc3.shape[1]
    rows = S // GRID_W
    nblk = rows // NA_QROWS
    nq = NA_QROWS * GRID_W

    def pat(b, j):
        return (jnp.where(j == 0, 0, jnp.where(j == nblk - 1, 2, 1)), 0, 0, 0)

    return pl.pallas_call(
        functools.partial(_na_kernel, rows=rows),
        out_shape=jax.ShapeDtypeStruct((bn, S, W_NA), F32),
        grid=(bn, nblk),
        in_specs=[pl.BlockSpec((1, nq, W_NA), lambda b, j: (b, j, Q_COL)),
                  pl.BlockSpec((1, S, W_NA), lambda b, j: (b, 0, K_COL)),
                  pl.BlockSpec((1, S, W_NA), lambda b, j: (b, 0, V_COL)),
                  pl.BlockSpec((1, Lc, W_NA), lambda b, j: (b, 0, K_COL)),
                  pl.BlockSpec((1, Lc, W_NA), lambda b, j: (b, 0, V_COL)),
                  pl.BlockSpec((1,) + bias.shape[1:], pat)],
        out_specs=pl.BlockSpec((1, nq, W_NA), lambda b, j: (b, j, 0)),
        compiler_params=_params("parallel", "arbitrary"),
        name="na_attention",
    )(pr3, pr3, pr3, prc3, prc3, bias)


def _ctx_attn_kernel(q_ref, k_ref, v_ref, o_ref):
    q = q_ref[0] * (1.0 / math.sqrt(NA_HEAD_DIM))
    o_ref[0] = _attend_heads(q, [(k_ref[0].astype(BF16), v_ref[0].astype(BF16))], lambda i, h: None)


def ctx_attention(prc3):
    bn, Lc, _ = prc3.shape
    return pl.pallas_call(
        _ctx_attn_kernel,
        out_shape=jax.ShapeDtypeStruct((bn, Lc, W_NA), F32),
        grid=(bn,),
        in_specs=[pl.BlockSpec((1, Lc, W_NA), lambda b: (b, 0, Q_COL)),
                  pl.BlockSpec((1, Lc, W_NA), lambda b: (b, 0, K_COL)),
                  pl.BlockSpec((1, Lc, W_NA), lambda b: (b, 0, V_COL))],
        out_specs=pl.BlockSpec((1, Lc, W_NA), lambda b: (b, 0, 0)),
        compiler_params=_params("parallel"),
        name="ctx_attention",
    )(prc3, prc3, prc3)


POOL_COL, CONV_H_COL, CONV_B_COL, CONV_C_COL = 0, 1, 2, 3


def _shift_rows(x, k, row):
    n = x.shape[0]
    y = pltpu.roll(x, k % n, 0)
    return jnp.where(row < k, 0.0, y) if k > 0 else jnp.where(row >= n + k, 0.0, y)


def _local_mix_kernel(u_ref, h_ref, bg_ref, cg_ref, pw_ref, ps_ref, cw_ref, op_ref, oc_ref, *, seq):
    row = lax.broadcasted_iota(jnp.int32, (seq, 1), 0)
    lane = lax.broadcasted_iota(jnp.int32, (1, W_POOL), 1)
    u = u_ref[0]
    trailing, leading = {1: u}, {1: u}
    for w in (1, 2, 4):
        trailing[2 * w] = trailing[w] + _shift_rows(trailing[w], w, row)
        leading[2 * w] = leading[w] + _shift_rows(leading[w], -w, row)
    rowf = row.astype(F32)
    win_sum = jnp.zeros_like(u)
    cnt = jnp.zeros_like(u)
    for g, win in enumerate(POOL_WINDOWS):
        half = win // 2
        mg = (lane >= g * POOL_GROUP) & (lane < (g + 1) * POOL_GROUP)
        s = _shift_rows(trailing[half], 1, row) + leading[half]
        n = jnp.minimum(rowf + half, float(seq)) - jnp.maximum(rowf - half, 0.0)
        win_sum = jnp.where(mg, s, win_sum)
        cnt = jnp.where(mg, n, cnt)
    p = win_sum / cnt - u
    op_ref[0] = jnp.dot(p.astype(BF16), pw_ref[...], preferred_element_type=F32) * ps_ref[...]
    v = cg_ref[0] * h_ref[0]
    conv = (_shift_rows(v, 1, row) * cw_ref[0:1, :] + v * cw_ref[1:2, :] + _shift_rows(v, -1, row) * cw_ref[2:3, :])
    oc_ref[0] = bg_ref[0] * conv


def local_mix(pr3, pool_w, pool_scale, conv_w):
    bn, seq, _ = pr3.shape
    pw = jax.scipy.linalg.block_diag(*[pool_w[g] for g in range(len(POOL_WINDOWS))]).astype(BF16)
    blk = lambda col: pl.BlockSpec((1, seq, W_POOL), lambda b: (b, 0, col))
    full = lambda a: pl.BlockSpec(a.shape, lambda b: (0,) * a.ndim)
    args = (pw, pool_scale[None, :], conv_w)
    return pl.pallas_call(
        functools.partial(_local_mix_kernel, seq=seq),
        out_shape=(jax.ShapeDtypeStruct((bn, seq, W_POOL), F32), jax.ShapeDtypeStruct((bn, seq, W_CONV), F32)),
        grid=(bn,),
        in_specs=[blk(POOL_COL), blk(CONV_H_COL), blk(CONV_B_COL), blk(CONV_C_COL)] + [full(a) for a in args],
        out_specs=(pl.BlockSpec((1, seq, W_POOL), lambda b: (b, 0, 0)),
                   pl.BlockSpec((1, seq, W_CONV), lambda b: (b, 0, 0))),
        compiler_params=_params("parallel"),
        name="local_mix",
    )(pr3, pr3, pr3, pr3, *args)


Z_COL, XS_COL, BS_COL, CS_COL = 7, 8, 9, 10
DT_COL = 22
HEADS_PER_GROUP = SSD_HEADS // SSD_GROUPS


def _expand_heads(v, d):
    lane = lax.broadcasted_iota(jnp.int32, (1, W_SSD), 1)
    out = jnp.zeros((v.shape[0], W_SSD), F32)
    for h in range(SSD_HEADS):
        k = d * SSD_HEADS + h
        mh = (lane >= h * SSD_HEAD_DIM) & (lane < (h + 1) * SSD_HEAD_DIM)
        out = jnp.where(mh, v[:, k:k + 1], out)
    return out


def _ssd_chunk(c, d, act_ref, dt_ref, la_ref, y_ref, st_ref, need_y):
    T = SSD_CHUNK
    r0 = pl.multiple_of(c * T, T)
    xs = act_ref[pl.ds(r0, T), 0:W_SSD]
    bm = act_ref[pl.ds(r0, T), W_SSD:2 * W_SSD]
    cm = act_ref[pl.ds(r0, T), 2 * W_SSD:3 * W_SSD]
    dt = dt_ref[pl.ds(r0, T), :]
    la = la_ref[pl.ds(r0, T), :]
    li = lax.broadcasted_iota(jnp.int32, (T, T), 0)
    si = lax.broadcasted_iota(jnp.int32, (T, T), 1)
    mask = (si <= li) if d == 0 else (si >= li)
    acum = jnp.dot(mask.astype(F32), la, precision=lax.Precision.HIGHEST, preferred_element_type=F32)
    tot = acum[T - 1:T, :] if d == 0 else acum[0:1, :]
    acum_e = _expand_heads(acum, d)
    tot_e = _expand_heads(tot, d)
    xdt = xs * _expand_heads(dt, d)
    xw = (xdt * jnp.exp(tot_e - acum_e)).astype(BF16)
    st = st_ref[d]
    bt = bm.T.astype(BF16)
    if need_y:
        acum_t = acum.T
        xdt_b = xdt.astype(BF16)
        lane = lax.broadcasted_iota(jnp.int32, (1, W_SSD), 1)
        y = jnp.zeros((T, W_SSD), F32)
        y_state = []
    new_st = []
    gw = HEADS_PER_GROUP * SSD_HEAD_DIM
    for g in range(SSD_GROUPS):
        bg = bm[:, g * SSD_STATE:(g + 1) * SSD_STATE].astype(BF16)
        cg = cm[:, g * SSD_STATE:(g + 1) * SSD_STATE].astype(BF16)
        st_g = st[:, g * gw:(g + 1) * gw]
        if need_y:
            cb = lax.dot_general(cg, bg, (((1,), (1,)), ((), ())), preferred_element_type=F32)
            for h in range(g * HEADS_PER_GROUP, (g + 1) * HEADS_PER_GROUP):
                k = d * SSD_HEADS + h
                decay = jnp.exp(jnp.where(mask, acum[:, k:k + 1] - acum_t[k:k + 1, :], -jnp.inf))
                mh = (lane >= h * SSD_HEAD_DIM) & (lane < (h + 1) * SSD_HEAD_DIM)
                y = y + jnp.where(mh, jnp.dot((cb * decay).astype(BF16), xdt_b, preferred_element_type=F32), 0.0)
            y_state.append(jnp.dot(cg, st_g.astype(BF16), preferred_element_type=F32))
        upd = jnp.dot(bt[g * SSD_STATE:(g + 1) * SSD_STATE, :], xw[:, g * gw:(g + 1) * gw],
                      preferred_element_type=F32)
        new_st.append(jnp.exp(tot_e[:, g * gw:(g + 1) * gw]) * st_g + upd)
    st_ref[d] = jnp.concatenate(new_st, axis=1)
    if need_y:
        y_ref[pl.ds(r0, T), :] += y + jnp.concatenate(y_state, axis=1) * jnp.exp(acum_e)


def _ssd_kernel(z_ref, xs_ref, bs_ref, cs_ref, dtr_ref, cw_ref, cb_ref, dtb_ref, alog_ref, dsk_ref, ng_ref, h0_ref,
                *refs, seq, need_y):
    if need_y:
        o_ref, hT_ref, act_ref, dt_ref, la_ref, st_ref, y_ref = refs
    else:
        hT_ref, act_ref, dt_ref, la_ref, st_ref = refs
        y_ref = None
    nc = seq // SSD_CHUNK
    row = lax.broadcasted_iota(jnp.int32, (seq, 1), 0)
    for gi, ref in enumerate((xs_ref, bs_ref, cs_ref)):
        sl = slice(gi * W_SSD, (gi + 1) * W_SSD)
        x = ref[0]
        cv = (_shift_rows(x, 1, row) * cw_ref[0:1, sl] + x * cw_ref[1:2, sl]
              + _shift_rows(x, -1, row) * cw_ref[2:3, sl] + cb_ref[:, sl])
        act_ref[:, sl] = cv * jax.nn.sigmoid(cv)
    dtv = dtr_ref[0] + dtb_ref[...]
    dt = jnp.maximum(dtv, 0.0) + jnp.log1p(jnp.exp(-jnp.abs(dtv)))
    dt_ref[...] = dt
    la_ref[...] = dt * (-jnp.exp(alog_ref[...]))
    st_ref[...] = h0_ref[0]
    if need_y:
        y_ref[...] = act_ref[:, 0:W_SSD] * dsk_ref[...]

    def body(i, carry):
        _ssd_chunk(i, 0, act_ref, dt_ref, la_ref, y_ref, st_ref, need_y)
        _ssd_chunk(nc - 1 - i, 1, act_ref, dt_ref, la_ref, y_ref, st_ref, need_y)
        return carry

    lax.fori_loop(0, nc, body, 0)
    hT_ref[0] = st_ref[...]
    if need_y:
        z = z_ref[0]
        yz = y_ref[...] * (z * jax.nn.sigmoid(z))
        ms = jnp.mean(yz * yz, axis=-1, keepdims=True)
        o_ref[0] = yz * lax.rsqrt(ms + EPS) * ng_ref[...]


def ssd_scan(pr3, h0, conv_w, conv_b, dt_bias, a_log, d_skip, norm_g, need_y):
    bn, seq, _ = pr3.shape
    pad = LANES - 2 * SSD_HEADS
    dtb = jnp.pad(dt_bias.reshape(1, -1), ((0, 0), (0, pad)))
    alog = jnp.pad(a_log.reshape(1, -1), ((0, 0), (0, pad)))
    dsk = jnp.repeat(d_skip, SSD_HEAD_DIM)[None, :]
    blk = lambda col, w=W_SSD: pl.BlockSpec((1, seq, w), lambda b: (b, 0, col))
    full = lambda a: pl.BlockSpec(a.shape, lambda b: (0,) * a.ndim)
    st_shape = (1, 2, SSD_STATE, W_SSD)
    st_spec = pl.BlockSpec(st_shape, lambda b: (b, 0, 0, 0))
    args = (conv_w, conv_b[None, :], dtb, alog, dsk, norm_g[None, :])
    out_shape = [jax.ShapeDtypeStruct((bn,) + st_shape[1:], F32)]
    out_specs = [st_spec]
    scratch = [pltpu.VMEM((seq, 3 * W_SSD), F32), pltpu.VMEM((seq, LANES), F32), pltpu.VMEM((seq, LANES), F32),
               pltpu.VMEM(st_shape[1:], F32)]
    if need_y:
        out_shape.insert(0, jax.ShapeDtypeStruct((bn, seq, W_SSD), F32))
        out_specs.insert(0, pl.BlockSpec((1, seq, W_SSD), lambda b: (b, 0, 0)))
        scratch.append(pltpu.VMEM((seq, W_SSD), F32))
    res = pl.pallas_call(
        functools.partial(_ssd_kernel, seq=seq, need_y=need_y),
        out_shape=out_shape,
        grid=(bn,),
        in_specs=[blk(Z_COL), blk(XS_COL), blk(BS_COL), blk(CS_COL), blk(DT_COL, LANES)]
                 + [full(a) for a in args] + [st_spec],
        out_specs=out_specs,
        scratch_shapes=scratch,
        compiler_params=_params("parallel"),
        name="ssd_scan",
    )(pr3, pr3, pr3, pr3, pr3, *args, h0)
    return (res[0], res[1]) if need_y else (None, res[0])


def ssd_mix(pr3, prc3, conv_w, conv_b, dt_bias, a_log, d_skip, norm_g, ctx_out):
    bn = pr3.shape[0]
    h0 = jnp.zeros((bn, 2, SSD_STATE, W_SSD), F32)
    oc, hc = ssd_scan(prc3, h0, conv_w, conv_b, dt_bias, a_log, d_skip, norm_g, ctx_out)
    o, _ = ssd_scan(pr3, hc, conv_w, conv_b, dt_bias, a_log, d_skip, norm_g, True)
    return o, oc


def mixer(pr3, prc3, pool_w, pool_scale, conv_w, rpb, s_conv_w, s_conv_b, dt_bias, a_log, d_skip, s_norm_g,
          ctx_out):
    bn, S, _ = pr3.shape
    o_ssd, oc_ssd = ssd_mix(pr3, prc3, s_conv_w, s_conv_b, dt_bias, a_log, d_skip, s_norm_g, ctx_out)
    o = [*local_mix(pr3, pool_w, pool_scale, conv_w),
         na_attention(pr3, prc3, na_bias_table(rpb, S // GRID_W)), o_ssd]
    if not ctx_out:
        return o, None
    oc = [*local_mix(prc3, pool_w, pool_scale, conv_w), ctx_attention(prc3), oc_ssd]
    return o, oc


TM = 512
TM_CTX = 256
TF = 1408
TM_GRP = 512


def kernel(x, c, ctx, c_ctx, w_ada, b_ada, g_mix, g_ffn, w_in, w_out, pool_w, pool_scale, conv_w, na_rpb,
           ssd_conv_w, ssd_conv_b, ssd_dt_bias, ssd_a_log, ssd_d, ssd_norm_g, ffn_w_gu, ffn_w_down,
           moe_router, moe_w_gu, moe_w_down, g_final):
    bn, S, d = x.shape
    Lc = ctx.shape[1]
    m, mc = bn * S, bn * Lc
    tpb = S // TM
    x2 = x.reshape(m, d)
    xc2 = ctx.reshape(mc, d)
    c_rows = jnp.concatenate([c, c_ctx[None, :], jnp.zeros((7, d), F32)], axis=0)
    for l in range(DEPTH):
        ctx_out = l < DEPTH - 1
        last = l == DEPTH - 1
        ada = ada_modulation(c_rows, w_ada[l].astype(BF16), b_ada[l][None, :])
        mod = ada[:bn].reshape(bn, 6, d)
        mod_c = ada[bn:bn + 1].reshape(1, 6, d)
        w_in_l = jnp.pad(w_in[l], ((0, 0), (0, D_IN_PAD - D_IN_PROJ))).astype(BF16)
        g_m = g_mix[l][None, :]
        g_f = g_ffn[l][None, :]
        pr = in_proj(x2, g_m, mod, w_in_l, tpb, TM)
        pr_c = in_proj(xc2, g_m, mod_c, w_in_l, None, TM_CTX)
        o, oc = mixer(pr.reshape(bn, S, -1), pr_c.reshape(bn, Lc, -1), pool_w[l], pool_scale[l], conv_w[l],
                      na_rpb[l], ssd_conv_w[l], ssd_conv_b[l], ssd_dt_bias[l], ssd_a_log[l], ssd_d[l],
                      ssd_norm_g[l], ctx_out)
        w_out_l = w_out[l].astype(BF16)
        x2 = out_proj(x2, [p.reshape(m, -1) for p in o], mod, w_out_l, tpb, TM)
        if ctx_out:
            xc2 = out_proj(xc2, [p.reshape(mc, -1) for p in oc], mod_c, w_out_l, None, TM_CTX)
        j = l // 2
        if l % 2 == 0:
            w_gu = ffn_w_gu[j].astype(BF16)
            w_dn = ffn_w_down[j].astype(BF16)
            x2 = ffn_dense(x2, g_f, mod, w_gu, w_dn, tpb, TM, TF)
            if ctx_out:
                xc2 = ffn_dense(xc2, g_f, mod_c, w_gu, w_dn, None, TM_CTX, TF)
        else:
            w_r = jnp.pad(moe_router[j], ((0, 0), (0, LANES - N_EXPERTS))).astype(BF16)
            w_gu = moe_w_gu[j].astype(BF16)
            w_dn = moe_w_down[j].astype(BF16)
            x2 = moe_block(x2, g_f, mod, w_r, w_gu, w_dn, tpb, TM, TM_GRP, g_final[None, :] if last else None)
            if ctx_out:
                xc2 = moe_block(xc2, g_f, mod_c, w_r, w_gu, w_dn, None, TM_CTX, TM_GRP)
            if last:
                return x2.reshape(bn, S, d)
    return final_norm(x2, g_final[None, :], TM).reshape(bn, S, d)
```

```python
import functools
import math

import numpy as np
import jax
import jax.numpy as jnp
from jax import lax
from jax.experimental import pallas as pl
from jax.experimental.pallas import tpu as pltpu

D_MODEL = 1024
DEPTH = 2
GRID_W = 64
EPS = 1e-6
W_POOL = 256
W_CONV = 256
W_NA = 256
W_SSD = 256
POOL_WINDOWS = (2, 4, 8, 16)
POOL_GROUP = W_POOL // len(POOL_WINDOWS)
NA_HEADS = 4
NA_HEAD_DIM = W_NA // NA_HEADS
WIN_R = 8
WIN_C = 16
COL_BLOCK = 16
COL_BAND = 32
SSD_HEAD_DIM = 64
SSD_HEADS = W_SSD // SSD_HEAD_DIM
SSD_GROUPS = 2
SSD_STATE = 128
SSD_CHUNK = 128
SSD_XBC = W_SSD + 2 * SSD_GROUPS * SSD_STATE
D_IN_PROJ = W_POOL + 3 * W_CONV + 3 * W_NA + W_SSD + SSD_XBC + 2 * SSD_HEADS
PROJ_SPLITS = (W_POOL, W_POOL + 3 * W_CONV, W_POOL + 3 * W_CONV + 3 * W_NA)
D_FF = 2816
N_EXPERTS = 8
TOP_K = 2
D_FF_EXPERT = 1408

LANES = 128
D_IN_PAD = -(-D_IN_PROJ // LANES) * LANES
VMEM_LIMIT = 56 * 1024 * 1024
BF16 = jnp.bfloat16
F32 = jnp.float32


def _params(*sem):
    return pltpu.CompilerParams(dimension_semantics=sem, vmem_limit_bytes=VMEM_LIMIT)


def _norm_mod(x, g, scale, shift):
    ms = jnp.mean(x * x, axis=-1, keepdims=True)
    return (x * lax.rsqrt(ms + EPS) * g) * (1.0 + scale) + shift


def _mod_map(tiles_per_batch):
    if tiles_per_batch is None:
        return lambda i, *_: (0, 0, 0)
    return lambda i, *_: (i // tiles_per_batch, 0, 0)


def _ada_kernel(c_ref, w_ref, b_ref, o_ref):
    c = c_ref[...]
    s = c * jax.nn.sigmoid(c)
    o_ref[...] = jnp.dot(s.astype(BF16), w_ref[...], preferred_element_type=F32) + b_ref[...]


def ada_modulation(c_rows, w, b):
    r, d = c_rows.shape
    n = w.shape[1]
    tn = 1536
    return pl.pallas_call(
        _ada_kernel,
        out_shape=jax.ShapeDtypeStruct((r, n), F32),
        grid=(n // tn,),
        in_specs=[pl.BlockSpec((r, d), lambda j: (0, 0)),
                  pl.BlockSpec((d, tn), lambda j: (0, j)),
                  pl.BlockSpec((1, tn), lambda j: (0, j))],
        out_specs=pl.BlockSpec((r, tn), lambda j: (0, j)),
        compiler_params=_params("arbitrary"),
        name="ada_modulation",
    )(c_rows, w, b)


def _in_proj_kernel(x_ref, g_ref, mod_ref, w_ref, o_ref):
    h = _norm_mod(x_ref[...], g_ref[...], mod_ref[0, 1:2, :], mod_ref[0, 0:1, :])
    o_ref[...] = jnp.dot(h.astype(BF16), w_ref[...], preferred_element_type=F32).astype(o_ref.dtype)


def in_proj(x2, g, mod, w, tiles_per_batch, tm, out_dtype=F32):
    m, d = x2.shape
    n = w.shape[1]
    return pl.pallas_call(
        _in_proj_kernel,
        out_shape=jax.ShapeDtypeStruct((m, n), out_dtype),
        grid=(m // tm,),
        in_specs=[pl.BlockSpec((tm, d), lambda i: (i, 0)),
                  pl.BlockSpec((1, d), lambda i: (0, 0)),
                  pl.BlockSpec((1, 6, d), _mod_map(tiles_per_batch)),
                  pl.BlockSpec((d, n), lambda i: (0, 0))],
        out_specs=pl.BlockSpec((tm, n), lambda i: (i, 0)),
        compiler_params=_params("parallel"),
        name="in_proj",
    )(x2, g, mod, w)


def _out_proj_kernel(x_ref, mod_ref, w_ref, *refs):
    part_refs, y_ref = refs[:-1], refs[-1]
    y = jnp.zeros(y_ref.shape, F32)
    k0 = 0
    for p_ref in part_refs:
        k = p_ref.shape[1]
        y = y + jnp.dot(p_ref[...].astype(BF16), w_ref[k0:k0 + k, :], preferred_element_type=F32)
        k0 += k
    y_ref[...] = x_ref[...] + mod_ref[0, 2:3, :] * y


def out_proj(x2, parts, mod, w, tiles_per_batch, tm):
    m, d = x2.shape
    return pl.pallas_call(
        _out_proj_kernel,
        out_shape=jax.ShapeDtypeStruct((m, d), F32),
        grid=(m // tm,),
        in_specs=[pl.BlockSpec((tm, d), lambda i: (i, 0)),
                  pl.BlockSpec((1, 6, d), _mod_map(tiles_per_batch)),
                  pl.BlockSpec(w.shape, lambda i: (0, 0))]
                 + [pl.BlockSpec((tm, p.shape[1]), lambda i: (i, 0)) for p in parts],
        out_specs=pl.BlockSpec((tm, d), lambda i: (i, 0)),
        compiler_params=_params("parallel"),
        name="out_proj",
    )(x2, mod, w, *parts)


def _ffn_kernel(x_ref, g_ref, mod_ref, wg_ref, wu_ref, wd_ref, y_ref, h_ref, acc_ref):
    f = pl.program_id(1)

    @pl.when(f == 0)
    def _():
        h = _norm_mod(x_ref[...], g_ref[...], mod_ref[0, 4:5, :], mod_ref[0, 3:4, :])
        h_ref[...] = h.astype(BF16)
        acc_ref[...] = jnp.zeros_like(acc_ref)

    h = h_ref[...]
    gg = jnp.dot(h, wg_ref[...], preferred_element_type=F32)
    uu = jnp.dot(h, wu_ref[...], preferred_element_type=F32)
    a = (gg * jax.nn.sigmoid(gg) * uu).astype(BF16)
    acc_ref[...] += jnp.dot(a, wd_ref[...], preferred_element_type=F32)

    @pl.when(f == pl.num_programs(1) - 1)
    def _():
        y_ref[...] = x_ref[...] + mod_ref[0, 5:6, :] * acc_ref[...]


def ffn_dense(x2, g, mod, w_gu, w_down, tiles_per_batch, tm, tf):
    m, d = x2.shape
    ff = w_down.shape[0]
    nf = ff // tf
    return pl.pallas_call(
        _ffn_kernel,
        out_shape=jax.ShapeDtypeStruct((m, d), F32),
        grid=(m // tm, nf),
        in_specs=[pl.BlockSpec((tm, d), lambda i, f: (i, 0)),
                  pl.BlockSpec((1, d), lambda i, f: (0, 0)),
                  pl.BlockSpec((1, 6, d), _mod_map(tiles_per_batch)),
                  pl.BlockSpec((d, tf), lambda i, f: (0, f)),
                  pl.BlockSpec((d, tf), lambda i, f: (0, nf + f)),
                  pl.BlockSpec((tf, d), lambda i, f: (f, 0))],
        out_specs=pl.BlockSpec((tm, d), lambda i, f: (i, 0)),
        scratch_shapes=[pltpu.VMEM((tm, d), BF16), pltpu.VMEM((tm, d), F32)],
        compiler_params=_params("parallel", "arbitrary"),
        name="ffn_dense",
    )(x2, g, mod, w_gu, w_gu, w_down)


ROUTE_E1, ROUTE_E2, ROUTE_R1, ROUTE_R2, ROUTE_G1, ROUTE_G2 = range(6)


def _router_kernel(x_ref, g_ref, mod_ref, wr_ref, h_ref, route_ref, cnt_ref, base_ref):
    @pl.when(pl.program_id(0) == 0)
    def _():
        base_ref[...] = jnp.zeros_like(base_ref)

    h = _norm_mod(x_ref[...], g_ref[...], mod_ref[0, 4:5, :], mod_ref[0, 3:4, :]).astype(BF16)
    h_ref[...] = h
    tm = h.shape[0]
    logits = jnp.dot(h, wr_ref[...], preferred_element_type=F32)
    lane = lax.broadcasted_iota(jnp.int32, logits.shape, 1)
    neg = jnp.float32(-jnp.inf)
    logits = jnp.where(lane < N_EXPERTS, logits, neg)
    m1 = jnp.max(logits, axis=-1, keepdims=True)
    i1 = jnp.min(jnp.where(logits == m1, lane, LANES), axis=-1, keepdims=True)
    rest = jnp.where(lane == i1, neg, logits)
    m2 = jnp.max(rest, axis=-1, keepdims=True)
    i2 = jnp.min(jnp.where(rest == m2, lane, LANES), axis=-1, keepdims=True)
    e2 = jnp.exp(m2 - m1)
    g1 = 1.0 / (1.0 + e2)
    g2 = e2 / (1.0 + e2)
    sel1, sel2 = lane == i1, lane == i2
    sel = jnp.where(sel1 | sel2, 1.0, 0.0)
    li = lax.broadcasted_iota(jnp.int32, (tm, tm), 0)
    si = lax.broadcasted_iota(jnp.int32, (tm, tm), 1)
    before = jnp.where(si < li, 1.0, 0.0).astype(BF16)
    rank = jnp.dot(before, sel.astype(BF16), preferred_element_type=F32) + base_ref[...]
    r1 = jnp.sum(jnp.where(sel1, rank, 0.0), axis=-1, keepdims=True)
    r2 = jnp.sum(jnp.where(sel2, rank, 0.0), axis=-1, keepdims=True)
    base_ref[...] += jnp.sum(sel, axis=0, keepdims=True)
    cnt_ref[...] = base_ref[...]
    fields = (i1.astype(F32), i2.astype(F32), r1, r2, g1, g2)
    route = jnp.zeros(logits.shape, F32)
    for k, v in enumerate(fields):
        route = jnp.where(lane == k, v, route)
    route_ref[...] = route


def moe_router(x2, g, mod, w_router, tiles_per_batch, tm):
    m, d = x2.shape
    return pl.pallas_call(
        _router_kernel,
        out_shape=(jax.ShapeDtypeStruct((m, d), BF16), jax.ShapeDtypeStruct((m, LANES), F32),
                   jax.ShapeDtypeStruct((1, LANES), F32)),
        grid=(m // tm,),
        in_specs=[pl.BlockSpec((tm, d), lambda i: (i, 0)),
                  pl.BlockSpec((1, d), lambda i: (0, 0)),
                  pl.BlockSpec((1, 6, d), _mod_map(tiles_per_batch)),
                  pl.BlockSpec((d, LANES), lambda i: (0, 0))],
        out_specs=(pl.BlockSpec((tm, d), lambda i: (i, 0)),
                   pl.BlockSpec((tm, LANES), lambda i: (i, 0)),
                   pl.BlockSpec((1, LANES), lambda i: (0, 0))),
        scratch_shapes=[pltpu.VMEM((1, LANES), F32)],
        compiler_params=_params("arbitrary"),
        name="moe_router",
    )(x2, g, mod, w_router)


def _moe_kernel(te_ref, nt_ref, xg_ref, wg_ref, wu_ref, wd_ref, y_ref):
    i = pl.program_id(0)

    @pl.when(i < nt_ref[0])
    def _():
        h = xg_ref[...]
        gg = jnp.dot(h, wg_ref[0], preferred_element_type=F32)
        uu = jnp.dot(h, wu_ref[0], preferred_element_type=F32)
        a = (gg * jax.nn.sigmoid(gg) * uu).astype(BF16)
        y_ref[...] = jnp.dot(a, wd_ref[0], preferred_element_type=F32).astype(y_ref.dtype)

    @pl.when(i >= nt_ref[0])
    def _():
        y_ref[...] = jnp.zeros_like(y_ref)


def moe_grouped(tile_expert, n_tiles_used, xg, w_gu, w_down, tm):
    p, d = xg.shape
    fe = w_down.shape[1]
    grid_spec = pltpu.PrefetchScalarGridSpec(
        num_scalar_prefetch=2,
        grid=(p // tm,),
        in_specs=[pl.BlockSpec((tm, d), lambda i, te, nt: (i, 0)),
                  pl.BlockSpec((1, d, fe), lambda i, te, nt: (te[i], 0, 0)),
                  pl.BlockSpec((1, d, fe), lambda i, te, nt: (te[i], 0, 1)),
                  pl.BlockSpec((1, fe, d), lambda i, te, nt: (te[i], 0, 0))],
        out_specs=pl.BlockSpec((tm, d), lambda i, te, nt: (i, 0)),
    )
    return pl.pallas_call(
        _moe_kernel,
        out_shape=jax.ShapeDtypeStruct((p, d), BF16),
        grid_spec=grid_spec,
        compiler_params=_params("arbitrary"),
        name="moe_grouped",
    )(tile_expert, n_tiles_used, xg, w_gu, w_gu, w_down)


def _moe_combine_kernel(x_ref, ya_ref, yb_ref, route_ref, mod_ref, *refs):
    o_ref = refs[-1]
    r = route_ref[...]
    y = (r[:, ROUTE_G1:ROUTE_G1 + 1] * ya_ref[...].astype(F32) + r[:, ROUTE_G2:ROUTE_G2 + 1] * yb_ref[...].astype(F32))
    x = x_ref[...] + mod_ref[0, 5:6, :] * y
    if len(refs) == 2:
        ms = jnp.mean(x * x, axis=-1, keepdims=True)
        x = x * lax.rsqrt(ms + EPS) * refs[0][...]
    o_ref[...] = x


def moe_combine(x2, ya, yb, route, mod, tiles_per_batch, tm, g_final=None):
    m, d = x2.shape
    row = pl.BlockSpec((tm, d), lambda i: (i, 0))
    specs = [row, row, row, pl.BlockSpec((tm, LANES), lambda i: (i, 0)),
             pl.BlockSpec((1, 6, d), _mod_map(tiles_per_batch))]
    args = [x2, ya, yb, route, mod]
    if g_final is not None:
        specs.append(pl.BlockSpec((1, d), lambda i: (0, 0)))
        args.append(g_final)
    return pl.pallas_call(
        _moe_combine_kernel,
        out_shape=jax.ShapeDtypeStruct((m, d), F32),
        grid=(m // tm,),
        in_specs=specs,
        out_specs=row,
        compiler_params=_params("parallel"),
        name="moe_combine",
    )(*args)


def moe_block(x2, g, mod, w_router, w_gu, w_down, tiles_per_batch, tm_tok, tm_grp, g_final=None):
    m, d = x2.shape
    h, route, cnt = moe_router(x2, g, mod, w_router, tiles_per_batch, tm_tok)
    cnt = cnt[0, :N_EXPERTS].astype(jnp.int32)
    gsize = ((cnt + tm_grp - 1) // tm_grp) * tm_grp
    gend = jnp.cumsum(gsize)
    gstart = gend - gsize
    p_max = m * TOP_K + N_EXPERTS * tm_grp
    n_tiles = p_max // tm_grp
    ri = route[:, :ROUTE_G1].astype(jnp.int32)
    onehot = lambda e: (e[:, None] == jnp.arange(N_EXPERTS)[None, :]).astype(jnp.int32)
    d1 = jnp.sum(onehot(ri[:, ROUTE_E1]) * gstart[None, :], axis=1) + ri[:, ROUTE_R1]
    d2 = jnp.sum(onehot(ri[:, ROUTE_E2]) * gstart[None, :], axis=1) + ri[:, ROUTE_R2]
    tok = jnp.arange(m, dtype=jnp.int32)
    src = jnp.zeros((p_max,), jnp.int32).at[jnp.concatenate([d1, d2])].set(jnp.concatenate([tok, tok]),
                                                                          unique_indices=True)
    tile_start = jnp.arange(n_tiles, dtype=jnp.int32) * tm_grp
    tile_expert = jnp.minimum(jnp.sum((tile_start[:, None] >= gend[None, :]).astype(jnp.int32), axis=1),
                              N_EXPERTS - 1).astype(jnp.int32)
    n_used = (gend[-1] // tm_grp).astype(jnp.int32).reshape(1)
    rows = lambda a, idx: a.at[idx].get(mode="promise_in_bounds")
    xg = rows(h, src)
    yg = moe_grouped(tile_expert, n_used, xg, w_gu, w_down, tm_grp)
    return moe_combine(x2, rows(yg, d1), rows(yg, d2), route, mod, tiles_per_batch,
                       tm_tok, g_final)


def _final_norm_kernel(x_ref, g_ref, o_ref):
    x = x_ref[...]
    ms = jnp.mean(x * x, axis=-1, keepdims=True)
    o_ref[...] = x * lax.rsqrt(ms + EPS) * g_ref[...]


def final_norm(x2, g, tm):
    m, d = x2.shape
    return pl.pallas_call(
        _final_norm_kernel,
        out_shape=jax.ShapeDtypeStruct((m, d), F32),
        grid=(m // tm,),
        in_specs=[pl.BlockSpec((tm, d), lambda i: (i, 0)), pl.BlockSpec((1, d), lambda i: (0, 0))],
        out_specs=pl.BlockSpec((tm, d), lambda i: (i, 0)),
        compiler_params=_params("parallel"),
        name="final_norm",
    )(x2, g)


NA_QROWS = 4
NA_KROWS = NA_QROWS + WIN_R
Q_COL, K_COL, V_COL = 4, 5, 6


def _na_key_start(j, rows):
    return np.clip(j * NA_QROWS - WIN_R // 2, 0, rows - NA_KROWS)


def _na_bias_index(rows):
    nblk = rows // NA_QROWS
    pats = []
    for j in range(nblk):
        start = _na_key_start(j, rows)
        r = j * NA_QROWS + np.arange(NA_QROWS)
        sr = np.clip(r - WIN_R // 2, 0, rows - WIN_R)
        kr = start + np.arange(NA_KROWS)
        rvalid = (kr[None, :] >= sr[:, None]) & (kr[None, :] < sr[:, None] + WIN_R)
        ri = np.clip(kr[None, :] - r[:, None] + WIN_R - 1, 0, 2 * WIN_R - 2)
        pats.append((ri, rvalid))
    for j in range(2, nblk - 1):
        assert all(np.array_equal(a, b) for a, b in zip(pats[1], pats[j]))
    sel = [pats[0], pats[1], pats[nblk - 1]]
    ri = np.stack([p[0] for p in sel])
    rvalid = np.stack([p[1] for p in sel])
    c = np.arange(GRID_W)
    sc = np.clip(c - WIN_C // 2, 0, GRID_W - WIN_C)
    cvalid = (c[None, :] >= sc[:, None]) & (c[None, :] < sc[:, None] + WIN_C)
    ci = np.clip(c[None, :] - c[:, None] + WIN_C - 1, 0, 2 * WIN_C - 2)
    c_onehot = (ci[..., None] == np.arange(2 * WIN_C - 1)).astype(np.float32)
    valid = rvalid[:, :, None, :, None] & cvalid[None, None, :, None, :]
    return ri, c_onehot, valid


def na_bias_table(rpb, rows):
    ri, c_onehot, valid = _na_bias_index(rows)
    toep = jnp.einsum('hrd,qkd->hrqk', rpb, c_onehot, precision=lax.Precision.HIGHEST)
    b = jnp.take(toep, ri.reshape(-1), axis=1).reshape((NA_HEADS,) + ri.shape + (GRID_W, GRID_W))
    b = jnp.transpose(b, (1, 0, 2, 4, 3, 5))
    b = jnp.where(valid[:, None], b, -jnp.inf)
    return b.reshape(3, NA_HEADS, NA_QROWS * GRID_W, NA_KROWS * GRID_W).astype(F32)


def _attend_heads(q, key_sets, bias_fn):
    n, w = q.shape
    lane = lax.broadcasted_iota(jnp.int32, (1, w), 1)
    out = jnp.zeros((n, w), F32)
    for h in range(NA_HEADS):
        mh = (lane >= h * NA_HEAD_DIM) & (lane < (h + 1) * NA_HEAD_DIM)
        qh = jnp.where(mh, q, 0.0).astype(BF16)
        scores = []
        for i, (k, _) in enumerate(key_sets):
            s = lax.dot_general(qh, k, (((1,), (1,)), ((), ())), preferred_element_type=F32)
            b = bias_fn(i, h)
            scores.append(s if b is None else s + b)
        m = scores[0].max(axis=-1, keepdims=True)
        for s in scores[1:]:
            m = jnp.maximum(m, s.max(axis=-1, keepdims=True))
        denom = jnp.zeros((n, 1), F32)
        acc = jnp.zeros((n, w), F32)
        for s, (_, v) in zip(scores, key_sets):
            p = jnp.exp(s - m)
            denom = denom + p.sum(axis=-1, keepdims=True)
            acc = acc + jnp.dot(p.astype(BF16), v, preferred_element_type=F32)
        out = out + jnp.where(mh, acc / denom, 0.0)
    return out


def _na_kernel(q_ref, k_ref, v_ref, kc_ref, vc_ref, bias_ref, o_ref, *, rows):
    j = pl.program_id(1)
    start = jnp.clip(j * NA_QROWS - WIN_R // 2, 0, rows - NA_KROWS)
    t0 = pl.multiple_of(start * GRID_W, GRID_W)
    nk = NA_KROWS * GRID_W
    kw = k_ref[0, pl.ds(t0, nk), :].astype(BF16)
    vw = v_ref[0, pl.ds(t0, nk), :].astype(BF16)
    kc = kc_ref[0].astype(BF16)
    vc = vc_ref[0].astype(BF16)
    q = q_ref[0] * (1.0 / math.sqrt(NA_HEAD_DIM))
    o_ref[0] = _attend_heads(q, [(kw, vw), (kc, vc)], lambda i, h: bias_ref[0, h] if i == 0 else None)


def na_attention(pr3, prc3, bias):
    bn, S, _ = pr3.shape
    Lc = prc3.shape[1]
    rows = S // GRID_W
    nblk = rows // NA_QROWS
    nq = NA_QROWS * GRID_W

    def pat(b, j):
        return (jnp.where(j == 0, 0, jnp.where(j == nblk - 1, 2, 1)), 0, 0, 0)

    return pl.pallas_call(
        functools.partial(_na_kernel, rows=rows),
        out_shape=jax.ShapeDtypeStruct((bn, S, W_NA), F32),
        grid=(bn, nblk),
        in_specs=[pl.BlockSpec((1, nq, W_NA), lambda b, j: (b, j, Q_COL)),
                  pl.BlockSpec((1, S, W_NA), lambda b, j: (b, 0, K_COL)),
                  pl.BlockSpec((1, S, W_NA), lambda b, j: (b, 0, V_COL)),
                  pl.BlockSpec((1, Lc, W_NA), lambda b, j: (b, 0, K_COL)),
                  pl.BlockSpec((1, Lc, W_NA), lambda b, j: (b, 0, V_COL)),
                  pl.BlockSpec((1,) + bias.shape[1:], pat)],
        out_specs=pl.BlockSpec((1, nq, W_NA), lambda b, j: (b, j, 0)),
        compiler_params=_params("parallel", "arbitrary"),
        name="na_attention",
    )(pr3, pr3, pr3, prc3, prc3, bias)


def _ctx_attn_kernel(q_ref, k_ref, v_ref, o_ref):
    q = q_ref[0] * (1.0 / math.sqrt(NA_HEAD_DIM))
    o_ref[0] = _attend_heads(q, [(k_ref[0].astype(BF16), v_ref[0].astype(BF16))], lambda i, h: None)


def ctx_attention(prc3):
    bn, Lc, _ = prc3.shape
    return pl.pallas_call(
        _ctx_attn_kernel,
        out_shape=jax.ShapeDtypeStruct((bn, Lc, W_NA), F32),
        grid=(bn,),
        in_specs=[pl.BlockSpec((1, Lc, W_NA), lambda b: (b, 0, Q_COL)),
                  pl.BlockSpec((1, Lc, W_NA), lambda b: (b, 0, K_COL)),
                  pl.BlockSpec((1, Lc, W_NA), lambda b: (b, 0, V_COL))],
        out_specs=pl.BlockSpec((1, Lc, W_NA), lambda b: (b, 0, 0)),
        compiler_params=_params("parallel"),
        name="ctx_attention",
    )(prc3, prc3, prc3)


POOL_COL, CONV_H_COL, CONV_B_COL, CONV_C_COL = 0, 1, 2, 3


def _shift_rows(x, k, row):
    n = x.shape[0]
    y = pltpu.roll(x, k % n, 0)
    return jnp.where(row < k, 0.0, y) if k > 0 else jnp.where(row >= n + k, 0.0, y)


def _local_mix_kernel(u_ref, h_ref, bg_ref, cg_ref, pw_ref, ps_ref, cw_ref, op_ref, oc_ref, *, seq):
    row = lax.broadcasted_iota(jnp.int32, (seq, 1), 0)
    lane = lax.broadcasted_iota(jnp.int32, (1, W_POOL), 1)
    u = u_ref[0]
    trailing, leading = {1: u}, {1: u}
    for w in (1, 2, 4):
        trailing[2 * w] = trailing[w] + _shift_rows(trailing[w], w, row)
        leading[2 * w] = leading[w] + _shift_rows(leading[w], -w, row)
    rowf = row.astype(F32)
    win_sum = jnp.zeros_like(u)
    cnt = jnp.zeros_like(u)
    for g, win in enumerate(POOL_WINDOWS):
        half = win // 2
        mg = (lane >= g * POOL_GROUP) & (lane < (g + 1) * POOL_GROUP)
        s = _shift_rows(trailing[half], 1, row) + leading[half]
        n = jnp.minimum(rowf + half, float(seq)) - jnp.maximum(rowf - half, 0.0)
        win_sum = jnp.where(mg, s, win_sum)
        cnt = jnp.where(mg, n, cnt)
    p = win_sum / cnt - u
    op_ref[0] = jnp.dot(p.astype(BF16), pw_ref[...], preferred_element_type=F32) * ps_ref[...]
    v = cg_ref[0] * h_ref[0]
    conv = (_shift_rows(v, 1, row) * cw_ref[0:1, :] + v * cw_ref[1:2, :] + _shift_rows(v, -1, row) * cw_ref[2:3, :])
    oc_ref[0] = bg_ref[0] * conv


def local_mix(pr3, pool_w, pool_scale, conv_w):
    bn, seq, _ = pr3.shape
    pw = jax.scipy.linalg.block_diag(*[pool_w[g] for g in range(len(POOL_WINDOWS))]).astype(BF16)
    blk = lambda col: pl.BlockSpec((1, seq, W_POOL), lambda b: (b, 0, col))
    full = lambda a: pl.BlockSpec(a.shape, lambda b: (0,) * a.ndim)
    args = (pw, pool_scale[None, :], conv_w)
    return pl.pallas_call(
        functools.partial(_local_mix_kernel, seq=seq),
        out_shape=(jax.ShapeDtypeStruct((bn, seq, W_POOL), F32), jax.ShapeDtypeStruct((bn, seq, W_CONV), F32)),
        grid=(bn,),
        in_specs=[blk(POOL_COL), blk(CONV_H_COL), blk(CONV_B_COL), blk(CONV_C_COL)] + [full(a) for a in args],
        out_specs=(pl.BlockSpec((1, seq, W_POOL), lambda b: (b, 0, 0)),
                   pl.BlockSpec((1, seq, W_CONV), lambda b: (b, 0, 0))),
        compiler_params=_params("parallel"),
        name="local_mix",
    )(pr3, pr3, pr3, pr3, *args)


Z_COL, XS_COL, BS_COL, CS_COL = 7, 8, 9, 10
DT_COL = 22
HEADS_PER_GROUP = SSD_HEADS // SSD_GROUPS


def _expand_heads(v, d):
    lane = lax.broadcasted_iota(jnp.int32, (1, W_SSD), 1)
    out = jnp.zeros((v.shape[0], W_SSD), F32)
    for h in range(SSD_HEADS):
        k = d * SSD_HEADS + h
        mh = (lane >= h * SSD_HEAD_DIM) & (lane < (h + 1) * SSD_HEAD_DIM)
        out = jnp.where(mh, v[:, k:k + 1], out)
    return out


def _ssd_chunk(c, d, act_ref, dt_ref, la_ref, y_ref, st_ref, need_y):
    T = SSD_CHUNK
    r0 = pl.multiple_of(c * T, T)
    xs = act_ref[pl.ds(r0, T), 0:W_SSD]
    bm = act_ref[pl.ds(r0, T), W_SSD:2 * W_SSD]
    cm = act_ref[pl.ds(r0, T), 2 * W_SSD:3 * W_SSD]
    dt = dt_ref[pl.ds(r0, T), :]
    la = la_ref[pl.ds(r0, T), :]
    li = lax.broadcasted_iota(jnp.int32, (T, T), 0)
    si = lax.broadcasted_iota(jnp.int32, (T, T), 1)
    mask = (si <= li) if d == 0 else (si >= li)
    acum = jnp.dot(mask.astype(F32), la, precision=lax.Precision.HIGHEST, preferred_element_type=F32)
    tot = acum[T - 1:T, :] if d == 0 else acum[0:1, :]
    acum_e = _expand_heads(acum, d)
    tot_e = _expand_heads(tot, d)
    xdt = xs * _expand_heads(dt, d)
    xw = (xdt * jnp.exp(tot_e - acum_e)).astype(BF16)
    st = st_ref[d]
    bt = bm.T.astype(BF16)
    if need_y:
        acum_t = acum.T
        xdt_b = xdt.astype(BF16)
        lane = lax.broadcasted_iota(jnp.int32, (1, W_SSD), 1)
        y = jnp.zeros((T, W_SSD), F32)
        y_state = []
    new_st = []
    gw = HEADS_PER_GROUP * SSD_HEAD_DIM
    for g in range(SSD_GROUPS):
        bg = bm[:, g * SSD_STATE:(g + 1) * SSD_STATE].astype(BF16)
        cg = cm[:, g * SSD_STATE:(g + 1) * SSD_STATE].astype(BF16)
        st_g = st[:, g * gw:(g + 1) * gw]
        if need_y:
            cb = lax.dot_general(cg, bg, (((1,), (1,)), ((), ())), preferred_element_type=F32)
            for h in range(g * HEADS_PER_GROUP, (g + 1) * HEADS_PER_GROUP):
                k = d * SSD_HEADS + h
                decay = jnp.exp(jnp.where(mask, acum[:, k:k + 1] - acum_t[k:k + 1, :], -jnp.inf))
                mh = (lane >= h * SSD_HEAD_DIM) & (lane < (h + 1) * SSD_HEAD_DIM)
                y = y + jnp.where(mh, jnp.dot((cb * decay).astype(BF16), xdt_b, preferred_element_type=F32), 0.0)
            y_state.append(jnp.dot(cg, st_g.astype(BF16), preferred_element_type=F32))
        upd = jnp.dot(bt[g * SSD_STATE:(g + 1) * SSD_STATE, :], xw[:, g * gw:(g + 1) * gw],
                      preferred_element_type=F32)
        new_st.append(jnp.exp(tot_e[:, g * gw:(g + 1) * gw]) * st_g + upd)
    st_ref[d] = jnp.concatenate(new_st, axis=1)
    if need_y:
        y_ref[pl.ds(r0, T), :] += y + jnp.concatenate(y_state, axis=1) * jnp.exp(acum_e)


def _ssd_kernel(z_ref, xs_ref, bs_ref, cs_ref, dtr_ref, cw_ref, cb_ref, dtb_ref, alog_ref, dsk_ref, ng_ref, h0_ref,
                *refs, seq, need_y):
    if need_y:
        o_ref, hT_ref, act_ref, dt_ref, la_ref, st_ref, y_ref = refs
    else:
        hT_ref, act_ref, dt_ref, la_ref, st_ref = refs
        y_ref = None
    nc = seq // SSD_CHUNK
    row = lax.broadcasted_iota(jnp.int32, (seq, 1), 0)
    for gi, ref in enumerate((xs_ref, bs_ref, cs_ref)):
        sl = slice(gi * W_SSD, (gi + 1) * W_SSD)
        x = ref[0]
        cv = (_shift_rows(x, 1, row) * cw_ref[0:1, sl] + x * cw_ref[1:2, sl]
              + _shift_rows(x, -1, row) * cw_ref[2:3, sl] + cb_ref[:, sl])
        act_ref[:, sl] = cv * jax.nn.sigmoid(cv)
    dtv = dtr_ref[0] + dtb_ref[...]
    dt = jnp.maximum(dtv, 0.0) + jnp.log1p(jnp.exp(-jnp.abs(dtv)))
    dt_ref[...] = dt
    la_ref[...] = dt * (-jnp.exp(alog_ref[...]))
    st_ref[...] = h0_ref[0]
    if need_y:
        y_ref[...] = act_ref[:, 0:W_SSD] * dsk_ref[...]

    def body(i, carry):
        _ssd_chunk(i, 0, act_ref, dt_ref, la_ref, y_ref, st_ref, need_y)
        _ssd_chunk(nc - 1 - i, 1, act_ref, dt_ref, la_ref, y_ref, st_ref, need_y)
        return carry

    lax.fori_loop(0, nc, body, 0, unroll=2)
    hT_ref[0] = st_ref[...]
    if need_y:
        z = z_ref[0]
        yz = y_ref[...] * (z * jax.nn.sigmoid(z))
        ms = jnp.mean(yz * yz, axis=-1, keepdims=True)
        o_ref[0] = yz * lax.rsqrt(ms + EPS) * ng_ref[...]


def ssd_scan(pr3, h0, conv_w, conv_b, dt_bias, a_log, d_skip, norm_g, need_y):
    bn, seq, _ = pr3.shape
    pad = LANES - 2 * SSD_HEADS
    dtb = jnp.pad(dt_bias.reshape(1, -1), ((0, 0), (0, pad)))
    alog = jnp.pad(a_log.reshape(1, -1), ((0, 0), (0, pad)))
    dsk = jnp.repeat(d_skip, SSD_HEAD_DIM)[None, :]
    blk = lambda col, w=W_SSD: pl.BlockSpec((1, seq, w), lambda b: (b, 0, col))
    full = lambda a: pl.BlockSpec(a.shape, lambda b: (0,) * a.ndim)
    st_shape = (1, 2, SSD_STATE, W_SSD)
    st_spec = pl.BlockSpec(st_shape, lambda b: (b, 0, 0, 0))
    args = (conv_w, conv_b[None, :], dtb, alog, dsk, norm_g[None, :])
    out_shape = [jax.ShapeDtypeStruct((bn,) + st_shape[1:], F32)]
    out_specs = [st_spec]
    scratch = [pltpu.VMEM((seq, 3 * W_SSD), F32), pltpu.VMEM((seq, LANES), F32), pltpu.VMEM((seq, LANES), F32),
               pltpu.VMEM(st_shape[1:], F32)]
    if need_y:
        out_shape.insert(0, jax.ShapeDtypeStruct((bn, seq, W_SSD), F32))
        out_specs.insert(0, pl.BlockSpec((1, seq, W_SSD), lambda b: (b, 0, 0)))
        scratch.append(pltpu.VMEM((seq, W_SSD), F32))
    res = pl.pallas_call(
        functools.partial(_ssd_kernel, seq=seq, need_y=need_y),
        out_shape=out_shape,
        grid=(bn,),
        in_specs=[blk(Z_COL), blk(XS_COL), blk(BS_COL), blk(CS_COL), blk(DT_COL, LANES)]
                 + [full(a) for a in args] + [st_spec],
        out_specs=out_specs,
        scratch_shapes=scratch,
        compiler_params=_params("parallel"),
        name="ssd_scan",
    )(pr3, pr3, pr3, pr3, pr3, *args, h0)
    return (res[0], res[1]) if need_y else (None, res[0])


def ssd_mix(pr3, prc3, conv_w, conv_b, dt_bias, a_log, d_skip, norm_g, ctx_out):
    bn = pr3.shape[0]
    h0 = jnp.zeros((bn, 2, SSD_STATE, W_SSD), F32)
    oc, hc = ssd_scan(prc3, h0, conv_w, conv_b, dt_bias, a_log, d_skip, norm_g, ctx_out)
    o, _ = ssd_scan(pr3, hc, conv_w, conv_b, dt_bias, a_log, d_skip, norm_g, True)
    return o, oc


def mixer(pr3, prc3, pool_w, pool_scale, conv_w, rpb, s_conv_w, s_conv_b, dt_bias, a_log, d_skip, s_norm_g,
          ctx_out):
    bn, S, _ = pr3.shape
    o_ssd, oc_ssd = ssd_mix(pr3, prc3, s_conv_w, s_conv_b, dt_bias, a_log, d_skip, s_norm_g, ctx_out)
    o = [*local_mix(pr3, pool_w, pool_scale, conv_w),
         na_attention(pr3, prc3, na_bias_table(rpb, S // GRID_W)), o_ssd]
    if not ctx_out:
        return o, None
    oc = [*local_mix(prc3, pool_w, pool_scale, conv_w), ctx_attention(prc3), oc_ssd]
    return o, oc


TM = 512
TM_CTX = 256
TF = 1408
TM_GRP = 512


def kernel(x, c, ctx, c_ctx, w_ada, b_ada, g_mix, g_ffn, w_in, w_out, pool_w, pool_scale, conv_w, na_rpb,
           ssd_conv_w, ssd_conv_b, ssd_dt_bias, ssd_a_log, ssd_d, ssd_norm_g, ffn_w_gu, ffn_w_down,
           moe_router, moe_w_gu, moe_w_down, g_final):
    bn, S, d = x.shape
    Lc = ctx.shape[1]
    m, mc = bn * S, bn * Lc
    tpb = S // TM
    x2 = x.reshape(m, d)
    xc2 = ctx.reshape(mc, d)
    c_rows = jnp.concatenate([c, c_ctx[None, :], jnp.zeros((7, d), F32)], axis=0)
    for l in range(DEPTH):
        ctx_out = l < DEPTH - 1
        last = l == DEPTH - 1
        ada = ada_modulation(c_rows, w_ada[l].astype(BF16), b_ada[l][None, :])
        mod = ada[:bn].reshape(bn, 6, d)
        mod_c = ada[bn:bn + 1].reshape(1, 6, d)
        w_in_l = jnp.pad(w_in[l], ((0, 0), (0, D_IN_PAD - D_IN_PROJ))).astype(BF16)
        g_m = g_mix[l][None, :]
        g_f = g_ffn[l][None, :]
        pr = in_proj(x2, g_m, mod, w_in_l, tpb, TM)
        pr_c = in_proj(xc2, g_m, mod_c, w_in_l, None, TM_CTX)
        o, oc = mixer(pr.reshape(bn, S, -1), pr_c.reshape(bn, Lc, -1), pool_w[l], pool_scale[l], conv_w[l],
                      na_rpb[l], ssd_conv_w[l], ssd_conv_b[l], ssd_dt_bias[l], ssd_a_log[l], ssd_d[l],
                      ssd_norm_g[l], ctx_out)
        w_out_l = w_out[l].astype(BF16)
        x2 = out_proj(x2, [p.reshape(m, -1) for p in o], mod, w_out_l, tpb, TM)
        if ctx_out:
            xc2 = out_proj(xc2, [p.reshape(mc, -1) for p in oc], mod_c, w_out_l, None, TM_CTX)
        j = l // 2
        if l % 2 == 0:
            w_gu = ffn_w_gu[j].astype(BF16)
            w_dn = ffn_w_down[j].astype(BF16)
            x2 = ffn_dense(x2, g_f, mod, w_gu, w_dn, tpb, TM, TF)
            if ctx_out:
                xc2 = ffn_dense(xc2, g_f, mod_c, w_gu, w_dn, None, TM_CTX, TF)
        else:
            w_r = jnp.pad(moe_router[j], ((0, 0), (0, LANES - N_EXPERTS))).astype(BF16)
            w_gu = moe_w_gu[j].astype(BF16)
            w_dn = moe_w_down[j].astype(BF16)
            x2 = moe_block(x2, g_f, mod, w_r, w_gu, w_dn, tpb, TM, TM_GRP, g_final[None, :] if last else None)
            if ctx_out:
                xc2 = moe_block(xc2, g_f, mod_c, w_r, w_gu, w_dn, None, TM_CTX, TM_GRP)
            if last:
                return x2.reshape(bn, S, d)
    return final_norm(x2, g_final[None, :], TM).reshape(bn, S, d)
```

```python
import functools
import math

import numpy as np
import jax
import jax.numpy as jnp
from jax import lax
from jax.experimental import pallas as pl
from jax.experimental.pallas import tpu as pltpu

D_MODEL = 1024
DEPTH = 2
GRID_W = 64
EPS = 1e-6
W_POOL = 256
W_CONV = 256
W_NA = 256
W_SSD = 256
POOL_WINDOWS = (2, 4, 8, 16)
POOL_GROUP = W_POOL // len(POOL_WINDOWS)
NA_HEADS = 4
NA_HEAD_DIM = W_NA // NA_HEADS
WIN_R = 8
WIN_C = 16
COL_BLOCK = 16
COL_BAND = 32
SSD_HEAD_DIM = 64
SSD_HEADS = W_SSD // SSD_HEAD_DIM
SSD_GROUPS = 2
SSD_STATE = 128
SSD_CHUNK = 128
SSD_XBC = W_SSD + 2 * SSD_GROUPS * SSD_STATE
D_IN_PROJ = W_POOL + 3 * W_CONV + 3 * W_NA + W_SSD + SSD_XBC + 2 * SSD_HEADS
PROJ_SPLITS = (W_POOL, W_POOL + 3 * W_CONV, W_POOL + 3 * W_CONV + 3 * W_NA)
D_FF = 2816
N_EXPERTS = 8
TOP_K = 2
D_FF_EXPERT = 1408

LANES = 128
D_IN_PAD = -(-D_IN_PROJ // LANES) * LANES
VMEM_LIMIT = 56 * 1024 * 1024
BF16 = jnp.bfloat16
F32 = jnp.float32


def _params(*sem):
    return pltpu.CompilerParams(dimension_semantics=sem, vmem_limit_bytes=VMEM_LIMIT)


def _norm_mod(x, g, scale, shift):
    ms = jnp.mean(x * x, axis=-1, keepdims=True)
    return (x * lax.rsqrt(ms + EPS) * g) * (1.0 + scale) + shift


def _mod_map(tiles_per_batch):
    if tiles_per_batch is None:
        return lambda i, *_: (0, 0, 0)
    return lambda i, *_: (i // tiles_per_batch, 0, 0)


def _ada_kernel(c_ref, w_ref, b_ref, o_ref):
    c = c_ref[...]
    s = c * jax.nn.sigmoid(c)
    o_ref[...] = jnp.dot(s.astype(BF16), w_ref[...], preferred_element_type=F32) + b_ref[...]


def ada_modulation(c_rows, w, b):
    r, d = c_rows.shape
    n = w.shape[1]
    tn = 1536
    return pl.pallas_call(
        _ada_kernel,
        out_shape=jax.ShapeDtypeStruct((r, n), F32),
        grid=(n // tn,),
        in_specs=[pl.BlockSpec((r, d), lambda j: (0, 0)),
                  pl.BlockSpec((d, tn), lambda j: (0, j)),
                  pl.BlockSpec((1, tn), lambda j: (0, j))],
        out_specs=pl.BlockSpec((r, tn), lambda j: (0, j)),
        compiler_params=_params("arbitrary"),
        name="ada_modulation",
    )(c_rows, w, b)


def _in_proj_kernel(x_ref, g_ref, mod_ref, w_ref, o_ref):
    h = _norm_mod(x_ref[...], g_ref[...], mod_ref[0, 1:2, :], mod_ref[0, 0:1, :])
    o_ref[...] = jnp.dot(h.astype(BF16), w_ref[...], preferred_element_type=F32).astype(o_ref.dtype)


def in_proj(x2, g, mod, w, tiles_per_batch, tm, out_dtype=F32):
    m, d = x2.shape
    n = w.shape[1]
    return pl.pallas_call(
        _in_proj_kernel,
        out_shape=jax.ShapeDtypeStruct((m, n), out_dtype),
        grid=(m // tm,),
        in_specs=[pl.BlockSpec((tm, d), lambda i: (i, 0)),
                  pl.BlockSpec((1, d), lambda i: (0, 0)),
                  pl.BlockSpec((1, 6, d), _mod_map(tiles_per_batch)),
                  pl.BlockSpec((d, n), lambda i: (0, 0))],
        out_specs=pl.BlockSpec((tm, n), lambda i: (i, 0)),
        compiler_params=_params("parallel"),
        name="in_proj",
    )(x2, g, mod, w)


def _out_proj_kernel(x_ref, mod_ref, w_ref, *refs):
    part_refs, y_ref = refs[:-1], refs[-1]
    y = jnp.zeros(y_ref.shape, F32)
    k0 = 0
    for p_ref in part_refs:
        k = p_ref.shape[1]
        y = y + jnp.dot(p_ref[...].astype(BF16), w_ref[k0:k0 + k, :], preferred_element_type=F32)
        k0 += k
    y_ref[...] = x_ref[...] + mod_ref[0, 2:3, :] * y


def out_proj(x2, parts, mod, w, tiles_per_batch, tm):
    m, d = x2.shape
    return pl.pallas_call(
        _out_proj_kernel,
        out_shape=jax.ShapeDtypeStruct((m, d), F32),
        grid=(m // tm,),
        in_specs=[pl.BlockSpec((tm, d), lambda i: (i, 0)),
                  pl.BlockSpec((1, 6, d), _mod_map(tiles_per_batch)),
                  pl.BlockSpec(w.shape, lambda i: (0, 0))]
                 + [pl.BlockSpec((tm, p.shape[1]), lambda i: (i, 0)) for p in parts],
        out_specs=pl.BlockSpec((tm, d), lambda i: (i, 0)),
        compiler_params=_params("parallel"),
        name="out_proj",
    )(x2, mod, w, *parts)


FF_CHUNK = 512


def _swiglu(h, wgu, wd, ff):
    acc = None
    for c0 in range(0, ff, FF_CHUNK):
        c1 = min(c0 + FF_CHUNK, ff)
        gg = jnp.dot(h, wgu(c0, c1), preferred_element_type=F32)
        uu = jnp.dot(h, wgu(ff + c0, ff + c1), preferred_element_type=F32)
        a = (gg * jax.nn.sigmoid(gg) * uu).astype(BF16)
        part = jnp.dot(a, wd(c0, c1), preferred_element_type=F32)
        acc = part if acc is None else acc + part
    return acc


def _ffn_kernel(x_ref, g_ref, mod_ref, wgu_ref, wd_ref, y_ref):
    x = x_ref[...]
    h = _norm_mod(x, g_ref[...], mod_ref[0, 4:5, :], mod_ref[0, 3:4, :]).astype(BF16)
    y = _swiglu(h, lambda a, b: wgu_ref[:, a:b], lambda a, b: wd_ref[a:b, :], wd_ref.shape[0])
    y_ref[...] = x + mod_ref[0, 5:6, :] * y


def _resident(shape, index_map):
    return pl.BlockSpec(shape, index_map, pipeline_mode=pl.Buffered(1))


def ffn_dense(x2, g, mod, w_gu, w_down, tiles_per_batch, tm):
    m, d = x2.shape
    return pl.pallas_call(
        _ffn_kernel,
        out_shape=jax.ShapeDtypeStruct((m, d), F32),
        grid=(m // tm,),
        in_specs=[pl.BlockSpec((tm, d), lambda i: (i, 0)),
                  pl.BlockSpec((1, d), lambda i: (0, 0)),
                  pl.BlockSpec((1, 6, d), _mod_map(tiles_per_batch)),
                  _resident(w_gu.shape, lambda i: (0, 0)),
                  _resident(w_down.shape, lambda i: (0, 0))],
        out_specs=pl.BlockSpec((tm, d), lambda i: (i, 0)),
        compiler_params=_params("parallel"),
        name="ffn_dense",
    )(x2, g, mod, w_gu, w_down)


ROUTE_E1, ROUTE_E2, ROUTE_R1, ROUTE_R2, ROUTE_G1, ROUTE_G2 = range(6)


def _router_kernel(x_ref, g_ref, mod_ref, wr_ref, h_ref, route_ref, cnt_ref, base_ref):
    @pl.when(pl.program_id(0) == 0)
    def _():
        base_ref[...] = jnp.zeros_like(base_ref)

    h = _norm_mod(x_ref[...], g_ref[...], mod_ref[0, 4:5, :], mod_ref[0, 3:4, :]).astype(BF16)
    h_ref[...] = h
    tm = h.shape[0]
    logits = jnp.dot(h, wr_ref[...], preferred_element_type=F32)
    lane = lax.broadcasted_iota(jnp.int32, logits.shape, 1)
    neg = jnp.float32(-jnp.inf)
    logits = jnp.where(lane < N_EXPERTS, logits, neg)
    m1 = jnp.max(logits, axis=-1, keepdims=True)
    i1 = jnp.min(jnp.where(logits == m1, lane, LANES), axis=-1, keepdims=True)
    rest = jnp.where(lane == i1, neg, logits)
    m2 = jnp.max(rest, axis=-1, keepdims=True)
    i2 = jnp.min(jnp.where(rest == m2, lane, LANES), axis=-1, keepdims=True)
    e2 = jnp.exp(m2 - m1)
    g1 = 1.0 / (1.0 + e2)
    g2 = e2 / (1.0 + e2)
    sel1, sel2 = lane == i1, lane == i2
    sel = jnp.where(sel1 | sel2, 1.0, 0.0)
    li = lax.broadcasted_iota(jnp.int32, (tm, tm), 0)
    si = lax.broadcasted_iota(jnp.int32, (tm, tm), 1)
    before = jnp.where(si < li, 1.0, 0.0).astype(BF16)
    rank = jnp.dot(before, sel.astype(BF16), preferred_element_type=F32) + base_ref[...]
    r1 = jnp.sum(jnp.where(sel1, rank, 0.0), axis=-1, keepdims=True)
    r2 = jnp.sum(jnp.where(sel2, rank, 0.0), axis=-1, keepdims=True)
    base_ref[...] += jnp.sum(sel, axis=0, keepdims=True)
    cnt_ref[...] = base_ref[...]
    fields = (i1.astype(F32), i2.astype(F32), r1, r2, g1, g2)
    route = jnp.zeros(logits.shape, F32)
    for k, v in enumerate(fields):
        route = jnp.where(lane == k, v, route)
    route_ref[...] = route


def moe_router(x2, g, mod, w_router, tiles_per_batch, tm):
    m, d = x2.shape
    return pl.pallas_call(
        _router_kernel,
        out_shape=(jax.ShapeDtypeStruct((m, d), BF16), jax.ShapeDtypeStruct((m, LANES), F32),
                   jax.ShapeDtypeStruct((1, LANES), F32)),
        grid=(m // tm,),
        in_specs=[pl.BlockSpec((tm, d), lambda i: (i, 0)),
                  pl.BlockSpec((1, d), lambda i: (0, 0)),
                  pl.BlockSpec((1, 6, d), _mod_map(tiles_per_batch)),
                  pl.BlockSpec((d, LANES), lambda i: (0, 0))],
        out_specs=(pl.BlockSpec((tm, d), lambda i: (i, 0)),
                   pl.BlockSpec((tm, LANES), lambda i: (i, 0)),
                   pl.BlockSpec((1, LANES), lambda i: (0, 0))),
        scratch_shapes=[pltpu.VMEM((1, LANES), F32)],
        compiler_params=_params("arbitrary"),
        name="moe_router",
    )(x2, g, mod, w_router)


def _moe_kernel(te_ref, nt_ref, xg_ref, wgu_ref, wd_ref, y_ref):
    i = pl.program_id(0)

    @pl.when(i < nt_ref[0])
    def _():
        y = _swiglu(xg_ref[...], lambda a, b: wgu_ref[0, :, a:b], lambda a, b: wd_ref[0, a:b, :], wd_ref.shape[1])
        y_ref[...] = y.astype(y_ref.dtype)

    @pl.when(i >= nt_ref[0])
    def _():
        y_ref[...] = jnp.zeros_like(y_ref)


def moe_grouped(tile_expert, n_tiles_used, xg, w_gu, w_down, tm):
    p, d = xg.shape
    grid_spec = pltpu.PrefetchScalarGridSpec(
        num_scalar_prefetch=2,
        grid=(p // tm,),
        in_specs=[pl.BlockSpec((tm, d), lambda i, te, nt: (i, 0)),
                  _resident((1,) + w_gu.shape[1:], lambda i, te, nt: (te[i], 0, 0)),
                  _resident((1,) + w_down.shape[1:], lambda i, te, nt: (te[i], 0, 0))],
        out_specs=pl.BlockSpec((tm, d), lambda i, te, nt: (i, 0)),
    )
    return pl.pallas_call(
        _moe_kernel,
        out_shape=jax.ShapeDtypeStruct((p, d), BF16),
        grid_spec=grid_spec,
        compiler_params=_params("arbitrary"),
        name="moe_grouped",
    )(tile_expert, n_tiles_used, xg, w_gu, w_down)


def _moe_combine_kernel(x_ref, ya_ref, yb_ref, route_ref, mod_ref, *refs):
    o_ref = refs[-1]
    r = route_ref[...]
    y = (r[:, ROUTE_G1:ROUTE_G1 + 1] * ya_ref[...].astype(F32) + r[:, ROUTE_G2:ROUTE_G2 + 1] * yb_ref[...].astype(F32))
    x = x_ref[...] + mod_ref[0, 5:6, :] * y
    if len(refs) == 2:
        ms = jnp.mean(x * x, axis=-1, keepdims=True)
        x = x * lax.rsqrt(ms + EPS) * refs[0][...]
    o_ref[...] = x


def moe_combine(x2, ya, yb, route, mod, tiles_per_batch, tm, g_final=None):
    m, d = x2.shape
    row = pl.BlockSpec((tm, d), lambda i: (i, 0))
    specs = [row, row, row, pl.BlockSpec((tm, LANES), lambda i: (i, 0)),
             pl.BlockSpec((1, 6, d), _mod_map(tiles_per_batch))]
    args = [x2, ya, yb, route, mod]
    if g_final is not None:
        specs.append(pl.BlockSpec((1, d), lambda i: (0, 0)))
        args.append(g_final)
    return pl.pallas_call(
        _moe_combine_kernel,
        out_shape=jax.ShapeDtypeStruct((m, d), F32),
        grid=(m // tm,),
        in_specs=specs,
        out_specs=row,
        compiler_params=_params("parallel"),
        name="moe_combine",
    )(*args)


def moe_block(x2, g, mod, w_router, w_gu, w_down, tiles_per_batch, tm_tok, tm_grp, g_final=None):
    m, d = x2.shape
    h, route, cnt = moe_router(x2, g, mod, w_router, tiles_per_batch, tm_tok)
    cnt = cnt[0, :N_EXPERTS].astype(jnp.int32)
    gsize = ((cnt + tm_grp - 1) // tm_grp) * tm_grp
    gend = jnp.cumsum(gsize)
    gstart = gend - gsize
    p_max = m * TOP_K + N_EXPERTS * tm_grp
    n_tiles = p_max // tm_grp
    ri = route[:, :ROUTE_G1].astype(jnp.int32)
    onehot = lambda e: (e[:, None] == jnp.arange(N_EXPERTS)[None, :]).astype(jnp.int32)
    d1 = jnp.sum(onehot(ri[:, ROUTE_E1]) * gstart[None, :], axis=1) + ri[:, ROUTE_R1]
    d2 = jnp.sum(onehot(ri[:, ROUTE_E2]) * gstart[None, :], axis=1) + ri[:, ROUTE_R2]
    tok = jnp.arange(m, dtype=jnp.int32)
    src = jnp.zeros((p_max,), jnp.int32).at[jnp.concatenate([d1, d2])].set(jnp.concatenate([tok, tok]),
                                                                          unique_indices=True)
    tile_start = jnp.arange(n_tiles, dtype=jnp.int32) * tm_grp
    tile_expert = jnp.minimum(jnp.sum((tile_start[:, None] >= gend[None, :]).astype(jnp.int32), axis=1),
                              N_EXPERTS - 1).astype(jnp.int32)
    n_used = (gend[-1] // tm_grp).astype(jnp.int32).reshape(1)
    rows = lambda a, idx: a.at[idx].get(mode="promise_in_bounds")
    xg = rows(h, src)
    yg = moe_grouped(tile_expert, n_used, xg, w_gu, w_down, tm_grp)
    return moe_combine(x2, rows(yg, d1), rows(yg, d2), route, mod, tiles_per_batch,
                       tm_tok, g_final)


def _final_norm_kernel(x_ref, g_ref, o_ref):
    x = x_ref[...]
    ms = jnp.mean(x * x, axis=-1, keepdims=True)
    o_ref[...] = x * lax.rsqrt(ms + EPS) * g_ref[...]


def final_norm(x2, g, tm):
    m, d = x2.shape
    return pl.pallas_call(
        _final_norm_kernel,
        out_shape=jax.ShapeDtypeStruct((m, d), F32),
        grid=(m // tm,),
        in_specs=[pl.BlockSpec((tm, d), lambda i: (i, 0)), pl.BlockSpec((1, d), lambda i: (0, 0))],
        out_specs=pl.BlockSpec((tm, d), lambda i: (i, 0)),
        compiler_params=_params("parallel"),
        name="final_norm",
    )(x2, g)


NA_QROWS = 4
NA_KROWS = NA_QROWS + WIN_R
Q_COL, K_COL, V_COL = 4, 5, 6


def _na_key_start(j, rows):
    return np.clip(j * NA_QROWS - WIN_R // 2, 0, rows - NA_KROWS)


def _na_bias_index(rows):
    nblk = rows // NA_QROWS
    pats = []
    for j in range(nblk):
        start = _na_key_start(j, rows)
        r = j * NA_QROWS + np.arange(NA_QROWS)
        sr = np.clip(r - WIN_R // 2, 0, rows - WIN_R)
        kr = start + np.arange(NA_KROWS)
        rvalid = (kr[None, :] >= sr[:, None]) & (kr[None, :] < sr[:, None] + WIN_R)
        ri = np.clip(kr[None, :] - r[:, None] + WIN_R - 1, 0, 2 * WIN_R - 2)
        pats.append((ri, rvalid))
    for j in range(2, nblk - 1):
        assert all(np.array_equal(a, b) for a, b in zip(pats[1], pats[j]))
    sel = [pats[0], pats[1], pats[nblk - 1]]
    ri = np.stack([p[0] for p in sel])
    rvalid = np.stack([p[1] for p in sel])
    c = np.arange(GRID_W)
    sc = np.clip(c - WIN_C // 2, 0, GRID_W - WIN_C)
    cvalid = (c[None, :] >= sc[:, None]) & (c[None, :] < sc[:, None] + WIN_C)
    ci = np.clip(c[None, :] - c[:, None] + WIN_C - 1, 0, 2 * WIN_C - 2)
    c_onehot = (ci[..., None] == np.arange(2 * WIN_C - 1)).astype(np.float32)
    valid = rvalid[:, :, None, :, None] & cvalid[None, None, :, None, :]
    return ri, c_onehot, valid


def na_bias_table(rpb, rows):
    ri, c_onehot, valid = _na_bias_index(rows)
    toep = jnp.einsum('hrd,qkd->hrqk', rpb, c_onehot, precision=lax.Precision.HIGHEST)
    b = jnp.take(toep, ri.reshape(-1), axis=1).reshape((NA_HEADS,) + ri.shape + (GRID_W, GRID_W))
    b = jnp.transpose(b, (1, 0, 2, 4, 3, 5))
    b = jnp.where(valid[:, None], b, -jnp.inf)
    return b.reshape(3, NA_HEADS, NA_QROWS * GRID_W, NA_KROWS * GRID_W).astype(F32)


def _attend_heads(q, key_sets, bias_fn):
    n, w = q.shape
    lane = lax.broadcasted_iota(jnp.int32, (1, w), 1)
    out = jnp.zeros((n, w), F32)
    for h in range(NA_HEADS):
        mh = (lane >= h * NA_HEAD_DIM) & (lane < (h + 1) * NA_HEAD_DIM)
        qh = jnp.where(mh, q, 0.0).astype(BF16)
        scores = []
        for i, (k, _) in enumerate(key_sets):
            s = lax.dot_general(qh, k, (((1,), (1,)), ((), ())), preferred_element_type=F32)
            b = bias_fn(i, h)
            scores.append(s if b is None else s + b)
        m = scores[0].max(axis=-1, keepdims=True)
        for s in scores[1:]:
            m = jnp.maximum(m, s.max(axis=-1, keepdims=True))
        denom = jnp.zeros((n, 1), F32)
        acc = jnp.zeros((n, w), F32)
        for s, (_, v) in zip(scores, key_sets):
            p = jnp.exp(s - m)
            denom = denom + p.sum(axis=-1, keepdims=True)
            acc = acc + jnp.dot(p.astype(BF16), v, preferred_element_type=F32)
        out = out + jnp.where(mh, acc / denom, 0.0)
    return out


def _na_kernel(q_ref, k_ref, v_ref, kc_ref, vc_ref, bias_ref, o_ref, *, rows):
    j = pl.program_id(1)
    start = jnp.clip(j * NA_QROWS - WIN_R // 2, 0, rows - NA_KROWS)
    t0 = pl.multiple_of(start * GRID_W, GRID_W)
    nk = NA_KROWS * GRID_W
    kw = k_ref[0, pl.ds(t0, nk), :].astype(BF16)
    vw = v_ref[0, pl.ds(t0, nk), :].astype(BF16)
    kc = kc_ref[0].astype(BF16)
    vc = vc_ref[0].astype(BF16)
    q = q_ref[0] * (1.0 / math.sqrt(NA_HEAD_DIM))
    o_ref[0] = _attend_heads(q, [(kw, vw), (kc, vc)], lambda i, h: bias_ref[0, h] if i == 0 else None)


def na_attention(pr3, prc3, bias):
    bn, S, _ = pr3.shape
    Lc = prc3.shape[1]
    rows = S // GRID_W
    nblk = rows // NA_QROWS
    nq = NA_QROWS * GRID_W

    def pat(b, j):
        return (jnp.where(j == 0, 0, jnp.where(j == nblk - 1, 2, 1)), 0, 0, 0)

    return pl.pallas_call(
        functools.partial(_na_kernel, rows=rows),
        out_shape=jax.ShapeDtypeStruct((bn, S, W_NA), F32),
        grid=(bn, nblk),
        in_specs=[pl.BlockSpec((1, nq, W_NA), lambda b, j: (b, j, Q_COL)),
                  pl.BlockSpec((1, S, W_NA), lambda b, j: (b, 0, K_COL)),
                  pl.BlockSpec((1, S, W_NA), lambda b, j: (b, 0, V_COL)),
                  pl.BlockSpec((1, Lc, W_NA), lambda b, j: (b, 0, K_COL)),
                  pl.BlockSpec((1, Lc, W_NA), lambda b, j: (b, 0, V_COL)),
                  pl.BlockSpec((1,) + bias.shape[1:], pat)],
        out_specs=pl.BlockSpec((1, nq, W_NA), lambda b, j: (b, j, 0)),
        compiler_params=_params("parallel", "arbitrary"),
        name="na_attention",
    )(pr3, pr3, pr3, prc3, prc3, bias)


def _ctx_attn_kernel(q_ref, k_ref, v_ref, o_ref):
    q = q_ref[0] * (1.0 / math.sqrt(NA_HEAD_DIM))
    o_ref[0] = _attend_heads(q, [(k_ref[0].astype(BF16), v_ref[0].astype(BF16))], lambda i, h: None)


def ctx_attention(prc3):
    bn, Lc, _ = prc3.shape
    return pl.pallas_call(
        _ctx_attn_kernel,
        out_shape=jax.ShapeDtypeStruct((bn, Lc, W_NA), F32),
        grid=(bn,),
        in_specs=[pl.BlockSpec((1, Lc, W_NA), lambda b: (b, 0, Q_COL)),
                  pl.BlockSpec((1, Lc, W_NA), lambda b: (b, 0, K_COL)),
                  pl.BlockSpec((1, Lc, W_NA), lambda b: (b, 0, V_COL))],
        out_specs=pl.BlockSpec((1, Lc, W_NA), lambda b: (b, 0, 0)),
        compiler_params=_params("parallel"),
        name="ctx_attention",
    )(prc3, prc3, prc3)


POOL_COL, CONV_H_COL, CONV_B_COL, CONV_C_COL = 0, 1, 2, 3


def _shift_rows(x, k, row):
    n = x.shape[0]
    y = pltpu.roll(x, k % n, 0)
    return jnp.where(row < k, 0.0, y) if k > 0 else jnp.where(row >= n + k, 0.0, y)


def _local_mix_kernel(u_ref, h_ref, bg_ref, cg_ref, pw_ref, ps_ref, cw_ref, op_ref, oc_ref, *, seq):
    row = lax.broadcasted_iota(jnp.int32, (seq, 1), 0)
    lane = lax.broadcasted_iota(jnp.int32, (1, W_POOL), 1)
    u = u_ref[0]
    trailing, leading = {1: u}, {1: u}
    for w in (1, 2, 4):
        trailing[2 * w] = trailing[w] + _shift_rows(trailing[w], w, row)
        leading[2 * w] = leading[w] + _shift_rows(leading[w], -w, row)
    rowf = row.astype(F32)
    win_sum = jnp.zeros_like(u)
    cnt = jnp.zeros_like(u)
    for g, win in enumerate(POOL_WINDOWS):
        half = win // 2
        mg = (lane >= g * POOL_GROUP) & (lane < (g + 1) * POOL_GROUP)
        s = _shift_rows(trailing[half], 1, row) + leading[half]
        n = jnp.minimum(rowf + half, float(seq)) - jnp.maximum(rowf - half, 0.0)
        win_sum = jnp.where(mg, s, win_sum)
        cnt = jnp.where(mg, n, cnt)
    p = win_sum / cnt - u
    op_ref[0] = jnp.dot(p.astype(BF16), pw_ref[...], preferred_element_type=F32) * ps_ref[...]
    v = cg_ref[0] * h_ref[0]
    conv = (_shift_rows(v, 1, row) * cw_ref[0:1, :] + v * cw_ref[1:2, :] + _shift_rows(v, -1, row) * cw_ref[2:3, :])
    oc_ref[0] = bg_ref[0] * conv


def local_mix(pr3, pool_w, pool_scale, conv_w):
    bn, seq, _ = pr3.shape
    pw = jax.scipy.linalg.block_diag(*[pool_w[g] for g in range(len(POOL_WINDOWS))]).astype(BF16)
    blk = lambda col: pl.BlockSpec((1, seq, W_POOL), lambda b: (b, 0, col))
    full = lambda a: pl.BlockSpec(a.shape, lambda b: (0,) * a.ndim)
    args = (pw, pool_scale[None, :], conv_w)
    return pl.pallas_call(
        functools.partial(_local_mix_kernel, seq=seq),
        out_shape=(jax.ShapeDtypeStruct((bn, seq, W_POOL), F32), jax.ShapeDtypeStruct((bn, seq, W_CONV), F32)),
        grid=(bn,),
        in_specs=[blk(POOL_COL), blk(CONV_H_COL), blk(CONV_B_COL), blk(CONV_C_COL)] + [full(a) for a in args],
        out_specs=(pl.BlockSpec((1, seq, W_POOL), lambda b: (b, 0, 0)),
                   pl.BlockSpec((1, seq, W_CONV), lambda b: (b, 0, 0))),
        compiler_params=_params("parallel"),
        name="local_mix",
    )(pr3, pr3, pr3, pr3, *args)


Z_COL, XS_COL, BS_COL, CS_COL = 7, 8, 9, 10
DT_COL = 22
HEADS_PER_GROUP = SSD_HEADS // SSD_GROUPS


def _expand_heads(v, d):
    lane = lax.broadcasted_iota(jnp.int32, (1, W_SSD), 1)
    out = jnp.zeros((v.shape[0], W_SSD), F32)
    for h in range(SSD_HEADS):
        k = d * SSD_HEADS + h
        mh = (lane >= h * SSD_HEAD_DIM) & (lane < (h + 1) * SSD_HEAD_DIM)
        out = jnp.where(mh, v[:, k:k + 1], out)
    return out


def _ssd_chunk(c, d, act_ref, dt_ref, la_ref, y_ref, st_ref, need_y):
    T = SSD_CHUNK
    r0 = pl.multiple_of(c * T, T)
    xs = act_ref[pl.ds(r0, T), 0:W_SSD]
    bm = act_ref[pl.ds(r0, T), W_SSD:2 * W_SSD]
    cm = act_ref[pl.ds(r0, T), 2 * W_SSD:3 * W_SSD]
    dt = dt_ref[pl.ds(r0, T), :]
    la = la_ref[pl.ds(r0, T), :]
    li = lax.broadcasted_iota(jnp.int32, (T, T), 0)
    si = lax.broadcasted_iota(jnp.int32, (T, T), 1)
    mask = (si <= li) if d == 0 else (si >= li)
    acum = jnp.dot(mask.astype(F32), la, precision=lax.Precision.HIGHEST, preferred_element_type=F32)
    tot = acum[T - 1:T, :] if d == 0 else acum[0:1, :]
    acum_e = _expand_heads(acum, d)
    tot_e = _expand_heads(tot, d)
    xdt = xs * _expand_heads(dt, d)
    xw = (xdt * jnp.exp(tot_e - acum_e)).astype(BF16)
    st = st_ref[d]
    bt = bm.T.astype(BF16)
    if need_y:
        acum_t = acum.T
        xdt_b = xdt.astype(BF16)
        lane = lax.broadcasted_iota(jnp.int32, (1, W_SSD), 1)
        y = jnp.zeros((T, W_SSD), F32)
        y_state = []
    new_st = []
    gw = HEADS_PER_GROUP * SSD_HEAD_DIM
    for g in range(SSD_GROUPS):
        bg = bm[:, g * SSD_STATE:(g + 1) * SSD_STATE].astype(BF16)
        cg = cm[:, g * SSD_STATE:(g + 1) * SSD_STATE].astype(BF16)
        st_g = st[:, g * gw:(g + 1) * gw]
        if need_y:
            cb = lax.dot_general(cg, bg, (((1,), (1,)), ((), ())), preferred_element_type=F32)
            for h in range(g * HEADS_PER_GROUP, (g + 1) * HEADS_PER_GROUP):
                k = d * SSD_HEADS + h
                decay = jnp.exp(jnp.where(mask, acum[:, k:k + 1] - acum_t[k:k + 1, :], -jnp.inf))
                mh = (lane >= h * SSD_HEAD_DIM) & (lane < (h + 1) * SSD_HEAD_DIM)
                y = y + jnp.where(mh, jnp.dot((cb * decay).astype(BF16), xdt_b, preferred_element_type=F32), 0.0)
            y_state.append(jnp.dot(cg, st_g.astype(BF16), preferred_element_type=F32))
        upd = jnp.dot(bt[g * SSD_STATE:(g + 1) * SSD_STATE, :], xw[:, g * gw:(g + 1) * gw],
                      preferred_element_type=F32)
        new_st.append(jnp.exp(tot_e[:, g * gw:(g + 1) * gw]) * st_g + upd)
    st_ref[d] = jnp.concatenate(new_st, axis=1)
    if need_y:
        y_ref[pl.ds(r0, T), :] += y + jnp.concatenate(y_state, axis=1) * jnp.exp(acum_e)


def _ssd_kernel(z_ref, xs_ref, bs_ref, cs_ref, dtr_ref, cw_ref, cb_ref, dtb_ref, alog_ref, dsk_ref, ng_ref, h0_ref,
                *refs, seq, need_y):
    if need_y:
        o_ref, hT_ref, act_ref, dt_ref, la_ref, st_ref, y_ref = refs
    else:
        hT_ref, act_ref, dt_ref, la_ref, st_ref = refs
        y_ref = None
    nc = seq // SSD_CHUNK
    row = lax.broadcasted_iota(jnp.int32, (seq, 1), 0)
    for gi, ref in enumerate((xs_ref, bs_ref, cs_ref)):
        sl = slice(gi * W_SSD, (gi + 1) * W_SSD)
        x = ref[0]
        cv = (_shift_rows(x, 1, row) * cw_ref[0:1, sl] + x * cw_ref[1:2, sl]
              + _shift_rows(x, -1, row) * cw_ref[2:3, sl] + cb_ref[:, sl])
        act_ref[:, sl] = cv * jax.nn.sigmoid(cv)
    dtv = dtr_ref[0] + dtb_ref[...]
    dt = jnp.maximum(dtv, 0.0) + jnp.log1p(jnp.exp(-jnp.abs(dtv)))
    dt_ref[...] = dt
    la_ref[...] = dt * (-jnp.exp(alog_ref[...]))
    st_ref[...] = h0_ref[0]
    if need_y:
        y_ref[...] = act_ref[:, 0:W_SSD] * dsk_ref[...]

    def body(i, carry):
        _ssd_chunk(i, 0, act_ref, dt_ref, la_ref, y_ref, st_ref, need_y)
        _ssd_chunk(nc - 1 - i, 1, act_ref, dt_ref, la_ref, y_ref, st_ref, need_y)
        return carry

    lax.fori_loop(0, nc, body, 0, unroll=2)
    hT_ref[0] = st_ref[...]
    if need_y:
        z = z_ref[0]
        yz = y_ref[...] * (z * jax.nn.sigmoid(z))
        ms = jnp.mean(yz * yz, axis=-1, keepdims=True)
        o_ref[0] = yz * lax.rsqrt(ms + EPS) * ng_ref[...]


def ssd_scan(pr3, h0, conv_w, conv_b, dt_bias, a_log, d_skip, norm_g, need_y):
    bn, seq, _ = pr3.shape
    pad = LANES - 2 * SSD_HEADS
    dtb = jnp.pad(dt_bias.reshape(1, -1), ((0, 0), (0, pad)))
    alog = jnp.pad(a_log.reshape(1, -1), ((0, 0), (0, pad)))
    dsk = jnp.repeat(d_skip, SSD_HEAD_DIM)[None, :]
    blk = lambda col, w=W_SSD: pl.BlockSpec((1, seq, w), lambda b: (b, 0, col))
    full = lambda a: pl.BlockSpec(a.shape, lambda b: (0,) * a.ndim)
    st_shape = (1, 2, SSD_STATE, W_SSD)
    st_spec = pl.BlockSpec(st_shape, lambda b: (b, 0, 0, 0))
    args = (conv_w, conv_b[None, :], dtb, alog, dsk, norm_g[None, :])
    out_shape = [jax.ShapeDtypeStruct((bn,) + st_shape[1:], F32)]
    out_specs = [st_spec]
    scratch = [pltpu.VMEM((seq, 3 * W_SSD), F32), pltpu.VMEM((seq, LANES), F32), pltpu.VMEM((seq, LANES), F32),
               pltpu.VMEM(st_shape[1:], F32)]
    if need_y:
        out_shape.insert(0, jax.ShapeDtypeStruct((bn, seq, W_SSD), F32))
        out_specs.insert(0, pl.BlockSpec((1, seq, W_SSD), lambda b: (b, 0, 0)))
        scratch.append(pltpu.VMEM((seq, W_SSD), F32))
    res = pl.pallas_call(
        functools.partial(_ssd_kernel, seq=seq, need_y=need_y),
        out_shape=out_shape,
        grid=(bn,),
        in_specs=[blk(Z_COL), blk(XS_COL), blk(BS_COL), blk(CS_COL), blk(DT_COL, LANES)]
                 + [full(a) for a in args] + [st_spec],
        out_specs=out_specs,
        scratch_shapes=scratch,
        compiler_params=_params("parallel"),
        name="ssd_scan",
    )(pr3, pr3, pr3, pr3, pr3, *args, h0)
    return (res[0], res[1]) if need_y else (None, res[0])


def ssd_mix(pr3, prc3, conv_w, conv_b, dt_bias, a_log, d_skip, norm_g, ctx_out):
    bn = pr3.shape[0]
    h0 = jnp.zeros((bn, 2, SSD_STATE, W_SSD), F32)
    oc, hc = ssd_scan(prc3, h0, conv_w, conv_b, dt_bias, a_log, d_skip, norm_g, ctx_out)
    o, _ = ssd_scan(pr3, hc, conv_w, conv_b, dt_bias, a_log, d_skip, norm_g, True)
    return o, oc


def mixer(pr3, prc3, pool_w, pool_scale, conv_w, rpb, s_conv_w, s_conv_b, dt_bias, a_log, d_skip, s_norm_g,
          ctx_out):
    bn, S, _ = pr3.shape
    o_ssd, oc_ssd = ssd_mix(pr3, prc3, s_conv_w, s_conv_b, dt_bias, a_log, d_skip, s_norm_g, ctx_out)
    o = [*local_mix(pr3, pool_w, pool_scale, conv_w),
         na_attention(pr3, prc3, na_bias_table(rpb, S // GRID_W)), o_ssd]
    if not ctx_out:
        return o, None
    oc = [*local_mix(prc3, pool_w, pool_scale, conv_w), ctx_attention(prc3), oc_ssd]
    return o, oc


TM = 512
TM_CTX = 256
TM_FFN = 1024
TM_GRP = 1024


def kernel(x, c, ctx, c_ctx, w_ada, b_ada, g_mix, g_ffn, w_in, w_out, pool_w, pool_scale, conv_w, na_rpb,
           ssd_conv_w, ssd_conv_b, ssd_dt_bias, ssd_a_log, ssd_d, ssd_norm_g, ffn_w_gu, ffn_w_down,
           moe_router, moe_w_gu, moe_w_down, g_final):
    bn, S, d = x.shape
    Lc = ctx.shape[1]
    m, mc = bn * S, bn * Lc
    tpb = S // TM
    x2 = x.reshape(m, d)
    xc2 = ctx.reshape(mc, d)
    c_rows = jnp.concatenate([c, c_ctx[None, :], jnp.zeros((7, d), F32)], axis=0)
    for l in range(DEPTH):
        ctx_out = l < DEPTH - 1
        last = l == DEPTH - 1
        ada = ada_modulation(c_rows, w_ada[l].astype(BF16), b_ada[l][None, :])
        mod = ada[:bn].reshape(bn, 6, d)
        mod_c = ada[bn:bn + 1].reshape(1, 6, d)
        w_in_l = jnp.pad(w_in[l], ((0, 0), (0, D_IN_PAD - D_IN_PROJ))).astype(BF16)
        g_m = g_mix[l][None, :]
        g_f = g_ffn[l][None, :]
        pr = in_proj(x2, g_m, mod, w_in_l, tpb, TM)
        pr_c = in_proj(xc2, g_m, mod_c, w_in_l, None, TM_CTX)
        o, oc = mixer(pr.reshape(bn, S, -1), pr_c.reshape(bn, Lc, -1), pool_w[l], pool_scale[l], conv_w[l],
                      na_rpb[l], ssd_conv_w[l], ssd_conv_b[l], ssd_dt_bias[l], ssd_a_log[l], ssd_d[l],
                      ssd_norm_g[l], ctx_out)
        w_out_l = w_out[l].astype(BF16)
        x2 = out_proj(x2, [p.reshape(m, -1) for p in o], mod, w_out_l, tpb, TM)
        if ctx_out:
            xc2 = out_proj(xc2, [p.reshape(mc, -1) for p in oc], mod_c, w_out_l, None, TM_CTX)
        j = l // 2
        if l % 2 == 0:
            w_gu = ffn_w_gu[j].astype(BF16)
            w_dn = ffn_w_down[j].astype(BF16)
            x2 = ffn_dense(x2, g_f, mod, w_gu, w_dn, S // TM_FFN, TM_FFN)
            if ctx_out:
                xc2 = ffn_dense(xc2, g_f, mod_c, w_gu, w_dn, None, min(TM_FFN, mc))
        else:
            w_r = jnp.pad(moe_router[j], ((0, 0), (0, LANES - N_EXPERTS))).astype(BF16)
            w_gu = moe_w_gu[j].astype(BF16)
            w_dn = moe_w_down[j].astype(BF16)
            x2 = moe_block(x2, g_f, mod, w_r, w_gu, w_dn, tpb, TM, TM_GRP, g_final[None, :] if last else None)
            if ctx_out:
                xc2 = moe_block(xc2, g_f, mod_c, w_r, w_gu, w_dn, None, TM_CTX, TM_GRP)
            if last:
                return x2.reshape(bn, S, d)
    return final_norm(x2, g_final[None, :], TM).reshape(bn, S, d)
```

```python
import functools
import math

import numpy as np
import jax
import jax.numpy as jnp
from jax import lax
from jax.experimental import pallas as pl
from jax.experimental.pallas import tpu as pltpu

D_MODEL = 1024
DEPTH = 2
GRID_W = 64
EPS = 1e-6
W_POOL = 256
W_CONV = 256
W_NA = 256
W_SSD = 256
POOL_WINDOWS = (2, 4, 8, 16)
POOL_GROUP = W_POOL // len(POOL_WINDOWS)
NA_HEADS = 4
NA_HEAD_DIM = W_NA // NA_HEADS
WIN_R = 8
WIN_C = 16
COL_BLOCK = 16
COL_BAND = 32
SSD_HEAD_DIM = 64
SSD_HEADS = W_SSD // SSD_HEAD_DIM
SSD_GROUPS = 2
SSD_STATE = 128
SSD_CHUNK = 128
SSD_XBC = W_SSD + 2 * SSD_GROUPS * SSD_STATE
D_IN_PROJ = W_POOL + 3 * W_CONV + 3 * W_NA + W_SSD + SSD_XBC + 2 * SSD_HEADS
PROJ_SPLITS = (W_POOL, W_POOL + 3 * W_CONV, W_POOL + 3 * W_CONV + 3 * W_NA)
D_FF = 2816
N_EXPERTS = 8
TOP_K = 2
D_FF_EXPERT = 1408

LANES = 128
D_IN_PAD = -(-D_IN_PROJ // LANES) * LANES
VMEM_LIMIT = 56 * 1024 * 1024
BF16 = jnp.bfloat16
F32 = jnp.float32


def _params(*sem):
    return pltpu.CompilerParams(dimension_semantics=sem, vmem_limit_bytes=VMEM_LIMIT)


def _norm_mod(x, g, scale, shift):
    ms = jnp.mean(x * x, axis=-1, keepdims=True)
    return (x * lax.rsqrt(ms + EPS) * g) * (1.0 + scale) + shift


def _mod_map(tiles_per_batch):
    if tiles_per_batch is None:
        return lambda i, *_: (0, 0, 0)
    return lambda i, *_: (i // tiles_per_batch, 0, 0)


def _ada_kernel(c_ref, w_ref, b_ref, o_ref):
    c = c_ref[...]
    s = c * jax.nn.sigmoid(c)
    o_ref[...] = jnp.dot(s.astype(BF16), w_ref[...], preferred_element_type=F32) + b_ref[...]


def ada_modulation(c_rows, w, b):
    r, d = c_rows.shape
    n = w.shape[1]
    tn = 1536
    return pl.pallas_call(
        _ada_kernel,
        out_shape=jax.ShapeDtypeStruct((r, n), F32),
        grid=(n // tn,),
        in_specs=[pl.BlockSpec((r, d), lambda j: (0, 0)),
                  pl.BlockSpec((d, tn), lambda j: (0, j)),
                  pl.BlockSpec((1, tn), lambda j: (0, j))],
        out_specs=pl.BlockSpec((r, tn), lambda j: (0, j)),
        compiler_params=_params("arbitrary"),
        name="ada_modulation",
    )(c_rows, w, b)


def _in_proj_kernel(x_ref, g_ref, mod_ref, w_ref, o_ref):
    h = _norm_mod(x_ref[...], g_ref[...], mod_ref[0, 1:2, :], mod_ref[0, 0:1, :])
    o_ref[...] = jnp.dot(h.astype(BF16), w_ref[...], preferred_element_type=F32).astype(o_ref.dtype)


def in_proj(x2, g, mod, w, tiles_per_batch, tm, out_dtype=F32):
    m, d = x2.shape
    n = w.shape[1]
    return pl.pallas_call(
        _in_proj_kernel,
        out_shape=jax.ShapeDtypeStruct((m, n), out_dtype),
        grid=(m // tm,),
        in_specs=[pl.BlockSpec((tm, d), lambda i: (i, 0)),
                  pl.BlockSpec((1, d), lambda i: (0, 0)),
                  pl.BlockSpec((1, 6, d), _mod_map(tiles_per_batch)),
                  pl.BlockSpec((d, n), lambda i: (0, 0))],
        out_specs=pl.BlockSpec((tm, n), lambda i: (i, 0)),
        compiler_params=_params("parallel"),
        name="in_proj",
    )(x2, g, mod, w)


def _out_proj_kernel(x_ref, mod_ref, w_ref, *refs):
    part_refs, y_ref = refs[:-1], refs[-1]
    y = jnp.zeros(y_ref.shape, F32)
    k0 = 0
    for p_ref in part_refs:
        k = p_ref.shape[1]
        y = y + jnp.dot(p_ref[...].astype(BF16), w_ref[k0:k0 + k, :], preferred_element_type=F32)
        k0 += k
    y_ref[...] = x_ref[...] + mod_ref[0, 2:3, :] * y


def out_proj(x2, parts, mod, w, tiles_per_batch, tm):
    m, d = x2.shape
    return pl.pallas_call(
        _out_proj_kernel,
        out_shape=jax.ShapeDtypeStruct((m, d), F32),
        grid=(m // tm,),
        in_specs=[pl.BlockSpec((tm, d), lambda i: (i, 0)),
                  pl.BlockSpec((1, 6, d), _mod_map(tiles_per_batch)),
                  pl.BlockSpec(w.shape, lambda i: (0, 0))]
                 + [pl.BlockSpec((tm, p.shape[1]), lambda i: (i, 0)) for p in parts],
        out_specs=pl.BlockSpec((tm, d), lambda i: (i, 0)),
        compiler_params=_params("parallel"),
        name="out_proj",
    )(x2, mod, w, *parts)


FF_CHUNK = 512


def _swiglu(h, wgu, wd, ff):
    acc = None
    for c0 in range(0, ff, FF_CHUNK):
        c1 = min(c0 + FF_CHUNK, ff)
        gg = jnp.dot(h, wgu(c0, c1), preferred_element_type=F32)
        uu = jnp.dot(h, wgu(ff + c0, ff + c1), preferred_element_type=F32)
        a = (gg * jax.nn.sigmoid(gg) * uu).astype(BF16)
        part = jnp.dot(a, wd(c0, c1), preferred_element_type=F32)
        acc = part if acc is None else acc + part
    return acc


def _ffn_kernel(x_ref, g_ref, mod_ref, wgu_ref, wd_ref, y_ref):
    x = x_ref[...]
    h = _norm_mod(x, g_ref[...], mod_ref[0, 4:5, :], mod_ref[0, 3:4, :]).astype(BF16)
    y = _swiglu(h, lambda a, b: wgu_ref[:, a:b], lambda a, b: wd_ref[a:b, :], wd_ref.shape[0])
    y_ref[...] = x + mod_ref[0, 5:6, :] * y


def _resident(shape, index_map):
    return pl.BlockSpec(shape, index_map, pipeline_mode=pl.Buffered(1))


def ffn_dense(x2, g, mod, w_gu, w_down, tiles_per_batch, tm):
    m, d = x2.shape
    return pl.pallas_call(
        _ffn_kernel,
        out_shape=jax.ShapeDtypeStruct((m, d), F32),
        grid=(m // tm,),
        in_specs=[pl.BlockSpec((tm, d), lambda i: (i, 0)),
                  pl.BlockSpec((1, d), lambda i: (0, 0)),
                  pl.BlockSpec((1, 6, d), _mod_map(tiles_per_batch)),
                  _resident(w_gu.shape, lambda i: (0, 0)),
                  _resident(w_down.shape, lambda i: (0, 0))],
        out_specs=pl.BlockSpec((tm, d), lambda i: (i, 0)),
        compiler_params=_params("parallel"),
        name="ffn_dense",
    )(x2, g, mod, w_gu, w_down)


ROUTE_E1, ROUTE_E2, ROUTE_R1, ROUTE_R2, ROUTE_G1, ROUTE_G2 = range(6)


def _router_kernel(x_ref, g_ref, mod_ref, wr_ref, h_ref, route_ref, cnt_ref, base_ref):
    @pl.when(pl.program_id(0) == 0)
    def _():
        base_ref[...] = jnp.zeros_like(base_ref)

    h = _norm_mod(x_ref[...], g_ref[...], mod_ref[0, 4:5, :], mod_ref[0, 3:4, :]).astype(BF16)
    h_ref[...] = h
    tm = h.shape[0]
    logits = jnp.dot(h, wr_ref[...], preferred_element_type=F32)
    lane = lax.broadcasted_iota(jnp.int32, logits.shape, 1)
    neg = jnp.float32(-jnp.inf)
    logits = jnp.where(lane < N_EXPERTS, logits, neg)
    m1 = jnp.max(logits, axis=-1, keepdims=True)
    i1 = jnp.min(jnp.where(logits == m1, lane, LANES), axis=-1, keepdims=True)
    rest = jnp.where(lane == i1, neg, logits)
    m2 = jnp.max(rest, axis=-1, keepdims=True)
    i2 = jnp.min(jnp.where(rest == m2, lane, LANES), axis=-1, keepdims=True)
    e2 = jnp.exp(m2 - m1)
    g1 = 1.0 / (1.0 + e2)
    g2 = e2 / (1.0 + e2)
    sel1, sel2 = lane == i1, lane == i2
    sel = jnp.where(sel1 | sel2, 1.0, 0.0)
    li = lax.broadcasted_iota(jnp.int32, (tm, tm), 0)
    si = lax.broadcasted_iota(jnp.int32, (tm, tm), 1)
    before = jnp.where(si < li, 1.0, 0.0).astype(BF16)
    rank = jnp.dot(before, sel.astype(BF16), preferred_element_type=F32) + base_ref[...]
    r1 = jnp.sum(jnp.where(sel1, rank, 0.0), axis=-1, keepdims=True)
    r2 = jnp.sum(jnp.where(sel2, rank, 0.0), axis=-1, keepdims=True)
    base_ref[...] += jnp.sum(sel, axis=0, keepdims=True)
    cnt_ref[...] = base_ref[...]
    fields = (i1.astype(F32), i2.astype(F32), r1, r2, g1, g2)
    route = jnp.zeros(logits.shape, F32)
    for k, v in enumerate(fields):
        route = jnp.where(lane == k, v, route)
    route_ref[...] = route


def moe_router(x2, g, mod, w_router, tiles_per_batch, tm):
    m, d = x2.shape
    return pl.pallas_call(
        _router_kernel,
        out_shape=(jax.ShapeDtypeStruct((m, d), BF16), jax.ShapeDtypeStruct((m, LANES), F32),
                   jax.ShapeDtypeStruct((1, LANES), F32)),
        grid=(m // tm,),
        in_specs=[pl.BlockSpec((tm, d), lambda i: (i, 0)),
                  pl.BlockSpec((1, d), lambda i: (0, 0)),
                  pl.BlockSpec((1, 6, d), _mod_map(tiles_per_batch)),
                  pl.BlockSpec((d, LANES), lambda i: (0, 0))],
        out_specs=(pl.BlockSpec((tm, d), lambda i: (i, 0)),
                   pl.BlockSpec((tm, LANES), lambda i: (i, 0)),
                   pl.BlockSpec((1, LANES), lambda i: (0, 0))),
        scratch_shapes=[pltpu.VMEM((1, LANES), F32)],
        compiler_params=_params("arbitrary"),
        name="moe_router",
    )(x2, g, mod, w_router)


def _moe_kernel(tile_ref, exp_ref, start_ref, end_ref, xg_ref, wgu_ref, wd_ref, y_ref):
    w = pl.program_id(0)
    tm = xg_ref.shape[0]
    start, end = start_ref[w], end_ref[w]
    t0 = tile_ref[w] * tm

    @pl.when(end > start)
    def _():
        y = _swiglu(xg_ref[...], lambda a, b: wgu_ref[0, :, a:b], lambda a, b: wd_ref[0, a:b, :], wd_ref.shape[1])
        y = y.astype(y_ref.dtype)

        @pl.when(start == t0)
        def _():
            y_ref[...] = y

        @pl.when(start != t0)
        def _():
            row = t0 + lax.broadcasted_iota(jnp.int32, (tm, 1), 0)
            y_ref[...] = jnp.where(row >= start, y, y_ref[...])


def moe_grouped(item_tile, item_expert, item_start, item_end, xg, w_gu, w_down, tm):
    p, d = xg.shape
    grid_spec = pltpu.PrefetchScalarGridSpec(
        num_scalar_prefetch=4,
        grid=(item_tile.shape[0],),
        in_specs=[pl.BlockSpec((tm, d), lambda w, it, ie, s, e: (it[w], 0)),
                  _resident((1,) + w_gu.shape[1:], lambda w, it, ie, s, e: (ie[w], 0, 0)),
                  _resident((1,) + w_down.shape[1:], lambda w, it, ie, s, e: (ie[w], 0, 0))],
        out_specs=pl.BlockSpec((tm, d), lambda w, it, ie, s, e: (it[w], 0)),
    )
    return pl.pallas_call(
        _moe_kernel,
        out_shape=jax.ShapeDtypeStruct((p, d), BF16),
        grid_spec=grid_spec,
        compiler_params=_params("arbitrary"),
        name="moe_grouped",
    )(item_tile, item_expert, item_start, item_end, xg, w_gu, w_down)


def _moe_combine_kernel(x_ref, ya_ref, yb_ref, route_ref, mod_ref, *refs):
    o_ref = refs[-1]
    r = route_ref[...]
    y = (r[:, ROUTE_G1:ROUTE_G1 + 1] * ya_ref[...].astype(F32) + r[:, ROUTE_G2:ROUTE_G2 + 1] * yb_ref[...].astype(F32))
    x = x_ref[...] + mod_ref[0, 5:6, :] * y
    if len(refs) == 2:
        ms = jnp.mean(x * x, axis=-1, keepdims=True)
        x = x * lax.rsqrt(ms + EPS) * refs[0][...]
    o_ref[...] = x


def moe_combine(x2, ya, yb, route, mod, tiles_per_batch, tm, g_final=None):
    m, d = x2.shape
    row = pl.BlockSpec((tm, d), lambda i: (i, 0))
    specs = [row, row, row, pl.BlockSpec((tm, LANES), lambda i: (i, 0)),
             pl.BlockSpec((1, 6, d), _mod_map(tiles_per_batch))]
    args = [x2, ya, yb, route, mod]
    if g_final is not None:
        specs.append(pl.BlockSpec((1, d), lambda i: (0, 0)))
        args.append(g_final)
    return pl.pallas_call(
        _moe_combine_kernel,
        out_shape=jax.ShapeDtypeStruct((m, d), F32),
        grid=(m // tm,),
        in_specs=specs,
        out_specs=row,
        compiler_params=_params("parallel"),
        name="moe_combine",
    )(*args)


def moe_block(x2, g, mod, w_router, w_gu, w_down, tiles_per_batch, tm_tok, tm_grp, g_final=None):
    m, d = x2.shape
    h, route, cnt = moe_router(x2, g, mod, w_router, tiles_per_batch, tm_tok)
    cnt = cnt[0, :N_EXPERTS].astype(jnp.int32)
    p = m * TOP_K
    off = jnp.cumsum(cnt) - cnt
    ri = route[:, :ROUTE_G1].astype(jnp.int32)
    onehot = lambda e: (e[:, None] == jnp.arange(N_EXPERTS)[None, :]).astype(jnp.int32)
    d1 = jnp.sum(onehot(ri[:, ROUTE_E1]) * off[None, :], axis=1) + ri[:, ROUTE_R1]
    d2 = jnp.sum(onehot(ri[:, ROUTE_E2]) * off[None, :], axis=1) + ri[:, ROUTE_R2]
    tok = jnp.arange(m, dtype=jnp.int32)
    _, src = lax.sort_key_val(jnp.concatenate([d1, d2]), jnp.concatenate([tok, tok]))
    n_row_tiles = p // tm_grp
    start = jnp.sort(jnp.concatenate([jnp.arange(n_row_tiles, dtype=jnp.int32) * tm_grp, off]))
    end = jnp.concatenate([start[1:], jnp.full((1,), p, jnp.int32)])
    item_tile = jnp.minimum(start // tm_grp, n_row_tiles - 1)
    item_expert = jnp.sum((off[None, :] <= start[:, None]).astype(jnp.int32), axis=1) - 1
    rows = lambda a, idx: a.at[idx].get(mode="promise_in_bounds")
    yg = moe_grouped(item_tile, item_expert, start, end, rows(h, src), w_gu, w_down, tm_grp)
    return moe_combine(x2, rows(yg, d1), rows(yg, d2), route, mod, tiles_per_batch, tm_tok, g_final)


def _final_norm_kernel(x_ref, g_ref, o_ref):
    x = x_ref[...]
    ms = jnp.mean(x * x, axis=-1, keepdims=True)
    o_ref[...] = x * lax.rsqrt(ms + EPS) * g_ref[...]


def final_norm(x2, g, tm):
    m, d = x2.shape
    return pl.pallas_call(
        _final_norm_kernel,
        out_shape=jax.ShapeDtypeStruct((m, d), F32),
        grid=(m // tm,),
        in_specs=[pl.BlockSpec((tm, d), lambda i: (i, 0)), pl.BlockSpec((1, d), lambda i: (0, 0))],
        out_specs=pl.BlockSpec((tm, d), lambda i: (i, 0)),
        compiler_params=_params("parallel"),
        name="final_norm",
    )(x2, g)


NA_QROWS = 4
NA_KROWS = NA_QROWS + WIN_R
Q_COL, K_COL, V_COL = 4, 5, 6


def _na_key_start(j, rows):
    return np.clip(j * NA_QROWS - WIN_R // 2, 0, rows - NA_KROWS)


def _na_bias_index(rows):
    nblk = rows // NA_QROWS
    pats = []
    for j in range(nblk):
        start = _na_key_start(j, rows)
        r = j * NA_QROWS + np.arange(NA_QROWS)
        sr = np.clip(r - WIN_R // 2, 0, rows - WIN_R)
        kr = start + np.arange(NA_KROWS)
        rvalid = (kr[None, :] >= sr[:, None]) & (kr[None, :] < sr[:, None] + WIN_R)
        ri = np.clip(kr[None, :] - r[:, None] + WIN_R - 1, 0, 2 * WIN_R - 2)
        pats.append((ri, rvalid))
    for j in range(2, nblk - 1):
        assert all(np.array_equal(a, b) for a, b in zip(pats[1], pats[j]))
    sel = [pats[0], pats[1], pats[nblk - 1]]
    ri = np.stack([p[0] for p in sel])
    rvalid = np.stack([p[1] for p in sel])
    c = np.arange(GRID_W)
    sc = np.clip(c - WIN_C // 2, 0, GRID_W - WIN_C)
    cvalid = (c[None, :] >= sc[:, None]) & (c[None, :] < sc[:, None] + WIN_C)
    ci = np.clip(c[None, :] - c[:, None] + WIN_C - 1, 0, 2 * WIN_C - 2)
    c_onehot = (ci[..., None] == np.arange(2 * WIN_C - 1)).astype(np.float32)
    valid = rvalid[:, :, None, :, None] & cvalid[None, None, :, None, :]
    return ri, c_onehot, valid


def na_bias_table(rpb, rows):
    ri, c_onehot, valid = _na_bias_index(rows)
    toep = jnp.einsum('hrd,qkd->hrqk', rpb, c_onehot, precision=lax.Precision.HIGHEST)
    b = jnp.take(toep, ri.reshape(-1), axis=1).reshape((NA_HEADS,) + ri.shape + (GRID_W, GRID_W))
    b = jnp.transpose(b, (1, 0, 2, 4, 3, 5))
    b = jnp.where(valid[:, None], b, -jnp.inf)
    return b.reshape(3, NA_HEADS, NA_QROWS * GRID_W, NA_KROWS * GRID_W).astype(F32)


def _attend_heads(q, key_sets, bias_fn):
    n, w = q.shape
    lane = lax.broadcasted_iota(jnp.int32, (1, w), 1)
    out = jnp.zeros((n, w), F32)
    for h in range(NA_HEADS):
        mh = (lane >= h * NA_HEAD_DIM) & (lane < (h + 1) * NA_HEAD_DIM)
        qh = jnp.where(mh, q, 0.0).astype(BF16)
        scores = []
        for i, (k, _) in enumerate(key_sets):
            s = lax.dot_general(qh, k, (((1,), (1,)), ((), ())), preferred_element_type=F32)
            b = bias_fn(i, h)
            scores.append(s if b is None else s + b)
        m = scores[0].max(axis=-1, keepdims=True)
        for s in scores[1:]:
            m = jnp.maximum(m, s.max(axis=-1, keepdims=True))
        denom = jnp.zeros((n, 1), F32)
        acc = jnp.zeros((n, w), F32)
        for s, (_, v) in zip(scores, key_sets):
            p = jnp.exp(s - m)
            denom = denom + p.sum(axis=-1, keepdims=True)
            acc = acc + jnp.dot(p.astype(BF16), v, preferred_element_type=F32)
        out = out + jnp.where(mh, acc / denom, 0.0)
    return out


def _na_kernel(q_ref, k_ref, v_ref, kc_ref, vc_ref, bias_ref, o_ref, *, rows):
    j = pl.program_id(1)
    start = jnp.clip(j * NA_QROWS - WIN_R // 2, 0, rows - NA_KROWS)
    t0 = pl.multiple_of(start * GRID_W, GRID_W)
    nk = NA_KROWS * GRID_W
    kw = k_ref[0, pl.ds(t0, nk), :].astype(BF16)
    vw = v_ref[0, pl.ds(t0, nk), :].astype(BF16)
    kc = kc_ref[0].astype(BF16)
    vc = vc_ref[0].astype(BF16)
    q = q_ref[0] * (1.0 / math.sqrt(NA_HEAD_DIM))
    o_ref[0] = _attend_heads(q, [(kw, vw), (kc, vc)], lambda i, h: bias_ref[0, h] if i == 0 else None)


def na_attention(pr3, prc3, bias):
    bn, S, _ = pr3.shape
    Lc = prc3.shape[1]
    rows = S // GRID_W
    nblk = rows // NA_QROWS
    nq = NA_QROWS * GRID_W

    def pat(b, j):
        return (jnp.where(j == 0, 0, jnp.where(j == nblk - 1, 2, 1)), 0, 0, 0)

    return pl.pallas_call(
        functools.partial(_na_kernel, rows=rows),
        out_shape=jax.ShapeDtypeStruct((bn, S, W_NA), F32),
        grid=(bn, nblk),
        in_specs=[pl.BlockSpec((1, nq, W_NA), lambda b, j: (b, j, Q_COL)),
                  pl.BlockSpec((1, S, W_NA), lambda b, j: (b, 0, K_COL)),
                  pl.BlockSpec((1, S, W_NA), lambda b, j: (b, 0, V_COL)),
                  pl.BlockSpec((1, Lc, W_NA), lambda b, j: (b, 0, K_COL)),
                  pl.BlockSpec((1, Lc, W_NA), lambda b, j: (b, 0, V_COL)),
                  pl.BlockSpec((1,) + bias.shape[1:], pat)],
        out_specs=pl.BlockSpec((1, nq, W_NA), lambda b, j: (b, j, 0)),
        compiler_params=_params("parallel", "arbitrary"),
        name="na_attention",
    )(pr3, pr3, pr3, prc3, prc3, bias)


def _ctx_attn_kernel(q_ref, k_ref, v_ref, o_ref):
    q = q_ref[0] * (1.0 / math.sqrt(NA_HEAD_DIM))
    o_ref[0] = _attend_heads(q, [(k_ref[0].astype(BF16), v_ref[0].astype(BF16))], lambda i, h: None)


def ctx_attention(prc3):
    bn, Lc, _ = prc3.shape
    return pl.pallas_call(
        _ctx_attn_kernel,
        out_shape=jax.ShapeDtypeStruct((bn, Lc, W_NA), F32),
        grid=(bn,),
        in_specs=[pl.BlockSpec((1, Lc, W_NA), lambda b: (b, 0, Q_COL)),
                  pl.BlockSpec((1, Lc, W_NA), lambda b: (b, 0, K_COL)),
                  pl.BlockSpec((1, Lc, W_NA), lambda b: (b, 0, V_COL))],
        out_specs=pl.BlockSpec((1, Lc, W_NA), lambda b: (b, 0, 0)),
        compiler_params=_params("parallel"),
        name="ctx_attention",
    )(prc3, prc3, prc3)


POOL_COL, CONV_H_COL, CONV_B_COL, CONV_C_COL = 0, 1, 2, 3


def _shift_rows(x, k, row):
    n = x.shape[0]
    y = pltpu.roll(x, k % n, 0)
    return jnp.where(row < k, 0.0, y) if k > 0 else jnp.where(row >= n + k, 0.0, y)


def _local_mix_kernel(u_ref, h_ref, bg_ref, cg_ref, pw_ref, ps_ref, cw_ref, op_ref, oc_ref, *, seq):
    row = lax.broadcasted_iota(jnp.int32, (seq, 1), 0)
    lane = lax.broadcasted_iota(jnp.int32, (1, W_POOL), 1)
    u = u_ref[0]
    trailing, leading = {1: u}, {1: u}
    for w in (1, 2, 4):
        trailing[2 * w] = trailing[w] + _shift_rows(trailing[w], w, row)
        leading[2 * w] = leading[w] + _shift_rows(leading[w], -w, row)
    rowf = row.astype(F32)
    win_sum = jnp.zeros_like(u)
    cnt = jnp.zeros_like(u)
    for g, win in enumerate(POOL_WINDOWS):
        half = win // 2
        mg = (lane >= g * POOL_GROUP) & (lane < (g + 1) * POOL_GROUP)
        s = _shift_rows(trailing[half], 1, row) + leading[half]
        n = jnp.minimum(rowf + half, float(seq)) - jnp.maximum(rowf - half, 0.0)
        win_sum = jnp.where(mg, s, win_sum)
        cnt = jnp.where(mg, n, cnt)
    p = win_sum / cnt - u
    op_ref[0] = jnp.dot(p.astype(BF16), pw_ref[...], preferred_element_type=F32) * ps_ref[...]
    v = cg_ref[0] * h_ref[0]
    conv = (_shift_rows(v, 1, row) * cw_ref[0:1, :] + v * cw_ref[1:2, :] + _shift_rows(v, -1, row) * cw_ref[2:3, :])
    oc_ref[0] = bg_ref[0] * conv


def local_mix(pr3, pool_w, pool_scale, conv_w):
    bn, seq, _ = pr3.shape
    pw = jax.scipy.linalg.block_diag(*[pool_w[g] for g in range(len(POOL_WINDOWS))]).astype(BF16)
    blk = lambda col: pl.BlockSpec((1, seq, W_POOL), lambda b: (b, 0, col))
    full = lambda a: pl.BlockSpec(a.shape, lambda b: (0,) * a.ndim)
    args = (pw, pool_scale[None, :], conv_w)
    return pl.pallas_call(
        functools.partial(_local_mix_kernel, seq=seq),
        out_shape=(jax.ShapeDtypeStruct((bn, seq, W_POOL), F32), jax.ShapeDtypeStruct((bn, seq, W_CONV), F32)),
        grid=(bn,),
        in_specs=[blk(POOL_COL), blk(CONV_H_COL), blk(CONV_B_COL), blk(CONV_C_COL)] + [full(a) for a in args],
        out_specs=(pl.BlockSpec((1, seq, W_POOL), lambda b: (b, 0, 0)),
                   pl.BlockSpec((1, seq, W_CONV), lambda b: (b, 0, 0))),
        compiler_params=_params("parallel"),
        name="local_mix",
    )(pr3, pr3, pr3, pr3, *args)


Z_COL, XS_COL, BS_COL, CS_COL = 7, 8, 9, 10
DT_COL = 22
HEADS_PER_GROUP = SSD_HEADS // SSD_GROUPS


def _expand_heads(v, d):
    lane = lax.broadcasted_iota(jnp.int32, (1, W_SSD), 1)
    out = jnp.zeros((v.shape[0], W_SSD), F32)
    for h in range(SSD_HEADS):
        k = d * SSD_HEADS + h
        mh = (lane >= h * SSD_HEAD_DIM) & (lane < (h + 1) * SSD_HEAD_DIM)
        out = jnp.where(mh, v[:, k:k + 1], out)
    return out


def _ssd_chunk(c, d, act_ref, dt_ref, la_ref, y_ref, st_ref, need_y):
    T = SSD_CHUNK
    r0 = pl.multiple_of(c * T, T)
    xs = act_ref[pl.ds(r0, T), 0:W_SSD]
    bm = act_ref[pl.ds(r0, T), W_SSD:2 * W_SSD]
    cm = act_ref[pl.ds(r0, T), 2 * W_SSD:3 * W_SSD]
    dt = dt_ref[pl.ds(r0, T), :]
    la = la_ref[pl.ds(r0, T), :]
    li = lax.broadcasted_iota(jnp.int32, (T, T), 0)
    si = lax.broadcasted_iota(jnp.int32, (T, T), 1)
    mask = (si <= li) if d == 0 else (si >= li)
    acum = jnp.dot(mask.astype(F32), la, precision=lax.Precision.HIGHEST, preferred_element_type=F32)
    tot = acum[T - 1:T, :] if d == 0 else acum[0:1, :]
    acum_e = _expand_heads(acum, d)
    tot_e = _expand_heads(tot, d)
    xdt = xs * _expand_heads(dt, d)
    xw = (xdt * jnp.exp(tot_e - acum_e)).astype(BF16)
    st = st_ref[d]
    bt = bm.T.astype(BF16)
    if need_y:
        acum_t = acum.T
        xdt_b = xdt.astype(BF16)
        lane = lax.broadcasted_iota(jnp.int32, (1, W_SSD), 1)
        y = jnp.zeros((T, W_SSD), F32)
        y_state = []
    new_st = []
    gw = HEADS_PER_GROUP * SSD_HEAD_DIM
    for g in range(SSD_GROUPS):
        bg = bm[:, g * SSD_STATE:(g + 1) * SSD_STATE].astype(BF16)
        cg = cm[:, g * SSD_STATE:(g + 1) * SSD_STATE].astype(BF16)
        st_g = st[:, g * gw:(g + 1) * gw]
        if need_y:
            cb = lax.dot_general(cg, bg, (((1,), (1,)), ((), ())), preferred_element_type=F32)
            for h in range(g * HEADS_PER_GROUP, (g + 1) * HEADS_PER_GROUP):
                k = d * SSD_HEADS + h
                decay = jnp.exp(jnp.where(mask, acum[:, k:k + 1] - acum_t[k:k + 1, :], -jnp.inf))
                mh = (lane >= h * SSD_HEAD_DIM) & (lane < (h + 1) * SSD_HEAD_DIM)
                y = y + jnp.where(mh, jnp.dot((cb * decay).astype(BF16), xdt_b, preferred_element_type=F32), 0.0)
            y_state.append(jnp.dot(cg, st_g.astype(BF16), preferred_element_type=F32))
        upd = jnp.dot(bt[g * SSD_STATE:(g + 1) * SSD_STATE, :], xw[:, g * gw:(g + 1) * gw],
                      preferred_element_type=F32)
        new_st.append(jnp.exp(tot_e[:, g * gw:(g + 1) * gw]) * st_g + upd)
    st_ref[d] = jnp.concatenate(new_st, axis=1)
    if need_y:
        y_ref[pl.ds(r0, T), :] += y + jnp.concatenate(y_state, axis=1) * jnp.exp(acum_e)


def _ssd_kernel(z_ref, xs_ref, bs_ref, cs_ref, dtr_ref, cw_ref, cb_ref, dtb_ref, alog_ref, dsk_ref, ng_ref, h0_ref,
                *refs, seq, need_y):
    if need_y:
        o_ref, hT_ref, act_ref, dt_ref, la_ref, st_ref, y_ref = refs
    else:
        hT_ref, act_ref, dt_ref, la_ref, st_ref = refs
        y_ref = None
    nc = seq // SSD_CHUNK
    row = lax.broadcasted_iota(jnp.int32, (seq, 1), 0)
    for gi, ref in enumerate((xs_ref, bs_ref, cs_ref)):
        sl = slice(gi * W_SSD, (gi + 1) * W_SSD)
        x = ref[0]
        cv = (_shift_rows(x, 1, row) * cw_ref[0:1, sl] + x * cw_ref[1:2, sl]
              + _shift_rows(x, -1, row) * cw_ref[2:3, sl] + cb_ref[:, sl])
        act_ref[:, sl] = cv * jax.nn.sigmoid(cv)
    dtv = dtr_ref[0] + dtb_ref[...]
    dt = jnp.maximum(dtv, 0.0) + jnp.log1p(jnp.exp(-jnp.abs(dtv)))
    dt_ref[...] = dt
    la_ref[...] = dt * (-jnp.exp(alog_ref[...]))
    st_ref[...] = h0_ref[0]
    if need_y:
        y_ref[...] = act_ref[:, 0:W_SSD] * dsk_ref[...]

    def body(i, carry):
        _ssd_chunk(i, 0, act_ref, dt_ref, la_ref, y_ref, st_ref, need_y)
        _ssd_chunk(nc - 1 - i, 1, act_ref, dt_ref, la_ref, y_ref, st_ref, need_y)
        return carry

    lax.fori_loop(0, nc, body, 0, unroll=2)
    hT_ref[0] = st_ref[...]
    if need_y:
        z = z_ref[0]
        yz = y_ref[...] * (z * jax.nn.sigmoid(z))
        ms = jnp.mean(yz * yz, axis=-1, keepdims=True)
        o_ref[0] = yz * lax.rsqrt(ms + EPS) * ng_ref[...]


def ssd_scan(pr3, h0, conv_w, conv_b, dt_bias, a_log, d_skip, norm_g, need_y):
    bn, seq, _ = pr3.shape
    pad = LANES - 2 * SSD_HEADS
    dtb = jnp.pad(dt_bias.reshape(1, -1), ((0, 0), (0, pad)))
    alog = jnp.pad(a_log.reshape(1, -1), ((0, 0), (0, pad)))
    dsk = jnp.repeat(d_skip, SSD_HEAD_DIM)[None, :]
    blk = lambda col, w=W_SSD: pl.BlockSpec((1, seq, w), lambda b: (b, 0, col))
    full = lambda a: pl.BlockSpec(a.shape, lambda b: (0,) * a.ndim)
    st_shape = (1, 2, SSD_STATE, W_SSD)
    st_spec = pl.BlockSpec(st_shape, lambda b: (b, 0, 0, 0))
    args = (conv_w, conv_b[None, :], dtb, alog, dsk, norm_g[None, :])
    out_shape = [jax.ShapeDtypeStruct((bn,) + st_shape[1:], F32)]
    out_specs = [st_spec]
    scratch = [pltpu.VMEM((seq, 3 * W_SSD), F32), pltpu.VMEM((seq, LANES), F32), pltpu.VMEM((seq, LANES), F32),
               pltpu.VMEM(st_shape[1:], F32)]
    if need_y:
        out_shape.insert(0, jax.ShapeDtypeStruct((bn, seq, W_SSD), F32))
        out_specs.insert(0, pl.BlockSpec((1, seq, W_SSD), lambda b: (b, 0, 0)))
        scratch.append(pltpu.VMEM((seq, W_SSD), F32))
    res = pl.pallas_call(
        functools.partial(_ssd_kernel, seq=seq, need_y=need_y),
        out_shape=out_shape,
        grid=(bn,),
        in_specs=[blk(Z_COL), blk(XS_COL), blk(BS_COL), blk(CS_COL), blk(DT_COL, LANES)]
                 + [full(a) for a in args] + [st_spec],
        out_specs=out_specs,
        scratch_shapes=scratch,
        compiler_params=_params("parallel"),
        name="ssd_scan",
    )(pr3, pr3, pr3, pr3, pr3, *args, h0)
    return (res[0], res[1]) if need_y else (None, res[0])


def ssd_mix(pr3, prc3, conv_w, conv_b, dt_bias, a_log, d_skip, norm_g, ctx_out):
    bn = pr3.shape[0]
    h0 = jnp.zeros((bn, 2, SSD_STATE, W_SSD), F32)
    oc, hc = ssd_scan(prc3, h0, conv_w, conv_b, dt_bias, a_log, d_skip, norm_g, ctx_out)
    o, _ = ssd_scan(pr3, hc, conv_w, conv_b, dt_bias, a_log, d_skip, norm_g, True)
    return o, oc


def mixer(pr3, prc3, pool_w, pool_scale, conv_w, rpb, s_conv_w, s_conv_b, dt_bias, a_log, d_skip, s_norm_g,
          ctx_out):
    bn, S, _ = pr3.shape
    o_ssd, oc_ssd = ssd_mix(pr3, prc3, s_conv_w, s_conv_b, dt_bias, a_log, d_skip, s_norm_g, ctx_out)
    o = [*local_mix(pr3, pool_w, pool_scale, conv_w),
         na_attention(pr3, prc3, na_bias_table(rpb, S // GRID_W)), o_ssd]
    if not ctx_out:
        return o, None
    oc = [*local_mix(prc3, pool_w, pool_scale, conv_w), ctx_attention(prc3), oc_ssd]
    return o, oc


TM = 512
TM_CTX = 256
TM_FFN = 1024
TM_GRP = 512


def kernel(x, c, ctx, c_ctx, w_ada, b_ada, g_mix, g_ffn, w_in, w_out, pool_w, pool_scale, conv_w, na_rpb,
           ssd_conv_w, ssd_conv_b, ssd_dt_bias, ssd_a_log, ssd_d, ssd_norm_g, ffn_w_gu, ffn_w_down,
           moe_router, moe_w_gu, moe_w_down, g_final):
    bn, S, d = x.shape
    Lc = ctx.shape[1]
    m, mc = bn * S, bn * Lc
    tpb = S // TM
    x2 = x.reshape(m, d)
    xc2 = ctx.reshape(mc, d)
    c_rows = jnp.concatenate([c, c_ctx[None, :], jnp.zeros((7, d), F32)], axis=0)
    for l in range(DEPTH):
        ctx_out = l < DEPTH - 1
        last = l == DEPTH - 1
        ada = ada_modulation(c_rows, w_ada[l].astype(BF16), b_ada[l][None, :])
        mod = ada[:bn].reshape(bn, 6, d)
        mod_c = ada[bn:bn + 1].reshape(1, 6, d)
        w_in_l = jnp.pad(w_in[l], ((0, 0), (0, D_IN_PAD - D_IN_PROJ))).astype(BF16)
        g_m = g_mix[l][None, :]
        g_f = g_ffn[l][None, :]
        pr = in_proj(x2, g_m, mod, w_in_l, tpb, TM)
        pr_c = in_proj(xc2, g_m, mod_c, w_in_l, None, TM_CTX)
        o, oc = mixer(pr.reshape(bn, S, -1), pr_c.reshape(bn, Lc, -1), pool_w[l], pool_scale[l], conv_w[l],
                      na_rpb[l], ssd_conv_w[l], ssd_conv_b[l], ssd_dt_bias[l], ssd_a_log[l], ssd_d[l],
                      ssd_norm_g[l], ctx_out)
        w_out_l = w_out[l].astype(BF16)
        x2 = out_proj(x2, [p.reshape(m, -1) for p in o], mod, w_out_l, tpb, TM)
        if ctx_out:
            xc2 = out_proj(xc2, [p.reshape(mc, -1) for p in oc], mod_c, w_out_l, None, TM_CTX)
        j = l // 2
        if l % 2 == 0:
            w_gu = ffn_w_gu[j].astype(BF16)
            w_dn = ffn_w_down[j].astype(BF16)
            x2 = ffn_dense(x2, g_f, mod, w_gu, w_dn, S // TM_FFN, TM_FFN)
            if ctx_out:
                xc2 = ffn_dense(xc2, g_f, mod_c, w_gu, w_dn, None, min(TM_FFN, mc))
        else:
            w_r = jnp.pad(moe_router[j], ((0, 0), (0, LANES - N_EXPERTS))).astype(BF16)
            w_gu = moe_w_gu[j].astype(BF16)
            w_dn = moe_w_down[j].astype(BF16)
            x2 = moe_block(x2, g_f, mod, w_r, w_gu, w_dn, tpb, TM, TM_GRP, g_final[None, :] if last else None)
            if ctx_out:
                xc2 = moe_block(xc2, g_f, mod_c, w_r, w_gu, w_dn, None, TM_CTX, TM_GRP)
            if last:
                return x2.reshape(bn, S, d)
    return final_norm(x2, g_final[None, :], TM).reshape(bn, S, d)
```

```python
import functools
import math

import numpy as np
import jax
import jax.numpy as jnp
from jax import lax
from jax.experimental import pallas as pl
from jax.experimental.pallas import tpu as pltpu

D_MODEL = 1024
DEPTH = 2
GRID_W = 64
EPS = 1e-6
W_POOL = 256
W_CONV = 256
W_NA = 256
W_SSD = 256
POOL_WINDOWS = (2, 4, 8, 16)
POOL_GROUP = W_POOL // len(POOL_WINDOWS)
NA_HEADS = 4
NA_HEAD_DIM = W_NA // NA_HEADS
WIN_R = 8
WIN_C = 16
COL_BLOCK = 16
COL_BAND = 32
SSD_HEAD_DIM = 64
SSD_HEADS = W_SSD // SSD_HEAD_DIM
SSD_GROUPS = 2
SSD_STATE = 128
SSD_CHUNK = 128
SSD_XBC = W_SSD + 2 * SSD_GROUPS * SSD_STATE
D_IN_PROJ = W_POOL + 3 * W_CONV + 3 * W_NA + W_SSD + SSD_XBC + 2 * SSD_HEADS
PROJ_SPLITS = (W_POOL, W_POOL + 3 * W_CONV, W_POOL + 3 * W_CONV + 3 * W_NA)
D_FF = 2816
N_EXPERTS = 8
TOP_K = 2
D_FF_EXPERT = 1408

LANES = 128
D_IN_PAD = -(-D_IN_PROJ // LANES) * LANES
VMEM_LIMIT = 56 * 1024 * 1024
BF16 = jnp.bfloat16
F32 = jnp.float32


def _params(*sem):
    return pltpu.CompilerParams(dimension_semantics=sem, vmem_limit_bytes=VMEM_LIMIT)


def _norm_mod(x, g, scale, shift):
    ms = jnp.mean(x * x, axis=-1, keepdims=True)
    return (x * lax.rsqrt(ms + EPS) * g) * (1.0 + scale) + shift


def _mod_map(tiles_per_batch):
    if tiles_per_batch is None:
        return lambda i, *_: (0, 0, 0)
    return lambda i, *_: (i // tiles_per_batch, 0, 0)


def _ada_kernel(c_ref, w_ref, b_ref, o_ref):
    c = c_ref[...]
    s = c * jax.nn.sigmoid(c)
    o_ref[...] = jnp.dot(s.astype(BF16), w_ref[...], preferred_element_type=F32) + b_ref[...]


def ada_modulation(c_rows, w, b):
    r, d = c_rows.shape
    n = w.shape[1]
    tn = 1536
    return pl.pallas_call(
        _ada_kernel,
        out_shape=jax.ShapeDtypeStruct((r, n), F32),
        grid=(n // tn,),
        in_specs=[pl.BlockSpec((r, d), lambda j: (0, 0)),
                  pl.BlockSpec((d, tn), lambda j: (0, j)),
                  pl.BlockSpec((1, tn), lambda j: (0, j))],
        out_specs=pl.BlockSpec((r, tn), lambda j: (0, j)),
        compiler_params=_params("arbitrary"),
        name="ada_modulation",
    )(c_rows, w, b)


def _resident(shape, index_map):
    return pl.BlockSpec(shape, index_map, pipeline_mode=pl.Buffered(1))


def _in_proj_kernel(x_ref, g_ref, mod_ref, w_ref, o_ref, dt_ref):
    h = _norm_mod(x_ref[...], g_ref[...], mod_ref[0, 1:2, :], mod_ref[0, 0:1, :])
    pr = jnp.dot(h.astype(BF16), w_ref[...], preferred_element_type=F32)
    o_ref[...] = pr.astype(o_ref.dtype)
    dt_ref[...] = pr[:, DT_COL * LANES:(DT_COL + 1) * LANES]


def in_proj(x2, g, mod, w, tiles_per_batch, tm):
    m, d = x2.shape
    n = w.shape[1]
    return pl.pallas_call(
        _in_proj_kernel,
        out_shape=(jax.ShapeDtypeStruct((m, n), BF16), jax.ShapeDtypeStruct((m, LANES), F32)),
        grid=(m // tm,),
        in_specs=[pl.BlockSpec((tm, d), lambda i: (i, 0)),
                  pl.BlockSpec((1, d), lambda i: (0, 0)),
                  pl.BlockSpec((1, 6, d), _mod_map(tiles_per_batch)),
                  _resident((d, n), lambda i: (0, 0))],
        out_specs=(pl.BlockSpec((tm, n), lambda i: (i, 0)), pl.BlockSpec((tm, LANES), lambda i: (i, 0))),
        compiler_params=_params("parallel"),
        name="in_proj",
    )(x2, g, mod, w)


def _out_proj_kernel(x_ref, mod_ref, w_ref, *refs):
    part_refs, y_ref = refs[:-1], refs[-1]
    y = jnp.zeros(y_ref.shape, F32)
    k0 = 0
    for p_ref in part_refs:
        k = p_ref.shape[1]
        y = y + jnp.dot(p_ref[...].astype(BF16), w_ref[k0:k0 + k, :], preferred_element_type=F32)
        k0 += k
    y_ref[...] = x_ref[...] + mod_ref[0, 2:3, :] * y


def out_proj(x2, parts, mod, w, tiles_per_batch, tm):
    m, d = x2.shape
    return pl.pallas_call(
        _out_proj_kernel,
        out_shape=jax.ShapeDtypeStruct((m, d), F32),
        grid=(m // tm,),
        in_specs=[pl.BlockSpec((tm, d), lambda i: (i, 0)),
                  pl.BlockSpec((1, 6, d), _mod_map(tiles_per_batch)),
                  pl.BlockSpec(w.shape, lambda i: (0, 0))]
                 + [pl.BlockSpec((tm, p.shape[1]), lambda i: (i, 0)) for p in parts],
        out_specs=pl.BlockSpec((tm, d), lambda i: (i, 0)),
        compiler_params=_params("parallel"),
        name="out_proj",
    )(x2, mod, w, *parts)


FF_CHUNK = 512


def _swiglu(h, wgu, wd, ff):
    acc = None
    for c0 in range(0, ff, FF_CHUNK):
        c1 = min(c0 + FF_CHUNK, ff)
        gg = jnp.dot(h, wgu(c0, c1), preferred_element_type=F32)
        uu = jnp.dot(h, wgu(ff + c0, ff + c1), preferred_element_type=F32)
        a = (gg * jax.nn.sigmoid(gg) * uu).astype(BF16)
        part = jnp.dot(a, wd(c0, c1), preferred_element_type=F32)
        acc = part if acc is None else acc + part
    return acc


def _ffn_kernel(x_ref, g_ref, mod_ref, wgu_ref, wd_ref, y_ref):
    x = x_ref[...]
    h = _norm_mod(x, g_ref[...], mod_ref[0, 4:5, :], mod_ref[0, 3:4, :]).astype(BF16)
    y = _swiglu(h, lambda a, b: wgu_ref[:, a:b], lambda a, b: wd_ref[a:b, :], wd_ref.shape[0])
    y_ref[...] = x + mod_ref[0, 5:6, :] * y


def ffn_dense(x2, g, mod, w_gu, w_down, tiles_per_batch, tm):
    m, d = x2.shape
    return pl.pallas_call(
        _ffn_kernel,
        out_shape=jax.ShapeDtypeStruct((m, d), F32),
        grid=(m // tm,),
        in_specs=[pl.BlockSpec((tm, d), lambda i: (i, 0)),
                  pl.BlockSpec((1, d), lambda i: (0, 0)),
                  pl.BlockSpec((1, 6, d), _mod_map(tiles_per_batch)),
                  _resident(w_gu.shape, lambda i: (0, 0)),
                  _resident(w_down.shape, lambda i: (0, 0))],
        out_specs=pl.BlockSpec((tm, d), lambda i: (i, 0)),
        compiler_params=_params("parallel"),
        name="ffn_dense",
    )(x2, g, mod, w_gu, w_down)


ROUTE_E1, ROUTE_E2, ROUTE_R1, ROUTE_R2, ROUTE_G1, ROUTE_G2 = range(6)


def _router_kernel(x_ref, g_ref, mod_ref, wr_ref, h_ref, route_ref, cnt_ref, base_ref):
    @pl.when(pl.program_id(0) == 0)
    def _():
        base_ref[...] = jnp.zeros_like(base_ref)

    h = _norm_mod(x_ref[...], g_ref[...], mod_ref[0, 4:5, :], mod_ref[0, 3:4, :]).astype(BF16)
    h_ref[...] = h
    tm = h.shape[0]
    logits = jnp.dot(h, wr_ref[...], preferred_element_type=F32)
    lane = lax.broadcasted_iota(jnp.int32, logits.shape, 1)
    neg = jnp.float32(-jnp.inf)
    logits = jnp.where(lane < N_EXPERTS, logits, neg)
    m1 = jnp.max(logits, axis=-1, keepdims=True)
    i1 = jnp.min(jnp.where(logits == m1, lane, LANES), axis=-1, keepdims=True)
    rest = jnp.where(lane == i1, neg, logits)
    m2 = jnp.max(rest, axis=-1, keepdims=True)
    i2 = jnp.min(jnp.where(rest == m2, lane, LANES), axis=-1, keepdims=True)
    e2 = jnp.exp(m2 - m1)
    g1 = 1.0 / (1.0 + e2)
    g2 = e2 / (1.0 + e2)
    sel1, sel2 = lane == i1, lane == i2
    sel = jnp.where(sel1 | sel2, 1.0, 0.0)
    li = lax.broadcasted_iota(jnp.int32, (tm, tm), 0)
    si = lax.broadcasted_iota(jnp.int32, (tm, tm), 1)
    before = jnp.where(si < li, 1.0, 0.0).astype(BF16)
    rank = jnp.dot(before, sel.astype(BF16), preferred_element_type=F32) + base_ref[...]
    r1 = jnp.sum(jnp.where(sel1, rank, 0.0), axis=-1, keepdims=True)
    r2 = jnp.sum(jnp.where(sel2, rank, 0.0), axis=-1, keepdims=True)
    base_ref[...] += jnp.sum(sel, axis=0, keepdims=True)
    cnt_ref[...] = base_ref[...]
    fields = (i1.astype(F32), i2.astype(F32), r1, r2, g1, g2)
    route = jnp.zeros(logits.shape, F32)
    for k, v in enumerate(fields):
        route = jnp.where(lane == k, v, route)
    route_ref[...] = route


def moe_router(x2, g, mod, w_router, tiles_per_batch, tm):
    m, d = x2.shape
    return pl.pallas_call(
        _router_kernel,
        out_shape=(jax.ShapeDtypeStruct((m, d), BF16), jax.ShapeDtypeStruct((m, LANES), F32),
                   jax.ShapeDtypeStruct((1, LANES), F32)),
        grid=(m // tm,),
        in_specs=[pl.BlockSpec((tm, d), lambda i: (i, 0)),
                  pl.BlockSpec((1, d), lambda i: (0, 0)),
                  pl.BlockSpec((1, 6, d), _mod_map(tiles_per_batch)),
                  pl.BlockSpec((d, LANES), lambda i: (0, 0))],
        out_specs=(pl.BlockSpec((tm, d), lambda i: (i, 0)),
                   pl.BlockSpec((tm, LANES), lambda i: (i, 0)),
                   pl.BlockSpec((1, LANES), lambda i: (0, 0))),
        scratch_shapes=[pltpu.VMEM((1, LANES), F32)],
        compiler_params=_params("arbitrary"),
        name="moe_router",
    )(x2, g, mod, w_router)


def _moe_kernel(tile_ref, exp_ref, start_ref, end_ref, xg_ref, wgu_ref, wd_ref, y_ref):
    w = pl.program_id(0)
    tm = xg_ref.shape[0]
    start, end = start_ref[w], end_ref[w]
    t0 = tile_ref[w] * tm

    @pl.when(end > start)
    def _():
        y = _swiglu(xg_ref[...], lambda a, b: wgu_ref[0, :, a:b], lambda a, b: wd_ref[0, a:b, :], wd_ref.shape[1])
        y = y.astype(y_ref.dtype)

        @pl.when(start == t0)
        def _():
            y_ref[...] = y

        @pl.when(start != t0)
        def _():
            row = t0 + lax.broadcasted_iota(jnp.int32, (tm, 1), 0)
            y_ref[...] = jnp.where(row >= start, y, y_ref[...])


def moe_grouped(item_tile, item_expert, item_start, item_end, xg, w_gu, w_down, tm):
    p, d = xg.shape
    grid_spec = pltpu.PrefetchScalarGridSpec(
        num_scalar_prefetch=4,
        grid=(item_tile.shape[0],),
        in_specs=[pl.BlockSpec((tm, d), lambda w, it, ie, s, e: (it[w], 0)),
                  _resident((1,) + w_gu.shape[1:], lambda w, it, ie, s, e: (ie[w], 0, 0)),
                  _resident((1,) + w_down.shape[1:], lambda w, it, ie, s, e: (ie[w], 0, 0))],
        out_specs=pl.BlockSpec((tm, d), lambda w, it, ie, s, e: (it[w], 0)),
    )
    return pl.pallas_call(
        _moe_kernel,
        out_shape=jax.ShapeDtypeStruct((p, d), BF16),
        grid_spec=grid_spec,
        compiler_params=_params("arbitrary"),
        name="moe_grouped",
    )(item_tile, item_expert, item_start, item_end, xg, w_gu, w_down)


def _moe_combine_kernel(x_ref, ya_ref, yb_ref, route_ref, mod_ref, *refs):
    o_ref = refs[-1]
    r = route_ref[...]
    y = (r[:, ROUTE_G1:ROUTE_G1 + 1] * ya_ref[...].astype(F32) + r[:, ROUTE_G2:ROUTE_G2 + 1] * yb_ref[...].astype(F32))
    x = x_ref[...] + mod_ref[0, 5:6, :] * y
    if len(refs) == 2:
        ms = jnp.mean(x * x, axis=-1, keepdims=True)
        x = x * lax.rsqrt(ms + EPS) * refs[0][...]
    o_ref[...] = x


def moe_combine(x2, ya, yb, route, mod, tiles_per_batch, tm, g_final=None):
    m, d = x2.shape
    row = pl.BlockSpec((tm, d), lambda i: (i, 0))
    specs = [row, row, row, pl.BlockSpec((tm, LANES), lambda i: (i, 0)),
             pl.BlockSpec((1, 6, d), _mod_map(tiles_per_batch))]
    args = [x2, ya, yb, route, mod]
    if g_final is not None:
        specs.append(pl.BlockSpec((1, d), lambda i: (0, 0)))
        args.append(g_final)
    return pl.pallas_call(
        _moe_combine_kernel,
        out_shape=jax.ShapeDtypeStruct((m, d), F32),
        grid=(m // tm,),
        in_specs=specs,
        out_specs=row,
        compiler_params=_params("parallel"),
        name="moe_combine",
    )(*args)


def moe_block(x2, g, mod, w_router, w_gu, w_down, tiles_per_batch, tm_tok, tm_grp, g_final=None):
    m, d = x2.shape
    h, route, cnt = moe_router(x2, g, mod, w_router, tiles_per_batch, tm_tok)
    cnt = cnt[0, :N_EXPERTS].astype(jnp.int32)
    p = m * TOP_K
    off = jnp.cumsum(cnt) - cnt
    ri = route[:, :ROUTE_G1].astype(jnp.int32)
    onehot = lambda e: (e[:, None] == jnp.arange(N_EXPERTS)[None, :]).astype(jnp.int32)
    d1 = jnp.sum(onehot(ri[:, ROUTE_E1]) * off[None, :], axis=1) + ri[:, ROUTE_R1]
    d2 = jnp.sum(onehot(ri[:, ROUTE_E2]) * off[None, :], axis=1) + ri[:, ROUTE_R2]
    tok = jnp.arange(m, dtype=jnp.int32)
    _, src = lax.sort_key_val(jnp.concatenate([d1, d2]), jnp.concatenate([tok, tok]))
    n_row_tiles = p // tm_grp
    start = jnp.sort(jnp.concatenate([jnp.arange(n_row_tiles, dtype=jnp.int32) * tm_grp, off]))
    end = jnp.concatenate([start[1:], jnp.full((1,), p, jnp.int32)])
    item_tile = jnp.minimum(start // tm_grp, n_row_tiles - 1)
    item_expert = jnp.sum((off[None, :] <= start[:, None]).astype(jnp.int32), axis=1) - 1
    rows = lambda a, idx: a.at[idx].get(mode="promise_in_bounds")
    yg = moe_grouped(item_tile, item_expert, start, end, rows(h, src), w_gu, w_down, tm_grp)
    return moe_combine(x2, rows(yg, d1), rows(yg, d2), route, mod, tiles_per_batch, tm_tok, g_final)


def _final_norm_kernel(x_ref, g_ref, o_ref):
    x = x_ref[...]
    ms = jnp.mean(x * x, axis=-1, keepdims=True)
    o_ref[...] = x * lax.rsqrt(ms + EPS) * g_ref[...]


def final_norm(x2, g, tm):
    m, d = x2.shape
    return pl.pallas_call(
        _final_norm_kernel,
        out_shape=jax.ShapeDtypeStruct((m, d), F32),
        grid=(m // tm,),
        in_specs=[pl.BlockSpec((tm, d), lambda i: (i, 0)), pl.BlockSpec((1, d), lambda i: (0, 0))],
        out_specs=pl.BlockSpec((tm, d), lambda i: (i, 0)),
        compiler_params=_params("parallel"),
        name="final_norm",
    )(x2, g)


NA_QROWS = 4
NA_KROWS = NA_QROWS + WIN_R
Q_COL, K_COL, V_COL = 4, 5, 6


def _na_key_start(j, rows):
    return np.clip(j * NA_QROWS - WIN_R // 2, 0, rows - NA_KROWS)


def _na_bias_index(rows):
    nblk = rows // NA_QROWS
    pats = []
    for j in range(nblk):
        start = _na_key_start(j, rows)
        r = j * NA_QROWS + np.arange(NA_QROWS)
        sr = np.clip(r - WIN_R // 2, 0, rows - WIN_R)
        kr = start + np.arange(NA_KROWS)
        rvalid = (kr[None, :] >= sr[:, None]) & (kr[None, :] < sr[:, None] + WIN_R)
        ri = np.clip(kr[None, :] - r[:, None] + WIN_R - 1, 0, 2 * WIN_R - 2)
        pats.append((ri, rvalid))
    for j in range(2, nblk - 1):
        assert all(np.array_equal(a, b) for a, b in zip(pats[1], pats[j]))
    sel = [pats[0], pats[1], pats[nblk - 1]]
    ri = np.stack([p[0] for p in sel])
    rvalid = np.stack([p[1] for p in sel])
    c = np.arange(GRID_W)
    sc = np.clip(c - WIN_C // 2, 0, GRID_W - WIN_C)
    cvalid = (c[None, :] >= sc[:, None]) & (c[None, :] < sc[:, None] + WIN_C)
    ci = np.clip(c[None, :] - c[:, None] + WIN_C - 1, 0, 2 * WIN_C - 2)
    c_onehot = (ci[..., None] == np.arange(2 * WIN_C - 1)).astype(np.float32)
    valid = rvalid[:, :, None, :, None] & cvalid[None, None, :, None, :]
    return ri, c_onehot, valid


def na_bias_table(rpb, rows):
    ri, c_onehot, valid = _na_bias_index(rows)
    toep = jnp.einsum('hrd,qkd->hrqk', rpb, c_onehot, precision=lax.Precision.HIGHEST)
    b = jnp.take(toep, ri.reshape(-1), axis=1).reshape((NA_HEADS,) + ri.shape + (GRID_W, GRID_W))
    b = jnp.transpose(b, (1, 0, 2, 4, 3, 5))
    b = jnp.where(valid[:, None], b, -jnp.inf)
    return b.reshape(3, NA_HEADS, NA_QROWS * GRID_W, NA_KROWS * GRID_W).astype(F32)


def _attend_heads(q, key_sets, bias_fn):
    n, w = q.shape
    lane = lax.broadcasted_iota(jnp.int32, (1, w), 1)
    out = jnp.zeros((n, w), F32)
    for h in range(NA_HEADS):
        mh = (lane >= h * NA_HEAD_DIM) & (lane < (h + 1) * NA_HEAD_DIM)
        qh = jnp.where(mh, q, 0.0).astype(BF16)
        scores = []
        for i, (k, _) in enumerate(key_sets):
            s = lax.dot_general(qh, k, (((1,), (1,)), ((), ())), preferred_element_type=F32)
            b = bias_fn(i, h)
            scores.append(s if b is None else s + b)
        m = scores[0].max(axis=-1, keepdims=True)
        for s in scores[1:]:
            m = jnp.maximum(m, s.max(axis=-1, keepdims=True))
        denom = jnp.zeros((n, 1), F32)
        acc = jnp.zeros((n, w), F32)
        for s, (_, v) in zip(scores, key_sets):
            p = jnp.exp(s - m)
            denom = denom + p.sum(axis=-1, keepdims=True)
            acc = acc + jnp.dot(p.astype(BF16), v, preferred_element_type=F32)
        out = out + jnp.where(mh, acc / denom, 0.0)
    return out


def _na_kernel(q_ref, k_ref, v_ref, kc_ref, vc_ref, bias_ref, o_ref, *, rows):
    j = pl.program_id(1)
    start = jnp.clip(j * NA_QROWS - WIN_R // 2, 0, rows - NA_KROWS)
    t0 = pl.multiple_of(start * GRID_W, GRID_W)
    nk = NA_KROWS * GRID_W
    kw = k_ref[0, pl.ds(t0, nk), :].astype(BF16)
    vw = v_ref[0, pl.ds(t0, nk), :].astype(BF16)
    kc = kc_ref[0].astype(BF16)
    vc = vc_ref[0].astype(BF16)
    q = q_ref[0] * (1.0 / math.sqrt(NA_HEAD_DIM))
    o_ref[0] = _attend_heads(q, [(kw, vw), (kc, vc)], lambda i, h: bias_ref[0, h] if i == 0 else None)


def na_attention(pr3, prc3, bias):
    bn, S, _ = pr3.shape
    Lc = prc3.shape[1]
    rows = S // GRID_W
    nblk = rows // NA_QROWS
    nq = NA_QROWS * GRID_W

    def pat(b, j):
        return (jnp.where(j == 0, 0, jnp.where(j == nblk - 1, 2, 1)), 0, 0, 0)

    return pl.pallas_call(
        functools.partial(_na_kernel, rows=rows),
        out_shape=jax.ShapeDtypeStruct((bn, S, W_NA), F32),
        grid=(bn, nblk),
        in_specs=[pl.BlockSpec((1, nq, W_NA), lambda b, j: (b, j, Q_COL)),
                  pl.BlockSpec((1, S, W_NA), lambda b, j: (b, 0, K_COL)),
                  pl.BlockSpec((1, S, W_NA), lambda b, j: (b, 0, V_COL)),
                  pl.BlockSpec((1, Lc, W_NA), lambda b, j: (b, 0, K_COL)),
                  pl.BlockSpec((1, Lc, W_NA), lambda b, j: (b, 0, V_COL)),
                  pl.BlockSpec((1,) + bias.shape[1:], pat)],
        out_specs=pl.BlockSpec((1, nq, W_NA), lambda b, j: (b, j, 0)),
        compiler_params=_params("parallel", "arbitrary"),
        name="na_attention",
    )(pr3, pr3, pr3, prc3, prc3, bias)


def _ctx_attn_kernel(q_ref, k_ref, v_ref, o_ref):
    q = q_ref[0] * (1.0 / math.sqrt(NA_HEAD_DIM))
    o_ref[0] = _attend_heads(q, [(k_ref[0].astype(BF16), v_ref[0].astype(BF16))], lambda i, h: None)


def ctx_attention(prc3):
    bn, Lc, _ = prc3.shape
    return pl.pallas_call(
        _ctx_attn_kernel,
        out_shape=jax.ShapeDtypeStruct((bn, Lc, W_NA), F32),
        grid=(bn,),
        in_specs=[pl.BlockSpec((1, Lc, W_NA), lambda b: (b, 0, Q_COL)),
                  pl.BlockSpec((1, Lc, W_NA), lambda b: (b, 0, K_COL)),
                  pl.BlockSpec((1, Lc, W_NA), lambda b: (b, 0, V_COL))],
        out_specs=pl.BlockSpec((1, Lc, W_NA), lambda b: (b, 0, 0)),
        compiler_params=_params("parallel"),
        name="ctx_attention",
    )(prc3, prc3, prc3)


POOL_COL, CONV_H_COL, CONV_B_COL, CONV_C_COL = 0, 1, 2, 3


def _shift_rows(x, k, row):
    n = x.shape[0]
    y = pltpu.roll(x, k % n, 0)
    return jnp.where(row < k, 0.0, y) if k > 0 else jnp.where(row >= n + k, 0.0, y)


def _local_mix_kernel(u_ref, h_ref, bg_ref, cg_ref, pw_ref, ps_ref, cw_ref, op_ref, oc_ref, *, seq):
    row = lax.broadcasted_iota(jnp.int32, (seq, 1), 0)
    lane = lax.broadcasted_iota(jnp.int32, (1, W_POOL), 1)
    u = u_ref[0].astype(F32)
    trailing, leading = {1: u}, {1: u}
    for w in (1, 2, 4):
        trailing[2 * w] = trailing[w] + _shift_rows(trailing[w], w, row)
        leading[2 * w] = leading[w] + _shift_rows(leading[w], -w, row)
    rowf = row.astype(F32)
    win_sum = jnp.zeros_like(u)
    cnt = jnp.zeros_like(u)
    for g, win in enumerate(POOL_WINDOWS):
        half = win // 2
        mg = (lane >= g * POOL_GROUP) & (lane < (g + 1) * POOL_GROUP)
        s = _shift_rows(trailing[half], 1, row) + leading[half]
        n = jnp.minimum(rowf + half, float(seq)) - jnp.maximum(rowf - half, 0.0)
        win_sum = jnp.where(mg, s, win_sum)
        cnt = jnp.where(mg, n, cnt)
    p = win_sum / cnt - u
    op_ref[0] = jnp.dot(p.astype(BF16), pw_ref[...], preferred_element_type=F32) * ps_ref[...]
    v = cg_ref[0].astype(F32) * h_ref[0].astype(F32)
    conv = (_shift_rows(v, 1, row) * cw_ref[0:1, :] + v * cw_ref[1:2, :] + _shift_rows(v, -1, row) * cw_ref[2:3, :])
    oc_ref[0] = bg_ref[0].astype(F32) * conv


def local_mix(pr3, pool_w, pool_scale, conv_w):
    bn, seq, _ = pr3.shape
    pw = jax.scipy.linalg.block_diag(*[pool_w[g] for g in range(len(POOL_WINDOWS))]).astype(BF16)
    blk = lambda col: pl.BlockSpec((1, seq, W_POOL), lambda b: (b, 0, col))
    full = lambda a: pl.BlockSpec(a.shape, lambda b: (0,) * a.ndim)
    args = (pw, pool_scale[None, :], conv_w)
    return pl.pallas_call(
        functools.partial(_local_mix_kernel, seq=seq),
        out_shape=(jax.ShapeDtypeStruct((bn, seq, W_POOL), F32), jax.ShapeDtypeStruct((bn, seq, W_CONV), F32)),
        grid=(bn,),
        in_specs=[blk(POOL_COL), blk(CONV_H_COL), blk(CONV_B_COL), blk(CONV_C_COL)] + [full(a) for a in args],
        out_specs=(pl.BlockSpec((1, seq, W_POOL), lambda b: (b, 0, 0)),
                   pl.BlockSpec((1, seq, W_CONV), lambda b: (b, 0, 0))),
        compiler_params=_params("parallel"),
        name="local_mix",
    )(pr3, pr3, pr3, pr3, *args)


Z_COL, XS_COL, BS_COL, CS_COL = 7, 8, 9, 10
DT_COL = 22
HEADS_PER_GROUP = SSD_HEADS // SSD_GROUPS


def _expand_heads(v, d):
    lane = lax.broadcasted_iota(jnp.int32, (1, W_SSD), 1)
    out = jnp.zeros((v.shape[0], W_SSD), F32)
    for h in range(SSD_HEADS):
        k = d * SSD_HEADS + h
        mh = (lane >= h * SSD_HEAD_DIM) & (lane < (h + 1) * SSD_HEAD_DIM)
        out = jnp.where(mh, v[:, k:k + 1], out)
    return out


def _ssd_chunk(c, d, act_ref, dt_ref, la_ref, y_ref, st_ref, need_y):
    T = SSD_CHUNK
    r0 = pl.multiple_of(c * T, T)
    xs = act_ref[pl.ds(r0, T), 0:W_SSD]
    bm = act_ref[pl.ds(r0, T), W_SSD:2 * W_SSD]
    cm = act_ref[pl.ds(r0, T), 2 * W_SSD:3 * W_SSD]
    dt = dt_ref[pl.ds(r0, T), :]
    la = la_ref[pl.ds(r0, T), :]
    li = lax.broadcasted_iota(jnp.int32, (T, T), 0)
    si = lax.broadcasted_iota(jnp.int32, (T, T), 1)
    mask = (si <= li) if d == 0 else (si >= li)
    acum = jnp.dot(mask.astype(F32), la, precision=lax.Precision.HIGHEST, preferred_element_type=F32)
    tot = acum[T - 1:T, :] if d == 0 else acum[0:1, :]
    acum_e = _expand_heads(acum, d)
    tot_e = _expand_heads(tot, d)
    xdt = xs * _expand_heads(dt, d)
    xw = (xdt * jnp.exp(tot_e - acum_e)).astype(BF16)
    st = st_ref[d]
    bt = bm.T.astype(BF16)
    if need_y:
        acum_t = acum.T
        xdt_b = xdt.astype(BF16)
        lane = lax.broadcasted_iota(jnp.int32, (1, W_SSD), 1)
        y = jnp.zeros((T, W_SSD), F32)
        y_state = []
    new_st = []
    gw = HEADS_PER_GROUP * SSD_HEAD_DIM
    for g in range(SSD_GROUPS):
        bg = bm[:, g * SSD_STATE:(g + 1) * SSD_STATE].astype(BF16)
        cg = cm[:, g * SSD_STATE:(g + 1) * SSD_STATE].astype(BF16)
        st_g = st[:, g * gw:(g + 1) * gw]
        if need_y:
            cb = lax.dot_general(cg, bg, (((1,), (1,)), ((), ())), preferred_element_type=F32)
            for h in range(g * HEADS_PER_GROUP, (g + 1) * HEADS_PER_GROUP):
                k = d * SSD_HEADS + h
                decay = jnp.exp(jnp.where(mask, acum[:, k:k + 1] - acum_t[k:k + 1, :], -jnp.inf))
                mh = (lane >= h * SSD_HEAD_DIM) & (lane < (h + 1) * SSD_HEAD_DIM)
                y = y + jnp.where(mh, jnp.dot((cb * decay).astype(BF16), xdt_b, preferred_element_type=F32), 0.0)
            y_state.append(jnp.dot(cg, st_g.astype(BF16), preferred_element_type=F32))
        upd = jnp.dot(bt[g * SSD_STATE:(g + 1) * SSD_STATE, :], xw[:, g * gw:(g + 1) * gw],
                      preferred_element_type=F32)
        new_st.append(jnp.exp(tot_e[:, g * gw:(g + 1) * gw]) * st_g + upd)
    st_ref[d] = jnp.concatenate(new_st, axis=1)
    if need_y:
        y_ref[pl.ds(r0, T), :] += y + jnp.concatenate(y_state, axis=1) * jnp.exp(acum_e)


def _ssd_kernel(z_ref, xs_ref, bs_ref, cs_ref, dtr_ref, cw_ref, cb_ref, dtb_ref, alog_ref, dsk_ref, ng_ref, h0_ref,
                *refs, seq, need_y):
    if need_y:
        o_ref, hT_ref, act_ref, dt_ref, la_ref, st_ref, y_ref = refs
    else:
        hT_ref, act_ref, dt_ref, la_ref, st_ref = refs
        y_ref = None
    nc = seq // SSD_CHUNK
    row = lax.broadcasted_iota(jnp.int32, (seq, 1), 0)
    for gi, ref in enumerate((xs_ref, bs_ref, cs_ref)):
        sl = slice(gi * W_SSD, (gi + 1) * W_SSD)
        x = ref[0].astype(F32)
        cv = (_shift_rows(x, 1, row) * cw_ref[0:1, sl] + x * cw_ref[1:2, sl]
              + _shift_rows(x, -1, row) * cw_ref[2:3, sl] + cb_ref[:, sl])
        act_ref[:, sl] = cv * jax.nn.sigmoid(cv)
    dtv = dtr_ref[0] + dtb_ref[...]
    dt = jnp.maximum(dtv, 0.0) + jnp.log1p(jnp.exp(-jnp.abs(dtv)))
    dt_ref[...] = dt
    la_ref[...] = dt * (-jnp.exp(alog_ref[...]))
    st_ref[...] = h0_ref[0]
    if need_y:
        y_ref[...] = act_ref[:, 0:W_SSD] * dsk_ref[...]

    def body(i, carry):
        _ssd_chunk(i, 0, act_ref, dt_ref, la_ref, y_ref, st_ref, need_y)
        _ssd_chunk(nc - 1 - i, 1, act_ref, dt_ref, la_ref, y_ref, st_ref, need_y)
        return carry

    lax.fori_loop(0, nc, body, 0, unroll=2)
    hT_ref[0] = st_ref[...]
    if need_y:
        z = z_ref[0].astype(F32)
        yz = y_ref[...] * (z * jax.nn.sigmoid(z))
        ms = jnp.mean(yz * yz, axis=-1, keepdims=True)
        o_ref[0] = yz * lax.rsqrt(ms + EPS) * ng_ref[...]


def ssd_scan(pr3, dt3, h0, conv_w, conv_b, dt_bias, a_log, d_skip, norm_g, need_y):
    bn, seq, _ = pr3.shape
    pad = LANES - 2 * SSD_HEADS
    dtb = jnp.pad(dt_bias.reshape(1, -1), ((0, 0), (0, pad)))
    alog = jnp.pad(a_log.reshape(1, -1), ((0, 0), (0, pad)))
    dsk = jnp.repeat(d_skip, SSD_HEAD_DIM)[None, :]
    blk = lambda col, w=W_SSD: pl.BlockSpec((1, seq, w), lambda b: (b, 0, col))
    full = lambda a: pl.BlockSpec(a.shape, lambda b: (0,) * a.ndim)
    st_shape = (1, 2, SSD_STATE, W_SSD)
    st_spec = pl.BlockSpec(st_shape, lambda b: (b, 0, 0, 0))
    args = (conv_w, conv_b[None, :], dtb, alog, dsk, norm_g[None, :])
    out_shape = [jax.ShapeDtypeStruct((bn,) + st_shape[1:], F32)]
    out_specs = [st_spec]
    scratch = [pltpu.VMEM((seq, 3 * W_SSD), F32), pltpu.VMEM((seq, LANES), F32), pltpu.VMEM((seq, LANES), F32),
               pltpu.VMEM(st_shape[1:], F32)]
    if need_y:
        out_shape.insert(0, jax.ShapeDtypeStruct((bn, seq, W_SSD), F32))
        out_specs.insert(0, pl.BlockSpec((1, seq, W_SSD), lambda b: (b, 0, 0)))
        scratch.append(pltpu.VMEM((seq, W_SSD), F32))
    res = pl.pallas_call(
        functools.partial(_ssd_kernel, seq=seq, need_y=need_y),
        out_shape=out_shape,
        grid=(bn,),
        in_specs=[blk(Z_COL), blk(XS_COL), blk(BS_COL), blk(CS_COL),
                  pl.BlockSpec((1, seq, LANES), lambda b: (b, 0, 0))]
                 + [full(a) for a in args] + [st_spec],
        out_specs=out_specs,
        scratch_shapes=scratch,
        compiler_params=_params("parallel"),
        name="ssd_scan",
    )(pr3, pr3, pr3, pr3, dt3, *args, h0)
    return (res[0], res[1]) if need_y else (None, res[0])


def ssd_mix(pr3, dt3, prc3, dtc3, conv_w, conv_b, dt_bias, a_log, d_skip, norm_g, ctx_out):
    bn = pr3.shape[0]
    h0 = jnp.zeros((bn, 2, SSD_STATE, W_SSD), F32)
    oc, hc = ssd_scan(prc3, dtc3, h0, conv_w, conv_b, dt_bias, a_log, d_skip, norm_g, ctx_out)
    o, _ = ssd_scan(pr3, dt3, hc, conv_w, conv_b, dt_bias, a_log, d_skip, norm_g, True)
    return o, oc


def mixer(pr3, dt3, prc3, dtc3, pool_w, pool_scale, conv_w, rpb, s_conv_w, s_conv_b, dt_bias, a_log, d_skip,
          s_norm_g, ctx_out):
    bn, S, _ = pr3.shape
    o_ssd, oc_ssd = ssd_mix(pr3, dt3, prc3, dtc3, s_conv_w, s_conv_b, dt_bias, a_log, d_skip, s_norm_g, ctx_out)
    o = [*local_mix(pr3, pool_w, pool_scale, conv_w),
         na_attention(pr3, prc3, na_bias_table(rpb, S // GRID_W)), o_ssd]
    if not ctx_out:
        return o, None
    oc = [*local_mix(prc3, pool_w, pool_scale, conv_w), ctx_attention(prc3), oc_ssd]
    return o, oc


TM = 512
TM_CTX = 256
TM_FFN = 1024
TM_GRP = 512


def kernel(x, c, ctx, c_ctx, w_ada, b_ada, g_mix, g_ffn, w_in, w_out, pool_w, pool_scale, conv_w, na_rpb,
           ssd_conv_w, ssd_conv_b, ssd_dt_bias, ssd_a_log, ssd_d, ssd_norm_g, ffn_w_gu, ffn_w_down,
           moe_router, moe_w_gu, moe_w_down, g_final):
    bn, S, d = x.shape
    Lc = ctx.shape[1]
    m, mc = bn * S, bn * Lc
    tpb = S // TM
    x2 = x.reshape(m, d)
    xc2 = ctx.reshape(mc, d)
    c_rows = jnp.concatenate([c, c_ctx[None, :], jnp.zeros((7, d), F32)], axis=0)
    for l in range(DEPTH):
        ctx_out = l < DEPTH - 1
        last = l == DEPTH - 1
        ada = ada_modulation(c_rows, w_ada[l].astype(BF16), b_ada[l][None, :])
        mod = ada[:bn].reshape(bn, 6, d)
        mod_c = ada[bn:bn + 1].reshape(1, 6, d)
        w_in_l = jnp.pad(w_in[l], ((0, 0), (0, D_IN_PAD - D_IN_PROJ))).astype(BF16)
        g_m = g_mix[l][None, :]
        g_f = g_ffn[l][None, :]
        pr, dt = in_proj(x2, g_m, mod, w_in_l, S // TM_FFN, TM_FFN)
        pr_c, dt_c = in_proj(xc2, g_m, mod_c, w_in_l, None, min(TM_FFN, mc))
        o, oc = mixer(pr.reshape(bn, S, -1), dt.reshape(bn, S, -1), pr_c.reshape(bn, Lc, -1),
                      dt_c.reshape(bn, Lc, -1), pool_w[l], pool_scale[l], conv_w[l],
                      na_rpb[l], ssd_conv_w[l], ssd_conv_b[l], ssd_dt_bias[l], ssd_a_log[l], ssd_d[l],
                      ssd_norm_g[l], ctx_out)
        w_out_l = w_out[l].astype(BF16)
        x2 = out_proj(x2, [p.reshape(m, -1) for p in o], mod, w_out_l, tpb, TM)
        if ctx_out:
            xc2 = out_proj(xc2, [p.reshape(mc, -1) for p in oc], mod_c, w_out_l, None, TM_CTX)
        j = l // 2
        if l % 2 == 0:
            w_gu = ffn_w_gu[j].astype(BF16)
            w_dn = ffn_w_down[j].astype(BF16)
            x2 = ffn_dense(x2, g_f, mod, w_gu, w_dn, S // TM_FFN, TM_FFN)
            if ctx_out:
                xc2 = ffn_dense(xc2, g_f, mod_c, w_gu, w_dn, None, min(TM_FFN, mc))
        else:
            w_r = jnp.pad(moe_router[j], ((0, 0), (0, LANES - N_EXPERTS))).astype(BF16)
            w_gu = moe_w_gu[j].astype(BF16)
            w_dn = moe_w_down[j].astype(BF16)
            x2 = moe_block(x2, g_f, mod, w_r, w_gu, w_dn, tpb, TM, TM_GRP, g_final[None, :] if last else None)
            if ctx_out:
                xc2 = moe_block(xc2, g_f, mod_c, w_r, w_gu, w_dn, None, TM_CTX, TM_GRP)
            if last:
                return x2.reshape(bn, S, d)
    return final_norm(x2, g_final[None, :], TM).reshape(bn, S, d)
```

```python
import functools
import math

import numpy as np
import jax
import jax.numpy as jnp
from jax import lax
from jax.experimental import pallas as pl
from jax.experimental.pallas import tpu as pltpu

D_MODEL = 1024
DEPTH = 2
GRID_W = 64
EPS = 1e-6
W_POOL = 256
W_CONV = 256
W_NA = 256
W_SSD = 256
POOL_WINDOWS = (2, 4, 8, 16)
POOL_GROUP = W_POOL // len(POOL_WINDOWS)
NA_HEADS = 4
NA_HEAD_DIM = W_NA // NA_HEADS
WIN_R = 8
WIN_C = 16
COL_BLOCK = 16
COL_BAND = 32
SSD_HEAD_DIM = 64
SSD_HEADS = W_SSD // SSD_HEAD_DIM
SSD_GROUPS = 2
SSD_STATE = 128
SSD_CHUNK = 128
SSD_XBC = W_SSD + 2 * SSD_GROUPS * SSD_STATE
D_IN_PROJ = W_POOL + 3 * W_CONV + 3 * W_NA + W_SSD + SSD_XBC + 2 * SSD_HEADS
PROJ_SPLITS = (W_POOL, W_POOL + 3 * W_CONV, W_POOL + 3 * W_CONV + 3 * W_NA)
D_FF = 2816
N_EXPERTS = 8
TOP_K = 2
D_FF_EXPERT = 1408

LANES = 128
D_IN_PAD = -(-D_IN_PROJ // LANES) * LANES
VMEM_LIMIT = 56 * 1024 * 1024
BF16 = jnp.bfloat16
F32 = jnp.float32


def _params(*sem):
    return pltpu.CompilerParams(dimension_semantics=sem, vmem_limit_bytes=VMEM_LIMIT)


def _norm_mod(x, g, scale, shift):
    ms = jnp.mean(x * x, axis=-1, keepdims=True)
    return (x * lax.rsqrt(ms + EPS) * g) * (1.0 + scale) + shift


def _mod_map(tiles_per_batch):
    if tiles_per_batch is None:
        return lambda i, *_: (0, 0, 0)
    return lambda i, *_: (i // tiles_per_batch, 0, 0)


def _ada_kernel(c_ref, w_ref, b_ref, o_ref):
    c = c_ref[...]
    s = c * jax.nn.sigmoid(c)
    o_ref[...] = jnp.dot(s.astype(BF16), w_ref[...], preferred_element_type=F32) + b_ref[...]


def ada_modulation(c_rows, w, b):
    r, d = c_rows.shape
    n = w.shape[1]
    tn = 1536
    return pl.pallas_call(
        _ada_kernel,
        out_shape=jax.ShapeDtypeStruct((r, n), F32),
        grid=(n // tn,),
        in_specs=[pl.BlockSpec((r, d), lambda j: (0, 0)),
                  pl.BlockSpec((d, tn), lambda j: (0, j)),
                  pl.BlockSpec((1, tn), lambda j: (0, j))],
        out_specs=pl.BlockSpec((r, tn), lambda j: (0, j)),
        compiler_params=_params("arbitrary"),
        name="ada_modulation",
    )(c_rows, w, b)


def _resident(shape, index_map):
    return pl.BlockSpec(shape, index_map, pipeline_mode=pl.Buffered(1))


def _in_proj_kernel(x_ref, g_ref, mod_ref, w_ref, o_ref, dt_ref):
    h = _norm_mod(x_ref[...], g_ref[...], mod_ref[0, 1:2, :], mod_ref[0, 0:1, :])
    pr = jnp.dot(h.astype(BF16), w_ref[...], preferred_element_type=F32)
    o_ref[...] = pr.astype(o_ref.dtype)
    dt_ref[...] = pr[:, DT_COL * LANES:(DT_COL + 1) * LANES]


def in_proj(x2, g, mod, w, tiles_per_batch, tm):
    m, d = x2.shape
    n = w.shape[1]
    return pl.pallas_call(
        _in_proj_kernel,
        out_shape=(jax.ShapeDtypeStruct((m, n), BF16), jax.ShapeDtypeStruct((m, LANES), F32)),
        grid=(m // tm,),
        in_specs=[pl.BlockSpec((tm, d), lambda i: (i, 0)),
                  pl.BlockSpec((1, d), lambda i: (0, 0)),
                  pl.BlockSpec((1, 6, d), _mod_map(tiles_per_batch)),
                  _resident((d, n), lambda i: (0, 0))],
        out_specs=(pl.BlockSpec((tm, n), lambda i: (i, 0)), pl.BlockSpec((tm, LANES), lambda i: (i, 0))),
        compiler_params=_params("parallel"),
        name="in_proj",
    )(x2, g, mod, w)


def _mix_residual(x, mod_ref, w_ref, part_refs):
    y, k0 = None, 0
    for p_ref in part_refs:
        k = p_ref.shape[1]
        t = jnp.dot(p_ref[...], w_ref[k0:k0 + k, :], preferred_element_type=F32)
        y = t if y is None else y + t
        k0 += k
    return x + mod_ref[0, 2:3, :] * y


def _part_specs(parts, tm):
    return [pl.BlockSpec((tm, p.shape[1]), lambda i, *_: (i, 0)) for p in parts]


FF_CHUNK = 512


def _swiglu(h, wgu, wd, ff):
    acc = None
    for c0 in range(0, ff, FF_CHUNK):
        c1 = min(c0 + FF_CHUNK, ff)
        gg = jnp.dot(h, wgu(c0, c1), preferred_element_type=F32)
        uu = jnp.dot(h, wgu(ff + c0, ff + c1), preferred_element_type=F32)
        a = (gg * jax.nn.sigmoid(gg) * uu).astype(BF16)
        part = jnp.dot(a, wd(c0, c1), preferred_element_type=F32)
        acc = part if acc is None else acc + part
    return acc


def _ffn_kernel(x_ref, g_ref, mod_ref, wout_ref, wgu_ref, wd_ref, *refs):
    x = _mix_residual(x_ref[...], mod_ref, wout_ref, refs[:-1])
    y_ref = refs[-1]
    h = _norm_mod(x, g_ref[...], mod_ref[0, 4:5, :], mod_ref[0, 3:4, :]).astype(BF16)
    y = _swiglu(h, lambda a, b: wgu_ref[:, a:b], lambda a, b: wd_ref[a:b, :], wd_ref.shape[0])
    y_ref[...] = x + mod_ref[0, 5:6, :] * y


def ffn_dense(x2, parts, g, mod, w_out, w_gu, w_down, tiles_per_batch, tm):
    m, d = x2.shape
    return pl.pallas_call(
        _ffn_kernel,
        out_shape=jax.ShapeDtypeStruct((m, d), F32),
        grid=(m // tm,),
        in_specs=[pl.BlockSpec((tm, d), lambda i: (i, 0)),
                  pl.BlockSpec((1, d), lambda i: (0, 0)),
                  pl.BlockSpec((1, 6, d), _mod_map(tiles_per_batch)),
                  _resident(w_out.shape, lambda i: (0, 0)),
                  _resident(w_gu.shape, lambda i: (0, 0)),
                  _resident(w_down.shape, lambda i: (0, 0))] + _part_specs(parts, tm),
        out_specs=pl.BlockSpec((tm, d), lambda i: (i, 0)),
        compiler_params=_params("parallel"),
        name="ffn_dense",
    )(x2, g, mod, w_out, w_gu, w_down, *parts)


ROUTE_E1, ROUTE_E2, ROUTE_R1, ROUTE_R2, ROUTE_G1, ROUTE_G2 = range(6)


def _router_kernel(x_ref, g_ref, mod_ref, wout_ref, wr_ref, *refs):
    part_refs = refs[:-5]
    x1_ref, h_ref, route_ref, cnt_ref, base_ref = refs[-5:]

    @pl.when(pl.program_id(0) == 0)
    def _():
        base_ref[...] = jnp.zeros_like(base_ref)

    x = _mix_residual(x_ref[...], mod_ref, wout_ref, part_refs)
    x1_ref[...] = x
    h = _norm_mod(x, g_ref[...], mod_ref[0, 4:5, :], mod_ref[0, 3:4, :]).astype(BF16)
    h_ref[...] = h
    tm = h.shape[0]
    logits = jnp.dot(h, wr_ref[...], preferred_element_type=F32)
    lane = lax.broadcasted_iota(jnp.int32, logits.shape, 1)
    neg = jnp.float32(-jnp.inf)
    logits = jnp.where(lane < N_EXPERTS, logits, neg)
    m1 = jnp.max(logits, axis=-1, keepdims=True)
    i1 = jnp.min(jnp.where(logits == m1, lane, LANES), axis=-1, keepdims=True)
    rest = jnp.where(lane == i1, neg, logits)
    m2 = jnp.max(rest, axis=-1, keepdims=True)
    i2 = jnp.min(jnp.where(rest == m2, lane, LANES), axis=-1, keepdims=True)
    e2 = jnp.exp(m2 - m1)
    g1 = 1.0 / (1.0 + e2)
    g2 = e2 / (1.0 + e2)
    sel1, sel2 = lane == i1, lane == i2
    sel = jnp.where(sel1 | sel2, 1.0, 0.0)
    li = lax.broadcasted_iota(jnp.int32, (tm, tm), 0)
    si = lax.broadcasted_iota(jnp.int32, (tm, tm), 1)
    before = jnp.where(si < li, 1.0, 0.0).astype(BF16)
    rank = jnp.dot(before, sel.astype(BF16), preferred_element_type=F32) + base_ref[...]
    r1 = jnp.sum(jnp.where(sel1, rank, 0.0), axis=-1, keepdims=True)
    r2 = jnp.sum(jnp.where(sel2, rank, 0.0), axis=-1, keepdims=True)
    base_ref[...] += jnp.sum(sel, axis=0, keepdims=True)
    cnt_ref[...] = base_ref[...]
    fields = (i1.astype(F32), i2.astype(F32), r1, r2, g1, g2)
    route = jnp.zeros(logits.shape, F32)
    for k, v in enumerate(fields):
        route = jnp.where(lane == k, v, route)
    route_ref[...] = route


def moe_router(x2, parts, g, mod, w_out, w_router, tiles_per_batch, tm):
    m, d = x2.shape
    return pl.pallas_call(
        _router_kernel,
        out_shape=(jax.ShapeDtypeStruct((m, d), F32), jax.ShapeDtypeStruct((m, d), BF16),
                   jax.ShapeDtypeStruct((m, LANES), F32), jax.ShapeDtypeStruct((1, LANES), F32)),
        grid=(m // tm,),
        in_specs=[pl.BlockSpec((tm, d), lambda i: (i, 0)),
                  pl.BlockSpec((1, d), lambda i: (0, 0)),
                  pl.BlockSpec((1, 6, d), _mod_map(tiles_per_batch)),
                  _resident(w_out.shape, lambda i: (0, 0)),
                  pl.BlockSpec((d, LANES), lambda i: (0, 0))] + _part_specs(parts, tm),
        out_specs=(pl.BlockSpec((tm, d), lambda i: (i, 0)),
                   pl.BlockSpec((tm, d), lambda i: (i, 0)),
                   pl.BlockSpec((tm, LANES), lambda i: (i, 0)),
                   pl.BlockSpec((1, LANES), lambda i: (0, 0))),
        scratch_shapes=[pltpu.VMEM((1, LANES), F32)],
        compiler_params=_params("arbitrary"),
        name="moe_router",
    )(x2, g, mod, w_out, w_router, *parts)


def _moe_kernel(tile_ref, exp_ref, start_ref, end_ref, xg_ref, wgu_ref, wd_ref, y_ref):
    w = pl.program_id(0)
    tm = xg_ref.shape[0]
    start, end = start_ref[w], end_ref[w]
    t0 = tile_ref[w] * tm

    @pl.when(end > start)
    def _():
        y = _swiglu(xg_ref[...], lambda a, b: wgu_ref[0, :, a:b], lambda a, b: wd_ref[0, a:b, :], wd_ref.shape[1])
        y = y.astype(y_ref.dtype)

        @pl.when(start == t0)
        def _():
            y_ref[...] = y

        @pl.when(start != t0)
        def _():
            row = t0 + lax.broadcasted_iota(jnp.int32, (tm, 1), 0)
            y_ref[...] = jnp.where(row >= start, y, y_ref[...])


def moe_grouped(item_tile, item_expert, item_start, item_end, xg, w_gu, w_down, tm):
    p, d = xg.shape
    grid_spec = pltpu.PrefetchScalarGridSpec(
        num_scalar_prefetch=4,
        grid=(item_tile.shape[0],),
        in_specs=[pl.BlockSpec((tm, d), lambda w, it, ie, s, e: (it[w], 0)),
                  _resident((1,) + w_gu.shape[1:], lambda w, it, ie, s, e: (ie[w], 0, 0)),
                  _resident((1,) + w_down.shape[1:], lambda w, it, ie, s, e: (ie[w], 0, 0))],
        out_specs=pl.BlockSpec((tm, d), lambda w, it, ie, s, e: (it[w], 0)),
    )
    return pl.pallas_call(
        _moe_kernel,
        out_shape=jax.ShapeDtypeStruct((p, d), BF16),
        grid_spec=grid_spec,
        compiler_params=_params("arbitrary"),
        name="moe_grouped",
    )(item_tile, item_expert, item_start, item_end, xg, w_gu, w_down)


def _moe_combine_kernel(x_ref, ya_ref, yb_ref, route_ref, mod_ref, *refs):
    o_ref = refs[-1]
    r = route_ref[...]
    y = (r[:, ROUTE_G1:ROUTE_G1 + 1] * ya_ref[...].astype(F32) + r[:, ROUTE_G2:ROUTE_G2 + 1] * yb_ref[...].astype(F32))
    x = x_ref[...] + mod_ref[0, 5:6, :] * y
    if len(refs) == 2:
        ms = jnp.mean(x * x, axis=-1, keepdims=True)
        x = x * lax.rsqrt(ms + EPS) * refs[0][...]
    o_ref[...] = x


def moe_combine(x2, ya, yb, route, mod, tiles_per_batch, tm, g_final=None):
    m, d = x2.shape
    row = pl.BlockSpec((tm, d), lambda i: (i, 0))
    specs = [row, row, row, pl.BlockSpec((tm, LANES), lambda i: (i, 0)),
             pl.BlockSpec((1, 6, d), _mod_map(tiles_per_batch))]
    args = [x2, ya, yb, route, mod]
    if g_final is not None:
        specs.append(pl.BlockSpec((1, d), lambda i: (0, 0)))
        args.append(g_final)
    return pl.pallas_call(
        _moe_combine_kernel,
        out_shape=jax.ShapeDtypeStruct((m, d), F32),
        grid=(m // tm,),
        in_specs=specs,
        out_specs=row,
        compiler_params=_params("parallel"),
        name="moe_combine",
    )(*args)


def moe_block(x2, parts, g, mod, w_out, w_router, w_gu, w_down, tiles_per_batch, tm_tok, tm_grp, g_final=None):
    m, d = x2.shape
    x2, h, route, cnt = moe_router(x2, parts, g, mod, w_out, w_router, tiles_per_batch, tm_tok)
    cnt = cnt[0, :N_EXPERTS].astype(jnp.int32)
    p = m * TOP_K
    off = jnp.cumsum(cnt) - cnt
    ri = route[:, :ROUTE_G1].astype(jnp.int32)
    onehot = lambda e: (e[:, None] == jnp.arange(N_EXPERTS)[None, :]).astype(jnp.int32)
    d1 = jnp.sum(onehot(ri[:, ROUTE_E1]) * off[None, :], axis=1) + ri[:, ROUTE_R1]
    d2 = jnp.sum(onehot(ri[:, ROUTE_E2]) * off[None, :], axis=1) + ri[:, ROUTE_R2]
    tok = jnp.arange(m, dtype=jnp.int32)
    _, src = lax.sort_key_val(jnp.concatenate([d1, d2]), jnp.concatenate([tok, tok]))
    n_row_tiles = p // tm_grp
    start = jnp.sort(jnp.concatenate([jnp.arange(n_row_tiles, dtype=jnp.int32) * tm_grp, off]))
    end = jnp.concatenate([start[1:], jnp.full((1,), p, jnp.int32)])
    item_tile = jnp.minimum(start // tm_grp, n_row_tiles - 1)
    item_expert = jnp.sum((off[None, :] <= start[:, None]).astype(jnp.int32), axis=1) - 1
    rows = lambda a, idx: a.at[idx].get(mode="promise_in_bounds")
    yg = moe_grouped(item_tile, item_expert, start, end, rows(h, src), w_gu, w_down, tm_grp)
    return moe_combine(x2, rows(yg, d1), rows(yg, d2), route, mod, tiles_per_batch, tm_tok, g_final)


def _final_norm_kernel(x_ref, g_ref, o_ref):
    x = x_ref[...]
    ms = jnp.mean(x * x, axis=-1, keepdims=True)
    o_ref[...] = x * lax.rsqrt(ms + EPS) * g_ref[...]


def final_norm(x2, g, tm):
    m, d = x2.shape
    return pl.pallas_call(
        _final_norm_kernel,
        out_shape=jax.ShapeDtypeStruct((m, d), F32),
        grid=(m // tm,),
        in_specs=[pl.BlockSpec((tm, d), lambda i: (i, 0)), pl.BlockSpec((1, d), lambda i: (0, 0))],
        out_specs=pl.BlockSpec((tm, d), lambda i: (i, 0)),
        compiler_params=_params("parallel"),
        name="final_norm",
    )(x2, g)


NA_QROWS = 4
NA_KROWS = NA_QROWS + WIN_R
Q_COL, K_COL, V_COL = 4, 5, 6


def _na_key_start(j, rows):
    return np.clip(j * NA_QROWS - WIN_R // 2, 0, rows - NA_KROWS)


def _na_bias_index(rows):
    nblk = rows // NA_QROWS
    pats = []
    for j in range(nblk):
        start = _na_key_start(j, rows)
        r = j * NA_QROWS + np.arange(NA_QROWS)
        sr = np.clip(r - WIN_R // 2, 0, rows - WIN_R)
        kr = start + np.arange(NA_KROWS)
        rvalid = (kr[None, :] >= sr[:, None]) & (kr[None, :] < sr[:, None] + WIN_R)
        ri = np.clip(kr[None, :] - r[:, None] + WIN_R - 1, 0, 2 * WIN_R - 2)
        pats.append((ri, rvalid))
    for j in range(2, nblk - 1):
        assert all(np.array_equal(a, b) for a, b in zip(pats[1], pats[j]))
    sel = [pats[0], pats[1], pats[nblk - 1]]
    ri = np.stack([p[0] for p in sel])
    rvalid = np.stack([p[1] for p in sel])
    c = np.arange(GRID_W)
    sc = np.clip(c - WIN_C // 2, 0, GRID_W - WIN_C)
    cvalid = (c[None, :] >= sc[:, None]) & (c[None, :] < sc[:, None] + WIN_C)
    ci = np.clip(c[None, :] - c[:, None] + WIN_C - 1, 0, 2 * WIN_C - 2)
    c_onehot = (ci[..., None] == np.arange(2 * WIN_C - 1)).astype(np.float32)
    valid = rvalid[:, :, None, :, None] & cvalid[None, None, :, None, :]
    return ri, c_onehot, valid


def na_bias_table(rpb, rows):
    ri, c_onehot, valid = _na_bias_index(rows)
    toep = jnp.einsum('hrd,qkd->hrqk', rpb, c_onehot, precision=lax.Precision.HIGHEST)
    b = jnp.take(toep, ri.reshape(-1), axis=1).reshape((NA_HEADS,) + ri.shape + (GRID_W, GRID_W))
    b = jnp.transpose(b, (1, 0, 2, 4, 3, 5))
    b = jnp.where(valid[:, None], b, -jnp.inf)
    return b.reshape(3, NA_HEADS, NA_QROWS * GRID_W, NA_KROWS * GRID_W).astype(F32)


def _attend_heads(q, key_sets, bias_fn):
    n, w = q.shape
    lane = lax.broadcasted_iota(jnp.int32, (1, w), 1)
    out = jnp.zeros((n, w), F32)
    for h in range(NA_HEADS):
        mh = (lane >= h * NA_HEAD_DIM) & (lane < (h + 1) * NA_HEAD_DIM)
        qh = jnp.where(mh, q, 0.0).astype(BF16)
        scores = []
        for i, (k, _) in enumerate(key_sets):
            s = lax.dot_general(qh, k, (((1,), (1,)), ((), ())), preferred_element_type=F32)
            b = bias_fn(i, h)
            scores.append(s if b is None else s + b)
        m = scores[0].max(axis=-1, keepdims=True)
        for s in scores[1:]:
            m = jnp.maximum(m, s.max(axis=-1, keepdims=True))
        denom = jnp.zeros((n, 1), F32)
        acc = jnp.zeros((n, w), F32)
        for s, (_, v) in zip(scores, key_sets):
            p = jnp.exp(s - m)
            denom = denom + p.sum(axis=-1, keepdims=True)
            acc = acc + jnp.dot(p.astype(BF16), v, preferred_element_type=F32)
        out = out + jnp.where(mh, acc / denom, 0.0)
    return out


def _na_kernel(q_ref, k_ref, v_ref, kc_ref, vc_ref, bias_ref, o_ref, *, rows):
    j = pl.program_id(1)
    start = jnp.clip(j * NA_QROWS - WIN_R // 2, 0, rows - NA_KROWS)
    t0 = pl.multiple_of(start * GRID_W, GRID_W)
    nk = NA_KROWS * GRID_W
    kw = k_ref[0, pl.ds(t0, nk), :].astype(BF16)
    vw = v_ref[0, pl.ds(t0, nk), :].astype(BF16)
    kc = kc_ref[0].astype(BF16)
    vc = vc_ref[0].astype(BF16)
    q = q_ref[0] * (1.0 / math.sqrt(NA_HEAD_DIM))
    o = _attend_heads(q, [(kw, vw), (kc, vc)], lambda i, h: bias_ref[0, h] if i == 0 else None)
    o_ref[0] = o.astype(o_ref.dtype)


def na_attention(pr3, prc3, bias):
    bn, S, _ = pr3.shape
    Lc = prc3.shape[1]
    rows = S // GRID_W
    nblk = rows // NA_QROWS
    nq = NA_QROWS * GRID_W

    def pat(b, j):
        return (jnp.where(j == 0, 0, jnp.where(j == nblk - 1, 2, 1)), 0, 0, 0)

    return pl.pallas_call(
        functools.partial(_na_kernel, rows=rows),
        out_shape=jax.ShapeDtypeStruct((bn, S, W_NA), BF16),
        grid=(bn, nblk),
        in_specs=[pl.BlockSpec((1, nq, W_NA), lambda b, j: (b, j, Q_COL)),
                  pl.BlockSpec((1, S, W_NA), lambda b, j: (b, 0, K_COL)),
                  pl.BlockSpec((1, S, W_NA), lambda b, j: (b, 0, V_COL)),
                  pl.BlockSpec((1, Lc, W_NA), lambda b, j: (b, 0, K_COL)),
                  pl.BlockSpec((1, Lc, W_NA), lambda b, j: (b, 0, V_COL)),
                  pl.BlockSpec((1,) + bias.shape[1:], pat)],
        out_specs=pl.BlockSpec((1, nq, W_NA), lambda b, j: (b, j, 0)),
        compiler_params=_params("parallel", "arbitrary"),
        name="na_attention",
    )(pr3, pr3, pr3, prc3, prc3, bias)


def _ctx_attn_kernel(q_ref, k_ref, v_ref, o_ref):
    q = q_ref[0] * (1.0 / math.sqrt(NA_HEAD_DIM))
    o = _attend_heads(q, [(k_ref[0].astype(BF16), v_ref[0].astype(BF16))], lambda i, h: None)
    o_ref[0] = o.astype(o_ref.dtype)


def ctx_attention(prc3):
    bn, Lc, _ = prc3.shape
    return pl.pallas_call(
        _ctx_attn_kernel,
        out_shape=jax.ShapeDtypeStruct((bn, Lc, W_NA), BF16),
        grid=(bn,),
        in_specs=[pl.BlockSpec((1, Lc, W_NA), lambda b: (b, 0, Q_COL)),
                  pl.BlockSpec((1, Lc, W_NA), lambda b: (b, 0, K_COL)),
                  pl.BlockSpec((1, Lc, W_NA), lambda b: (b, 0, V_COL))],
        out_specs=pl.BlockSpec((1, Lc, W_NA), lambda b: (b, 0, 0)),
        compiler_params=_params("parallel"),
        name="ctx_attention",
    )(prc3, prc3, prc3)


POOL_COL, CONV_H_COL, CONV_B_COL, CONV_C_COL = 0, 1, 2, 3


def _shift_rows(x, k, row):
    n = x.shape[0]
    y = pltpu.roll(x, k % n, 0)
    return jnp.where(row < k, 0.0, y) if k > 0 else jnp.where(row >= n + k, 0.0, y)


def _local_mix_kernel(u_ref, h_ref, bg_ref, cg_ref, pw_ref, ps_ref, cw_ref, op_ref, oc_ref, *, seq):
    row = lax.broadcasted_iota(jnp.int32, (seq, 1), 0)
    lane = lax.broadcasted_iota(jnp.int32, (1, W_POOL), 1)
    u = u_ref[0].astype(F32)
    trailing, leading = {1: u}, {1: u}
    for w in (1, 2, 4):
        trailing[2 * w] = trailing[w] + _shift_rows(trailing[w], w, row)
        leading[2 * w] = leading[w] + _shift_rows(leading[w], -w, row)
    rowf = row.astype(F32)
    win_sum = jnp.zeros_like(u)
    cnt = jnp.zeros_like(u)
    for g, win in enumerate(POOL_WINDOWS):
        half = win // 2
        mg = (lane >= g * POOL_GROUP) & (lane < (g + 1) * POOL_GROUP)
        s = _shift_rows(trailing[half], 1, row) + leading[half]
        n = jnp.minimum(rowf + half, float(seq)) - jnp.maximum(rowf - half, 0.0)
        win_sum = jnp.where(mg, s, win_sum)
        cnt = jnp.where(mg, n, cnt)
    p = win_sum / cnt - u
    pooled = jnp.dot(p.astype(BF16), pw_ref[...], preferred_element_type=F32) * ps_ref[...]
    op_ref[0] = pooled.astype(op_ref.dtype)
    v = cg_ref[0].astype(F32) * h_ref[0].astype(F32)
    conv = (_shift_rows(v, 1, row) * cw_ref[0:1, :] + v * cw_ref[1:2, :] + _shift_rows(v, -1, row) * cw_ref[2:3, :])
    oc_ref[0] = (bg_ref[0].astype(F32) * conv).astype(oc_ref.dtype)


def local_mix(pr3, pool_w, pool_scale, conv_w):
    bn, seq, _ = pr3.shape
    pw = jax.scipy.linalg.block_diag(*[pool_w[g] for g in range(len(POOL_WINDOWS))]).astype(BF16)
    blk = lambda col: pl.BlockSpec((1, seq, W_POOL), lambda b: (b, 0, col))
    full = lambda a: pl.BlockSpec(a.shape, lambda b: (0,) * a.ndim)
    args = (pw, pool_scale[None, :], conv_w)
    return pl.pallas_call(
        functools.partial(_local_mix_kernel, seq=seq),
        out_shape=(jax.ShapeDtypeStruct((bn, seq, W_POOL), BF16), jax.ShapeDtypeStruct((bn, seq, W_CONV), BF16)),
        grid=(bn,),
        in_specs=[blk(POOL_COL), blk(CONV_H_COL), blk(CONV_B_COL), blk(CONV_C_COL)] + [full(a) for a in args],
        out_specs=(pl.BlockSpec((1, seq, W_POOL), lambda b: (b, 0, 0)),
                   pl.BlockSpec((1, seq, W_CONV), lambda b: (b, 0, 0))),
        compiler_params=_params("parallel"),
        name="local_mix",
    )(pr3, pr3, pr3, pr3, *args)


Z_COL, XS_COL, BS_COL, CS_COL = 7, 8, 9, 10
DT_COL = 22
HEADS_PER_GROUP = SSD_HEADS // SSD_GROUPS


def _expand_heads(v, d):
    lane = lax.broadcasted_iota(jnp.int32, (1, W_SSD), 1)
    out = jnp.zeros((v.shape[0], W_SSD), F32)
    for h in range(SSD_HEADS):
        k = d * SSD_HEADS + h
        mh = (lane >= h * SSD_HEAD_DIM) & (lane < (h + 1) * SSD_HEAD_DIM)
        out = jnp.where(mh, v[:, k:k + 1], out)
    return out


def _ssd_chunk(c, d, act_ref, dt_ref, la_ref, y_ref, st_ref, need_y):
    T = SSD_CHUNK
    r0 = pl.multiple_of(c * T, T)
    xs = act_ref[pl.ds(r0, T), 0:W_SSD]
    bm = act_ref[pl.ds(r0, T), W_SSD:2 * W_SSD]
    cm = act_ref[pl.ds(r0, T), 2 * W_SSD:3 * W_SSD]
    dt = dt_ref[pl.ds(r0, T), :]
    la = la_ref[pl.ds(r0, T), :]
    li = lax.broadcasted_iota(jnp.int32, (T, T), 0)
    si = lax.broadcasted_iota(jnp.int32, (T, T), 1)
    mask = (si <= li) if d == 0 else (si >= li)
    acum = jnp.dot(mask.astype(F32), la, precision=lax.Precision.HIGHEST, preferred_element_type=F32)
    tot = acum[T - 1:T, :] if d == 0 else acum[0:1, :]
    acum_e = _expand_heads(acum, d)
    tot_e = _expand_heads(tot, d)
    xdt = xs * _expand_heads(dt, d)
    xw = (xdt * jnp.exp(tot_e - acum_e)).astype(BF16)
    st = st_ref[d]
    bt = bm.T.astype(BF16)
    if need_y:
        acum_t = acum.T
        xdt_b = xdt.astype(BF16)
        lane = lax.broadcasted_iota(jnp.int32, (1, W_SSD), 1)
        y = jnp.zeros((T, W_SSD), F32)
        y_state = []
    new_st = []
    gw = HEADS_PER_GROUP * SSD_HEAD_DIM
    for g in range(SSD_GROUPS):
        bg = bm[:, g * SSD_STATE:(g + 1) * SSD_STATE].astype(BF16)
        cg = cm[:, g * SSD_STATE:(g + 1) * SSD_STATE].astype(BF16)
        st_g = st[:, g * gw:(g + 1) * gw]
        if need_y:
            cb = lax.dot_general(cg, bg, (((1,), (1,)), ((), ())), preferred_element_type=F32)
            for h in range(g * HEADS_PER_GROUP, (g + 1) * HEADS_PER_GROUP):
                k = d * SSD_HEADS + h
                decay = jnp.exp(jnp.where(mask, acum[:, k:k + 1] - acum_t[k:k + 1, :], -jnp.inf))
                mh = (lane >= h * SSD_HEAD_DIM) & (lane < (h + 1) * SSD_HEAD_DIM)
                y = y + jnp.where(mh, jnp.dot((cb * decay).astype(BF16), xdt_b, preferred_element_type=F32), 0.0)
            y_state.append(jnp.dot(cg, st_g.astype(BF16), preferred_element_type=F32))
        upd = jnp.dot(bt[g * SSD_STATE:(g + 1) * SSD_STATE, :], xw[:, g * gw:(g + 1) * gw],
                      preferred_element_type=F32)
        new_st.append(jnp.exp(tot_e[:, g * gw:(g + 1) * gw]) * st_g + upd)
    st_ref[d] = jnp.concatenate(new_st, axis=1)
    if need_y:
        y_ref[pl.ds(r0, T), :] += y + jnp.concatenate(y_state, axis=1) * jnp.exp(acum_e)


def _ssd_kernel(z_ref, xs_ref, bs_ref, cs_ref, dtr_ref, cw_ref, cb_ref, dtb_ref, alog_ref, dsk_ref, ng_ref, h0_ref,
                *refs, seq, need_y):
    if need_y:
        o_ref, hT_ref, act_ref, dt_ref, la_ref, st_ref, y_ref = refs
    else:
        hT_ref, act_ref, dt_ref, la_ref, st_ref = refs
        y_ref = None
    nc = seq // SSD_CHUNK
    row = lax.broadcasted_iota(jnp.int32, (seq, 1), 0)
    for gi, ref in enumerate((xs_ref, bs_ref, cs_ref)):
        sl = slice(gi * W_SSD, (gi + 1) * W_SSD)
        x = ref[0].astype(F32)
        cv = (_shift_rows(x, 1, row) * cw_ref[0:1, sl] + x * cw_ref[1:2, sl]
              + _shift_rows(x, -1, row) * cw_ref[2:3, sl] + cb_ref[:, sl])
        act_ref[:, sl] = cv * jax.nn.sigmoid(cv)
    dtv = dtr_ref[0] + dtb_ref[...]
    dt = jnp.maximum(dtv, 0.0) + jnp.log1p(jnp.exp(-jnp.abs(dtv)))
    dt_ref[...] = dt
    la_ref[...] = dt * (-jnp.exp(alog_ref[...]))
    st_ref[...] = h0_ref[0]
    if need_y:
        y_ref[...] = act_ref[:, 0:W_SSD] * dsk_ref[...]

    def body(i, carry):
        _ssd_chunk(i, 0, act_ref, dt_ref, la_ref, y_ref, st_ref, need_y)
        _ssd_chunk(nc - 1 - i, 1, act_ref, dt_ref, la_ref, y_ref, st_ref, need_y)
        return carry

    lax.fori_loop(0, nc, body, 0, unroll=2)
    hT_ref[0] = st_ref[...]
    if need_y:
        z = z_ref[0].astype(F32)
        yz = y_ref[...] * (z * jax.nn.sigmoid(z))
        ms = jnp.mean(yz * yz, axis=-1, keepdims=True)
        o_ref[0] = (yz * lax.rsqrt(ms + EPS) * ng_ref[...]).astype(o_ref.dtype)


def ssd_scan(pr3, dt3, h0, conv_w, conv_b, dt_bias, a_log, d_skip, norm_g, need_y):
    bn, seq, _ = pr3.shape
    pad = LANES - 2 * SSD_HEADS
    dtb = jnp.pad(dt_bias.reshape(1, -1), ((0, 0), (0, pad)))
    alog = jnp.pad(a_log.reshape(1, -1), ((0, 0), (0, pad)))
    dsk = jnp.repeat(d_skip, SSD_HEAD_DIM)[None, :]
    blk = lambda col, w=W_SSD: pl.BlockSpec((1, seq, w), lambda b: (b, 0, col))
    full = lambda a: pl.BlockSpec(a.shape, lambda b: (0,) * a.ndim)
    st_shape = (1, 2, SSD_STATE, W_SSD)
    st_spec = pl.BlockSpec(st_shape, lambda b: (b, 0, 0, 0))
    args = (conv_w, conv_b[None, :], dtb, alog, dsk, norm_g[None, :])
    out_shape = [jax.ShapeDtypeStruct((bn,) + st_shape[1:], F32)]
    out_specs = [st_spec]
    scratch = [pltpu.VMEM((seq, 3 * W_SSD), F32), pltpu.VMEM((seq, LANES), F32), pltpu.VMEM((seq, LANES), F32),
               pltpu.VMEM(st_shape[1:], F32)]
    if need_y:
        out_shape.insert(0, jax.ShapeDtypeStruct((bn, seq, W_SSD), BF16))
        out_specs.insert(0, pl.BlockSpec((1, seq, W_SSD), lambda b: (b, 0, 0)))
        scratch.append(pltpu.VMEM((seq, W_SSD), F32))
    res = pl.pallas_call(
        functools.partial(_ssd_kernel, seq=seq, need_y=need_y),
        out_shape=out_shape,
        grid=(bn,),
        in_specs=[blk(Z_COL), blk(XS_COL), blk(BS_COL), blk(CS_COL),
                  pl.BlockSpec((1, seq, LANES), lambda b: (b, 0, 0))]
                 + [full(a) for a in args] + [st_spec],
        out_specs=out_specs,
        scratch_shapes=scratch,
        compiler_params=_params("parallel"),
        name="ssd_scan",
    )(pr3, pr3, pr3, pr3, dt3, *args, h0)
    return (res[0], res[1]) if need_y else (None, res[0])


def ssd_mix(pr3, dt3, prc3, dtc3, conv_w, conv_b, dt_bias, a_log, d_skip, norm_g, ctx_out):
    bn = pr3.shape[0]
    h0 = jnp.zeros((bn, 2, SSD_STATE, W_SSD), F32)
    oc, hc = ssd_scan(prc3, dtc3, h0, conv_w, conv_b, dt_bias, a_log, d_skip, norm_g, ctx_out)
    o, _ = ssd_scan(pr3, dt3, hc, conv_w, conv_b, dt_bias, a_log, d_skip, norm_g, True)
    return o, oc


def mixer(pr3, dt3, prc3, dtc3, pool_w, pool_scale, conv_w, rpb, s_conv_w, s_conv_b, dt_bias, a_log, d_skip,
          s_norm_g, ctx_out):
    bn, S, _ = pr3.shape
    o_ssd, oc_ssd = ssd_mix(pr3, dt3, prc3, dtc3, s_conv_w, s_conv_b, dt_bias, a_log, d_skip, s_norm_g, ctx_out)
    o = [*local_mix(pr3, pool_w, pool_scale, conv_w),
         na_attention(pr3, prc3, na_bias_table(rpb, S // GRID_W)), o_ssd]
    if not ctx_out:
        return o, None
    oc = [*local_mix(prc3, pool_w, pool_scale, conv_w), ctx_attention(prc3), oc_ssd]
    return o, oc


TM = 512
TM_PROJ = 1024
TM_GRP = 512


def kernel(x, c, ctx, c_ctx, w_ada, b_ada, g_mix, g_ffn, w_in, w_out, pool_w, pool_scale, conv_w, na_rpb,
           ssd_conv_w, ssd_conv_b, ssd_dt_bias, ssd_a_log, ssd_d, ssd_norm_g, ffn_w_gu, ffn_w_down,
           moe_router, moe_w_gu, moe_w_down, g_final):
    bn, S, d = x.shape
    Lc = ctx.shape[1]
    m, mc = bn * S, bn * Lc
    tpb = S // TM
    x2 = x.reshape(m, d)
    xc2 = ctx.reshape(mc, d)
    c_rows = jnp.concatenate([c, c_ctx[None, :], jnp.zeros((7, d), F32)], axis=0)
    for l in range(DEPTH):
        ctx_out = l < DEPTH - 1
        last = l == DEPTH - 1
        ada = ada_modulation(c_rows, w_ada[l].astype(BF16), b_ada[l][None, :])
        mod = ada[:bn].reshape(bn, 6, d)
        mod_c = ada[bn:bn + 1].reshape(1, 6, d)
        w_in_l = jnp.pad(w_in[l], ((0, 0), (0, D_IN_PAD - D_IN_PROJ))).astype(BF16)
        g_m = g_mix[l][None, :]
        g_f = g_ffn[l][None, :]
        pr, dt = in_proj(x2, g_m, mod, w_in_l, S // TM_PROJ, TM_PROJ)
        pr_c, dt_c = in_proj(xc2, g_m, mod_c, w_in_l, None, min(TM_PROJ, mc))
        o, oc = mixer(pr.reshape(bn, S, -1), dt.reshape(bn, S, -1), pr_c.reshape(bn, Lc, -1),
                      dt_c.reshape(bn, Lc, -1), pool_w[l], pool_scale[l], conv_w[l],
                      na_rpb[l], ssd_conv_w[l], ssd_conv_b[l], ssd_dt_bias[l], ssd_a_log[l], ssd_d[l],
                      ssd_norm_g[l], ctx_out)
        w_out_l = w_out[l].astype(BF16)
        o = [p.reshape(m, -1) for p in o]
        oc = [p.reshape(mc, -1) for p in oc] if ctx_out else None
        j = l // 2
        if l % 2 == 0:
            w_gu = ffn_w_gu[j].astype(BF16)
            w_dn = ffn_w_down[j].astype(BF16)
            x2 = ffn_dense(x2, o, g_f, mod, w_out_l, w_gu, w_dn, tpb, TM)
            if ctx_out:
                xc2 = ffn_dense(xc2, oc, g_f, mod_c, w_out_l, w_gu, w_dn, None, min(TM, mc))
        else:
            w_r = jnp.pad(moe_router[j], ((0, 0), (0, LANES - N_EXPERTS))).astype(BF16)
            w_gu = moe_w_gu[j].astype(BF16)
            w_dn = moe_w_down[j].astype(BF16)
            x2 = moe_block(x2, o, g_f, mod, w_out_l, w_r, w_gu, w_dn, tpb, TM, TM_GRP,
                           g_final[None, :] if last else None)
            if ctx_out:
                xc2 = moe_block(xc2, oc, g_f, mod_c, w_out_l, w_r, w_gu, w_dn, None, min(TM, mc), TM_GRP)
            if last:
                return x2.reshape(bn, S, d)
    return final_norm(x2, g_final[None, :], TM).reshape(bn, S, d)
```

```python
import functools
import math

import numpy as np
import jax
import jax.numpy as jnp
from jax import lax
from jax.experimental import pallas as pl
from jax.experimental.pallas import tpu as pltpu

DEPTH = 2
GRID_W = 64
EPS = 1e-6
W_POOL = 256
W_CONV = 256
W_NA = 256
W_SSD = 256
POOL_WINDOWS = (2, 4, 8, 16)
POOL_GROUP = W_POOL // len(POOL_WINDOWS)
NA_HEADS = 4
NA_HEAD_DIM = W_NA // NA_HEADS
WIN_R = 8
WIN_C = 16
SSD_HEAD_DIM = 64
SSD_HEADS = W_SSD // SSD_HEAD_DIM
SSD_GROUPS = 2
SSD_STATE = 128
SSD_CHUNK = 128
SSD_XBC = W_SSD + 2 * SSD_GROUPS * SSD_STATE
D_IN_PROJ = W_POOL + 3 * W_CONV + 3 * W_NA + W_SSD + SSD_XBC + 2 * SSD_HEADS
N_EXPERTS = 8
TOP_K = 2

LANES = 128
D_IN_PAD = -(-D_IN_PROJ // LANES) * LANES
VMEM_LIMIT = 56 * 1024 * 1024
BF16 = jnp.bfloat16
F32 = jnp.float32


def _params(*sem):
    return pltpu.CompilerParams(dimension_semantics=sem, vmem_limit_bytes=VMEM_LIMIT)


def _norm_mod(x, g, scale, shift):
    ms = jnp.mean(x * x, axis=-1, keepdims=True)
    return (x * lax.rsqrt(ms + EPS) * g) * (1.0 + scale) + shift


def _mod_map(tiles_per_batch):
    if tiles_per_batch is None:
        return lambda i, *_: (0, 0, 0)
    return lambda i, *_: (i // tiles_per_batch, 0, 0)


def _resident(shape, index_map):
    return pl.BlockSpec(shape, index_map, pipeline_mode=pl.Buffered(1))


def _ada_kernel(c_ref, w_ref, b_ref, o_ref):
    c = c_ref[...]
    s = c * jax.nn.sigmoid(c)
    o_ref[...] = jnp.dot(s.astype(BF16), w_ref[...], preferred_element_type=F32) + b_ref[...]


def ada_modulation(c_rows, w, b):
    r, d = c_rows.shape
    n = w.shape[1]
    tn = 1536
    return pl.pallas_call(
        _ada_kernel,
        out_shape=jax.ShapeDtypeStruct((r, n), F32),
        grid=(n // tn,),
        in_specs=[pl.BlockSpec((r, d), lambda j: (0, 0)),
                  pl.BlockSpec((d, tn), lambda j: (0, j)),
                  pl.BlockSpec((1, tn), lambda j: (0, j))],
        out_specs=pl.BlockSpec((r, tn), lambda j: (0, j)),
        compiler_params=_params("arbitrary"),
        name="ada_modulation",
    )(c_rows, w, b)


def _in_proj_kernel(x_ref, g_ref, mod_ref, w_ref, o_ref, dt_ref):
    h = _norm_mod(x_ref[...], g_ref[...], mod_ref[0, 1:2, :], mod_ref[0, 0:1, :])
    pr = jnp.dot(h.astype(BF16), w_ref[...], preferred_element_type=F32)
    o_ref[...] = pr.astype(o_ref.dtype)
    dt_ref[...] = pr[:, DT_COL * LANES:(DT_COL + 1) * LANES]


def in_proj(x2, g, mod, w, tiles_per_batch, tm):
    m, d = x2.shape
    n = w.shape[1]
    return pl.pallas_call(
        _in_proj_kernel,
        out_shape=(jax.ShapeDtypeStruct((m, n), BF16), jax.ShapeDtypeStruct((m, LANES), F32)),
        grid=(m // tm,),
        in_specs=[pl.BlockSpec((tm, d), lambda i: (i, 0)),
                  pl.BlockSpec((1, d), lambda i: (0, 0)),
                  pl.BlockSpec((1, 6, d), _mod_map(tiles_per_batch)),
                  _resident((d, n), lambda i: (0, 0))],
        out_specs=(pl.BlockSpec((tm, n), lambda i: (i, 0)), pl.BlockSpec((tm, LANES), lambda i: (i, 0))),
        compiler_params=_params("parallel"),
        name="in_proj",
    )(x2, g, mod, w)


def _mix_residual(x, mod_ref, w_ref, part_refs):
    y, k0 = None, 0
    for p_ref in part_refs:
        k = p_ref.shape[1]
        t = jnp.dot(p_ref[...], w_ref[k0:k0 + k, :], preferred_element_type=F32)
        y = t if y is None else y + t
        k0 += k
    return x + mod_ref[0, 2:3, :] * y


def _part_specs(parts, tm):
    return [pl.BlockSpec((tm, p.shape[1]), lambda i, *_: (i, 0)) for p in parts]


FF_CHUNK = 512


def _swiglu(h, wgu, wd, ff):
    acc = None
    for c0 in range(0, ff, FF_CHUNK):
        c1 = min(c0 + FF_CHUNK, ff)
        gg = jnp.dot(h, wgu(c0, c1), preferred_element_type=F32)
        uu = jnp.dot(h, wgu(ff + c0, ff + c1), preferred_element_type=F32)
        a = (gg * jax.nn.sigmoid(gg) * uu).astype(BF16)
        part = jnp.dot(a, wd(c0, c1), preferred_element_type=F32)
        acc = part if acc is None else acc + part
    return acc


def _ffn_kernel(x_ref, g_ref, mod_ref, wout_ref, wgu_ref, wd_ref, *refs):
    x = _mix_residual(x_ref[...], mod_ref, wout_ref, refs[:-1])
    y_ref = refs[-1]
    h = _norm_mod(x, g_ref[...], mod_ref[0, 4:5, :], mod_ref[0, 3:4, :]).astype(BF16)
    y = _swiglu(h, lambda a, b: wgu_ref[:, a:b], lambda a, b: wd_ref[a:b, :], wd_ref.shape[0])
    y_ref[...] = x + mod_ref[0, 5:6, :] * y


def ffn_dense(x2, parts, g, mod, w_out, w_gu, w_down, tiles_per_batch, tm):
    m, d = x2.shape
    return pl.pallas_call(
        _ffn_kernel,
        out_shape=jax.ShapeDtypeStruct((m, d), F32),
        grid=(m // tm,),
        in_specs=[pl.BlockSpec((tm, d), lambda i: (i, 0)),
                  pl.BlockSpec((1, d), lambda i: (0, 0)),
                  pl.BlockSpec((1, 6, d), _mod_map(tiles_per_batch)),
                  _resident(w_out.shape, lambda i: (0, 0)),
                  _resident(w_gu.shape, lambda i: (0, 0)),
                  _resident(w_down.shape, lambda i: (0, 0))] + _part_specs(parts, tm),
        out_specs=pl.BlockSpec((tm, d), lambda i: (i, 0)),
        compiler_params=_params("parallel"),
        name="ffn_dense",
    )(x2, g, mod, w_out, w_gu, w_down, *parts)


ROUTE_E1, ROUTE_E2, ROUTE_R1, ROUTE_R2, ROUTE_G1, ROUTE_G2 = range(6)


def _router_kernel(x_ref, g_ref, mod_ref, wout_ref, wr_ref, *refs):
    part_refs = refs[:-5]
    x1_ref, h_ref, route_ref, cnt_ref, base_ref = refs[-5:]

    @pl.when(pl.program_id(0) == 0)
    def _():
        base_ref[...] = jnp.zeros_like(base_ref)

    x = _mix_residual(x_ref[...], mod_ref, wout_ref, part_refs)
    x1_ref[...] = x
    h = _norm_mod(x, g_ref[...], mod_ref[0, 4:5, :], mod_ref[0, 3:4, :]).astype(BF16)
    h_ref[...] = h
    tm = h.shape[0]
    logits = jnp.dot(h, wr_ref[...], preferred_element_type=F32)
    lane = lax.broadcasted_iota(jnp.int32, logits.shape, 1)
    neg = jnp.float32(-jnp.inf)
    logits = jnp.where(lane < N_EXPERTS, logits, neg)
    m1 = jnp.max(logits, axis=-1, keepdims=True)
    i1 = jnp.min(jnp.where(logits == m1, lane, LANES), axis=-1, keepdims=True)
    rest = jnp.where(lane == i1, neg, logits)
    m2 = jnp.max(rest, axis=-1, keepdims=True)
    i2 = jnp.min(jnp.where(rest == m2, lane, LANES), axis=-1, keepdims=True)
    e2 = jnp.exp(m2 - m1)
    g1 = 1.0 / (1.0 + e2)
    g2 = e2 / (1.0 + e2)
    sel1, sel2 = lane == i1, lane == i2
    sel = jnp.where(sel1 | sel2, 1.0, 0.0)
    li = lax.broadcasted_iota(jnp.int32, (tm, tm), 0)
    si = lax.broadcasted_iota(jnp.int32, (tm, tm), 1)
    before = jnp.where(si < li, 1.0, 0.0).astype(BF16)
    rank = jnp.dot(before, sel.astype(BF16), preferred_element_type=F32) + base_ref[...]
    r1 = jnp.sum(jnp.where(sel1, rank, 0.0), axis=-1, keepdims=True)
    r2 = jnp.sum(jnp.where(sel2, rank, 0.0), axis=-1, keepdims=True)
    base_ref[...] += jnp.sum(sel, axis=0, keepdims=True)
    cnt_ref[...] = base_ref[...]
    fields = (i1.astype(F32), i2.astype(F32), r1, r2, g1, g2)
    route = jnp.zeros(logits.shape, F32)
    for k, v in enumerate(fields):
        route = jnp.where(lane == k, v, route)
    route_ref[...] = route


def moe_router(x2, parts, g, mod, w_out, w_router, tiles_per_batch, tm):
    m, d = x2.shape
    return pl.pallas_call(
        _router_kernel,
        out_shape=(jax.ShapeDtypeStruct((m, d), F32), jax.ShapeDtypeStruct((m, d), BF16),
                   jax.ShapeDtypeStruct((m, LANES), F32), jax.ShapeDtypeStruct((1, LANES), F32)),
        grid=(m // tm,),
        in_specs=[pl.BlockSpec((tm, d), lambda i: (i, 0)),
                  pl.BlockSpec((1, d), lambda i: (0, 0)),
                  pl.BlockSpec((1, 6, d), _mod_map(tiles_per_batch)),
                  _resident(w_out.shape, lambda i: (0, 0)),
                  pl.BlockSpec((d, LANES), lambda i: (0, 0))] + _part_specs(parts, tm),
        out_specs=(pl.BlockSpec((tm, d), lambda i: (i, 0)),
                   pl.BlockSpec((tm, d), lambda i: (i, 0)),
                   pl.BlockSpec((tm, LANES), lambda i: (i, 0)),
                   pl.BlockSpec((1, LANES), lambda i: (0, 0))),
        scratch_shapes=[pltpu.VMEM((1, LANES), F32)],
        compiler_params=_params("arbitrary"),
        name="moe_router",
    )(x2, g, mod, w_out, w_router, *parts)


def _moe_kernel(tile_ref, exp_ref, start_ref, end_ref, xg_ref, wgu_ref, wd_ref, y_ref):
    w = pl.program_id(0)
    tm = xg_ref.shape[0]
    start, end = start_ref[w], end_ref[w]
    t0 = tile_ref[w] * tm

    @pl.when(end > start)
    def _():
        y = _swiglu(xg_ref[...], lambda a, b: wgu_ref[0, :, a:b], lambda a, b: wd_ref[0, a:b, :], wd_ref.shape[1])
        y = y.astype(y_ref.dtype)

        @pl.when(start == t0)
        def _():
            y_ref[...] = y

        @pl.when(start != t0)
        def _():
            row = t0 + lax.broadcasted_iota(jnp.int32, (tm, 1), 0)
            y_ref[...] = jnp.where(row >= start, y, y_ref[...])


def moe_grouped(item_tile, item_expert, item_start, item_end, xg, w_gu, w_down, tm):
    p, d = xg.shape
    grid_spec = pltpu.PrefetchScalarGridSpec(
        num_scalar_prefetch=4,
        grid=(item_tile.shape[0],),
        in_specs=[pl.BlockSpec((tm, d), lambda w, it, ie, s, e: (it[w], 0)),
                  _resident((1,) + w_gu.shape[1:], lambda w, it, ie, s, e: (ie[w], 0, 0)),
                  _resident((1,) + w_down.shape[1:], lambda w, it, ie, s, e: (ie[w], 0, 0))],
        out_specs=pl.BlockSpec((tm, d), lambda w, it, ie, s, e: (it[w], 0)),
    )
    return pl.pallas_call(
        _moe_kernel,
        out_shape=jax.ShapeDtypeStruct((p, d), BF16),
        grid_spec=grid_spec,
        compiler_params=_params("arbitrary"),
        name="moe_grouped",
    )(item_tile, item_expert, item_start, item_end, xg, w_gu, w_down)


def _moe_combine_kernel(x_ref, ya_ref, yb_ref, route_ref, mod_ref, *refs):
    o_ref = refs[-1]
    r = route_ref[...]
    y = (r[:, ROUTE_G1:ROUTE_G1 + 1] * ya_ref[...].astype(F32) + r[:, ROUTE_G2:ROUTE_G2 + 1] * yb_ref[...].astype(F32))
    x = x_ref[...] + mod_ref[0, 5:6, :] * y
    if len(refs) == 2:
        ms = jnp.mean(x * x, axis=-1, keepdims=True)
        x = x * lax.rsqrt(ms + EPS) * refs[0][...]
    o_ref[...] = x


def moe_combine(x2, ya, yb, route, mod, tiles_per_batch, tm, g_final=None):
    m, d = x2.shape
    row = pl.BlockSpec((tm, d), lambda i: (i, 0))
    specs = [row, row, row, pl.BlockSpec((tm, LANES), lambda i: (i, 0)),
             pl.BlockSpec((1, 6, d), _mod_map(tiles_per_batch))]
    args = [x2, ya, yb, route, mod]
    if g_final is not None:
        specs.append(pl.BlockSpec((1, d), lambda i: (0, 0)))
        args.append(g_final)
    return pl.pallas_call(
        _moe_combine_kernel,
        out_shape=jax.ShapeDtypeStruct((m, d), F32),
        grid=(m // tm,),
        in_specs=specs,
        out_specs=row,
        compiler_params=_params("parallel"),
        name="moe_combine",
    )(*args)


def moe_block(x2, parts, g, mod, w_out, w_router, w_gu, w_down, tiles_per_batch, tm_tok, tm_grp, g_final=None):
    m, d = x2.shape
    x2, h, route, cnt = moe_router(x2, parts, g, mod, w_out, w_router, tiles_per_batch, tm_tok)
    cnt = cnt[0, :N_EXPERTS].astype(jnp.int32)
    p = m * TOP_K
    off = jnp.cumsum(cnt) - cnt
    ri = route[:, :ROUTE_G1].astype(jnp.int32)
    onehot = lambda e: (e[:, None] == jnp.arange(N_EXPERTS)[None, :]).astype(jnp.int32)
    d1 = jnp.sum(onehot(ri[:, ROUTE_E1]) * off[None, :], axis=1) + ri[:, ROUTE_R1]
    d2 = jnp.sum(onehot(ri[:, ROUTE_E2]) * off[None, :], axis=1) + ri[:, ROUTE_R2]
    tok = jnp.arange(m, dtype=jnp.int32)
    _, src = lax.sort_key_val(jnp.concatenate([d1, d2]), jnp.concatenate([tok, tok]))
    n_row_tiles = p // tm_grp
    start = jnp.sort(jnp.concatenate([jnp.arange(n_row_tiles, dtype=jnp.int32) * tm_grp, off]))
    end = jnp.concatenate([start[1:], jnp.full((1,), p, jnp.int32)])
    item_tile = jnp.minimum(start // tm_grp, n_row_tiles - 1)
    item_expert = jnp.sum((off[None, :] <= start[:, None]).astype(jnp.int32), axis=1) - 1
    rows = lambda a, idx: a.at[idx].get(mode="promise_in_bounds")
    yg = moe_grouped(item_tile, item_expert, start, end, rows(h, src), w_gu, w_down, tm_grp)
    return moe_combine(x2, rows(yg, d1), rows(yg, d2), route, mod, tiles_per_batch, tm_tok, g_final)


def _final_norm_kernel(x_ref, g_ref, o_ref):
    x = x_ref[...]
    ms = jnp.mean(x * x, axis=-1, keepdims=True)
    o_ref[...] = x * lax.rsqrt(ms + EPS) * g_ref[...]


def final_norm(x2, g, tm):
    m, d = x2.shape
    return pl.pallas_call(
        _final_norm_kernel,
        out_shape=jax.ShapeDtypeStruct((m, d), F32),
        grid=(m // tm,),
        in_specs=[pl.BlockSpec((tm, d), lambda i: (i, 0)), pl.BlockSpec((1, d), lambda i: (0, 0))],
        out_specs=pl.BlockSpec((tm, d), lambda i: (i, 0)),
        compiler_params=_params("parallel"),
        name="final_norm",
    )(x2, g)


NA_QROWS = 4
NA_KROWS = NA_QROWS + WIN_R
Q_COL, K_COL, V_COL = 4, 5, 6


def _na_key_start(j, rows):
    return np.clip(j * NA_QROWS - WIN_R // 2, 0, rows - NA_KROWS)


def _na_bias_index(rows):
    nblk = rows // NA_QROWS
    pats = []
    for j in range(nblk):
        start = _na_key_start(j, rows)
        r = j * NA_QROWS + np.arange(NA_QROWS)
        sr = np.clip(r - WIN_R // 2, 0, rows - WIN_R)
        kr = start + np.arange(NA_KROWS)
        rvalid = (kr[None, :] >= sr[:, None]) & (kr[None, :] < sr[:, None] + WIN_R)
        ri = np.clip(kr[None, :] - r[:, None] + WIN_R - 1, 0, 2 * WIN_R - 2)
        pats.append((ri, rvalid))
    for j in range(2, nblk - 1):
        assert all(np.array_equal(a, b) for a, b in zip(pats[1], pats[j]))
    sel = [pats[0], pats[1], pats[nblk - 1]]
    ri = np.stack([p[0] for p in sel])
    rvalid = np.stack([p[1] for p in sel])
    c = np.arange(GRID_W)
    sc = np.clip(c - WIN_C // 2, 0, GRID_W - WIN_C)
    cvalid = (c[None, :] >= sc[:, None]) & (c[None, :] < sc[:, None] + WIN_C)
    ci = np.clip(c[None, :] - c[:, None] + WIN_C - 1, 0, 2 * WIN_C - 2)
    c_onehot = (ci[..., None] == np.arange(2 * WIN_C - 1)).astype(np.float32)
    valid = rvalid[:, :, None, :, None] & cvalid[None, None, :, None, :]
    return ri, c_onehot, valid


def na_bias_table(rpb, rows):
    ri, c_onehot, valid = _na_bias_index(rows)
    toep = jnp.einsum('hrd,qkd->hrqk', rpb, c_onehot, precision=lax.Precision.HIGHEST)
    b = jnp.take(toep, ri.reshape(-1), axis=1).reshape((NA_HEADS,) + ri.shape + (GRID_W, GRID_W))
    b = jnp.transpose(b, (1, 0, 2, 4, 3, 5))
    b = jnp.where(valid[:, None], b, -jnp.inf)
    return b.reshape(3, NA_HEADS, NA_QROWS * GRID_W, NA_KROWS * GRID_W).astype(F32)


def _attend_heads(q, key_sets, bias_fn):
    n, w = q.shape
    lane = lax.broadcasted_iota(jnp.int32, (1, w), 1)
    out = jnp.zeros((n, w), F32)
    for h in range(NA_HEADS):
        mh = (lane >= h * NA_HEAD_DIM) & (lane < (h + 1) * NA_HEAD_DIM)
        qh = jnp.where(mh, q, 0.0).astype(BF16)
        scores = []
        for i, (k, _) in enumerate(key_sets):
            s = lax.dot_general(qh, k, (((1,), (1,)), ((), ())), preferred_element_type=F32)
            b = bias_fn(i, h)
            scores.append(s if b is None else s + b)
        m = scores[0].max(axis=-1, keepdims=True)
        for s in scores[1:]:
            m = jnp.maximum(m, s.max(axis=-1, keepdims=True))
        denom = jnp.zeros((n, 1), F32)
        acc = jnp.zeros((n, w), F32)
        for s, (_, v) in zip(scores, key_sets):
            p = jnp.exp(s - m)
            denom = denom + p.sum(axis=-1, keepdims=True)
            acc = acc + jnp.dot(p.astype(BF16), v, preferred_element_type=F32)
        out = out + jnp.where(mh, acc / denom, 0.0)
    return out


def _na_kernel(q_ref, k_ref, v_ref, kc_ref, vc_ref, bias_ref, o_ref, *, rows):
    j = pl.program_id(1)
    start = jnp.clip(j * NA_QROWS - WIN_R // 2, 0, rows - NA_KROWS)
    t0 = pl.multiple_of(start * GRID_W, GRID_W)
    nk = NA_KROWS * GRID_W
    kw = k_ref[0, pl.ds(t0, nk), :]
    vw = v_ref[0, pl.ds(t0, nk), :]
    q = q_ref[0] * (1.0 / math.sqrt(NA_HEAD_DIM))
    o = _attend_heads(q, [(kw, vw), (kc_ref[0], vc_ref[0])], lambda i, h: bias_ref[0, h] if i == 0 else None)
    o_ref[0] = o.astype(o_ref.dtype)


def na_attention(pr3, prc3, bias):
    bn, S, _ = pr3.shape
    Lc = prc3.shape[1]
    rows = S // GRID_W
    nblk = rows // NA_QROWS
    nq = NA_QROWS * GRID_W

    def pat(b, j):
        return (jnp.where(j == 0, 0, jnp.where(j == nblk - 1, 2, 1)), 0, 0, 0)

    return pl.pallas_call(
        functools.partial(_na_kernel, rows=rows),
        out_shape=jax.ShapeDtypeStruct((bn, S, W_NA), BF16),
        grid=(bn, nblk),
        in_specs=[pl.BlockSpec((1, nq, W_NA), lambda b, j: (b, j, Q_COL)),
                  pl.BlockSpec((1, S, W_NA), lambda b, j: (b, 0, K_COL)),
                  pl.BlockSpec((1, S, W_NA), lambda b, j: (b, 0, V_COL)),
                  pl.BlockSpec((1, Lc, W_NA), lambda b, j: (b, 0, K_COL)),
                  pl.BlockSpec((1, Lc, W_NA), lambda b, j: (b, 0, V_COL)),
                  pl.BlockSpec((1,) + bias.shape[1:], pat)],
        out_specs=pl.BlockSpec((1, nq, W_NA), lambda b, j: (b, j, 0)),
        compiler_params=_params("parallel", "arbitrary"),
        name="na_attention",
    )(pr3, pr3, pr3, prc3, prc3, bias)


def _ctx_attn_kernel(q_ref, k_ref, v_ref, o_ref):
    q = q_ref[0] * (1.0 / math.sqrt(NA_HEAD_DIM))
    o = _attend_heads(q, [(k_ref[0], v_ref[0])], lambda i, h: None)
    o_ref[0] = o.astype(o_ref.dtype)


def ctx_attention(prc3):
    bn, Lc, _ = prc3.shape
    return pl.pallas_call(
        _ctx_attn_kernel,
        out_shape=jax.ShapeDtypeStruct((bn, Lc, W_NA), BF16),
        grid=(bn,),
        in_specs=[pl.BlockSpec((1, Lc, W_NA), lambda b: (b, 0, Q_COL)),
                  pl.BlockSpec((1, Lc, W_NA), lambda b: (b, 0, K_COL)),
                  pl.BlockSpec((1, Lc, W_NA), lambda b: (b, 0, V_COL))],
        out_specs=pl.BlockSpec((1, Lc, W_NA), lambda b: (b, 0, 0)),
        compiler_params=_params("parallel"),
        name="ctx_attention",
    )(prc3, prc3, prc3)


POOL_COL, CONV_H_COL, CONV_B_COL, CONV_C_COL = 0, 1, 2, 3


def _shift_rows(x, k, row):
    n = x.shape[0]
    y = pltpu.roll(x, k % n, 0)
    return jnp.where(row < k, 0.0, y) if k > 0 else jnp.where(row >= n + k, 0.0, y)


def _local_mix_kernel(u_ref, h_ref, bg_ref, cg_ref, pw_ref, ps_ref, cw_ref, op_ref, oc_ref, *, seq):
    row = lax.broadcasted_iota(jnp.int32, (seq, 1), 0)
    lane = lax.broadcasted_iota(jnp.int32, (1, W_POOL), 1)
    u = u_ref[0].astype(F32)
    trailing, leading = {1: u}, {1: u}
    for w in (1, 2, 4):
        trailing[2 * w] = trailing[w] + _shift_rows(trailing[w], w, row)
        leading[2 * w] = leading[w] + _shift_rows(leading[w], -w, row)
    rowf = row.astype(F32)
    win_sum = jnp.zeros_like(u)
    cnt = jnp.zeros_like(u)
    for g, win in enumerate(POOL_WINDOWS):
        half = win // 2
        mg = (lane >= g * POOL_GROUP) & (lane < (g + 1) * POOL_GROUP)
        s = _shift_rows(trailing[half], 1, row) + leading[half]
        n = jnp.minimum(rowf + half, float(seq)) - jnp.maximum(rowf - half, 0.0)
        win_sum = jnp.where(mg, s, win_sum)
        cnt = jnp.where(mg, n, cnt)
    p = win_sum / cnt - u
    pooled = jnp.dot(p.astype(BF16), pw_ref[...], preferred_element_type=F32) * ps_ref[...]
    op_ref[0] = pooled.astype(op_ref.dtype)
    v = cg_ref[0].astype(F32) * h_ref[0].astype(F32)
    conv = (_shift_rows(v, 1, row) * cw_ref[0:1, :] + v * cw_ref[1:2, :] + _shift_rows(v, -1, row) * cw_ref[2:3, :])
    oc_ref[0] = (bg_ref[0].astype(F32) * conv).astype(oc_ref.dtype)


def local_mix(pr3, pool_w, pool_scale, conv_w):
    bn, seq, _ = pr3.shape
    pw = jax.scipy.linalg.block_diag(*[pool_w[g] for g in range(len(POOL_WINDOWS))]).astype(BF16)
    blk = lambda col: pl.BlockSpec((1, seq, W_POOL), lambda b: (b, 0, col))
    full = lambda a: pl.BlockSpec(a.shape, lambda b: (0,) * a.ndim)
    args = (pw, pool_scale[None, :], conv_w)
    return pl.pallas_call(
        functools.partial(_local_mix_kernel, seq=seq),
        out_shape=(jax.ShapeDtypeStruct((bn, seq, W_POOL), BF16), jax.ShapeDtypeStruct((bn, seq, W_CONV), BF16)),
        grid=(bn,),
        in_specs=[blk(POOL_COL), blk(CONV_H_COL), blk(CONV_B_COL), blk(CONV_C_COL)] + [full(a) for a in args],
        out_specs=(pl.BlockSpec((1, seq, W_POOL), lambda b: (b, 0, 0)),
                   pl.BlockSpec((1, seq, W_CONV), lambda b: (b, 0, 0))),
        compiler_params=_params("parallel"),
        name="local_mix",
    )(pr3, pr3, pr3, pr3, *args)


Z_COL, XS_COL, BS_COL, CS_COL = 7, 8, 9, 10
DT_COL = 22
HEADS_PER_GROUP = SSD_HEADS // SSD_GROUPS
GROUP_W = HEADS_PER_GROUP * SSD_HEAD_DIM


def _head_mask(h):
    lane = lax.broadcasted_iota(jnp.int32, (1, W_SSD), 1)
    return (lane >= h * SSD_HEAD_DIM) & (lane < (h + 1) * SSD_HEAD_DIM)


def _expand_heads(v, d):
    out = jnp.zeros((v.shape[0], W_SSD), F32)
    for h in range(SSD_HEADS):
        k = d * SSD_HEADS + h
        out = jnp.where(_head_mask(h), v[:, k:k + 1], out)
    return out


def _ssd_chunk_terms(c, xs_ref, b_ref, c_ref, dt_ref, la_ref, dsk_ref, y_ref, upd_ref, dec_ref, need_y):
    T = SSD_CHUNK
    r0 = pl.multiple_of(c * T, T)
    xs = xs_ref[pl.ds(r0, T), :]
    bm = b_ref[pl.ds(r0, T), :]
    cm = c_ref[pl.ds(r0, T), :]
    dt = dt_ref[pl.ds(r0, T), :]
    la = la_ref[pl.ds(r0, T), :]
    li = lax.broadcasted_iota(jnp.int32, (T, T), 0)
    si = lax.broadcasted_iota(jnp.int32, (T, T), 1)
    causal = si <= li
    prefix = jnp.dot(causal.astype(F32), la, precision=lax.Precision.HIGHEST, preferred_element_type=F32)
    total = prefix[T - 1:T, :]
    lane = lax.broadcasted_iota(jnp.int32, (1, LANES), 1)
    acum = jnp.where(lane < SSD_HEADS, prefix, total - prefix + la)
    bt = bm.astype(F32).T.astype(BF16)
    if need_y:
        acum_t = acum.T
        cb = [lax.dot_general(cm[:, g * SSD_STATE:(g + 1) * SSD_STATE], bm[:, g * SSD_STATE:(g + 1) * SSD_STATE],
                              (((1,), (1,)), ((), ())), preferred_element_type=F32) for g in range(SSD_GROUPS)]
        y = jnp.zeros((T, W_SSD), F32)
    for d in range(2):
        acum_e = _expand_heads(acum, d)
        tot_e = _expand_heads(total, d)
        xdt = xs * _expand_heads(dt, d)
        xw = (xdt * jnp.exp(tot_e - acum_e)).astype(BF16)
        upd = [jnp.dot(bt[g * SSD_STATE:(g + 1) * SSD_STATE, :], xw[:, g * GROUP_W:(g + 1) * GROUP_W],
                       preferred_element_type=F32) for g in range(SSD_GROUPS)]
        upd_ref[c, d] = jnp.concatenate(upd, axis=1)
        dec_ref[c, d, 0:T, :] = jnp.exp(acum_e)
        dec_ref[c, d, T:T + 1, :] = jnp.exp(tot_e)
        if need_y:
            mask = causal if d == 0 else (si >= li)
            xdt_b = xdt.astype(BF16)
            for h in range(SSD_HEADS):
                k = d * SSD_HEADS + h
                decay = jnp.exp(jnp.where(mask, acum[:, k:k + 1] - acum_t[k:k + 1, :], -jnp.inf))
                scores = (cb[h // HEADS_PER_GROUP] * decay).astype(BF16)
                y = y + jnp.where(_head_mask(h), jnp.dot(scores, xdt_b, preferred_element_type=F32), 0.0)
    if need_y:
        y_ref[pl.ds(r0, T), :] = xs * dsk_ref[...] + y


def _ssd_chunk_state(c, d, c_ref, y_ref, upd_ref, dec_ref, st_ref, need_y):
    T = SSD_CHUNK
    r0 = pl.multiple_of(c * T, T)
    st = st_ref[d]
    if need_y:
        cm = c_ref[pl.ds(r0, T), :]
        st_b = st.astype(BF16)
        ys = [jnp.dot(cm[:, g * SSD_STATE:(g + 1) * SSD_STATE], st_b[:, g * GROUP_W:(g + 1) * GROUP_W],
                      preferred_element_type=F32) for g in range(SSD_GROUPS)]
        y_ref[pl.ds(r0, T), :] += jnp.concatenate(ys, axis=1) * dec_ref[c, d, 0:T, :]
    st_ref[d] = dec_ref[c, d, T:T + 1, :] * st + upd_ref[c, d]


def _ssd_kernel(z_ref, xs_in, bs_in, cs_in, dtr_ref, cw_ref, cb_ref, dtb_ref, alog_ref, dsk_ref, ng_ref, h0_ref,
                *refs, seq, need_y):
    if need_y:
        o_ref, hT_ref, xs_ref, b_ref, c_ref, dt_ref, la_ref, upd_ref, dec_ref, st_ref, y_ref = refs
    else:
        hT_ref, xs_ref, b_ref, c_ref, dt_ref, la_ref, upd_ref, dec_ref, st_ref = refs
        y_ref = None
    nc = seq // SSD_CHUNK
    row = lax.broadcasted_iota(jnp.int32, (seq, 1), 0)
    for gi, (src, dst) in enumerate(((xs_in, xs_ref), (bs_in, b_ref), (cs_in, c_ref))):
        sl = slice(gi * W_SSD, (gi + 1) * W_SSD)
        x = src[0].astype(F32)
        cv = (_shift_rows(x, 1, row) * cw_ref[0:1, sl] + x * cw_ref[1:2, sl]
              + _shift_rows(x, -1, row) * cw_ref[2:3, sl] + cb_ref[:, sl])
        dst[...] = (cv * jax.nn.sigmoid(cv)).astype(dst.dtype)
    dtv = dtr_ref[0] + dtb_ref[...]
    dt = jnp.maximum(dtv, 0.0) + jnp.log1p(jnp.exp(-jnp.abs(dtv)))
    dt_ref[...] = dt
    la_ref[...] = dt * (-jnp.exp(alog_ref[...]))
    st_ref[...] = h0_ref[0]

    def terms(c, carry):
        _ssd_chunk_terms(c, xs_ref, b_ref, c_ref, dt_ref, la_ref, dsk_ref, y_ref, upd_ref, dec_ref, need_y)
        return carry

    lax.fori_loop(0, nc, terms, 0, unroll=2)

    def recur(i, carry):
        _ssd_chunk_state(i, 0, c_ref, y_ref, upd_ref, dec_ref, st_ref, need_y)
        _ssd_chunk_state(nc - 1 - i, 1, c_ref, y_ref, upd_ref, dec_ref, st_ref, need_y)
        return carry

    lax.fori_loop(0, nc, recur, 0, unroll=2)
    hT_ref[0] = st_ref[...]
    if need_y:
        z = z_ref[0].astype(F32)
        yz = y_ref[...] * (z * jax.nn.sigmoid(z))
        ms = jnp.mean(yz * yz, axis=-1, keepdims=True)
        o_ref[0] = (yz * lax.rsqrt(ms + EPS) * ng_ref[...]).astype(o_ref.dtype)


def ssd_scan(pr3, dt3, h0, conv_w, conv_b, dt_bias, a_log, d_skip, norm_g, need_y):
    bn, seq, _ = pr3.shape
    nc = seq // SSD_CHUNK
    pad = LANES - 2 * SSD_HEADS
    dtb = jnp.pad(dt_bias.reshape(1, -1), ((0, 0), (0, pad)))
    alog = jnp.pad(a_log.reshape(1, -1), ((0, 0), (0, pad)))
    dsk = jnp.repeat(d_skip, SSD_HEAD_DIM)[None, :]
    blk = lambda col: pl.BlockSpec((1, seq, W_SSD), lambda b: (b, 0, col))
    full = lambda a: pl.BlockSpec(a.shape, lambda b: (0,) * a.ndim)
    st_shape = (1, 2, SSD_STATE, W_SSD)
    st_spec = pl.BlockSpec(st_shape, lambda b: (b, 0, 0, 0))
    args = (conv_w, conv_b[None, :], dtb, alog, dsk, norm_g[None, :])
    out_shape = [jax.ShapeDtypeStruct((bn,) + st_shape[1:], F32)]
    out_specs = [st_spec]
    scratch = [pltpu.VMEM((seq, W_SSD), F32),
               pltpu.VMEM((seq, W_SSD), BF16), pltpu.VMEM((seq, W_SSD), BF16),
               pltpu.VMEM((seq, LANES), F32), pltpu.VMEM((seq, LANES), F32),
               pltpu.VMEM((nc, 2, SSD_STATE, W_SSD), F32),
               pltpu.VMEM((nc, 2, SSD_CHUNK + 8, W_SSD), F32),
               pltpu.VMEM(st_shape[1:], F32)]
    if need_y:
        out_shape.insert(0, jax.ShapeDtypeStruct((bn, seq, W_SSD), BF16))
        out_specs.insert(0, pl.BlockSpec((1, seq, W_SSD), lambda b: (b, 0, 0)))
        scratch.append(pltpu.VMEM((seq, W_SSD), F32))
    res = pl.pallas_call(
        functools.partial(_ssd_kernel, seq=seq, need_y=need_y),
        out_shape=out_shape,
        grid=(bn,),
        in_specs=[blk(Z_COL), blk(XS_COL), blk(BS_COL), blk(CS_COL),
                  pl.BlockSpec((1, seq, LANES), lambda b: (b, 0, 0))]
                 + [full(a) for a in args] + [st_spec],
        out_specs=out_specs,
        scratch_shapes=scratch,
        compiler_params=_params("parallel"),
        name="ssd_scan",
    )(pr3, pr3, pr3, pr3, dt3, *args, h0)
    return (res[0], res[1]) if need_y else (None, res[0])


def ssd_mix(pr3, dt3, prc3, dtc3, conv_w, conv_b, dt_bias, a_log, d_skip, norm_g, ctx_out):
    bn = pr3.shape[0]
    h0 = jnp.zeros((bn, 2, SSD_STATE, W_SSD), F32)
    oc, hc = ssd_scan(prc3, dtc3, h0, conv_w, conv_b, dt_bias, a_log, d_skip, norm_g, ctx_out)
    o, _ = ssd_scan(pr3, dt3, hc, conv_w, conv_b, dt_bias, a_log, d_skip, norm_g, True)
    return o, oc


def mixer(pr3, dt3, prc3, dtc3, pool_w, pool_scale, conv_w, rpb, s_conv_w, s_conv_b, dt_bias, a_log, d_skip,
          s_norm_g, ctx_out):
    bn, S, _ = pr3.shape
    o_ssd, oc_ssd = ssd_mix(pr3, dt3, prc3, dtc3, s_conv_w, s_conv_b, dt_bias, a_log, d_skip, s_norm_g, ctx_out)
    o = [*local_mix(pr3, pool_w, pool_scale, conv_w),
         na_attention(pr3, prc3, na_bias_table(rpb, S // GRID_W)), o_ssd]
    if not ctx_out:
        return o, None
    oc = [*local_mix(prc3, pool_w, pool_scale, conv_w), ctx_attention(prc3), oc_ssd]
    return o, oc


TM = 512
TM_PROJ = 1024
TM_GRP = 512


def kernel(x, c, ctx, c_ctx, w_ada, b_ada, g_mix, g_ffn, w_in, w_out, pool_w, pool_scale, conv_w, na_rpb,
           ssd_conv_w, ssd_conv_b, ssd_dt_bias, ssd_a_log, ssd_d, ssd_norm_g, ffn_w_gu, ffn_w_down,
           moe_router, moe_w_gu, moe_w_down, g_final):
    bn, S, d = x.shape
    Lc = ctx.shape[1]
    m, mc = bn * S, bn * Lc
    tpb = S // TM
    x2 = x.reshape(m, d)
    xc2 = ctx.reshape(mc, d)
    c_rows = jnp.concatenate([c, c_ctx[None, :], jnp.zeros((7, d), F32)], axis=0)
    for l in range(DEPTH):
        ctx_out = l < DEPTH - 1
        last = l == DEPTH - 1
        ada = ada_modulation(c_rows, w_ada[l].astype(BF16), b_ada[l][None, :])
        mod = ada[:bn].reshape(bn, 6, d)
        mod_c = ada[bn:bn + 1].reshape(1, 6, d)
        w_in_l = jnp.pad(w_in[l], ((0, 0), (0, D_IN_PAD - D_IN_PROJ))).astype(BF16)
        g_m = g_mix[l][None, :]
        g_f = g_ffn[l][None, :]
        pr, dt = in_proj(x2, g_m, mod, w_in_l, S // TM_PROJ, TM_PROJ)
        pr_c, dt_c = in_proj(xc2, g_m, mod_c, w_in_l, None, min(TM_PROJ, mc))
        o, oc = mixer(pr.reshape(bn, S, -1), dt.reshape(bn, S, -1), pr_c.reshape(bn, Lc, -1),
                      dt_c.reshape(bn, Lc, -1), pool_w[l], pool_scale[l], conv_w[l],
                      na_rpb[l], ssd_conv_w[l], ssd_conv_b[l], ssd_dt_bias[l], ssd_a_log[l], ssd_d[l],
                      ssd_norm_g[l], ctx_out)
        w_out_l = w_out[l].astype(BF16)
        o = [p.reshape(m, -1) for p in o]
        oc = [p.reshape(mc, -1) for p in oc] if ctx_out else None
        j = l // 2
        if l % 2 == 0:
            w_gu = ffn_w_gu[j].astype(BF16)
            w_dn = ffn_w_down[j].astype(BF16)
            x2 = ffn_dense(x2, o, g_f, mod, w_out_l, w_gu, w_dn, tpb, TM)
            if ctx_out:
                xc2 = ffn_dense(xc2, oc, g_f, mod_c, w_out_l, w_gu, w_dn, None, min(TM, mc))
        else:
            w_r = jnp.pad(moe_router[j], ((0, 0), (0, LANES - N_EXPERTS))).astype(BF16)
            w_gu = moe_w_gu[j].astype(BF16)
            w_dn = moe_w_down[j].astype(BF16)
            x2 = moe_block(x2, o, g_f, mod, w_out_l, w_r, w_gu, w_dn, tpb, TM, TM_GRP,
                           g_final[None, :] if last else None)
            if ctx_out:
                xc2 = moe_block(xc2, oc, g_f, mod_c, w_out_l, w_r, w_gu, w_dn, None, min(TM, mc), TM_GRP)
            if last:
                return x2.reshape(bn, S, d)
    return final_norm(x2, g_final[None, :], TM).reshape(bn, S, d)
```

```python
import functools
import math

import numpy as np
import jax
import jax.numpy as jnp
from jax import lax
from jax.experimental import pallas as pl
from jax.experimental.pallas import tpu as pltpu

DEPTH = 2
GRID_W = 64
EPS = 1e-6
W_POOL = 256
W_CONV = 256
W_NA = 256
W_SSD = 256
POOL_WINDOWS = (2, 4, 8, 16)
POOL_GROUP = W_POOL // len(POOL_WINDOWS)
NA_HEADS = 4
NA_HEAD_DIM = W_NA // NA_HEADS
WIN_R = 8
WIN_C = 16
SSD_HEAD_DIM = 64
SSD_HEADS = W_SSD // SSD_HEAD_DIM
SSD_GROUPS = 2
SSD_STATE = 128
SSD_CHUNK = 128
SSD_XBC = W_SSD + 2 * SSD_GROUPS * SSD_STATE
D_IN_PROJ = W_POOL + 3 * W_CONV + 3 * W_NA + W_SSD + SSD_XBC + 2 * SSD_HEADS
N_EXPERTS = 8
TOP_K = 2

LANES = 128
D_IN_PAD = -(-D_IN_PROJ // LANES) * LANES
VMEM_LIMIT = 56 * 1024 * 1024
BF16 = jnp.bfloat16
F32 = jnp.float32


def _params(*sem):
    return pltpu.CompilerParams(dimension_semantics=sem, vmem_limit_bytes=VMEM_LIMIT)


def _norm_mod(x, g, scale, shift):
    ms = jnp.mean(x * x, axis=-1, keepdims=True)
    return (x * lax.rsqrt(ms + EPS) * g) * (1.0 + scale) + shift


def _mod_map(tiles_per_batch):
    if tiles_per_batch is None:
        return lambda i, *_: (0, 0, 0)
    return lambda i, *_: (i // tiles_per_batch, 0, 0)


def _resident(shape, index_map):
    return pl.BlockSpec(shape, index_map, pipeline_mode=pl.Buffered(1))


def _ada_kernel(c_ref, w_ref, b_ref, o_ref):
    c = c_ref[...]
    s = c * jax.nn.sigmoid(c)
    o_ref[...] = jnp.dot(s.astype(BF16), w_ref[...], preferred_element_type=F32) + b_ref[...]


def ada_modulation(c_rows, w, b):
    r, d = c_rows.shape
    n = w.shape[1]
    tn = 1536
    return pl.pallas_call(
        _ada_kernel,
        out_shape=jax.ShapeDtypeStruct((r, n), F32),
        grid=(n // tn,),
        in_specs=[pl.BlockSpec((r, d), lambda j: (0, 0)),
                  pl.BlockSpec((d, tn), lambda j: (0, j)),
                  pl.BlockSpec((1, tn), lambda j: (0, j))],
        out_specs=pl.BlockSpec((r, tn), lambda j: (0, j)),
        compiler_params=_params("arbitrary"),
        name="ada_modulation",
    )(c_rows, w, b)


def _in_proj_kernel(x_ref, g_ref, mod_ref, w_ref, o_ref, dt_ref):
    h = _norm_mod(x_ref[...], g_ref[...], mod_ref[0, 1:2, :], mod_ref[0, 0:1, :])
    pr = jnp.dot(h.astype(BF16), w_ref[...], preferred_element_type=F32)
    o_ref[...] = pr.astype(o_ref.dtype)
    dt_ref[...] = pr[:, DT_COL * LANES:(DT_COL + 1) * LANES]


def in_proj(x2, g, mod, w, tiles_per_batch, tm):
    m, d = x2.shape
    n = w.shape[1]
    return pl.pallas_call(
        _in_proj_kernel,
        out_shape=(jax.ShapeDtypeStruct((m, n), BF16), jax.ShapeDtypeStruct((m, LANES), F32)),
        grid=(m // tm,),
        in_specs=[pl.BlockSpec((tm, d), lambda i: (i, 0)),
                  pl.BlockSpec((1, d), lambda i: (0, 0)),
                  pl.BlockSpec((1, 6, d), _mod_map(tiles_per_batch)),
                  _resident((d, n), lambda i: (0, 0))],
        out_specs=(pl.BlockSpec((tm, n), lambda i: (i, 0)), pl.BlockSpec((tm, LANES), lambda i: (i, 0))),
        compiler_params=_params("parallel"),
        name="in_proj",
    )(x2, g, mod, w)


def _mix_residual(x, mod_ref, w_ref, part_refs):
    y, k0 = None, 0
    for p_ref in part_refs:
        k = p_ref.shape[1]
        t = jnp.dot(p_ref[...], w_ref[k0:k0 + k, :], preferred_element_type=F32)
        y = t if y is None else y + t
        k0 += k
    return x + mod_ref[0, 2:3, :] * y


def _part_specs(parts, tm):
    return [pl.BlockSpec((tm, p.shape[1]), lambda i, *_: (i, 0)) for p in parts]


FF_CHUNK = 512


def _swiglu(h, wgu, wd, ff):
    acc = None
    for c0 in range(0, ff, FF_CHUNK):
        c1 = min(c0 + FF_CHUNK, ff)
        gg = jnp.dot(h, wgu(c0, c1), preferred_element_type=F32)
        uu = jnp.dot(h, wgu(ff + c0, ff + c1), preferred_element_type=F32)
        a = (gg * jax.nn.sigmoid(gg) * uu).astype(BF16)
        part = jnp.dot(a, wd(c0, c1), preferred_element_type=F32)
        acc = part if acc is None else acc + part
    return acc


def _ffn_kernel(x_ref, g_ref, mod_ref, wout_ref, wgu_ref, wd_ref, *refs):
    x = _mix_residual(x_ref[...], mod_ref, wout_ref, refs[:-1])
    y_ref = refs[-1]
    h = _norm_mod(x, g_ref[...], mod_ref[0, 4:5, :], mod_ref[0, 3:4, :]).astype(BF16)
    y = _swiglu(h, lambda a, b: wgu_ref[:, a:b], lambda a, b: wd_ref[a:b, :], wd_ref.shape[0])
    y_ref[...] = x + mod_ref[0, 5:6, :] * y


def ffn_dense(x2, parts, g, mod, w_out, w_gu, w_down, tiles_per_batch, tm):
    m, d = x2.shape
    return pl.pallas_call(
        _ffn_kernel,
        out_shape=jax.ShapeDtypeStruct((m, d), F32),
        grid=(m // tm,),
        in_specs=[pl.BlockSpec((tm, d), lambda i: (i, 0)),
                  pl.BlockSpec((1, d), lambda i: (0, 0)),
                  pl.BlockSpec((1, 6, d), _mod_map(tiles_per_batch)),
                  _resident(w_out.shape, lambda i: (0, 0)),
                  _resident(w_gu.shape, lambda i: (0, 0)),
                  _resident(w_down.shape, lambda i: (0, 0))] + _part_specs(parts, tm),
        out_specs=pl.BlockSpec((tm, d), lambda i: (i, 0)),
        compiler_params=_params("parallel"),
        name="ffn_dense",
    )(x2, g, mod, w_out, w_gu, w_down, *parts)


ROUTE_E1, ROUTE_E2, ROUTE_R1, ROUTE_R2, ROUTE_G1, ROUTE_G2 = range(6)


def _router_kernel(x_ref, g_ref, mod_ref, wout_ref, wr_ref, *refs):
    part_refs = refs[:-5]
    x1_ref, h_ref, route_ref, cnt_ref, base_ref = refs[-5:]

    @pl.when(pl.program_id(0) == 0)
    def _():
        base_ref[...] = jnp.zeros_like(base_ref)

    x = _mix_residual(x_ref[...], mod_ref, wout_ref, part_refs)
    x1_ref[...] = x
    h = _norm_mod(x, g_ref[...], mod_ref[0, 4:5, :], mod_ref[0, 3:4, :]).astype(BF16)
    h_ref[...] = h
    tm = h.shape[0]
    logits = jnp.dot(h, wr_ref[...], preferred_element_type=F32)
    lane = lax.broadcasted_iota(jnp.int32, logits.shape, 1)
    neg = jnp.float32(-jnp.inf)
    logits = jnp.where(lane < N_EXPERTS, logits, neg)
    m1 = jnp.max(logits, axis=-1, keepdims=True)
    i1 = jnp.min(jnp.where(logits == m1, lane, LANES), axis=-1, keepdims=True)
    rest = jnp.where(lane == i1, neg, logits)
    m2 = jnp.max(rest, axis=-1, keepdims=True)
    i2 = jnp.min(jnp.where(rest == m2, lane, LANES), axis=-1, keepdims=True)
    e2 = jnp.exp(m2 - m1)
    g1 = 1.0 / (1.0 + e2)
    g2 = e2 / (1.0 + e2)
    sel1, sel2 = lane == i1, lane == i2
    sel = jnp.where(sel1 | sel2, 1.0, 0.0)
    li = lax.broadcasted_iota(jnp.int32, (tm, tm), 0)
    si = lax.broadcasted_iota(jnp.int32, (tm, tm), 1)
    before = jnp.where(si < li, 1.0, 0.0).astype(BF16)
    rank = jnp.dot(before, sel.astype(BF16), preferred_element_type=F32) + base_ref[...]
    r1 = jnp.sum(jnp.where(sel1, rank, 0.0), axis=-1, keepdims=True)
    r2 = jnp.sum(jnp.where(sel2, rank, 0.0), axis=-1, keepdims=True)
    base_ref[...] += jnp.sum(sel, axis=0, keepdims=True)
    cnt_ref[...] = base_ref[...]
    fields = (i1.astype(F32), i2.astype(F32), r1, r2, g1, g2)
    route = jnp.zeros(logits.shape, F32)
    for k, v in enumerate(fields):
        route = jnp.where(lane == k, v, route)
    route_ref[...] = route


def moe_router(x2, parts, g, mod, w_out, w_router, tiles_per_batch, tm):
    m, d = x2.shape
    return pl.pallas_call(
        _router_kernel,
        out_shape=(jax.ShapeDtypeStruct((m, d), F32), jax.ShapeDtypeStruct((m, d), BF16),
                   jax.ShapeDtypeStruct((m, LANES), F32), jax.ShapeDtypeStruct((1, LANES), F32)),
        grid=(m // tm,),
        in_specs=[pl.BlockSpec((tm, d), lambda i: (i, 0)),
                  pl.BlockSpec((1, d), lambda i: (0, 0)),
                  pl.BlockSpec((1, 6, d), _mod_map(tiles_per_batch)),
                  _resident(w_out.shape, lambda i: (0, 0)),
                  pl.BlockSpec((d, LANES), lambda i: (0, 0))] + _part_specs(parts, tm),
        out_specs=(pl.BlockSpec((tm, d), lambda i: (i, 0)),
                   pl.BlockSpec((tm, d), lambda i: (i, 0)),
                   pl.BlockSpec((tm, LANES), lambda i: (i, 0)),
                   pl.BlockSpec((1, LANES), lambda i: (0, 0))),
        scratch_shapes=[pltpu.VMEM((1, LANES), F32)],
        compiler_params=_params("arbitrary"),
        name="moe_router",
    )(x2, g, mod, w_out, w_router, *parts)


def _moe_kernel(tile_ref, exp_ref, start_ref, end_ref, xg_ref, wgu_ref, wd_ref, y_ref):
    w = pl.program_id(0)
    tm = xg_ref.shape[0]
    start, end = start_ref[w], end_ref[w]
    t0 = tile_ref[w] * tm

    @pl.when(end > start)
    def _():
        y = _swiglu(xg_ref[...], lambda a, b: wgu_ref[0, :, a:b], lambda a, b: wd_ref[0, a:b, :], wd_ref.shape[1])
        y = y.astype(y_ref.dtype)

        @pl.when(start == t0)
        def _():
            y_ref[...] = y

        @pl.when(start != t0)
        def _():
            row = t0 + lax.broadcasted_iota(jnp.int32, (tm, 1), 0)
            y_ref[...] = jnp.where(row >= start, y, y_ref[...])


def moe_grouped(item_tile, item_expert, item_start, item_end, xg, w_gu, w_down, tm):
    p, d = xg.shape
    grid_spec = pltpu.PrefetchScalarGridSpec(
        num_scalar_prefetch=4,
        grid=(item_tile.shape[0],),
        in_specs=[pl.BlockSpec((tm, d), lambda w, it, ie, s, e: (it[w], 0)),
                  _resident((1,) + w_gu.shape[1:], lambda w, it, ie, s, e: (ie[w], 0, 0)),
                  _resident((1,) + w_down.shape[1:], lambda w, it, ie, s, e: (ie[w], 0, 0))],
        out_specs=pl.BlockSpec((tm, d), lambda w, it, ie, s, e: (it[w], 0)),
    )
    return pl.pallas_call(
        _moe_kernel,
        out_shape=jax.ShapeDtypeStruct((p, d), BF16),
        grid_spec=grid_spec,
        compiler_params=_params("arbitrary"),
        name="moe_grouped",
    )(item_tile, item_expert, item_start, item_end, xg, w_gu, w_down)


def _moe_combine_kernel(x_ref, ya_ref, yb_ref, route_ref, mod_ref, *refs):
    o_ref = refs[-1]
    r = route_ref[...]
    y = (r[:, ROUTE_G1:ROUTE_G1 + 1] * ya_ref[...].astype(F32) + r[:, ROUTE_G2:ROUTE_G2 + 1] * yb_ref[...].astype(F32))
    x = x_ref[...] + mod_ref[0, 5:6, :] * y
    if len(refs) == 2:
        ms = jnp.mean(x * x, axis=-1, keepdims=True)
        x = x * lax.rsqrt(ms + EPS) * refs[0][...]
    o_ref[...] = x


def moe_combine(x2, ya, yb, route, mod, tiles_per_batch, tm, g_final=None):
    m, d = x2.shape
    row = pl.BlockSpec((tm, d), lambda i: (i, 0))
    specs = [row, row, row, pl.BlockSpec((tm, LANES), lambda i: (i, 0)),
             pl.BlockSpec((1, 6, d), _mod_map(tiles_per_batch))]
    args = [x2, ya, yb, route, mod]
    if g_final is not None:
        specs.append(pl.BlockSpec((1, d), lambda i: (0, 0)))
        args.append(g_final)
    return pl.pallas_call(
        _moe_combine_kernel,
        out_shape=jax.ShapeDtypeStruct((m, d), F32),
        grid=(m // tm,),
        in_specs=specs,
        out_specs=row,
        compiler_params=_params("parallel"),
        name="moe_combine",
    )(*args)


def moe_block(x2, parts, g, mod, w_out, w_router, w_gu, w_down, tiles_per_batch, tm_tok, tm_grp, g_final=None):
    m, d = x2.shape
    x2, h, route, cnt = moe_router(x2, parts, g, mod, w_out, w_router, tiles_per_batch, tm_tok)
    cnt = cnt[0, :N_EXPERTS].astype(jnp.int32)
    p = m * TOP_K
    off = jnp.cumsum(cnt) - cnt
    field = lambda k: route[:, k].astype(jnp.int32)

    def row_of(e, r):
        for k in range(N_EXPERTS):
            r = r + jnp.where(e == k, off[k], 0)
        return r

    d1 = row_of(field(ROUTE_E1), field(ROUTE_R1))
    d2 = row_of(field(ROUTE_E2), field(ROUTE_R2))
    tok = jnp.arange(m, dtype=jnp.int32)
    _, src = lax.sort_key_val(jnp.concatenate([d1, d2]), jnp.concatenate([tok, tok]))
    n_row_tiles = p // tm_grp
    start = jnp.sort(jnp.concatenate([jnp.arange(n_row_tiles, dtype=jnp.int32) * tm_grp, off]))
    end = jnp.concatenate([start[1:], jnp.full((1,), p, jnp.int32)])
    item_tile = jnp.minimum(start // tm_grp, n_row_tiles - 1)
    item_expert = jnp.sum((off[None, :] <= start[:, None]).astype(jnp.int32), axis=1) - 1
    rows = lambda a, idx: a.at[idx].get(mode="promise_in_bounds")
    yg = moe_grouped(item_tile, item_expert, start, end, rows(h, src), w_gu, w_down, tm_grp)
    return moe_combine(x2, rows(yg, d1), rows(yg, d2), route, mod, tiles_per_batch, tm_tok, g_final)


def _final_norm_kernel(x_ref, g_ref, o_ref):
    x = x_ref[...]
    ms = jnp.mean(x * x, axis=-1, keepdims=True)
    o_ref[...] = x * lax.rsqrt(ms + EPS) * g_ref[...]


def final_norm(x2, g, tm):
    m, d = x2.shape
    return pl.pallas_call(
        _final_norm_kernel,
        out_shape=jax.ShapeDtypeStruct((m, d), F32),
        grid=(m // tm,),
        in_specs=[pl.BlockSpec((tm, d), lambda i: (i, 0)), pl.BlockSpec((1, d), lambda i: (0, 0))],
        out_specs=pl.BlockSpec((tm, d), lambda i: (i, 0)),
        compiler_params=_params("parallel"),
        name="final_norm",
    )(x2, g)


NA_QROWS = 4
NA_KROWS = NA_QROWS + WIN_R
Q_COL, K_COL, V_COL = 4, 5, 6


def _na_key_start(j, rows):
    return np.clip(j * NA_QROWS - WIN_R // 2, 0, rows - NA_KROWS)


def _na_bias_index(rows):
    nblk = rows // NA_QROWS
    pats = []
    for j in range(nblk):
        start = _na_key_start(j, rows)
        r = j * NA_QROWS + np.arange(NA_QROWS)
        sr = np.clip(r - WIN_R // 2, 0, rows - WIN_R)
        kr = start + np.arange(NA_KROWS)
        rvalid = (kr[None, :] >= sr[:, None]) & (kr[None, :] < sr[:, None] + WIN_R)
        ri = np.clip(kr[None, :] - r[:, None] + WIN_R - 1, 0, 2 * WIN_R - 2)
        pats.append((ri, rvalid))
    for j in range(2, nblk - 1):
        assert all(np.array_equal(a, b) for a, b in zip(pats[1], pats[j]))
    sel = [pats[0], pats[1], pats[nblk - 1]]
    ri = np.stack([p[0] for p in sel])
    rvalid = np.stack([p[1] for p in sel])
    c = np.arange(GRID_W)
    sc = np.clip(c - WIN_C // 2, 0, GRID_W - WIN_C)
    cvalid = (c[None, :] >= sc[:, None]) & (c[None, :] < sc[:, None] + WIN_C)
    ci = np.clip(c[None, :] - c[:, None] + WIN_C - 1, 0, 2 * WIN_C - 2)
    c_onehot = (ci[..., None] == np.arange(2 * WIN_C - 1)).astype(np.float32)
    valid = rvalid[:, :, None, :, None] & cvalid[None, None, :, None, :]
    return ri, c_onehot, valid


def na_bias_table(rpb, rows):
    ri, c_onehot, valid = _na_bias_index(rows)
    toep = jnp.einsum('hrd,qkd->hrqk', rpb, c_onehot, precision=lax.Precision.HIGHEST)
    b = jnp.take(toep, ri.reshape(-1), axis=1).reshape((NA_HEADS,) + ri.shape + (GRID_W, GRID_W))
    b = jnp.transpose(b, (1, 0, 2, 4, 3, 5))
    b = jnp.where(valid[:, None], b, -jnp.inf)
    return b.reshape(3, NA_HEADS, NA_QROWS * GRID_W, NA_KROWS * GRID_W).astype(F32)


def _attend_heads(q, key_sets, bias_fn):
    n, w = q.shape
    lane = lax.broadcasted_iota(jnp.int32, (1, w), 1)
    out = jnp.zeros((n, w), F32)
    for h in range(NA_HEADS):
        mh = (lane >= h * NA_HEAD_DIM) & (lane < (h + 1) * NA_HEAD_DIM)
        qh = jnp.where(mh, q, 0.0).astype(BF16)
        scores = []
        for i, (k, _) in enumerate(key_sets):
            s = lax.dot_general(qh, k, (((1,), (1,)), ((), ())), preferred_element_type=F32)
            b = bias_fn(i, h)
            scores.append(s if b is None else s + b)
        m = scores[0].max(axis=-1, keepdims=True)
        for s in scores[1:]:
            m = jnp.maximum(m, s.max(axis=-1, keepdims=True))
        denom = jnp.zeros((n, 1), F32)
        acc = jnp.zeros((n, w), F32)
        for s, (_, v) in zip(scores, key_sets):
            p = jnp.exp(s - m)
            denom = denom + p.sum(axis=-1, keepdims=True)
            acc = acc + jnp.dot(p.astype(BF16), v, preferred_element_type=F32)
        out = out + jnp.where(mh, acc / denom, 0.0)
    return out


def _na_kernel(q_ref, k_ref, v_ref, kc_ref, vc_ref, bias_ref, o_ref, *, rows):
    j = pl.program_id(1)
    start = jnp.clip(j * NA_QROWS - WIN_R // 2, 0, rows - NA_KROWS)
    t0 = pl.multiple_of(start * GRID_W, GRID_W)
    nk = NA_KROWS * GRID_W
    kw = k_ref[0, pl.ds(t0, nk), :]
    vw = v_ref[0, pl.ds(t0, nk), :]
    q = q_ref[0] * (1.0 / math.sqrt(NA_HEAD_DIM))
    o = _attend_heads(q, [(kw, vw), (kc_ref[0], vc_ref[0])], lambda i, h: bias_ref[0, h] if i == 0 else None)
    o_ref[0] = o.astype(o_ref.dtype)


def na_attention(pr3, prc3, bias):
    bn, S, _ = pr3.shape
    Lc = prc3.shape[1]
    rows = S // GRID_W
    nblk = rows // NA_QROWS
    nq = NA_QROWS * GRID_W

    def pat(b, j):
        return (jnp.where(j == 0, 0, jnp.where(j == nblk - 1, 2, 1)), 0, 0, 0)

    return pl.pallas_call(
        functools.partial(_na_kernel, rows=rows),
        out_shape=jax.ShapeDtypeStruct((bn, S, W_NA), BF16),
        grid=(bn, nblk),
        in_specs=[pl.BlockSpec((1, nq, W_NA), lambda b, j: (b, j, Q_COL)),
                  pl.BlockSpec((1, S, W_NA), lambda b, j: (b, 0, K_COL)),
                  pl.BlockSpec((1, S, W_NA), lambda b, j: (b, 0, V_COL)),
                  pl.BlockSpec((1, Lc, W_NA), lambda b, j: (b, 0, K_COL)),
                  pl.BlockSpec((1, Lc, W_NA), lambda b, j: (b, 0, V_COL)),
                  pl.BlockSpec((1,) + bias.shape[1:], pat)],
        out_specs=pl.BlockSpec((1, nq, W_NA), lambda b, j: (b, j, 0)),
        compiler_params=_params("parallel", "arbitrary"),
        name="na_attention",
    )(pr3, pr3, pr3, prc3, prc3, bias)


def _ctx_attn_kernel(q_ref, k_ref, v_ref, o_ref):
    q = q_ref[0] * (1.0 / math.sqrt(NA_HEAD_DIM))
    o = _attend_heads(q, [(k_ref[0], v_ref[0])], lambda i, h: None)
    o_ref[0] = o.astype(o_ref.dtype)


def ctx_attention(prc3):
    bn, Lc, _ = prc3.shape
    return pl.pallas_call(
        _ctx_attn_kernel,
        out_shape=jax.ShapeDtypeStruct((bn, Lc, W_NA), BF16),
        grid=(bn,),
        in_specs=[pl.BlockSpec((1, Lc, W_NA), lambda b: (b, 0, Q_COL)),
                  pl.BlockSpec((1, Lc, W_NA), lambda b: (b, 0, K_COL)),
                  pl.BlockSpec((1, Lc, W_NA), lambda b: (b, 0, V_COL))],
        out_specs=pl.BlockSpec((1, Lc, W_NA), lambda b: (b, 0, 0)),
        compiler_params=_params("parallel"),
        name="ctx_attention",
    )(prc3, prc3, prc3)


POOL_COL, CONV_H_COL, CONV_B_COL, CONV_C_COL = 0, 1, 2, 3


def _shift_rows(x, k, row):
    n = x.shape[0]
    y = pltpu.roll(x, k % n, 0)
    return jnp.where(row < k, 0.0, y) if k > 0 else jnp.where(row >= n + k, 0.0, y)


def _local_mix_kernel(u_ref, h_ref, bg_ref, cg_ref, pw_ref, ps_ref, cw_ref, op_ref, oc_ref, *, seq):
    row = lax.broadcasted_iota(jnp.int32, (seq, 1), 0)
    lane = lax.broadcasted_iota(jnp.int32, (1, W_POOL), 1)
    u = u_ref[0].astype(F32)
    trailing, leading = {1: u}, {1: u}
    for w in (1, 2, 4):
        trailing[2 * w] = trailing[w] + _shift_rows(trailing[w], w, row)
        leading[2 * w] = leading[w] + _shift_rows(leading[w], -w, row)
    rowf = row.astype(F32)
    win_sum = jnp.zeros_like(u)
    cnt = jnp.zeros_like(u)
    for g, win in enumerate(POOL_WINDOWS):
        half = win // 2
        mg = (lane >= g * POOL_GROUP) & (lane < (g + 1) * POOL_GROUP)
        s = _shift_rows(trailing[half], 1, row) + leading[half]
        n = jnp.minimum(rowf + half, float(seq)) - jnp.maximum(rowf - half, 0.0)
        win_sum = jnp.where(mg, s, win_sum)
        cnt = jnp.where(mg, n, cnt)
    p = win_sum / cnt - u
    pooled = jnp.dot(p.astype(BF16), pw_ref[...], preferred_element_type=F32) * ps_ref[...]
    op_ref[0] = pooled.astype(op_ref.dtype)
    v = cg_ref[0].astype(F32) * h_ref[0].astype(F32)
    conv = (_shift_rows(v, 1, row) * cw_ref[0:1, :] + v * cw_ref[1:2, :] + _shift_rows(v, -1, row) * cw_ref[2:3, :])
    oc_ref[0] = (bg_ref[0].astype(F32) * conv).astype(oc_ref.dtype)


def local_mix(pr3, pool_w, pool_scale, conv_w):
    bn, seq, _ = pr3.shape
    pw = jax.scipy.linalg.block_diag(*[pool_w[g] for g in range(len(POOL_WINDOWS))]).astype(BF16)
    blk = lambda col: pl.BlockSpec((1, seq, W_POOL), lambda b: (b, 0, col))
    full = lambda a: pl.BlockSpec(a.shape, lambda b: (0,) * a.ndim)
    args = (pw, pool_scale[None, :], conv_w)
    return pl.pallas_call(
        functools.partial(_local_mix_kernel, seq=seq),
        out_shape=(jax.ShapeDtypeStruct((bn, seq, W_POOL), BF16), jax.ShapeDtypeStruct((bn, seq, W_CONV), BF16)),
        grid=(bn,),
        in_specs=[blk(POOL_COL), blk(CONV_H_COL), blk(CONV_B_COL), blk(CONV_C_COL)] + [full(a) for a in args],
        out_specs=(pl.BlockSpec((1, seq, W_POOL), lambda b: (b, 0, 0)),
                   pl.BlockSpec((1, seq, W_CONV), lambda b: (b, 0, 0))),
        compiler_params=_params("parallel"),
        name="local_mix",
    )(pr3, pr3, pr3, pr3, *args)


Z_COL, XS_COL, BS_COL, CS_COL = 7, 8, 9, 10
DT_COL = 22
HEADS_PER_GROUP = SSD_HEADS // SSD_GROUPS
GROUP_W = HEADS_PER_GROUP * SSD_HEAD_DIM


def _head_mask(h):
    lane = lax.broadcasted_iota(jnp.int32, (1, W_SSD), 1)
    return (lane >= h * SSD_HEAD_DIM) & (lane < (h + 1) * SSD_HEAD_DIM)


def _expand_heads(v, d):
    out = jnp.zeros((v.shape[0], W_SSD), F32)
    for h in range(SSD_HEADS):
        k = d * SSD_HEADS + h
        out = jnp.where(_head_mask(h), v[:, k:k + 1], out)
    return out


def _ssd_chunk_terms(c, xs_ref, b_ref, c_ref, dt_ref, la_ref, dsk_ref, y_ref, upd_ref, dec_ref, need_y):
    T = SSD_CHUNK
    r0 = pl.multiple_of(c * T, T)
    xs = xs_ref[pl.ds(r0, T), :]
    bm = b_ref[pl.ds(r0, T), :]
    cm = c_ref[pl.ds(r0, T), :]
    dt = dt_ref[pl.ds(r0, T), :]
    la = la_ref[pl.ds(r0, T), :]
    li = lax.broadcasted_iota(jnp.int32, (T, T), 0)
    si = lax.broadcasted_iota(jnp.int32, (T, T), 1)
    causal = si <= li
    prefix = jnp.dot(causal.astype(F32), la, precision=lax.Precision.HIGHEST, preferred_element_type=F32)
    total = prefix[T - 1:T, :]
    lane = lax.broadcasted_iota(jnp.int32, (1, LANES), 1)
    acum = jnp.where(lane < SSD_HEADS, prefix, total - prefix + la)
    bt = bm.astype(F32).T.astype(BF16)
    if need_y:
        acum_t = acum.T
        cb = [lax.dot_general(cm[:, g * SSD_STATE:(g + 1) * SSD_STATE], bm[:, g * SSD_STATE:(g + 1) * SSD_STATE],
                              (((1,), (1,)), ((), ())), preferred_element_type=F32) for g in range(SSD_GROUPS)]
        y = jnp.zeros((T, W_SSD), F32)
    for d in range(2):
        acum_e = _expand_heads(acum, d)
        tot_e = _expand_heads(total, d)
        xdt = xs * _expand_heads(dt, d)
        xw = (xdt * jnp.exp(tot_e - acum_e)).astype(BF16)
        upd = [jnp.dot(bt[g * SSD_STATE:(g + 1) * SSD_STATE, :], xw[:, g * GROUP_W:(g + 1) * GROUP_W],
                       preferred_element_type=F32) for g in range(SSD_GROUPS)]
        upd_ref[c, d] = jnp.concatenate(upd, axis=1)
        dec_ref[c, d, 0:T, :] = jnp.exp(acum_e)
        dec_ref[c, d, T:T + 1, :] = jnp.exp(tot_e)
        if need_y:
            mask = causal if d == 0 else (si >= li)
            xdt_b = xdt.astype(BF16)
            for h in range(SSD_HEADS):
                k = d * SSD_HEADS + h
                decay = jnp.exp(jnp.where(mask, acum[:, k:k + 1] - acum_t[k:k + 1, :], -jnp.inf))
                scores = (cb[h // HEADS_PER_GROUP] * decay).astype(BF16)
                y = y + jnp.where(_head_mask(h), jnp.dot(scores, xdt_b, preferred_element_type=F32), 0.0)
    if need_y:
        y_ref[pl.ds(r0, T), :] = xs * dsk_ref[...] + y


def _ssd_chunk_state(c, d, c_ref, y_ref, upd_ref, dec_ref, st_ref, need_y):
    T = SSD_CHUNK
    r0 = pl.multiple_of(c * T, T)
    st = st_ref[d]
    if need_y:
        cm = c_ref[pl.ds(r0, T), :]
        st_b = st.astype(BF16)
        ys = [jnp.dot(cm[:, g * SSD_STATE:(g + 1) * SSD_STATE], st_b[:, g * GROUP_W:(g + 1) * GROUP_W],
                      preferred_element_type=F32) for g in range(SSD_GROUPS)]
        y_ref[pl.ds(r0, T), :] += jnp.concatenate(ys, axis=1) * dec_ref[c, d, 0:T, :]
    st_ref[d] = dec_ref[c, d, T:T + 1, :] * st + upd_ref[c, d]


def _ssd_kernel(z_ref, xs_in, bs_in, cs_in, dtr_ref, cw_ref, cb_ref, dtb_ref, alog_ref, dsk_ref, ng_ref, h0_ref,
                *refs, seq, need_y):
    if need_y:
        o_ref, hT_ref, xs_ref, b_ref, c_ref, dt_ref, la_ref, upd_ref, dec_ref, st_ref, y_ref = refs
    else:
        hT_ref, xs_ref, b_ref, c_ref, dt_ref, la_ref, upd_ref, dec_ref, st_ref = refs
        y_ref = None
    nc = seq // SSD_CHUNK
    row = lax.broadcasted_iota(jnp.int32, (seq, 1), 0)
    for gi, (src, dst) in enumerate(((xs_in, xs_ref), (bs_in, b_ref), (cs_in, c_ref))):
        sl = slice(gi * W_SSD, (gi + 1) * W_SSD)
        x = src[0].astype(F32)
        cv = (_shift_rows(x, 1, row) * cw_ref[0:1, sl] + x * cw_ref[1:2, sl]
              + _shift_rows(x, -1, row) * cw_ref[2:3, sl] + cb_ref[:, sl])
        dst[...] = (cv * jax.nn.sigmoid(cv)).astype(dst.dtype)
    dtv = dtr_ref[0] + dtb_ref[...]
    dt = jnp.maximum(dtv, 0.0) + jnp.log1p(jnp.exp(-jnp.abs(dtv)))
    dt_ref[...] = dt
    la_ref[...] = dt * (-jnp.exp(alog_ref[...]))
    st_ref[...] = h0_ref[0]

    def terms(c, carry):
        _ssd_chunk_terms(c, xs_ref, b_ref, c_ref, dt_ref, la_ref, dsk_ref, y_ref, upd_ref, dec_ref, need_y)
        return carry

    lax.fori_loop(0, nc, terms, 0, unroll=2)

    def recur(i, carry):
        _ssd_chunk_state(i, 0, c_ref, y_ref, upd_ref, dec_ref, st_ref, need_y)
        _ssd_chunk_state(nc - 1 - i, 1, c_ref, y_ref, upd_ref, dec_ref, st_ref, need_y)
        return carry

    lax.fori_loop(0, nc, recur, 0, unroll=2)
    hT_ref[0] = st_ref[...]
    if need_y:
        z = z_ref[0].astype(F32)
        yz = y_ref[...] * (z * jax.nn.sigmoid(z))
        ms = jnp.mean(yz * yz, axis=-1, keepdims=True)
        o_ref[0] = (yz * lax.rsqrt(ms + EPS) * ng_ref[...]).astype(o_ref.dtype)


def ssd_scan(pr3, dt3, h0, conv_w, conv_b, dt_bias, a_log, d_skip, norm_g, need_y):
    bn, seq, _ = pr3.shape
    nc = seq // SSD_CHUNK
    pad = LANES - 2 * SSD_HEADS
    dtb = jnp.pad(dt_bias.reshape(1, -1), ((0, 0), (0, pad)))
    alog = jnp.pad(a_log.reshape(1, -1), ((0, 0), (0, pad)))
    dsk = jnp.repeat(d_skip, SSD_HEAD_DIM)[None, :]
    blk = lambda col: pl.BlockSpec((1, seq, W_SSD), lambda b: (b, 0, col))
    full = lambda a: pl.BlockSpec(a.shape, lambda b: (0,) * a.ndim)
    st_shape = (1, 2, SSD_STATE, W_SSD)
    st_spec = pl.BlockSpec(st_shape, lambda b: (b, 0, 0, 0))
    args = (conv_w, conv_b[None, :], dtb, alog, dsk, norm_g[None, :])
    out_shape = [jax.ShapeDtypeStruct((bn,) + st_shape[1:], F32)]
    out_specs = [st_spec]
    scratch = [pltpu.VMEM((seq, W_SSD), F32),
               pltpu.VMEM((seq, W_SSD), BF16), pltpu.VMEM((seq, W_SSD), BF16),
               pltpu.VMEM((seq, LANES), F32), pltpu.VMEM((seq, LANES), F32),
               pltpu.VMEM((nc, 2, SSD_STATE, W_SSD), F32),
               pltpu.VMEM((nc, 2, SSD_CHUNK + 8, W_SSD), F32),
               pltpu.VMEM(st_shape[1:], F32)]
    if need_y:
        out_shape.insert(0, jax.ShapeDtypeStruct((bn, seq, W_SSD), BF16))
        out_specs.insert(0, pl.BlockSpec((1, seq, W_SSD), lambda b: (b, 0, 0)))
        scratch.append(pltpu.VMEM((seq, W_SSD), F32))
    res = pl.pallas_call(
        functools.partial(_ssd_kernel, seq=seq, need_y=need_y),
        out_shape=out_shape,
        grid=(bn,),
        in_specs=[blk(Z_COL), blk(XS_COL), blk(BS_COL), blk(CS_COL),
                  pl.BlockSpec((1, seq, LANES), lambda b: (b, 0, 0))]
                 + [full(a) for a in args] + [st_spec],
        out_specs=out_specs,
        scratch_shapes=scratch,
        compiler_params=_params("parallel"),
        name="ssd_scan",
    )(pr3, pr3, pr3, pr3, dt3, *args, h0)
    return (res[0], res[1]) if need_y else (None, res[0])


def ssd_mix(pr3, dt3, prc3, dtc3, conv_w, conv_b, dt_bias, a_log, d_skip, norm_g, ctx_out):
    bn = pr3.shape[0]
    h0 = jnp.zeros((bn, 2, SSD_STATE, W_SSD), F32)
    oc, hc = ssd_scan(prc3, dtc3, h0, conv_w, conv_b, dt_bias, a_log, d_skip, norm_g, ctx_out)
    o, _ = ssd_scan(pr3, dt3, hc, conv_w, conv_b, dt_bias, a_log, d_skip, norm_g, True)
    return o, oc


def mixer(pr3, dt3, prc3, dtc3, pool_w, pool_scale, conv_w, rpb, s_conv_w, s_conv_b, dt_bias, a_log, d_skip,
          s_norm_g, ctx_out):
    bn, S, _ = pr3.shape
    o_ssd, oc_ssd = ssd_mix(pr3, dt3, prc3, dtc3, s_conv_w, s_conv_b, dt_bias, a_log, d_skip, s_norm_g, ctx_out)
    o = [*local_mix(pr3, pool_w, pool_scale, conv_w),
         na_attention(pr3, prc3, na_bias_table(rpb, S // GRID_W)), o_ssd]
    if not ctx_out:
        return o, None
    oc = [*local_mix(prc3, pool_w, pool_scale, conv_w), ctx_attention(prc3), oc_ssd]
    return o, oc


TM = 512
TM_PROJ = 1024
TM_GRP = 512


def kernel(x, c, ctx, c_ctx, w_ada, b_ada, g_mix, g_ffn, w_in, w_out, pool_w, pool_scale, conv_w, na_rpb,
           ssd_conv_w, ssd_conv_b, ssd_dt_bias, ssd_a_log, ssd_d, ssd_norm_g, ffn_w_gu, ffn_w_down,
           moe_router, moe_w_gu, moe_w_down, g_final):
    bn, S, d = x.shape
    Lc = ctx.shape[1]
    m, mc = bn * S, bn * Lc
    tpb = S // TM
    x2 = x.reshape(m, d)
    xc2 = ctx.reshape(mc, d)
    c_rows = jnp.concatenate([c, c_ctx[None, :], jnp.zeros((7, d), F32)], axis=0)
    for l in range(DEPTH):
        ctx_out = l < DEPTH - 1
        last = l == DEPTH - 1
        ada = ada_modulation(c_rows, w_ada[l].astype(BF16), b_ada[l][None, :])
        mod = ada[:bn].reshape(bn, 6, d)
        mod_c = ada[bn:bn + 1].reshape(1, 6, d)
        w_in_l = jnp.pad(w_in[l], ((0, 0), (0, D_IN_PAD - D_IN_PROJ))).astype(BF16)
        g_m = g_mix[l][None, :]
        g_f = g_ffn[l][None, :]
        pr, dt = in_proj(x2, g_m, mod, w_in_l, S // TM_PROJ, TM_PROJ)
        pr_c, dt_c = in_proj(xc2, g_m, mod_c, w_in_l, None, min(TM_PROJ, mc))
        o, oc = mixer(pr.reshape(bn, S, -1), dt.reshape(bn, S, -1), pr_c.reshape(bn, Lc, -1),
                      dt_c.reshape(bn, Lc, -1), pool_w[l], pool_scale[l], conv_w[l],
                      na_rpb[l], ssd_conv_w[l], ssd_conv_b[l], ssd_dt_bias[l], ssd_a_log[l], ssd_d[l],
                      ssd_norm_g[l], ctx_out)
        w_out_l = w_out[l].astype(BF16)
        o = [p.reshape(m, -1) for p in o]
        oc = [p.reshape(mc, -1) for p in oc] if ctx_out else None
        j = l // 2
        if l % 2 == 0:
            w_gu = ffn_w_gu[j].astype(BF16)
            w_dn = ffn_w_down[j].astype(BF16)
            x2 = ffn_dense(x2, o, g_f, mod, w_out_l, w_gu, w_dn, tpb, TM)
            if ctx_out:
                xc2 = ffn_dense(xc2, oc, g_f, mod_c, w_out_l, w_gu, w_dn, None, min(TM, mc))
        else:
            w_r = jnp.pad(moe_router[j], ((0, 0), (0, LANES - N_EXPERTS))).astype(BF16)
            w_gu = moe_w_gu[j].astype(BF16)
            w_dn = moe_w_down[j].astype(BF16)
            x2 = moe_block(x2, o, g_f, mod, w_out_l, w_r, w_gu, w_dn, tpb, TM, TM_GRP,
                           g_final[None, :] if last else None)
            if ctx_out:
                xc2 = moe_block(xc2, oc, g_f, mod_c, w_out_l, w_r, w_gu, w_dn, None, min(TM, mc), TM_GRP)
            if last:
                return x2.reshape(bn, S, d)
    return final_norm(x2, g_final[None, :], TM).reshape(bn, S, d)
```

```python
import functools
import math

import numpy as np
import jax
import jax.numpy as jnp
from jax import lax
from jax.experimental import pallas as pl
from jax.experimental.pallas import tpu as pltpu

DEPTH = 2
GRID_W = 64
EPS = 1e-6
W_POOL = 256
W_CONV = 256
W_NA = 256
W_SSD = 256
POOL_WINDOWS = (2, 4, 8, 16)
POOL_GROUP = W_POOL // len(POOL_WINDOWS)
NA_HEADS = 4
NA_HEAD_DIM = W_NA // NA_HEADS
WIN_R = 8
WIN_C = 16
SSD_HEAD_DIM = 64
SSD_HEADS = W_SSD // SSD_HEAD_DIM
SSD_GROUPS = 2
SSD_STATE = 128
SSD_CHUNK = 128
SSD_XBC = W_SSD + 2 * SSD_GROUPS * SSD_STATE
D_IN_PROJ = W_POOL + 3 * W_CONV + 3 * W_NA + W_SSD + SSD_XBC + 2 * SSD_HEADS
N_EXPERTS = 8
TOP_K = 2

LANES = 128
D_IN_PAD = -(-D_IN_PROJ // LANES) * LANES
VMEM_LIMIT = 56 * 1024 * 1024
BF16 = jnp.bfloat16
F32 = jnp.float32


def _params(*sem):
    return pltpu.CompilerParams(dimension_semantics=sem, vmem_limit_bytes=VMEM_LIMIT)


def _norm_mod(x, g, scale, shift):
    ms = jnp.mean(x * x, axis=-1, keepdims=True)
    return (x * lax.rsqrt(ms + EPS) * g) * (1.0 + scale) + shift


def _mod_map(tiles_per_batch):
    if tiles_per_batch is None:
        return lambda i, *_: (0, 0, 0)
    return lambda i, *_: (i // tiles_per_batch, 0, 0)


def _resident(shape, index_map):
    return pl.BlockSpec(shape, index_map, pipeline_mode=pl.Buffered(1))


def _ada_kernel(c_ref, w_ref, b_ref, o_ref):
    c = c_ref[...]
    s = c * jax.nn.sigmoid(c)
    o_ref[...] = jnp.dot(s.astype(BF16), w_ref[...], preferred_element_type=F32) + b_ref[...]


def ada_modulation(c_rows, w, b):
    r, d = c_rows.shape
    n = w.shape[1]
    tn = 1536
    return pl.pallas_call(
        _ada_kernel,
        out_shape=jax.ShapeDtypeStruct((r, n), F32),
        grid=(n // tn,),
        in_specs=[pl.BlockSpec((r, d), lambda j: (0, 0)),
                  pl.BlockSpec((d, tn), lambda j: (0, j)),
                  pl.BlockSpec((1, tn), lambda j: (0, j))],
        out_specs=pl.BlockSpec((r, tn), lambda j: (0, j)),
        compiler_params=_params("arbitrary"),
        name="ada_modulation",
    )(c_rows, w, b)


def _in_proj_kernel(x_ref, g_ref, mod_ref, w_ref, o_ref, dt_ref):
    h = _norm_mod(x_ref[...], g_ref[...], mod_ref[0, 1:2, :], mod_ref[0, 0:1, :])
    pr = jnp.dot(h.astype(BF16), w_ref[...], preferred_element_type=F32)
    o_ref[...] = pr.astype(o_ref.dtype)
    dt_ref[...] = pr[:, DT_COL * LANES:(DT_COL + 1) * LANES]


def in_proj(x2, g, mod, w, tiles_per_batch, tm):
    m, d = x2.shape
    n = w.shape[1]
    return pl.pallas_call(
        _in_proj_kernel,
        out_shape=(jax.ShapeDtypeStruct((m, n), BF16), jax.ShapeDtypeStruct((m, LANES), F32)),
        grid=(m // tm,),
        in_specs=[pl.BlockSpec((tm, d), lambda i: (i, 0)),
                  pl.BlockSpec((1, d), lambda i: (0, 0)),
                  pl.BlockSpec((1, 6, d), _mod_map(tiles_per_batch)),
                  _resident((d, n), lambda i: (0, 0))],
        out_specs=(pl.BlockSpec((tm, n), lambda i: (i, 0)), pl.BlockSpec((tm, LANES), lambda i: (i, 0))),
        compiler_params=_params("parallel"),
        name="in_proj",
    )(x2, g, mod, w)


def _mix_residual(x, mod_ref, w_ref, part_refs):
    y, k0 = None, 0
    for p_ref in part_refs:
        k = p_ref.shape[1]
        t = jnp.dot(p_ref[...], w_ref[k0:k0 + k, :], preferred_element_type=F32)
        y = t if y is None else y + t
        k0 += k
    return x + mod_ref[0, 2:3, :] * y


def _part_specs(parts, tm):
    return [pl.BlockSpec((tm, p.shape[1]), lambda i, *_: (i, 0)) for p in parts]


FF_CHUNK = 512


def _swiglu(h, wgu, wd, ff):
    acc = None
    for c0 in range(0, ff, FF_CHUNK):
        c1 = min(c0 + FF_CHUNK, ff)
        gg = jnp.dot(h, wgu(c0, c1), preferred_element_type=F32)
        uu = jnp.dot(h, wgu(ff + c0, ff + c1), preferred_element_type=F32)
        a = (gg * jax.nn.sigmoid(gg) * uu).astype(BF16)
        part = jnp.dot(a, wd(c0, c1), preferred_element_type=F32)
        acc = part if acc is None else acc + part
    return acc


def _ffn_kernel(x_ref, g_ref, mod_ref, wout_ref, wgu_ref, wd_ref, *refs):
    x = _mix_residual(x_ref[...], mod_ref, wout_ref, refs[:-1])
    y_ref = refs[-1]
    h = _norm_mod(x, g_ref[...], mod_ref[0, 4:5, :], mod_ref[0, 3:4, :]).astype(BF16)
    y = _swiglu(h, lambda a, b: wgu_ref[:, a:b], lambda a, b: wd_ref[a:b, :], wd_ref.shape[0])
    y_ref[...] = x + mod_ref[0, 5:6, :] * y


def ffn_dense(x2, parts, g, mod, w_out, w_gu, w_down, tiles_per_batch, tm):
    m, d = x2.shape
    return pl.pallas_call(
        _ffn_kernel,
        out_shape=jax.ShapeDtypeStruct((m, d), F32),
        grid=(m // tm,),
        in_specs=[pl.BlockSpec((tm, d), lambda i: (i, 0)),
                  pl.BlockSpec((1, d), lambda i: (0, 0)),
                  pl.BlockSpec((1, 6, d), _mod_map(tiles_per_batch)),
                  _resident(w_out.shape, lambda i: (0, 0)),
                  _resident(w_gu.shape, lambda i: (0, 0)),
                  _resident(w_down.shape, lambda i: (0, 0))] + _part_specs(parts, tm),
        out_specs=pl.BlockSpec((tm, d), lambda i: (i, 0)),
        compiler_params=_params("parallel"),
        name="ffn_dense",
    )(x2, g, mod, w_out, w_gu, w_down, *parts)


ROUTE_E1, ROUTE_E2, ROUTE_R1, ROUTE_R2, ROUTE_G1, ROUTE_G2 = range(6)
ROUTE_ROWS = 8


def _router_kernel(x_ref, g_ref, mod_ref, wout_ref, wr_ref, *refs):
    part_refs = refs[:-6]
    x1_ref, h_ref, route_ref, route_t_ref, cnt_ref, base_ref = refs[-6:]

    @pl.when(pl.program_id(0) == 0)
    def _():
        base_ref[...] = jnp.zeros_like(base_ref)

    x = _mix_residual(x_ref[...], mod_ref, wout_ref, part_refs)
    x1_ref[...] = x
    h = _norm_mod(x, g_ref[...], mod_ref[0, 4:5, :], mod_ref[0, 3:4, :]).astype(BF16)
    h_ref[...] = h
    tm = h.shape[0]
    logits = jnp.dot(h, wr_ref[...], preferred_element_type=F32)
    lane = lax.broadcasted_iota(jnp.int32, logits.shape, 1)
    neg = jnp.float32(-jnp.inf)
    logits = jnp.where(lane < N_EXPERTS, logits, neg)
    m1 = jnp.max(logits, axis=-1, keepdims=True)
    i1 = jnp.min(jnp.where(logits == m1, lane, LANES), axis=-1, keepdims=True)
    rest = jnp.where(lane == i1, neg, logits)
    m2 = jnp.max(rest, axis=-1, keepdims=True)
    i2 = jnp.min(jnp.where(rest == m2, lane, LANES), axis=-1, keepdims=True)
    e2 = jnp.exp(m2 - m1)
    g1 = 1.0 / (1.0 + e2)
    g2 = e2 / (1.0 + e2)
    sel1, sel2 = lane == i1, lane == i2
    sel = jnp.where(sel1 | sel2, 1.0, 0.0)
    li = lax.broadcasted_iota(jnp.int32, (tm, tm), 0)
    si = lax.broadcasted_iota(jnp.int32, (tm, tm), 1)
    before = jnp.where(si < li, 1.0, 0.0).astype(BF16)
    rank = jnp.dot(before, sel.astype(BF16), preferred_element_type=F32) + base_ref[...]
    r1 = jnp.sum(jnp.where(sel1, rank, 0.0), axis=-1, keepdims=True)
    r2 = jnp.sum(jnp.where(sel2, rank, 0.0), axis=-1, keepdims=True)
    base_ref[...] += jnp.sum(sel, axis=0, keepdims=True)
    cnt_ref[...] = base_ref[...]
    fields = (i1.astype(F32), i2.astype(F32), r1, r2, g1, g2)
    route = jnp.zeros(logits.shape, F32)
    for k, v in enumerate(fields):
        route = jnp.where(lane == k, v, route)
    route_ref[...] = route
    route_t_ref[...] = route.T[:ROUTE_ROWS, :]


def moe_router(x2, parts, g, mod, w_out, w_router, tiles_per_batch, tm):
    m, d = x2.shape
    return pl.pallas_call(
        _router_kernel,
        out_shape=(jax.ShapeDtypeStruct((m, d), F32), jax.ShapeDtypeStruct((m, d), BF16),
                   jax.ShapeDtypeStruct((m, LANES), F32), jax.ShapeDtypeStruct((ROUTE_ROWS, m), F32),
                   jax.ShapeDtypeStruct((1, LANES), F32)),
        grid=(m // tm,),
        in_specs=[pl.BlockSpec((tm, d), lambda i: (i, 0)),
                  pl.BlockSpec((1, d), lambda i: (0, 0)),
                  pl.BlockSpec((1, 6, d), _mod_map(tiles_per_batch)),
                  _resident(w_out.shape, lambda i: (0, 0)),
                  pl.BlockSpec((d, LANES), lambda i: (0, 0))] + _part_specs(parts, tm),
        out_specs=(pl.BlockSpec((tm, d), lambda i: (i, 0)),
                   pl.BlockSpec((tm, d), lambda i: (i, 0)),
                   pl.BlockSpec((tm, LANES), lambda i: (i, 0)),
                   pl.BlockSpec((ROUTE_ROWS, tm), lambda i: (0, i)),
                   pl.BlockSpec((1, LANES), lambda i: (0, 0))),
        scratch_shapes=[pltpu.VMEM((1, LANES), F32)],
        compiler_params=_params("arbitrary"),
        name="moe_router",
    )(x2, g, mod, w_out, w_router, *parts)


def _moe_kernel(tile_ref, exp_ref, start_ref, end_ref, xg_ref, wgu_ref, wd_ref, y_ref):
    w = pl.program_id(0)
    tm = xg_ref.shape[0]
    start, end = start_ref[w], end_ref[w]
    t0 = tile_ref[w] * tm

    @pl.when(end > start)
    def _():
        y = _swiglu(xg_ref[...], lambda a, b: wgu_ref[0, :, a:b], lambda a, b: wd_ref[0, a:b, :], wd_ref.shape[1])
        y = y.astype(y_ref.dtype)

        @pl.when(start == t0)
        def _():
            y_ref[...] = y

        @pl.when(start != t0)
        def _():
            row = t0 + lax.broadcasted_iota(jnp.int32, (tm, 1), 0)
            y_ref[...] = jnp.where(row >= start, y, y_ref[...])


def moe_grouped(item_tile, item_expert, item_start, item_end, xg, w_gu, w_down, tm):
    p, d = xg.shape
    grid_spec = pltpu.PrefetchScalarGridSpec(
        num_scalar_prefetch=4,
        grid=(item_tile.shape[0],),
        in_specs=[pl.BlockSpec((tm, d), lambda w, it, ie, s, e: (it[w], 0)),
                  _resident((1,) + w_gu.shape[1:], lambda w, it, ie, s, e: (ie[w], 0, 0)),
                  _resident((1,) + w_down.shape[1:], lambda w, it, ie, s, e: (ie[w], 0, 0))],
        out_specs=pl.BlockSpec((tm, d), lambda w, it, ie, s, e: (it[w], 0)),
    )
    return pl.pallas_call(
        _moe_kernel,
        out_shape=jax.ShapeDtypeStruct((p, d), BF16),
        grid_spec=grid_spec,
        compiler_params=_params("arbitrary"),
        name="moe_grouped",
    )(item_tile, item_expert, item_start, item_end, xg, w_gu, w_down)


def _moe_combine_kernel(x_ref, ya_ref, yb_ref, route_ref, mod_ref, *refs):
    o_ref = refs[-1]
    r = route_ref[...]
    y = (r[:, ROUTE_G1:ROUTE_G1 + 1] * ya_ref[...].astype(F32) + r[:, ROUTE_G2:ROUTE_G2 + 1] * yb_ref[...].astype(F32))
    x = x_ref[...] + mod_ref[0, 5:6, :] * y
    if len(refs) == 2:
        ms = jnp.mean(x * x, axis=-1, keepdims=True)
        x = x * lax.rsqrt(ms + EPS) * refs[0][...]
    o_ref[...] = x


def moe_combine(x2, ya, yb, route, mod, tiles_per_batch, tm, g_final=None):
    m, d = x2.shape
    row = pl.BlockSpec((tm, d), lambda i: (i, 0))
    specs = [row, row, row, pl.BlockSpec((tm, LANES), lambda i: (i, 0)),
             pl.BlockSpec((1, 6, d), _mod_map(tiles_per_batch))]
    args = [x2, ya, yb, route, mod]
    if g_final is not None:
        specs.append(pl.BlockSpec((1, d), lambda i: (0, 0)))
        args.append(g_final)
    return pl.pallas_call(
        _moe_combine_kernel,
        out_shape=jax.ShapeDtypeStruct((m, d), F32),
        grid=(m // tm,),
        in_specs=specs,
        out_specs=row,
        compiler_params=_params("parallel"),
        name="moe_combine",
    )(*args)


def moe_block(x2, parts, g, mod, w_out, w_router, w_gu, w_down, tiles_per_batch, tm_tok, tm_grp, g_final=None):
    m, d = x2.shape
    x2, h, route, route_t, cnt = moe_router(x2, parts, g, mod, w_out, w_router, tiles_per_batch, tm_tok)
    cnt = cnt[0, :N_EXPERTS].astype(jnp.int32)
    p = m * TOP_K
    off = jnp.cumsum(cnt) - cnt
    field = lambda k: route_t[k].astype(jnp.int32)

    def row_of(e, r):
        for k in range(N_EXPERTS):
            r = r + jnp.where(e == k, off[k], 0)
        return r

    d1 = row_of(field(ROUTE_E1), field(ROUTE_R1))
    d2 = row_of(field(ROUTE_E2), field(ROUTE_R2))
    tok = jnp.arange(m, dtype=jnp.int32)
    _, src = lax.sort_key_val(jnp.concatenate([d1, d2]), jnp.concatenate([tok, tok]))
    n_row_tiles = p // tm_grp
    start = jnp.sort(jnp.concatenate([jnp.arange(n_row_tiles, dtype=jnp.int32) * tm_grp, off]))
    end = jnp.concatenate([start[1:], jnp.full((1,), p, jnp.int32)])
    item_tile = jnp.minimum(start // tm_grp, n_row_tiles - 1)
    item_expert = jnp.sum((off[None, :] <= start[:, None]).astype(jnp.int32), axis=1) - 1
    rows = lambda a, idx: a.at[idx].get(mode="promise_in_bounds")
    yg = moe_grouped(item_tile, item_expert, start, end, rows(h, src), w_gu, w_down, tm_grp)
    return moe_combine(x2, rows(yg, d1), rows(yg, d2), route, mod, tiles_per_batch, tm_tok, g_final)


def _final_norm_kernel(x_ref, g_ref, o_ref):
    x = x_ref[...]
    ms = jnp.mean(x * x, axis=-1, keepdims=True)
    o_ref[...] = x * lax.rsqrt(ms + EPS) * g_ref[...]


def final_norm(x2, g, tm):
    m, d = x2.shape
    return pl.pallas_call(
        _final_norm_kernel,
        out_shape=jax.ShapeDtypeStruct((m, d), F32),
        grid=(m // tm,),
        in_specs=[pl.BlockSpec((tm, d), lambda i: (i, 0)), pl.BlockSpec((1, d), lambda i: (0, 0))],
        out_specs=pl.BlockSpec((tm, d), lambda i: (i, 0)),
        compiler_params=_params("parallel"),
        name="final_norm",
    )(x2, g)


NA_QROWS = 4
NA_KROWS = NA_QROWS + WIN_R
Q_COL, K_COL, V_COL = 4, 5, 6


def _na_key_start(j, rows):
    return np.clip(j * NA_QROWS - WIN_R // 2, 0, rows - NA_KROWS)


def _na_bias_index(rows):
    nblk = rows // NA_QROWS
    pats = []
    for j in range(nblk):
        start = _na_key_start(j, rows)
        r = j * NA_QROWS + np.arange(NA_QROWS)
        sr = np.clip(r - WIN_R // 2, 0, rows - WIN_R)
        kr = start + np.arange(NA_KROWS)
        rvalid = (kr[None, :] >= sr[:, None]) & (kr[None, :] < sr[:, None] + WIN_R)
        ri = np.clip(kr[None, :] - r[:, None] + WIN_R - 1, 0, 2 * WIN_R - 2)
        pats.append((ri, rvalid))
    for j in range(2, nblk - 1):
        assert all(np.array_equal(a, b) for a, b in zip(pats[1], pats[j]))
    sel = [pats[0], pats[1], pats[nblk - 1]]
    ri = np.stack([p[0] for p in sel])
    rvalid = np.stack([p[1] for p in sel])
    c = np.arange(GRID_W)
    sc = np.clip(c - WIN_C // 2, 0, GRID_W - WIN_C)
    cvalid = (c[None, :] >= sc[:, None]) & (c[None, :] < sc[:, None] + WIN_C)
    ci = np.clip(c[None, :] - c[:, None] + WIN_C - 1, 0, 2 * WIN_C - 2)
    c_onehot = (ci[..., None] == np.arange(2 * WIN_C - 1)).astype(np.float32)
    valid = rvalid[:, :, None, :, None] & cvalid[None, None, :, None, :]
    return ri, c_onehot, valid


def na_bias_table(rpb, rows):
    ri, c_onehot, valid = _na_bias_index(rows)
    toep = jnp.einsum('hrd,qkd->hrqk', rpb, c_onehot, precision=lax.Precision.HIGHEST)
    b = jnp.take(toep, ri.reshape(-1), axis=1).reshape((NA_HEADS,) + ri.shape + (GRID_W, GRID_W))
    b = jnp.transpose(b, (1, 0, 2, 4, 3, 5))
    b = jnp.where(valid[:, None], b, -jnp.inf)
    return b.reshape(3, NA_HEADS, NA_QROWS * GRID_W, NA_KROWS * GRID_W).astype(F32)


def _attend_heads(q, key_sets, bias_fn):
    n, w = q.shape
    lane = lax.broadcasted_iota(jnp.int32, (1, w), 1)
    out = jnp.zeros((n, w), F32)
    for h in range(NA_HEADS):
        mh = (lane >= h * NA_HEAD_DIM) & (lane < (h + 1) * NA_HEAD_DIM)
        qh = jnp.where(mh, q, 0.0).astype(BF16)
        scores = []
        for i, (k, _) in enumerate(key_sets):
            s = lax.dot_general(qh, k, (((1,), (1,)), ((), ())), preferred_element_type=F32)
            b = bias_fn(i, h)
            scores.append(s if b is None else s + b)
        m = scores[0].max(axis=-1, keepdims=True)
        for s in scores[1:]:
            m = jnp.maximum(m, s.max(axis=-1, keepdims=True))
        denom = jnp.zeros((n, 1), F32)
        acc = jnp.zeros((n, w), F32)
        for s, (_, v) in zip(scores, key_sets):
            p = jnp.exp(s - m)
            denom = denom + p.sum(axis=-1, keepdims=True)
            acc = acc + jnp.dot(p.astype(BF16), v, preferred_element_type=F32)
        out = out + jnp.where(mh, acc / denom, 0.0)
    return out


def _na_kernel(q_ref, k_ref, v_ref, kc_ref, vc_ref, bias_ref, o_ref, *, rows):
    j = pl.program_id(1)
    start = jnp.clip(j * NA_QROWS - WIN_R // 2, 0, rows - NA_KROWS)
    t0 = pl.multiple_of(start * GRID_W, GRID_W)
    nk = NA_KROWS * GRID_W
    kw = k_ref[0, pl.ds(t0, nk), :]
    vw = v_ref[0, pl.ds(t0, nk), :]
    q = q_ref[0] * (1.0 / math.sqrt(NA_HEAD_DIM))
    o = _attend_heads(q, [(kw, vw), (kc_ref[0], vc_ref[0])], lambda i, h: bias_ref[0, h] if i == 0 else None)
    o_ref[0] = o.astype(o_ref.dtype)


def na_attention(pr3, prc3, bias):
    bn, S, _ = pr3.shape
    Lc = prc3.shape[1]
    rows = S // GRID_W
    nblk = rows // NA_QROWS
    nq = NA_QROWS * GRID_W

    def pat(b, j):
        return (jnp.where(j == 0, 0, jnp.where(j == nblk - 1, 2, 1)), 0, 0, 0)

    return pl.pallas_call(
        functools.partial(_na_kernel, rows=rows),
        out_shape=jax.ShapeDtypeStruct((bn, S, W_NA), BF16),
        grid=(bn, nblk),
        in_specs=[pl.BlockSpec((1, nq, W_NA), lambda b, j: (b, j, Q_COL)),
                  pl.BlockSpec((1, S, W_NA), lambda b, j: (b, 0, K_COL)),
                  pl.BlockSpec((1, S, W_NA), lambda b, j: (b, 0, V_COL)),
                  pl.BlockSpec((1, Lc, W_NA), lambda b, j: (b, 0, K_COL)),
                  pl.BlockSpec((1, Lc, W_NA), lambda b, j: (b, 0, V_COL)),
                  pl.BlockSpec((1,) + bias.shape[1:], pat)],
        out_specs=pl.BlockSpec((1, nq, W_NA), lambda b, j: (b, j, 0)),
        compiler_params=_params("parallel", "arbitrary"),
        name="na_attention",
    )(pr3, pr3, pr3, prc3, prc3, bias)


def _ctx_attn_kernel(q_ref, k_ref, v_ref, o_ref):
    q = q_ref[0] * (1.0 / math.sqrt(NA_HEAD_DIM))
    o = _attend_heads(q, [(k_ref[0], v_ref[0])], lambda i, h: None)
    o_ref[0] = o.astype(o_ref.dtype)


def ctx_attention(prc3):
    bn, Lc, _ = prc3.shape
    return pl.pallas_call(
        _ctx_attn_kernel,
        out_shape=jax.ShapeDtypeStruct((bn, Lc, W_NA), BF16),
        grid=(bn,),
        in_specs=[pl.BlockSpec((1, Lc, W_NA), lambda b: (b, 0, Q_COL)),
                  pl.BlockSpec((1, Lc, W_NA), lambda b: (b, 0, K_COL)),
                  pl.BlockSpec((1, Lc, W_NA), lambda b: (b, 0, V_COL))],
        out_specs=pl.BlockSpec((1, Lc, W_NA), lambda b: (b, 0, 0)),
        compiler_params=_params("parallel"),
        name="ctx_attention",
    )(prc3, prc3, prc3)


POOL_COL, CONV_H_COL, CONV_B_COL, CONV_C_COL = 0, 1, 2, 3


def _shift_rows(x, k, row):
    n = x.shape[0]
    y = pltpu.roll(x, k % n, 0)
    return jnp.where(row < k, 0.0, y) if k > 0 else jnp.where(row >= n + k, 0.0, y)


def _local_mix_kernel(u_ref, h_ref, bg_ref, cg_ref, pw_ref, ps_ref, cw_ref, op_ref, oc_ref, *, seq):
    row = lax.broadcasted_iota(jnp.int32, (seq, 1), 0)
    lane = lax.broadcasted_iota(jnp.int32, (1, W_POOL), 1)
    u = u_ref[0].astype(F32)
    trailing, leading = {1: u}, {1: u}
    for w in (1, 2, 4):
        trailing[2 * w] = trailing[w] + _shift_rows(trailing[w], w, row)
        leading[2 * w] = leading[w] + _shift_rows(leading[w], -w, row)
    rowf = row.astype(F32)
    win_sum = jnp.zeros_like(u)
    cnt = jnp.zeros_like(u)
    for g, win in enumerate(POOL_WINDOWS):
        half = win // 2
        mg = (lane >= g * POOL_GROUP) & (lane < (g + 1) * POOL_GROUP)
        s = _shift_rows(trailing[half], 1, row) + leading[half]
        n = jnp.minimum(rowf + half, float(seq)) - jnp.maximum(rowf - half, 0.0)
        win_sum = jnp.where(mg, s, win_sum)
        cnt = jnp.where(mg, n, cnt)
    p = win_sum / cnt - u
    pooled = jnp.dot(p.astype(BF16), pw_ref[...], preferred_element_type=F32) * ps_ref[...]
    op_ref[0] = pooled.astype(op_ref.dtype)
    v = cg_ref[0].astype(F32) * h_ref[0].astype(F32)
    conv = (_shift_rows(v, 1, row) * cw_ref[0:1, :] + v * cw_ref[1:2, :] + _shift_rows(v, -1, row) * cw_ref[2:3, :])
    oc_ref[0] = (bg_ref[0].astype(F32) * conv).astype(oc_ref.dtype)


def local_mix(pr3, pool_w, pool_scale, conv_w):
    bn, seq, _ = pr3.shape
    pw = jax.scipy.linalg.block_diag(*[pool_w[g] for g in range(len(POOL_WINDOWS))]).astype(BF16)
    blk = lambda col: pl.BlockSpec((1, seq, W_POOL), lambda b: (b, 0, col))
    full = lambda a: pl.BlockSpec(a.shape, lambda b: (0,) * a.ndim)
    args = (pw, pool_scale[None, :], conv_w)
    return pl.pallas_call(
        functools.partial(_local_mix_kernel, seq=seq),
        out_shape=(jax.ShapeDtypeStruct((bn, seq, W_POOL), BF16), jax.ShapeDtypeStruct((bn, seq, W_CONV), BF16)),
        grid=(bn,),
        in_specs=[blk(POOL_COL), blk(CONV_H_COL), blk(CONV_B_COL), blk(CONV_C_COL)] + [full(a) for a in args],
        out_specs=(pl.BlockSpec((1, seq, W_POOL), lambda b: (b, 0, 0)),
                   pl.BlockSpec((1, seq, W_CONV), lambda b: (b, 0, 0))),
        compiler_params=_params("parallel"),
        name="local_mix",
    )(pr3, pr3, pr3, pr3, *args)


Z_COL, XS_COL, BS_COL, CS_COL = 7, 8, 9, 10
DT_COL = 22
HEADS_PER_GROUP = SSD_HEADS // SSD_GROUPS
GROUP_W = HEADS_PER_GROUP * SSD_HEAD_DIM


def _head_mask(h):
    lane = lax.broadcasted_iota(jnp.int32, (1, W_SSD), 1)
    return (lane >= h * SSD_HEAD_DIM) & (lane < (h + 1) * SSD_HEAD_DIM)


def _expand_heads(v, d):
    out = jnp.zeros((v.shape[0], W_SSD), F32)
    for h in range(SSD_HEADS):
        k = d * SSD_HEADS + h
        out = jnp.where(_head_mask(h), v[:, k:k + 1], out)
    return out


def _ssd_chunk_terms(c, xs_ref, b_ref, c_ref, dt_ref, la_ref, dsk_ref, y_ref, upd_ref, dec_ref, need_y):
    T = SSD_CHUNK
    r0 = pl.multiple_of(c * T, T)
    xs = xs_ref[pl.ds(r0, T), :]
    bm = b_ref[pl.ds(r0, T), :]
    cm = c_ref[pl.ds(r0, T), :]
    dt = dt_ref[pl.ds(r0, T), :]
    la = la_ref[pl.ds(r0, T), :]
    li = lax.broadcasted_iota(jnp.int32, (T, T), 0)
    si = lax.broadcasted_iota(jnp.int32, (T, T), 1)
    causal = si <= li
    prefix = jnp.dot(causal.astype(F32), la, precision=lax.Precision.HIGHEST, preferred_element_type=F32)
    total = prefix[T - 1:T, :]
    lane = lax.broadcasted_iota(jnp.int32, (1, LANES), 1)
    acum = jnp.where(lane < SSD_HEADS, prefix, total - prefix + la)
    bt = bm.astype(F32).T.astype(BF16)
    if need_y:
        acum_t = acum.T
        cb = [lax.dot_general(cm[:, g * SSD_STATE:(g + 1) * SSD_STATE], bm[:, g * SSD_STATE:(g + 1) * SSD_STATE],
                              (((1,), (1,)), ((), ())), preferred_element_type=F32) for g in range(SSD_GROUPS)]
        y = jnp.zeros((T, W_SSD), F32)
    for d in range(2):
        acum_e = _expand_heads(acum, d)
        tot_e = _expand_heads(total, d)
        xdt = xs * _expand_heads(dt, d)
        xw = (xdt * jnp.exp(tot_e - acum_e)).astype(BF16)
        upd = [jnp.dot(bt[g * SSD_STATE:(g + 1) * SSD_STATE, :], xw[:, g * GROUP_W:(g + 1) * GROUP_W],
                       preferred_element_type=F32) for g in range(SSD_GROUPS)]
        upd_ref[c, d] = jnp.concatenate(upd, axis=1)
        dec_ref[c, d, 0:T, :] = jnp.exp(acum_e)
        dec_ref[c, d, T:T + 1, :] = jnp.exp(tot_e)
        if need_y:
            mask = causal if d == 0 else (si >= li)
            xdt_b = xdt.astype(BF16)
            for h in range(SSD_HEADS):
                k = d * SSD_HEADS + h
                decay = jnp.exp(jnp.where(mask, acum[:, k:k + 1] - acum_t[k:k + 1, :], -jnp.inf))
                scores = (cb[h // HEADS_PER_GROUP] * decay).astype(BF16)
                y = y + jnp.where(_head_mask(h), jnp.dot(scores, xdt_b, preferred_element_type=F32), 0.0)
    if need_y:
        y_ref[pl.ds(r0, T), :] = xs * dsk_ref[...] + y


def _ssd_chunk_state(c, d, c_ref, y_ref, upd_ref, dec_ref, st_ref, need_y):
    T = SSD_CHUNK
    r0 = pl.multiple_of(c * T, T)
    st = st_ref[d]
    if need_y:
        cm = c_ref[pl.ds(r0, T), :]
        st_b = st.astype(BF16)
        ys = [jnp.dot(cm[:, g * SSD_STATE:(g + 1) * SSD_STATE], st_b[:, g * GROUP_W:(g + 1) * GROUP_W],
                      preferred_element_type=F32) for g in range(SSD_GROUPS)]
        y_ref[pl.ds(r0, T), :] += jnp.concatenate(ys, axis=1) * dec_ref[c, d, 0:T, :]
    st_ref[d] = dec_ref[c, d, T:T + 1, :] * st + upd_ref[c, d]


def _ssd_kernel(z_ref, xs_in, bs_in, cs_in, dtr_ref, cw_ref, cb_ref, dtb_ref, alog_ref, dsk_ref, ng_ref, h0_ref,
                *refs, seq, need_y):
    if need_y:
        o_ref, hT_ref, xs_ref, b_ref, c_ref, dt_ref, la_ref, upd_ref, dec_ref, st_ref, y_ref = refs
    else:
        hT_ref, xs_ref, b_ref, c_ref, dt_ref, la_ref, upd_ref, dec_ref, st_ref = refs
        y_ref = None
    nc = seq // SSD_CHUNK
    row = lax.broadcasted_iota(jnp.int32, (seq, 1), 0)
    for gi, (src, dst) in enumerate(((xs_in, xs_ref), (bs_in, b_ref), (cs_in, c_ref))):
        sl = slice(gi * W_SSD, (gi + 1) * W_SSD)
        x = src[0].astype(F32)
        cv = (_shift_rows(x, 1, row) * cw_ref[0:1, sl] + x * cw_ref[1:2, sl]
              + _shift_rows(x, -1, row) * cw_ref[2:3, sl] + cb_ref[:, sl])
        dst[...] = (cv * jax.nn.sigmoid(cv)).astype(dst.dtype)
    dtv = dtr_ref[0] + dtb_ref[...]
    dt = jnp.maximum(dtv, 0.0) + jnp.log1p(jnp.exp(-jnp.abs(dtv)))
    dt_ref[...] = dt
    la_ref[...] = dt * (-jnp.exp(alog_ref[...]))
    st_ref[...] = h0_ref[0]

    def terms(c, carry):
        _ssd_chunk_terms(c, xs_ref, b_ref, c_ref, dt_ref, la_ref, dsk_ref, y_ref, upd_ref, dec_ref, need_y)
        return carry

    lax.fori_loop(0, nc, terms, 0, unroll=2)

    def recur(i, carry):
        _ssd_chunk_state(i, 0, c_ref, y_ref, upd_ref, dec_ref, st_ref, need_y)
        _ssd_chunk_state(nc - 1 - i, 1, c_ref, y_ref, upd_ref, dec_ref, st_ref, need_y)
        return carry

    lax.fori_loop(0, nc, recur, 0, unroll=2)
    hT_ref[0] = st_ref[...]
    if need_y:
        z = z_ref[0].astype(F32)
        yz = y_ref[...] * (z * jax.nn.sigmoid(z))
        ms = jnp.mean(yz * yz, axis=-1, keepdims=True)
        o_ref[0] = (yz * lax.rsqrt(ms + EPS) * ng_ref[...]).astype(o_ref.dtype)


def ssd_scan(pr3, dt3, h0, conv_w, conv_b, dt_bias, a_log, d_skip, norm_g, need_y):
    bn, seq, _ = pr3.shape
    nc = seq // SSD_CHUNK
    pad = LANES - 2 * SSD_HEADS
    dtb = jnp.pad(dt_bias.reshape(1, -1), ((0, 0), (0, pad)))
    alog = jnp.pad(a_log.reshape(1, -1), ((0, 0), (0, pad)))
    dsk = jnp.repeat(d_skip, SSD_HEAD_DIM)[None, :]
    blk = lambda col: pl.BlockSpec((1, seq, W_SSD), lambda b: (b, 0, col))
    full = lambda a: pl.BlockSpec(a.shape, lambda b: (0,) * a.ndim)
    st_shape = (1, 2, SSD_STATE, W_SSD)
    st_spec = pl.BlockSpec(st_shape, lambda b: (b, 0, 0, 0))
    args = (conv_w, conv_b[None, :], dtb, alog, dsk, norm_g[None, :])
    out_shape = [jax.ShapeDtypeStruct((bn,) + st_shape[1:], F32)]
    out_specs = [st_spec]
    scratch = [pltpu.VMEM((seq, W_SSD), F32),
               pltpu.VMEM((seq, W_SSD), BF16), pltpu.VMEM((seq, W_SSD), BF16),
               pltpu.VMEM((seq, LANES), F32), pltpu.VMEM((seq, LANES), F32),
               pltpu.VMEM((nc, 2, SSD_STATE, W_SSD), F32),
               pltpu.VMEM((nc, 2, SSD_CHUNK + 8, W_SSD), F32),
               pltpu.VMEM(st_shape[1:], F32)]
    if need_y:
        out_shape.insert(0, jax.ShapeDtypeStruct((bn, seq, W_SSD), BF16))
        out_specs.insert(0, pl.BlockSpec((1, seq, W_SSD), lambda b: (b, 0, 0)))
        scratch.append(pltpu.VMEM((seq, W_SSD), F32))
    res = pl.pallas_call(
        functools.partial(_ssd_kernel, seq=seq, need_y=need_y),
        out_shape=out_shape,
        grid=(bn,),
        in_specs=[blk(Z_COL), blk(XS_COL), blk(BS_COL), blk(CS_COL),
                  pl.BlockSpec((1, seq, LANES), lambda b: (b, 0, 0))]
                 + [full(a) for a in args] + [st_spec],
        out_specs=out_specs,
        scratch_shapes=scratch,
        compiler_params=_params("parallel"),
        name="ssd_scan",
    )(pr3, pr3, pr3, pr3, dt3, *args, h0)
    return (res[0], res[1]) if need_y else (None, res[0])


def ssd_mix(pr3, dt3, prc3, dtc3, conv_w, conv_b, dt_bias, a_log, d_skip, norm_g, ctx_out):
    bn = pr3.shape[0]
    h0 = jnp.zeros((bn, 2, SSD_STATE, W_SSD), F32)
    oc, hc = ssd_scan(prc3, dtc3, h0, conv_w, conv_b, dt_bias, a_log, d_skip, norm_g, ctx_out)
    o, _ = ssd_scan(pr3, dt3, hc, conv_w, conv_b, dt_bias, a_log, d_skip, norm_g, True)
    return o, oc


def mixer(pr3, dt3, prc3, dtc3, pool_w, pool_scale, conv_w, rpb, s_conv_w, s_conv_b, dt_bias, a_log, d_skip,
          s_norm_g, ctx_out):
    bn, S, _ = pr3.shape
    o_ssd, oc_ssd = ssd_mix(pr3, dt3, prc3, dtc3, s_conv_w, s_conv_b, dt_bias, a_log, d_skip, s_norm_g, ctx_out)
    o = [*local_mix(pr3, pool_w, pool_scale, conv_w),
         na_attention(pr3, prc3, na_bias_table(rpb, S // GRID_W)), o_ssd]
    if not ctx_out:
        return o, None
    oc = [*local_mix(prc3, pool_w, pool_scale, conv_w), ctx_attention(prc3), oc_ssd]
    return o, oc


TM = 512
TM_PROJ = 1024
TM_GRP = 512


def kernel(x, c, ctx, c_ctx, w_ada, b_ada, g_mix, g_ffn, w_in, w_out, pool_w, pool_scale, conv_w, na_rpb,
           ssd_conv_w, ssd_conv_b, ssd_dt_bias, ssd_a_log, ssd_d, ssd_norm_g, ffn_w_gu, ffn_w_down,
           moe_router, moe_w_gu, moe_w_down, g_final):
    bn, S, d = x.shape
    Lc = ctx.shape[1]
    m, mc = bn * S, bn * Lc
    tpb = S // TM
    x2 = x.reshape(m, d)
    xc2 = ctx.reshape(mc, d)
    c_rows = jnp.concatenate([c, c_ctx[None, :], jnp.zeros((7, d), F32)], axis=0)
    for l in range(DEPTH):
        ctx_out = l < DEPTH - 1
        last = l == DEPTH - 1
        ada = ada_modulation(c_rows, w_ada[l].astype(BF16), b_ada[l][None, :])
        mod = ada[:bn].reshape(bn, 6, d)
        mod_c = ada[bn:bn + 1].reshape(1, 6, d)
        w_in_l = jnp.pad(w_in[l], ((0, 0), (0, D_IN_PAD - D_IN_PROJ))).astype(BF16)
        g_m = g_mix[l][None, :]
        g_f = g_ffn[l][None, :]
        pr, dt = in_proj(x2, g_m, mod, w_in_l, S // TM_PROJ, TM_PROJ)
        pr_c, dt_c = in_proj(xc2, g_m, mod_c, w_in_l, None, min(TM_PROJ, mc))
        o, oc = mixer(pr.reshape(bn, S, -1), dt.reshape(bn, S, -1), pr_c.reshape(bn, Lc, -1),
                      dt_c.reshape(bn, Lc, -1), pool_w[l], pool_scale[l], conv_w[l],
                      na_rpb[l], ssd_conv_w[l], ssd_conv_b[l], ssd_dt_bias[l], ssd_a_log[l], ssd_d[l],
                      ssd_norm_g[l], ctx_out)
        w_out_l = w_out[l].astype(BF16)
        o = [p.reshape(m, -1) for p in o]
        oc = [p.reshape(mc, -1) for p in oc] if ctx_out else None
        j = l // 2
        if l % 2 == 0:
            w_gu = ffn_w_gu[j].astype(BF16)
            w_dn = ffn_w_down[j].astype(BF16)
            x2 = ffn_dense(x2, o, g_f, mod, w_out_l, w_gu, w_dn, tpb, TM)
            if ctx_out:
                xc2 = ffn_dense(xc2, oc, g_f, mod_c, w_out_l, w_gu, w_dn, None, min(TM, mc))
        else:
            w_r = jnp.pad(moe_router[j], ((0, 0), (0, LANES - N_EXPERTS))).astype(BF16)
            w_gu = moe_w_gu[j].astype(BF16)
            w_dn = moe_w_down[j].astype(BF16)
            x2 = moe_block(x2, o, g_f, mod, w_out_l, w_r, w_gu, w_dn, tpb, TM, TM_GRP,
                           g_final[None, :] if last else None)
            if ctx_out:
                xc2 = moe_block(xc2, oc, g_f, mod_c, w_out_l, w_r, w_gu, w_dn, None, min(TM, mc), TM_GRP)
            if last:
                return x2.reshape(bn, S, d)
    return final_norm(x2, g_final[None, :], TM).reshape(bn, S, d)
```

```python
import functools
import math

import numpy as np
import jax
import jax.numpy as jnp
from jax import lax
from jax.experimental import pallas as pl
from jax.experimental.pallas import tpu as pltpu

DEPTH = 2
GRID_W = 64
EPS = 1e-6
W_POOL = 256
W_CONV = 256
W_NA = 256
W_SSD = 256
POOL_WINDOWS = (2, 4, 8, 16)
POOL_GROUP = W_POOL // len(POOL_WINDOWS)
NA_HEADS = 4
NA_HEAD_DIM = W_NA // NA_HEADS
WIN_R = 8
WIN_C = 16
SSD_HEAD_DIM = 64
SSD_HEADS = W_SSD // SSD_HEAD_DIM
SSD_GROUPS = 2
SSD_STATE = 128
SSD_CHUNK = 128
SSD_XBC = W_SSD + 2 * SSD_GROUPS * SSD_STATE
D_IN_PROJ = W_POOL + 3 * W_CONV + 3 * W_NA + W_SSD + SSD_XBC + 2 * SSD_HEADS
N_EXPERTS = 8
TOP_K = 2

LANES = 128
D_IN_PAD = -(-D_IN_PROJ // LANES) * LANES
VMEM_LIMIT = 56 * 1024 * 1024
BF16 = jnp.bfloat16
F32 = jnp.float32


def _params(*sem):
    return pltpu.CompilerParams(dimension_semantics=sem, vmem_limit_bytes=VMEM_LIMIT)


def _norm_mod(x, g, scale, shift):
    ms = jnp.mean(x * x, axis=-1, keepdims=True)
    return (x * lax.rsqrt(ms + EPS) * g) * (1.0 + scale) + shift


def _mod_map(tiles_per_batch):
    if tiles_per_batch is None:
        return lambda i, *_: (0, 0, 0)
    return lambda i, *_: (i // tiles_per_batch, 0, 0)


def _resident(shape, index_map):
    return pl.BlockSpec(shape, index_map, pipeline_mode=pl.Buffered(1))


def _ada_kernel(c_ref, w_ref, b_ref, o_ref):
    c = c_ref[...]
    s = c * jax.nn.sigmoid(c)
    o_ref[...] = jnp.dot(s.astype(BF16), w_ref[...], preferred_element_type=F32) + b_ref[...]


def ada_modulation(c_rows, w, b):
    r, d = c_rows.shape
    n = w.shape[1]
    tn = 1536
    return pl.pallas_call(
        _ada_kernel,
        out_shape=jax.ShapeDtypeStruct((r, n), F32),
        grid=(n // tn,),
        in_specs=[pl.BlockSpec((r, d), lambda j: (0, 0)),
                  pl.BlockSpec((d, tn), lambda j: (0, j)),
                  pl.BlockSpec((1, tn), lambda j: (0, j))],
        out_specs=pl.BlockSpec((r, tn), lambda j: (0, j)),
        compiler_params=_params("arbitrary"),
        name="ada_modulation",
    )(c_rows, w, b)


def _in_proj_kernel(x_ref, g_ref, mod_ref, w_ref, o_ref, dt_ref):
    h = _norm_mod(x_ref[...], g_ref[...], mod_ref[0, 1:2, :], mod_ref[0, 0:1, :])
    pr = jnp.dot(h.astype(BF16), w_ref[...], preferred_element_type=F32)
    o_ref[...] = pr.astype(o_ref.dtype)
    dt_ref[...] = pr[:, DT_COL * LANES:(DT_COL + 1) * LANES]


def in_proj(x2, g, mod, w, tiles_per_batch, tm):
    m, d = x2.shape
    n = w.shape[1]
    return pl.pallas_call(
        _in_proj_kernel,
        out_shape=(jax.ShapeDtypeStruct((m, n), BF16), jax.ShapeDtypeStruct((m, LANES), F32)),
        grid=(m // tm,),
        in_specs=[pl.BlockSpec((tm, d), lambda i: (i, 0)),
                  pl.BlockSpec((1, d), lambda i: (0, 0)),
                  pl.BlockSpec((1, 6, d), _mod_map(tiles_per_batch)),
                  _resident((d, n), lambda i: (0, 0))],
        out_specs=(pl.BlockSpec((tm, n), lambda i: (i, 0)), pl.BlockSpec((tm, LANES), lambda i: (i, 0))),
        compiler_params=_params("parallel"),
        name="in_proj",
    )(x2, g, mod, w)


def _mix_residual(x, mod_ref, w_ref, part_refs):
    y, k0 = None, 0
    for p_ref in part_refs:
        k = p_ref.shape[1]
        t = jnp.dot(p_ref[...], w_ref[k0:k0 + k, :], preferred_element_type=F32)
        y = t if y is None else y + t
        k0 += k
    return x + mod_ref[0, 2:3, :] * y


def _part_specs(parts, tm):
    return [pl.BlockSpec((tm, p.shape[1]), lambda i, *_: (i, 0)) for p in parts]


FF_CHUNK = 512


def _swiglu(h, wgu, wd, ff):
    acc = None
    for c0 in range(0, ff, FF_CHUNK):
        c1 = min(c0 + FF_CHUNK, ff)
        gg = jnp.dot(h, wgu(c0, c1), preferred_element_type=F32)
        uu = jnp.dot(h, wgu(ff + c0, ff + c1), preferred_element_type=F32)
        a = (gg * jax.nn.sigmoid(gg) * uu).astype(BF16)
        part = jnp.dot(a, wd(c0, c1), preferred_element_type=F32)
        acc = part if acc is None else acc + part
    return acc


def _ffn_kernel(x_ref, g_ref, mod_ref, wout_ref, wgu_ref, wd_ref, *refs):
    x = _mix_residual(x_ref[...], mod_ref, wout_ref, refs[:-1])
    y_ref = refs[-1]
    h = _norm_mod(x, g_ref[...], mod_ref[0, 4:5, :], mod_ref[0, 3:4, :]).astype(BF16)
    y = _swiglu(h, lambda a, b: wgu_ref[:, a:b], lambda a, b: wd_ref[a:b, :], wd_ref.shape[0])
    y_ref[...] = x + mod_ref[0, 5:6, :] * y


def ffn_dense(x2, parts, g, mod, w_out, w_gu, w_down, tiles_per_batch, tm):
    m, d = x2.shape
    return pl.pallas_call(
        _ffn_kernel,
        out_shape=jax.ShapeDtypeStruct((m, d), F32),
        grid=(m // tm,),
        in_specs=[pl.BlockSpec((tm, d), lambda i: (i, 0)),
                  pl.BlockSpec((1, d), lambda i: (0, 0)),
                  pl.BlockSpec((1, 6, d), _mod_map(tiles_per_batch)),
                  _resident(w_out.shape, lambda i: (0, 0)),
                  _resident(w_gu.shape, lambda i: (0, 0)),
                  _resident(w_down.shape, lambda i: (0, 0))] + _part_specs(parts, tm),
        out_specs=pl.BlockSpec((tm, d), lambda i: (i, 0)),
        compiler_params=_params("parallel"),
        name="ffn_dense",
    )(x2, g, mod, w_out, w_gu, w_down, *parts)


ROUTE_E1, ROUTE_E2, ROUTE_R1, ROUTE_R2, ROUTE_G1, ROUTE_G2 = range(6)
ROUTE_ROWS = 8


def _router_kernel(x_ref, g_ref, mod_ref, wout_ref, wr_ref, *refs):
    part_refs = refs[:-6]
    x1_ref, h_ref, route_ref, route_t_ref, cnt_ref, base_ref = refs[-6:]

    @pl.when(pl.program_id(0) == 0)
    def _():
        base_ref[...] = jnp.zeros_like(base_ref)

    x = _mix_residual(x_ref[...], mod_ref, wout_ref, part_refs)
    x1_ref[...] = x
    h = _norm_mod(x, g_ref[...], mod_ref[0, 4:5, :], mod_ref[0, 3:4, :]).astype(BF16)
    h_ref[...] = h
    tm = h.shape[0]
    logits = jnp.dot(h, wr_ref[...], preferred_element_type=F32)
    lane = lax.broadcasted_iota(jnp.int32, logits.shape, 1)
    neg = jnp.float32(-jnp.inf)
    logits = jnp.where(lane < N_EXPERTS, logits, neg)
    m1 = jnp.max(logits, axis=-1, keepdims=True)
    i1 = jnp.min(jnp.where(logits == m1, lane, LANES), axis=-1, keepdims=True)
    rest = jnp.where(lane == i1, neg, logits)
    m2 = jnp.max(rest, axis=-1, keepdims=True)
    i2 = jnp.min(jnp.where(rest == m2, lane, LANES), axis=-1, keepdims=True)
    e2 = jnp.exp(m2 - m1)
    g1 = 1.0 / (1.0 + e2)
    g2 = e2 / (1.0 + e2)
    sel1, sel2 = lane == i1, lane == i2
    sel = jnp.where(sel1 | sel2, 1.0, 0.0)
    li = lax.broadcasted_iota(jnp.int32, (tm, tm), 0)
    si = lax.broadcasted_iota(jnp.int32, (tm, tm), 1)
    before = jnp.where(si < li, 1.0, 0.0).astype(BF16)
    rank = jnp.dot(before, sel.astype(BF16), preferred_element_type=F32) + base_ref[...]
    r1 = jnp.sum(jnp.where(sel1, rank, 0.0), axis=-1, keepdims=True)
    r2 = jnp.sum(jnp.where(sel2, rank, 0.0), axis=-1, keepdims=True)
    base_ref[...] += jnp.sum(sel, axis=0, keepdims=True)
    cnt_ref[...] = base_ref[...]
    fields = (i1.astype(F32), i2.astype(F32), r1, r2, g1, g2)
    route = jnp.zeros(logits.shape, F32)
    for k, v in enumerate(fields):
        route = jnp.where(lane == k, v, route)
    route_ref[...] = route
    route_t_ref[...] = route.T[:ROUTE_ROWS, :]


def moe_router(x2, parts, g, mod, w_out, w_router, tiles_per_batch, tm):
    m, d = x2.shape
    return pl.pallas_call(
        _router_kernel,
        out_shape=(jax.ShapeDtypeStruct((m, d), F32), jax.ShapeDtypeStruct((m, d), BF16),
                   jax.ShapeDtypeStruct((m, LANES), F32), jax.ShapeDtypeStruct((ROUTE_ROWS, m), F32),
                   jax.ShapeDtypeStruct((1, LANES), F32)),
        grid=(m // tm,),
        in_specs=[pl.BlockSpec((tm, d), lambda i: (i, 0)),
                  pl.BlockSpec((1, d), lambda i: (0, 0)),
                  pl.BlockSpec((1, 6, d), _mod_map(tiles_per_batch)),
                  _resident(w_out.shape, lambda i: (0, 0)),
                  pl.BlockSpec((d, LANES), lambda i: (0, 0))] + _part_specs(parts, tm),
        out_specs=(pl.BlockSpec((tm, d), lambda i: (i, 0)),
                   pl.BlockSpec((tm, d), lambda i: (i, 0)),
                   pl.BlockSpec((tm, LANES), lambda i: (i, 0)),
                   pl.BlockSpec((ROUTE_ROWS, tm), lambda i: (0, i)),
                   pl.BlockSpec((1, LANES), lambda i: (0, 0))),
        scratch_shapes=[pltpu.VMEM((1, LANES), F32)],
        compiler_params=_params("arbitrary"),
        name="moe_router",
    )(x2, g, mod, w_out, w_router, *parts)


def _moe_kernel(tile_ref, exp_ref, start_ref, end_ref, xg_ref, wgu_ref, wd_ref, y_ref, wgu_b, wd_b):
    w = pl.program_id(0)
    tm = xg_ref.shape[0]
    start, end = start_ref[w], end_ref[w]
    t0 = tile_ref[w] * tm

    @pl.when((w == 0) | (exp_ref[w] != exp_ref[jnp.maximum(w - 1, 0)]))
    def _():
        wgu_b[...] = wgu_ref[0].astype(BF16)
        wd_b[...] = wd_ref[0].astype(BF16)

    @pl.when(end > start)
    def _():
        y = _swiglu(xg_ref[...], lambda a, b: wgu_b[:, a:b], lambda a, b: wd_b[a:b, :], wd_b.shape[0])
        y = y.astype(y_ref.dtype)

        @pl.when(start == t0)
        def _():
            y_ref[...] = y

        @pl.when(start != t0)
        def _():
            row = t0 + lax.broadcasted_iota(jnp.int32, (tm, 1), 0)
            y_ref[...] = jnp.where(row >= start, y, y_ref[...])


def moe_grouped(item_tile, item_expert, item_start, item_end, xg, w_gu, w_down, tm):
    p, d = xg.shape
    grid_spec = pltpu.PrefetchScalarGridSpec(
        num_scalar_prefetch=4,
        grid=(item_tile.shape[0],),
        in_specs=[pl.BlockSpec((tm, d), lambda w, it, ie, s, e: (it[w], 0)),
                  _resident((1,) + w_gu.shape[1:], lambda w, it, ie, s, e: (ie[w], 0, 0)),
                  _resident((1,) + w_down.shape[1:], lambda w, it, ie, s, e: (ie[w], 0, 0))],
        out_specs=pl.BlockSpec((tm, d), lambda w, it, ie, s, e: (it[w], 0)),
        scratch_shapes=[pltpu.VMEM(w_gu.shape[1:], BF16), pltpu.VMEM(w_down.shape[1:], BF16)],
    )
    return pl.pallas_call(
        _moe_kernel,
        out_shape=jax.ShapeDtypeStruct((p, d), BF16),
        grid_spec=grid_spec,
        compiler_params=_params("arbitrary"),
        name="moe_grouped",
    )(item_tile, item_expert, item_start, item_end, xg, w_gu, w_down)


def _moe_combine_kernel(x_ref, yab_ref, route_ref, mod_ref, *refs):
    o_ref = refs[-1]
    d = x_ref.shape[1]
    r = route_ref[...]
    y = (r[:, ROUTE_G1:ROUTE_G1 + 1] * yab_ref[:, 0:d].astype(F32)
         + r[:, ROUTE_G2:ROUTE_G2 + 1] * yab_ref[:, d:2 * d].astype(F32))
    x = x_ref[...] + mod_ref[0, 5:6, :] * y
    if len(refs) == 2:
        ms = jnp.mean(x * x, axis=-1, keepdims=True)
        x = x * lax.rsqrt(ms + EPS) * refs[0][...]
    o_ref[...] = x


def moe_combine(x2, yab, route, mod, tiles_per_batch, tm, g_final=None):
    m, d = x2.shape
    row = pl.BlockSpec((tm, d), lambda i: (i, 0))
    specs = [row, pl.BlockSpec((tm, 2 * d), lambda i: (i, 0)), pl.BlockSpec((tm, LANES), lambda i: (i, 0)),
             pl.BlockSpec((1, 6, d), _mod_map(tiles_per_batch))]
    args = [x2, yab, route, mod]
    if g_final is not None:
        specs.append(pl.BlockSpec((1, d), lambda i: (0, 0)))
        args.append(g_final)
    return pl.pallas_call(
        _moe_combine_kernel,
        out_shape=jax.ShapeDtypeStruct((m, d), F32),
        grid=(m // tm,),
        in_specs=specs,
        out_specs=row,
        compiler_params=_params("parallel"),
        name="moe_combine",
    )(*args)


def moe_block(x2, parts, g, mod, w_out, w_router, w_gu, w_down, tiles_per_batch, tm_tok, tm_grp, g_final=None):
    m, d = x2.shape
    x2, h, route, route_t, cnt = moe_router(x2, parts, g, mod, w_out, w_router, tiles_per_batch, tm_tok)
    cnt = cnt[0, :N_EXPERTS].astype(jnp.int32)
    p = m * TOP_K
    off = jnp.cumsum(cnt) - cnt
    field = lambda k: route_t[k].astype(jnp.int32)

    def row_of(e, r):
        for k in range(N_EXPERTS):
            r = r + jnp.where(e == k, off[k], 0)
        return r

    d1 = row_of(field(ROUTE_E1), field(ROUTE_R1))
    d2 = row_of(field(ROUTE_E2), field(ROUTE_R2))
    tok = jnp.arange(m, dtype=jnp.int32)
    _, src = lax.sort_key_val(jnp.concatenate([d1, d2]), jnp.concatenate([tok, tok]))
    n_row_tiles = p // tm_grp
    start = jnp.sort(jnp.concatenate([jnp.arange(n_row_tiles, dtype=jnp.int32) * tm_grp, off]))
    end = jnp.concatenate([start[1:], jnp.full((1,), p, jnp.int32)])
    item_tile = jnp.minimum(start // tm_grp, n_row_tiles - 1)
    item_expert = jnp.sum((off[None, :] <= start[:, None]).astype(jnp.int32), axis=1) - 1
    rows = lambda a, idx: a.at[idx].get(mode="promise_in_bounds")
    yg = moe_grouped(item_tile, item_expert, start, end, rows(h, src), w_gu, w_down, tm_grp)
    yab = rows(yg, jnp.stack([d1, d2], axis=1).reshape(-1)).reshape(m, TOP_K * d)
    return moe_combine(x2, yab, route, mod, tiles_per_batch, tm_tok, g_final)


def _final_norm_kernel(x_ref, g_ref, o_ref):
    x = x_ref[...]
    ms = jnp.mean(x * x, axis=-1, keepdims=True)
    o_ref[...] = x * lax.rsqrt(ms + EPS) * g_ref[...]


def final_norm(x2, g, tm):
    m, d = x2.shape
    return pl.pallas_call(
        _final_norm_kernel,
        out_shape=jax.ShapeDtypeStruct((m, d), F32),
        grid=(m // tm,),
        in_specs=[pl.BlockSpec((tm, d), lambda i: (i, 0)), pl.BlockSpec((1, d), lambda i: (0, 0))],
        out_specs=pl.BlockSpec((tm, d), lambda i: (i, 0)),
        compiler_params=_params("parallel"),
        name="final_norm",
    )(x2, g)


NA_QROWS = 4
NA_KROWS = NA_QROWS + WIN_R
Q_COL, K_COL, V_COL = 4, 5, 6


def _na_key_start(j, rows):
    return np.clip(j * NA_QROWS - WIN_R // 2, 0, rows - NA_KROWS)


def _na_bias_index(rows):
    nblk = rows // NA_QROWS
    pats = []
    for j in range(nblk):
        start = _na_key_start(j, rows)
        r = j * NA_QROWS + np.arange(NA_QROWS)
        sr = np.clip(r - WIN_R // 2, 0, rows - WIN_R)
        kr = start + np.arange(NA_KROWS)
        rvalid = (kr[None, :] >= sr[:, None]) & (kr[None, :] < sr[:, None] + WIN_R)
        ri = np.clip(kr[None, :] - r[:, None] + WIN_R - 1, 0, 2 * WIN_R - 2)
        pats.append((ri, rvalid))
    for j in range(2, nblk - 1):
        assert all(np.array_equal(a, b) for a, b in zip(pats[1], pats[j]))
    sel = [pats[0], pats[1], pats[nblk - 1]]
    ri = np.stack([p[0] for p in sel])
    rvalid = np.stack([p[1] for p in sel])
    c = np.arange(GRID_W)
    sc = np.clip(c - WIN_C // 2, 0, GRID_W - WIN_C)
    cvalid = (c[None, :] >= sc[:, None]) & (c[None, :] < sc[:, None] + WIN_C)
    ci = np.clip(c[None, :] - c[:, None] + WIN_C - 1, 0, 2 * WIN_C - 2)
    c_onehot = (ci[..., None] == np.arange(2 * WIN_C - 1)).astype(np.float32)
    valid = rvalid[:, :, None, :, None] & cvalid[None, None, :, None, :]
    return ri, c_onehot, valid


def na_bias_table(rpb, rows):
    ri, c_onehot, valid = _na_bias_index(rows)
    toep = jnp.einsum('hrd,qkd->hrqk', rpb, c_onehot, precision=lax.Precision.HIGHEST)
    b = jnp.take(toep, ri.reshape(-1), axis=1).reshape((NA_HEADS,) + ri.shape + (GRID_W, GRID_W))
    b = jnp.transpose(b, (1, 0, 2, 4, 3, 5))
    b = jnp.where(valid[:, None], b, -jnp.inf)
    return b.reshape(3, NA_HEADS, NA_QROWS * GRID_W, NA_KROWS * GRID_W).astype(F32)


def _attend_heads(q, key_sets, bias_fn):
    n, w = q.shape
    lane = lax.broadcasted_iota(jnp.int32, (1, w), 1)
    out = jnp.zeros((n, w), F32)
    for h in range(NA_HEADS):
        mh = (lane >= h * NA_HEAD_DIM) & (lane < (h + 1) * NA_HEAD_DIM)
        qh = jnp.where(mh, q, 0.0).astype(BF16)
        scores = []
        for i, (k, _) in enumerate(key_sets):
            s = lax.dot_general(qh, k, (((1,), (1,)), ((), ())), preferred_element_type=F32)
            b = bias_fn(i, h)
            scores.append(s if b is None else s + b)
        m = scores[0].max(axis=-1, keepdims=True)
        for s in scores[1:]:
            m = jnp.maximum(m, s.max(axis=-1, keepdims=True))
        denom = jnp.zeros((n, 1), F32)
        acc = jnp.zeros((n, w), F32)
        for s, (_, v) in zip(scores, key_sets):
            p = jnp.exp(s - m)
            denom = denom + p.sum(axis=-1, keepdims=True)
            acc = acc + jnp.dot(p.astype(BF16), v, preferred_element_type=F32)
        out = out + jnp.where(mh, acc / denom, 0.0)
    return out


def _na_kernel(q_ref, k_ref, v_ref, kc_ref, vc_ref, bias_ref, o_ref, *, rows):
    j = pl.program_id(1)
    start = jnp.clip(j * NA_QROWS - WIN_R // 2, 0, rows - NA_KROWS)
    t0 = pl.multiple_of(start * GRID_W, GRID_W)
    nk = NA_KROWS * GRID_W
    kw = k_ref[0, pl.ds(t0, nk), :]
    vw = v_ref[0, pl.ds(t0, nk), :]
    q = q_ref[0] * (1.0 / math.sqrt(NA_HEAD_DIM))
    o = _attend_heads(q, [(kw, vw), (kc_ref[0], vc_ref[0])], lambda i, h: bias_ref[0, h] if i == 0 else None)
    o_ref[0] = o.astype(o_ref.dtype)


def na_attention(pr3, prc3, bias):
    bn, S, _ = pr3.shape
    Lc = prc3.shape[1]
    rows = S // GRID_W
    nblk = rows // NA_QROWS
    nq = NA_QROWS * GRID_W

    def pat(b, j):
        return (jnp.where(j == 0, 0, jnp.where(j == nblk - 1, 2, 1)), 0, 0, 0)

    return pl.pallas_call(
        functools.partial(_na_kernel, rows=rows),
        out_shape=jax.ShapeDtypeStruct((bn, S, W_NA), BF16),
        grid=(bn, nblk),
        in_specs=[pl.BlockSpec((1, nq, W_NA), lambda b, j: (b, j, Q_COL)),
                  pl.BlockSpec((1, S, W_NA), lambda b, j: (b, 0, K_COL)),
                  pl.BlockSpec((1, S, W_NA), lambda b, j: (b, 0, V_COL)),
                  pl.BlockSpec((1, Lc, W_NA), lambda b, j: (b, 0, K_COL)),
                  pl.BlockSpec((1, Lc, W_NA), lambda b, j: (b, 0, V_COL)),
                  pl.BlockSpec((1,) + bias.shape[1:], pat)],
        out_specs=pl.BlockSpec((1, nq, W_NA), lambda b, j: (b, j, 0)),
        compiler_params=_params("parallel", "arbitrary"),
        name="na_attention",
    )(pr3, pr3, pr3, prc3, prc3, bias)


def _ctx_attn_kernel(q_ref, k_ref, v_ref, o_ref):
    q = q_ref[0] * (1.0 / math.sqrt(NA_HEAD_DIM))
    o = _attend_heads(q, [(k_ref[0], v_ref[0])], lambda i, h: None)
    o_ref[0] = o.astype(o_ref.dtype)


def ctx_attention(prc3):
    bn, Lc, _ = prc3.shape
    return pl.pallas_call(
        _ctx_attn_kernel,
        out_shape=jax.ShapeDtypeStruct((bn, Lc, W_NA), BF16),
        grid=(bn,),
        in_specs=[pl.BlockSpec((1, Lc, W_NA), lambda b: (b, 0, Q_COL)),
                  pl.BlockSpec((1, Lc, W_NA), lambda b: (b, 0, K_COL)),
                  pl.BlockSpec((1, Lc, W_NA), lambda b: (b, 0, V_COL))],
        out_specs=pl.BlockSpec((1, Lc, W_NA), lambda b: (b, 0, 0)),
        compiler_params=_params("parallel"),
        name="ctx_attention",
    )(prc3, prc3, prc3)


POOL_COL, CONV_H_COL, CONV_B_COL, CONV_C_COL = 0, 1, 2, 3


def _shift_rows(x, k, row):
    n = x.shape[0]
    y = pltpu.roll(x, k % n, 0)
    return jnp.where(row < k, 0.0, y) if k > 0 else jnp.where(row >= n + k, 0.0, y)


def _local_mix_kernel(u_ref, h_ref, bg_ref, cg_ref, pw_ref, ps_ref, cw_ref, op_ref, oc_ref, *, seq):
    row = lax.broadcasted_iota(jnp.int32, (seq, 1), 0)
    lane = lax.broadcasted_iota(jnp.int32, (1, W_POOL), 1)
    u = u_ref[0].astype(F32)
    trailing, leading = {1: u}, {1: u}
    for w in (1, 2, 4):
        trailing[2 * w] = trailing[w] + _shift_rows(trailing[w], w, row)
        leading[2 * w] = leading[w] + _shift_rows(leading[w], -w, row)
    rowf = row.astype(F32)
    win_sum = jnp.zeros_like(u)
    cnt = jnp.zeros_like(u)
    for g, win in enumerate(POOL_WINDOWS):
        half = win // 2
        mg = (lane >= g * POOL_GROUP) & (lane < (g + 1) * POOL_GROUP)
        s = _shift_rows(trailing[half], 1, row) + leading[half]
        n = jnp.minimum(rowf + half, float(seq)) - jnp.maximum(rowf - half, 0.0)
        win_sum = jnp.where(mg, s, win_sum)
        cnt = jnp.where(mg, n, cnt)
    p = win_sum / cnt - u
    pooled = jnp.dot(p.astype(BF16), pw_ref[...], preferred_element_type=F32) * ps_ref[...]
    op_ref[0] = pooled.astype(op_ref.dtype)
    v = cg_ref[0].astype(F32) * h_ref[0].astype(F32)
    conv = (_shift_rows(v, 1, row) * cw_ref[0:1, :] + v * cw_ref[1:2, :] + _shift_rows(v, -1, row) * cw_ref[2:3, :])
    oc_ref[0] = (bg_ref[0].astype(F32) * conv).astype(oc_ref.dtype)


def local_mix(pr3, pool_w, pool_scale, conv_w):
    bn, seq, _ = pr3.shape
    pw = jax.scipy.linalg.block_diag(*[pool_w[g] for g in range(len(POOL_WINDOWS))]).astype(BF16)
    blk = lambda col: pl.BlockSpec((1, seq, W_POOL), lambda b: (b, 0, col))
    full = lambda a: pl.BlockSpec(a.shape, lambda b: (0,) * a.ndim)
    args = (pw, pool_scale[None, :], conv_w)
    return pl.pallas_call(
        functools.partial(_local_mix_kernel, seq=seq),
        out_shape=(jax.ShapeDtypeStruct((bn, seq, W_POOL), BF16), jax.ShapeDtypeStruct((bn, seq, W_CONV), BF16)),
        grid=(bn,),
        in_specs=[blk(POOL_COL), blk(CONV_H_COL), blk(CONV_B_COL), blk(CONV_C_COL)] + [full(a) for a in args],
        out_specs=(pl.BlockSpec((1, seq, W_POOL), lambda b: (b, 0, 0)),
                   pl.BlockSpec((1, seq, W_CONV), lambda b: (b, 0, 0))),
        compiler_params=_params("parallel"),
        name="local_mix",
    )(pr3, pr3, pr3, pr3, *args)


Z_COL, XS_COL, BS_COL, CS_COL = 7, 8, 9, 10
DT_COL = 22
HEADS_PER_GROUP = SSD_HEADS // SSD_GROUPS
GROUP_W = HEADS_PER_GROUP * SSD_HEAD_DIM


def _head_mask(h):
    lane = lax.broadcasted_iota(jnp.int32, (1, W_SSD), 1)
    return (lane >= h * SSD_HEAD_DIM) & (lane < (h + 1) * SSD_HEAD_DIM)


def _expand_heads(v, d):
    out = jnp.zeros((v.shape[0], W_SSD), F32)
    for h in range(SSD_HEADS):
        k = d * SSD_HEADS + h
        out = jnp.where(_head_mask(h), v[:, k:k + 1], out)
    return out


def _ssd_chunk_terms(c, xs_ref, b_ref, c_ref, dt_ref, la_ref, dsk_ref, y_ref, upd_ref, dec_ref, need_y):
    T = SSD_CHUNK
    r0 = pl.multiple_of(c * T, T)
    xs = xs_ref[pl.ds(r0, T), :]
    bm = b_ref[pl.ds(r0, T), :]
    cm = c_ref[pl.ds(r0, T), :]
    dt = dt_ref[pl.ds(r0, T), :]
    la = la_ref[pl.ds(r0, T), :]
    li = lax.broadcasted_iota(jnp.int32, (T, T), 0)
    si = lax.broadcasted_iota(jnp.int32, (T, T), 1)
    causal = si <= li
    prefix = jnp.dot(causal.astype(F32), la, precision=lax.Precision.HIGHEST, preferred_element_type=F32)
    total = prefix[T - 1:T, :]
    lane = lax.broadcasted_iota(jnp.int32, (1, LANES), 1)
    acum = jnp.where(lane < SSD_HEADS, prefix, total - prefix + la)
    bt = bm.astype(F32).T.astype(BF16)
    if need_y:
        acum_t = acum.T
        cb = [lax.dot_general(cm[:, g * SSD_STATE:(g + 1) * SSD_STATE], bm[:, g * SSD_STATE:(g + 1) * SSD_STATE],
                              (((1,), (1,)), ((), ())), preferred_element_type=F32) for g in range(SSD_GROUPS)]
        y = jnp.zeros((T, W_SSD), F32)
    for d in range(2):
        acum_e = _expand_heads(acum, d)
        tot_e = _expand_heads(total, d)
        xdt = xs * _expand_heads(dt, d)
        xw = (xdt * jnp.exp(tot_e - acum_e)).astype(BF16)
        upd = [jnp.dot(bt[g * SSD_STATE:(g + 1) * SSD_STATE, :], xw[:, g * GROUP_W:(g + 1) * GROUP_W],
                       preferred_element_type=F32) for g in range(SSD_GROUPS)]
        upd_ref[c, d] = jnp.concatenate(upd, axis=1)
        dec_ref[c, d, 0:T, :] = jnp.exp(acum_e)
        dec_ref[c, d, T:T + 1, :] = jnp.exp(tot_e)
        if need_y:
            mask = causal if d == 0 else (si >= li)
            xdt_b = xdt.astype(BF16)
            for h in range(SSD_HEADS):
                k = d * SSD_HEADS + h
                decay = jnp.exp(jnp.where(mask, acum[:, k:k + 1] - acum_t[k:k + 1, :], -jnp.inf))
                scores = (cb[h // HEADS_PER_GROUP] * decay).astype(BF16)
                y = y + jnp.where(_head_mask(h), jnp.dot(scores, xdt_b, preferred_element_type=F32), 0.0)
    if need_y:
        y_ref[pl.ds(r0, T), :] = xs * dsk_ref[...] + y


def _ssd_chunk_state(c, d, c_ref, y_ref, upd_ref, dec_ref, st_ref, need_y):
    T = SSD_CHUNK
    r0 = pl.multiple_of(c * T, T)
    st = st_ref[d]
    if need_y:
        cm = c_ref[pl.ds(r0, T), :]
        st_b = st.astype(BF16)
        ys = [jnp.dot(cm[:, g * SSD_STATE:(g + 1) * SSD_STATE], st_b[:, g * GROUP_W:(g + 1) * GROUP_W],
                      preferred_element_type=F32) for g in range(SSD_GROUPS)]
        y_ref[pl.ds(r0, T), :] += jnp.concatenate(ys, axis=1) * dec_ref[c, d, 0:T, :]
    st_ref[d] = dec_ref[c, d, T:T + 1, :] * st + upd_ref[c, d]


def _ssd_kernel(z_ref, xs_in, bs_in, cs_in, dtr_ref, cw_ref, cb_ref, dtb_ref, alog_ref, dsk_ref, ng_ref, h0_ref,
                *refs, seq, need_y):
    if need_y:
        o_ref, hT_ref, xs_ref, b_ref, c_ref, dt_ref, la_ref, upd_ref, dec_ref, st_ref, y_ref = refs
    else:
        hT_ref, xs_ref, b_ref, c_ref, dt_ref, la_ref, upd_ref, dec_ref, st_ref = refs
        y_ref = None
    nc = seq // SSD_CHUNK
    row = lax.broadcasted_iota(jnp.int32, (seq, 1), 0)
    for gi, (src, dst) in enumerate(((xs_in, xs_ref), (bs_in, b_ref), (cs_in, c_ref))):
        sl = slice(gi * W_SSD, (gi + 1) * W_SSD)
        x = src[0].astype(F32)
        cv = (_shift_rows(x, 1, row) * cw_ref[0:1, sl] + x * cw_ref[1:2, sl]
              + _shift_rows(x, -1, row) * cw_ref[2:3, sl] + cb_ref[:, sl])
        dst[...] = (cv * jax.nn.sigmoid(cv)).astype(dst.dtype)
    dtv = dtr_ref[0] + dtb_ref[...]
    dt = jnp.maximum(dtv, 0.0) + jnp.log1p(jnp.exp(-jnp.abs(dtv)))
    dt_ref[...] = dt
    la_ref[...] = dt * (-jnp.exp(alog_ref[...]))
    st_ref[...] = h0_ref[0]

    def terms(c, carry):
        _ssd_chunk_terms(c, xs_ref, b_ref, c_ref, dt_ref, la_ref, dsk_ref, y_ref, upd_ref, dec_ref, need_y)
        return carry

    lax.fori_loop(0, nc, terms, 0, unroll=2)

    def recur(i, carry):
        _ssd_chunk_state(i, 0, c_ref, y_ref, upd_ref, dec_ref, st_ref, need_y)
        _ssd_chunk_state(nc - 1 - i, 1, c_ref, y_ref, upd_ref, dec_ref, st_ref, need_y)
        return carry

    lax.fori_loop(0, nc, recur, 0, unroll=2)
    hT_ref[0] = st_ref[...]
    if need_y:
        z = z_ref[0].astype(F32)
        yz = y_ref[...] * (z * jax.nn.sigmoid(z))
        ms = jnp.mean(yz * yz, axis=-1, keepdims=True)
        o_ref[0] = (yz * lax.rsqrt(ms + EPS) * ng_ref[...]).astype(o_ref.dtype)


def ssd_scan(pr3, dt3, h0, conv_w, conv_b, dt_bias, a_log, d_skip, norm_g, need_y):
    bn, seq, _ = pr3.shape
    nc = seq // SSD_CHUNK
    pad = LANES - 2 * SSD_HEADS
    dtb = jnp.pad(dt_bias.reshape(1, -1), ((0, 0), (0, pad)))
    alog = jnp.pad(a_log.reshape(1, -1), ((0, 0), (0, pad)))
    dsk = jnp.repeat(d_skip, SSD_HEAD_DIM)[None, :]
    blk = lambda col: pl.BlockSpec((1, seq, W_SSD), lambda b: (b, 0, col))
    full = lambda a: pl.BlockSpec(a.shape, lambda b: (0,) * a.ndim)
    st_shape = (1, 2, SSD_STATE, W_SSD)
    st_spec = pl.BlockSpec(st_shape, lambda b: (b, 0, 0, 0))
    args = (conv_w, conv_b[None, :], dtb, alog, dsk, norm_g[None, :])
    out_shape = [jax.ShapeDtypeStruct((bn,) + st_shape[1:], F32)]
    out_specs = [st_spec]
    scratch = [pltpu.VMEM((seq, W_SSD), F32),
               pltpu.VMEM((seq, W_SSD), BF16), pltpu.VMEM((seq, W_SSD), BF16),
               pltpu.VMEM((seq, LANES), F32), pltpu.VMEM((seq, LANES), F32),
               pltpu.VMEM((nc, 2, SSD_STATE, W_SSD), F32),
               pltpu.VMEM((nc, 2, SSD_CHUNK + 8, W_SSD), F32),
               pltpu.VMEM(st_shape[1:], F32)]
    if need_y:
        out_shape.insert(0, jax.ShapeDtypeStruct((bn, seq, W_SSD), BF16))
        out_specs.insert(0, pl.BlockSpec((1, seq, W_SSD), lambda b: (b, 0, 0)))
        scratch.append(pltpu.VMEM((seq, W_SSD), F32))
    res = pl.pallas_call(
        functools.partial(_ssd_kernel, seq=seq, need_y=need_y),
        out_shape=out_shape,
        grid=(bn,),
        in_specs=[blk(Z_COL), blk(XS_COL), blk(BS_COL), blk(CS_COL),
                  pl.BlockSpec((1, seq, LANES), lambda b: (b, 0, 0))]
                 + [full(a) for a in args] + [st_spec],
        out_specs=out_specs,
        scratch_shapes=scratch,
        compiler_params=_params("parallel"),
        name="ssd_scan",
    )(pr3, pr3, pr3, pr3, dt3, *args, h0)
    return (res[0], res[1]) if need_y else (None, res[0])


def ssd_mix(pr3, dt3, prc3, dtc3, conv_w, conv_b, dt_bias, a_log, d_skip, norm_g, ctx_out):
    bn = pr3.shape[0]
    h0 = jnp.zeros((bn, 2, SSD_STATE, W_SSD), F32)
    oc, hc = ssd_scan(prc3, dtc3, h0, conv_w, conv_b, dt_bias, a_log, d_skip, norm_g, ctx_out)
    o, _ = ssd_scan(pr3, dt3, hc, conv_w, conv_b, dt_bias, a_log, d_skip, norm_g, True)
    return o, oc


def mixer(pr3, dt3, prc3, dtc3, pool_w, pool_scale, conv_w, rpb, s_conv_w, s_conv_b, dt_bias, a_log, d_skip,
          s_norm_g, ctx_out):
    bn, S, _ = pr3.shape
    o_ssd, oc_ssd = ssd_mix(pr3, dt3, prc3, dtc3, s_conv_w, s_conv_b, dt_bias, a_log, d_skip, s_norm_g, ctx_out)
    o = [*local_mix(pr3, pool_w, pool_scale, conv_w),
         na_attention(pr3, prc3, na_bias_table(rpb, S // GRID_W)), o_ssd]
    if not ctx_out:
        return o, None
    oc = [*local_mix(prc3, pool_w, pool_scale, conv_w), ctx_attention(prc3), oc_ssd]
    return o, oc


TM = 512
TM_PROJ = 1024
TM_GRP = 512


def kernel(x, c, ctx, c_ctx, w_ada, b_ada, g_mix, g_ffn, w_in, w_out, pool_w, pool_scale, conv_w, na_rpb,
           ssd_conv_w, ssd_conv_b, ssd_dt_bias, ssd_a_log, ssd_d, ssd_norm_g, ffn_w_gu, ffn_w_down,
           moe_router, moe_w_gu, moe_w_down, g_final):
    bn, S, d = x.shape
    Lc = ctx.shape[1]
    m, mc = bn * S, bn * Lc
    tpb = S // TM
    x2 = x.reshape(m, d)
    xc2 = ctx.reshape(mc, d)
    c_rows = jnp.concatenate([c, c_ctx[None, :], jnp.zeros((7, d), F32)], axis=0)
    for l in range(DEPTH):
        ctx_out = l < DEPTH - 1
        last = l == DEPTH - 1
        ada = ada_modulation(c_rows, w_ada[l].astype(BF16), b_ada[l][None, :])
        mod = ada[:bn].reshape(bn, 6, d)
        mod_c = ada[bn:bn + 1].reshape(1, 6, d)
        w_in_l = jnp.pad(w_in[l], ((0, 0), (0, D_IN_PAD - D_IN_PROJ))).astype(BF16)
        g_m = g_mix[l][None, :]
        g_f = g_ffn[l][None, :]
        pr, dt = in_proj(x2, g_m, mod, w_in_l, S // TM_PROJ, TM_PROJ)
        pr_c, dt_c = in_proj(xc2, g_m, mod_c, w_in_l, None, min(TM_PROJ, mc))
        o, oc = mixer(pr.reshape(bn, S, -1), dt.reshape(bn, S, -1), pr_c.reshape(bn, Lc, -1),
                      dt_c.reshape(bn, Lc, -1), pool_w[l], pool_scale[l], conv_w[l],
                      na_rpb[l], ssd_conv_w[l], ssd_conv_b[l], ssd_dt_bias[l], ssd_a_log[l], ssd_d[l],
                      ssd_norm_g[l], ctx_out)
        w_out_l = w_out[l].astype(BF16)
        o = [p.reshape(m, -1) for p in o]
        oc = [p.reshape(mc, -1) for p in oc] if ctx_out else None
        j = l // 2
        if l % 2 == 0:
            w_gu = ffn_w_gu[j].astype(BF16)
            w_dn = ffn_w_down[j].astype(BF16)
            x2 = ffn_dense(x2, o, g_f, mod, w_out_l, w_gu, w_dn, tpb, TM)
            if ctx_out:
                xc2 = ffn_dense(xc2, oc, g_f, mod_c, w_out_l, w_gu, w_dn, None, min(TM, mc))
        else:
            w_r = jnp.pad(moe_router[j], ((0, 0), (0, LANES - N_EXPERTS))).astype(BF16)
            w_gu, w_dn = moe_w_gu[j], moe_w_down[j]
            x2 = moe_block(x2, o, g_f, mod, w_out_l, w_r, w_gu, w_dn, tpb, TM, TM_GRP,
                           g_final[None, :] if last else None)
            if ctx_out:
                xc2 = moe_block(xc2, oc, g_f, mod_c, w_out_l, w_r, w_gu, w_dn, None, min(TM, mc), TM_GRP)
            if last:
                return x2.reshape(bn, S, d)
    return final_norm(x2, g_final[None, :], TM).reshape(bn, S, d)
```

```python
import functools
import math

import numpy as np
import jax
import jax.numpy as jnp
from jax import lax
from jax.experimental import pallas as pl
from jax.experimental.pallas import tpu as pltpu

DEPTH = 2
GRID_W = 64
EPS = 1e-6
W_POOL = 256
W_CONV = 256
W_NA = 256
W_SSD = 256
POOL_WINDOWS = (2, 4, 8, 16)
POOL_GROUP = W_POOL // len(POOL_WINDOWS)
NA_HEADS = 4
NA_HEAD_DIM = W_NA // NA_HEADS
WIN_R = 8
WIN_C = 16
SSD_HEAD_DIM = 64
SSD_HEADS = W_SSD // SSD_HEAD_DIM
SSD_GROUPS = 2
SSD_STATE = 128
SSD_CHUNK = 128
SSD_XBC = W_SSD + 2 * SSD_GROUPS * SSD_STATE
D_IN_PROJ = W_POOL + 3 * W_CONV + 3 * W_NA + W_SSD + SSD_XBC + 2 * SSD_HEADS
N_EXPERTS = 8
TOP_K = 2

LANES = 128
D_IN_PAD = -(-D_IN_PROJ // LANES) * LANES
VMEM_LIMIT = 56 * 1024 * 1024
BF16 = jnp.bfloat16
F32 = jnp.float32


def _params(*sem):
    return pltpu.CompilerParams(dimension_semantics=sem, vmem_limit_bytes=VMEM_LIMIT)


def _norm_mod(x, g, scale, shift):
    ms = jnp.mean(x * x, axis=-1, keepdims=True)
    return (x * lax.rsqrt(ms + EPS) * g) * (1.0 + scale) + shift


def _mod_map(tiles_per_batch):
    if tiles_per_batch is None:
        return lambda i, *_: (0, 0, 0)
    return lambda i, *_: (i // tiles_per_batch, 0, 0)


def _resident(shape, index_map):
    return pl.BlockSpec(shape, index_map, pipeline_mode=pl.Buffered(1))


def _ada_kernel(c_ref, w_ref, b_ref, o_ref):
    c = c_ref[...]
    s = c * jax.nn.sigmoid(c)
    o_ref[...] = jnp.dot(s.astype(BF16), w_ref[...], preferred_element_type=F32) + b_ref[...]


def ada_modulation(c_rows, w, b):
    r, d = c_rows.shape
    n = w.shape[1]
    tn = 1536
    return pl.pallas_call(
        _ada_kernel,
        out_shape=jax.ShapeDtypeStruct((r, n), F32),
        grid=(n // tn,),
        in_specs=[pl.BlockSpec((r, d), lambda j: (0, 0)),
                  pl.BlockSpec((d, tn), lambda j: (0, j)),
                  pl.BlockSpec((1, tn), lambda j: (0, j))],
        out_specs=pl.BlockSpec((r, tn), lambda j: (0, j)),
        compiler_params=_params("arbitrary"),
        name="ada_modulation",
    )(c_rows, w, b)


def _in_proj_kernel(x_ref, g_ref, mod_ref, w_ref, o_ref, dt_ref):
    h = _norm_mod(x_ref[...], g_ref[...], mod_ref[0, 1:2, :], mod_ref[0, 0:1, :])
    pr = jnp.dot(h.astype(BF16), w_ref[...], preferred_element_type=F32)
    o_ref[...] = pr.astype(o_ref.dtype)
    dt_ref[...] = pr[:, DT_COL * LANES:(DT_COL + 1) * LANES]


def in_proj(x2, g, mod, w, tiles_per_batch, tm):
    m, d = x2.shape
    n = w.shape[1]
    return pl.pallas_call(
        _in_proj_kernel,
        out_shape=(jax.ShapeDtypeStruct((m, n), BF16), jax.ShapeDtypeStruct((m, LANES), F32)),
        grid=(m // tm,),
        in_specs=[pl.BlockSpec((tm, d), lambda i: (i, 0)),
                  pl.BlockSpec((1, d), lambda i: (0, 0)),
                  pl.BlockSpec((1, 6, d), _mod_map(tiles_per_batch)),
                  _resident((d, n), lambda i: (0, 0))],
        out_specs=(pl.BlockSpec((tm, n), lambda i: (i, 0)), pl.BlockSpec((tm, LANES), lambda i: (i, 0))),
        compiler_params=_params("parallel"),
        name="in_proj",
    )(x2, g, mod, w)


def _mix_residual(x, mod_ref, w_ref, part_refs):
    y, k0 = None, 0
    for p_ref in part_refs:
        k = p_ref.shape[1]
        t = jnp.dot(p_ref[...], w_ref[k0:k0 + k, :], preferred_element_type=F32)
        y = t if y is None else y + t
        k0 += k
    return x + mod_ref[0, 2:3, :] * y


def _part_specs(parts, tm):
    return [pl.BlockSpec((tm, p.shape[1]), lambda i, *_: (i, 0)) for p in parts]


FF_CHUNK = 512


def _swiglu(h, wgu, wd, ff):
    acc = None
    for c0 in range(0, ff, FF_CHUNK):
        c1 = min(c0 + FF_CHUNK, ff)
        gg = jnp.dot(h, wgu(c0, c1), preferred_element_type=F32)
        uu = jnp.dot(h, wgu(ff + c0, ff + c1), preferred_element_type=F32)
        a = (gg * jax.nn.sigmoid(gg) * uu).astype(BF16)
        part = jnp.dot(a, wd(c0, c1), preferred_element_type=F32)
        acc = part if acc is None else acc + part
    return acc


def _ffn_kernel(x_ref, g_ref, mod_ref, wout_ref, wgu_ref, wd_ref, *refs):
    x = _mix_residual(x_ref[...], mod_ref, wout_ref, refs[:-1])
    y_ref = refs[-1]
    h = _norm_mod(x, g_ref[...], mod_ref[0, 4:5, :], mod_ref[0, 3:4, :]).astype(BF16)
    y = _swiglu(h, lambda a, b: wgu_ref[:, a:b], lambda a, b: wd_ref[a:b, :], wd_ref.shape[0])
    y_ref[...] = x + mod_ref[0, 5:6, :] * y


def ffn_dense(x2, parts, g, mod, w_out, w_gu, w_down, tiles_per_batch, tm):
    m, d = x2.shape
    return pl.pallas_call(
        _ffn_kernel,
        out_shape=jax.ShapeDtypeStruct((m, d), F32),
        grid=(m // tm,),
        in_specs=[pl.BlockSpec((tm, d), lambda i: (i, 0)),
                  pl.BlockSpec((1, d), lambda i: (0, 0)),
                  pl.BlockSpec((1, 6, d), _mod_map(tiles_per_batch)),
                  _resident(w_out.shape, lambda i: (0, 0)),
                  _resident(w_gu.shape, lambda i: (0, 0)),
                  _resident(w_down.shape, lambda i: (0, 0))] + _part_specs(parts, tm),
        out_specs=pl.BlockSpec((tm, d), lambda i: (i, 0)),
        compiler_params=_params("parallel"),
        name="ffn_dense",
    )(x2, g, mod, w_out, w_gu, w_down, *parts)


ROUTE_E1, ROUTE_E2, ROUTE_R1, ROUTE_R2, ROUTE_G1, ROUTE_G2 = range(6)
ROUTE_ROWS = 8


def _router_kernel(x_ref, g_ref, mod_ref, wout_ref, wr_ref, *refs):
    part_refs = refs[:-6]
    x1_ref, h_ref, route_ref, route_t_ref, cnt_ref, base_ref = refs[-6:]

    @pl.when(pl.program_id(0) == 0)
    def _():
        base_ref[...] = jnp.zeros_like(base_ref)

    x = _mix_residual(x_ref[...], mod_ref, wout_ref, part_refs)
    x1_ref[...] = x
    h = _norm_mod(x, g_ref[...], mod_ref[0, 4:5, :], mod_ref[0, 3:4, :]).astype(BF16)
    h_ref[...] = h
    tm = h.shape[0]
    logits = jnp.dot(h, wr_ref[...], preferred_element_type=F32)
    lane = lax.broadcasted_iota(jnp.int32, logits.shape, 1)
    neg = jnp.float32(-jnp.inf)
    logits = jnp.where(lane < N_EXPERTS, logits, neg)
    m1 = jnp.max(logits, axis=-1, keepdims=True)
    i1 = jnp.min(jnp.where(logits == m1, lane, LANES), axis=-1, keepdims=True)
    rest = jnp.where(lane == i1, neg, logits)
    m2 = jnp.max(rest, axis=-1, keepdims=True)
    i2 = jnp.min(jnp.where(rest == m2, lane, LANES), axis=-1, keepdims=True)
    e2 = jnp.exp(m2 - m1)
    g1 = 1.0 / (1.0 + e2)
    g2 = e2 / (1.0 + e2)
    sel1, sel2 = lane == i1, lane == i2
    sel = jnp.where(sel1 | sel2, 1.0, 0.0)
    li = lax.broadcasted_iota(jnp.int32, (tm, tm), 0)
    si = lax.broadcasted_iota(jnp.int32, (tm, tm), 1)
    before = jnp.where(si < li, 1.0, 0.0).astype(BF16)
    rank = jnp.dot(before, sel.astype(BF16), preferred_element_type=F32) + base_ref[...]
    r1 = jnp.sum(jnp.where(sel1, rank, 0.0), axis=-1, keepdims=True)
    r2 = jnp.sum(jnp.where(sel2, rank, 0.0), axis=-1, keepdims=True)
    base_ref[...] += jnp.sum(sel, axis=0, keepdims=True)
    cnt_ref[...] = base_ref[...]
    fields = (i1.astype(F32), i2.astype(F32), r1, r2, g1, g2)
    route = jnp.zeros(logits.shape, F32)
    for k, v in enumerate(fields):
        route = jnp.where(lane == k, v, route)
    route_ref[...] = route
    route_t_ref[...] = route.T[:ROUTE_ROWS, :]


def moe_router(x2, parts, g, mod, w_out, w_router, tiles_per_batch, tm):
    m, d = x2.shape
    return pl.pallas_call(
        _router_kernel,
        out_shape=(jax.ShapeDtypeStruct((m, d), F32), jax.ShapeDtypeStruct((m, d), BF16),
                   jax.ShapeDtypeStruct((m, LANES), F32), jax.ShapeDtypeStruct((ROUTE_ROWS, m), F32),
                   jax.ShapeDtypeStruct((1, LANES), F32)),
        grid=(m // tm,),
        in_specs=[pl.BlockSpec((tm, d), lambda i: (i, 0)),
                  pl.BlockSpec((1, d), lambda i: (0, 0)),
                  pl.BlockSpec((1, 6, d), _mod_map(tiles_per_batch)),
                  _resident(w_out.shape, lambda i: (0, 0)),
                  pl.BlockSpec((d, LANES), lambda i: (0, 0))] + _part_specs(parts, tm),
        out_specs=(pl.BlockSpec((tm, d), lambda i: (i, 0)),
                   pl.BlockSpec((tm, d), lambda i: (i, 0)),
                   pl.BlockSpec((tm, LANES), lambda i: (i, 0)),
                   pl.BlockSpec((ROUTE_ROWS, tm), lambda i: (0, i)),
                   pl.BlockSpec((1, LANES), lambda i: (0, 0))),
        scratch_shapes=[pltpu.VMEM((1, LANES), F32)],
        compiler_params=_params("arbitrary"),
        name="moe_router",
    )(x2, g, mod, w_out, w_router, *parts)


def _moe_kernel(tile_ref, exp_ref, start_ref, end_ref, xg_ref, wgu_ref, wd_ref, y_ref, wgu_b, wd_b):
    w = pl.program_id(0)
    tm = xg_ref.shape[0]
    start, end = start_ref[w], end_ref[w]
    t0 = tile_ref[w] * tm

    @pl.when((w == 0) | (exp_ref[w] != exp_ref[jnp.maximum(w - 1, 0)]))
    def _():
        wgu_b[...] = wgu_ref[0].astype(BF16)
        wd_b[...] = wd_ref[0].astype(BF16)

    @pl.when(end > start)
    def _():
        y = _swiglu(xg_ref[...], lambda a, b: wgu_b[:, a:b], lambda a, b: wd_b[a:b, :], wd_b.shape[0])
        y = y.astype(y_ref.dtype)

        @pl.when(start == t0)
        def _():
            y_ref[...] = y

        @pl.when(start != t0)
        def _():
            row = t0 + lax.broadcasted_iota(jnp.int32, (tm, 1), 0)
            y_ref[...] = jnp.where(row >= start, y, y_ref[...])


def moe_grouped(item_tile, item_expert, item_start, item_end, xg, w_gu, w_down, tm):
    p, d = xg.shape
    grid_spec = pltpu.PrefetchScalarGridSpec(
        num_scalar_prefetch=4,
        grid=(item_tile.shape[0],),
        in_specs=[pl.BlockSpec((tm, d), lambda w, it, ie, s, e: (it[w], 0)),
                  _resident((1,) + w_gu.shape[1:], lambda w, it, ie, s, e: (ie[w], 0, 0)),
                  _resident((1,) + w_down.shape[1:], lambda w, it, ie, s, e: (ie[w], 0, 0))],
        out_specs=pl.BlockSpec((tm, d), lambda w, it, ie, s, e: (it[w], 0)),
        scratch_shapes=[pltpu.VMEM(w_gu.shape[1:], BF16), pltpu.VMEM(w_down.shape[1:], BF16)],
    )
    return pl.pallas_call(
        _moe_kernel,
        out_shape=jax.ShapeDtypeStruct((p, d), BF16),
        grid_spec=grid_spec,
        compiler_params=_params("arbitrary"),
        name="moe_grouped",
    )(item_tile, item_expert, item_start, item_end, xg, w_gu, w_down)


def _moe_combine_kernel(x_ref, ya_ref, yb_ref, route_ref, mod_ref, *refs):
    o_ref = refs[-1]
    r = route_ref[...]
    y = (r[:, ROUTE_G1:ROUTE_G1 + 1] * ya_ref[...].astype(F32) + r[:, ROUTE_G2:ROUTE_G2 + 1] * yb_ref[...].astype(F32))
    x = x_ref[...] + mod_ref[0, 5:6, :] * y
    if len(refs) == 2:
        ms = jnp.mean(x * x, axis=-1, keepdims=True)
        x = x * lax.rsqrt(ms + EPS) * refs[0][...]
    o_ref[...] = x


def moe_combine(x2, ya, yb, route, mod, tiles_per_batch, tm, g_final=None):
    m, d = x2.shape
    row = pl.BlockSpec((tm, d), lambda i: (i, 0))
    specs = [row, row, row, pl.BlockSpec((tm, LANES), lambda i: (i, 0)),
             pl.BlockSpec((1, 6, d), _mod_map(tiles_per_batch))]
    args = [x2, ya, yb, route, mod]
    if g_final is not None:
        specs.append(pl.BlockSpec((1, d), lambda i: (0, 0)))
        args.append(g_final)
    return pl.pallas_call(
        _moe_combine_kernel,
        out_shape=jax.ShapeDtypeStruct((m, d), F32),
        grid=(m // tm,),
        in_specs=specs,
        out_specs=row,
        compiler_params=_params("parallel"),
        name="moe_combine",
    )(*args)


def moe_block(x2, parts, g, mod, w_out, w_router, w_gu, w_down, tiles_per_batch, tm_tok, tm_grp, g_final=None):
    m, d = x2.shape
    x2, h, route, route_t, cnt = moe_router(x2, parts, g, mod, w_out, w_router, tiles_per_batch, tm_tok)
    cnt = cnt[0, :N_EXPERTS].astype(jnp.int32)
    p = m * TOP_K
    off = jnp.cumsum(cnt) - cnt
    field = lambda k: route_t[k].astype(jnp.int32)

    def row_of(e, r):
        for k in range(N_EXPERTS):
            r = r + jnp.where(e == k, off[k], 0)
        return r

    d1 = row_of(field(ROUTE_E1), field(ROUTE_R1))
    d2 = row_of(field(ROUTE_E2), field(ROUTE_R2))
    tok = jnp.arange(m, dtype=jnp.int32)
    _, src = lax.sort_key_val(jnp.concatenate([d1, d2]), jnp.concatenate([tok, tok]))
    n_row_tiles = p // tm_grp
    start = jnp.sort(jnp.concatenate([jnp.arange(n_row_tiles, dtype=jnp.int32) * tm_grp, off]))
    end = jnp.concatenate([start[1:], jnp.full((1,), p, jnp.int32)])
    item_tile = jnp.minimum(start // tm_grp, n_row_tiles - 1)
    item_expert = jnp.sum((off[None, :] <= start[:, None]).astype(jnp.int32), axis=1) - 1
    rows = lambda a, idx: a.at[idx].get(mode="promise_in_bounds")
    yg = moe_grouped(item_tile, item_expert, start, end, rows(h, src), w_gu, w_down, tm_grp)
    return moe_combine(x2, rows(yg, d1), rows(yg, d2), route, mod, tiles_per_batch, tm_tok, g_final)


def _final_norm_kernel(x_ref, g_ref, o_ref):
    x = x_ref[...]
    ms = jnp.mean(x * x, axis=-1, keepdims=True)
    o_ref[...] = x * lax.rsqrt(ms + EPS) * g_ref[...]


def final_norm(x2, g, tm):
    m, d = x2.shape
    return pl.pallas_call(
        _final_norm_kernel,
        out_shape=jax.ShapeDtypeStruct((m, d), F32),
        grid=(m // tm,),
        in_specs=[pl.BlockSpec((tm, d), lambda i: (i, 0)), pl.BlockSpec((1, d), lambda i: (0, 0))],
        out_specs=pl.BlockSpec((tm, d), lambda i: (i, 0)),
        compiler_params=_params("parallel"),
        name="final_norm",
    )(x2, g)


NA_QROWS = 4
NA_KROWS = NA_QROWS + WIN_R
Q_COL, K_COL, V_COL = 4, 5, 6


def _na_key_start(j, rows):
    return np.clip(j * NA_QROWS - WIN_R // 2, 0, rows - NA_KROWS)


def _na_bias_index(rows):
    nblk = rows // NA_QROWS
    pats = []
    for j in range(nblk):
        start = _na_key_start(j, rows)
        r = j * NA_QROWS + np.arange(NA_QROWS)
        sr = np.clip(r - WIN_R // 2, 0, rows - WIN_R)
        kr = start + np.arange(NA_KROWS)
        rvalid = (kr[None, :] >= sr[:, None]) & (kr[None, :] < sr[:, None] + WIN_R)
        ri = np.clip(kr[None, :] - r[:, None] + WIN_R - 1, 0, 2 * WIN_R - 2)
        pats.append((ri, rvalid))
    for j in range(2, nblk - 1):
        assert all(np.array_equal(a, b) for a, b in zip(pats[1], pats[j]))
    sel = [pats[0], pats[1], pats[nblk - 1]]
    ri = np.stack([p[0] for p in sel])
    rvalid = np.stack([p[1] for p in sel])
    c = np.arange(GRID_W)
    sc = np.clip(c - WIN_C // 2, 0, GRID_W - WIN_C)
    cvalid = (c[None, :] >= sc[:, None]) & (c[None, :] < sc[:, None] + WIN_C)
    ci = np.clip(c[None, :] - c[:, None] + WIN_C - 1, 0, 2 * WIN_C - 2)
    c_onehot = (ci[..., None] == np.arange(2 * WIN_C - 1)).astype(np.float32)
    valid = rvalid[:, :, None, :, None] & cvalid[None, None, :, None, :]
    return ri, c_onehot, valid


def na_bias_table(rpb, rows):
    ri, c_onehot, valid = _na_bias_index(rows)
    toep = jnp.einsum('hrd,qkd->hrqk', rpb, c_onehot, precision=lax.Precision.HIGHEST)
    b = jnp.take(toep, ri.reshape(-1), axis=1).reshape((NA_HEADS,) + ri.shape + (GRID_W, GRID_W))
    b = jnp.transpose(b, (1, 0, 2, 4, 3, 5))
    b = jnp.where(valid[:, None], b, -jnp.inf)
    return b.reshape(3, NA_HEADS, NA_QROWS * GRID_W, NA_KROWS * GRID_W).astype(F32)


def _attend_heads(q, key_sets, bias_fn):
    n, w = q.shape
    lane = lax.broadcasted_iota(jnp.int32, (1, w), 1)
    out = jnp.zeros((n, w), F32)
    for h in range(NA_HEADS):
        mh = (lane >= h * NA_HEAD_DIM) & (lane < (h + 1) * NA_HEAD_DIM)
        qh = jnp.where(mh, q, 0.0).astype(BF16)
        scores = []
        for i, (k, _) in enumerate(key_sets):
            s = lax.dot_general(qh, k, (((1,), (1,)), ((), ())), preferred_element_type=F32)
            b = bias_fn(i, h)
            scores.append(s if b is None else s + b)
        m = scores[0].max(axis=-1, keepdims=True)
        for s in scores[1:]:
            m = jnp.maximum(m, s.max(axis=-1, keepdims=True))
        denom = jnp.zeros((n, 1), F32)
        acc = jnp.zeros((n, w), F32)
        for s, (_, v) in zip(scores, key_sets):
            p = jnp.exp(s - m)
            denom = denom + p.sum(axis=-1, keepdims=True)
            acc = acc + jnp.dot(p.astype(BF16), v, preferred_element_type=F32)
        out = out + jnp.where(mh, acc / denom, 0.0)
    return out


def _na_kernel(q_ref, k_ref, v_ref, kc_ref, vc_ref, bias_ref, o_ref, *, rows):
    j = pl.program_id(1)
    start = jnp.clip(j * NA_QROWS - WIN_R // 2, 0, rows - NA_KROWS)
    t0 = pl.multiple_of(start * GRID_W, GRID_W)
    nk = NA_KROWS * GRID_W
    kw = k_ref[0, pl.ds(t0, nk), :]
    vw = v_ref[0, pl.ds(t0, nk), :]
    q = q_ref[0] * (1.0 / math.sqrt(NA_HEAD_DIM))
    o = _attend_heads(q, [(kw, vw), (kc_ref[0], vc_ref[0])], lambda i, h: bias_ref[0, h] if i == 0 else None)
    o_ref[0] = o.astype(o_ref.dtype)


def na_attention(pr3, prc3, bias):
    bn, S, _ = pr3.shape
    Lc = prc3.shape[1]
    rows = S // GRID_W
    nblk = rows // NA_QROWS
    nq = NA_QROWS * GRID_W

    def pat(b, j):
        return (jnp.where(j == 0, 0, jnp.where(j == nblk - 1, 2, 1)), 0, 0, 0)

    return pl.pallas_call(
        functools.partial(_na_kernel, rows=rows),
        out_shape=jax.ShapeDtypeStruct((bn, S, W_NA), BF16),
        grid=(bn, nblk),
        in_specs=[pl.BlockSpec((1, nq, W_NA), lambda b, j: (b, j, Q_COL)),
                  pl.BlockSpec((1, S, W_NA), lambda b, j: (b, 0, K_COL)),
                  pl.BlockSpec((1, S, W_NA), lambda b, j: (b, 0, V_COL)),
                  pl.BlockSpec((1, Lc, W_NA), lambda b, j: (b, 0, K_COL)),
                  pl.BlockSpec((1, Lc, W_NA), lambda b, j: (b, 0, V_COL)),
                  pl.BlockSpec((1,) + bias.shape[1:], pat)],
        out_specs=pl.BlockSpec((1, nq, W_NA), lambda b, j: (b, j, 0)),
        compiler_params=_params("parallel", "arbitrary"),
        name="na_attention",
    )(pr3, pr3, pr3, prc3, prc3, bias)


def _ctx_attn_kernel(q_ref, k_ref, v_ref, o_ref):
    q = q_ref[0] * (1.0 / math.sqrt(NA_HEAD_DIM))
    o = _attend_heads(q, [(k_ref[0], v_ref[0])], lambda i, h: None)
    o_ref[0] = o.astype(o_ref.dtype)


def ctx_attention(prc3):
    bn, Lc, _ = prc3.shape
    return pl.pallas_call(
        _ctx_attn_kernel,
        out_shape=jax.ShapeDtypeStruct((bn, Lc, W_NA), BF16),
        grid=(bn,),
        in_specs=[pl.BlockSpec((1, Lc, W_NA), lambda b: (b, 0, Q_COL)),
                  pl.BlockSpec((1, Lc, W_NA), lambda b: (b, 0, K_COL)),
                  pl.BlockSpec((1, Lc, W_NA), lambda b: (b, 0, V_COL))],
        out_specs=pl.BlockSpec((1, Lc, W_NA), lambda b: (b, 0, 0)),
        compiler_params=_params("parallel"),
        name="ctx_attention",
    )(prc3, prc3, prc3)


POOL_COL, CONV_H_COL, CONV_B_COL, CONV_C_COL = 0, 1, 2, 3


def _shift_rows(x, k, row):
    n = x.shape[0]
    y = pltpu.roll(x, k % n, 0)
    return jnp.where(row < k, 0.0, y) if k > 0 else jnp.where(row >= n + k, 0.0, y)


def _local_mix_kernel(u_ref, h_ref, bg_ref, cg_ref, pw_ref, ps_ref, cw_ref, op_ref, oc_ref, *, seq):
    row = lax.broadcasted_iota(jnp.int32, (seq, 1), 0)
    lane = lax.broadcasted_iota(jnp.int32, (1, W_POOL), 1)
    u = u_ref[0].astype(F32)
    trailing, leading = {1: u}, {1: u}
    for w in (1, 2, 4):
        trailing[2 * w] = trailing[w] + _shift_rows(trailing[w], w, row)
        leading[2 * w] = leading[w] + _shift_rows(leading[w], -w, row)
    rowf = row.astype(F32)
    win_sum = jnp.zeros_like(u)
    cnt = jnp.zeros_like(u)
    for g, win in enumerate(POOL_WINDOWS):
        half = win // 2
        mg = (lane >= g * POOL_GROUP) & (lane < (g + 1) * POOL_GROUP)
        s = _shift_rows(trailing[half], 1, row) + leading[half]
        n = jnp.minimum(rowf + half, float(seq)) - jnp.maximum(rowf - half, 0.0)
        win_sum = jnp.where(mg, s, win_sum)
        cnt = jnp.where(mg, n, cnt)
    p = win_sum / cnt - u
    pooled = jnp.dot(p.astype(BF16), pw_ref[...], preferred_element_type=F32) * ps_ref[...]
    op_ref[0] = pooled.astype(op_ref.dtype)
    v = cg_ref[0].astype(F32) * h_ref[0].astype(F32)
    conv = (_shift_rows(v, 1, row) * cw_ref[0:1, :] + v * cw_ref[1:2, :] + _shift_rows(v, -1, row) * cw_ref[2:3, :])
    oc_ref[0] = (bg_ref[0].astype(F32) * conv).astype(oc_ref.dtype)


def local_mix(pr3, pool_w, pool_scale, conv_w):
    bn, seq, _ = pr3.shape
    pw = jax.scipy.linalg.block_diag(*[pool_w[g] for g in range(len(POOL_WINDOWS))]).astype(BF16)
    blk = lambda col: pl.BlockSpec((1, seq, W_POOL), lambda b: (b, 0, col))
    full = lambda a: pl.BlockSpec(a.shape, lambda b: (0,) * a.ndim)
    args = (pw, pool_scale[None, :], conv_w)
    return pl.pallas_call(
        functools.partial(_local_mix_kernel, seq=seq),
        out_shape=(jax.ShapeDtypeStruct((bn, seq, W_POOL), BF16), jax.ShapeDtypeStruct((bn, seq, W_CONV), BF16)),
        grid=(bn,),
        in_specs=[blk(POOL_COL), blk(CONV_H_COL), blk(CONV_B_COL), blk(CONV_C_COL)] + [full(a) for a in args],
        out_specs=(pl.BlockSpec((1, seq, W_POOL), lambda b: (b, 0, 0)),
                   pl.BlockSpec((1, seq, W_CONV), lambda b: (b, 0, 0))),
        compiler_params=_params("parallel"),
        name="local_mix",
    )(pr3, pr3, pr3, pr3, *args)


Z_COL, XS_COL, BS_COL, CS_COL = 7, 8, 9, 10
DT_COL = 22
HEADS_PER_GROUP = SSD_HEADS // SSD_GROUPS
GROUP_W = HEADS_PER_GROUP * SSD_HEAD_DIM


def _head_mask(h):
    lane = lax.broadcasted_iota(jnp.int32, (1, W_SSD), 1)
    return (lane >= h * SSD_HEAD_DIM) & (lane < (h + 1) * SSD_HEAD_DIM)


def _expand_heads(v, d):
    out = jnp.zeros((v.shape[0], W_SSD), F32)
    for h in range(SSD_HEADS):
        k = d * SSD_HEADS + h
        out = jnp.where(_head_mask(h), v[:, k:k + 1], out)
    return out


def _ssd_chunk_terms(c, xs_ref, b_ref, c_ref, dt_ref, la_ref, dsk_ref, y_ref, upd_ref, dec_ref, need_y):
    T = SSD_CHUNK
    r0 = pl.multiple_of(c * T, T)
    xs = xs_ref[pl.ds(r0, T), :]
    bm = b_ref[pl.ds(r0, T), :]
    cm = c_ref[pl.ds(r0, T), :]
    dt = dt_ref[pl.ds(r0, T), :]
    la = la_ref[pl.ds(r0, T), :]
    li = lax.broadcasted_iota(jnp.int32, (T, T), 0)
    si = lax.broadcasted_iota(jnp.int32, (T, T), 1)
    causal = si <= li
    prefix = jnp.dot(causal.astype(F32), la, precision=lax.Precision.HIGHEST, preferred_element_type=F32)
    total = prefix[T - 1:T, :]
    lane = lax.broadcasted_iota(jnp.int32, (1, LANES), 1)
    acum = jnp.where(lane < SSD_HEADS, prefix, total - prefix + la)
    bt = bm.astype(F32).T.astype(BF16)
    if need_y:
        acum_t = acum.T
        cb = [lax.dot_general(cm[:, g * SSD_STATE:(g + 1) * SSD_STATE], bm[:, g * SSD_STATE:(g + 1) * SSD_STATE],
                              (((1,), (1,)), ((), ())), preferred_element_type=F32) for g in range(SSD_GROUPS)]
        y = jnp.zeros((T, W_SSD), F32)
    for d in range(2):
        acum_e = _expand_heads(acum, d)
        tot_e = _expand_heads(total, d)
        xdt = xs * _expand_heads(dt, d)
        xw = (xdt * jnp.exp(tot_e - acum_e)).astype(BF16)
        upd = [jnp.dot(bt[g * SSD_STATE:(g + 1) * SSD_STATE, :], xw[:, g * GROUP_W:(g + 1) * GROUP_W],
                       preferred_element_type=F32) for g in range(SSD_GROUPS)]
        upd_ref[c, d] = jnp.concatenate(upd, axis=1)
        dec_ref[c, d, 0:T, :] = jnp.exp(acum_e)
        dec_ref[c, d, T:T + 1, :] = jnp.exp(tot_e)
        if need_y:
            mask = causal if d == 0 else (si >= li)
            xdt_b = xdt.astype(BF16)
            for h in range(SSD_HEADS):
                k = d * SSD_HEADS + h
                decay = jnp.exp(jnp.where(mask, acum[:, k:k + 1] - acum_t[k:k + 1, :], -jnp.inf))
                scores = (cb[h // HEADS_PER_GROUP] * decay).astype(BF16)
                y = y + jnp.where(_head_mask(h), jnp.dot(scores, xdt_b, preferred_element_type=F32), 0.0)
    if need_y:
        y_ref[pl.ds(r0, T), :] = xs * dsk_ref[...] + y


def _ssd_chunk_state(c, d, c_ref, y_ref, upd_ref, dec_ref, st_ref, need_y):
    T = SSD_CHUNK
    r0 = pl.multiple_of(c * T, T)
    st = st_ref[d]
    if need_y:
        cm = c_ref[pl.ds(r0, T), :]
        st_b = st.astype(BF16)
        ys = [jnp.dot(cm[:, g * SSD_STATE:(g + 1) * SSD_STATE], st_b[:, g * GROUP_W:(g + 1) * GROUP_W],
                      preferred_element_type=F32) for g in range(SSD_GROUPS)]
        y_ref[pl.ds(r0, T), :] += jnp.concatenate(ys, axis=1) * dec_ref[c, d, 0:T, :]
    st_ref[d] = dec_ref[c, d, T:T + 1, :] * st + upd_ref[c, d]


def _ssd_kernel(z_ref, xs_in, bs_in, cs_in, dtr_ref, cw_ref, cb_ref, dtb_ref, alog_ref, dsk_ref, ng_ref, h0_ref,
                *refs, seq, need_y):
    if need_y:
        o_ref, hT_ref, xs_ref, b_ref, c_ref, dt_ref, la_ref, upd_ref, dec_ref, st_ref, y_ref = refs
    else:
        hT_ref, xs_ref, b_ref, c_ref, dt_ref, la_ref, upd_ref, dec_ref, st_ref = refs
        y_ref = None
    nc = seq // SSD_CHUNK
    row = lax.broadcasted_iota(jnp.int32, (seq, 1), 0)
    for gi, (src, dst) in enumerate(((xs_in, xs_ref), (bs_in, b_ref), (cs_in, c_ref))):
        sl = slice(gi * W_SSD, (gi + 1) * W_SSD)
        x = src[0].astype(F32)
        cv = (_shift_rows(x, 1, row) * cw_ref[0:1, sl] + x * cw_ref[1:2, sl]
              + _shift_rows(x, -1, row) * cw_ref[2:3, sl] + cb_ref[:, sl])
        dst[...] = (cv * jax.nn.sigmoid(cv)).astype(dst.dtype)
    dtv = dtr_ref[0] + dtb_ref[...]
    dt = jnp.maximum(dtv, 0.0) + jnp.log1p(jnp.exp(-jnp.abs(dtv)))
    dt_ref[...] = dt
    la_ref[...] = dt * (-jnp.exp(alog_ref[...]))
    st_ref[...] = h0_ref[0]

    def terms(c, carry):
        _ssd_chunk_terms(c, xs_ref, b_ref, c_ref, dt_ref, la_ref, dsk_ref, y_ref, upd_ref, dec_ref, need_y)
        return carry

    lax.fori_loop(0, nc, terms, 0, unroll=2)

    def recur(i, carry):
        _ssd_chunk_state(i, 0, c_ref, y_ref, upd_ref, dec_ref, st_ref, need_y)
        _ssd_chunk_state(nc - 1 - i, 1, c_ref, y_ref, upd_ref, dec_ref, st_ref, need_y)
        return carry

    lax.fori_loop(0, nc, recur, 0, unroll=2)
    hT_ref[0] = st_ref[...]
    if need_y:
        z = z_ref[0].astype(F32)
        yz = y_ref[...] * (z * jax.nn.sigmoid(z))
        ms = jnp.mean(yz * yz, axis=-1, keepdims=True)
        o_ref[0] = (yz * lax.rsqrt(ms + EPS) * ng_ref[...]).astype(o_ref.dtype)


def ssd_scan(pr3, dt3, h0, conv_w, conv_b, dt_bias, a_log, d_skip, norm_g, need_y):
    bn, seq, _ = pr3.shape
    nc = seq // SSD_CHUNK
    pad = LANES - 2 * SSD_HEADS
    dtb = jnp.pad(dt_bias.reshape(1, -1), ((0, 0), (0, pad)))
    alog = jnp.pad(a_log.reshape(1, -1), ((0, 0), (0, pad)))
    dsk = jnp.repeat(d_skip, SSD_HEAD_DIM)[None, :]
    blk = lambda col: pl.BlockSpec((1, seq, W_SSD), lambda b: (b, 0, col))
    full = lambda a: pl.BlockSpec(a.shape, lambda b: (0,) * a.ndim)
    st_shape = (1, 2, SSD_STATE, W_SSD)
    st_spec = pl.BlockSpec(st_shape, lambda b: (b, 0, 0, 0))
    args = (conv_w, conv_b[None, :], dtb, alog, dsk, norm_g[None, :])
    out_shape = [jax.ShapeDtypeStruct((bn,) + st_shape[1:], F32)]
    out_specs = [st_spec]
    scratch = [pltpu.VMEM((seq, W_SSD), F32),
               pltpu.VMEM((seq, W_SSD), BF16), pltpu.VMEM((seq, W_SSD), BF16),
               pltpu.VMEM((seq, LANES), F32), pltpu.VMEM((seq, LANES), F32),
               pltpu.VMEM((nc, 2, SSD_STATE, W_SSD), F32),
               pltpu.VMEM((nc, 2, SSD_CHUNK + 8, W_SSD), F32),
               pltpu.VMEM(st_shape[1:], F32)]
    if need_y:
        out_shape.insert(0, jax.ShapeDtypeStruct((bn, seq, W_SSD), BF16))
        out_specs.insert(0, pl.BlockSpec((1, seq, W_SSD), lambda b: (b, 0, 0)))
        scratch.append(pltpu.VMEM((seq, W_SSD), F32))
    res = pl.pallas_call(
        functools.partial(_ssd_kernel, seq=seq, need_y=need_y),
        out_shape=out_shape,
        grid=(bn,),
        in_specs=[blk(Z_COL), blk(XS_COL), blk(BS_COL), blk(CS_COL),
                  pl.BlockSpec((1, seq, LANES), lambda b: (b, 0, 0))]
                 + [full(a) for a in args] + [st_spec],
        out_specs=out_specs,
        scratch_shapes=scratch,
        compiler_params=_params("parallel"),
        name="ssd_scan",
    )(pr3, pr3, pr3, pr3, dt3, *args, h0)
    return (res[0], res[1]) if need_y else (None, res[0])


def ssd_mix(pr3, dt3, prc3, dtc3, conv_w, conv_b, dt_bias, a_log, d_skip, norm_g, ctx_out):
    bn = pr3.shape[0]
    h0 = jnp.zeros((bn, 2, SSD_STATE, W_SSD), F32)
    oc, hc = ssd_scan(prc3, dtc3, h0, conv_w, conv_b, dt_bias, a_log, d_skip, norm_g, ctx_out)
    o, _ = ssd_scan(pr3, dt3, hc, conv_w, conv_b, dt_bias, a_log, d_skip, norm_g, True)
    return o, oc


def mixer(pr3, dt3, prc3, dtc3, pool_w, pool_scale, conv_w, rpb, s_conv_w, s_conv_b, dt_bias, a_log, d_skip,
          s_norm_g, ctx_out):
    bn, S, _ = pr3.shape
    o_ssd, oc_ssd = ssd_mix(pr3, dt3, prc3, dtc3, s_conv_w, s_conv_b, dt_bias, a_log, d_skip, s_norm_g, ctx_out)
    o = [*local_mix(pr3, pool_w, pool_scale, conv_w),
         na_attention(pr3, prc3, na_bias_table(rpb, S // GRID_W)), o_ssd]
    if not ctx_out:
        return o, None
    oc = [*local_mix(prc3, pool_w, pool_scale, conv_w), ctx_attention(prc3), oc_ssd]
    return o, oc


TM = 512
TM_PROJ = 1024
TM_GRP = 512


def kernel(x, c, ctx, c_ctx, w_ada, b_ada, g_mix, g_ffn, w_in, w_out, pool_w, pool_scale, conv_w, na_rpb,
           ssd_conv_w, ssd_conv_b, ssd_dt_bias, ssd_a_log, ssd_d, ssd_norm_g, ffn_w_gu, ffn_w_down,
           moe_router, moe_w_gu, moe_w_down, g_final):
    bn, S, d = x.shape
    Lc = ctx.shape[1]
    m, mc = bn * S, bn * Lc
    tpb = S // TM
    x2 = x.reshape(m, d)
    xc2 = ctx.reshape(mc, d)
    c_rows = jnp.concatenate([c, c_ctx[None, :], jnp.zeros((7, d), F32)], axis=0)
    for l in range(DEPTH):
        ctx_out = l < DEPTH - 1
        last = l == DEPTH - 1
        ada = ada_modulation(c_rows, w_ada[l].astype(BF16), b_ada[l][None, :])
        mod = ada[:bn].reshape(bn, 6, d)
        mod_c = ada[bn:bn + 1].reshape(1, 6, d)
        w_in_l = jnp.pad(w_in[l], ((0, 0), (0, D_IN_PAD - D_IN_PROJ))).astype(BF16)
        g_m = g_mix[l][None, :]
        g_f = g_ffn[l][None, :]
        pr, dt = in_proj(x2, g_m, mod, w_in_l, S // TM_PROJ, TM_PROJ)
        pr_c, dt_c = in_proj(xc2, g_m, mod_c, w_in_l, None, min(TM_PROJ, mc))
        o, oc = mixer(pr.reshape(bn, S, -1), dt.reshape(bn, S, -1), pr_c.reshape(bn, Lc, -1),
                      dt_c.reshape(bn, Lc, -1), pool_w[l], pool_scale[l], conv_w[l],
                      na_rpb[l], ssd_conv_w[l], ssd_conv_b[l], ssd_dt_bias[l], ssd_a_log[l], ssd_d[l],
                      ssd_norm_g[l], ctx_out)
        w_out_l = w_out[l].astype(BF16)
        o = [p.reshape(m, -1) for p in o]
        oc = [p.reshape(mc, -1) for p in oc] if ctx_out else None
        j = l // 2
        if l % 2 == 0:
            w_gu = ffn_w_gu[j].astype(BF16)
            w_dn = ffn_w_down[j].astype(BF16)
            x2 = ffn_dense(x2, o, g_f, mod, w_out_l, w_gu, w_dn, tpb, TM)
            if ctx_out:
                xc2 = ffn_dense(xc2, oc, g_f, mod_c, w_out_l, w_gu, w_dn, None, min(TM, mc))
        else:
            w_r = jnp.pad(moe_router[j], ((0, 0), (0, LANES - N_EXPERTS))).astype(BF16)
            w_gu, w_dn = moe_w_gu[j], moe_w_down[j]
            x2 = moe_block(x2, o, g_f, mod, w_out_l, w_r, w_gu, w_dn, tpb, TM, TM_GRP,
                           g_final[None, :] if last else None)
            if ctx_out:
                xc2 = moe_block(xc2, oc, g_f, mod_c, w_out_l, w_r, w_gu, w_dn, None, min(TM, mc), TM_GRP)
            if last:
                return x2.reshape(bn, S, d)
    return final_norm(x2, g_final[None, :], TM).reshape(bn, S, d)
```

```python
import functools
import math

import numpy as np
import jax
import jax.numpy as jnp
from jax import lax
from jax.experimental import pallas as pl
from jax.experimental.pallas import tpu as pltpu

DEPTH = 2
GRID_W = 64
EPS = 1e-6
W_POOL = 256
W_CONV = 256
W_NA = 256
W_SSD = 256
POOL_WINDOWS = (2, 4, 8, 16)
POOL_GROUP = W_POOL // len(POOL_WINDOWS)
NA_HEADS = 4
NA_HEAD_DIM = W_NA // NA_HEADS
WIN_R = 8
WIN_C = 16
SSD_HEAD_DIM = 64
SSD_HEADS = W_SSD // SSD_HEAD_DIM
SSD_GROUPS = 2
SSD_STATE = 128
SSD_CHUNK = 128
SSD_XBC = W_SSD + 2 * SSD_GROUPS * SSD_STATE
D_IN_PROJ = W_POOL + 3 * W_CONV + 3 * W_NA + W_SSD + SSD_XBC + 2 * SSD_HEADS
N_EXPERTS = 8
TOP_K = 2

LANES = 128
D_IN_PAD = -(-D_IN_PROJ // LANES) * LANES
VMEM_LIMIT = 56 * 1024 * 1024
BF16 = jnp.bfloat16
F32 = jnp.float32


def _params(*sem):
    return pltpu.CompilerParams(dimension_semantics=sem, vmem_limit_bytes=VMEM_LIMIT)


def _norm_mod(x, g, scale, shift):
    ms = jnp.mean(x * x, axis=-1, keepdims=True)
    return (x * lax.rsqrt(ms + EPS) * g) * (1.0 + scale) + shift


def _mod_map(tiles_per_batch):
    if tiles_per_batch is None:
        return lambda i, *_: (0, 0, 0)
    return lambda i, *_: (i // tiles_per_batch, 0, 0)


def _resident(shape, index_map):
    return pl.BlockSpec(shape, index_map, pipeline_mode=pl.Buffered(1))


def _ada_kernel(c_ref, w_ref, b_ref, o_ref):
    c = c_ref[...]
    s = c * jax.nn.sigmoid(c)
    o_ref[...] = jnp.dot(s.astype(BF16), w_ref[...], preferred_element_type=F32) + b_ref[...]


def ada_modulation(c_rows, w, b):
    r, d = c_rows.shape
    n = w.shape[1]
    tn = 1536
    return pl.pallas_call(
        _ada_kernel,
        out_shape=jax.ShapeDtypeStruct((r, n), F32),
        grid=(n // tn,),
        in_specs=[pl.BlockSpec((r, d), lambda j: (0, 0)),
                  pl.BlockSpec((d, tn), lambda j: (0, j)),
                  pl.BlockSpec((1, tn), lambda j: (0, j))],
        out_specs=pl.BlockSpec((r, tn), lambda j: (0, j)),
        compiler_params=_params("arbitrary"),
        name="ada_modulation",
    )(c_rows, w, b)


def _in_proj_kernel(x_ref, g_ref, mod_ref, w_ref, o_ref, dt_ref):
    h = _norm_mod(x_ref[...], g_ref[...], mod_ref[0, 1:2, :], mod_ref[0, 0:1, :])
    pr = jnp.dot(h.astype(BF16), w_ref[...], preferred_element_type=F32)
    o_ref[...] = pr.astype(o_ref.dtype)
    dt_ref[...] = pr[:, DT_COL * LANES:(DT_COL + 1) * LANES]


def in_proj(x2, g, mod, w, tiles_per_batch, tm):
    m, d = x2.shape
    n = w.shape[1]
    return pl.pallas_call(
        _in_proj_kernel,
        out_shape=(jax.ShapeDtypeStruct((m, n), BF16), jax.ShapeDtypeStruct((m, LANES), F32)),
        grid=(m // tm,),
        in_specs=[pl.BlockSpec((tm, d), lambda i: (i, 0)),
                  pl.BlockSpec((1, d), lambda i: (0, 0)),
                  pl.BlockSpec((1, 6, d), _mod_map(tiles_per_batch)),
                  _resident((d, n), lambda i: (0, 0))],
        out_specs=(pl.BlockSpec((tm, n), lambda i: (i, 0)), pl.BlockSpec((tm, LANES), lambda i: (i, 0))),
        compiler_params=_params("parallel"),
        name="in_proj",
    )(x2, g, mod, w)


def _mix_residual(x, mod_ref, w_ref, part_refs):
    y, k0 = None, 0
    for p_ref in part_refs:
        k = p_ref.shape[1]
        t = jnp.dot(p_ref[...], w_ref[k0:k0 + k, :], preferred_element_type=F32)
        y = t if y is None else y + t
        k0 += k
    return x + mod_ref[0, 2:3, :] * y


def _part_specs(parts, tm):
    return [pl.BlockSpec((tm, p.shape[1]), lambda i, *_: (i, 0)) for p in parts]


FF_CHUNK = 512


def _swiglu(h, wgu, wd, ff):
    acc = None
    for c0 in range(0, ff, FF_CHUNK):
        c1 = min(c0 + FF_CHUNK, ff)
        gg = jnp.dot(h, wgu(c0, c1), preferred_element_type=F32)
        uu = jnp.dot(h, wgu(ff + c0, ff + c1), preferred_element_type=F32)
        a = (gg * jax.nn.sigmoid(gg) * uu).astype(BF16)
        part = jnp.dot(a, wd(c0, c1), preferred_element_type=F32)
        acc = part if acc is None else acc + part
    return acc


def _ffn_kernel(x_ref, g_ref, mod_ref, wout_ref, wgu_ref, wd_ref, *refs):
    x = _mix_residual(x_ref[...], mod_ref, wout_ref, refs[:-1])
    y_ref = refs[-1]
    h = _norm_mod(x, g_ref[...], mod_ref[0, 4:5, :], mod_ref[0, 3:4, :]).astype(BF16)
    y = _swiglu(h, lambda a, b: wgu_ref[:, a:b], lambda a, b: wd_ref[a:b, :], wd_ref.shape[0])
    y_ref[...] = x + mod_ref[0, 5:6, :] * y


def ffn_dense(x2, parts, g, mod, w_out, w_gu, w_down, tiles_per_batch, tm):
    m, d = x2.shape
    return pl.pallas_call(
        _ffn_kernel,
        out_shape=jax.ShapeDtypeStruct((m, d), F32),
        grid=(m // tm,),
        in_specs=[pl.BlockSpec((tm, d), lambda i: (i, 0)),
                  pl.BlockSpec((1, d), lambda i: (0, 0)),
                  pl.BlockSpec((1, 6, d), _mod_map(tiles_per_batch)),
                  _resident(w_out.shape, lambda i: (0, 0)),
                  _resident(w_gu.shape, lambda i: (0, 0)),
                  _resident(w_down.shape, lambda i: (0, 0))] + _part_specs(parts, tm),
        out_specs=pl.BlockSpec((tm, d), lambda i: (i, 0)),
        compiler_params=_params("parallel"),
        name="ffn_dense",
    )(x2, g, mod, w_out, w_gu, w_down, *parts)


ROUTE_E1, ROUTE_E2, ROUTE_R1, ROUTE_R2, ROUTE_G1, ROUTE_G2 = range(6)
ROUTE_ROWS = 8


def _router_kernel(x_ref, g_ref, mod_ref, wout_ref, wr_ref, *refs):
    part_refs = refs[:-6]
    x1_ref, h_ref, route_ref, route_t_ref, cnt_ref, base_ref = refs[-6:]

    @pl.when(pl.program_id(0) == 0)
    def _():
        base_ref[...] = jnp.zeros_like(base_ref)

    x = _mix_residual(x_ref[...], mod_ref, wout_ref, part_refs)
    x1_ref[...] = x
    h = _norm_mod(x, g_ref[...], mod_ref[0, 4:5, :], mod_ref[0, 3:4, :]).astype(BF16)
    h_ref[...] = h
    tm = h.shape[0]
    logits = jnp.dot(h, wr_ref[...], preferred_element_type=F32)
    lane = lax.broadcasted_iota(jnp.int32, logits.shape, 1)
    neg = jnp.float32(-jnp.inf)
    logits = jnp.where(lane < N_EXPERTS, logits, neg)
    m1 = jnp.max(logits, axis=-1, keepdims=True)
    i1 = jnp.min(jnp.where(logits == m1, lane, LANES), axis=-1, keepdims=True)
    rest = jnp.where(lane == i1, neg, logits)
    m2 = jnp.max(rest, axis=-1, keepdims=True)
    i2 = jnp.min(jnp.where(rest == m2, lane, LANES), axis=-1, keepdims=True)
    e2 = jnp.exp(m2 - m1)
    g1 = 1.0 / (1.0 + e2)
    g2 = e2 / (1.0 + e2)
    sel1, sel2 = lane == i1, lane == i2
    sel = jnp.where(sel1 | sel2, 1.0, 0.0)
    li = lax.broadcasted_iota(jnp.int32, (tm, tm), 0)
    si = lax.broadcasted_iota(jnp.int32, (tm, tm), 1)
    before = jnp.where(si < li, 1.0, 0.0).astype(BF16)
    rank = jnp.dot(before, sel.astype(BF16), preferred_element_type=F32) + base_ref[...]
    r1 = jnp.sum(jnp.where(sel1, rank, 0.0), axis=-1, keepdims=True)
    r2 = jnp.sum(jnp.where(sel2, rank, 0.0), axis=-1, keepdims=True)
    base_ref[...] += jnp.sum(sel, axis=0, keepdims=True)
    cnt_ref[...] = base_ref[...]
    fields = (i1.astype(F32), i2.astype(F32), r1, r2, g1, g2)
    route = jnp.zeros(logits.shape, F32)
    for k, v in enumerate(fields):
        route = jnp.where(lane == k, v, route)
    route_ref[...] = route
    route_t_ref[...] = route.T[:ROUTE_ROWS, :]


def moe_router(x2, parts, g, mod, w_out, w_router, tiles_per_batch, tm):
    m, d = x2.shape
    return pl.pallas_call(
        _router_kernel,
        out_shape=(jax.ShapeDtypeStruct((m, d), F32), jax.ShapeDtypeStruct((m, d), BF16),
                   jax.ShapeDtypeStruct((m, LANES), F32), jax.ShapeDtypeStruct((ROUTE_ROWS, m), F32),
                   jax.ShapeDtypeStruct((1, LANES), F32)),
        grid=(m // tm,),
        in_specs=[pl.BlockSpec((tm, d), lambda i: (i, 0)),
                  pl.BlockSpec((1, d), lambda i: (0, 0)),
                  pl.BlockSpec((1, 6, d), _mod_map(tiles_per_batch)),
                  _resident(w_out.shape, lambda i: (0, 0)),
                  pl.BlockSpec((d, LANES), lambda i: (0, 0))] + _part_specs(parts, tm),
        out_specs=(pl.BlockSpec((tm, d), lambda i: (i, 0)),
                   pl.BlockSpec((tm, d), lambda i: (i, 0)),
                   pl.BlockSpec((tm, LANES), lambda i: (i, 0)),
                   pl.BlockSpec((ROUTE_ROWS, tm), lambda i: (0, i)),
                   pl.BlockSpec((1, LANES), lambda i: (0, 0))),
        scratch_shapes=[pltpu.VMEM((1, LANES), F32)],
        compiler_params=_params("arbitrary"),
        name="moe_router",
    )(x2, g, mod, w_out, w_router, *parts)


def _moe_kernel(tile_ref, exp_ref, start_ref, end_ref, xg_ref, wgu_ref, wd_ref, y_ref, wgu_b, wd_b):
    w = pl.program_id(0)
    tm = xg_ref.shape[0]
    start, end = start_ref[w], end_ref[w]
    t0 = tile_ref[w] * tm

    @pl.when((w == 0) | (exp_ref[w] != exp_ref[jnp.maximum(w - 1, 0)]))
    def _():
        wgu_b[...] = wgu_ref[0].astype(BF16)
        wd_b[...] = wd_ref[0].astype(BF16)

    @pl.when(end > start)
    def _():
        y = _swiglu(xg_ref[...], lambda a, b: wgu_b[:, a:b], lambda a, b: wd_b[a:b, :], wd_b.shape[0])
        y = y.astype(y_ref.dtype)

        @pl.when(start == t0)
        def _():
            y_ref[...] = y

        @pl.when(start != t0)
        def _():
            row = t0 + lax.broadcasted_iota(jnp.int32, (tm, 1), 0)
            y_ref[...] = jnp.where(row >= start, y, y_ref[...])


def moe_grouped(item_tile, item_expert, item_start, item_end, xg, w_gu, w_down, tm):
    p, d = xg.shape
    grid_spec = pltpu.PrefetchScalarGridSpec(
        num_scalar_prefetch=4,
        grid=(item_tile.shape[0],),
        in_specs=[pl.BlockSpec((tm, d), lambda w, it, ie, s, e: (it[w], 0)),
                  _resident((1,) + w_gu.shape[1:], lambda w, it, ie, s, e: (ie[w], 0, 0)),
                  _resident((1,) + w_down.shape[1:], lambda w, it, ie, s, e: (ie[w], 0, 0))],
        out_specs=pl.BlockSpec((tm, d), lambda w, it, ie, s, e: (it[w], 0)),
        scratch_shapes=[pltpu.VMEM(w_gu.shape[1:], BF16), pltpu.VMEM(w_down.shape[1:], BF16)],
    )
    return pl.pallas_call(
        _moe_kernel,
        out_shape=jax.ShapeDtypeStruct((p, d), BF16),
        grid_spec=grid_spec,
        compiler_params=_params("arbitrary"),
        name="moe_grouped",
    )(item_tile, item_expert, item_start, item_end, xg, w_gu, w_down)


def _moe_combine_kernel(x_ref, ya_ref, yb_ref, route_ref, mod_ref, *refs):
    o_ref = refs[-1]
    r = route_ref[...]
    y = (r[:, ROUTE_G1:ROUTE_G1 + 1] * ya_ref[...].astype(F32) + r[:, ROUTE_G2:ROUTE_G2 + 1] * yb_ref[...].astype(F32))
    x = x_ref[...] + mod_ref[0, 5:6, :] * y
    if len(refs) == 2:
        ms = jnp.mean(x * x, axis=-1, keepdims=True)
        x = x * lax.rsqrt(ms + EPS) * refs[0][...]
    o_ref[...] = x


def moe_combine(x2, ya, yb, route, mod, tiles_per_batch, tm, g_final=None):
    m, d = x2.shape
    row = pl.BlockSpec((tm, d), lambda i: (i, 0))
    specs = [row, row, row, pl.BlockSpec((tm, LANES), lambda i: (i, 0)),
             pl.BlockSpec((1, 6, d), _mod_map(tiles_per_batch))]
    args = [x2, ya, yb, route, mod]
    if g_final is not None:
        specs.append(pl.BlockSpec((1, d), lambda i: (0, 0)))
        args.append(g_final)
    return pl.pallas_call(
        _moe_combine_kernel,
        out_shape=jax.ShapeDtypeStruct((m, d), F32),
        grid=(m // tm,),
        in_specs=specs,
        out_specs=row,
        compiler_params=_params("parallel"),
        name="moe_combine",
    )(*args)


def moe_block(x2, parts, g, mod, w_out, w_router, w_gu, w_down, tiles_per_batch, tm_tok, tm_grp, g_final=None):
    m, d = x2.shape
    x2, h, route, route_t, cnt = moe_router(x2, parts, g, mod, w_out, w_router, tiles_per_batch, tm_tok)
    cnt = cnt[0, :N_EXPERTS].astype(jnp.int32)
    p = m * TOP_K
    off = jnp.cumsum(cnt) - cnt
    field = lambda k: route_t[k].astype(jnp.int32)

    def row_of(e, r):
        for k in range(N_EXPERTS):
            r = r + jnp.where(e == k, off[k], 0)
        return r

    d1 = row_of(field(ROUTE_E1), field(ROUTE_R1))
    d2 = row_of(field(ROUTE_E2), field(ROUTE_R2))
    tok = jnp.arange(m, dtype=jnp.int32)
    _, src = lax.sort_key_val(jnp.concatenate([d1, d2]), jnp.concatenate([tok, tok]))
    n_row_tiles = p // tm_grp
    start = jnp.sort(jnp.concatenate([jnp.arange(n_row_tiles, dtype=jnp.int32) * tm_grp, off]))
    end = jnp.concatenate([start[1:], jnp.full((1,), p, jnp.int32)])
    item_tile = jnp.minimum(start // tm_grp, n_row_tiles - 1)
    item_expert = jnp.sum((off[None, :] <= start[:, None]).astype(jnp.int32), axis=1) - 1
    rows = lambda a, idx: a.at[idx].get(mode="promise_in_bounds")
    yg = moe_grouped(item_tile, item_expert, start, end, rows(h, src), w_gu, w_down, tm_grp)
    return moe_combine(x2, rows(yg, d1), rows(yg, d2), route, mod, tiles_per_batch, tm_tok, g_final)


def _final_norm_kernel(x_ref, g_ref, o_ref):
    x = x_ref[...]
    ms = jnp.mean(x * x, axis=-1, keepdims=True)
    o_ref[...] = x * lax.rsqrt(ms + EPS) * g_ref[...]


def final_norm(x2, g, tm):
    m, d = x2.shape
    return pl.pallas_call(
        _final_norm_kernel,
        out_shape=jax.ShapeDtypeStruct((m, d), F32),
        grid=(m // tm,),
        in_specs=[pl.BlockSpec((tm, d), lambda i: (i, 0)), pl.BlockSpec((1, d), lambda i: (0, 0))],
        out_specs=pl.BlockSpec((tm, d), lambda i: (i, 0)),
        compiler_params=_params("parallel"),
        name="final_norm",
    )(x2, g)


NA_QROWS = 4
NA_KROWS = NA_QROWS + WIN_R
Q_COL, K_COL, V_COL = 4, 5, 6


def _na_key_start(j, rows):
    return np.clip(j * NA_QROWS - WIN_R // 2, 0, rows - NA_KROWS)


def _na_bias_index(rows):
    nblk = rows // NA_QROWS
    pats = []
    for j in range(nblk):
        start = _na_key_start(j, rows)
        r = j * NA_QROWS + np.arange(NA_QROWS)
        sr = np.clip(r - WIN_R // 2, 0, rows - WIN_R)
        kr = start + np.arange(NA_KROWS)
        rvalid = (kr[None, :] >= sr[:, None]) & (kr[None, :] < sr[:, None] + WIN_R)
        ri = np.clip(kr[None, :] - r[:, None] + WIN_R - 1, 0, 2 * WIN_R - 2)
        pats.append((ri, rvalid))
    for j in range(2, nblk - 1):
        assert all(np.array_equal(a, b) for a, b in zip(pats[1], pats[j]))
    sel = [pats[0], pats[1], pats[nblk - 1]]
    ri = np.stack([p[0] for p in sel])
    rvalid = np.stack([p[1] for p in sel])
    c = np.arange(GRID_W)
    sc = np.clip(c - WIN_C // 2, 0, GRID_W - WIN_C)
    cvalid = (c[None, :] >= sc[:, None]) & (c[None, :] < sc[:, None] + WIN_C)
    ci = np.clip(c[None, :] - c[:, None] + WIN_C - 1, 0, 2 * WIN_C - 2)
    c_onehot = (ci[..., None] == np.arange(2 * WIN_C - 1)).astype(np.float32)
    valid = rvalid[:, :, None, :, None] & cvalid[None, None, :, None, :]
    return ri, c_onehot, valid


def na_bias_table(rpb, rows):
    ri, c_onehot, valid = _na_bias_index(rows)
    toep = jnp.einsum('hrd,qkd->hrqk', rpb, c_onehot, precision=lax.Precision.HIGHEST)
    b = jnp.take(toep, ri.reshape(-1), axis=1).reshape((NA_HEADS,) + ri.shape + (GRID_W, GRID_W))
    b = jnp.transpose(b, (1, 0, 2, 4, 3, 5))
    b = jnp.where(valid[:, None], b, -jnp.inf)
    return b.reshape(3, NA_HEADS, NA_QROWS * GRID_W, NA_KROWS * GRID_W).astype(F32)


def _attend_heads(q, key_sets, bias_fn):
    n, w = q.shape
    lane = lax.broadcasted_iota(jnp.int32, (1, w), 1)
    out = jnp.zeros((n, w), F32)
    for h in range(NA_HEADS):
        mh = (lane >= h * NA_HEAD_DIM) & (lane < (h + 1) * NA_HEAD_DIM)
        qh = jnp.where(mh, q, 0.0).astype(BF16)
        scores = []
        for i, (k, _) in enumerate(key_sets):
            s = lax.dot_general(qh, k, (((1,), (1,)), ((), ())), preferred_element_type=F32)
            b = bias_fn(i, h)
            scores.append(s if b is None else s + b)
        m = scores[0].max(axis=-1, keepdims=True)
        for s in scores[1:]:
            m = jnp.maximum(m, s.max(axis=-1, keepdims=True))
        denom = jnp.zeros((n, 1), F32)
        acc = jnp.zeros((n, w), F32)
        for s, (_, v) in zip(scores, key_sets):
            p = jnp.exp(s - m)
            denom = denom + p.sum(axis=-1, keepdims=True)
            acc = acc + jnp.dot(p.astype(BF16), v, preferred_element_type=F32)
        out = out + jnp.where(mh, acc / denom, 0.0)
    return out


def _na_kernel(q_ref, k_ref, v_ref, kc_ref, vc_ref, bias_ref, o_ref, *, rows):
    j = pl.program_id(1)
    start = jnp.clip(j * NA_QROWS - WIN_R // 2, 0, rows - NA_KROWS)
    t0 = pl.multiple_of(start * GRID_W, GRID_W)
    nk = NA_KROWS * GRID_W
    kw = k_ref[0, pl.ds(t0, nk), :]
    vw = v_ref[0, pl.ds(t0, nk), :]
    q = q_ref[0] * (1.0 / math.sqrt(NA_HEAD_DIM))
    o = _attend_heads(q, [(kw, vw), (kc_ref[0], vc_ref[0])], lambda i, h: bias_ref[0, h] if i == 0 else None)
    o_ref[0] = o.astype(o_ref.dtype)


def na_attention(pr3, prc3, bias):
    bn, S, _ = pr3.shape
    Lc = prc3.shape[1]
    rows = S // GRID_W
    nblk = rows // NA_QROWS
    nq = NA_QROWS * GRID_W

    def pat(b, j):
        return (jnp.where(j == 0, 0, jnp.where(j == nblk - 1, 2, 1)), 0, 0, 0)

    return pl.pallas_call(
        functools.partial(_na_kernel, rows=rows),
        out_shape=jax.ShapeDtypeStruct((bn, S, W_NA), BF16),
        grid=(bn, nblk),
        in_specs=[pl.BlockSpec((1, nq, W_NA), lambda b, j: (b, j, Q_COL)),
                  pl.BlockSpec((1, S, W_NA), lambda b, j: (b, 0, K_COL)),
                  pl.BlockSpec((1, S, W_NA), lambda b, j: (b, 0, V_COL)),
                  pl.BlockSpec((1, Lc, W_NA), lambda b, j: (b, 0, K_COL)),
                  pl.BlockSpec((1, Lc, W_NA), lambda b, j: (b, 0, V_COL)),
                  pl.BlockSpec((1,) + bias.shape[1:], pat)],
        out_specs=pl.BlockSpec((1, nq, W_NA), lambda b, j: (b, j, 0)),
        compiler_params=_params("parallel", "arbitrary"),
        name="na_attention",
    )(pr3, pr3, pr3, prc3, prc3, bias)


def _ctx_attn_kernel(q_ref, k_ref, v_ref, o_ref):
    q = q_ref[0] * (1.0 / math.sqrt(NA_HEAD_DIM))
    o = _attend_heads(q, [(k_ref[0], v_ref[0])], lambda i, h: None)
    o_ref[0] = o.astype(o_ref.dtype)


def ctx_attention(prc3):
    bn, Lc, _ = prc3.shape
    return pl.pallas_call(
        _ctx_attn_kernel,
        out_shape=jax.ShapeDtypeStruct((bn, Lc, W_NA), BF16),
        grid=(bn,),
        in_specs=[pl.BlockSpec((1, Lc, W_NA), lambda b: (b, 0, Q_COL)),
                  pl.BlockSpec((1, Lc, W_NA), lambda b: (b, 0, K_COL)),
                  pl.BlockSpec((1, Lc, W_NA), lambda b: (b, 0, V_COL))],
        out_specs=pl.BlockSpec((1, Lc, W_NA), lambda b: (b, 0, 0)),
        compiler_params=_params("parallel"),
        name="ctx_attention",
    )(prc3, prc3, prc3)


POOL_COL, CONV_H_COL, CONV_B_COL, CONV_C_COL = 0, 1, 2, 3


SHIFT_PAD = 8


def _stage_rows(pad_ref, x):
    n = x.shape[0]
    zeros = jnp.zeros((SHIFT_PAD, x.shape[1]), pad_ref.dtype)
    pad_ref[0:SHIFT_PAD, :] = zeros
    pad_ref[SHIFT_PAD + n:2 * SHIFT_PAD + n, :] = zeros
    pad_ref[SHIFT_PAD:SHIFT_PAD + n, :] = x
    return x


def _shifted_rows(pad_ref, k, n):
    return pad_ref[SHIFT_PAD - k:SHIFT_PAD - k + n, :]


def _local_mix_kernel(u_ref, h_ref, bg_ref, cg_ref, pw_ref, ps_ref, cw_ref, op_ref, oc_ref, *stage, seq):
    su, st2, st4, st8, sl2, sl4, sv = stage
    row = lax.broadcasted_iota(jnp.int32, (seq, 1), 0)
    lane = lax.broadcasted_iota(jnp.int32, (1, W_POOL), 1)
    u = _stage_rows(su, u_ref[0].astype(F32))
    trailing, leading = {1: u}, {1: u}
    staged_t, staged_l = {1: su, 2: st2, 4: st4, 8: st8}, {1: su, 2: sl2, 4: sl4}
    for w in (1, 2, 4):
        trailing[2 * w] = _stage_rows(staged_t[2 * w], trailing[w] + _shifted_rows(staged_t[w], w, seq))
        leading[2 * w] = leading[w] + _shifted_rows(staged_l[w], -w, seq)
        if 2 * w in staged_l:
            _stage_rows(staged_l[2 * w], leading[2 * w])
    rowf = row.astype(F32)
    win_sum = jnp.zeros_like(u)
    cnt = jnp.zeros_like(u)
    for g, win in enumerate(POOL_WINDOWS):
        half = win // 2
        mg = (lane >= g * POOL_GROUP) & (lane < (g + 1) * POOL_GROUP)
        s = _shifted_rows(staged_t[half], 1, seq) + leading[half]
        n = jnp.minimum(rowf + half, float(seq)) - jnp.maximum(rowf - half, 0.0)
        win_sum = jnp.where(mg, s, win_sum)
        cnt = jnp.where(mg, n, cnt)
    p = win_sum / cnt - u
    pooled = jnp.dot(p.astype(BF16), pw_ref[...], preferred_element_type=F32) * ps_ref[...]
    op_ref[0] = pooled.astype(op_ref.dtype)
    v = _stage_rows(sv, cg_ref[0].astype(F32) * h_ref[0].astype(F32))
    conv = (_shifted_rows(sv, 1, seq) * cw_ref[0:1, :] + v * cw_ref[1:2, :] + _shifted_rows(sv, -1, seq) * cw_ref[2:3, :])
    oc_ref[0] = (bg_ref[0].astype(F32) * conv).astype(oc_ref.dtype)


def local_mix(pr3, pool_w, pool_scale, conv_w):
    bn, seq, _ = pr3.shape
    pw = jax.scipy.linalg.block_diag(*[pool_w[g] for g in range(len(POOL_WINDOWS))]).astype(BF16)
    blk = lambda col: pl.BlockSpec((1, seq, W_POOL), lambda b: (b, 0, col))
    full = lambda a: pl.BlockSpec(a.shape, lambda b: (0,) * a.ndim)
    args = (pw, pool_scale[None, :], conv_w)
    return pl.pallas_call(
        functools.partial(_local_mix_kernel, seq=seq),
        out_shape=(jax.ShapeDtypeStruct((bn, seq, W_POOL), BF16), jax.ShapeDtypeStruct((bn, seq, W_CONV), BF16)),
        grid=(bn,),
        in_specs=[blk(POOL_COL), blk(CONV_H_COL), blk(CONV_B_COL), blk(CONV_C_COL)] + [full(a) for a in args],
        out_specs=(pl.BlockSpec((1, seq, W_POOL), lambda b: (b, 0, 0)),
                   pl.BlockSpec((1, seq, W_CONV), lambda b: (b, 0, 0))),
        scratch_shapes=[pltpu.VMEM((seq + 2 * SHIFT_PAD, W_POOL), F32)] * 7,
        compiler_params=_params("parallel"),
        name="local_mix",
    )(pr3, pr3, pr3, pr3, *args)


Z_COL, XS_COL, BS_COL, CS_COL = 7, 8, 9, 10
DT_COL = 22
HEADS_PER_GROUP = SSD_HEADS // SSD_GROUPS
GROUP_W = HEADS_PER_GROUP * SSD_HEAD_DIM


def _head_mask(h):
    lane = lax.broadcasted_iota(jnp.int32, (1, W_SSD), 1)
    return (lane >= h * SSD_HEAD_DIM) & (lane < (h + 1) * SSD_HEAD_DIM)


def _expand_heads(v, d):
    out = jnp.zeros((v.shape[0], W_SSD), F32)
    for h in range(SSD_HEADS):
        k = d * SSD_HEADS + h
        out = jnp.where(_head_mask(h), v[:, k:k + 1], out)
    return out


def _ssd_chunk_terms(c, xs_ref, b_ref, c_ref, dt_ref, la_ref, dsk_ref, y_ref, upd_ref, dec_ref, need_y):
    T = SSD_CHUNK
    r0 = pl.multiple_of(c * T, T)
    xs = xs_ref[pl.ds(r0, T), :]
    bm = b_ref[pl.ds(r0, T), :]
    cm = c_ref[pl.ds(r0, T), :]
    dt = dt_ref[pl.ds(r0, T), :]
    la = la_ref[pl.ds(r0, T), :]
    li = lax.broadcasted_iota(jnp.int32, (T, T), 0)
    si = lax.broadcasted_iota(jnp.int32, (T, T), 1)
    causal = si <= li
    prefix = jnp.dot(causal.astype(F32), la, precision=lax.Precision.HIGHEST, preferred_element_type=F32)
    total = prefix[T - 1:T, :]
    lane = lax.broadcasted_iota(jnp.int32, (1, LANES), 1)
    acum = jnp.where(lane < SSD_HEADS, prefix, total - prefix + la)
    bt = bm.astype(F32).T.astype(BF16)
    if need_y:
        acum_t = acum.T
        cb = [lax.dot_general(cm[:, g * SSD_STATE:(g + 1) * SSD_STATE], bm[:, g * SSD_STATE:(g + 1) * SSD_STATE],
                              (((1,), (1,)), ((), ())), preferred_element_type=F32) for g in range(SSD_GROUPS)]
        y = jnp.zeros((T, W_SSD), F32)
    for d in range(2):
        acum_e = _expand_heads(acum, d)
        tot_e = _expand_heads(total, d)
        xdt = xs * _expand_heads(dt, d)
        xw = (xdt * jnp.exp(tot_e - acum_e)).astype(BF16)
        upd = [jnp.dot(bt[g * SSD_STATE:(g + 1) * SSD_STATE, :], xw[:, g * GROUP_W:(g + 1) * GROUP_W],
                       preferred_element_type=F32) for g in range(SSD_GROUPS)]
        upd_ref[c, d] = jnp.concatenate(upd, axis=1)
        dec_ref[c, d, 0:T, :] = jnp.exp(acum_e)
        dec_ref[c, d, T:T + 1, :] = jnp.exp(tot_e)
        if need_y:
            mask = causal if d == 0 else (si >= li)
            xdt_b = xdt.astype(BF16)
            for h in range(SSD_HEADS):
                k = d * SSD_HEADS + h
                decay = jnp.exp(jnp.where(mask, acum[:, k:k + 1] - acum_t[k:k + 1, :], -jnp.inf))
                scores = (cb[h // HEADS_PER_GROUP] * decay).astype(BF16)
                y = y + jnp.where(_head_mask(h), jnp.dot(scores, xdt_b, preferred_element_type=F32), 0.0)
    if need_y:
        y_ref[pl.ds(r0, T), :] = xs * dsk_ref[...] + y


def _ssd_chunk_state(c, d, c_ref, y_ref, upd_ref, dec_ref, st_ref, need_y):
    T = SSD_CHUNK
    r0 = pl.multiple_of(c * T, T)
    st = st_ref[d]
    if need_y:
        cm = c_ref[pl.ds(r0, T), :]
        st_b = st.astype(BF16)
        ys = [jnp.dot(cm[:, g * SSD_STATE:(g + 1) * SSD_STATE], st_b[:, g * GROUP_W:(g + 1) * GROUP_W],
                      preferred_element_type=F32) for g in range(SSD_GROUPS)]
        y_ref[pl.ds(r0, T), :] += jnp.concatenate(ys, axis=1) * dec_ref[c, d, 0:T, :]
    st_ref[d] = dec_ref[c, d, T:T + 1, :] * st + upd_ref[c, d]


def _ssd_kernel(z_ref, xs_in, bs_in, cs_in, dtr_ref, cw_ref, cb_ref, dtb_ref, alog_ref, dsk_ref, ng_ref, h0_ref,
                *refs, seq, need_y):
    if need_y:
        o_ref, hT_ref, xs_ref, b_ref, c_ref, dt_ref, la_ref, upd_ref, dec_ref, st_ref, *stage, y_ref = refs
    else:
        hT_ref, xs_ref, b_ref, c_ref, dt_ref, la_ref, upd_ref, dec_ref, st_ref, *stage = refs
        y_ref = None
    nc = seq // SSD_CHUNK
    for gi, (src, dst, pad) in enumerate(zip((xs_in, bs_in, cs_in), (xs_ref, b_ref, c_ref), stage)):
        sl = slice(gi * W_SSD, (gi + 1) * W_SSD)
        x = _stage_rows(pad, src[0].astype(F32))
        cv = (_shifted_rows(pad, 1, seq) * cw_ref[0:1, sl] + x * cw_ref[1:2, sl]
              + _shifted_rows(pad, -1, seq) * cw_ref[2:3, sl] + cb_ref[:, sl])
        dst[...] = (cv * jax.nn.sigmoid(cv)).astype(dst.dtype)
    dtv = dtr_ref[0] + dtb_ref[...]
    dt = jnp.maximum(dtv, 0.0) + jnp.log1p(jnp.exp(-jnp.abs(dtv)))
    dt_ref[...] = dt
    la_ref[...] = dt * (-jnp.exp(alog_ref[...]))
    st_ref[...] = h0_ref[0]

    def terms(c, carry):
        _ssd_chunk_terms(c, xs_ref, b_ref, c_ref, dt_ref, la_ref, dsk_ref, y_ref, upd_ref, dec_ref, need_y)
        return carry

    lax.fori_loop(0, nc, terms, 0, unroll=2)

    def recur(i, carry):
        _ssd_chunk_state(i, 0, c_ref, y_ref, upd_ref, dec_ref, st_ref, need_y)
        _ssd_chunk_state(nc - 1 - i, 1, c_ref, y_ref, upd_ref, dec_ref, st_ref, need_y)
        return carry

    lax.fori_loop(0, nc, recur, 0, unroll=2)
    hT_ref[0] = st_ref[...]
    if need_y:
        z = z_ref[0].astype(F32)
        yz = y_ref[...] * (z * jax.nn.sigmoid(z))
        ms = jnp.mean(yz * yz, axis=-1, keepdims=True)
        o_ref[0] = (yz * lax.rsqrt(ms + EPS) * ng_ref[...]).astype(o_ref.dtype)


def ssd_scan(pr3, dt3, h0, conv_w, conv_b, dt_bias, a_log, d_skip, norm_g, need_y):
    bn, seq, _ = pr3.shape
    nc = seq // SSD_CHUNK
    pad = LANES - 2 * SSD_HEADS
    dtb = jnp.pad(dt_bias.reshape(1, -1), ((0, 0), (0, pad)))
    alog = jnp.pad(a_log.reshape(1, -1), ((0, 0), (0, pad)))
    dsk = jnp.repeat(d_skip, SSD_HEAD_DIM)[None, :]
    blk = lambda col: pl.BlockSpec((1, seq, W_SSD), lambda b: (b, 0, col))
    full = lambda a: pl.BlockSpec(a.shape, lambda b: (0,) * a.ndim)
    st_shape = (1, 2, SSD_STATE, W_SSD)
    st_spec = pl.BlockSpec(st_shape, lambda b: (b, 0, 0, 0))
    args = (conv_w, conv_b[None, :], dtb, alog, dsk, norm_g[None, :])
    out_shape = [jax.ShapeDtypeStruct((bn,) + st_shape[1:], F32)]
    out_specs = [st_spec]
    scratch = [pltpu.VMEM((seq, W_SSD), F32),
               pltpu.VMEM((seq, W_SSD), BF16), pltpu.VMEM((seq, W_SSD), BF16),
               pltpu.VMEM((seq, LANES), F32), pltpu.VMEM((seq, LANES), F32),
               pltpu.VMEM((nc, 2, SSD_STATE, W_SSD), F32),
               pltpu.VMEM((nc, 2, SSD_CHUNK + 8, W_SSD), F32),
               pltpu.VMEM(st_shape[1:], F32)]
    scratch += [pltpu.VMEM((seq + 2 * SHIFT_PAD, W_SSD), F32)] * 3
    if need_y:
        out_shape.insert(0, jax.ShapeDtypeStruct((bn, seq, W_SSD), BF16))
        out_specs.insert(0, pl.BlockSpec((1, seq, W_SSD), lambda b: (b, 0, 0)))
        scratch.append(pltpu.VMEM((seq, W_SSD), F32))
    res = pl.pallas_call(
        functools.partial(_ssd_kernel, seq=seq, need_y=need_y),
        out_shape=out_shape,
        grid=(bn,),
        in_specs=[blk(Z_COL), blk(XS_COL), blk(BS_COL), blk(CS_COL),
                  pl.BlockSpec((1, seq, LANES), lambda b: (b, 0, 0))]
                 + [full(a) for a in args] + [st_spec],
        out_specs=out_specs,
        scratch_shapes=scratch,
        compiler_params=_params("parallel"),
        name="ssd_scan",
    )(pr3, pr3, pr3, pr3, dt3, *args, h0)
    return (res[0], res[1]) if need_y else (None, res[0])


def ssd_mix(pr3, dt3, prc3, dtc3, conv_w, conv_b, dt_bias, a_log, d_skip, norm_g, ctx_out):
    bn = pr3.shape[0]
    h0 = jnp.zeros((bn, 2, SSD_STATE, W_SSD), F32)
    oc, hc = ssd_scan(prc3, dtc3, h0, conv_w, conv_b, dt_bias, a_log, d_skip, norm_g, ctx_out)
    o, _ = ssd_scan(pr3, dt3, hc, conv_w, conv_b, dt_bias, a_log, d_skip, norm_g, True)
    return o, oc


def mixer(pr3, dt3, prc3, dtc3, pool_w, pool_scale, conv_w, rpb, s_conv_w, s_conv_b, dt_bias, a_log, d_skip,
          s_norm_g, ctx_out):
    bn, S, _ = pr3.shape
    o_ssd, oc_ssd = ssd_mix(pr3, dt3, prc3, dtc3, s_conv_w, s_conv_b, dt_bias, a_log, d_skip, s_norm_g, ctx_out)
    o = [*local_mix(pr3, pool_w, pool_scale, conv_w),
         na_attention(pr3, prc3, na_bias_table(rpb, S // GRID_W)), o_ssd]
    if not ctx_out:
        return o, None
    oc = [*local_mix(prc3, pool_w, pool_scale, conv_w), ctx_attention(prc3), oc_ssd]
    return o, oc


TM = 512
TM_PROJ = 1024
TM_GRP = 512


def kernel(x, c, ctx, c_ctx, w_ada, b_ada, g_mix, g_ffn, w_in, w_out, pool_w, pool_scale, conv_w, na_rpb,
           ssd_conv_w, ssd_conv_b, ssd_dt_bias, ssd_a_log, ssd_d, ssd_norm_g, ffn_w_gu, ffn_w_down,
           moe_router, moe_w_gu, moe_w_down, g_final):
    bn, S, d = x.shape
    Lc = ctx.shape[1]
    m, mc = bn * S, bn * Lc
    tpb = S // TM
    x2 = x.reshape(m, d)
    xc2 = ctx.reshape(mc, d)
    c_rows = jnp.concatenate([c, c_ctx[None, :], jnp.zeros((7, d), F32)], axis=0)
    for l in range(DEPTH):
        ctx_out = l < DEPTH - 1
        last = l == DEPTH - 1
        ada = ada_modulation(c_rows, w_ada[l].astype(BF16), b_ada[l][None, :])
        mod = ada[:bn].reshape(bn, 6, d)
        mod_c = ada[bn:bn + 1].reshape(1, 6, d)
        w_in_l = jnp.pad(w_in[l], ((0, 0), (0, D_IN_PAD - D_IN_PROJ))).astype(BF16)
        g_m = g_mix[l][None, :]
        g_f = g_ffn[l][None, :]
        pr, dt = in_proj(x2, g_m, mod, w_in_l, S // TM_PROJ, TM_PROJ)
        pr_c, dt_c = in_proj(xc2, g_m, mod_c, w_in_l, None, min(TM_PROJ, mc))
        o, oc = mixer(pr.reshape(bn, S, -1), dt.reshape(bn, S, -1), pr_c.reshape(bn, Lc, -1),
                      dt_c.reshape(bn, Lc, -1), pool_w[l], pool_scale[l], conv_w[l],
                      na_rpb[l], ssd_conv_w[l], ssd_conv_b[l], ssd_dt_bias[l], ssd_a_log[l], ssd_d[l],
                      ssd_norm_g[l], ctx_out)
        w_out_l = w_out[l].astype(BF16)
        o = [p.reshape(m, -1) for p in o]
        oc = [p.reshape(mc, -1) for p in oc] if ctx_out else None
        j = l // 2
        if l % 2 == 0:
            w_gu = ffn_w_gu[j].astype(BF16)
            w_dn = ffn_w_down[j].astype(BF16)
            x2 = ffn_dense(x2, o, g_f, mod, w_out_l, w_gu, w_dn, tpb, TM)
            if ctx_out:
                xc2 = ffn_dense(xc2, oc, g_f, mod_c, w_out_l, w_gu, w_dn, None, min(TM, mc))
        else:
            w_r = jnp.pad(moe_router[j], ((0, 0), (0, LANES - N_EXPERTS))).astype(BF16)
            w_gu, w_dn = moe_w_gu[j], moe_w_down[j]
            x2 = moe_block(x2, o, g_f, mod, w_out_l, w_r, w_gu, w_dn, tpb, TM, TM_GRP,
                           g_final[None, :] if last else None)
            if ctx_out:
                xc2 = moe_block(xc2, oc, g_f, mod_c, w_out_l, w_r, w_gu, w_dn, None, min(TM, mc), TM_GRP)
            if last:
                return x2.reshape(bn, S, d)
    return final_norm(x2, g_final[None, :], TM).reshape(bn, S, d)
```

```python
import functools
import math

import numpy as np
import jax
import jax.numpy as jnp
from jax import lax
from jax.experimental import pallas as pl
from jax.experimental.pallas import tpu as pltpu

DEPTH = 2
GRID_W = 64
EPS = 1e-6
W_POOL = 256
W_CONV = 256
W_NA = 256
W_SSD = 256
POOL_WINDOWS = (2, 4, 8, 16)
POOL_GROUP = W_POOL // len(POOL_WINDOWS)
NA_HEADS = 4
NA_HEAD_DIM = W_NA // NA_HEADS
WIN_R = 8
WIN_C = 16
SSD_HEAD_DIM = 64
SSD_HEADS = W_SSD // SSD_HEAD_DIM
SSD_GROUPS = 2
SSD_STATE = 128
SSD_CHUNK = 128
SSD_XBC = W_SSD + 2 * SSD_GROUPS * SSD_STATE
D_IN_PROJ = W_POOL + 3 * W_CONV + 3 * W_NA + W_SSD + SSD_XBC + 2 * SSD_HEADS
N_EXPERTS = 8
TOP_K = 2

LANES = 128
D_IN_PAD = -(-D_IN_PROJ // LANES) * LANES
VMEM_LIMIT = 56 * 1024 * 1024
BF16 = jnp.bfloat16
F32 = jnp.float32


def _params(*sem):
    return pltpu.CompilerParams(dimension_semantics=sem, vmem_limit_bytes=VMEM_LIMIT)


def _norm_mod(x, g, scale, shift):
    ms = jnp.mean(x * x, axis=-1, keepdims=True)
    return (x * lax.rsqrt(ms + EPS) * g) * (1.0 + scale) + shift


def _mod_map(tiles_per_batch):
    if tiles_per_batch is None:
        return lambda i, *_: (0, 0, 0)
    return lambda i, *_: (i // tiles_per_batch, 0, 0)


def _resident(shape, index_map):
    return pl.BlockSpec(shape, index_map, pipeline_mode=pl.Buffered(1))


def _ada_kernel(c_ref, w_ref, b_ref, o_ref):
    c = c_ref[...]
    s = c * jax.nn.sigmoid(c)
    o_ref[...] = jnp.dot(s.astype(BF16), w_ref[...], preferred_element_type=F32) + b_ref[...]


def ada_modulation(c_rows, w, b):
    r, d = c_rows.shape
    n = w.shape[1]
    tn = 1536
    return pl.pallas_call(
        _ada_kernel,
        out_shape=jax.ShapeDtypeStruct((r, n), F32),
        grid=(n // tn,),
        in_specs=[pl.BlockSpec((r, d), lambda j: (0, 0)),
                  pl.BlockSpec((d, tn), lambda j: (0, j)),
                  pl.BlockSpec((1, tn), lambda j: (0, j))],
        out_specs=pl.BlockSpec((r, tn), lambda j: (0, j)),
        compiler_params=_params("arbitrary"),
        name="ada_modulation",
    )(c_rows, w, b)


def _in_proj_kernel(x_ref, g_ref, mod_ref, w_ref, o_ref, dt_ref):
    h = _norm_mod(x_ref[...], g_ref[...], mod_ref[0, 1:2, :], mod_ref[0, 0:1, :])
    pr = jnp.dot(h.astype(BF16), w_ref[...], preferred_element_type=F32)
    o_ref[...] = pr.astype(o_ref.dtype)
    dt_ref[...] = pr[:, DT_COL * LANES:(DT_COL + 1) * LANES]


def in_proj(x2, g, mod, w, tiles_per_batch, tm):
    m, d = x2.shape
    n = w.shape[1]
    return pl.pallas_call(
        _in_proj_kernel,
        out_shape=(jax.ShapeDtypeStruct((m, n), BF16), jax.ShapeDtypeStruct((m, LANES), F32)),
        grid=(m // tm,),
        in_specs=[pl.BlockSpec((tm, d), lambda i: (i, 0)),
                  pl.BlockSpec((1, d), lambda i: (0, 0)),
                  pl.BlockSpec((1, 6, d), _mod_map(tiles_per_batch)),
                  _resident((d, n), lambda i: (0, 0))],
        out_specs=(pl.BlockSpec((tm, n), lambda i: (i, 0)), pl.BlockSpec((tm, LANES), lambda i: (i, 0))),
        compiler_params=_params("parallel"),
        name="in_proj",
    )(x2, g, mod, w)


def _mix_residual(x, mod_ref, w_ref, part_refs):
    y, k0 = None, 0
    for p_ref in part_refs:
        k = p_ref.shape[1]
        t = jnp.dot(p_ref[...], w_ref[k0:k0 + k, :], preferred_element_type=F32)
        y = t if y is None else y + t
        k0 += k
    return x + mod_ref[0, 2:3, :] * y


def _part_specs(parts, tm):
    return [pl.BlockSpec((tm, p.shape[1]), lambda i, *_: (i, 0)) for p in parts]


FF_CHUNK = 512


def _swiglu(h, wgu, wd, ff):
    acc = None
    for c0 in range(0, ff, FF_CHUNK):
        c1 = min(c0 + FF_CHUNK, ff)
        gg = jnp.dot(h, wgu(c0, c1), preferred_element_type=F32)
        uu = jnp.dot(h, wgu(ff + c0, ff + c1), preferred_element_type=F32)
        a = (gg * jax.nn.sigmoid(gg) * uu).astype(BF16)
        part = jnp.dot(a, wd(c0, c1), preferred_element_type=F32)
        acc = part if acc is None else acc + part
    return acc


def _ffn_kernel(x_ref, g_ref, mod_ref, wout_ref, wgu_ref, wd_ref, *refs):
    x = _mix_residual(x_ref[...], mod_ref, wout_ref, refs[:-1])
    y_ref = refs[-1]
    h = _norm_mod(x, g_ref[...], mod_ref[0, 4:5, :], mod_ref[0, 3:4, :]).astype(BF16)
    y = _swiglu(h, lambda a, b: wgu_ref[:, a:b], lambda a, b: wd_ref[a:b, :], wd_ref.shape[0])
    y_ref[...] = x + mod_ref[0, 5:6, :] * y


def ffn_dense(x2, parts, g, mod, w_out, w_gu, w_down, tiles_per_batch, tm):
    m, d = x2.shape
    return pl.pallas_call(
        _ffn_kernel,
        out_shape=jax.ShapeDtypeStruct((m, d), F32),
        grid=(m // tm,),
        in_specs=[pl.BlockSpec((tm, d), lambda i: (i, 0)),
                  pl.BlockSpec((1, d), lambda i: (0, 0)),
                  pl.BlockSpec((1, 6, d), _mod_map(tiles_per_batch)),
                  _resident(w_out.shape, lambda i: (0, 0)),
                  _resident(w_gu.shape, lambda i: (0, 0)),
                  _resident(w_down.shape, lambda i: (0, 0))] + _part_specs(parts, tm),
        out_specs=pl.BlockSpec((tm, d), lambda i: (i, 0)),
        compiler_params=_params("parallel"),
        name="ffn_dense",
    )(x2, g, mod, w_out, w_gu, w_down, *parts)


ROUTE_E1, ROUTE_E2, ROUTE_R1, ROUTE_R2, ROUTE_G1, ROUTE_G2 = range(6)
ROUTE_ROWS = 8


def _router_kernel(x_ref, g_ref, mod_ref, wout_ref, wr_ref, *refs):
    part_refs = refs[:-6]
    x1_ref, h_ref, route_ref, route_t_ref, cnt_ref, base_ref = refs[-6:]

    @pl.when(pl.program_id(0) == 0)
    def _():
        base_ref[...] = jnp.zeros_like(base_ref)

    x = _mix_residual(x_ref[...], mod_ref, wout_ref, part_refs)
    x1_ref[...] = x
    h = _norm_mod(x, g_ref[...], mod_ref[0, 4:5, :], mod_ref[0, 3:4, :]).astype(BF16)
    h_ref[...] = h
    tm = h.shape[0]
    logits = jnp.dot(h, wr_ref[...], preferred_element_type=F32)
    lane = lax.broadcasted_iota(jnp.int32, logits.shape, 1)
    neg = jnp.float32(-jnp.inf)
    logits = jnp.where(lane < N_EXPERTS, logits, neg)
    m1 = jnp.max(logits, axis=-1, keepdims=True)
    i1 = jnp.min(jnp.where(logits == m1, lane, LANES), axis=-1, keepdims=True)
    rest = jnp.where(lane == i1, neg, logits)
    m2 = jnp.max(rest, axis=-1, keepdims=True)
    i2 = jnp.min(jnp.where(rest == m2, lane, LANES), axis=-1, keepdims=True)
    e2 = jnp.exp(m2 - m1)
    g1 = 1.0 / (1.0 + e2)
    g2 = e2 / (1.0 + e2)
    sel1, sel2 = lane == i1, lane == i2
    sel = jnp.where(sel1 | sel2, 1.0, 0.0)
    li = lax.broadcasted_iota(jnp.int32, (tm, tm), 0)
    si = lax.broadcasted_iota(jnp.int32, (tm, tm), 1)
    before = jnp.where(si < li, 1.0, 0.0).astype(BF16)
    rank = jnp.dot(before, sel.astype(BF16), preferred_element_type=F32) + base_ref[...]
    r1 = jnp.sum(jnp.where(sel1, rank, 0.0), axis=-1, keepdims=True)
    r2 = jnp.sum(jnp.where(sel2, rank, 0.0), axis=-1, keepdims=True)
    base_ref[...] += jnp.sum(sel, axis=0, keepdims=True)
    cnt_ref[...] = base_ref[...]
    fields = (i1.astype(F32), i2.astype(F32), r1, r2, g1, g2)
    route = jnp.zeros(logits.shape, F32)
    for k, v in enumerate(fields):
        route = jnp.where(lane == k, v, route)
    route_ref[...] = route
    route_t_ref[...] = route.T[:ROUTE_ROWS, :]


def moe_router(x2, parts, g, mod, w_out, w_router, tiles_per_batch, tm):
    m, d = x2.shape
    return pl.pallas_call(
        _router_kernel,
        out_shape=(jax.ShapeDtypeStruct((m, d), F32), jax.ShapeDtypeStruct((m, d), BF16),
                   jax.ShapeDtypeStruct((m, LANES), F32), jax.ShapeDtypeStruct((ROUTE_ROWS, m), F32),
                   jax.ShapeDtypeStruct((1, LANES), F32)),
        grid=(m // tm,),
        in_specs=[pl.BlockSpec((tm, d), lambda i: (i, 0)),
                  pl.BlockSpec((1, d), lambda i: (0, 0)),
                  pl.BlockSpec((1, 6, d), _mod_map(tiles_per_batch)),
                  _resident(w_out.shape, lambda i: (0, 0)),
                  pl.BlockSpec((d, LANES), lambda i: (0, 0))] + _part_specs(parts, tm),
        out_specs=(pl.BlockSpec((tm, d), lambda i: (i, 0)),
                   pl.BlockSpec((tm, d), lambda i: (i, 0)),
                   pl.BlockSpec((tm, LANES), lambda i: (i, 0)),
                   pl.BlockSpec((ROUTE_ROWS, tm), lambda i: (0, i)),
                   pl.BlockSpec((1, LANES), lambda i: (0, 0))),
        scratch_shapes=[pltpu.VMEM((1, LANES), F32)],
        compiler_params=_params("arbitrary"),
        name="moe_router",
    )(x2, g, mod, w_out, w_router, *parts)


def _moe_kernel(tile_ref, exp_ref, start_ref, end_ref, xg_ref, wgu_ref, wd_ref, y_ref, wgu_b, wd_b):
    w = pl.program_id(0)
    tm = xg_ref.shape[0]
    start, end = start_ref[w], end_ref[w]
    t0 = tile_ref[w] * tm

    @pl.when((w == 0) | (exp_ref[w] != exp_ref[jnp.maximum(w - 1, 0)]))
    def _():
        wgu_b[...] = wgu_ref[0].astype(BF16)
        wd_b[...] = wd_ref[0].astype(BF16)

    @pl.when(end > start)
    def _():
        y = _swiglu(xg_ref[...], lambda a, b: wgu_b[:, a:b], lambda a, b: wd_b[a:b, :], wd_b.shape[0])
        y = y.astype(y_ref.dtype)

        @pl.when(start == t0)
        def _():
            y_ref[...] = y

        @pl.when(start != t0)
        def _():
            row = t0 + lax.broadcasted_iota(jnp.int32, (tm, 1), 0)
            y_ref[...] = jnp.where(row >= start, y, y_ref[...])


def moe_grouped(item_tile, item_expert, item_start, item_end, xg, w_gu, w_down, tm):
    p, d = xg.shape
    grid_spec = pltpu.PrefetchScalarGridSpec(
        num_scalar_prefetch=4,
        grid=(item_tile.shape[0],),
        in_specs=[pl.BlockSpec((tm, d), lambda w, it, ie, s, e: (it[w], 0)),
                  _resident((1,) + w_gu.shape[1:], lambda w, it, ie, s, e: (ie[w], 0, 0)),
                  _resident((1,) + w_down.shape[1:], lambda w, it, ie, s, e: (ie[w], 0, 0))],
        out_specs=pl.BlockSpec((tm, d), lambda w, it, ie, s, e: (it[w], 0)),
        scratch_shapes=[pltpu.VMEM(w_gu.shape[1:], BF16), pltpu.VMEM(w_down.shape[1:], BF16)],
    )
    return pl.pallas_call(
        _moe_kernel,
        out_shape=jax.ShapeDtypeStruct((p, d), BF16),
        grid_spec=grid_spec,
        compiler_params=_params("arbitrary"),
        name="moe_grouped",
    )(item_tile, item_expert, item_start, item_end, xg, w_gu, w_down)


def _moe_combine_kernel(x_ref, ya_ref, yb_ref, route_ref, mod_ref, *refs):
    o_ref = refs[-1]
    r = route_ref[...]
    y = (r[:, ROUTE_G1:ROUTE_G1 + 1] * ya_ref[...].astype(F32) + r[:, ROUTE_G2:ROUTE_G2 + 1] * yb_ref[...].astype(F32))
    x = x_ref[...] + mod_ref[0, 5:6, :] * y
    if len(refs) == 2:
        ms = jnp.mean(x * x, axis=-1, keepdims=True)
        x = x * lax.rsqrt(ms + EPS) * refs[0][...]
    o_ref[...] = x


def moe_combine(x2, ya, yb, route, mod, tiles_per_batch, tm, g_final=None):
    m, d = x2.shape
    row = pl.BlockSpec((tm, d), lambda i: (i, 0))
    specs = [row, row, row, pl.BlockSpec((tm, LANES), lambda i: (i, 0)),
             pl.BlockSpec((1, 6, d), _mod_map(tiles_per_batch))]
    args = [x2, ya, yb, route, mod]
    if g_final is not None:
        specs.append(pl.BlockSpec((1, d), lambda i: (0, 0)))
        args.append(g_final)
    return pl.pallas_call(
        _moe_combine_kernel,
        out_shape=jax.ShapeDtypeStruct((m, d), F32),
        grid=(m // tm,),
        in_specs=specs,
        out_specs=row,
        compiler_params=_params("parallel"),
        name="moe_combine",
    )(*args)


def moe_block(x2, parts, g, mod, w_out, w_router, w_gu, w_down, tiles_per_batch, tm_tok, tm_grp, g_final=None):
    m, d = x2.shape
    x2, h, route, route_t, cnt = moe_router(x2, parts, g, mod, w_out, w_router, tiles_per_batch, tm_tok)
    cnt = cnt[0, :N_EXPERTS].astype(jnp.int32)
    p = m * TOP_K
    off = jnp.cumsum(cnt) - cnt
    field = lambda k: route_t[k].astype(jnp.int32)

    def row_of(e, r):
        for k in range(N_EXPERTS):
            r = r + jnp.where(e == k, off[k], 0)
        return r

    d1 = row_of(field(ROUTE_E1), field(ROUTE_R1))
    d2 = row_of(field(ROUTE_E2), field(ROUTE_R2))
    tok = jnp.arange(m, dtype=jnp.int32)
    _, src = lax.sort_key_val(jnp.concatenate([d1, d2]), jnp.concatenate([tok, tok]))
    n_row_tiles = p // tm_grp
    start = jnp.sort(jnp.concatenate([jnp.arange(n_row_tiles, dtype=jnp.int32) * tm_grp, off]))
    end = jnp.concatenate([start[1:], jnp.full((1,), p, jnp.int32)])
    item_tile = jnp.minimum(start // tm_grp, n_row_tiles - 1)
    item_expert = jnp.sum((off[None, :] <= start[:, None]).astype(jnp.int32), axis=1) - 1
    rows = lambda a, idx: a.at[idx].get(mode="promise_in_bounds")
    yg = moe_grouped(item_tile, item_expert, start, end, rows(h, src), w_gu, w_down, tm_grp)
    return moe_combine(x2, rows(yg, d1), rows(yg, d2), route, mod, tiles_per_batch, tm_tok, g_final)


def _final_norm_kernel(x_ref, g_ref, o_ref):
    x = x_ref[...]
    ms = jnp.mean(x * x, axis=-1, keepdims=True)
    o_ref[...] = x * lax.rsqrt(ms + EPS) * g_ref[...]


def final_norm(x2, g, tm):
    m, d = x2.shape
    return pl.pallas_call(
        _final_norm_kernel,
        out_shape=jax.ShapeDtypeStruct((m, d), F32),
        grid=(m // tm,),
        in_specs=[pl.BlockSpec((tm, d), lambda i: (i, 0)), pl.BlockSpec((1, d), lambda i: (0, 0))],
        out_specs=pl.BlockSpec((tm, d), lambda i: (i, 0)),
        compiler_params=_params("parallel"),
        name="final_norm",
    )(x2, g)


NA_QROWS = 4
NA_KROWS = NA_QROWS + WIN_R
Q_COL, K_COL, V_COL = 4, 5, 6


def _na_key_start(j, rows):
    return np.clip(j * NA_QROWS - WIN_R // 2, 0, rows - NA_KROWS)


def _na_bias_index(rows):
    nblk = rows // NA_QROWS
    pats = []
    for j in range(nblk):
        start = _na_key_start(j, rows)
        r = j * NA_QROWS + np.arange(NA_QROWS)
        sr = np.clip(r - WIN_R // 2, 0, rows - WIN_R)
        kr = start + np.arange(NA_KROWS)
        rvalid = (kr[None, :] >= sr[:, None]) & (kr[None, :] < sr[:, None] + WIN_R)
        ri = np.clip(kr[None, :] - r[:, None] + WIN_R - 1, 0, 2 * WIN_R - 2)
        pats.append((ri, rvalid))
    for j in range(2, nblk - 1):
        assert all(np.array_equal(a, b) for a, b in zip(pats[1], pats[j]))
    sel = [pats[0], pats[1], pats[nblk - 1]]
    ri = np.stack([p[0] for p in sel])
    rvalid = np.stack([p[1] for p in sel])
    c = np.arange(GRID_W)
    sc = np.clip(c - WIN_C // 2, 0, GRID_W - WIN_C)
    cvalid = (c[None, :] >= sc[:, None]) & (c[None, :] < sc[:, None] + WIN_C)
    ci = np.clip(c[None, :] - c[:, None] + WIN_C - 1, 0, 2 * WIN_C - 2)
    c_onehot = (ci[..., None] == np.arange(2 * WIN_C - 1)).astype(np.float32)
    valid = rvalid[:, :, None, :, None] & cvalid[None, None, :, None, :]
    return ri, c_onehot, valid


def na_bias_table(rpb, rows):
    ri, c_onehot, valid = _na_bias_index(rows)
    toep = jnp.einsum('lhrd,qkd->lhrqk', rpb, c_onehot, precision=lax.Precision.HIGHEST)
    b = jnp.take(toep, ri.reshape(-1), axis=2).reshape(rpb.shape[:2] + ri.shape + (GRID_W, GRID_W))
    b = jnp.transpose(b, (0, 2, 1, 3, 5, 4, 6))
    b = jnp.where(valid[:, None], b, -jnp.inf)
    return b.reshape(rpb.shape[0], 3, NA_HEADS, NA_QROWS * GRID_W, NA_KROWS * GRID_W).astype(F32)


def _attend_heads(q, key_sets, bias_fn):
    n, w = q.shape
    lane = lax.broadcasted_iota(jnp.int32, (1, w), 1)
    out = jnp.zeros((n, w), F32)
    for h in range(NA_HEADS):
        mh = (lane >= h * NA_HEAD_DIM) & (lane < (h + 1) * NA_HEAD_DIM)
        qh = jnp.where(mh, q, 0.0).astype(BF16)
        scores = []
        for i, (k, _) in enumerate(key_sets):
            s = lax.dot_general(qh, k, (((1,), (1,)), ((), ())), preferred_element_type=F32)
            b = bias_fn(i, h)
            scores.append(s if b is None else s + b)
        m = scores[0].max(axis=-1, keepdims=True)
        for s in scores[1:]:
            m = jnp.maximum(m, s.max(axis=-1, keepdims=True))
        denom = jnp.zeros((n, 1), F32)
        acc = jnp.zeros((n, w), F32)
        for s, (_, v) in zip(scores, key_sets):
            p = jnp.exp(s - m)
            denom = denom + p.sum(axis=-1, keepdims=True)
            acc = acc + jnp.dot(p.astype(BF16), v, preferred_element_type=F32)
        out = out + jnp.where(mh, acc / denom, 0.0)
    return out


def _na_kernel(q_ref, k_ref, v_ref, kc_ref, vc_ref, bias_ref, o_ref, *, rows):
    j = pl.program_id(1)
    start = jnp.clip(j * NA_QROWS - WIN_R // 2, 0, rows - NA_KROWS)
    t0 = pl.multiple_of(start * GRID_W, GRID_W)
    nk = NA_KROWS * GRID_W
    kw = k_ref[pl.ds(t0, nk), :]
    vw = v_ref[pl.ds(t0, nk), :]
    q = q_ref[...] * (1.0 / math.sqrt(NA_HEAD_DIM))
    o = _attend_heads(q, [(kw, vw), (kc_ref[...], vc_ref[...])], lambda i, h: bias_ref[0, h] if i == 0 else None)
    o_ref[...] = o.astype(o_ref.dtype)


def na_attention(pr, pr_c, bias, bn):
    S, Lc = pr.shape[0] // bn, pr_c.shape[0] // bn
    rows = S // GRID_W
    nblk = rows // NA_QROWS
    nq = NA_QROWS * GRID_W

    def pat(b, j):
        return (jnp.where(j == 0, 0, jnp.where(j == nblk - 1, 2, 1)), 0, 0, 0)

    return pl.pallas_call(
        functools.partial(_na_kernel, rows=rows),
        out_shape=jax.ShapeDtypeStruct((bn * S, W_NA), BF16),
        grid=(bn, nblk),
        in_specs=[pl.BlockSpec((nq, W_NA), lambda b, j: (b * nblk + j, Q_COL)),
                  pl.BlockSpec((S, W_NA), lambda b, j: (b, K_COL)),
                  pl.BlockSpec((S, W_NA), lambda b, j: (b, V_COL)),
                  pl.BlockSpec((Lc, W_NA), lambda b, j: (b, K_COL)),
                  pl.BlockSpec((Lc, W_NA), lambda b, j: (b, V_COL)),
                  pl.BlockSpec((1,) + bias.shape[1:], pat)],
        out_specs=pl.BlockSpec((nq, W_NA), lambda b, j: (b * nblk + j, 0)),
        compiler_params=_params("parallel", "arbitrary"),
        name="na_attention",
    )(pr, pr, pr, pr_c, pr_c, bias)


def _ctx_attn_kernel(q_ref, k_ref, v_ref, o_ref):
    q = q_ref[...] * (1.0 / math.sqrt(NA_HEAD_DIM))
    o = _attend_heads(q, [(k_ref[...], v_ref[...])], lambda i, h: None)
    o_ref[...] = o.astype(o_ref.dtype)


def ctx_attention(pr_c, bn):
    Lc = pr_c.shape[0] // bn
    return pl.pallas_call(
        _ctx_attn_kernel,
        out_shape=jax.ShapeDtypeStruct((bn * Lc, W_NA), BF16),
        grid=(bn,),
        in_specs=[pl.BlockSpec((Lc, W_NA), lambda b: (b, Q_COL)),
                  pl.BlockSpec((Lc, W_NA), lambda b: (b, K_COL)),
                  pl.BlockSpec((Lc, W_NA), lambda b: (b, V_COL))],
        out_specs=pl.BlockSpec((Lc, W_NA), lambda b: (b, 0)),
        compiler_params=_params("parallel"),
        name="ctx_attention",
    )(pr_c, pr_c, pr_c)


POOL_COL, CONV_H_COL, CONV_B_COL, CONV_C_COL = 0, 1, 2, 3


def _shift_rows(x, k, row):
    n = x.shape[0]
    y = pltpu.roll(x, k % n, 0)
    return jnp.where(row < k, 0.0, y) if k > 0 else jnp.where(row >= n + k, 0.0, y)


def _local_mix_kernel(u_ref, h_ref, bg_ref, cg_ref, pw_ref, ps_ref, cw_ref, op_ref, oc_ref, *, seq):
    row = lax.broadcasted_iota(jnp.int32, (seq, 1), 0)
    lane = lax.broadcasted_iota(jnp.int32, (1, W_POOL), 1)
    u = u_ref[...].astype(F32)
    trailing, leading = {1: u}, {1: u}
    for w in (1, 2, 4):
        trailing[2 * w] = trailing[w] + _shift_rows(trailing[w], w, row)
        leading[2 * w] = leading[w] + _shift_rows(leading[w], -w, row)
    rowf = row.astype(F32)
    win_sum = jnp.zeros_like(u)
    cnt = jnp.zeros_like(u)
    for g, win in enumerate(POOL_WINDOWS):
        half = win // 2
        mg = (lane >= g * POOL_GROUP) & (lane < (g + 1) * POOL_GROUP)
        s = _shift_rows(trailing[half], 1, row) + leading[half]
        n = jnp.minimum(rowf + half, float(seq)) - jnp.maximum(rowf - half, 0.0)
        win_sum = jnp.where(mg, s, win_sum)
        cnt = jnp.where(mg, n, cnt)
    p = win_sum / cnt - u
    pooled = jnp.dot(p.astype(BF16), pw_ref[...], preferred_element_type=F32) * ps_ref[...]
    op_ref[...] = pooled.astype(op_ref.dtype)
    v = cg_ref[...].astype(F32) * h_ref[...].astype(F32)
    conv = (_shift_rows(v, 1, row) * cw_ref[0:1, :] + v * cw_ref[1:2, :] + _shift_rows(v, -1, row) * cw_ref[2:3, :])
    oc_ref[...] = (bg_ref[...].astype(F32) * conv).astype(oc_ref.dtype)


def local_mix(pr, pool_w, pool_scale, conv_w, bn):
    seq = pr.shape[0] // bn
    pw = jax.scipy.linalg.block_diag(*[pool_w[g] for g in range(len(POOL_WINDOWS))]).astype(BF16)
    blk = lambda col: pl.BlockSpec((seq, W_POOL), lambda b: (b, col))
    full = lambda a: pl.BlockSpec(a.shape, lambda b: (0,) * a.ndim)
    args = (pw, pool_scale[None, :], conv_w)
    return pl.pallas_call(
        functools.partial(_local_mix_kernel, seq=seq),
        out_shape=(jax.ShapeDtypeStruct((bn * seq, W_POOL), BF16), jax.ShapeDtypeStruct((bn * seq, W_CONV), BF16)),
        grid=(bn,),
        in_specs=[blk(POOL_COL), blk(CONV_H_COL), blk(CONV_B_COL), blk(CONV_C_COL)] + [full(a) for a in args],
        out_specs=(pl.BlockSpec((seq, W_POOL), lambda b: (b, 0)),
                   pl.BlockSpec((seq, W_CONV), lambda b: (b, 0))),
        compiler_params=_params("parallel"),
        name="local_mix",
    )(pr, pr, pr, pr, *args)


Z_COL, XS_COL, BS_COL, CS_COL = 7, 8, 9, 10
DT_COL = 22
HEADS_PER_GROUP = SSD_HEADS // SSD_GROUPS
GROUP_W = HEADS_PER_GROUP * SSD_HEAD_DIM


def _head_mask(h):
    lane = lax.broadcasted_iota(jnp.int32, (1, W_SSD), 1)
    return (lane >= h * SSD_HEAD_DIM) & (lane < (h + 1) * SSD_HEAD_DIM)


def _expand_heads(v, d):
    out = jnp.zeros((v.shape[0], W_SSD), F32)
    for h in range(SSD_HEADS):
        k = d * SSD_HEADS + h
        out = jnp.where(_head_mask(h), v[:, k:k + 1], out)
    return out


def _ssd_chunk_terms(c, xs_ref, b_ref, c_ref, dt_ref, la_ref, dsk_ref, y_ref, upd_ref, dec_ref, need_y):
    T = SSD_CHUNK
    r0 = pl.multiple_of(c * T, T)
    xs = xs_ref[pl.ds(r0, T), :]
    bm = b_ref[pl.ds(r0, T), :]
    cm = c_ref[pl.ds(r0, T), :]
    dt = dt_ref[pl.ds(r0, T), :]
    la = la_ref[pl.ds(r0, T), :]
    li = lax.broadcasted_iota(jnp.int32, (T, T), 0)
    si = lax.broadcasted_iota(jnp.int32, (T, T), 1)
    causal = si <= li
    prefix = jnp.dot(causal.astype(F32), la, precision=lax.Precision.HIGHEST, preferred_element_type=F32)
    total = prefix[T - 1:T, :]
    lane = lax.broadcasted_iota(jnp.int32, (1, LANES), 1)
    acum = jnp.where(lane < SSD_HEADS, prefix, total - prefix + la)
    bt = bm.astype(F32).T.astype(BF16)
    if need_y:
        acum_t = acum.T
        cb = [lax.dot_general(cm[:, g * SSD_STATE:(g + 1) * SSD_STATE], bm[:, g * SSD_STATE:(g + 1) * SSD_STATE],
                              (((1,), (1,)), ((), ())), preferred_element_type=F32) for g in range(SSD_GROUPS)]
        y = jnp.zeros((T, W_SSD), F32)
    for d in range(2):
        acum_e = _expand_heads(acum, d)
        tot_e = _expand_heads(total, d)
        xdt = xs * _expand_heads(dt, d)
        xw = (xdt * jnp.exp(tot_e - acum_e)).astype(BF16)
        upd = [jnp.dot(bt[g * SSD_STATE:(g + 1) * SSD_STATE, :], xw[:, g * GROUP_W:(g + 1) * GROUP_W],
                       preferred_element_type=F32) for g in range(SSD_GROUPS)]
        upd_ref[c, d] = jnp.concatenate(upd, axis=1)
        dec_ref[c, d, 0:T, :] = jnp.exp(acum_e)
        dec_ref[c, d, T:T + 1, :] = jnp.exp(tot_e)
        if need_y:
            mask = causal if d == 0 else (si >= li)
            xdt_b = xdt.astype(BF16)
            for h in range(SSD_HEADS):
                k = d * SSD_HEADS + h
                decay = jnp.exp(jnp.where(mask, acum[:, k:k + 1] - acum_t[k:k + 1, :], -jnp.inf))
                scores = (cb[h // HEADS_PER_GROUP] * decay).astype(BF16)
                y = y + jnp.where(_head_mask(h), jnp.dot(scores, xdt_b, preferred_element_type=F32), 0.0)
    if need_y:
        y_ref[pl.ds(r0, T), :] = xs * dsk_ref[...] + y


def _ssd_chunk_state(c, d, c_ref, y_ref, upd_ref, dec_ref, st_ref, need_y):
    T = SSD_CHUNK
    r0 = pl.multiple_of(c * T, T)
    st = st_ref[d]
    if need_y:
        cm = c_ref[pl.ds(r0, T), :]
        st_b = st.astype(BF16)
        ys = [jnp.dot(cm[:, g * SSD_STATE:(g + 1) * SSD_STATE], st_b[:, g * GROUP_W:(g + 1) * GROUP_W],
                      preferred_element_type=F32) for g in range(SSD_GROUPS)]
        y_ref[pl.ds(r0, T), :] += jnp.concatenate(ys, axis=1) * dec_ref[c, d, 0:T, :]
    st_ref[d] = dec_ref[c, d, T:T + 1, :] * st + upd_ref[c, d]


def _ssd_kernel(z_ref, xs_in, bs_in, cs_in, dtr_ref, cw_ref, cb_ref, dtb_ref, alog_ref, dsk_ref, ng_ref, h0_ref,
                *refs, seq, need_y):
    if need_y:
        o_ref, hT_ref, xs_ref, b_ref, c_ref, dt_ref, la_ref, upd_ref, dec_ref, st_ref, y_ref = refs
    else:
        hT_ref, xs_ref, b_ref, c_ref, dt_ref, la_ref, upd_ref, dec_ref, st_ref = refs
        y_ref = None
    nc = seq // SSD_CHUNK
    row = lax.broadcasted_iota(jnp.int32, (seq, 1), 0)
    for gi, (src, dst) in enumerate(((xs_in, xs_ref), (bs_in, b_ref), (cs_in, c_ref))):
        sl = slice(gi * W_SSD, (gi + 1) * W_SSD)
        x = src[...].astype(F32)
        cv = (_shift_rows(x, 1, row) * cw_ref[0:1, sl] + x * cw_ref[1:2, sl]
              + _shift_rows(x, -1, row) * cw_ref[2:3, sl] + cb_ref[:, sl])
        dst[...] = (cv * jax.nn.sigmoid(cv)).astype(dst.dtype)
    dtv = dtr_ref[...] + dtb_ref[...]
    dt = jnp.maximum(dtv, 0.0) + jnp.log1p(jnp.exp(-jnp.abs(dtv)))
    dt_ref[...] = dt
    la_ref[...] = dt * (-jnp.exp(alog_ref[...]))
    st_ref[...] = h0_ref[0]

    def terms(c, carry):
        _ssd_chunk_terms(c, xs_ref, b_ref, c_ref, dt_ref, la_ref, dsk_ref, y_ref, upd_ref, dec_ref, need_y)
        return carry

    lax.fori_loop(0, nc, terms, 0, unroll=2)

    def recur(i, carry):
        _ssd_chunk_state(i, 0, c_ref, y_ref, upd_ref, dec_ref, st_ref, need_y)
        _ssd_chunk_state(nc - 1 - i, 1, c_ref, y_ref, upd_ref, dec_ref, st_ref, need_y)
        return carry

    lax.fori_loop(0, nc, recur, 0, unroll=2)
    hT_ref[0] = st_ref[...]
    if need_y:
        z = z_ref[...].astype(F32)
        yz = y_ref[...] * (z * jax.nn.sigmoid(z))
        ms = jnp.mean(yz * yz, axis=-1, keepdims=True)
        o_ref[...] = (yz * lax.rsqrt(ms + EPS) * ng_ref[...]).astype(o_ref.dtype)


def ssd_scan(pr, dt, h0, conv_w, conv_b, dt_bias, a_log, d_skip, norm_g, need_y):
    bn = h0.shape[0]
    seq = pr.shape[0] // bn
    nc = seq // SSD_CHUNK
    pad = LANES - 2 * SSD_HEADS
    dtb = jnp.pad(dt_bias.reshape(1, -1), ((0, 0), (0, pad)))
    alog = jnp.pad(a_log.reshape(1, -1), ((0, 0), (0, pad)))
    dsk = jnp.repeat(d_skip, SSD_HEAD_DIM)[None, :]
    blk = lambda col: pl.BlockSpec((seq, W_SSD), lambda b: (b, col))
    full = lambda a: pl.BlockSpec(a.shape, lambda b: (0,) * a.ndim)
    st_shape = (1, 2, SSD_STATE, W_SSD)
    st_spec = pl.BlockSpec(st_shape, lambda b: (b, 0, 0, 0))
    args = (conv_w, conv_b[None, :], dtb, alog, dsk, norm_g[None, :])
    out_shape = [jax.ShapeDtypeStruct((bn,) + st_shape[1:], F32)]
    out_specs = [st_spec]
    scratch = [pltpu.VMEM((seq, W_SSD), F32),
               pltpu.VMEM((seq, W_SSD), BF16), pltpu.VMEM((seq, W_SSD), BF16),
               pltpu.VMEM((seq, LANES), F32), pltpu.VMEM((seq, LANES), F32),
               pltpu.VMEM((nc, 2, SSD_STATE, W_SSD), F32),
               pltpu.VMEM((nc, 2, SSD_CHUNK + 8, W_SSD), F32),
               pltpu.VMEM(st_shape[1:], F32)]
    if need_y:
        out_shape.insert(0, jax.ShapeDtypeStruct((bn * seq, W_SSD), BF16))
        out_specs.insert(0, pl.BlockSpec((seq, W_SSD), lambda b: (b, 0)))
        scratch.append(pltpu.VMEM((seq, W_SSD), F32))
    res = pl.pallas_call(
        functools.partial(_ssd_kernel, seq=seq, need_y=need_y),
        out_shape=out_shape,
        grid=(bn,),
        in_specs=[blk(Z_COL), blk(XS_COL), blk(BS_COL), blk(CS_COL),
                  pl.BlockSpec((seq, LANES), lambda b: (b, 0))]
                 + [full(a) for a in args] + [st_spec],
        out_specs=out_specs,
        scratch_shapes=scratch,
        compiler_params=_params("parallel"),
        name="ssd_scan",
    )(pr, pr, pr, pr, dt, *args, h0)
    return (res[0], res[1]) if need_y else (None, res[0])


def ssd_mix(pr, dt, pr_c, dt_c, conv_w, conv_b, dt_bias, a_log, d_skip, norm_g, ctx_out, bn):
    h0 = jnp.zeros((bn, 2, SSD_STATE, W_SSD), F32)
    oc, hc = ssd_scan(pr_c, dt_c, h0, conv_w, conv_b, dt_bias, a_log, d_skip, norm_g, ctx_out)
    o, _ = ssd_scan(pr, dt, hc, conv_w, conv_b, dt_bias, a_log, d_skip, norm_g, True)
    return o, oc


def mixer(pr, dt, pr_c, dt_c, pool_w, pool_scale, conv_w, na_bias, s_conv_w, s_conv_b, dt_bias, a_log, d_skip,
          s_norm_g, ctx_out, bn):
    o_ssd, oc_ssd = ssd_mix(pr, dt, pr_c, dt_c, s_conv_w, s_conv_b, dt_bias, a_log, d_skip, s_norm_g, ctx_out, bn)
    o = [*local_mix(pr, pool_w, pool_scale, conv_w, bn), na_attention(pr, pr_c, na_bias, bn), o_ssd]
    if not ctx_out:
        return o, None
    oc = [*local_mix(pr_c, pool_w, pool_scale, conv_w, bn), ctx_attention(pr_c, bn), oc_ssd]
    return o, oc


TM = 512
TM_PROJ = 1024
TM_GRP = 512


def kernel(x, c, ctx, c_ctx, w_ada, b_ada, g_mix, g_ffn, w_in, w_out, pool_w, pool_scale, conv_w, na_rpb,
           ssd_conv_w, ssd_conv_b, ssd_dt_bias, ssd_a_log, ssd_d, ssd_norm_g, ffn_w_gu, ffn_w_down,
           moe_router, moe_w_gu, moe_w_down, g_final):
    bn, S, d = x.shape
    Lc = ctx.shape[1]
    m, mc = bn * S, bn * Lc
    tpb = S // TM
    x2 = x.reshape(m, d)
    xc2 = ctx.reshape(mc, d)
    c_rows = jnp.concatenate([c, c_ctx[None, :], jnp.zeros((7, d), F32)], axis=0)
    na_bias = na_bias_table(na_rpb, S // GRID_W)
    for l in range(DEPTH):
        ctx_out = l < DEPTH - 1
        last = l == DEPTH - 1
        ada = ada_modulation(c_rows, w_ada[l].astype(BF16), b_ada[l][None, :])
        mod = ada[:bn].reshape(bn, 6, d)
        mod_c = ada[bn:bn + 1].reshape(1, 6, d)
        w_in_l = jnp.pad(w_in[l], ((0, 0), (0, D_IN_PAD - D_IN_PROJ))).astype(BF16)
        g_m = g_mix[l][None, :]
        g_f = g_ffn[l][None, :]
        pr, dt = in_proj(x2, g_m, mod, w_in_l, S // TM_PROJ, TM_PROJ)
        pr_c, dt_c = in_proj(xc2, g_m, mod_c, w_in_l, None, min(TM_PROJ, mc))
        o, oc = mixer(pr, dt, pr_c, dt_c, pool_w[l], pool_scale[l], conv_w[l], na_bias[l], ssd_conv_w[l],
                      ssd_conv_b[l], ssd_dt_bias[l], ssd_a_log[l], ssd_d[l], ssd_norm_g[l], ctx_out, bn)
        w_out_l = w_out[l].astype(BF16)
        j = l // 2
        if l % 2 == 0:
            w_gu = ffn_w_gu[j].astype(BF16)
            w_dn = ffn_w_down[j].astype(BF16)
            x2 = ffn_dense(x2, o, g_f, mod, w_out_l, w_gu, w_dn, tpb, TM)
            if ctx_out:
                xc2 = ffn_dense(xc2, oc, g_f, mod_c, w_out_l, w_gu, w_dn, None, min(TM, mc))
        else:
            w_r = jnp.pad(moe_router[j], ((0, 0), (0, LANES - N_EXPERTS))).astype(BF16)
            w_gu, w_dn = moe_w_gu[j], moe_w_down[j]
            x2 = moe_block(x2, o, g_f, mod, w_out_l, w_r, w_gu, w_dn, tpb, TM, TM_GRP,
                           g_final[None, :] if last else None)
            if ctx_out:
                xc2 = moe_block(xc2, oc, g_f, mod_c, w_out_l, w_r, w_gu, w_dn, None, min(TM, mc), TM_GRP)
            if last:
                return x2.reshape(bn, S, d)
    return final_norm(x2, g_final[None, :], TM).reshape(bn, S, d)
```

```python
import functools
import math

import numpy as np
import jax
import jax.numpy as jnp
from jax import lax
from jax.experimental import pallas as pl
from jax.experimental.pallas import tpu as pltpu

DEPTH = 2
GRID_W = 64
EPS = 1e-6
W_POOL = 256
W_CONV = 256
W_NA = 256
W_SSD = 256
POOL_WINDOWS = (2, 4, 8, 16)
POOL_GROUP = W_POOL // len(POOL_WINDOWS)
NA_HEADS = 4
NA_HEAD_DIM = W_NA // NA_HEADS
WIN_R = 8
WIN_C = 16
SSD_HEAD_DIM = 64
SSD_HEADS = W_SSD // SSD_HEAD_DIM
SSD_GROUPS = 2
SSD_STATE = 128
SSD_CHUNK = 128
SSD_XBC = W_SSD + 2 * SSD_GROUPS * SSD_STATE
D_IN_PROJ = W_POOL + 3 * W_CONV + 3 * W_NA + W_SSD + SSD_XBC + 2 * SSD_HEADS
N_EXPERTS = 8
TOP_K = 2

LANES = 128
D_IN_PAD = -(-D_IN_PROJ // LANES) * LANES
VMEM_LIMIT = 56 * 1024 * 1024
BF16 = jnp.bfloat16
F32 = jnp.float32


def _params(*sem):
    return pltpu.CompilerParams(dimension_semantics=sem, vmem_limit_bytes=VMEM_LIMIT)


def _norm_mod(x, g, scale, shift):
    ms = jnp.mean(x * x, axis=-1, keepdims=True)
    return (x * lax.rsqrt(ms + EPS) * g) * (1.0 + scale) + shift


def _mod_map(tiles_per_batch):
    if tiles_per_batch is None:
        return lambda i, *_: (0, 0, 0)
    return lambda i, *_: (i // tiles_per_batch, 0, 0)


def _resident(shape, index_map):
    return pl.BlockSpec(shape, index_map, pipeline_mode=pl.Buffered(1))


def _ada_kernel(c_ref, w_ref, b_ref, o_ref):
    c = c_ref[...]
    s = c * jax.nn.sigmoid(c)
    o_ref[...] = jnp.dot(s.astype(BF16), w_ref[...], preferred_element_type=F32) + b_ref[...]


def ada_modulation(c_rows, w, b):
    r, d = c_rows.shape
    n = w.shape[1]
    tn = 1536
    return pl.pallas_call(
        _ada_kernel,
        out_shape=jax.ShapeDtypeStruct((r, n), F32),
        grid=(n // tn,),
        in_specs=[pl.BlockSpec((r, d), lambda j: (0, 0)),
                  pl.BlockSpec((d, tn), lambda j: (0, j)),
                  pl.BlockSpec((1, tn), lambda j: (0, j))],
        out_specs=pl.BlockSpec((r, tn), lambda j: (0, j)),
        compiler_params=_params("arbitrary"),
        name="ada_modulation",
    )(c_rows, w, b)


def _in_proj_kernel(x_ref, g_ref, mod_ref, w_ref, o_ref, dt_ref):
    h = _norm_mod(x_ref[...], g_ref[...], mod_ref[0, 1:2, :], mod_ref[0, 0:1, :])
    pr = jnp.dot(h.astype(BF16), w_ref[...], preferred_element_type=F32)
    o_ref[...] = pr.astype(o_ref.dtype)
    dt_ref[...] = pr[:, DT_COL * LANES:(DT_COL + 1) * LANES]


def in_proj(x2, g, mod, w, tiles_per_batch, tm):
    m, d = x2.shape
    n = w.shape[1]
    return pl.pallas_call(
        _in_proj_kernel,
        out_shape=(jax.ShapeDtypeStruct((m, n), BF16), jax.ShapeDtypeStruct((m, LANES), F32)),
        grid=(m // tm,),
        in_specs=[pl.BlockSpec((tm, d), lambda i: (i, 0)),
                  pl.BlockSpec((1, d), lambda i: (0, 0)),
                  pl.BlockSpec((1, 6, d), _mod_map(tiles_per_batch)),
                  _resident((d, n), lambda i: (0, 0))],
        out_specs=(pl.BlockSpec((tm, n), lambda i: (i, 0)), pl.BlockSpec((tm, LANES), lambda i: (i, 0))),
        compiler_params=_params("parallel"),
        name="in_proj",
    )(x2, g, mod, w)


def _mix_residual(x, mod_ref, w_ref, part_refs):
    y, k0 = None, 0
    for p_ref in part_refs:
        k = p_ref.shape[1]
        t = jnp.dot(p_ref[...], w_ref[k0:k0 + k, :], preferred_element_type=F32)
        y = t if y is None else y + t
        k0 += k
    return x + mod_ref[0, 2:3, :] * y


def _part_specs(parts, tm):
    return [pl.BlockSpec((tm, p.shape[1]), lambda i, *_: (i, 0)) for p in parts]


FF_CHUNK = 512


def _swiglu(h, wgu, wd, ff):
    acc = None
    for c0 in range(0, ff, FF_CHUNK):
        c1 = min(c0 + FF_CHUNK, ff)
        gg = jnp.dot(h, wgu(c0, c1), preferred_element_type=F32)
        uu = jnp.dot(h, wgu(ff + c0, ff + c1), preferred_element_type=F32)
        a = (gg * jax.nn.sigmoid(gg) * uu).astype(BF16)
        part = jnp.dot(a, wd(c0, c1), preferred_element_type=F32)
        acc = part if acc is None else acc + part
    return acc


def _ffn_kernel(x_ref, g_ref, mod_ref, wout_ref, wgu_ref, wd_ref, *refs):
    x = _mix_residual(x_ref[...], mod_ref, wout_ref, refs[:-1])
    y_ref = refs[-1]
    h = _norm_mod(x, g_ref[...], mod_ref[0, 4:5, :], mod_ref[0, 3:4, :]).astype(BF16)
    y = _swiglu(h, lambda a, b: wgu_ref[:, a:b], lambda a, b: wd_ref[a:b, :], wd_ref.shape[0])
    y_ref[...] = x + mod_ref[0, 5:6, :] * y


def ffn_dense(x2, parts, g, mod, w_out, w_gu, w_down, tiles_per_batch, tm):
    m, d = x2.shape
    return pl.pallas_call(
        _ffn_kernel,
        out_shape=jax.ShapeDtypeStruct((m, d), F32),
        grid=(m // tm,),
        in_specs=[pl.BlockSpec((tm, d), lambda i: (i, 0)),
                  pl.BlockSpec((1, d), lambda i: (0, 0)),
                  pl.BlockSpec((1, 6, d), _mod_map(tiles_per_batch)),
                  _resident(w_out.shape, lambda i: (0, 0)),
                  _resident(w_gu.shape, lambda i: (0, 0)),
                  _resident(w_down.shape, lambda i: (0, 0))] + _part_specs(parts, tm),
        out_specs=pl.BlockSpec((tm, d), lambda i: (i, 0)),
        compiler_params=_params("parallel"),
        name="ffn_dense",
    )(x2, g, mod, w_out, w_gu, w_down, *parts)


ROUTE_E1, ROUTE_E2, ROUTE_R1, ROUTE_R2, ROUTE_G1, ROUTE_G2 = range(6)
ROUTE_ROWS = 8


def _router_kernel(x_ref, g_ref, mod_ref, wout_ref, wr_ref, *refs):
    part_refs = refs[:-6]
    x1_ref, h_ref, route_ref, route_t_ref, cnt_ref, base_ref = refs[-6:]

    @pl.when(pl.program_id(0) == 0)
    def _():
        base_ref[...] = jnp.zeros_like(base_ref)

    x = _mix_residual(x_ref[...], mod_ref, wout_ref, part_refs)
    x1_ref[...] = x
    h = _norm_mod(x, g_ref[...], mod_ref[0, 4:5, :], mod_ref[0, 3:4, :]).astype(BF16)
    h_ref[...] = h
    tm = h.shape[0]
    logits = jnp.dot(h, wr_ref[...], preferred_element_type=F32)
    lane = lax.broadcasted_iota(jnp.int32, logits.shape, 1)
    neg = jnp.float32(-jnp.inf)
    logits = jnp.where(lane < N_EXPERTS, logits, neg)
    m1 = jnp.max(logits, axis=-1, keepdims=True)
    i1 = jnp.min(jnp.where(logits == m1, lane, LANES), axis=-1, keepdims=True)
    rest = jnp.where(lane == i1, neg, logits)
    m2 = jnp.max(rest, axis=-1, keepdims=True)
    i2 = jnp.min(jnp.where(rest == m2, lane, LANES), axis=-1, keepdims=True)
    e2 = jnp.exp(m2 - m1)
    g1 = 1.0 / (1.0 + e2)
    g2 = e2 / (1.0 + e2)
    sel1, sel2 = lane == i1, lane == i2
    sel = jnp.where(sel1 | sel2, 1.0, 0.0)
    li = lax.broadcasted_iota(jnp.int32, (tm, tm), 0)
    si = lax.broadcasted_iota(jnp.int32, (tm, tm), 1)
    before = jnp.where(si < li, 1.0, 0.0).astype(BF16)
    rank = jnp.dot(before, sel.astype(BF16), preferred_element_type=F32) + base_ref[...]
    r1 = jnp.sum(jnp.where(sel1, rank, 0.0), axis=-1, keepdims=True)
    r2 = jnp.sum(jnp.where(sel2, rank, 0.0), axis=-1, keepdims=True)
    base_ref[...] += jnp.sum(sel, axis=0, keepdims=True)
    cnt_ref[...] = base_ref[...]
    fields = (i1.astype(F32), i2.astype(F32), r1, r2, g1, g2)
    route = jnp.zeros(logits.shape, F32)
    for k, v in enumerate(fields):
        route = jnp.where(lane == k, v, route)
    route_ref[...] = route
    route_t_ref[...] = route.T[:ROUTE_ROWS, :]


def moe_router(x2, parts, g, mod, w_out, w_router, tiles_per_batch, tm):
    m, d = x2.shape
    return pl.pallas_call(
        _router_kernel,
        out_shape=(jax.ShapeDtypeStruct((m, d), F32), jax.ShapeDtypeStruct((m, d), BF16),
                   jax.ShapeDtypeStruct((m, LANES), F32), jax.ShapeDtypeStruct((ROUTE_ROWS, m), F32),
                   jax.ShapeDtypeStruct((1, LANES), F32)),
        grid=(m // tm,),
        in_specs=[pl.BlockSpec((tm, d), lambda i: (i, 0)),
                  pl.BlockSpec((1, d), lambda i: (0, 0)),
                  pl.BlockSpec((1, 6, d), _mod_map(tiles_per_batch)),
                  _resident(w_out.shape, lambda i: (0, 0)),
                  pl.BlockSpec((d, LANES), lambda i: (0, 0))] + _part_specs(parts, tm),
        out_specs=(pl.BlockSpec((tm, d), lambda i: (i, 0)),
                   pl.BlockSpec((tm, d), lambda i: (i, 0)),
                   pl.BlockSpec((tm, LANES), lambda i: (i, 0)),
                   pl.BlockSpec((ROUTE_ROWS, tm), lambda i: (0, i)),
                   pl.BlockSpec((1, LANES), lambda i: (0, 0))),
        scratch_shapes=[pltpu.VMEM((1, LANES), F32)],
        compiler_params=_params("arbitrary"),
        name="moe_router",
    )(x2, g, mod, w_out, w_router, *parts)


def _moe_kernel(tile_ref, exp_ref, start_ref, end_ref, xg_ref, wgu_ref, wd_ref, y_ref, wgu_b, wd_b):
    w = pl.program_id(0)
    tm = xg_ref.shape[0]
    start, end = start_ref[w], end_ref[w]
    t0 = tile_ref[w] * tm

    @pl.when((w == 0) | (exp_ref[w] != exp_ref[jnp.maximum(w - 1, 0)]))
    def _():
        wgu_b[...] = wgu_ref[0].astype(BF16)
        wd_b[...] = wd_ref[0].astype(BF16)

    @pl.when(end > start)
    def _():
        y = _swiglu(xg_ref[...], lambda a, b: wgu_b[:, a:b], lambda a, b: wd_b[a:b, :], wd_b.shape[0])
        y = y.astype(y_ref.dtype)

        @pl.when(start == t0)
        def _():
            y_ref[...] = y

        @pl.when(start != t0)
        def _():
            row = t0 + lax.broadcasted_iota(jnp.int32, (tm, 1), 0)
            y_ref[...] = jnp.where(row >= start, y, y_ref[...])


def moe_grouped(item_tile, item_expert, item_start, item_end, xg, w_gu, w_down, tm):
    p, d = xg.shape
    grid_spec = pltpu.PrefetchScalarGridSpec(
        num_scalar_prefetch=4,
        grid=(item_tile.shape[0],),
        in_specs=[pl.BlockSpec((tm, d), lambda w, it, ie, s, e: (it[w], 0)),
                  _resident((1,) + w_gu.shape[1:], lambda w, it, ie, s, e: (ie[w], 0, 0)),
                  _resident((1,) + w_down.shape[1:], lambda w, it, ie, s, e: (ie[w], 0, 0))],
        out_specs=pl.BlockSpec((tm, d), lambda w, it, ie, s, e: (it[w], 0)),
        scratch_shapes=[pltpu.VMEM(w_gu.shape[1:], BF16), pltpu.VMEM(w_down.shape[1:], BF16)],
    )
    return pl.pallas_call(
        _moe_kernel,
        out_shape=jax.ShapeDtypeStruct((p, d), BF16),
        grid_spec=grid_spec,
        compiler_params=_params("arbitrary"),
        name="moe_grouped",
    )(item_tile, item_expert, item_start, item_end, xg, w_gu, w_down)


def _moe_combine_kernel(x_ref, ya_ref, yb_ref, route_ref, mod_ref, *refs):
    o_ref = refs[-1]
    r = route_ref[...]
    y = (r[:, ROUTE_G1:ROUTE_G1 + 1] * ya_ref[...].astype(F32) + r[:, ROUTE_G2:ROUTE_G2 + 1] * yb_ref[...].astype(F32))
    x = x_ref[...] + mod_ref[0, 5:6, :] * y
    if len(refs) == 2:
        ms = jnp.mean(x * x, axis=-1, keepdims=True)
        x = x * lax.rsqrt(ms + EPS) * refs[0][...]
    o_ref[...] = x


def moe_combine(x2, ya, yb, route, mod, tiles_per_batch, tm, g_final=None):
    m, d = x2.shape
    row = pl.BlockSpec((tm, d), lambda i: (i, 0))
    specs = [row, row, row, pl.BlockSpec((tm, LANES), lambda i: (i, 0)),
             pl.BlockSpec((1, 6, d), _mod_map(tiles_per_batch))]
    args = [x2, ya, yb, route, mod]
    if g_final is not None:
        specs.append(pl.BlockSpec((1, d), lambda i: (0, 0)))
        args.append(g_final)
    return pl.pallas_call(
        _moe_combine_kernel,
        out_shape=jax.ShapeDtypeStruct((m, d), F32),
        grid=(m // tm,),
        in_specs=specs,
        out_specs=row,
        compiler_params=_params("parallel"),
        name="moe_combine",
    )(*args)


def moe_block(x2, parts, g, mod, w_out, w_router, w_gu, w_down, tiles_per_batch, tm_tok, tm_grp, g_final=None):
    m, d = x2.shape
    x2, h, route, route_t, cnt = moe_router(x2, parts, g, mod, w_out, w_router, tiles_per_batch, tm_tok)
    cnt = cnt[0, :N_EXPERTS].astype(jnp.int32)
    p = m * TOP_K
    off = jnp.cumsum(cnt) - cnt
    field = lambda k: route_t[k].astype(jnp.int32)

    def row_of(e, r):
        for k in range(N_EXPERTS):
            r = r + jnp.where(e == k, off[k], 0)
        return r

    d1 = row_of(field(ROUTE_E1), field(ROUTE_R1))
    d2 = row_of(field(ROUTE_E2), field(ROUTE_R2))
    tok = jnp.arange(m, dtype=jnp.int32)
    _, src = lax.sort_key_val(jnp.concatenate([d1, d2]), jnp.concatenate([tok, tok]))
    n_row_tiles = p // tm_grp
    start = jnp.sort(jnp.concatenate([jnp.arange(n_row_tiles, dtype=jnp.int32) * tm_grp, off]))
    end = jnp.concatenate([start[1:], jnp.full((1,), p, jnp.int32)])
    item_tile = jnp.minimum(start // tm_grp, n_row_tiles - 1)
    item_expert = jnp.sum((off[None, :] <= start[:, None]).astype(jnp.int32), axis=1) - 1
    rows = lambda a, idx: a.at[idx].get(mode="promise_in_bounds")
    yg = moe_grouped(item_tile, item_expert, start, end, rows(h, src), w_gu, w_down, tm_grp)
    return moe_combine(x2, rows(yg, d1), rows(yg, d2), route, mod, tiles_per_batch, tm_tok, g_final)


def _final_norm_kernel(x_ref, g_ref, o_ref):
    x = x_ref[...]
    ms = jnp.mean(x * x, axis=-1, keepdims=True)
    o_ref[...] = x * lax.rsqrt(ms + EPS) * g_ref[...]


def final_norm(x2, g, tm):
    m, d = x2.shape
    return pl.pallas_call(
        _final_norm_kernel,
        out_shape=jax.ShapeDtypeStruct((m, d), F32),
        grid=(m // tm,),
        in_specs=[pl.BlockSpec((tm, d), lambda i: (i, 0)), pl.BlockSpec((1, d), lambda i: (0, 0))],
        out_specs=pl.BlockSpec((tm, d), lambda i: (i, 0)),
        compiler_params=_params("parallel"),
        name="final_norm",
    )(x2, g)


NA_QROWS = 4
NA_KROWS = NA_QROWS + WIN_R
Q_COL, K_COL, V_COL = 4, 5, 6


def _na_key_start(j, rows):
    return np.clip(j * NA_QROWS - WIN_R // 2, 0, rows - NA_KROWS)


def _na_bias_index(rows):
    nblk = rows // NA_QROWS
    pats = []
    for j in range(nblk):
        start = _na_key_start(j, rows)
        r = j * NA_QROWS + np.arange(NA_QROWS)
        sr = np.clip(r - WIN_R // 2, 0, rows - WIN_R)
        kr = start + np.arange(NA_KROWS)
        rvalid = (kr[None, :] >= sr[:, None]) & (kr[None, :] < sr[:, None] + WIN_R)
        ri = np.clip(kr[None, :] - r[:, None] + WIN_R - 1, 0, 2 * WIN_R - 2)
        pats.append((ri, rvalid))
    for j in range(2, nblk - 1):
        assert all(np.array_equal(a, b) for a, b in zip(pats[1], pats[j]))
    sel = [pats[0], pats[1], pats[nblk - 1]]
    ri = np.stack([p[0] for p in sel])
    rvalid = np.stack([p[1] for p in sel])
    c = np.arange(GRID_W)
    sc = np.clip(c - WIN_C // 2, 0, GRID_W - WIN_C)
    cvalid = (c[None, :] >= sc[:, None]) & (c[None, :] < sc[:, None] + WIN_C)
    ci = np.clip(c[None, :] - c[:, None] + WIN_C - 1, 0, 2 * WIN_C - 2)
    c_onehot = (ci[..., None] == np.arange(2 * WIN_C - 1)).astype(np.float32)
    valid = rvalid[:, :, None, :, None] & cvalid[None, None, :, None, :]
    return ri, c_onehot, valid


def na_bias_table(rpb, rows):
    ri, c_onehot, valid = _na_bias_index(rows)
    toep = jnp.einsum('hrd,qkd->hrqk', rpb, c_onehot, precision=lax.Precision.HIGHEST)
    b = jnp.take(toep, ri.reshape(-1), axis=1).reshape((NA_HEADS,) + ri.shape + (GRID_W, GRID_W))
    b = jnp.transpose(b, (1, 0, 2, 4, 3, 5))
    b = jnp.where(valid[:, None], b, -jnp.inf)
    return b.reshape(3, NA_HEADS, NA_QROWS * GRID_W, NA_KROWS * GRID_W).astype(F32)


def _attend_heads(q, key_sets, bias_fn):
    n, w = q.shape
    lane = lax.broadcasted_iota(jnp.int32, (1, w), 1)
    out = jnp.zeros((n, w), F32)
    for h in range(NA_HEADS):
        mh = (lane >= h * NA_HEAD_DIM) & (lane < (h + 1) * NA_HEAD_DIM)
        qh = jnp.where(mh, q, 0.0).astype(BF16)
        scores = []
        for i, (k, _) in enumerate(key_sets):
            s = lax.dot_general(qh, k, (((1,), (1,)), ((), ())), preferred_element_type=F32)
            b = bias_fn(i, h)
            scores.append(s if b is None else s + b)
        m = scores[0].max(axis=-1, keepdims=True)
        for s in scores[1:]:
            m = jnp.maximum(m, s.max(axis=-1, keepdims=True))
        denom = jnp.zeros((n, 1), F32)
        acc = jnp.zeros((n, w), F32)
        for s, (_, v) in zip(scores, key_sets):
            p = jnp.exp(s - m)
            denom = denom + p.sum(axis=-1, keepdims=True)
            acc = acc + jnp.dot(p.astype(BF16), v, preferred_element_type=F32)
        out = out + jnp.where(mh, acc / denom, 0.0)
    return out


def _na_kernel(q_ref, k_ref, v_ref, kc_ref, vc_ref, bias_ref, o_ref, *, rows):
    j = pl.program_id(1)
    start = jnp.clip(j * NA_QROWS - WIN_R // 2, 0, rows - NA_KROWS)
    t0 = pl.multiple_of(start * GRID_W, GRID_W)
    nk = NA_KROWS * GRID_W
    kw = k_ref[pl.ds(t0, nk), :]
    vw = v_ref[pl.ds(t0, nk), :]
    q = q_ref[...] * (1.0 / math.sqrt(NA_HEAD_DIM))
    o = _attend_heads(q, [(kw, vw), (kc_ref[...], vc_ref[...])], lambda i, h: bias_ref[0, h] if i == 0 else None)
    o_ref[...] = o.astype(o_ref.dtype)


def na_attention(pr, pr_c, bias, bn):
    S, Lc = pr.shape[0] // bn, pr_c.shape[0] // bn
    rows = S // GRID_W
    nblk = rows // NA_QROWS
    nq = NA_QROWS * GRID_W

    def pat(b, j):
        return (jnp.where(j == 0, 0, jnp.where(j == nblk - 1, 2, 1)), 0, 0, 0)

    return pl.pallas_call(
        functools.partial(_na_kernel, rows=rows),
        out_shape=jax.ShapeDtypeStruct((bn * S, W_NA), BF16),
        grid=(bn, nblk),
        in_specs=[pl.BlockSpec((nq, W_NA), lambda b, j: (b * nblk + j, Q_COL)),
                  pl.BlockSpec((S, W_NA), lambda b, j: (b, K_COL)),
                  pl.BlockSpec((S, W_NA), lambda b, j: (b, V_COL)),
                  pl.BlockSpec((Lc, W_NA), lambda b, j: (b, K_COL)),
                  pl.BlockSpec((Lc, W_NA), lambda b, j: (b, V_COL)),
                  pl.BlockSpec((1,) + bias.shape[1:], pat)],
        out_specs=pl.BlockSpec((nq, W_NA), lambda b, j: (b * nblk + j, 0)),
        compiler_params=_params("parallel", "arbitrary"),
        name="na_attention",
    )(pr, pr, pr, pr_c, pr_c, bias)


def _ctx_attn_kernel(q_ref, k_ref, v_ref, o_ref):
    q = q_ref[...] * (1.0 / math.sqrt(NA_HEAD_DIM))
    o = _attend_heads(q, [(k_ref[...], v_ref[...])], lambda i, h: None)
    o_ref[...] = o.astype(o_ref.dtype)


def ctx_attention(pr_c, bn):
    Lc = pr_c.shape[0] // bn
    return pl.pallas_call(
        _ctx_attn_kernel,
        out_shape=jax.ShapeDtypeStruct((bn * Lc, W_NA), BF16),
        grid=(bn,),
        in_specs=[pl.BlockSpec((Lc, W_NA), lambda b: (b, Q_COL)),
                  pl.BlockSpec((Lc, W_NA), lambda b: (b, K_COL)),
                  pl.BlockSpec((Lc, W_NA), lambda b: (b, V_COL))],
        out_specs=pl.BlockSpec((Lc, W_NA), lambda b: (b, 0)),
        compiler_params=_params("parallel"),
        name="ctx_attention",
    )(pr_c, pr_c, pr_c)


POOL_COL, CONV_H_COL, CONV_B_COL, CONV_C_COL = 0, 1, 2, 3


def _shift_rows(x, k, row):
    n = x.shape[0]
    y = pltpu.roll(x, k % n, 0)
    return jnp.where(row < k, 0.0, y) if k > 0 else jnp.where(row >= n + k, 0.0, y)


def _local_mix_kernel(u_ref, h_ref, bg_ref, cg_ref, pw_ref, ps_ref, cw_ref, op_ref, oc_ref, *, seq):
    row = lax.broadcasted_iota(jnp.int32, (seq, 1), 0)
    lane = lax.broadcasted_iota(jnp.int32, (1, W_POOL), 1)
    u = u_ref[...].astype(F32)
    trailing, leading = {1: u}, {1: u}
    for w in (1, 2, 4):
        trailing[2 * w] = trailing[w] + _shift_rows(trailing[w], w, row)
        leading[2 * w] = leading[w] + _shift_rows(leading[w], -w, row)
    rowf = row.astype(F32)
    win_sum = jnp.zeros_like(u)
    cnt = jnp.zeros_like(u)
    for g, win in enumerate(POOL_WINDOWS):
        half = win // 2
        mg = (lane >= g * POOL_GROUP) & (lane < (g + 1) * POOL_GROUP)
        s = _shift_rows(trailing[half], 1, row) + leading[half]
        n = jnp.minimum(rowf + half, float(seq)) - jnp.maximum(rowf - half, 0.0)
        win_sum = jnp.where(mg, s, win_sum)
        cnt = jnp.where(mg, n, cnt)
    p = win_sum / cnt - u
    pooled = jnp.dot(p.astype(BF16), pw_ref[...], preferred_element_type=F32) * ps_ref[...]
    op_ref[...] = pooled.astype(op_ref.dtype)
    v = cg_ref[...].astype(F32) * h_ref[...].astype(F32)
    conv = (_shift_rows(v, 1, row) * cw_ref[0:1, :] + v * cw_ref[1:2, :] + _shift_rows(v, -1, row) * cw_ref[2:3, :])
    oc_ref[...] = (bg_ref[...].astype(F32) * conv).astype(oc_ref.dtype)


def local_mix(pr, pool_w, pool_scale, conv_w, bn):
    seq = pr.shape[0] // bn
    pw = jax.scipy.linalg.block_diag(*[pool_w[g] for g in range(len(POOL_WINDOWS))]).astype(BF16)
    blk = lambda col: pl.BlockSpec((seq, W_POOL), lambda b: (b, col))
    full = lambda a: pl.BlockSpec(a.shape, lambda b: (0,) * a.ndim)
    args = (pw, pool_scale[None, :], conv_w)
    return pl.pallas_call(
        functools.partial(_local_mix_kernel, seq=seq),
        out_shape=(jax.ShapeDtypeStruct((bn * seq, W_POOL), BF16), jax.ShapeDtypeStruct((bn * seq, W_CONV), BF16)),
        grid=(bn,),
        in_specs=[blk(POOL_COL), blk(CONV_H_COL), blk(CONV_B_COL), blk(CONV_C_COL)] + [full(a) for a in args],
        out_specs=(pl.BlockSpec((seq, W_POOL), lambda b: (b, 0)),
                   pl.BlockSpec((seq, W_CONV), lambda b: (b, 0))),
        compiler_params=_params("parallel"),
        name="local_mix",
    )(pr, pr, pr, pr, *args)


Z_COL, XS_COL, BS_COL, CS_COL = 7, 8, 9, 10
DT_COL = 22
HEADS_PER_GROUP = SSD_HEADS // SSD_GROUPS
GROUP_W = HEADS_PER_GROUP * SSD_HEAD_DIM


def _head_mask(h):
    lane = lax.broadcasted_iota(jnp.int32, (1, W_SSD), 1)
    return (lane >= h * SSD_HEAD_DIM) & (lane < (h + 1) * SSD_HEAD_DIM)


def _expand_heads(v, d):
    out = jnp.zeros((v.shape[0], W_SSD), F32)
    for h in range(SSD_HEADS):
        k = d * SSD_HEADS + h
        out = jnp.where(_head_mask(h), v[:, k:k + 1], out)
    return out


def _ssd_chunk_terms(c, xs_ref, b_ref, c_ref, dt_ref, la_ref, dsk_ref, y_ref, upd_ref, dec_ref, need_y):
    T = SSD_CHUNK
    r0 = pl.multiple_of(c * T, T)
    xs = xs_ref[pl.ds(r0, T), :]
    bm = b_ref[pl.ds(r0, T), :]
    cm = c_ref[pl.ds(r0, T), :]
    dt = dt_ref[pl.ds(r0, T), :]
    la = la_ref[pl.ds(r0, T), :]
    li = lax.broadcasted_iota(jnp.int32, (T, T), 0)
    si = lax.broadcasted_iota(jnp.int32, (T, T), 1)
    causal = si <= li
    prefix = jnp.dot(causal.astype(F32), la, precision=lax.Precision.HIGHEST, preferred_element_type=F32)
    total = prefix[T - 1:T, :]
    lane = lax.broadcasted_iota(jnp.int32, (1, LANES), 1)
    acum = jnp.where(lane < SSD_HEADS, prefix, total - prefix + la)
    bt = bm.astype(F32).T.astype(BF16)
    if need_y:
        acum_t = acum.T
        cb = [lax.dot_general(cm[:, g * SSD_STATE:(g + 1) * SSD_STATE], bm[:, g * SSD_STATE:(g + 1) * SSD_STATE],
                              (((1,), (1,)), ((), ())), preferred_element_type=F32) for g in range(SSD_GROUPS)]
        y = jnp.zeros((T, W_SSD), F32)
    for d in range(2):
        acum_e = _expand_heads(acum, d)
        tot_e = _expand_heads(total, d)
        xdt = xs * _expand_heads(dt, d)
        xw = (xdt * jnp.exp(tot_e - acum_e)).astype(BF16)
        upd = [jnp.dot(bt[g * SSD_STATE:(g + 1) * SSD_STATE, :], xw[:, g * GROUP_W:(g + 1) * GROUP_W],
                       preferred_element_type=F32) for g in range(SSD_GROUPS)]
        upd_ref[c, d] = jnp.concatenate(upd, axis=1)
        dec_ref[c, d, 0:T, :] = jnp.exp(acum_e)
        dec_ref[c, d, T:T + 1, :] = jnp.exp(tot_e)
        if need_y:
            mask = causal if d == 0 else (si >= li)
            xdt_b = xdt.astype(BF16)
            for h in range(SSD_HEADS):
                k = d * SSD_HEADS + h
                decay = jnp.exp(jnp.where(mask, acum[:, k:k + 1] - acum_t[k:k + 1, :], -jnp.inf))
                scores = (cb[h // HEADS_PER_GROUP] * decay).astype(BF16)
                y = y + jnp.where(_head_mask(h), jnp.dot(scores, xdt_b, preferred_element_type=F32), 0.0)
    if need_y:
        y_ref[pl.ds(r0, T), :] = xs * dsk_ref[...] + y


def _ssd_chunk_state(c, d, c_ref, y_ref, upd_ref, dec_ref, st_ref, need_y):
    T = SSD_CHUNK
    r0 = pl.multiple_of(c * T, T)
    st = st_ref[d]
    if need_y:
        cm = c_ref[pl.ds(r0, T), :]
        st_b = st.astype(BF16)
        ys = [jnp.dot(cm[:, g * SSD_STATE:(g + 1) * SSD_STATE], st_b[:, g * GROUP_W:(g + 1) * GROUP_W],
                      preferred_element_type=F32) for g in range(SSD_GROUPS)]
        y_ref[pl.ds(r0, T), :] += jnp.concatenate(ys, axis=1) * dec_ref[c, d, 0:T, :]
    st_ref[d] = dec_ref[c, d, T:T + 1, :] * st + upd_ref[c, d]


def _ssd_kernel(z_ref, xs_in, bs_in, cs_in, dtr_ref, cw_ref, cb_ref, dtb_ref, alog_ref, dsk_ref, ng_ref, h0_ref,
                *refs, seq, need_y):
    if need_y:
        o_ref, hT_ref, xs_ref, b_ref, c_ref, dt_ref, la_ref, upd_ref, dec_ref, st_ref, y_ref = refs
    else:
        hT_ref, xs_ref, b_ref, c_ref, dt_ref, la_ref, upd_ref, dec_ref, st_ref = refs
        y_ref = None
    nc = seq // SSD_CHUNK
    row = lax.broadcasted_iota(jnp.int32, (seq, 1), 0)
    for gi, (src, dst) in enumerate(((xs_in, xs_ref), (bs_in, b_ref), (cs_in, c_ref))):
        sl = slice(gi * W_SSD, (gi + 1) * W_SSD)
        x = src[...].astype(F32)
        cv = (_shift_rows(x, 1, row) * cw_ref[0:1, sl] + x * cw_ref[1:2, sl]
              + _shift_rows(x, -1, row) * cw_ref[2:3, sl] + cb_ref[:, sl])
        dst[...] = (cv * jax.nn.sigmoid(cv)).astype(dst.dtype)
    dtv = dtr_ref[...] + dtb_ref[...]
    dt = jnp.maximum(dtv, 0.0) + jnp.log1p(jnp.exp(-jnp.abs(dtv)))
    dt_ref[...] = dt
    la_ref[...] = dt * (-jnp.exp(alog_ref[...]))
    st_ref[...] = h0_ref[0]

    def terms(c, carry):
        _ssd_chunk_terms(c, xs_ref, b_ref, c_ref, dt_ref, la_ref, dsk_ref, y_ref, upd_ref, dec_ref, need_y)
        return carry

    lax.fori_loop(0, nc, terms, 0, unroll=2)

    def recur(i, carry):
        _ssd_chunk_state(i, 0, c_ref, y_ref, upd_ref, dec_ref, st_ref, need_y)
        _ssd_chunk_state(nc - 1 - i, 1, c_ref, y_ref, upd_ref, dec_ref, st_ref, need_y)
        return carry

    lax.fori_loop(0, nc, recur, 0, unroll=2)
    hT_ref[0] = st_ref[...]
    if need_y:
        z = z_ref[...].astype(F32)
        yz = y_ref[...] * (z * jax.nn.sigmoid(z))
        ms = jnp.mean(yz * yz, axis=-1, keepdims=True)
        o_ref[...] = (yz * lax.rsqrt(ms + EPS) * ng_ref[...]).astype(o_ref.dtype)


def ssd_scan(pr, dt, h0, conv_w, conv_b, dt_bias, a_log, d_skip, norm_g, need_y):
    bn = h0.shape[0]
    seq = pr.shape[0] // bn
    nc = seq // SSD_CHUNK
    pad = LANES - 2 * SSD_HEADS
    dtb = jnp.pad(dt_bias.reshape(1, -1), ((0, 0), (0, pad)))
    alog = jnp.pad(a_log.reshape(1, -1), ((0, 0), (0, pad)))
    dsk = jnp.repeat(d_skip, SSD_HEAD_DIM)[None, :]
    blk = lambda col: pl.BlockSpec((seq, W_SSD), lambda b: (b, col))
    full = lambda a: pl.BlockSpec(a.shape, lambda b: (0,) * a.ndim)
    st_shape = (1, 2, SSD_STATE, W_SSD)
    st_spec = pl.BlockSpec(st_shape, lambda b: (b, 0, 0, 0))
    args = (conv_w, conv_b[None, :], dtb, alog, dsk, norm_g[None, :])
    out_shape = [jax.ShapeDtypeStruct((bn,) + st_shape[1:], F32)]
    out_specs = [st_spec]
    scratch = [pltpu.VMEM((seq, W_SSD), F32),
               pltpu.VMEM((seq, W_SSD), BF16), pltpu.VMEM((seq, W_SSD), BF16),
               pltpu.VMEM((seq, LANES), F32), pltpu.VMEM((seq, LANES), F32),
               pltpu.VMEM((nc, 2, SSD_STATE, W_SSD), F32),
               pltpu.VMEM((nc, 2, SSD_CHUNK + 8, W_SSD), F32),
               pltpu.VMEM(st_shape[1:], F32)]
    if need_y:
        out_shape.insert(0, jax.ShapeDtypeStruct((bn * seq, W_SSD), BF16))
        out_specs.insert(0, pl.BlockSpec((seq, W_SSD), lambda b: (b, 0)))
        scratch.append(pltpu.VMEM((seq, W_SSD), F32))
    res = pl.pallas_call(
        functools.partial(_ssd_kernel, seq=seq, need_y=need_y),
        out_shape=out_shape,
        grid=(bn,),
        in_specs=[blk(Z_COL), blk(XS_COL), blk(BS_COL), blk(CS_COL),
                  pl.BlockSpec((seq, LANES), lambda b: (b, 0))]
                 + [full(a) for a in args] + [st_spec],
        out_specs=out_specs,
        scratch_shapes=scratch,
        compiler_params=_params("parallel"),
        name="ssd_scan",
    )(pr, pr, pr, pr, dt, *args, h0)
    return (res[0], res[1]) if need_y else (None, res[0])


def ssd_mix(pr, dt, pr_c, dt_c, conv_w, conv_b, dt_bias, a_log, d_skip, norm_g, ctx_out, bn):
    h0 = jnp.zeros((bn, 2, SSD_STATE, W_SSD), F32)
    oc, hc = ssd_scan(pr_c, dt_c, h0, conv_w, conv_b, dt_bias, a_log, d_skip, norm_g, ctx_out)
    o, _ = ssd_scan(pr, dt, hc, conv_w, conv_b, dt_bias, a_log, d_skip, norm_g, True)
    return o, oc


def mixer(pr, dt, pr_c, dt_c, pool_w, pool_scale, conv_w, na_bias, s_conv_w, s_conv_b, dt_bias, a_log, d_skip,
          s_norm_g, ctx_out, bn):
    o_ssd, oc_ssd = ssd_mix(pr, dt, pr_c, dt_c, s_conv_w, s_conv_b, dt_bias, a_log, d_skip, s_norm_g, ctx_out, bn)
    o = [*local_mix(pr, pool_w, pool_scale, conv_w, bn), na_attention(pr, pr_c, na_bias, bn), o_ssd]
    if not ctx_out:
        return o, None
    oc = [*local_mix(pr_c, pool_w, pool_scale, conv_w, bn), ctx_attention(pr_c, bn), oc_ssd]
    return o, oc


TM = 512
TM_PROJ = 1024
TM_GRP = 512


def kernel(x, c, ctx, c_ctx, w_ada, b_ada, g_mix, g_ffn, w_in, w_out, pool_w, pool_scale, conv_w, na_rpb,
           ssd_conv_w, ssd_conv_b, ssd_dt_bias, ssd_a_log, ssd_d, ssd_norm_g, ffn_w_gu, ffn_w_down,
           moe_router, moe_w_gu, moe_w_down, g_final):
    bn, S, d = x.shape
    Lc = ctx.shape[1]
    m, mc = bn * S, bn * Lc
    tpb = S // TM
    x2 = x.reshape(m, d)
    xc2 = ctx.reshape(mc, d)
    c_rows = jnp.concatenate([c, c_ctx[None, :], jnp.zeros((7, d), F32)], axis=0)
    for l in range(DEPTH):
        ctx_out = l < DEPTH - 1
        last = l == DEPTH - 1
        ada = ada_modulation(c_rows, w_ada[l].astype(BF16), b_ada[l][None, :])
        mod = ada[:bn].reshape(bn, 6, d)
        mod_c = ada[bn:bn + 1].reshape(1, 6, d)
        w_in_l = jnp.pad(w_in[l], ((0, 0), (0, D_IN_PAD - D_IN_PROJ))).astype(BF16)
        g_m = g_mix[l][None, :]
        g_f = g_ffn[l][None, :]
        pr, dt = in_proj(x2, g_m, mod, w_in_l, S // TM_PROJ, TM_PROJ)
        pr_c, dt_c = in_proj(xc2, g_m, mod_c, w_in_l, None, min(TM_PROJ, mc))
        na_bias = na_bias_table(na_rpb[l], S // GRID_W)
        o, oc = mixer(pr, dt, pr_c, dt_c, pool_w[l], pool_scale[l], conv_w[l], na_bias, ssd_conv_w[l],
                      ssd_conv_b[l], ssd_dt_bias[l], ssd_a_log[l], ssd_d[l], ssd_norm_g[l], ctx_out, bn)
        w_out_l = w_out[l].astype(BF16)
        j = l // 2
        if l % 2 == 0:
            w_gu = ffn_w_gu[j].astype(BF16)
            w_dn = ffn_w_down[j].astype(BF16)
            x2 = ffn_dense(x2, o, g_f, mod, w_out_l, w_gu, w_dn, tpb, TM)
            if ctx_out:
                xc2 = ffn_dense(xc2, oc, g_f, mod_c, w_out_l, w_gu, w_dn, None, min(TM, mc))
        else:
            w_r = jnp.pad(moe_router[j], ((0, 0), (0, LANES - N_EXPERTS))).astype(BF16)
            w_gu, w_dn = moe_w_gu[j], moe_w_down[j]
            x2 = moe_block(x2, o, g_f, mod, w_out_l, w_r, w_gu, w_dn, tpb, TM, TM_GRP,
                           g_final[None, :] if last else None)
            if ctx_out:
                xc2 = moe_block(xc2, oc, g_f, mod_c, w_out_l, w_r, w_gu, w_dn, None, min(TM, mc), TM_GRP)
            if last:
                return x2.reshape(bn, S, d)
    return final_norm(x2, g_final[None, :], TM).reshape(bn, S, d)
```

```python
import functools
import math

import numpy as np
import jax
import jax.numpy as jnp
from jax import lax
from jax.experimental import pallas as pl
from jax.experimental.pallas import tpu as pltpu

DEPTH = 2
GRID_W = 64
EPS = 1e-6
W_POOL = 256
W_CONV = 256
W_NA = 256
W_SSD = 256
POOL_WINDOWS = (2, 4, 8, 16)
POOL_GROUP = W_POOL // len(POOL_WINDOWS)
NA_HEADS = 4
NA_HEAD_DIM = W_NA // NA_HEADS
WIN_R = 8
WIN_C = 16
SSD_HEAD_DIM = 64
SSD_HEADS = W_SSD // SSD_HEAD_DIM
SSD_GROUPS = 2
SSD_STATE = 128
SSD_CHUNK = 128
SSD_XBC = W_SSD + 2 * SSD_GROUPS * SSD_STATE
D_IN_PROJ = W_POOL + 3 * W_CONV + 3 * W_NA + W_SSD + SSD_XBC + 2 * SSD_HEADS
N_EXPERTS = 8
TOP_K = 2

LANES = 128
D_IN_PAD = -(-D_IN_PROJ // LANES) * LANES
VMEM_LIMIT = 56 * 1024 * 1024
BF16 = jnp.bfloat16
F32 = jnp.float32


def _params(*sem):
    return pltpu.CompilerParams(dimension_semantics=sem, vmem_limit_bytes=VMEM_LIMIT)


def _norm_mod(x, g, scale, shift):
    ms = jnp.mean(x * x, axis=-1, keepdims=True)
    return (x * lax.rsqrt(ms + EPS) * g) * (1.0 + scale) + shift


def _mod_map(tiles_per_batch):
    if tiles_per_batch is None:
        return lambda i, *_: (0, 0, 0)
    return lambda i, *_: (i // tiles_per_batch, 0, 0)


def _resident(shape, index_map):
    return pl.BlockSpec(shape, index_map, pipeline_mode=pl.Buffered(1))


def _ada_kernel(c_ref, w_ref, b_ref, o_ref):
    c = c_ref[...]
    s = c * jax.nn.sigmoid(c)
    o_ref[...] = jnp.dot(s.astype(BF16), w_ref[...], preferred_element_type=F32) + b_ref[...]


def ada_modulation(c_rows, w, b):
    r, d = c_rows.shape
    n = w.shape[1]
    tn = 1536
    return pl.pallas_call(
        _ada_kernel,
        out_shape=jax.ShapeDtypeStruct((r, n), F32),
        grid=(n // tn,),
        in_specs=[pl.BlockSpec((r, d), lambda j: (0, 0)),
                  pl.BlockSpec((d, tn), lambda j: (0, j)),
                  pl.BlockSpec((1, tn), lambda j: (0, j))],
        out_specs=pl.BlockSpec((r, tn), lambda j: (0, j)),
        compiler_params=_params("arbitrary"),
        name="ada_modulation",
    )(c_rows, w, b)


def _in_proj_kernel(x_ref, g_ref, mod_ref, w_ref, o_ref, dt_ref):
    h = _norm_mod(x_ref[...], g_ref[...], mod_ref[0, 1:2, :], mod_ref[0, 0:1, :])
    pr = jnp.dot(h.astype(BF16), w_ref[...], preferred_element_type=F32)
    o_ref[...] = pr.astype(o_ref.dtype)
    dt_ref[...] = pr[:, DT_COL * LANES:(DT_COL + 1) * LANES]


def in_proj(x2, g, mod, w, tiles_per_batch, tm):
    m, d = x2.shape
    n = w.shape[1]
    return pl.pallas_call(
        _in_proj_kernel,
        out_shape=(jax.ShapeDtypeStruct((m, n), BF16), jax.ShapeDtypeStruct((m, LANES), F32)),
        grid=(m // tm,),
        in_specs=[pl.BlockSpec((tm, d), lambda i: (i, 0)),
                  pl.BlockSpec((1, d), lambda i: (0, 0)),
                  pl.BlockSpec((1, 6, d), _mod_map(tiles_per_batch)),
                  _resident((d, n), lambda i: (0, 0))],
        out_specs=(pl.BlockSpec((tm, n), lambda i: (i, 0)), pl.BlockSpec((tm, LANES), lambda i: (i, 0))),
        compiler_params=_params("parallel"),
        name="in_proj",
    )(x2, g, mod, w)


def _mix_residual(x, mod_ref, w_ref, part_refs):
    y, k0 = None, 0
    for p_ref in part_refs:
        k = p_ref.shape[1]
        t = jnp.dot(p_ref[...], w_ref[k0:k0 + k, :], preferred_element_type=F32)
        y = t if y is None else y + t
        k0 += k
    return x + mod_ref[0, 2:3, :] * y


def _part_specs(parts, tm):
    return [pl.BlockSpec((tm, p.shape[1]), lambda i, *_: (i, 0)) for p in parts]


FF_CHUNK = 512


def _swiglu(h, wgu, wd, ff):
    acc = None
    for c0 in range(0, ff, FF_CHUNK):
        c1 = min(c0 + FF_CHUNK, ff)
        gg = jnp.dot(h, wgu(c0, c1), preferred_element_type=F32)
        uu = jnp.dot(h, wgu(ff + c0, ff + c1), preferred_element_type=F32)
        a = (gg * jax.nn.sigmoid(gg) * uu).astype(BF16)
        part = jnp.dot(a, wd(c0, c1), preferred_element_type=F32)
        acc = part if acc is None else acc + part
    return acc


def _ffn_kernel(x_ref, g_ref, mod_ref, wout_ref, wgu_ref, wd_ref, *refs):
    x = _mix_residual(x_ref[...], mod_ref, wout_ref, refs[:-1])
    y_ref = refs[-1]
    h = _norm_mod(x, g_ref[...], mod_ref[0, 4:5, :], mod_ref[0, 3:4, :]).astype(BF16)
    y = _swiglu(h, lambda a, b: wgu_ref[:, a:b], lambda a, b: wd_ref[a:b, :], wd_ref.shape[0])
    y_ref[...] = x + mod_ref[0, 5:6, :] * y


def ffn_dense(x2, parts, g, mod, w_out, w_gu, w_down, tiles_per_batch, tm):
    m, d = x2.shape
    return pl.pallas_call(
        _ffn_kernel,
        out_shape=jax.ShapeDtypeStruct((m, d), F32),
        grid=(m // tm,),
        in_specs=[pl.BlockSpec((tm, d), lambda i: (i, 0)),
                  pl.BlockSpec((1, d), lambda i: (0, 0)),
                  pl.BlockSpec((1, 6, d), _mod_map(tiles_per_batch)),
                  _resident(w_out.shape, lambda i: (0, 0)),
                  _resident(w_gu.shape, lambda i: (0, 0)),
                  _resident(w_down.shape, lambda i: (0, 0))] + _part_specs(parts, tm),
        out_specs=pl.BlockSpec((tm, d), lambda i: (i, 0)),
        compiler_params=_params("parallel"),
        name="ffn_dense",
    )(x2, g, mod, w_out, w_gu, w_down, *parts)


ROUTE_E1, ROUTE_E2, ROUTE_R1, ROUTE_R2, ROUTE_G1, ROUTE_G2 = range(6)
ROUTE_ROWS = 8


def _router_kernel(x_ref, g_ref, mod_ref, wout_ref, wr_ref, *refs):
    part_refs = refs[:-6]
    x1_ref, h_ref, route_ref, route_t_ref, cnt_ref, base_ref = refs[-6:]

    @pl.when(pl.program_id(0) == 0)
    def _():
        base_ref[...] = jnp.zeros_like(base_ref)

    x = _mix_residual(x_ref[...], mod_ref, wout_ref, part_refs)
    x1_ref[...] = x
    h = _norm_mod(x, g_ref[...], mod_ref[0, 4:5, :], mod_ref[0, 3:4, :]).astype(BF16)
    h_ref[...] = h
    tm = h.shape[0]
    logits = jnp.dot(h, wr_ref[...], preferred_element_type=F32)
    lane = lax.broadcasted_iota(jnp.int32, logits.shape, 1)
    neg = jnp.float32(-jnp.inf)
    logits = jnp.where(lane < N_EXPERTS, logits, neg)
    m1 = jnp.max(logits, axis=-1, keepdims=True)
    i1 = jnp.min(jnp.where(logits == m1, lane, LANES), axis=-1, keepdims=True)
    rest = jnp.where(lane == i1, neg, logits)
    m2 = jnp.max(rest, axis=-1, keepdims=True)
    i2 = jnp.min(jnp.where(rest == m2, lane, LANES), axis=-1, keepdims=True)
    e2 = jnp.exp(m2 - m1)
    g1 = 1.0 / (1.0 + e2)
    g2 = e2 / (1.0 + e2)
    sel1, sel2 = lane == i1, lane == i2
    sel = jnp.where(sel1 | sel2, 1.0, 0.0)
    li = lax.broadcasted_iota(jnp.int32, (tm, tm), 0)
    si = lax.broadcasted_iota(jnp.int32, (tm, tm), 1)
    before = jnp.where(si < li, 1.0, 0.0).astype(BF16)
    rank = jnp.dot(before, sel.astype(BF16), preferred_element_type=F32) + base_ref[...]
    r1 = jnp.sum(jnp.where(sel1, rank, 0.0), axis=-1, keepdims=True)
    r2 = jnp.sum(jnp.where(sel2, rank, 0.0), axis=-1, keepdims=True)
    base_ref[...] += jnp.sum(sel, axis=0, keepdims=True)
    cnt_ref[...] = base_ref[...]
    fields = (i1.astype(F32), i2.astype(F32), r1, r2, g1, g2)
    route = jnp.zeros(logits.shape, F32)
    for k, v in enumerate(fields):
        route = jnp.where(lane == k, v, route)
    route_ref[...] = route
    route_t_ref[...] = route.T[:ROUTE_ROWS, :]


def moe_router(x2, parts, g, mod, w_out, w_router, tiles_per_batch, tm):
    m, d = x2.shape
    return pl.pallas_call(
        _router_kernel,
        out_shape=(jax.ShapeDtypeStruct((m, d), F32), jax.ShapeDtypeStruct((m, d), BF16),
                   jax.ShapeDtypeStruct((m, LANES), F32), jax.ShapeDtypeStruct((ROUTE_ROWS, m), F32),
                   jax.ShapeDtypeStruct((1, LANES), F32)),
        grid=(m // tm,),
        in_specs=[pl.BlockSpec((tm, d), lambda i: (i, 0)),
                  pl.BlockSpec((1, d), lambda i: (0, 0)),
                  pl.BlockSpec((1, 6, d), _mod_map(tiles_per_batch)),
                  _resident(w_out.shape, lambda i: (0, 0)),
                  pl.BlockSpec((d, LANES), lambda i: (0, 0))] + _part_specs(parts, tm),
        out_specs=(pl.BlockSpec((tm, d), lambda i: (i, 0)),
                   pl.BlockSpec((tm, d), lambda i: (i, 0)),
                   pl.BlockSpec((tm, LANES), lambda i: (i, 0)),
                   pl.BlockSpec((ROUTE_ROWS, tm), lambda i: (0, i)),
                   pl.BlockSpec((1, LANES), lambda i: (0, 0))),
        scratch_shapes=[pltpu.VMEM((1, LANES), F32)],
        compiler_params=_params("arbitrary"),
        name="moe_router",
    )(x2, g, mod, w_out, w_router, *parts)


def _moe_kernel(tile_ref, exp_ref, start_ref, end_ref, xg_ref, wgu_ref, wd_ref, y_ref, wgu_b, wd_b):
    w = pl.program_id(0)
    tm = xg_ref.shape[0]
    start, end = start_ref[w], end_ref[w]
    t0 = tile_ref[w] * tm

    @pl.when((w == 0) | (exp_ref[w] != exp_ref[jnp.maximum(w - 1, 0)]))
    def _():
        wgu_b[...] = wgu_ref[0].astype(BF16)
        wd_b[...] = wd_ref[0].astype(BF16)

    @pl.when(end > start)
    def _():
        y = _swiglu(xg_ref[...], lambda a, b: wgu_b[:, a:b], lambda a, b: wd_b[a:b, :], wd_b.shape[0])
        y = y.astype(y_ref.dtype)

        @pl.when(start == t0)
        def _():
            y_ref[...] = y

        @pl.when(start != t0)
        def _():
            row = t0 + lax.broadcasted_iota(jnp.int32, (tm, 1), 0)
            y_ref[...] = jnp.where(row >= start, y, y_ref[...])


def moe_grouped(item_tile, item_expert, item_start, item_end, xg, w_gu, w_down, tm):
    p, d = xg.shape
    grid_spec = pltpu.PrefetchScalarGridSpec(
        num_scalar_prefetch=4,
        grid=(item_tile.shape[0],),
        in_specs=[pl.BlockSpec((tm, d), lambda w, it, ie, s, e: (it[w], 0)),
                  _resident((1,) + w_gu.shape[1:], lambda w, it, ie, s, e: (ie[w], 0, 0)),
                  _resident((1,) + w_down.shape[1:], lambda w, it, ie, s, e: (ie[w], 0, 0))],
        out_specs=pl.BlockSpec((tm, d), lambda w, it, ie, s, e: (it[w], 0)),
        scratch_shapes=[pltpu.VMEM(w_gu.shape[1:], BF16), pltpu.VMEM(w_down.shape[1:], BF16)],
    )
    return pl.pallas_call(
        _moe_kernel,
        out_shape=jax.ShapeDtypeStruct((p, d), BF16),
        grid_spec=grid_spec,
        compiler_params=_params("arbitrary"),
        name="moe_grouped",
    )(item_tile, item_expert, item_start, item_end, xg, w_gu, w_down)


def _moe_combine_kernel(x_ref, ya_ref, yb_ref, route_ref, mod_ref, *refs):
    o_ref = refs[-1]
    r = route_ref[...]
    y = (r[:, ROUTE_G1:ROUTE_G1 + 1] * ya_ref[...].astype(F32) + r[:, ROUTE_G2:ROUTE_G2 + 1] * yb_ref[...].astype(F32))
    x = x_ref[...] + mod_ref[0, 5:6, :] * y
    if len(refs) == 2:
        ms = jnp.mean(x * x, axis=-1, keepdims=True)
        x = x * lax.rsqrt(ms + EPS) * refs[0][...]
    o_ref[...] = x


def moe_combine(x2, ya, yb, route, mod, tiles_per_batch, tm, g_final=None):
    m, d = x2.shape
    row = pl.BlockSpec((tm, d), lambda i: (i, 0))
    specs = [row, row, row, pl.BlockSpec((tm, LANES), lambda i: (i, 0)),
             pl.BlockSpec((1, 6, d), _mod_map(tiles_per_batch))]
    args = [x2, ya, yb, route, mod]
    if g_final is not None:
        specs.append(pl.BlockSpec((1, d), lambda i: (0, 0)))
        args.append(g_final)
    return pl.pallas_call(
        _moe_combine_kernel,
        out_shape=jax.ShapeDtypeStruct((m, d), F32),
        grid=(m // tm,),
        in_specs=specs,
        out_specs=row,
        compiler_params=_params("parallel"),
        name="moe_combine",
    )(*args)


def moe_block(x2, parts, g, mod, w_out, w_router, w_gu, w_down, tiles_per_batch, tm_tok, tm_grp, g_final=None):
    m, d = x2.shape
    x2, h, route, route_t, cnt = moe_router(x2, parts, g, mod, w_out, w_router, tiles_per_batch, tm_tok)
    cnt = cnt[0, :N_EXPERTS].astype(jnp.int32)
    p = m * TOP_K
    off = jnp.cumsum(cnt) - cnt
    field = lambda k: route_t[k].astype(jnp.int32)

    def row_of(e, r):
        for k in range(N_EXPERTS):
            r = r + jnp.where(e == k, off[k], 0)
        return r

    d1 = row_of(field(ROUTE_E1), field(ROUTE_R1))
    d2 = row_of(field(ROUTE_E2), field(ROUTE_R2))
    tok = jnp.arange(m, dtype=jnp.int32)
    _, src = lax.sort_key_val(jnp.concatenate([d1, d2]), jnp.concatenate([tok, tok]))
    n_row_tiles = p // tm_grp
    start = jnp.sort(jnp.concatenate([jnp.arange(n_row_tiles, dtype=jnp.int32) * tm_grp, off]))
    end = jnp.concatenate([start[1:], jnp.full((1,), p, jnp.int32)])
    item_tile = jnp.minimum(start // tm_grp, n_row_tiles - 1)
    item_expert = jnp.sum((off[None, :] <= start[:, None]).astype(jnp.int32), axis=1) - 1
    rows = lambda a, idx: a.at[idx].get(mode="promise_in_bounds")
    yg = moe_grouped(item_tile, item_expert, start, end, rows(h, src), w_gu, w_down, tm_grp)
    return moe_combine(x2, rows(yg, d1), rows(yg, d2), route, mod, tiles_per_batch, tm_tok, g_final)


def _final_norm_kernel(x_ref, g_ref, o_ref):
    x = x_ref[...]
    ms = jnp.mean(x * x, axis=-1, keepdims=True)
    o_ref[...] = x * lax.rsqrt(ms + EPS) * g_ref[...]


def final_norm(x2, g, tm):
    m, d = x2.shape
    return pl.pallas_call(
        _final_norm_kernel,
        out_shape=jax.ShapeDtypeStruct((m, d), F32),
        grid=(m // tm,),
        in_specs=[pl.BlockSpec((tm, d), lambda i: (i, 0)), pl.BlockSpec((1, d), lambda i: (0, 0))],
        out_specs=pl.BlockSpec((tm, d), lambda i: (i, 0)),
        compiler_params=_params("parallel"),
        name="final_norm",
    )(x2, g)


NA_QROWS = 4
NA_KROWS = NA_QROWS + WIN_R
Q_COL, K_COL, V_COL = 4, 5, 6


def _na_key_start(j, rows):
    return np.clip(j * NA_QROWS - WIN_R // 2, 0, rows - NA_KROWS)


def _na_bias_index(rows):
    nblk = rows // NA_QROWS
    pats = []
    for j in range(nblk):
        start = _na_key_start(j, rows)
        r = j * NA_QROWS + np.arange(NA_QROWS)
        sr = np.clip(r - WIN_R // 2, 0, rows - WIN_R)
        kr = start + np.arange(NA_KROWS)
        rvalid = (kr[None, :] >= sr[:, None]) & (kr[None, :] < sr[:, None] + WIN_R)
        ri = np.clip(kr[None, :] - r[:, None] + WIN_R - 1, 0, 2 * WIN_R - 2)
        pats.append((ri, rvalid))
    for j in range(2, nblk - 1):
        assert all(np.array_equal(a, b) for a, b in zip(pats[1], pats[j]))
    sel = [pats[0], pats[1], pats[nblk - 1]]
    ri = np.stack([p[0] for p in sel])
    rvalid = np.stack([p[1] for p in sel])
    c = np.arange(GRID_W)
    sc = np.clip(c - WIN_C // 2, 0, GRID_W - WIN_C)
    cvalid = (c[None, :] >= sc[:, None]) & (c[None, :] < sc[:, None] + WIN_C)
    ci = np.clip(c[None, :] - c[:, None] + WIN_C - 1, 0, 2 * WIN_C - 2)
    c_onehot = (ci[..., None] == np.arange(2 * WIN_C - 1)).astype(np.float32)
    valid = rvalid[:, :, None, :, None] & cvalid[None, None, :, None, :]
    return ri, c_onehot, valid


def na_bias_table(rpb, rows):
    ri, c_onehot, valid = _na_bias_index(rows)
    toep = jnp.einsum('hrd,qkd->hrqk', rpb, c_onehot, precision=lax.Precision.HIGHEST)
    b = jnp.take(toep, ri.reshape(-1), axis=1).reshape((NA_HEADS,) + ri.shape + (GRID_W, GRID_W))
    b = jnp.transpose(b, (1, 0, 2, 4, 3, 5))
    b = jnp.where(valid[:, None], b, -jnp.inf)
    return b.reshape(3, NA_HEADS, NA_QROWS * GRID_W, NA_KROWS * GRID_W).astype(F32)


def _attend_heads(q, key_sets, bias_fn):
    n, w = q.shape
    lane = lax.broadcasted_iota(jnp.int32, (1, w), 1)
    head_masks = [(lane >= h * NA_HEAD_DIM) & (lane < (h + 1) * NA_HEAD_DIM) for h in range(NA_HEADS)]
    all_scores = []
    for h, mh in enumerate(head_masks):
        qh = jnp.where(mh, q, 0.0).astype(BF16)
        scores = []
        for i, (k, _) in enumerate(key_sets):
            s = lax.dot_general(qh, k, (((1,), (1,)), ((), ())), preferred_element_type=F32)
            b = bias_fn(i, h)
            scores.append(s if b is None else s + b)
        all_scores.append(scores)
    out = jnp.zeros((n, w), F32)
    for mh, scores in zip(head_masks, all_scores):
        m = scores[0].max(axis=-1, keepdims=True)
        for s in scores[1:]:
            m = jnp.maximum(m, s.max(axis=-1, keepdims=True))
        denom = jnp.zeros((n, 1), F32)
        acc = jnp.zeros((n, w), F32)
        for s, (_, v) in zip(scores, key_sets):
            p = jnp.exp(s - m)
            denom = denom + p.sum(axis=-1, keepdims=True)
            acc = acc + jnp.dot(p.astype(BF16), v, preferred_element_type=F32)
        out = out + jnp.where(mh, acc / denom, 0.0)
    return out


def _na_kernel(q_ref, k_ref, v_ref, kc_ref, vc_ref, bias_ref, o_ref, *, rows):
    j = pl.program_id(1)
    start = jnp.clip(j * NA_QROWS - WIN_R // 2, 0, rows - NA_KROWS)
    t0 = pl.multiple_of(start * GRID_W, GRID_W)
    nk = NA_KROWS * GRID_W
    kw = k_ref[pl.ds(t0, nk), :]
    vw = v_ref[pl.ds(t0, nk), :]
    q = q_ref[...] * (1.0 / math.sqrt(NA_HEAD_DIM))
    o = _attend_heads(q, [(kw, vw), (kc_ref[...], vc_ref[...])], lambda i, h: bias_ref[0, h] if i == 0 else None)
    o_ref[...] = o.astype(o_ref.dtype)


def na_attention(pr, pr_c, bias, bn):
    S, Lc = pr.shape[0] // bn, pr_c.shape[0] // bn
    rows = S // GRID_W
    nblk = rows // NA_QROWS
    nq = NA_QROWS * GRID_W

    def pat(b, j):
        return (jnp.where(j == 0, 0, jnp.where(j == nblk - 1, 2, 1)), 0, 0, 0)

    return pl.pallas_call(
        functools.partial(_na_kernel, rows=rows),
        out_shape=jax.ShapeDtypeStruct((bn * S, W_NA), BF16),
        grid=(bn, nblk),
        in_specs=[pl.BlockSpec((nq, W_NA), lambda b, j: (b * nblk + j, Q_COL)),
                  pl.BlockSpec((S, W_NA), lambda b, j: (b, K_COL)),
                  pl.BlockSpec((S, W_NA), lambda b, j: (b, V_COL)),
                  pl.BlockSpec((Lc, W_NA), lambda b, j: (b, K_COL)),
                  pl.BlockSpec((Lc, W_NA), lambda b, j: (b, V_COL)),
                  pl.BlockSpec((1,) + bias.shape[1:], pat)],
        out_specs=pl.BlockSpec((nq, W_NA), lambda b, j: (b * nblk + j, 0)),
        compiler_params=_params("parallel", "arbitrary"),
        name="na_attention",
    )(pr, pr, pr, pr_c, pr_c, bias)


def _ctx_attn_kernel(q_ref, k_ref, v_ref, o_ref):
    q = q_ref[...] * (1.0 / math.sqrt(NA_HEAD_DIM))
    o = _attend_heads(q, [(k_ref[...], v_ref[...])], lambda i, h: None)
    o_ref[...] = o.astype(o_ref.dtype)


def ctx_attention(pr_c, bn):
    Lc = pr_c.shape[0] // bn
    return pl.pallas_call(
        _ctx_attn_kernel,
        out_shape=jax.ShapeDtypeStruct((bn * Lc, W_NA), BF16),
        grid=(bn,),
        in_specs=[pl.BlockSpec((Lc, W_NA), lambda b: (b, Q_COL)),
                  pl.BlockSpec((Lc, W_NA), lambda b: (b, K_COL)),
                  pl.BlockSpec((Lc, W_NA), lambda b: (b, V_COL))],
        out_specs=pl.BlockSpec((Lc, W_NA), lambda b: (b, 0)),
        compiler_params=_params("parallel"),
        name="ctx_attention",
    )(pr_c, pr_c, pr_c)


POOL_COL, CONV_H_COL, CONV_B_COL, CONV_C_COL = 0, 1, 2, 3


def _shift_rows(x, k, row):
    n = x.shape[0]
    y = pltpu.roll(x, k % n, 0)
    return jnp.where(row < k, 0.0, y) if k > 0 else jnp.where(row >= n + k, 0.0, y)


def _local_mix_kernel(u_ref, h_ref, bg_ref, cg_ref, pw_ref, ps_ref, cw_ref, op_ref, oc_ref, *, seq):
    row = lax.broadcasted_iota(jnp.int32, (seq, 1), 0)
    lane = lax.broadcasted_iota(jnp.int32, (1, W_POOL), 1)
    u = u_ref[...].astype(F32)
    trailing, leading = {1: u}, {1: u}
    for w in (1, 2, 4):
        trailing[2 * w] = trailing[w] + _shift_rows(trailing[w], w, row)
        leading[2 * w] = leading[w] + _shift_rows(leading[w], -w, row)
    rowf = row.astype(F32)
    win_sum = jnp.zeros_like(u)
    cnt = jnp.zeros_like(u)
    for g, win in enumerate(POOL_WINDOWS):
        half = win // 2
        mg = (lane >= g * POOL_GROUP) & (lane < (g + 1) * POOL_GROUP)
        s = _shift_rows(trailing[half], 1, row) + leading[half]
        n = jnp.minimum(rowf + half, float(seq)) - jnp.maximum(rowf - half, 0.0)
        win_sum = jnp.where(mg, s, win_sum)
        cnt = jnp.where(mg, n, cnt)
    p = win_sum / cnt - u
    pooled = jnp.dot(p.astype(BF16), pw_ref[...], preferred_element_type=F32) * ps_ref[...]
    op_ref[...] = pooled.astype(op_ref.dtype)
    v = cg_ref[...].astype(F32) * h_ref[...].astype(F32)
    conv = (_shift_rows(v, 1, row) * cw_ref[0:1, :] + v * cw_ref[1:2, :] + _shift_rows(v, -1, row) * cw_ref[2:3, :])
    oc_ref[...] = (bg_ref[...].astype(F32) * conv).astype(oc_ref.dtype)


def local_mix(pr, pool_w, pool_scale, conv_w, bn):
    seq = pr.shape[0] // bn
    pw = jax.scipy.linalg.block_diag(*[pool_w[g] for g in range(len(POOL_WINDOWS))]).astype(BF16)
    blk = lambda col: pl.BlockSpec((seq, W_POOL), lambda b: (b, col))
    full = lambda a: pl.BlockSpec(a.shape, lambda b: (0,) * a.ndim)
    args = (pw, pool_scale[None, :], conv_w)
    return pl.pallas_call(
        functools.partial(_local_mix_kernel, seq=seq),
        out_shape=(jax.ShapeDtypeStruct((bn * seq, W_POOL), BF16), jax.ShapeDtypeStruct((bn * seq, W_CONV), BF16)),
        grid=(bn,),
        in_specs=[blk(POOL_COL), blk(CONV_H_COL), blk(CONV_B_COL), blk(CONV_C_COL)] + [full(a) for a in args],
        out_specs=(pl.BlockSpec((seq, W_POOL), lambda b: (b, 0)),
                   pl.BlockSpec((seq, W_CONV), lambda b: (b, 0))),
        compiler_params=_params("parallel"),
        name="local_mix",
    )(pr, pr, pr, pr, *args)


Z_COL, XS_COL, BS_COL, CS_COL = 7, 8, 9, 10
DT_COL = 22
HEADS_PER_GROUP = SSD_HEADS // SSD_GROUPS
GROUP_W = HEADS_PER_GROUP * SSD_HEAD_DIM


def _head_mask(h):
    lane = lax.broadcasted_iota(jnp.int32, (1, W_SSD), 1)
    return (lane >= h * SSD_HEAD_DIM) & (lane < (h + 1) * SSD_HEAD_DIM)


def _expand_heads(v, d):
    out = jnp.zeros((v.shape[0], W_SSD), F32)
    for h in range(SSD_HEADS):
        k = d * SSD_HEADS + h
        out = jnp.where(_head_mask(h), v[:, k:k + 1], out)
    return out


def _ssd_chunk_terms(c, xs_ref, b_ref, c_ref, dt_ref, la_ref, dsk_ref, y_ref, upd_ref, dec_ref, need_y):
    T = SSD_CHUNK
    r0 = pl.multiple_of(c * T, T)
    xs = xs_ref[pl.ds(r0, T), :]
    bm = b_ref[pl.ds(r0, T), :]
    cm = c_ref[pl.ds(r0, T), :]
    dt = dt_ref[pl.ds(r0, T), :]
    la = la_ref[pl.ds(r0, T), :]
    li = lax.broadcasted_iota(jnp.int32, (T, T), 0)
    si = lax.broadcasted_iota(jnp.int32, (T, T), 1)
    causal = si <= li
    prefix = jnp.dot(causal.astype(F32), la, precision=lax.Precision.HIGHEST, preferred_element_type=F32)
    total = prefix[T - 1:T, :]
    lane = lax.broadcasted_iota(jnp.int32, (1, LANES), 1)
    acum = jnp.where(lane < SSD_HEADS, prefix, total - prefix + la)
    bt = bm.astype(F32).T.astype(BF16)
    if need_y:
        acum_t = acum.T
        cb = [lax.dot_general(cm[:, g * SSD_STATE:(g + 1) * SSD_STATE], bm[:, g * SSD_STATE:(g + 1) * SSD_STATE],
                              (((1,), (1,)), ((), ())), preferred_element_type=F32) for g in range(SSD_GROUPS)]
        y = jnp.zeros((T, W_SSD), F32)
    for d in range(2):
        acum_e = _expand_heads(acum, d)
        tot_e = _expand_heads(total, d)
        xdt = xs * _expand_heads(dt, d)
        xw = (xdt * jnp.exp(tot_e - acum_e)).astype(BF16)
        upd = [jnp.dot(bt[g * SSD_STATE:(g + 1) * SSD_STATE, :], xw[:, g * GROUP_W:(g + 1) * GROUP_W],
                       preferred_element_type=F32) for g in range(SSD_GROUPS)]
        upd_ref[c, d] = jnp.concatenate(upd, axis=1)
        dec_ref[c, d, 0:T, :] = jnp.exp(acum_e)
        dec_ref[c, d, T:T + 1, :] = jnp.exp(tot_e)
        if need_y:
            mask = causal if d == 0 else (si >= li)
            xdt_b = xdt.astype(BF16)
            for h in range(SSD_HEADS):
                k = d * SSD_HEADS + h
                decay = jnp.exp(jnp.where(mask, acum[:, k:k + 1] - acum_t[k:k + 1, :], -jnp.inf))
                scores = (cb[h // HEADS_PER_GROUP] * decay).astype(BF16)
                y = y + jnp.where(_head_mask(h), jnp.dot(scores, xdt_b, preferred_element_type=F32), 0.0)
    if need_y:
        y_ref[pl.ds(r0, T), :] = xs * dsk_ref[...] + y


def _ssd_chunk_state(c, d, c_ref, y_ref, upd_ref, dec_ref, st_ref, need_y):
    T = SSD_CHUNK
    r0 = pl.multiple_of(c * T, T)
    st = st_ref[d]
    if need_y:
        cm = c_ref[pl.ds(r0, T), :]
        st_b = st.astype(BF16)
        ys = [jnp.dot(cm[:, g * SSD_STATE:(g + 1) * SSD_STATE], st_b[:, g * GROUP_W:(g + 1) * GROUP_W],
                      preferred_element_type=F32) for g in range(SSD_GROUPS)]
        y_ref[pl.ds(r0, T), :] += jnp.concatenate(ys, axis=1) * dec_ref[c, d, 0:T, :]
    st_ref[d] = dec_ref[c, d, T:T + 1, :] * st + upd_ref[c, d]


def _ssd_kernel(z_ref, xs_in, bs_in, cs_in, dtr_ref, cw_ref, cb_ref, dtb_ref, alog_ref, dsk_ref, ng_ref, h0_ref,
                *refs, seq, need_y):
    if need_y:
        o_ref, hT_ref, xs_ref, b_ref, c_ref, dt_ref, la_ref, upd_ref, dec_ref, st_ref, y_ref = refs
    else:
        hT_ref, xs_ref, b_ref, c_ref, dt_ref, la_ref, upd_ref, dec_ref, st_ref = refs
        y_ref = None
    nc = seq // SSD_CHUNK
    row = lax.broadcasted_iota(jnp.int32, (seq, 1), 0)
    for gi, (src, dst) in enumerate(((xs_in, xs_ref), (bs_in, b_ref), (cs_in, c_ref))):
        sl = slice(gi * W_SSD, (gi + 1) * W_SSD)
        x = src[...].astype(F32)
        cv = (_shift_rows(x, 1, row) * cw_ref[0:1, sl] + x * cw_ref[1:2, sl]
              + _shift_rows(x, -1, row) * cw_ref[2:3, sl] + cb_ref[:, sl])
        dst[...] = (cv * jax.nn.sigmoid(cv)).astype(dst.dtype)
    dtv = dtr_ref[...] + dtb_ref[...]
    dt = jnp.maximum(dtv, 0.0) + jnp.log1p(jnp.exp(-jnp.abs(dtv)))
    dt_ref[...] = dt
    la_ref[...] = dt * (-jnp.exp(alog_ref[...]))
    st_ref[...] = h0_ref[0]

    def terms(c, carry):
        _ssd_chunk_terms(c, xs_ref, b_ref, c_ref, dt_ref, la_ref, dsk_ref, y_ref, upd_ref, dec_ref, need_y)
        return carry

    lax.fori_loop(0, nc, terms, 0, unroll=2)

    def recur(i, carry):
        _ssd_chunk_state(i, 0, c_ref, y_ref, upd_ref, dec_ref, st_ref, need_y)
        _ssd_chunk_state(nc - 1 - i, 1, c_ref, y_ref, upd_ref, dec_ref, st_ref, need_y)
        return carry

    lax.fori_loop(0, nc, recur, 0, unroll=2)
    hT_ref[0] = st_ref[...]
    if need_y:
        z = z_ref[...].astype(F32)
        yz = y_ref[...] * (z * jax.nn.sigmoid(z))
        ms = jnp.mean(yz * yz, axis=-1, keepdims=True)
        o_ref[...] = (yz * lax.rsqrt(ms + EPS) * ng_ref[...]).astype(o_ref.dtype)


def ssd_scan(pr, dt, h0, conv_w, conv_b, dt_bias, a_log, d_skip, norm_g, need_y):
    bn = h0.shape[0]
    seq = pr.shape[0] // bn
    nc = seq // SSD_CHUNK
    pad = LANES - 2 * SSD_HEADS
    dtb = jnp.pad(dt_bias.reshape(1, -1), ((0, 0), (0, pad)))
    alog = jnp.pad(a_log.reshape(1, -1), ((0, 0), (0, pad)))
    dsk = jnp.repeat(d_skip, SSD_HEAD_DIM)[None, :]
    blk = lambda col: pl.BlockSpec((seq, W_SSD), lambda b: (b, col))
    full = lambda a: pl.BlockSpec(a.shape, lambda b: (0,) * a.ndim)
    st_shape = (1, 2, SSD_STATE, W_SSD)
    st_spec = pl.BlockSpec(st_shape, lambda b: (b, 0, 0, 0))
    args = (conv_w, conv_b[None, :], dtb, alog, dsk, norm_g[None, :])
    out_shape = [jax.ShapeDtypeStruct((bn,) + st_shape[1:], F32)]
    out_specs = [st_spec]
    scratch = [pltpu.VMEM((seq, W_SSD), F32),
               pltpu.VMEM((seq, W_SSD), BF16), pltpu.VMEM((seq, W_SSD), BF16),
               pltpu.VMEM((seq, LANES), F32), pltpu.VMEM((seq, LANES), F32),
               pltpu.VMEM((nc, 2, SSD_STATE, W_SSD), F32),
               pltpu.VMEM((nc, 2, SSD_CHUNK + 8, W_SSD), F32),
               pltpu.VMEM(st_shape[1:], F32)]
    if need_y:
        out_shape.insert(0, jax.ShapeDtypeStruct((bn * seq, W_SSD), BF16))
        out_specs.insert(0, pl.BlockSpec((seq, W_SSD), lambda b: (b, 0)))
        scratch.append(pltpu.VMEM((seq, W_SSD), F32))
    res = pl.pallas_call(
        functools.partial(_ssd_kernel, seq=seq, need_y=need_y),
        out_shape=out_shape,
        grid=(bn,),
        in_specs=[blk(Z_COL), blk(XS_COL), blk(BS_COL), blk(CS_COL),
                  pl.BlockSpec((seq, LANES), lambda b: (b, 0))]
                 + [full(a) for a in args] + [st_spec],
        out_specs=out_specs,
        scratch_shapes=scratch,
        compiler_params=_params("parallel"),
        name="ssd_scan",
    )(pr, pr, pr, pr, dt, *args, h0)
    return (res[0], res[1]) if need_y else (None, res[0])


def ssd_mix(pr, dt, pr_c, dt_c, conv_w, conv_b, dt_bias, a_log, d_skip, norm_g, ctx_out, bn):
    h0 = jnp.zeros((bn, 2, SSD_STATE, W_SSD), F32)
    oc, hc = ssd_scan(pr_c, dt_c, h0, conv_w, conv_b, dt_bias, a_log, d_skip, norm_g, ctx_out)
    o, _ = ssd_scan(pr, dt, hc, conv_w, conv_b, dt_bias, a_log, d_skip, norm_g, True)
    return o, oc


def mixer(pr, dt, pr_c, dt_c, pool_w, pool_scale, conv_w, na_bias, s_conv_w, s_conv_b, dt_bias, a_log, d_skip,
          s_norm_g, ctx_out, bn):
    o_ssd, oc_ssd = ssd_mix(pr, dt, pr_c, dt_c, s_conv_w, s_conv_b, dt_bias, a_log, d_skip, s_norm_g, ctx_out, bn)
    o = [*local_mix(pr, pool_w, pool_scale, conv_w, bn), na_attention(pr, pr_c, na_bias, bn), o_ssd]
    if not ctx_out:
        return o, None
    oc = [*local_mix(pr_c, pool_w, pool_scale, conv_w, bn), ctx_attention(pr_c, bn), oc_ssd]
    return o, oc


TM = 512
TM_PROJ = 1024
TM_GRP = 512


def kernel(x, c, ctx, c_ctx, w_ada, b_ada, g_mix, g_ffn, w_in, w_out, pool_w, pool_scale, conv_w, na_rpb,
           ssd_conv_w, ssd_conv_b, ssd_dt_bias, ssd_a_log, ssd_d, ssd_norm_g, ffn_w_gu, ffn_w_down,
           moe_router, moe_w_gu, moe_w_down, g_final):
    bn, S, d = x.shape
    Lc = ctx.shape[1]
    m, mc = bn * S, bn * Lc
    tpb = S // TM
    x2 = x.reshape(m, d)
    xc2 = ctx.reshape(mc, d)
    c_rows = jnp.concatenate([c, c_ctx[None, :], jnp.zeros((7, d), F32)], axis=0)
    for l in range(DEPTH):
        ctx_out = l < DEPTH - 1
        last = l == DEPTH - 1
        ada = ada_modulation(c_rows, w_ada[l].astype(BF16), b_ada[l][None, :])
        mod = ada[:bn].reshape(bn, 6, d)
        mod_c = ada[bn:bn + 1].reshape(1, 6, d)
        w_in_l = jnp.pad(w_in[l], ((0, 0), (0, D_IN_PAD - D_IN_PROJ))).astype(BF16)
        g_m = g_mix[l][None, :]
        g_f = g_ffn[l][None, :]
        pr, dt = in_proj(x2, g_m, mod, w_in_l, S // TM_PROJ, TM_PROJ)
        pr_c, dt_c = in_proj(xc2, g_m, mod_c, w_in_l, None, min(TM_PROJ, mc))
        na_bias = na_bias_table(na_rpb[l], S // GRID_W)
        o, oc = mixer(pr, dt, pr_c, dt_c, pool_w[l], pool_scale[l], conv_w[l], na_bias, ssd_conv_w[l],
                      ssd_conv_b[l], ssd_dt_bias[l], ssd_a_log[l], ssd_d[l], ssd_norm_g[l], ctx_out, bn)
        w_out_l = w_out[l].astype(BF16)
        j = l // 2
        if l % 2 == 0:
            w_gu = ffn_w_gu[j].astype(BF16)
            w_dn = ffn_w_down[j].astype(BF16)
            x2 = ffn_dense(x2, o, g_f, mod, w_out_l, w_gu, w_dn, tpb, TM)
            if ctx_out:
                xc2 = ffn_dense(xc2, oc, g_f, mod_c, w_out_l, w_gu, w_dn, None, min(TM, mc))
        else:
            w_r = jnp.pad(moe_router[j], ((0, 0), (0, LANES - N_EXPERTS))).astype(BF16)
            w_gu, w_dn = moe_w_gu[j], moe_w_down[j]
            x2 = moe_block(x2, o, g_f, mod, w_out_l, w_r, w_gu, w_dn, tpb, TM, TM_GRP,
                           g_final[None, :] if last else None)
            if ctx_out:
                xc2 = moe_block(xc2, oc, g_f, mod_c, w_out_l, w_r, w_gu, w_dn, None, min(TM, mc), TM_GRP)
            if last:
                return x2.reshape(bn, S, d)
    return final_norm(x2, g_final[None, :], TM).reshape(bn, S, d)
```

```python
import functools
import math

import numpy as np
import jax
import jax.numpy as jnp
from jax import lax
from jax.experimental import pallas as pl
from jax.experimental.pallas import tpu as pltpu

DEPTH = 2
GRID_W = 64
EPS = 1e-6
W_POOL = 256
W_CONV = 256
W_NA = 256
W_SSD = 256
POOL_WINDOWS = (2, 4, 8, 16)
POOL_GROUP = W_POOL // len(POOL_WINDOWS)
NA_HEADS = 4
NA_HEAD_DIM = W_NA // NA_HEADS
WIN_R = 8
WIN_C = 16
SSD_HEAD_DIM = 64
SSD_HEADS = W_SSD // SSD_HEAD_DIM
SSD_GROUPS = 2
SSD_STATE = 128
SSD_CHUNK = 128
SSD_XBC = W_SSD + 2 * SSD_GROUPS * SSD_STATE
D_IN_PROJ = W_POOL + 3 * W_CONV + 3 * W_NA + W_SSD + SSD_XBC + 2 * SSD_HEADS
N_EXPERTS = 8
TOP_K = 2

LANES = 128
D_IN_PAD = -(-D_IN_PROJ // LANES) * LANES
VMEM_LIMIT = 56 * 1024 * 1024
BF16 = jnp.bfloat16
F32 = jnp.float32


def _params(*sem):
    return pltpu.CompilerParams(dimension_semantics=sem, vmem_limit_bytes=VMEM_LIMIT)


def _norm_mod(x, g, scale, shift):
    ms = jnp.mean(x * x, axis=-1, keepdims=True)
    return (x * lax.rsqrt(ms + EPS) * g) * (1.0 + scale) + shift


def _mod_map(tiles_per_batch):
    if tiles_per_batch is None:
        return lambda i, *_: (0, 0, 0)
    return lambda i, *_: (i // tiles_per_batch, 0, 0)


def _resident(shape, index_map):
    return pl.BlockSpec(shape, index_map, pipeline_mode=pl.Buffered(1))


def _ada_kernel(c_ref, w_ref, b_ref, o_ref):
    c = c_ref[...]
    s = c * jax.nn.sigmoid(c)
    o_ref[...] = jnp.dot(s.astype(BF16), w_ref[...], preferred_element_type=F32) + b_ref[...]


def ada_modulation(c_rows, w, b):
    r, d = c_rows.shape
    n = w.shape[1]
    tn = 1536
    return pl.pallas_call(
        _ada_kernel,
        out_shape=jax.ShapeDtypeStruct((r, n), F32),
        grid=(n // tn,),
        in_specs=[pl.BlockSpec((r, d), lambda j: (0, 0)),
                  pl.BlockSpec((d, tn), lambda j: (0, j)),
                  pl.BlockSpec((1, tn), lambda j: (0, j))],
        out_specs=pl.BlockSpec((r, tn), lambda j: (0, j)),
        compiler_params=_params("arbitrary"),
        name="ada_modulation",
    )(c_rows, w, b)


def _in_proj_kernel(x_ref, g_ref, mod_ref, w_ref, o_ref, dt_ref):
    h = _norm_mod(x_ref[...], g_ref[...], mod_ref[0, 1:2, :], mod_ref[0, 0:1, :])
    pr = jnp.dot(h.astype(BF16), w_ref[...], preferred_element_type=F32)
    o_ref[...] = pr.astype(o_ref.dtype)
    dt_ref[...] = pr[:, DT_COL * LANES:(DT_COL + 1) * LANES]


def in_proj(x2, g, mod, w, tiles_per_batch, tm):
    m, d = x2.shape
    n = w.shape[1]
    return pl.pallas_call(
        _in_proj_kernel,
        out_shape=(jax.ShapeDtypeStruct((m, n), BF16), jax.ShapeDtypeStruct((m, LANES), F32)),
        grid=(m // tm,),
        in_specs=[pl.BlockSpec((tm, d), lambda i: (i, 0)),
                  pl.BlockSpec((1, d), lambda i: (0, 0)),
                  pl.BlockSpec((1, 6, d), _mod_map(tiles_per_batch)),
                  _resident((d, n), lambda i: (0, 0))],
        out_specs=(pl.BlockSpec((tm, n), lambda i: (i, 0)), pl.BlockSpec((tm, LANES), lambda i: (i, 0))),
        compiler_params=_params("parallel"),
        name="in_proj",
    )(x2, g, mod, w)


def _mix_residual(x, mod_ref, w_ref, part_refs):
    y, k0 = None, 0
    for p_ref in part_refs:
        k = p_ref.shape[1]
        t = jnp.dot(p_ref[...], w_ref[k0:k0 + k, :], preferred_element_type=F32)
        y = t if y is None else y + t
        k0 += k
    return x + mod_ref[0, 2:3, :] * y


def _part_specs(parts, tm):
    return [pl.BlockSpec((tm, p.shape[1]), lambda i, *_: (i, 0)) for p in parts]


FF_CHUNK = 512


def _swiglu(h, wgu, wd, ff):
    acc = None
    for c0 in range(0, ff, FF_CHUNK):
        c1 = min(c0 + FF_CHUNK, ff)
        gg = jnp.dot(h, wgu(c0, c1), preferred_element_type=F32)
        uu = jnp.dot(h, wgu(ff + c0, ff + c1), preferred_element_type=F32)
        a = (gg * jax.nn.sigmoid(gg) * uu).astype(BF16)
        part = jnp.dot(a, wd(c0, c1), preferred_element_type=F32)
        acc = part if acc is None else acc + part
    return acc


def _ffn_kernel(x_ref, g_ref, mod_ref, wout_ref, wgu_ref, wd_ref, *refs):
    x = _mix_residual(x_ref[...], mod_ref, wout_ref, refs[:-1])
    y_ref = refs[-1]
    h = _norm_mod(x, g_ref[...], mod_ref[0, 4:5, :], mod_ref[0, 3:4, :]).astype(BF16)
    y = _swiglu(h, lambda a, b: wgu_ref[:, a:b], lambda a, b: wd_ref[a:b, :], wd_ref.shape[0])
    y_ref[...] = x + mod_ref[0, 5:6, :] * y


def ffn_dense(x2, parts, g, mod, w_out, w_gu, w_down, tiles_per_batch, tm):
    m, d = x2.shape
    return pl.pallas_call(
        _ffn_kernel,
        out_shape=jax.ShapeDtypeStruct((m, d), F32),
        grid=(m // tm,),
        in_specs=[pl.BlockSpec((tm, d), lambda i: (i, 0)),
                  pl.BlockSpec((1, d), lambda i: (0, 0)),
                  pl.BlockSpec((1, 6, d), _mod_map(tiles_per_batch)),
                  _resident(w_out.shape, lambda i: (0, 0)),
                  _resident(w_gu.shape, lambda i: (0, 0)),
                  _resident(w_down.shape, lambda i: (0, 0))] + _part_specs(parts, tm),
        out_specs=pl.BlockSpec((tm, d), lambda i: (i, 0)),
        compiler_params=_params("parallel"),
        name="ffn_dense",
    )(x2, g, mod, w_out, w_gu, w_down, *parts)


ROUTE_E1, ROUTE_E2, ROUTE_R1, ROUTE_R2, ROUTE_G1, ROUTE_G2 = range(6)
ROUTE_ROWS = 8


def _router_kernel(x_ref, g_ref, mod_ref, wout_ref, wr_ref, *refs):
    part_refs = refs[:-6]
    x1_ref, h_ref, route_ref, route_t_ref, cnt_ref, base_ref = refs[-6:]

    @pl.when(pl.program_id(0) == 0)
    def _():
        base_ref[...] = jnp.zeros_like(base_ref)

    x = _mix_residual(x_ref[...], mod_ref, wout_ref, part_refs)
    x1_ref[...] = x
    h = _norm_mod(x, g_ref[...], mod_ref[0, 4:5, :], mod_ref[0, 3:4, :]).astype(BF16)
    h_ref[...] = h
    tm = h.shape[0]
    logits = jnp.dot(h, wr_ref[...], preferred_element_type=F32)
    lane = lax.broadcasted_iota(jnp.int32, logits.shape, 1)
    neg = jnp.float32(-jnp.inf)
    logits = jnp.where(lane < N_EXPERTS, logits, neg)
    m1 = jnp.max(logits, axis=-1, keepdims=True)
    i1 = jnp.min(jnp.where(logits == m1, lane, LANES), axis=-1, keepdims=True)
    rest = jnp.where(lane == i1, neg, logits)
    m2 = jnp.max(rest, axis=-1, keepdims=True)
    i2 = jnp.min(jnp.where(rest == m2, lane, LANES), axis=-1, keepdims=True)
    e2 = jnp.exp(m2 - m1)
    g1 = 1.0 / (1.0 + e2)
    g2 = e2 / (1.0 + e2)
    sel1, sel2 = lane == i1, lane == i2
    sel = jnp.where(sel1 | sel2, 1.0, 0.0)
    li = lax.broadcasted_iota(jnp.int32, (tm, tm), 0)
    si = lax.broadcasted_iota(jnp.int32, (tm, tm), 1)
    before = jnp.where(si < li, 1.0, 0.0).astype(BF16)
    rank = jnp.dot(before, sel.astype(BF16), preferred_element_type=F32) + base_ref[...]
    r1 = jnp.sum(jnp.where(sel1, rank, 0.0), axis=-1, keepdims=True)
    r2 = jnp.sum(jnp.where(sel2, rank, 0.0), axis=-1, keepdims=True)
    base_ref[...] += jnp.sum(sel, axis=0, keepdims=True)
    cnt_ref[...] = base_ref[...]
    fields = (i1.astype(F32), i2.astype(F32), r1, r2, g1, g2)
    route = jnp.zeros(logits.shape, F32)
    for k, v in enumerate(fields):
        route = jnp.where(lane == k, v, route)
    route_ref[...] = route
    route_t_ref[...] = route.T[:ROUTE_ROWS, :]


def moe_router(x2, parts, g, mod, w_out, w_router, tiles_per_batch, tm):
    m, d = x2.shape
    return pl.pallas_call(
        _router_kernel,
        out_shape=(jax.ShapeDtypeStruct((m, d), F32), jax.ShapeDtypeStruct((m, d), BF16),
                   jax.ShapeDtypeStruct((m, LANES), F32), jax.ShapeDtypeStruct((ROUTE_ROWS, m), F32),
                   jax.ShapeDtypeStruct((1, LANES), F32)),
        grid=(m // tm,),
        in_specs=[pl.BlockSpec((tm, d), lambda i: (i, 0)),
                  pl.BlockSpec((1, d), lambda i: (0, 0)),
                  pl.BlockSpec((1, 6, d), _mod_map(tiles_per_batch)),
                  _resident(w_out.shape, lambda i: (0, 0)),
                  pl.BlockSpec((d, LANES), lambda i: (0, 0))] + _part_specs(parts, tm),
        out_specs=(pl.BlockSpec((tm, d), lambda i: (i, 0)),
                   pl.BlockSpec((tm, d), lambda i: (i, 0)),
                   pl.BlockSpec((tm, LANES), lambda i: (i, 0)),
                   pl.BlockSpec((ROUTE_ROWS, tm), lambda i: (0, i)),
                   pl.BlockSpec((1, LANES), lambda i: (0, 0))),
        scratch_shapes=[pltpu.VMEM((1, LANES), F32)],
        compiler_params=_params("arbitrary"),
        name="moe_router",
    )(x2, g, mod, w_out, w_router, *parts)


def _moe_kernel(tile_ref, exp_ref, start_ref, end_ref, xg_ref, wgu_ref, wd_ref, y_ref, wgu_b, wd_b):
    w = pl.program_id(0)
    tm = xg_ref.shape[0]
    start, end = start_ref[w], end_ref[w]
    t0 = tile_ref[w] * tm

    @pl.when((w == 0) | (exp_ref[w] != exp_ref[jnp.maximum(w - 1, 0)]))
    def _():
        wgu_b[...] = wgu_ref[0].astype(BF16)
        wd_b[...] = wd_ref[0].astype(BF16)

    @pl.when(end > start)
    def _():
        y = _swiglu(xg_ref[...], lambda a, b: wgu_b[:, a:b], lambda a, b: wd_b[a:b, :], wd_b.shape[0])
        y = y.astype(y_ref.dtype)

        @pl.when(start == t0)
        def _():
            y_ref[...] = y

        @pl.when(start != t0)
        def _():
            row = t0 + lax.broadcasted_iota(jnp.int32, (tm, 1), 0)
            y_ref[...] = jnp.where(row >= start, y, y_ref[...])


def moe_grouped(item_tile, item_expert, item_start, item_end, xg, w_gu, w_down, tm):
    p, d = xg.shape
    grid_spec = pltpu.PrefetchScalarGridSpec(
        num_scalar_prefetch=4,
        grid=(item_tile.shape[0],),
        in_specs=[pl.BlockSpec((tm, d), lambda w, it, ie, s, e: (it[w], 0)),
                  _resident((1,) + w_gu.shape[1:], lambda w, it, ie, s, e: (ie[w], 0, 0)),
                  _resident((1,) + w_down.shape[1:], lambda w, it, ie, s, e: (ie[w], 0, 0))],
        out_specs=pl.BlockSpec((tm, d), lambda w, it, ie, s, e: (it[w], 0)),
        scratch_shapes=[pltpu.VMEM(w_gu.shape[1:], BF16), pltpu.VMEM(w_down.shape[1:], BF16)],
    )
    return pl.pallas_call(
        _moe_kernel,
        out_shape=jax.ShapeDtypeStruct((p, d), BF16),
        grid_spec=grid_spec,
        compiler_params=_params("arbitrary"),
        name="moe_grouped",
    )(item_tile, item_expert, item_start, item_end, xg, w_gu, w_down)


def _moe_combine_kernel(x_ref, ya_ref, yb_ref, route_ref, mod_ref, *refs):
    o_ref = refs[-1]
    r = route_ref[...]
    y = (r[:, ROUTE_G1:ROUTE_G1 + 1] * ya_ref[...].astype(F32) + r[:, ROUTE_G2:ROUTE_G2 + 1] * yb_ref[...].astype(F32))
    x = x_ref[...] + mod_ref[0, 5:6, :] * y
    if len(refs) == 2:
        ms = jnp.mean(x * x, axis=-1, keepdims=True)
        x = x * lax.rsqrt(ms + EPS) * refs[0][...]
    o_ref[...] = x


def moe_combine(x2, ya, yb, route, mod, tiles_per_batch, tm, g_final=None):
    m, d = x2.shape
    row = pl.BlockSpec((tm, d), lambda i: (i, 0))
    specs = [row, row, row, pl.BlockSpec((tm, LANES), lambda i: (i, 0)),
             pl.BlockSpec((1, 6, d), _mod_map(tiles_per_batch))]
    args = [x2, ya, yb, route, mod]
    if g_final is not None:
        specs.append(pl.BlockSpec((1, d), lambda i: (0, 0)))
        args.append(g_final)
    return pl.pallas_call(
        _moe_combine_kernel,
        out_shape=jax.ShapeDtypeStruct((m, d), F32),
        grid=(m // tm,),
        in_specs=specs,
        out_specs=row,
        compiler_params=_params("parallel"),
        name="moe_combine",
    )(*args)


def moe_block(x2, parts, g, mod, w_out, w_router, w_gu, w_down, tiles_per_batch, tm_tok, tm_grp, g_final=None):
    m, d = x2.shape
    x2, h, route, route_t, cnt = moe_router(x2, parts, g, mod, w_out, w_router, tiles_per_batch, tm_tok)
    cnt = cnt[0, :N_EXPERTS].astype(jnp.int32)
    p = m * TOP_K
    off = jnp.cumsum(cnt) - cnt
    field = lambda k: route_t[k].astype(jnp.int32)

    def row_of(e, r):
        for k in range(N_EXPERTS):
            r = r + jnp.where(e == k, off[k], 0)
        return r

    d1 = row_of(field(ROUTE_E1), field(ROUTE_R1))
    d2 = row_of(field(ROUTE_E2), field(ROUTE_R2))
    tok = jnp.arange(m, dtype=jnp.int32)
    _, src = lax.sort_key_val(jnp.concatenate([d1, d2]), jnp.concatenate([tok, tok]))
    n_row_tiles = p // tm_grp
    start = jnp.sort(jnp.concatenate([jnp.arange(n_row_tiles, dtype=jnp.int32) * tm_grp, off]))
    end = jnp.concatenate([start[1:], jnp.full((1,), p, jnp.int32)])
    item_tile = jnp.minimum(start // tm_grp, n_row_tiles - 1)
    item_expert = jnp.sum((off[None, :] <= start[:, None]).astype(jnp.int32), axis=1) - 1
    rows = lambda a, idx: a.at[idx].get(mode="promise_in_bounds")
    yg = moe_grouped(item_tile, item_expert, start, end, rows(h, src), w_gu, w_down, tm_grp)
    return moe_combine(x2, rows(yg, d1), rows(yg, d2), route, mod, tiles_per_batch, tm_tok, g_final)


def _final_norm_kernel(x_ref, g_ref, o_ref):
    x = x_ref[...]
    ms = jnp.mean(x * x, axis=-1, keepdims=True)
    o_ref[...] = x * lax.rsqrt(ms + EPS) * g_ref[...]


def final_norm(x2, g, tm):
    m, d = x2.shape
    return pl.pallas_call(
        _final_norm_kernel,
        out_shape=jax.ShapeDtypeStruct((m, d), F32),
        grid=(m // tm,),
        in_specs=[pl.BlockSpec((tm, d), lambda i: (i, 0)), pl.BlockSpec((1, d), lambda i: (0, 0))],
        out_specs=pl.BlockSpec((tm, d), lambda i: (i, 0)),
        compiler_params=_params("parallel"),
        name="final_norm",
    )(x2, g)


NA_QROWS = 4
NA_KROWS = NA_QROWS + WIN_R
Q_COL, K_COL, V_COL = 4, 5, 6


def _na_key_start(j, rows):
    return np.clip(j * NA_QROWS - WIN_R // 2, 0, rows - NA_KROWS)


def _na_bias_index(rows):
    nblk = rows // NA_QROWS
    pats = []
    for j in range(nblk):
        start = _na_key_start(j, rows)
        r = j * NA_QROWS + np.arange(NA_QROWS)
        sr = np.clip(r - WIN_R // 2, 0, rows - WIN_R)
        kr = start + np.arange(NA_KROWS)
        rvalid = (kr[None, :] >= sr[:, None]) & (kr[None, :] < sr[:, None] + WIN_R)
        ri = np.clip(kr[None, :] - r[:, None] + WIN_R - 1, 0, 2 * WIN_R - 2)
        pats.append((ri, rvalid))
    for j in range(2, nblk - 1):
        assert all(np.array_equal(a, b) for a, b in zip(pats[1], pats[j]))
    sel = [pats[0], pats[1], pats[nblk - 1]]
    ri = np.stack([p[0] for p in sel])
    rvalid = np.stack([p[1] for p in sel])
    c = np.arange(GRID_W)
    sc = np.clip(c - WIN_C // 2, 0, GRID_W - WIN_C)
    cvalid = (c[None, :] >= sc[:, None]) & (c[None, :] < sc[:, None] + WIN_C)
    ci = np.clip(c[None, :] - c[:, None] + WIN_C - 1, 0, 2 * WIN_C - 2)
    c_onehot = (ci[..., None] == np.arange(2 * WIN_C - 1)).astype(np.float32)
    valid = rvalid[:, :, None, :, None] & cvalid[None, None, :, None, :]
    return ri, c_onehot, valid


def na_bias_table(rpb, rows):
    ri, c_onehot, valid = _na_bias_index(rows)
    toep = jnp.einsum('hrd,qkd->hrqk', rpb, c_onehot, precision=lax.Precision.HIGHEST)
    b = jnp.take(toep, ri.reshape(-1), axis=1).reshape((NA_HEADS,) + ri.shape + (GRID_W, GRID_W))
    b = jnp.transpose(b, (1, 0, 2, 4, 3, 5))
    b = jnp.where(valid[:, None], b, -jnp.inf)
    return b.reshape(3, NA_HEADS, NA_QROWS * GRID_W, NA_KROWS * GRID_W).astype(F32)


def _attend_heads(q, key_sets, bias_fn):
    n, w = q.shape
    lane = lax.broadcasted_iota(jnp.int32, (1, w), 1)
    head_masks = [(lane >= h * NA_HEAD_DIM) & (lane < (h + 1) * NA_HEAD_DIM) for h in range(NA_HEADS)]
    all_scores = []
    for h, mh in enumerate(head_masks):
        qh = jnp.where(mh, q, 0.0).astype(BF16)
        scores = []
        for i, (k, _) in enumerate(key_sets):
            s = lax.dot_general(qh, k, (((1,), (1,)), ((), ())), preferred_element_type=F32)
            b = bias_fn(i, h)
            scores.append(s if b is None else s + b)
        all_scores.append(scores)
    all_probs = []
    for scores in all_scores:
        m = scores[0].max(axis=-1, keepdims=True)
        for s in scores[1:]:
            m = jnp.maximum(m, s.max(axis=-1, keepdims=True))
        denom = jnp.zeros((n, 1), F32)
        probs = []
        for s in scores:
            p = jnp.exp(s - m)
            denom = denom + p.sum(axis=-1, keepdims=True)
            probs.append(p.astype(BF16))
        all_probs.append((probs, denom))
    out = jnp.zeros((n, w), F32)
    for mh, (probs, denom) in zip(head_masks, all_probs):
        acc = jnp.zeros((n, w), F32)
        for p, (_, v) in zip(probs, key_sets):
            acc = acc + jnp.dot(p, v, preferred_element_type=F32)
        out = out + jnp.where(mh, acc / denom, 0.0)
    return out


def _na_kernel(q_ref, k_ref, v_ref, kc_ref, vc_ref, bias_ref, o_ref, *, rows):
    j = pl.program_id(1)
    start = jnp.clip(j * NA_QROWS - WIN_R // 2, 0, rows - NA_KROWS)
    t0 = pl.multiple_of(start * GRID_W, GRID_W)
    nk = NA_KROWS * GRID_W
    kw = k_ref[pl.ds(t0, nk), :]
    vw = v_ref[pl.ds(t0, nk), :]
    q = q_ref[...] * (1.0 / math.sqrt(NA_HEAD_DIM))
    o = _attend_heads(q, [(kw, vw), (kc_ref[...], vc_ref[...])], lambda i, h: bias_ref[0, h] if i == 0 else None)
    o_ref[...] = o.astype(o_ref.dtype)


def na_attention(pr, pr_c, bias, bn):
    S, Lc = pr.shape[0] // bn, pr_c.shape[0] // bn
    rows = S // GRID_W
    nblk = rows // NA_QROWS
    nq = NA_QROWS * GRID_W

    def pat(b, j):
        return (jnp.where(j == 0, 0, jnp.where(j == nblk - 1, 2, 1)), 0, 0, 0)

    return pl.pallas_call(
        functools.partial(_na_kernel, rows=rows),
        out_shape=jax.ShapeDtypeStruct((bn * S, W_NA), BF16),
        grid=(bn, nblk),
        in_specs=[pl.BlockSpec((nq, W_NA), lambda b, j: (b * nblk + j, Q_COL)),
                  pl.BlockSpec((S, W_NA), lambda b, j: (b, K_COL)),
                  pl.BlockSpec((S, W_NA), lambda b, j: (b, V_COL)),
                  pl.BlockSpec((Lc, W_NA), lambda b, j: (b, K_COL)),
                  pl.BlockSpec((Lc, W_NA), lambda b, j: (b, V_COL)),
                  pl.BlockSpec((1,) + bias.shape[1:], pat)],
        out_specs=pl.BlockSpec((nq, W_NA), lambda b, j: (b * nblk + j, 0)),
        compiler_params=_params("parallel", "arbitrary"),
        name="na_attention",
    )(pr, pr, pr, pr_c, pr_c, bias)


def _ctx_attn_kernel(q_ref, k_ref, v_ref, o_ref):
    q = q_ref[...] * (1.0 / math.sqrt(NA_HEAD_DIM))
    o = _attend_heads(q, [(k_ref[...], v_ref[...])], lambda i, h: None)
    o_ref[...] = o.astype(o_ref.dtype)


def ctx_attention(pr_c, bn):
    Lc = pr_c.shape[0] // bn
    return pl.pallas_call(
        _ctx_attn_kernel,
        out_shape=jax.ShapeDtypeStruct((bn * Lc, W_NA), BF16),
        grid=(bn,),
        in_specs=[pl.BlockSpec((Lc, W_NA), lambda b: (b, Q_COL)),
                  pl.BlockSpec((Lc, W_NA), lambda b: (b, K_COL)),
                  pl.BlockSpec((Lc, W_NA), lambda b: (b, V_COL))],
        out_specs=pl.BlockSpec((Lc, W_NA), lambda b: (b, 0)),
        compiler_params=_params("parallel"),
        name="ctx_attention",
    )(pr_c, pr_c, pr_c)


POOL_COL, CONV_H_COL, CONV_B_COL, CONV_C_COL = 0, 1, 2, 3


def _shift_rows(x, k, row):
    n = x.shape[0]
    y = pltpu.roll(x, k % n, 0)
    return jnp.where(row < k, 0.0, y) if k > 0 else jnp.where(row >= n + k, 0.0, y)


def _local_mix_kernel(u_ref, h_ref, bg_ref, cg_ref, pw_ref, ps_ref, cw_ref, op_ref, oc_ref, *, seq):
    row = lax.broadcasted_iota(jnp.int32, (seq, 1), 0)
    lane = lax.broadcasted_iota(jnp.int32, (1, W_POOL), 1)
    u = u_ref[...].astype(F32)
    trailing, leading = {1: u}, {1: u}
    for w in (1, 2, 4):
        trailing[2 * w] = trailing[w] + _shift_rows(trailing[w], w, row)
        leading[2 * w] = leading[w] + _shift_rows(leading[w], -w, row)
    rowf = row.astype(F32)
    win_sum = jnp.zeros_like(u)
    cnt = jnp.zeros_like(u)
    for g, win in enumerate(POOL_WINDOWS):
        half = win // 2
        mg = (lane >= g * POOL_GROUP) & (lane < (g + 1) * POOL_GROUP)
        s = _shift_rows(trailing[half], 1, row) + leading[half]
        n = jnp.minimum(rowf + half, float(seq)) - jnp.maximum(rowf - half, 0.0)
        win_sum = jnp.where(mg, s, win_sum)
        cnt = jnp.where(mg, n, cnt)
    p = win_sum / cnt - u
    pooled = jnp.dot(p.astype(BF16), pw_ref[...], preferred_element_type=F32) * ps_ref[...]
    op_ref[...] = pooled.astype(op_ref.dtype)
    v = cg_ref[...].astype(F32) * h_ref[...].astype(F32)
    conv = (_shift_rows(v, 1, row) * cw_ref[0:1, :] + v * cw_ref[1:2, :] + _shift_rows(v, -1, row) * cw_ref[2:3, :])
    oc_ref[...] = (bg_ref[...].astype(F32) * conv).astype(oc_ref.dtype)


def local_mix(pr, pool_w, pool_scale, conv_w, bn):
    seq = pr.shape[0] // bn
    pw = jax.scipy.linalg.block_diag(*[pool_w[g] for g in range(len(POOL_WINDOWS))]).astype(BF16)
    blk = lambda col: pl.BlockSpec((seq, W_POOL), lambda b: (b, col))
    full = lambda a: pl.BlockSpec(a.shape, lambda b: (0,) * a.ndim)
    args = (pw, pool_scale[None, :], conv_w)
    return pl.pallas_call(
        functools.partial(_local_mix_kernel, seq=seq),
        out_shape=(jax.ShapeDtypeStruct((bn * seq, W_POOL), BF16), jax.ShapeDtypeStruct((bn * seq, W_CONV), BF16)),
        grid=(bn,),
        in_specs=[blk(POOL_COL), blk(CONV_H_COL), blk(CONV_B_COL), blk(CONV_C_COL)] + [full(a) for a in args],
        out_specs=(pl.BlockSpec((seq, W_POOL), lambda b: (b, 0)),
                   pl.BlockSpec((seq, W_CONV), lambda b: (b, 0))),
        compiler_params=_params("parallel"),
        name="local_mix",
    )(pr, pr, pr, pr, *args)


Z_COL, XS_COL, BS_COL, CS_COL = 7, 8, 9, 10
DT_COL = 22
HEADS_PER_GROUP = SSD_HEADS // SSD_GROUPS
GROUP_W = HEADS_PER_GROUP * SSD_HEAD_DIM


def _head_mask(h):
    lane = lax.broadcasted_iota(jnp.int32, (1, W_SSD), 1)
    return (lane >= h * SSD_HEAD_DIM) & (lane < (h + 1) * SSD_HEAD_DIM)


def _expand_heads(v, d):
    out = jnp.zeros((v.shape[0], W_SSD), F32)
    for h in range(SSD_HEADS):
        k = d * SSD_HEADS + h
        out = jnp.where(_head_mask(h), v[:, k:k + 1], out)
    return out


def _ssd_chunk_terms(c, xs_ref, b_ref, c_ref, dt_ref, la_ref, dsk_ref, y_ref, upd_ref, dec_ref, need_y):
    T = SSD_CHUNK
    r0 = pl.multiple_of(c * T, T)
    xs = xs_ref[pl.ds(r0, T), :]
    bm = b_ref[pl.ds(r0, T), :]
    cm = c_ref[pl.ds(r0, T), :]
    dt = dt_ref[pl.ds(r0, T), :]
    la = la_ref[pl.ds(r0, T), :]
    li = lax.broadcasted_iota(jnp.int32, (T, T), 0)
    si = lax.broadcasted_iota(jnp.int32, (T, T), 1)
    causal = si <= li
    prefix = jnp.dot(causal.astype(F32), la, precision=lax.Precision.HIGHEST, preferred_element_type=F32)
    total = prefix[T - 1:T, :]
    lane = lax.broadcasted_iota(jnp.int32, (1, LANES), 1)
    acum = jnp.where(lane < SSD_HEADS, prefix, total - prefix + la)
    bt = bm.astype(F32).T.astype(BF16)
    if need_y:
        acum_t = acum.T
        cb = [lax.dot_general(cm[:, g * SSD_STATE:(g + 1) * SSD_STATE], bm[:, g * SSD_STATE:(g + 1) * SSD_STATE],
                              (((1,), (1,)), ((), ())), preferred_element_type=F32) for g in range(SSD_GROUPS)]
        y = jnp.zeros((T, W_SSD), F32)
    per_dir = []
    for d in range(2):
        acum_e = _expand_heads(acum, d)
        tot_e = _expand_heads(total, d)
        xdt = xs * _expand_heads(dt, d)
        xw = (xdt * jnp.exp(tot_e - acum_e)).astype(BF16)
        dec_ref[c, d, 0:T, :] = jnp.exp(acum_e)
        dec_ref[c, d, T:T + 1, :] = jnp.exp(tot_e)
        scores = []
        if need_y:
            mask = causal if d == 0 else (si >= li)
            for h in range(SSD_HEADS):
                k = d * SSD_HEADS + h
                decay = jnp.exp(jnp.where(mask, acum[:, k:k + 1] - acum_t[k:k + 1, :], -jnp.inf))
                scores.append((cb[h // HEADS_PER_GROUP] * decay).astype(BF16))
        per_dir.append((xw, xdt.astype(BF16), scores))
    for d, (xw, xdt_b, scores) in enumerate(per_dir):
        upd = [jnp.dot(bt[g * SSD_STATE:(g + 1) * SSD_STATE, :], xw[:, g * GROUP_W:(g + 1) * GROUP_W],
                       preferred_element_type=F32) for g in range(SSD_GROUPS)]
        upd_ref[c, d] = jnp.concatenate(upd, axis=1)
        for h, sc in enumerate(scores):
            y = y + jnp.where(_head_mask(h), jnp.dot(sc, xdt_b, preferred_element_type=F32), 0.0)
    if need_y:
        y_ref[pl.ds(r0, T), :] = xs * dsk_ref[...] + y


def _ssd_chunk_state(c, d, c_ref, y_ref, upd_ref, dec_ref, st_ref, need_y):
    T = SSD_CHUNK
    r0 = pl.multiple_of(c * T, T)
    st = st_ref[d]
    if need_y:
        cm = c_ref[pl.ds(r0, T), :]
        st_b = st.astype(BF16)
        ys = [jnp.dot(cm[:, g * SSD_STATE:(g + 1) * SSD_STATE], st_b[:, g * GROUP_W:(g + 1) * GROUP_W],
                      preferred_element_type=F32) for g in range(SSD_GROUPS)]
        y_ref[pl.ds(r0, T), :] += jnp.concatenate(ys, axis=1) * dec_ref[c, d, 0:T, :]
    st_ref[d] = dec_ref[c, d, T:T + 1, :] * st + upd_ref[c, d]


def _ssd_kernel(z_ref, xs_in, bs_in, cs_in, dtr_ref, cw_ref, cb_ref, dtb_ref, alog_ref, dsk_ref, ng_ref, h0_ref,
                *refs, seq, need_y):
    if need_y:
        o_ref, hT_ref, xs_ref, b_ref, c_ref, dt_ref, la_ref, upd_ref, dec_ref, st_ref, y_ref = refs
    else:
        hT_ref, xs_ref, b_ref, c_ref, dt_ref, la_ref, upd_ref, dec_ref, st_ref = refs
        y_ref = None
    nc = seq // SSD_CHUNK
    row = lax.broadcasted_iota(jnp.int32, (seq, 1), 0)
    for gi, (src, dst) in enumerate(((xs_in, xs_ref), (bs_in, b_ref), (cs_in, c_ref))):
        sl = slice(gi * W_SSD, (gi + 1) * W_SSD)
        x = src[...].astype(F32)
        cv = (_shift_rows(x, 1, row) * cw_ref[0:1, sl] + x * cw_ref[1:2, sl]
              + _shift_rows(x, -1, row) * cw_ref[2:3, sl] + cb_ref[:, sl])
        dst[...] = (cv * jax.nn.sigmoid(cv)).astype(dst.dtype)
    dtv = dtr_ref[...] + dtb_ref[...]
    dt = jnp.maximum(dtv, 0.0) + jnp.log1p(jnp.exp(-jnp.abs(dtv)))
    dt_ref[...] = dt
    la_ref[...] = dt * (-jnp.exp(alog_ref[...]))
    st_ref[...] = h0_ref[0]

    def terms(c, carry):
        _ssd_chunk_terms(c, xs_ref, b_ref, c_ref, dt_ref, la_ref, dsk_ref, y_ref, upd_ref, dec_ref, need_y)
        return carry

    lax.fori_loop(0, nc, terms, 0, unroll=2)

    def recur(i, carry):
        _ssd_chunk_state(i, 0, c_ref, y_ref, upd_ref, dec_ref, st_ref, need_y)
        _ssd_chunk_state(nc - 1 - i, 1, c_ref, y_ref, upd_ref, dec_ref, st_ref, need_y)
        return carry

    lax.fori_loop(0, nc, recur, 0, unroll=2)
    hT_ref[0] = st_ref[...]
    if need_y:
        z = z_ref[...].astype(F32)
        yz = y_ref[...] * (z * jax.nn.sigmoid(z))
        ms = jnp.mean(yz * yz, axis=-1, keepdims=True)
        o_ref[...] = (yz * lax.rsqrt(ms + EPS) * ng_ref[...]).astype(o_ref.dtype)


def ssd_scan(pr, dt, h0, conv_w, conv_b, dt_bias, a_log, d_skip, norm_g, need_y):
    bn = h0.shape[0]
    seq = pr.shape[0] // bn
    nc = seq // SSD_CHUNK
    pad = LANES - 2 * SSD_HEADS
    dtb = jnp.pad(dt_bias.reshape(1, -1), ((0, 0), (0, pad)))
    alog = jnp.pad(a_log.reshape(1, -1), ((0, 0), (0, pad)))
    dsk = jnp.repeat(d_skip, SSD_HEAD_DIM)[None, :]
    blk = lambda col: pl.BlockSpec((seq, W_SSD), lambda b: (b, col))
    full = lambda a: pl.BlockSpec(a.shape, lambda b: (0,) * a.ndim)
    st_shape = (1, 2, SSD_STATE, W_SSD)
    st_spec = pl.BlockSpec(st_shape, lambda b: (b, 0, 0, 0))
    args = (conv_w, conv_b[None, :], dtb, alog, dsk, norm_g[None, :])
    out_shape = [jax.ShapeDtypeStruct((bn,) + st_shape[1:], F32)]
    out_specs = [st_spec]
    scratch = [pltpu.VMEM((seq, W_SSD), F32),
               pltpu.VMEM((seq, W_SSD), BF16), pltpu.VMEM((seq, W_SSD), BF16),
               pltpu.VMEM((seq, LANES), F32), pltpu.VMEM((seq, LANES), F32),
               pltpu.VMEM((nc, 2, SSD_STATE, W_SSD), F32),
               pltpu.VMEM((nc, 2, SSD_CHUNK + 8, W_SSD), F32),
               pltpu.VMEM(st_shape[1:], F32)]
    if need_y:
        out_shape.insert(0, jax.ShapeDtypeStruct((bn * seq, W_SSD), BF16))
        out_specs.insert(0, pl.BlockSpec((seq, W_SSD), lambda b: (b, 0)))
        scratch.append(pltpu.VMEM((seq, W_SSD), F32))
    res = pl.pallas_call(
        functools.partial(_ssd_kernel, seq=seq, need_y=need_y),
        out_shape=out_shape,
        grid=(bn,),
        in_specs=[blk(Z_COL), blk(XS_COL), blk(BS_COL), blk(CS_COL),
                  pl.BlockSpec((seq, LANES), lambda b: (b, 0))]
                 + [full(a) for a in args] + [st_spec],
        out_specs=out_specs,
        scratch_shapes=scratch,
        compiler_params=_params("parallel"),
        name="ssd_scan",
    )(pr, pr, pr, pr, dt, *args, h0)
    return (res[0], res[1]) if need_y else (None, res[0])


def ssd_mix(pr, dt, pr_c, dt_c, conv_w, conv_b, dt_bias, a_log, d_skip, norm_g, ctx_out, bn):
    h0 = jnp.zeros((bn, 2, SSD_STATE, W_SSD), F32)
    oc, hc = ssd_scan(pr_c, dt_c, h0, conv_w, conv_b, dt_bias, a_log, d_skip, norm_g, ctx_out)
    o, _ = ssd_scan(pr, dt, hc, conv_w, conv_b, dt_bias, a_log, d_skip, norm_g, True)
    return o, oc


def mixer(pr, dt, pr_c, dt_c, pool_w, pool_scale, conv_w, na_bias, s_conv_w, s_conv_b, dt_bias, a_log, d_skip,
          s_norm_g, ctx_out, bn):
    o_ssd, oc_ssd = ssd_mix(pr, dt, pr_c, dt_c, s_conv_w, s_conv_b, dt_bias, a_log, d_skip, s_norm_g, ctx_out, bn)
    o = [*local_mix(pr, pool_w, pool_scale, conv_w, bn), na_attention(pr, pr_c, na_bias, bn), o_ssd]
    if not ctx_out:
        return o, None
    oc = [*local_mix(pr_c, pool_w, pool_scale, conv_w, bn), ctx_attention(pr_c, bn), oc_ssd]
    return o, oc


TM = 512
TM_PROJ = 1024
TM_GRP = 512


def kernel(x, c, ctx, c_ctx, w_ada, b_ada, g_mix, g_ffn, w_in, w_out, pool_w, pool_scale, conv_w, na_rpb,
           ssd_conv_w, ssd_conv_b, ssd_dt_bias, ssd_a_log, ssd_d, ssd_norm_g, ffn_w_gu, ffn_w_down,
           moe_router, moe_w_gu, moe_w_down, g_final):
    bn, S, d = x.shape
    Lc = ctx.shape[1]
    m, mc = bn * S, bn * Lc
    tpb = S // TM
    x2 = x.reshape(m, d)
    xc2 = ctx.reshape(mc, d)
    c_rows = jnp.concatenate([c, c_ctx[None, :], jnp.zeros((7, d), F32)], axis=0)
    for l in range(DEPTH):
        ctx_out = l < DEPTH - 1
        last = l == DEPTH - 1
        ada = ada_modulation(c_rows, w_ada[l].astype(BF16), b_ada[l][None, :])
        mod = ada[:bn].reshape(bn, 6, d)
        mod_c = ada[bn:bn + 1].reshape(1, 6, d)
        w_in_l = jnp.pad(w_in[l], ((0, 0), (0, D_IN_PAD - D_IN_PROJ))).astype(BF16)
        g_m = g_mix[l][None, :]
        g_f = g_ffn[l][None, :]
        pr, dt = in_proj(x2, g_m, mod, w_in_l, S // TM_PROJ, TM_PROJ)
        pr_c, dt_c = in_proj(xc2, g_m, mod_c, w_in_l, None, min(TM_PROJ, mc))
        na_bias = na_bias_table(na_rpb[l], S // GRID_W)
        o, oc = mixer(pr, dt, pr_c, dt_c, pool_w[l], pool_scale[l], conv_w[l], na_bias, ssd_conv_w[l],
                      ssd_conv_b[l], ssd_dt_bias[l], ssd_a_log[l], ssd_d[l], ssd_norm_g[l], ctx_out, bn)
        w_out_l = w_out[l].astype(BF16)
        j = l // 2
        if l % 2 == 0:
            w_gu = ffn_w_gu[j].astype(BF16)
            w_dn = ffn_w_down[j].astype(BF16)
            x2 = ffn_dense(x2, o, g_f, mod, w_out_l, w_gu, w_dn, tpb, TM)
            if ctx_out:
                xc2 = ffn_dense(xc2, oc, g_f, mod_c, w_out_l, w_gu, w_dn, None, min(TM, mc))
        else:
            w_r = jnp.pad(moe_router[j], ((0, 0), (0, LANES - N_EXPERTS))).astype(BF16)
            w_gu, w_dn = moe_w_gu[j], moe_w_down[j]
            x2 = moe_block(x2, o, g_f, mod, w_out_l, w_r, w_gu, w_dn, tpb, TM, TM_GRP,
                           g_final[None, :] if last else None)
            if ctx_out:
                xc2 = moe_block(xc2, oc, g_f, mod_c, w_out_l, w_r, w_gu, w_dn, None, min(TM, mc), TM_GRP)
            if last:
                return x2.reshape(bn, S, d)
    return final_norm(x2, g_final[None, :], TM).reshape(bn, S, d)
```

```python
import functools
import math

import numpy as np
import jax
import jax.numpy as jnp
from jax import lax
from jax.experimental import pallas as pl
from jax.experimental.pallas import tpu as pltpu

DEPTH = 2
GRID_W = 64
EPS = 1e-6
W_POOL = 256
W_CONV = 256
W_NA = 256
W_SSD = 256
POOL_WINDOWS = (2, 4, 8, 16)
POOL_GROUP = W_POOL // len(POOL_WINDOWS)
NA_HEADS = 4
NA_HEAD_DIM = W_NA // NA_HEADS
WIN_R = 8
WIN_C = 16
SSD_HEAD_DIM = 64
SSD_HEADS = W_SSD // SSD_HEAD_DIM
SSD_GROUPS = 2
SSD_STATE = 128
SSD_CHUNK = 128
SSD_XBC = W_SSD + 2 * SSD_GROUPS * SSD_STATE
D_IN_PROJ = W_POOL + 3 * W_CONV + 3 * W_NA + W_SSD + SSD_XBC + 2 * SSD_HEADS
N_EXPERTS = 8
TOP_K = 2

LANES = 128
D_IN_PAD = -(-D_IN_PROJ // LANES) * LANES
VMEM_LIMIT = 56 * 1024 * 1024
BF16 = jnp.bfloat16
F32 = jnp.float32


def _params(*sem):
    return pltpu.CompilerParams(dimension_semantics=sem, vmem_limit_bytes=VMEM_LIMIT)


def _norm_mod(x, g, scale, shift):
    ms = jnp.mean(x * x, axis=-1, keepdims=True)
    return (x * lax.rsqrt(ms + EPS) * g) * (1.0 + scale) + shift


def _mod_map(tiles_per_batch):
    if tiles_per_batch is None:
        return lambda i, *_: (0, 0, 0)
    return lambda i, *_: (i // tiles_per_batch, 0, 0)


def _resident(shape, index_map):
    return pl.BlockSpec(shape, index_map, pipeline_mode=pl.Buffered(1))


def _ada_kernel(c_ref, w_ref, b_ref, o_ref):
    c = c_ref[...]
    s = c * jax.nn.sigmoid(c)
    o_ref[...] = jnp.dot(s.astype(BF16), w_ref[...], preferred_element_type=F32) + b_ref[...]


def ada_modulation(c_rows, w, b):
    r, d = c_rows.shape
    n = w.shape[1]
    tn = 1536
    return pl.pallas_call(
        _ada_kernel,
        out_shape=jax.ShapeDtypeStruct((r, n), F32),
        grid=(n // tn,),
        in_specs=[pl.BlockSpec((r, d), lambda j: (0, 0)),
                  pl.BlockSpec((d, tn), lambda j: (0, j)),
                  pl.BlockSpec((1, tn), lambda j: (0, j))],
        out_specs=pl.BlockSpec((r, tn), lambda j: (0, j)),
        compiler_params=_params("arbitrary"),
        name="ada_modulation",
    )(c_rows, w, b)


def _in_proj_kernel(x_ref, g_ref, mod_ref, w_ref, o_ref, dt_ref):
    h = _norm_mod(x_ref[...], g_ref[...], mod_ref[0, 1:2, :], mod_ref[0, 0:1, :])
    pr = jnp.dot(h.astype(BF16), w_ref[...], preferred_element_type=F32)
    o_ref[...] = pr.astype(o_ref.dtype)
    dt_ref[...] = pr[:, DT_COL * LANES:(DT_COL + 1) * LANES]


def in_proj(x2, g, mod, w, tiles_per_batch, tm):
    m, d = x2.shape
    n = w.shape[1]
    return pl.pallas_call(
        _in_proj_kernel,
        out_shape=(jax.ShapeDtypeStruct((m, n), BF16), jax.ShapeDtypeStruct((m, LANES), F32)),
        grid=(m // tm,),
        in_specs=[pl.BlockSpec((tm, d), lambda i: (i, 0)),
                  pl.BlockSpec((1, d), lambda i: (0, 0)),
                  pl.BlockSpec((1, 6, d), _mod_map(tiles_per_batch)),
                  _resident((d, n), lambda i: (0, 0))],
        out_specs=(pl.BlockSpec((tm, n), lambda i: (i, 0)), pl.BlockSpec((tm, LANES), lambda i: (i, 0))),
        compiler_params=_params("parallel"),
        name="in_proj",
    )(x2, g, mod, w)


def _mix_residual(x, mod_ref, w_ref, part_refs):
    y, k0 = None, 0
    for p_ref in part_refs:
        k = p_ref.shape[1]
        t = jnp.dot(p_ref[...], w_ref[k0:k0 + k, :], preferred_element_type=F32)
        y = t if y is None else y + t
        k0 += k
    return x + mod_ref[0, 2:3, :] * y


def _part_specs(parts, tm):
    return [pl.BlockSpec((tm, p.shape[1]), lambda i, *_: (i, 0)) for p in parts]


FF_CHUNK = 512


def _swiglu(h, wgu, wd, ff):
    chunks = [(c0, min(c0 + FF_CHUNK, ff)) for c0 in range(0, ff, FF_CHUNK)]

    def gate_up(c0, c1):
        return (jnp.dot(h, wgu(c0, c1), preferred_element_type=F32),
                jnp.dot(h, wgu(ff + c0, ff + c1), preferred_element_type=F32))

    acc = None
    nxt = gate_up(*chunks[0])
    for i, (c0, c1) in enumerate(chunks):
        gg, uu = nxt
        if i + 1 < len(chunks):
            nxt = gate_up(*chunks[i + 1])
        a = (gg * jax.nn.sigmoid(gg) * uu).astype(BF16)
        part = jnp.dot(a, wd(c0, c1), preferred_element_type=F32)
        acc = part if acc is None else acc + part
    return acc


def _ffn_kernel(x_ref, g_ref, mod_ref, wout_ref, wgu_ref, wd_ref, *refs):
    x = _mix_residual(x_ref[...], mod_ref, wout_ref, refs[:-1])
    y_ref = refs[-1]
    h = _norm_mod(x, g_ref[...], mod_ref[0, 4:5, :], mod_ref[0, 3:4, :]).astype(BF16)
    y = _swiglu(h, lambda a, b: wgu_ref[:, a:b], lambda a, b: wd_ref[a:b, :], wd_ref.shape[0])
    y_ref[...] = x + mod_ref[0, 5:6, :] * y


def ffn_dense(x2, parts, g, mod, w_out, w_gu, w_down, tiles_per_batch, tm):
    m, d = x2.shape
    return pl.pallas_call(
        _ffn_kernel,
        out_shape=jax.ShapeDtypeStruct((m, d), F32),
        grid=(m // tm,),
        in_specs=[pl.BlockSpec((tm, d), lambda i: (i, 0)),
                  pl.BlockSpec((1, d), lambda i: (0, 0)),
                  pl.BlockSpec((1, 6, d), _mod_map(tiles_per_batch)),
                  _resident(w_out.shape, lambda i: (0, 0)),
                  _resident(w_gu.shape, lambda i: (0, 0)),
                  _resident(w_down.shape, lambda i: (0, 0))] + _part_specs(parts, tm),
        out_specs=pl.BlockSpec((tm, d), lambda i: (i, 0)),
        compiler_params=_params("parallel"),
        name="ffn_dense",
    )(x2, g, mod, w_out, w_gu, w_down, *parts)


ROUTE_E1, ROUTE_E2, ROUTE_R1, ROUTE_R2, ROUTE_G1, ROUTE_G2 = range(6)
ROUTE_ROWS = 8


def _router_kernel(x_ref, g_ref, mod_ref, wout_ref, wr_ref, *refs):
    part_refs = refs[:-6]
    x1_ref, h_ref, route_ref, route_t_ref, cnt_ref, base_ref = refs[-6:]

    @pl.when(pl.program_id(0) == 0)
    def _():
        base_ref[...] = jnp.zeros_like(base_ref)

    x = _mix_residual(x_ref[...], mod_ref, wout_ref, part_refs)
    x1_ref[...] = x
    h = _norm_mod(x, g_ref[...], mod_ref[0, 4:5, :], mod_ref[0, 3:4, :]).astype(BF16)
    h_ref[...] = h
    tm = h.shape[0]
    logits = jnp.dot(h, wr_ref[...], preferred_element_type=F32)
    lane = lax.broadcasted_iota(jnp.int32, logits.shape, 1)
    neg = jnp.float32(-jnp.inf)
    logits = jnp.where(lane < N_EXPERTS, logits, neg)
    m1 = jnp.max(logits, axis=-1, keepdims=True)
    i1 = jnp.min(jnp.where(logits == m1, lane, LANES), axis=-1, keepdims=True)
    rest = jnp.where(lane == i1, neg, logits)
    m2 = jnp.max(rest, axis=-1, keepdims=True)
    i2 = jnp.min(jnp.where(rest == m2, lane, LANES), axis=-1, keepdims=True)
    e2 = jnp.exp(m2 - m1)
    g1 = 1.0 / (1.0 + e2)
    g2 = e2 / (1.0 + e2)
    sel1, sel2 = lane == i1, lane == i2
    sel = jnp.where(sel1 | sel2, 1.0, 0.0)
    li = lax.broadcasted_iota(jnp.int32, (tm, tm), 0)
    si = lax.broadcasted_iota(jnp.int32, (tm, tm), 1)
    before = jnp.where(si < li, 1.0, 0.0).astype(BF16)
    rank = jnp.dot(before, sel.astype(BF16), preferred_element_type=F32) + base_ref[...]
    r1 = jnp.sum(jnp.where(sel1, rank, 0.0), axis=-1, keepdims=True)
    r2 = jnp.sum(jnp.where(sel2, rank, 0.0), axis=-1, keepdims=True)
    base_ref[...] += jnp.sum(sel, axis=0, keepdims=True)
    cnt_ref[...] = base_ref[...]
    fields = (i1.astype(F32), i2.astype(F32), r1, r2, g1, g2)
    route = jnp.zeros(logits.shape, F32)
    for k, v in enumerate(fields):
        route = jnp.where(lane == k, v, route)
    route_ref[...] = route
    route_t_ref[...] = route.T[:ROUTE_ROWS, :]


def moe_router(x2, parts, g, mod, w_out, w_router, tiles_per_batch, tm):
    m, d = x2.shape
    return pl.pallas_call(
        _router_kernel,
        out_shape=(jax.ShapeDtypeStruct((m, d), F32), jax.ShapeDtypeStruct((m, d), BF16),
                   jax.ShapeDtypeStruct((m, LANES), F32), jax.ShapeDtypeStruct((ROUTE_ROWS, m), F32),
                   jax.ShapeDtypeStruct((1, LANES), F32)),
        grid=(m // tm,),
        in_specs=[pl.BlockSpec((tm, d), lambda i: (i, 0)),
                  pl.BlockSpec((1, d), lambda i: (0, 0)),
                  pl.BlockSpec((1, 6, d), _mod_map(tiles_per_batch)),
                  _resident(w_out.shape, lambda i: (0, 0)),
                  pl.BlockSpec((d, LANES), lambda i: (0, 0))] + _part_specs(parts, tm),
        out_specs=(pl.BlockSpec((tm, d), lambda i: (i, 0)),
                   pl.BlockSpec((tm, d), lambda i: (i, 0)),
                   pl.BlockSpec((tm, LANES), lambda i: (i, 0)),
                   pl.BlockSpec((ROUTE_ROWS, tm), lambda i: (0, i)),
                   pl.BlockSpec((1, LANES), lambda i: (0, 0))),
        scratch_shapes=[pltpu.VMEM((1, LANES), F32)],
        compiler_params=_params("arbitrary"),
        name="moe_router",
    )(x2, g, mod, w_out, w_router, *parts)


def _moe_kernel(tile_ref, exp_ref, start_ref, end_ref, xg_ref, wgu_ref, wd_ref, y_ref, wgu_b, wd_b):
    w = pl.program_id(0)
    tm = xg_ref.shape[0]
    start, end = start_ref[w], end_ref[w]
    t0 = tile_ref[w] * tm

    @pl.when((w == 0) | (exp_ref[w] != exp_ref[jnp.maximum(w - 1, 0)]))
    def _():
        wgu_b[...] = wgu_ref[0].astype(BF16)
        wd_b[...] = wd_ref[0].astype(BF16)

    @pl.when(end > start)
    def _():
        y = _swiglu(xg_ref[...], lambda a, b: wgu_b[:, a:b], lambda a, b: wd_b[a:b, :], wd_b.shape[0])
        y = y.astype(y_ref.dtype)

        @pl.when(start == t0)
        def _():
            y_ref[...] = y

        @pl.when(start != t0)
        def _():
            row = t0 + lax.broadcasted_iota(jnp.int32, (tm, 1), 0)
            y_ref[...] = jnp.where(row >= start, y, y_ref[...])


def moe_grouped(item_tile, item_expert, item_start, item_end, xg, w_gu, w_down, tm):
    p, d = xg.shape
    grid_spec = pltpu.PrefetchScalarGridSpec(
        num_scalar_prefetch=4,
        grid=(item_tile.shape[0],),
        in_specs=[pl.BlockSpec((tm, d), lambda w, it, ie, s, e: (it[w], 0)),
                  _resident((1,) + w_gu.shape[1:], lambda w, it, ie, s, e: (ie[w], 0, 0)),
                  _resident((1,) + w_down.shape[1:], lambda w, it, ie, s, e: (ie[w], 0, 0))],
        out_specs=pl.BlockSpec((tm, d), lambda w, it, ie, s, e: (it[w], 0)),
        scratch_shapes=[pltpu.VMEM(w_gu.shape[1:], BF16), pltpu.VMEM(w_down.shape[1:], BF16)],
    )
    return pl.pallas_call(
        _moe_kernel,
        out_shape=jax.ShapeDtypeStruct((p, d), BF16),
        grid_spec=grid_spec,
        compiler_params=_params("arbitrary"),
        name="moe_grouped",
    )(item_tile, item_expert, item_start, item_end, xg, w_gu, w_down)


def _moe_combine_kernel(x_ref, ya_ref, yb_ref, route_ref, mod_ref, *refs):
    o_ref = refs[-1]
    r = route_ref[...]
    y = (r[:, ROUTE_G1:ROUTE_G1 + 1] * ya_ref[...].astype(F32) + r[:, ROUTE_G2:ROUTE_G2 + 1] * yb_ref[...].astype(F32))
    x = x_ref[...] + mod_ref[0, 5:6, :] * y
    if len(refs) == 2:
        ms = jnp.mean(x * x, axis=-1, keepdims=True)
        x = x * lax.rsqrt(ms + EPS) * refs[0][...]
    o_ref[...] = x


def moe_combine(x2, ya, yb, route, mod, tiles_per_batch, tm, g_final=None):
    m, d = x2.shape
    row = pl.BlockSpec((tm, d), lambda i: (i, 0))
    specs = [row, row, row, pl.BlockSpec((tm, LANES), lambda i: (i, 0)),
             pl.BlockSpec((1, 6, d), _mod_map(tiles_per_batch))]
    args = [x2, ya, yb, route, mod]
    if g_final is not None:
        specs.append(pl.BlockSpec((1, d), lambda i: (0, 0)))
        args.append(g_final)
    return pl.pallas_call(
        _moe_combine_kernel,
        out_shape=jax.ShapeDtypeStruct((m, d), F32),
        grid=(m // tm,),
        in_specs=specs,
        out_specs=row,
        compiler_params=_params("parallel"),
        name="moe_combine",
    )(*args)


def moe_block(x2, parts, g, mod, w_out, w_router, w_gu, w_down, tiles_per_batch, tm_tok, tm_grp, g_final=None):
    m, d = x2.shape
    x2, h, route, route_t, cnt = moe_router(x2, parts, g, mod, w_out, w_router, tiles_per_batch, tm_tok)
    cnt = cnt[0, :N_EXPERTS].astype(jnp.int32)
    p = m * TOP_K
    off = jnp.cumsum(cnt) - cnt
    field = lambda k: route_t[k].astype(jnp.int32)

    def row_of(e, r):
        for k in range(N_EXPERTS):
            r = r + jnp.where(e == k, off[k], 0)
        return r

    d1 = row_of(field(ROUTE_E1), field(ROUTE_R1))
    d2 = row_of(field(ROUTE_E2), field(ROUTE_R2))
    tok = jnp.arange(m, dtype=jnp.int32)
    _, src = lax.sort_key_val(jnp.concatenate([d1, d2]), jnp.concatenate([tok, tok]))
    n_row_tiles = p // tm_grp
    start = jnp.sort(jnp.concatenate([jnp.arange(n_row_tiles, dtype=jnp.int32) * tm_grp, off]))
    end = jnp.concatenate([start[1:], jnp.full((1,), p, jnp.int32)])
    item_tile = jnp.minimum(start // tm_grp, n_row_tiles - 1)
    item_expert = jnp.sum((off[None, :] <= start[:, None]).astype(jnp.int32), axis=1) - 1
    rows = lambda a, idx: a.at[idx].get(mode="promise_in_bounds")
    yg = moe_grouped(item_tile, item_expert, start, end, rows(h, src), w_gu, w_down, tm_grp)
    return moe_combine(x2, rows(yg, d1), rows(yg, d2), route, mod, tiles_per_batch, tm_tok, g_final)


def _final_norm_kernel(x_ref, g_ref, o_ref):
    x = x_ref[...]
    ms = jnp.mean(x * x, axis=-1, keepdims=True)
    o_ref[...] = x * lax.rsqrt(ms + EPS) * g_ref[...]


def final_norm(x2, g, tm):
    m, d = x2.shape
    return pl.pallas_call(
        _final_norm_kernel,
        out_shape=jax.ShapeDtypeStruct((m, d), F32),
        grid=(m // tm,),
        in_specs=[pl.BlockSpec((tm, d), lambda i: (i, 0)), pl.BlockSpec((1, d), lambda i: (0, 0))],
        out_specs=pl.BlockSpec((tm, d), lambda i: (i, 0)),
        compiler_params=_params("parallel"),
        name="final_norm",
    )(x2, g)


NA_QROWS = 4
NA_KROWS = NA_QROWS + WIN_R
Q_COL, K_COL, V_COL = 4, 5, 6


def _na_key_start(j, rows):
    return np.clip(j * NA_QROWS - WIN_R // 2, 0, rows - NA_KROWS)


def _na_bias_index(rows):
    nblk = rows // NA_QROWS
    pats = []
    for j in range(nblk):
        start = _na_key_start(j, rows)
        r = j * NA_QROWS + np.arange(NA_QROWS)
        sr = np.clip(r - WIN_R // 2, 0, rows - WIN_R)
        kr = start + np.arange(NA_KROWS)
        rvalid = (kr[None, :] >= sr[:, None]) & (kr[None, :] < sr[:, None] + WIN_R)
        ri = np.clip(kr[None, :] - r[:, None] + WIN_R - 1, 0, 2 * WIN_R - 2)
        pats.append((ri, rvalid))
    for j in range(2, nblk - 1):
        assert all(np.array_equal(a, b) for a, b in zip(pats[1], pats[j]))
    sel = [pats[0], pats[1], pats[nblk - 1]]
    ri = np.stack([p[0] for p in sel])
    rvalid = np.stack([p[1] for p in sel])
    c = np.arange(GRID_W)
    sc = np.clip(c - WIN_C // 2, 0, GRID_W - WIN_C)
    cvalid = (c[None, :] >= sc[:, None]) & (c[None, :] < sc[:, None] + WIN_C)
    ci = np.clip(c[None, :] - c[:, None] + WIN_C - 1, 0, 2 * WIN_C - 2)
    c_onehot = (ci[..., None] == np.arange(2 * WIN_C - 1)).astype(np.float32)
    valid = rvalid[:, :, None, :, None] & cvalid[None, None, :, None, :]
    return ri, c_onehot, valid


def na_bias_table(rpb, rows):
    ri, c_onehot, valid = _na_bias_index(rows)
    toep = jnp.einsum('hrd,qkd->hrqk', rpb, c_onehot, precision=lax.Precision.HIGHEST)
    b = jnp.take(toep, ri.reshape(-1), axis=1).reshape((NA_HEADS,) + ri.shape + (GRID_W, GRID_W))
    b = jnp.transpose(b, (1, 0, 2, 4, 3, 5))
    b = jnp.where(valid[:, None], b, -jnp.inf)
    return b.reshape(3, NA_HEADS, NA_QROWS * GRID_W, NA_KROWS * GRID_W).astype(F32)


def _attend_heads(q, key_sets, bias_fn):
    n, w = q.shape
    lane = lax.broadcasted_iota(jnp.int32, (1, w), 1)
    head_masks = [(lane >= h * NA_HEAD_DIM) & (lane < (h + 1) * NA_HEAD_DIM) for h in range(NA_HEADS)]
    all_scores = []
    for h, mh in enumerate(head_masks):
        qh = jnp.where(mh, q, 0.0).astype(BF16)
        scores = []
        for i, (k, _) in enumerate(key_sets):
            s = lax.dot_general(qh, k, (((1,), (1,)), ((), ())), preferred_element_type=F32)
            b = bias_fn(i, h)
            scores.append(s if b is None else s + b)
        all_scores.append(scores)
    all_probs = []
    for scores in all_scores:
        m = scores[0].max(axis=-1, keepdims=True)
        for s in scores[1:]:
            m = jnp.maximum(m, s.max(axis=-1, keepdims=True))
        denom = jnp.zeros((n, 1), F32)
        probs = []
        for s in scores:
            p = jnp.exp(s - m)
            denom = denom + p.sum(axis=-1, keepdims=True)
            probs.append(p.astype(BF16))
        all_probs.append((probs, denom))
    out = jnp.zeros((n, w), F32)
    for mh, (probs, denom) in zip(head_masks, all_probs):
        acc = jnp.zeros((n, w), F32)
        for p, (_, v) in zip(probs, key_sets):
            acc = acc + jnp.dot(p, v, preferred_element_type=F32)
        out = out + jnp.where(mh, acc / denom, 0.0)
    return out


def _na_kernel(q_ref, k_ref, v_ref, kc_ref, vc_ref, bias_ref, o_ref, *, rows):
    j = pl.program_id(1)
    start = jnp.clip(j * NA_QROWS - WIN_R // 2, 0, rows - NA_KROWS)
    t0 = pl.multiple_of(start * GRID_W, GRID_W)
    nk = NA_KROWS * GRID_W
    kw = k_ref[pl.ds(t0, nk), :]
    vw = v_ref[pl.ds(t0, nk), :]
    q = q_ref[...] * (1.0 / math.sqrt(NA_HEAD_DIM))
    o = _attend_heads(q, [(kw, vw), (kc_ref[...], vc_ref[...])], lambda i, h: bias_ref[0, h] if i == 0 else None)
    o_ref[...] = o.astype(o_ref.dtype)


def na_attention(pr, pr_c, bias, bn):
    S, Lc = pr.shape[0] // bn, pr_c.shape[0] // bn
    rows = S // GRID_W
    nblk = rows // NA_QROWS
    nq = NA_QROWS * GRID_W

    def pat(b, j):
        return (jnp.where(j == 0, 0, jnp.where(j == nblk - 1, 2, 1)), 0, 0, 0)

    return pl.pallas_call(
        functools.partial(_na_kernel, rows=rows),
        out_shape=jax.ShapeDtypeStruct((bn * S, W_NA), BF16),
        grid=(bn, nblk),
        in_specs=[pl.BlockSpec((nq, W_NA), lambda b, j: (b * nblk + j, Q_COL)),
                  pl.BlockSpec((S, W_NA), lambda b, j: (b, K_COL)),
                  pl.BlockSpec((S, W_NA), lambda b, j: (b, V_COL)),
                  pl.BlockSpec((Lc, W_NA), lambda b, j: (b, K_COL)),
                  pl.BlockSpec((Lc, W_NA), lambda b, j: (b, V_COL)),
                  pl.BlockSpec((1,) + bias.shape[1:], pat)],
        out_specs=pl.BlockSpec((nq, W_NA), lambda b, j: (b * nblk + j, 0)),
        compiler_params=_params("parallel", "arbitrary"),
        name="na_attention",
    )(pr, pr, pr, pr_c, pr_c, bias)


def _ctx_attn_kernel(q_ref, k_ref, v_ref, o_ref):
    q = q_ref[...] * (1.0 / math.sqrt(NA_HEAD_DIM))
    o = _attend_heads(q, [(k_ref[...], v_ref[...])], lambda i, h: None)
    o_ref[...] = o.astype(o_ref.dtype)


def ctx_attention(pr_c, bn):
    Lc = pr_c.shape[0] // bn
    return pl.pallas_call(
        _ctx_attn_kernel,
        out_shape=jax.ShapeDtypeStruct((bn * Lc, W_NA), BF16),
        grid=(bn,),
        in_specs=[pl.BlockSpec((Lc, W_NA), lambda b: (b, Q_COL)),
                  pl.BlockSpec((Lc, W_NA), lambda b: (b, K_COL)),
                  pl.BlockSpec((Lc, W_NA), lambda b: (b, V_COL))],
        out_specs=pl.BlockSpec((Lc, W_NA), lambda b: (b, 0)),
        compiler_params=_params("parallel"),
        name="ctx_attention",
    )(pr_c, pr_c, pr_c)


POOL_COL, CONV_H_COL, CONV_B_COL, CONV_C_COL = 0, 1, 2, 3


def _shift_rows(x, k, row):
    n = x.shape[0]
    y = pltpu.roll(x, k % n, 0)
    return jnp.where(row < k, 0.0, y) if k > 0 else jnp.where(row >= n + k, 0.0, y)


def _local_mix_kernel(u_ref, h_ref, bg_ref, cg_ref, pw_ref, ps_ref, cw_ref, op_ref, oc_ref, *, seq):
    row = lax.broadcasted_iota(jnp.int32, (seq, 1), 0)
    lane = lax.broadcasted_iota(jnp.int32, (1, W_POOL), 1)
    u = u_ref[...].astype(F32)
    trailing, leading = {1: u}, {1: u}
    for w in (1, 2, 4):
        trailing[2 * w] = trailing[w] + _shift_rows(trailing[w], w, row)
        leading[2 * w] = leading[w] + _shift_rows(leading[w], -w, row)
    rowf = row.astype(F32)
    win_sum = jnp.zeros_like(u)
    cnt = jnp.zeros_like(u)
    for g, win in enumerate(POOL_WINDOWS):
        half = win // 2
        mg = (lane >= g * POOL_GROUP) & (lane < (g + 1) * POOL_GROUP)
        s = _shift_rows(trailing[half], 1, row) + leading[half]
        n = jnp.minimum(rowf + half, float(seq)) - jnp.maximum(rowf - half, 0.0)
        win_sum = jnp.where(mg, s, win_sum)
        cnt = jnp.where(mg, n, cnt)
    p = win_sum / cnt - u
    pooled = jnp.dot(p.astype(BF16), pw_ref[...], preferred_element_type=F32) * ps_ref[...]
    op_ref[...] = pooled.astype(op_ref.dtype)
    v = cg_ref[...].astype(F32) * h_ref[...].astype(F32)
    conv = (_shift_rows(v, 1, row) * cw_ref[0:1, :] + v * cw_ref[1:2, :] + _shift_rows(v, -1, row) * cw_ref[2:3, :])
    oc_ref[...] = (bg_ref[...].astype(F32) * conv).astype(oc_ref.dtype)


def local_mix(pr, pool_w, pool_scale, conv_w, bn):
    seq = pr.shape[0] // bn
    pw = jax.scipy.linalg.block_diag(*[pool_w[g] for g in range(len(POOL_WINDOWS))]).astype(BF16)
    blk = lambda col: pl.BlockSpec((seq, W_POOL), lambda b: (b, col))
    full = lambda a: pl.BlockSpec(a.shape, lambda b: (0,) * a.ndim)
    args = (pw, pool_scale[None, :], conv_w)
    return pl.pallas_call(
        functools.partial(_local_mix_kernel, seq=seq),
        out_shape=(jax.ShapeDtypeStruct((bn * seq, W_POOL), BF16), jax.ShapeDtypeStruct((bn * seq, W_CONV), BF16)),
        grid=(bn,),
        in_specs=[blk(POOL_COL), blk(CONV_H_COL), blk(CONV_B_COL), blk(CONV_C_COL)] + [full(a) for a in args],
        out_specs=(pl.BlockSpec((seq, W_POOL), lambda b: (b, 0)),
                   pl.BlockSpec((seq, W_CONV), lambda b: (b, 0))),
        compiler_params=_params("parallel"),
        name="local_mix",
    )(pr, pr, pr, pr, *args)


Z_COL, XS_COL, BS_COL, CS_COL = 7, 8, 9, 10
DT_COL = 22
HEADS_PER_GROUP = SSD_HEADS // SSD_GROUPS
GROUP_W = HEADS_PER_GROUP * SSD_HEAD_DIM


def _head_mask(h):
    lane = lax.broadcasted_iota(jnp.int32, (1, W_SSD), 1)
    return (lane >= h * SSD_HEAD_DIM) & (lane < (h + 1) * SSD_HEAD_DIM)


def _expand_heads(v, d):
    out = jnp.zeros((v.shape[0], W_SSD), F32)
    for h in range(SSD_HEADS):
        k = d * SSD_HEADS + h
        out = jnp.where(_head_mask(h), v[:, k:k + 1], out)
    return out


def _ssd_chunk_terms(c, xs_ref, b_ref, c_ref, dt_ref, la_ref, dsk_ref, y_ref, upd_ref, dec_ref, need_y):
    T = SSD_CHUNK
    r0 = pl.multiple_of(c * T, T)
    xs = xs_ref[pl.ds(r0, T), :]
    bm = b_ref[pl.ds(r0, T), :]
    cm = c_ref[pl.ds(r0, T), :]
    dt = dt_ref[pl.ds(r0, T), :]
    la = la_ref[pl.ds(r0, T), :]
    li = lax.broadcasted_iota(jnp.int32, (T, T), 0)
    si = lax.broadcasted_iota(jnp.int32, (T, T), 1)
    causal = si <= li
    prefix = jnp.dot(causal.astype(F32), la, precision=lax.Precision.HIGHEST, preferred_element_type=F32)
    total = prefix[T - 1:T, :]
    lane = lax.broadcasted_iota(jnp.int32, (1, LANES), 1)
    acum = jnp.where(lane < SSD_HEADS, prefix, total - prefix + la)
    bt = bm.astype(F32).T.astype(BF16)
    if need_y:
        acum_t = acum.T
        cb = [lax.dot_general(cm[:, g * SSD_STATE:(g + 1) * SSD_STATE], bm[:, g * SSD_STATE:(g + 1) * SSD_STATE],
                              (((1,), (1,)), ((), ())), preferred_element_type=F32) for g in range(SSD_GROUPS)]
        y = jnp.zeros((T, W_SSD), F32)
    per_dir = []
    for d in range(2):
        acum_e = _expand_heads(acum, d)
        tot_e = _expand_heads(total, d)
        xdt = xs * _expand_heads(dt, d)
        xw = (xdt * jnp.exp(tot_e - acum_e)).astype(BF16)
        dec_ref[c, d, 0:T, :] = jnp.exp(acum_e)
        dec_ref[c, d, T:T + 1, :] = jnp.exp(tot_e)
        scores = []
        if need_y:
            mask = causal if d == 0 else (si >= li)
            for h in range(SSD_HEADS):
                k = d * SSD_HEADS + h
                decay = jnp.exp(jnp.where(mask, acum[:, k:k + 1] - acum_t[k:k + 1, :], -jnp.inf))
                scores.append((cb[h // HEADS_PER_GROUP] * decay).astype(BF16))
        per_dir.append((xw, xdt.astype(BF16), scores))
    for d, (xw, xdt_b, scores) in enumerate(per_dir):
        upd = [jnp.dot(bt[g * SSD_STATE:(g + 1) * SSD_STATE, :], xw[:, g * GROUP_W:(g + 1) * GROUP_W],
                       preferred_element_type=F32) for g in range(SSD_GROUPS)]
        upd_ref[c, d] = jnp.concatenate(upd, axis=1)
        for h, sc in enumerate(scores):
            y = y + jnp.where(_head_mask(h), jnp.dot(sc, xdt_b, preferred_element_type=F32), 0.0)
    if need_y:
        y_ref[pl.ds(r0, T), :] = xs * dsk_ref[...] + y


def _ssd_chunk_state(c, d, c_ref, y_ref, upd_ref, dec_ref, st_ref, need_y):
    T = SSD_CHUNK
    r0 = pl.multiple_of(c * T, T)
    st = st_ref[d]
    if need_y:
        cm = c_ref[pl.ds(r0, T), :]
        st_b = st.astype(BF16)
        ys = [jnp.dot(cm[:, g * SSD_STATE:(g + 1) * SSD_STATE], st_b[:, g * GROUP_W:(g + 1) * GROUP_W],
                      preferred_element_type=F32) for g in range(SSD_GROUPS)]
        y_ref[pl.ds(r0, T), :] += jnp.concatenate(ys, axis=1) * dec_ref[c, d, 0:T, :]
    st_ref[d] = dec_ref[c, d, T:T + 1, :] * st + upd_ref[c, d]


def _ssd_kernel(z_ref, xs_in, bs_in, cs_in, dtr_ref, cw_ref, cb_ref, dtb_ref, alog_ref, dsk_ref, ng_ref, h0_ref,
                *refs, seq, need_y):
    if need_y:
        o_ref, hT_ref, xs_ref, b_ref, c_ref, dt_ref, la_ref, upd_ref, dec_ref, st_ref, y_ref = refs
    else:
        hT_ref, xs_ref, b_ref, c_ref, dt_ref, la_ref, upd_ref, dec_ref, st_ref = refs
        y_ref = None
    nc = seq // SSD_CHUNK
    row = lax.broadcasted_iota(jnp.int32, (seq, 1), 0)
    for gi, (src, dst) in enumerate(((xs_in, xs_ref), (bs_in, b_ref), (cs_in, c_ref))):
        sl = slice(gi * W_SSD, (gi + 1) * W_SSD)
        x = src[...].astype(F32)
        cv = (_shift_rows(x, 1, row) * cw_ref[0:1, sl] + x * cw_ref[1:2, sl]
              + _shift_rows(x, -1, row) * cw_ref[2:3, sl] + cb_ref[:, sl])
        dst[...] = (cv * jax.nn.sigmoid(cv)).astype(dst.dtype)
    dtv = dtr_ref[...] + dtb_ref[...]
    dt = jnp.maximum(dtv, 0.0) + jnp.log1p(jnp.exp(-jnp.abs(dtv)))
    dt_ref[...] = dt
    la_ref[...] = dt * (-jnp.exp(alog_ref[...]))
    st_ref[...] = h0_ref[0]

    def terms(c, carry):
        _ssd_chunk_terms(c, xs_ref, b_ref, c_ref, dt_ref, la_ref, dsk_ref, y_ref, upd_ref, dec_ref, need_y)
        return carry

    lax.fori_loop(0, nc, terms, 0, unroll=2)

    def recur(i, carry):
        _ssd_chunk_state(i, 0, c_ref, y_ref, upd_ref, dec_ref, st_ref, need_y)
        _ssd_chunk_state(nc - 1 - i, 1, c_ref, y_ref, upd_ref, dec_ref, st_ref, need_y)
        return carry

    lax.fori_loop(0, nc, recur, 0, unroll=2)
    hT_ref[0] = st_ref[...]
    if need_y:
        z = z_ref[...].astype(F32)
        yz = y_ref[...] * (z * jax.nn.sigmoid(z))
        ms = jnp.mean(yz * yz, axis=-1, keepdims=True)
        o_ref[...] = (yz * lax.rsqrt(ms + EPS) * ng_ref[...]).astype(o_ref.dtype)


def ssd_scan(pr, dt, h0, conv_w, conv_b, dt_bias, a_log, d_skip, norm_g, need_y):
    bn = h0.shape[0]
    seq = pr.shape[0] // bn
    nc = seq // SSD_CHUNK
    pad = LANES - 2 * SSD_HEADS
    dtb = jnp.pad(dt_bias.reshape(1, -1), ((0, 0), (0, pad)))
    alog = jnp.pad(a_log.reshape(1, -1), ((0, 0), (0, pad)))
    dsk = jnp.repeat(d_skip, SSD_HEAD_DIM)[None, :]
    blk = lambda col: pl.BlockSpec((seq, W_SSD), lambda b: (b, col))
    full = lambda a: pl.BlockSpec(a.shape, lambda b: (0,) * a.ndim)
    st_shape = (1, 2, SSD_STATE, W_SSD)
    st_spec = pl.BlockSpec(st_shape, lambda b: (b, 0, 0, 0))
    args = (conv_w, conv_b[None, :], dtb, alog, dsk, norm_g[None, :])
    out_shape = [jax.ShapeDtypeStruct((bn,) + st_shape[1:], F32)]
    out_specs = [st_spec]
    scratch = [pltpu.VMEM((seq, W_SSD), F32),
               pltpu.VMEM((seq, W_SSD), BF16), pltpu.VMEM((seq, W_SSD), BF16),
               pltpu.VMEM((seq, LANES), F32), pltpu.VMEM((seq, LANES), F32),
               pltpu.VMEM((nc, 2, SSD_STATE, W_SSD), F32),
               pltpu.VMEM((nc, 2, SSD_CHUNK + 8, W_SSD), F32),
               pltpu.VMEM(st_shape[1:], F32)]
    if need_y:
        out_shape.insert(0, jax.ShapeDtypeStruct((bn * seq, W_SSD), BF16))
        out_specs.insert(0, pl.BlockSpec((seq, W_SSD), lambda b: (b, 0)))
        scratch.append(pltpu.VMEM((seq, W_SSD), F32))
    res = pl.pallas_call(
        functools.partial(_ssd_kernel, seq=seq, need_y=need_y),
        out_shape=out_shape,
        grid=(bn,),
        in_specs=[blk(Z_COL), blk(XS_COL), blk(BS_COL), blk(CS_COL),
                  pl.BlockSpec((seq, LANES), lambda b: (b, 0))]
                 + [full(a) for a in args] + [st_spec],
        out_specs=out_specs,
        scratch_shapes=scratch,
        compiler_params=_params("parallel"),
        name="ssd_scan",
    )(pr, pr, pr, pr, dt, *args, h0)
    return (res[0], res[1]) if need_y else (None, res[0])


def ssd_mix(pr, dt, pr_c, dt_c, conv_w, conv_b, dt_bias, a_log, d_skip, norm_g, ctx_out, bn):
    h0 = jnp.zeros((bn, 2, SSD_STATE, W_SSD), F32)
    oc, hc = ssd_scan(pr_c, dt_c, h0, conv_w, conv_b, dt_bias, a_log, d_skip, norm_g, ctx_out)
    o, _ = ssd_scan(pr, dt, hc, conv_w, conv_b, dt_bias, a_log, d_skip, norm_g, True)
    return o, oc


def mixer(pr, dt, pr_c, dt_c, pool_w, pool_scale, conv_w, na_bias, s_conv_w, s_conv_b, dt_bias, a_log, d_skip,
          s_norm_g, ctx_out, bn):
    o_ssd, oc_ssd = ssd_mix(pr, dt, pr_c, dt_c, s_conv_w, s_conv_b, dt_bias, a_log, d_skip, s_norm_g, ctx_out, bn)
    o = [*local_mix(pr, pool_w, pool_scale, conv_w, bn), na_attention(pr, pr_c, na_bias, bn), o_ssd]
    if not ctx_out:
        return o, None
    oc = [*local_mix(pr_c, pool_w, pool_scale, conv_w, bn), ctx_attention(pr_c, bn), oc_ssd]
    return o, oc


TM = 512
TM_PROJ = 1024
TM_GRP = 512


def kernel(x, c, ctx, c_ctx, w_ada, b_ada, g_mix, g_ffn, w_in, w_out, pool_w, pool_scale, conv_w, na_rpb,
           ssd_conv_w, ssd_conv_b, ssd_dt_bias, ssd_a_log, ssd_d, ssd_norm_g, ffn_w_gu, ffn_w_down,
           moe_router, moe_w_gu, moe_w_down, g_final):
    bn, S, d = x.shape
    Lc = ctx.shape[1]
    m, mc = bn * S, bn * Lc
    tpb = S // TM
    x2 = x.reshape(m, d)
    xc2 = ctx.reshape(mc, d)
    c_rows = jnp.concatenate([c, c_ctx[None, :], jnp.zeros((7, d), F32)], axis=0)
    for l in range(DEPTH):
        ctx_out = l < DEPTH - 1
        last = l == DEPTH - 1
        ada = ada_modulation(c_rows, w_ada[l].astype(BF16), b_ada[l][None, :])
        mod = ada[:bn].reshape(bn, 6, d)
        mod_c = ada[bn:bn + 1].reshape(1, 6, d)
        w_in_l = jnp.pad(w_in[l], ((0, 0), (0, D_IN_PAD - D_IN_PROJ))).astype(BF16)
        g_m = g_mix[l][None, :]
        g_f = g_ffn[l][None, :]
        pr, dt = in_proj(x2, g_m, mod, w_in_l, S // TM_PROJ, TM_PROJ)
        pr_c, dt_c = in_proj(xc2, g_m, mod_c, w_in_l, None, min(TM_PROJ, mc))
        na_bias = na_bias_table(na_rpb[l], S // GRID_W)
        o, oc = mixer(pr, dt, pr_c, dt_c, pool_w[l], pool_scale[l], conv_w[l], na_bias, ssd_conv_w[l],
                      ssd_conv_b[l], ssd_dt_bias[l], ssd_a_log[l], ssd_d[l], ssd_norm_g[l], ctx_out, bn)
        w_out_l = w_out[l].astype(BF16)
        j = l // 2
        if l % 2 == 0:
            w_gu = ffn_w_gu[j].astype(BF16)
            w_dn = ffn_w_down[j].astype(BF16)
            x2 = ffn_dense(x2, o, g_f, mod, w_out_l, w_gu, w_dn, tpb, TM)
            if ctx_out:
                xc2 = ffn_dense(xc2, oc, g_f, mod_c, w_out_l, w_gu, w_dn, None, min(TM, mc))
        else:
            w_r = jnp.pad(moe_router[j], ((0, 0), (0, LANES - N_EXPERTS))).astype(BF16)
            w_gu, w_dn = moe_w_gu[j], moe_w_down[j]
            x2 = moe_block(x2, o, g_f, mod, w_out_l, w_r, w_gu, w_dn, tpb, TM, TM_GRP,
                           g_final[None, :] if last else None)
            if ctx_out:
                xc2 = moe_block(xc2, oc, g_f, mod_c, w_out_l, w_r, w_gu, w_dn, None, min(TM, mc), TM_GRP)
            if last:
                return x2.reshape(bn, S, d)
    return final_norm(x2, g_final[None, :], TM).reshape(bn, S, d)
```

```python
import functools
import math

import numpy as np
import jax
import jax.numpy as jnp
from jax import lax
from jax.experimental import pallas as pl
from jax.experimental.pallas import tpu as pltpu

DEPTH = 2
GRID_W = 64
EPS = 1e-6
W_POOL = 256
W_CONV = 256
W_NA = 256
W_SSD = 256
POOL_WINDOWS = (2, 4, 8, 16)
POOL_GROUP = W_POOL // len(POOL_WINDOWS)
NA_HEADS = 4
NA_HEAD_DIM = W_NA // NA_HEADS
WIN_R = 8
WIN_C = 16
SSD_HEAD_DIM = 64
SSD_HEADS = W_SSD // SSD_HEAD_DIM
SSD_GROUPS = 2
SSD_STATE = 128
SSD_CHUNK = 128
SSD_XBC = W_SSD + 2 * SSD_GROUPS * SSD_STATE
D_IN_PROJ = W_POOL + 3 * W_CONV + 3 * W_NA + W_SSD + SSD_XBC + 2 * SSD_HEADS
N_EXPERTS = 8
TOP_K = 2

LANES = 128
D_IN_PAD = -(-D_IN_PROJ // LANES) * LANES
VMEM_LIMIT = 56 * 1024 * 1024
BF16 = jnp.bfloat16
F32 = jnp.float32


def _params(*sem):
    return pltpu.CompilerParams(dimension_semantics=sem, vmem_limit_bytes=VMEM_LIMIT)


def _norm_mod(x, g, scale, shift):
    ms = jnp.mean(x * x, axis=-1, keepdims=True)
    return (x * lax.rsqrt(ms + EPS) * g) * (1.0 + scale) + shift


def _mod_map(tiles_per_batch):
    if tiles_per_batch is None:
        return lambda i, *_: (0, 0, 0)
    return lambda i, *_: (i // tiles_per_batch, 0, 0)


def _resident(shape, index_map):
    return pl.BlockSpec(shape, index_map, pipeline_mode=pl.Buffered(1))


def _ada_kernel(c_ref, w_ref, b_ref, o_ref):
    c = c_ref[...]
    s = c * jax.nn.sigmoid(c)
    o_ref[...] = jnp.dot(s.astype(BF16), w_ref[...], preferred_element_type=F32) + b_ref[...]


def ada_modulation(c_rows, w, b):
    r, d = c_rows.shape
    n = w.shape[1]
    tn = 1536
    return pl.pallas_call(
        _ada_kernel,
        out_shape=jax.ShapeDtypeStruct((r, n), F32),
        grid=(n // tn,),
        in_specs=[pl.BlockSpec((r, d), lambda j: (0, 0)),
                  pl.BlockSpec((d, tn), lambda j: (0, j)),
                  pl.BlockSpec((1, tn), lambda j: (0, j))],
        out_specs=pl.BlockSpec((r, tn), lambda j: (0, j)),
        compiler_params=_params("arbitrary"),
        name="ada_modulation",
    )(c_rows, w, b)


def _in_proj_kernel(x_ref, g_ref, mod_ref, w_ref, o_ref, dt_ref):
    h = _norm_mod(x_ref[...], g_ref[...], mod_ref[0, 1:2, :], mod_ref[0, 0:1, :])
    pr = jnp.dot(h.astype(BF16), w_ref[...], preferred_element_type=F32)
    o_ref[...] = pr.astype(o_ref.dtype)
    dt_ref[...] = pr[:, DT_COL * LANES:(DT_COL + 1) * LANES]


def in_proj(x2, g, mod, w, tiles_per_batch, tm):
    m, d = x2.shape
    n = w.shape[1]
    return pl.pallas_call(
        _in_proj_kernel,
        out_shape=(jax.ShapeDtypeStruct((m, n), BF16), jax.ShapeDtypeStruct((m, LANES), F32)),
        grid=(m // tm,),
        in_specs=[pl.BlockSpec((tm, d), lambda i: (i, 0)),
                  pl.BlockSpec((1, d), lambda i: (0, 0)),
                  pl.BlockSpec((1, 6, d), _mod_map(tiles_per_batch)),
                  _resident((d, n), lambda i: (0, 0))],
        out_specs=(pl.BlockSpec((tm, n), lambda i: (i, 0)), pl.BlockSpec((tm, LANES), lambda i: (i, 0))),
        compiler_params=_params("parallel"),
        name="in_proj",
    )(x2, g, mod, w)


def _mix_residual(x, mod_ref, w_ref, part_refs):
    y, k0 = None, 0
    for p_ref in part_refs:
        k = p_ref.shape[1]
        t = jnp.dot(p_ref[...], w_ref[k0:k0 + k, :], preferred_element_type=F32)
        y = t if y is None else y + t
        k0 += k
    return x + mod_ref[0, 2:3, :] * y


def _part_specs(parts, tm):
    return [pl.BlockSpec((tm, p.shape[1]), lambda i, *_: (i, 0)) for p in parts]


FF_CHUNK = 512


def _swiglu(h, wgu, wd, ff):
    acc = None
    for c0 in range(0, ff, FF_CHUNK):
        c1 = min(c0 + FF_CHUNK, ff)
        gg = jnp.dot(h, wgu(c0, c1), preferred_element_type=F32)
        uu = jnp.dot(h, wgu(ff + c0, ff + c1), preferred_element_type=F32)
        a = (gg * jax.nn.sigmoid(gg) * uu).astype(BF16)
        part = jnp.dot(a, wd(c0, c1), preferred_element_type=F32)
        acc = part if acc is None else acc + part
    return acc


def _ffn_kernel(x_ref, g_ref, mod_ref, wout_ref, wgu_ref, wd_ref, *refs):
    x = _mix_residual(x_ref[...], mod_ref, wout_ref, refs[:-1])
    y_ref = refs[-1]
    h = _norm_mod(x, g_ref[...], mod_ref[0, 4:5, :], mod_ref[0, 3:4, :]).astype(BF16)
    y = _swiglu(h, lambda a, b: wgu_ref[:, a:b], lambda a, b: wd_ref[a:b, :], wd_ref.shape[0])
    y_ref[...] = x + mod_ref[0, 5:6, :] * y


def ffn_dense(x2, parts, g, mod, w_out, w_gu, w_down, tiles_per_batch, tm):
    m, d = x2.shape
    return pl.pallas_call(
        _ffn_kernel,
        out_shape=jax.ShapeDtypeStruct((m, d), F32),
        grid=(m // tm,),
        in_specs=[pl.BlockSpec((tm, d), lambda i: (i, 0)),
                  pl.BlockSpec((1, d), lambda i: (0, 0)),
                  pl.BlockSpec((1, 6, d), _mod_map(tiles_per_batch)),
                  _resident(w_out.shape, lambda i: (0, 0)),
                  _resident(w_gu.shape, lambda i: (0, 0)),
                  _resident(w_down.shape, lambda i: (0, 0))] + _part_specs(parts, tm),
        out_specs=pl.BlockSpec((tm, d), lambda i: (i, 0)),
        compiler_params=_params("parallel"),
        name="ffn_dense",
    )(x2, g, mod, w_out, w_gu, w_down, *parts)


ROUTE_E1, ROUTE_E2, ROUTE_R1, ROUTE_R2, ROUTE_G1, ROUTE_G2 = range(6)
ROUTE_ROWS = 8


def _router_kernel(x_ref, g_ref, mod_ref, wout_ref, wr_ref, *refs):
    part_refs = refs[:-6]
    x1_ref, h_ref, route_ref, route_t_ref, cnt_ref, base_ref = refs[-6:]

    @pl.when(pl.program_id(0) == 0)
    def _():
        base_ref[...] = jnp.zeros_like(base_ref)

    x = _mix_residual(x_ref[...], mod_ref, wout_ref, part_refs)
    x1_ref[...] = x
    h = _norm_mod(x, g_ref[...], mod_ref[0, 4:5, :], mod_ref[0, 3:4, :]).astype(BF16)
    h_ref[...] = h
    tm = h.shape[0]
    logits = jnp.dot(h, wr_ref[...], preferred_element_type=F32)
    lane = lax.broadcasted_iota(jnp.int32, logits.shape, 1)
    neg = jnp.float32(-jnp.inf)
    logits = jnp.where(lane < N_EXPERTS, logits, neg)
    m1 = jnp.max(logits, axis=-1, keepdims=True)
    i1 = jnp.min(jnp.where(logits == m1, lane, LANES), axis=-1, keepdims=True)
    rest = jnp.where(lane == i1, neg, logits)
    m2 = jnp.max(rest, axis=-1, keepdims=True)
    i2 = jnp.min(jnp.where(rest == m2, lane, LANES), axis=-1, keepdims=True)
    e2 = jnp.exp(m2 - m1)
    g1 = 1.0 / (1.0 + e2)
    g2 = e2 / (1.0 + e2)
    sel1, sel2 = lane == i1, lane == i2
    sel = jnp.where(sel1 | sel2, 1.0, 0.0)
    li = lax.broadcasted_iota(jnp.int32, (tm, tm), 0)
    si = lax.broadcasted_iota(jnp.int32, (tm, tm), 1)
    before = jnp.where(si < li, 1.0, 0.0).astype(BF16)
    rank = jnp.dot(before, sel.astype(BF16), preferred_element_type=F32) + base_ref[...]
    r1 = jnp.sum(jnp.where(sel1, rank, 0.0), axis=-1, keepdims=True)
    r2 = jnp.sum(jnp.where(sel2, rank, 0.0), axis=-1, keepdims=True)
    base_ref[...] += jnp.sum(sel, axis=0, keepdims=True)
    cnt_ref[...] = base_ref[...]
    fields = (i1.astype(F32), i2.astype(F32), r1, r2, g1, g2)
    route = jnp.zeros(logits.shape, F32)
    for k, v in enumerate(fields):
        route = jnp.where(lane == k, v, route)
    route_ref[...] = route
    route_t_ref[...] = route.T[:ROUTE_ROWS, :]


def moe_router(x2, parts, g, mod, w_out, w_router, tiles_per_batch, tm):
    m, d = x2.shape
    return pl.pallas_call(
        _router_kernel,
        out_shape=(jax.ShapeDtypeStruct((m, d), F32), jax.ShapeDtypeStruct((m, d), BF16),
                   jax.ShapeDtypeStruct((m, LANES), F32), jax.ShapeDtypeStruct((ROUTE_ROWS, m), F32),
                   jax.ShapeDtypeStruct((1, LANES), F32)),
        grid=(m // tm,),
        in_specs=[pl.BlockSpec((tm, d), lambda i: (i, 0)),
                  pl.BlockSpec((1, d), lambda i: (0, 0)),
                  pl.BlockSpec((1, 6, d), _mod_map(tiles_per_batch)),
                  _resident(w_out.shape, lambda i: (0, 0)),
                  pl.BlockSpec((d, LANES), lambda i: (0, 0))] + _part_specs(parts, tm),
        out_specs=(pl.BlockSpec((tm, d), lambda i: (i, 0)),
                   pl.BlockSpec((tm, d), lambda i: (i, 0)),
                   pl.BlockSpec((tm, LANES), lambda i: (i, 0)),
                   pl.BlockSpec((ROUTE_ROWS, tm), lambda i: (0, i)),
                   pl.BlockSpec((1, LANES), lambda i: (0, 0))),
        scratch_shapes=[pltpu.VMEM((1, LANES), F32)],
        compiler_params=_params("arbitrary"),
        name="moe_router",
    )(x2, g, mod, w_out, w_router, *parts)


def _moe_kernel(tile_ref, exp_ref, start_ref, end_ref, xg_ref, wgu_ref, wd_ref, y_ref, wgu_b, wd_b):
    w = pl.program_id(0)
    tm = xg_ref.shape[0]
    start, end = start_ref[w], end_ref[w]
    t0 = tile_ref[w] * tm

    @pl.when((w == 0) | (exp_ref[w] != exp_ref[jnp.maximum(w - 1, 0)]))
    def _():
        wgu_b[...] = wgu_ref[0].astype(BF16)
        wd_b[...] = wd_ref[0].astype(BF16)

    @pl.when(end > start)
    def _():
        y = _swiglu(xg_ref[...], lambda a, b: wgu_b[:, a:b], lambda a, b: wd_b[a:b, :], wd_b.shape[0])
        y = y.astype(y_ref.dtype)

        @pl.when(start == t0)
        def _():
            y_ref[...] = y

        @pl.when(start != t0)
        def _():
            row = t0 + lax.broadcasted_iota(jnp.int32, (tm, 1), 0)
            y_ref[...] = jnp.where(row >= start, y, y_ref[...])


def moe_grouped(item_tile, item_expert, item_start, item_end, xg, w_gu, w_down, tm):
    p, d = xg.shape
    grid_spec = pltpu.PrefetchScalarGridSpec(
        num_scalar_prefetch=4,
        grid=(item_tile.shape[0],),
        in_specs=[pl.BlockSpec((tm, d), lambda w, it, ie, s, e: (it[w], 0)),
                  _resident((1,) + w_gu.shape[1:], lambda w, it, ie, s, e: (ie[w], 0, 0)),
                  _resident((1,) + w_down.shape[1:], lambda w, it, ie, s, e: (ie[w], 0, 0))],
        out_specs=pl.BlockSpec((tm, d), lambda w, it, ie, s, e: (it[w], 0)),
        scratch_shapes=[pltpu.VMEM(w_gu.shape[1:], BF16), pltpu.VMEM(w_down.shape[1:], BF16)],
    )
    return pl.pallas_call(
        _moe_kernel,
        out_shape=jax.ShapeDtypeStruct((p, d), BF16),
        grid_spec=grid_spec,
        compiler_params=_params("arbitrary"),
        name="moe_grouped",
    )(item_tile, item_expert, item_start, item_end, xg, w_gu, w_down)


def _moe_combine_kernel(x_ref, ya_ref, yb_ref, route_ref, mod_ref, *refs):
    o_ref = refs[-1]
    r = route_ref[...]
    y = (r[:, ROUTE_G1:ROUTE_G1 + 1] * ya_ref[...].astype(F32) + r[:, ROUTE_G2:ROUTE_G2 + 1] * yb_ref[...].astype(F32))
    x = x_ref[...] + mod_ref[0, 5:6, :] * y
    if len(refs) == 2:
        ms = jnp.mean(x * x, axis=-1, keepdims=True)
        x = x * lax.rsqrt(ms + EPS) * refs[0][...]
    o_ref[...] = x


def moe_combine(x2, ya, yb, route, mod, tiles_per_batch, tm, g_final=None):
    m, d = x2.shape
    row = pl.BlockSpec((tm, d), lambda i: (i, 0))
    specs = [row, row, row, pl.BlockSpec((tm, LANES), lambda i: (i, 0)),
             pl.BlockSpec((1, 6, d), _mod_map(tiles_per_batch))]
    args = [x2, ya, yb, route, mod]
    if g_final is not None:
        specs.append(pl.BlockSpec((1, d), lambda i: (0, 0)))
        args.append(g_final)
    return pl.pallas_call(
        _moe_combine_kernel,
        out_shape=jax.ShapeDtypeStruct((m, d), F32),
        grid=(m // tm,),
        in_specs=specs,
        out_specs=row,
        compiler_params=_params("parallel"),
        name="moe_combine",
    )(*args)


def moe_block(x2, parts, g, mod, w_out, w_router, w_gu, w_down, tiles_per_batch, tm_tok, tm_grp, g_final=None):
    m, d = x2.shape
    x2, h, route, route_t, cnt = moe_router(x2, parts, g, mod, w_out, w_router, tiles_per_batch, tm_tok)
    cnt = cnt[0, :N_EXPERTS].astype(jnp.int32)
    p = m * TOP_K
    off = jnp.cumsum(cnt) - cnt
    field = lambda k: route_t[k].astype(jnp.int32)

    def row_of(e, r):
        for k in range(N_EXPERTS):
            r = r + jnp.where(e == k, off[k], 0)
        return r

    d1 = row_of(field(ROUTE_E1), field(ROUTE_R1))
    d2 = row_of(field(ROUTE_E2), field(ROUTE_R2))
    tok = jnp.arange(m, dtype=jnp.int32)
    _, src = lax.sort_key_val(jnp.concatenate([d1, d2]), jnp.concatenate([tok, tok]))
    n_row_tiles = p // tm_grp
    start = jnp.sort(jnp.concatenate([jnp.arange(n_row_tiles, dtype=jnp.int32) * tm_grp, off]))
    end = jnp.concatenate([start[1:], jnp.full((1,), p, jnp.int32)])
    item_tile = jnp.minimum(start // tm_grp, n_row_tiles - 1)
    item_expert = jnp.sum((off[None, :] <= start[:, None]).astype(jnp.int32), axis=1) - 1
    rows = lambda a, idx: a.at[idx].get(mode="promise_in_bounds")
    yg = moe_grouped(item_tile, item_expert, start, end, rows(h, src), w_gu, w_down, tm_grp)
    return moe_combine(x2, rows(yg, d1), rows(yg, d2), route, mod, tiles_per_batch, tm_tok, g_final)


def _final_norm_kernel(x_ref, g_ref, o_ref):
    x = x_ref[...]
    ms = jnp.mean(x * x, axis=-1, keepdims=True)
    o_ref[...] = x * lax.rsqrt(ms + EPS) * g_ref[...]


def final_norm(x2, g, tm):
    m, d = x2.shape
    return pl.pallas_call(
        _final_norm_kernel,
        out_shape=jax.ShapeDtypeStruct((m, d), F32),
        grid=(m // tm,),
        in_specs=[pl.BlockSpec((tm, d), lambda i: (i, 0)), pl.BlockSpec((1, d), lambda i: (0, 0))],
        out_specs=pl.BlockSpec((tm, d), lambda i: (i, 0)),
        compiler_params=_params("parallel"),
        name="final_norm",
    )(x2, g)


NA_QROWS = 4
NA_KROWS = NA_QROWS + WIN_R
Q_COL, K_COL, V_COL = 4, 5, 6


def _na_key_start(j, rows):
    return np.clip(j * NA_QROWS - WIN_R // 2, 0, rows - NA_KROWS)


def _na_bias_index(rows):
    nblk = rows // NA_QROWS
    pats = []
    for j in range(nblk):
        start = _na_key_start(j, rows)
        r = j * NA_QROWS + np.arange(NA_QROWS)
        sr = np.clip(r - WIN_R // 2, 0, rows - WIN_R)
        kr = start + np.arange(NA_KROWS)
        rvalid = (kr[None, :] >= sr[:, None]) & (kr[None, :] < sr[:, None] + WIN_R)
        ri = np.clip(kr[None, :] - r[:, None] + WIN_R - 1, 0, 2 * WIN_R - 2)
        pats.append((ri, rvalid))
    for j in range(2, nblk - 1):
        assert all(np.array_equal(a, b) for a, b in zip(pats[1], pats[j]))
    sel = [pats[0], pats[1], pats[nblk - 1]]
    ri = np.stack([p[0] for p in sel])
    rvalid = np.stack([p[1] for p in sel])
    c = np.arange(GRID_W)
    sc = np.clip(c - WIN_C // 2, 0, GRID_W - WIN_C)
    cvalid = (c[None, :] >= sc[:, None]) & (c[None, :] < sc[:, None] + WIN_C)
    ci = np.clip(c[None, :] - c[:, None] + WIN_C - 1, 0, 2 * WIN_C - 2)
    c_onehot = (ci[..., None] == np.arange(2 * WIN_C - 1)).astype(np.float32)
    valid = rvalid[:, :, None, :, None] & cvalid[None, None, :, None, :]
    return ri, c_onehot, valid


def na_bias_table(rpb, rows):
    ri, c_onehot, valid = _na_bias_index(rows)
    toep = jnp.einsum('hrd,qkd->hrqk', rpb, c_onehot, precision=lax.Precision.HIGHEST)
    b = jnp.take(toep, ri.reshape(-1), axis=1).reshape((NA_HEADS,) + ri.shape + (GRID_W, GRID_W))
    b = jnp.transpose(b, (1, 0, 2, 4, 3, 5))
    b = jnp.where(valid[:, None], b, -jnp.inf)
    return b.reshape(3, NA_HEADS, NA_QROWS * GRID_W, NA_KROWS * GRID_W).astype(F32)


def _attend_heads(q, key_sets, bias_fn):
    n, w = q.shape
    lane = lax.broadcasted_iota(jnp.int32, (1, w), 1)
    head_masks = [(lane >= h * NA_HEAD_DIM) & (lane < (h + 1) * NA_HEAD_DIM) for h in range(NA_HEADS)]
    all_scores = []
    for h, mh in enumerate(head_masks):
        qh = jnp.where(mh, q, 0.0).astype(BF16)
        scores = []
        for i, (k, _) in enumerate(key_sets):
            s = lax.dot_general(qh, k, (((1,), (1,)), ((), ())), preferred_element_type=F32)
            b = bias_fn(i, h)
            scores.append(s if b is None else s + b)
        all_scores.append(scores)
    all_probs = []
    for scores in all_scores:
        m = scores[0].max(axis=-1, keepdims=True)
        for s in scores[1:]:
            m = jnp.maximum(m, s.max(axis=-1, keepdims=True))
        denom = jnp.zeros((n, 1), F32)
        probs = []
        for s in scores:
            p = jnp.exp(s - m)
            denom = denom + p.sum(axis=-1, keepdims=True)
            probs.append(p.astype(BF16))
        all_probs.append((probs, denom))
    out = jnp.zeros((n, w), F32)
    for mh, (probs, denom) in zip(head_masks, all_probs):
        acc = jnp.zeros((n, w), F32)
        for p, (_, v) in zip(probs, key_sets):
            acc = acc + jnp.dot(p, v, preferred_element_type=F32)
        out = out + jnp.where(mh, acc / denom, 0.0)
    return out


def _na_kernel(q_ref, k_ref, v_ref, kc_ref, vc_ref, bias_ref, o_ref, *, rows):
    j = pl.program_id(1)
    start = jnp.clip(j * NA_QROWS - WIN_R // 2, 0, rows - NA_KROWS)
    t0 = pl.multiple_of(start * GRID_W, GRID_W)
    nk = NA_KROWS * GRID_W
    kw = k_ref[pl.ds(t0, nk), :]
    vw = v_ref[pl.ds(t0, nk), :]
    q = q_ref[...] * (1.0 / math.sqrt(NA_HEAD_DIM))
    o = _attend_heads(q, [(kw, vw), (kc_ref[...], vc_ref[...])], lambda i, h: bias_ref[0, h] if i == 0 else None)
    o_ref[...] = o.astype(o_ref.dtype)


def na_attention(pr, pr_c, bias, bn):
    S, Lc = pr.shape[0] // bn, pr_c.shape[0] // bn
    rows = S // GRID_W
    nblk = rows // NA_QROWS
    nq = NA_QROWS * GRID_W

    def pat(b, j):
        return (jnp.where(j == 0, 0, jnp.where(j == nblk - 1, 2, 1)), 0, 0, 0)

    return pl.pallas_call(
        functools.partial(_na_kernel, rows=rows),
        out_shape=jax.ShapeDtypeStruct((bn * S, W_NA), BF16),
        grid=(bn, nblk),
        in_specs=[pl.BlockSpec((nq, W_NA), lambda b, j: (b * nblk + j, Q_COL)),
                  pl.BlockSpec((S, W_NA), lambda b, j: (b, K_COL)),
                  pl.BlockSpec((S, W_NA), lambda b, j: (b, V_COL)),
                  pl.BlockSpec((Lc, W_NA), lambda b, j: (b, K_COL)),
                  pl.BlockSpec((Lc, W_NA), lambda b, j: (b, V_COL)),
                  pl.BlockSpec((1,) + bias.shape[1:], pat)],
        out_specs=pl.BlockSpec((nq, W_NA), lambda b, j: (b * nblk + j, 0)),
        compiler_params=_params("parallel", "arbitrary"),
        name="na_attention",
    )(pr, pr, pr, pr_c, pr_c, bias)


def _ctx_attn_kernel(q_ref, k_ref, v_ref, o_ref):
    q = q_ref[...] * (1.0 / math.sqrt(NA_HEAD_DIM))
    o = _attend_heads(q, [(k_ref[...], v_ref[...])], lambda i, h: None)
    o_ref[...] = o.astype(o_ref.dtype)


def ctx_attention(pr_c, bn):
    Lc = pr_c.shape[0] // bn
    return pl.pallas_call(
        _ctx_attn_kernel,
        out_shape=jax.ShapeDtypeStruct((bn * Lc, W_NA), BF16),
        grid=(bn,),
        in_specs=[pl.BlockSpec((Lc, W_NA), lambda b: (b, Q_COL)),
                  pl.BlockSpec((Lc, W_NA), lambda b: (b, K_COL)),
                  pl.BlockSpec((Lc, W_NA), lambda b: (b, V_COL))],
        out_specs=pl.BlockSpec((Lc, W_NA), lambda b: (b, 0)),
        compiler_params=_params("parallel"),
        name="ctx_attention",
    )(pr_c, pr_c, pr_c)


POOL_COL, CONV_H_COL, CONV_B_COL, CONV_C_COL = 0, 1, 2, 3


def _shift_rows(x, k, row):
    n = x.shape[0]
    y = pltpu.roll(x, k % n, 0)
    return jnp.where(row < k, 0.0, y) if k > 0 else jnp.where(row >= n + k, 0.0, y)


def _local_mix_kernel(u_ref, h_ref, bg_ref, cg_ref, pw_ref, ps_ref, cw_ref, op_ref, oc_ref, *, seq):
    row = lax.broadcasted_iota(jnp.int32, (seq, 1), 0)
    lane = lax.broadcasted_iota(jnp.int32, (1, W_POOL), 1)
    u = u_ref[...].astype(F32)
    trailing, leading = {1: u}, {1: u}
    for w in (1, 2, 4):
        trailing[2 * w] = trailing[w] + _shift_rows(trailing[w], w, row)
        leading[2 * w] = leading[w] + _shift_rows(leading[w], -w, row)
    rowf = row.astype(F32)
    win_sum = jnp.zeros_like(u)
    cnt = jnp.zeros_like(u)
    for g, win in enumerate(POOL_WINDOWS):
        half = win // 2
        mg = (lane >= g * POOL_GROUP) & (lane < (g + 1) * POOL_GROUP)
        s = _shift_rows(trailing[half], 1, row) + leading[half]
        n = jnp.minimum(rowf + half, float(seq)) - jnp.maximum(rowf - half, 0.0)
        win_sum = jnp.where(mg, s, win_sum)
        cnt = jnp.where(mg, n, cnt)
    p = win_sum / cnt - u
    pooled = jnp.dot(p.astype(BF16), pw_ref[...], preferred_element_type=F32) * ps_ref[...]
    op_ref[...] = pooled.astype(op_ref.dtype)
    v = cg_ref[...].astype(F32) * h_ref[...].astype(F32)
    conv = (_shift_rows(v, 1, row) * cw_ref[0:1, :] + v * cw_ref[1:2, :] + _shift_rows(v, -1, row) * cw_ref[2:3, :])
    oc_ref[...] = (bg_ref[...].astype(F32) * conv).astype(oc_ref.dtype)


def local_mix(pr, pool_w, pool_scale, conv_w, bn):
    seq = pr.shape[0] // bn
    pw = jax.scipy.linalg.block_diag(*[pool_w[g] for g in range(len(POOL_WINDOWS))]).astype(BF16)
    blk = lambda col: pl.BlockSpec((seq, W_POOL), lambda b: (b, col))
    full = lambda a: pl.BlockSpec(a.shape, lambda b: (0,) * a.ndim)
    args = (pw, pool_scale[None, :], conv_w)
    return pl.pallas_call(
        functools.partial(_local_mix_kernel, seq=seq),
        out_shape=(jax.ShapeDtypeStruct((bn * seq, W_POOL), BF16), jax.ShapeDtypeStruct((bn * seq, W_CONV), BF16)),
        grid=(bn,),
        in_specs=[blk(POOL_COL), blk(CONV_H_COL), blk(CONV_B_COL), blk(CONV_C_COL)] + [full(a) for a in args],
        out_specs=(pl.BlockSpec((seq, W_POOL), lambda b: (b, 0)),
                   pl.BlockSpec((seq, W_CONV), lambda b: (b, 0))),
        compiler_params=_params("parallel"),
        name="local_mix",
    )(pr, pr, pr, pr, *args)


Z_COL, XS_COL, BS_COL, CS_COL = 7, 8, 9, 10
DT_COL = 22
HEADS_PER_GROUP = SSD_HEADS // SSD_GROUPS
GROUP_W = HEADS_PER_GROUP * SSD_HEAD_DIM


def _head_mask(h):
    lane = lax.broadcasted_iota(jnp.int32, (1, W_SSD), 1)
    return (lane >= h * SSD_HEAD_DIM) & (lane < (h + 1) * SSD_HEAD_DIM)


def _expand_heads(v, d):
    out = jnp.zeros((v.shape[0], W_SSD), F32)
    for h in range(SSD_HEADS):
        k = d * SSD_HEADS + h
        out = jnp.where(_head_mask(h), v[:, k:k + 1], out)
    return out


def _ssd_chunk_terms(c, xs_ref, b_ref, c_ref, dt_ref, la_ref, dsk_ref, y_ref, upd_ref, dec_ref, need_y):
    T = SSD_CHUNK
    r0 = pl.multiple_of(c * T, T)
    xs = xs_ref[pl.ds(r0, T), :]
    bm = b_ref[pl.ds(r0, T), :]
    cm = c_ref[pl.ds(r0, T), :]
    dt = dt_ref[pl.ds(r0, T), :]
    la = la_ref[pl.ds(r0, T), :]
    li = lax.broadcasted_iota(jnp.int32, (T, T), 0)
    si = lax.broadcasted_iota(jnp.int32, (T, T), 1)
    causal = si <= li
    prefix = jnp.dot(causal.astype(F32), la, precision=lax.Precision.HIGHEST, preferred_element_type=F32)
    total = prefix[T - 1:T, :]
    lane = lax.broadcasted_iota(jnp.int32, (1, LANES), 1)
    acum = jnp.where(lane < SSD_HEADS, prefix, total - prefix + la)
    bt = bm.astype(F32).T.astype(BF16)
    if need_y:
        acum_t = acum.T
        cb = [lax.dot_general(cm[:, g * SSD_STATE:(g + 1) * SSD_STATE], bm[:, g * SSD_STATE:(g + 1) * SSD_STATE],
                              (((1,), (1,)), ((), ())), preferred_element_type=F32) for g in range(SSD_GROUPS)]
        y = jnp.zeros((T, W_SSD), F32)
    per_dir = []
    for d in range(2):
        acum_e = _expand_heads(acum, d)
        tot_e = _expand_heads(total, d)
        xdt = xs * _expand_heads(dt, d)
        xw = (xdt * jnp.exp(tot_e - acum_e)).astype(BF16)
        dec_ref[c, d, 0:T, :] = jnp.exp(acum_e)
        dec_ref[c, d, T:T + 1, :] = jnp.exp(tot_e)
        scores = []
        if need_y:
            mask = causal if d == 0 else (si >= li)
            for h in range(SSD_HEADS):
                k = d * SSD_HEADS + h
                decay = jnp.exp(jnp.where(mask, acum[:, k:k + 1] - acum_t[k:k + 1, :], -jnp.inf))
                scores.append((cb[h // HEADS_PER_GROUP] * decay).astype(BF16))
        per_dir.append((xw, xdt.astype(BF16), scores))
    for d, (xw, xdt_b, scores) in enumerate(per_dir):
        upd = [jnp.dot(bt[g * SSD_STATE:(g + 1) * SSD_STATE, :], xw[:, g * GROUP_W:(g + 1) * GROUP_W],
                       preferred_element_type=F32) for g in range(SSD_GROUPS)]
        upd_ref[c, d] = jnp.concatenate(upd, axis=1)
        for h, sc in enumerate(scores):
            y = y + jnp.where(_head_mask(h), jnp.dot(sc, xdt_b, preferred_element_type=F32), 0.0)
    if need_y:
        y_ref[pl.ds(r0, T), :] = xs * dsk_ref[...] + y


def _ssd_chunk_state(c, d, c_ref, y_ref, upd_ref, dec_ref, st_ref, need_y):
    T = SSD_CHUNK
    r0 = pl.multiple_of(c * T, T)
    st = st_ref[d]
    if need_y:
        cm = c_ref[pl.ds(r0, T), :]
        st_b = st.astype(BF16)
        ys = [jnp.dot(cm[:, g * SSD_STATE:(g + 1) * SSD_STATE], st_b[:, g * GROUP_W:(g + 1) * GROUP_W],
                      preferred_element_type=F32) for g in range(SSD_GROUPS)]
        y_ref[pl.ds(r0, T), :] += jnp.concatenate(ys, axis=1) * dec_ref[c, d, 0:T, :]
    st_ref[d] = dec_ref[c, d, T:T + 1, :] * st + upd_ref[c, d]


def _ssd_kernel(z_ref, xs_in, bs_in, cs_in, dtr_ref, cw_ref, cb_ref, dtb_ref, alog_ref, dsk_ref, ng_ref, h0_ref,
                *refs, seq, need_y):
    if need_y:
        o_ref, hT_ref, xs_ref, b_ref, c_ref, dt_ref, la_ref, upd_ref, dec_ref, st_ref, y_ref = refs
    else:
        hT_ref, xs_ref, b_ref, c_ref, dt_ref, la_ref, upd_ref, dec_ref, st_ref = refs
        y_ref = None
    nc = seq // SSD_CHUNK
    row = lax.broadcasted_iota(jnp.int32, (seq, 1), 0)
    for gi, (src, dst) in enumerate(((xs_in, xs_ref), (bs_in, b_ref), (cs_in, c_ref))):
        sl = slice(gi * W_SSD, (gi + 1) * W_SSD)
        x = src[...].astype(F32)
        cv = (_shift_rows(x, 1, row) * cw_ref[0:1, sl] + x * cw_ref[1:2, sl]
              + _shift_rows(x, -1, row) * cw_ref[2:3, sl] + cb_ref[:, sl])
        dst[...] = (cv * jax.nn.sigmoid(cv)).astype(dst.dtype)
    dtv = dtr_ref[...] + dtb_ref[...]
    dt = jnp.maximum(dtv, 0.0) + jnp.log1p(jnp.exp(-jnp.abs(dtv)))
    dt_ref[...] = dt
    la_ref[...] = dt * (-jnp.exp(alog_ref[...]))
    st_ref[...] = h0_ref[0]

    def terms(c, carry):
        _ssd_chunk_terms(c, xs_ref, b_ref, c_ref, dt_ref, la_ref, dsk_ref, y_ref, upd_ref, dec_ref, need_y)
        return carry

    lax.fori_loop(0, nc, terms, 0, unroll=2)

    def recur(i, carry):
        _ssd_chunk_state(i, 0, c_ref, y_ref, upd_ref, dec_ref, st_ref, need_y)
        _ssd_chunk_state(nc - 1 - i, 1, c_ref, y_ref, upd_ref, dec_ref, st_ref, need_y)
        return carry

    lax.fori_loop(0, nc, recur, 0, unroll=2)
    hT_ref[0] = st_ref[...]
    if need_y:
        z = z_ref[...].astype(F32)
        yz = y_ref[...] * (z * jax.nn.sigmoid(z))
        ms = jnp.mean(yz * yz, axis=-1, keepdims=True)
        o_ref[...] = (yz * lax.rsqrt(ms + EPS) * ng_ref[...]).astype(o_ref.dtype)


def ssd_scan(pr, dt, h0, conv_w, conv_b, dt_bias, a_log, d_skip, norm_g, need_y):
    bn = h0.shape[0]
    seq = pr.shape[0] // bn
    nc = seq // SSD_CHUNK
    pad = LANES - 2 * SSD_HEADS
    dtb = jnp.pad(dt_bias.reshape(1, -1), ((0, 0), (0, pad)))
    alog = jnp.pad(a_log.reshape(1, -1), ((0, 0), (0, pad)))
    dsk = jnp.repeat(d_skip, SSD_HEAD_DIM)[None, :]
    blk = lambda col: pl.BlockSpec((seq, W_SSD), lambda b: (b, col))
    full = lambda a: pl.BlockSpec(a.shape, lambda b: (0,) * a.ndim)
    st_shape = (1, 2, SSD_STATE, W_SSD)
    st_spec = pl.BlockSpec(st_shape, lambda b: (b, 0, 0, 0))
    args = (conv_w, conv_b[None, :], dtb, alog, dsk, norm_g[None, :])
    out_shape = [jax.ShapeDtypeStruct((bn,) + st_shape[1:], F32)]
    out_specs = [st_spec]
    scratch = [pltpu.VMEM((seq, W_SSD), F32),
               pltpu.VMEM((seq, W_SSD), BF16), pltpu.VMEM((seq, W_SSD), BF16),
               pltpu.VMEM((seq, LANES), F32), pltpu.VMEM((seq, LANES), F32),
               pltpu.VMEM((nc, 2, SSD_STATE, W_SSD), F32),
               pltpu.VMEM((nc, 2, SSD_CHUNK + 8, W_SSD), F32),
               pltpu.VMEM(st_shape[1:], F32)]
    if need_y:
        out_shape.insert(0, jax.ShapeDtypeStruct((bn * seq, W_SSD), BF16))
        out_specs.insert(0, pl.BlockSpec((seq, W_SSD), lambda b: (b, 0)))
        scratch.append(pltpu.VMEM((seq, W_SSD), F32))
    res = pl.pallas_call(
        functools.partial(_ssd_kernel, seq=seq, need_y=need_y),
        out_shape=out_shape,
        grid=(bn,),
        in_specs=[blk(Z_COL), blk(XS_COL), blk(BS_COL), blk(CS_COL),
                  pl.BlockSpec((seq, LANES), lambda b: (b, 0))]
                 + [full(a) for a in args] + [st_spec],
        out_specs=out_specs,
        scratch_shapes=scratch,
        compiler_params=_params("parallel"),
        name="ssd_scan",
    )(pr, pr, pr, pr, dt, *args, h0)
    return (res[0], res[1]) if need_y else (None, res[0])


def ssd_mix(pr, dt, pr_c, dt_c, conv_w, conv_b, dt_bias, a_log, d_skip, norm_g, ctx_out, bn):
    h0 = jnp.zeros((bn, 2, SSD_STATE, W_SSD), F32)
    oc, hc = ssd_scan(pr_c, dt_c, h0, conv_w, conv_b, dt_bias, a_log, d_skip, norm_g, ctx_out)
    o, _ = ssd_scan(pr, dt, hc, conv_w, conv_b, dt_bias, a_log, d_skip, norm_g, True)
    return o, oc


def mixer(pr, dt, pr_c, dt_c, pool_w, pool_scale, conv_w, na_bias, s_conv_w, s_conv_b, dt_bias, a_log, d_skip,
          s_norm_g, ctx_out, bn):
    o_ssd, oc_ssd = ssd_mix(pr, dt, pr_c, dt_c, s_conv_w, s_conv_b, dt_bias, a_log, d_skip, s_norm_g, ctx_out, bn)
    o = [*local_mix(pr, pool_w, pool_scale, conv_w, bn), na_attention(pr, pr_c, na_bias, bn), o_ssd]
    if not ctx_out:
        return o, None
    oc = [*local_mix(pr_c, pool_w, pool_scale, conv_w, bn), ctx_attention(pr_c, bn), oc_ssd]
    return o, oc


TM = 512
TM_ROUTE = 1024
TM_PROJ = 1024
TM_GRP = 512


def kernel(x, c, ctx, c_ctx, w_ada, b_ada, g_mix, g_ffn, w_in, w_out, pool_w, pool_scale, conv_w, na_rpb,
           ssd_conv_w, ssd_conv_b, ssd_dt_bias, ssd_a_log, ssd_d, ssd_norm_g, ffn_w_gu, ffn_w_down,
           moe_router, moe_w_gu, moe_w_down, g_final):
    bn, S, d = x.shape
    Lc = ctx.shape[1]
    m, mc = bn * S, bn * Lc
    tpb = S // TM
    x2 = x.reshape(m, d)
    xc2 = ctx.reshape(mc, d)
    c_rows = jnp.concatenate([c, c_ctx[None, :], jnp.zeros((7, d), F32)], axis=0)
    for l in range(DEPTH):
        ctx_out = l < DEPTH - 1
        last = l == DEPTH - 1
        ada = ada_modulation(c_rows, w_ada[l].astype(BF16), b_ada[l][None, :])
        mod = ada[:bn].reshape(bn, 6, d)
        mod_c = ada[bn:bn + 1].reshape(1, 6, d)
        w_in_l = jnp.pad(w_in[l], ((0, 0), (0, D_IN_PAD - D_IN_PROJ))).astype(BF16)
        g_m = g_mix[l][None, :]
        g_f = g_ffn[l][None, :]
        pr, dt = in_proj(x2, g_m, mod, w_in_l, S // TM_PROJ, TM_PROJ)
        pr_c, dt_c = in_proj(xc2, g_m, mod_c, w_in_l, None, min(TM_PROJ, mc))
        na_bias = na_bias_table(na_rpb[l], S // GRID_W)
        o, oc = mixer(pr, dt, pr_c, dt_c, pool_w[l], pool_scale[l], conv_w[l], na_bias, ssd_conv_w[l],
                      ssd_conv_b[l], ssd_dt_bias[l], ssd_a_log[l], ssd_d[l], ssd_norm_g[l], ctx_out, bn)
        w_out_l = w_out[l].astype(BF16)
        j = l // 2
        if l % 2 == 0:
            w_gu = ffn_w_gu[j].astype(BF16)
            w_dn = ffn_w_down[j].astype(BF16)
            x2 = ffn_dense(x2, o, g_f, mod, w_out_l, w_gu, w_dn, tpb, TM)
            if ctx_out:
                xc2 = ffn_dense(xc2, oc, g_f, mod_c, w_out_l, w_gu, w_dn, None, min(TM, mc))
        else:
            w_r = jnp.pad(moe_router[j], ((0, 0), (0, LANES - N_EXPERTS))).astype(BF16)
            w_gu, w_dn = moe_w_gu[j], moe_w_down[j]
            x2 = moe_block(x2, o, g_f, mod, w_out_l, w_r, w_gu, w_dn, S // TM_ROUTE, TM_ROUTE, TM_GRP,
                           g_final[None, :] if last else None)
            if ctx_out:
                xc2 = moe_block(xc2, oc, g_f, mod_c, w_out_l, w_r, w_gu, w_dn, None, min(TM_ROUTE, mc), TM_GRP)
            if last:
                return x2.reshape(bn, S, d)
    return final_norm(x2, g_final[None, :], TM).reshape(bn, S, d)
```

```python
import functools
import math

import numpy as np
import jax
import jax.numpy as jnp
from jax import lax
from jax.experimental import pallas as pl
from jax.experimental.pallas import tpu as pltpu

DEPTH = 2
GRID_W = 64
EPS = 1e-6
W_POOL = 256
W_CONV = 256
W_NA = 256
W_SSD = 256
POOL_WINDOWS = (2, 4, 8, 16)
POOL_GROUP = W_POOL // len(POOL_WINDOWS)
NA_HEADS = 4
NA_HEAD_DIM = W_NA // NA_HEADS
WIN_R = 8
WIN_C = 16
SSD_HEAD_DIM = 64
SSD_HEADS = W_SSD // SSD_HEAD_DIM
SSD_GROUPS = 2
SSD_STATE = 128
SSD_CHUNK = 128
SSD_XBC = W_SSD + 2 * SSD_GROUPS * SSD_STATE
D_IN_PROJ = W_POOL + 3 * W_CONV + 3 * W_NA + W_SSD + SSD_XBC + 2 * SSD_HEADS
N_EXPERTS = 8
TOP_K = 2

LANES = 128
D_IN_PAD = -(-D_IN_PROJ // LANES) * LANES
VMEM_LIMIT = 56 * 1024 * 1024
BF16 = jnp.bfloat16
F32 = jnp.float32


def _params(*sem):
    return pltpu.CompilerParams(dimension_semantics=sem, vmem_limit_bytes=VMEM_LIMIT)


def _norm_mod(x, g, scale, shift):
    ms = jnp.mean(x * x, axis=-1, keepdims=True)
    return (x * lax.rsqrt(ms + EPS) * g) * (1.0 + scale) + shift


def _mod_map(tiles_per_batch):
    if tiles_per_batch is None:
        return lambda i, *_: (0, 0, 0)
    return lambda i, *_: (i // tiles_per_batch, 0, 0)


def _resident(shape, index_map):
    return pl.BlockSpec(shape, index_map, pipeline_mode=pl.Buffered(1))


def _ada_kernel(c_ref, w_ref, b_ref, o_ref):
    c = c_ref[...]
    s = c * jax.nn.sigmoid(c)
    o_ref[...] = jnp.dot(s.astype(BF16), w_ref[...], preferred_element_type=F32) + b_ref[...]


def ada_modulation(c_rows, w, b):
    r, d = c_rows.shape
    n = w.shape[1]
    tn = 1536
    return pl.pallas_call(
        _ada_kernel,
        out_shape=jax.ShapeDtypeStruct((r, n), F32),
        grid=(n // tn,),
        in_specs=[pl.BlockSpec((r, d), lambda j: (0, 0)),
                  pl.BlockSpec((d, tn), lambda j: (0, j)),
                  pl.BlockSpec((1, tn), lambda j: (0, j))],
        out_specs=pl.BlockSpec((r, tn), lambda j: (0, j)),
        compiler_params=_params("arbitrary"),
        name="ada_modulation",
    )(c_rows, w, b)


def _in_proj_kernel(x_ref, g_ref, mod_ref, w_ref, o_ref, dt_ref):
    h = _norm_mod(x_ref[...], g_ref[...], mod_ref[0, 1:2, :], mod_ref[0, 0:1, :])
    pr = jnp.dot(h.astype(BF16), w_ref[...], preferred_element_type=F32)
    o_ref[...] = pr.astype(o_ref.dtype)
    dt_ref[...] = pr[:, DT_COL * LANES:(DT_COL + 1) * LANES]


def in_proj(x2, g, mod, w, tiles_per_batch, tm):
    m, d = x2.shape
    n = w.shape[1]
    return pl.pallas_call(
        _in_proj_kernel,
        out_shape=(jax.ShapeDtypeStruct((m, n), BF16), jax.ShapeDtypeStruct((m, LANES), F32)),
        grid=(m // tm,),
        in_specs=[pl.BlockSpec((tm, d), lambda i: (i, 0)),
                  pl.BlockSpec((1, d), lambda i: (0, 0)),
                  pl.BlockSpec((1, 6, d), _mod_map(tiles_per_batch)),
                  _resident((d, n), lambda i: (0, 0))],
        out_specs=(pl.BlockSpec((tm, n), lambda i: (i, 0)), pl.BlockSpec((tm, LANES), lambda i: (i, 0))),
        compiler_params=_params("parallel"),
        name="in_proj",
    )(x2, g, mod, w)


def _mix_residual(x, mod_ref, w_ref, part_refs):
    y, k0 = None, 0
    for p_ref in part_refs:
        k = p_ref.shape[1]
        t = jnp.dot(p_ref[...], w_ref[k0:k0 + k, :], preferred_element_type=F32)
        y = t if y is None else y + t
        k0 += k
    return x + mod_ref[0, 2:3, :] * y


def _part_specs(parts, tm):
    return [pl.BlockSpec((tm, p.shape[1]), lambda i, *_: (i, 0)) for p in parts]


FF_CHUNK = 512


def _swiglu(h, wgu, wd, ff):
    acc = None
    for c0 in range(0, ff, FF_CHUNK):
        c1 = min(c0 + FF_CHUNK, ff)
        gg = jnp.dot(h, wgu(c0, c1), preferred_element_type=F32)
        uu = jnp.dot(h, wgu(ff + c0, ff + c1), preferred_element_type=F32)
        a = (gg * jax.nn.sigmoid(gg) * uu).astype(BF16)
        part = jnp.dot(a, wd(c0, c1), preferred_element_type=F32)
        acc = part if acc is None else acc + part
    return acc


def _ffn_kernel(x_ref, g_ref, mod_ref, wout_ref, wgu_ref, wd_ref, *refs):
    x = _mix_residual(x_ref[...], mod_ref, wout_ref, refs[:-1])
    y_ref = refs[-1]
    h = _norm_mod(x, g_ref[...], mod_ref[0, 4:5, :], mod_ref[0, 3:4, :]).astype(BF16)
    y = _swiglu(h, lambda a, b: wgu_ref[:, a:b], lambda a, b: wd_ref[a:b, :], wd_ref.shape[0])
    y_ref[...] = x + mod_ref[0, 5:6, :] * y


def ffn_dense(x2, parts, g, mod, w_out, w_gu, w_down, tiles_per_batch, tm):
    m, d = x2.shape
    return pl.pallas_call(
        _ffn_kernel,
        out_shape=jax.ShapeDtypeStruct((m, d), F32),
        grid=(m // tm,),
        in_specs=[pl.BlockSpec((tm, d), lambda i: (i, 0)),
                  pl.BlockSpec((1, d), lambda i: (0, 0)),
                  pl.BlockSpec((1, 6, d), _mod_map(tiles_per_batch)),
                  _resident(w_out.shape, lambda i: (0, 0)),
                  _resident(w_gu.shape, lambda i: (0, 0)),
                  _resident(w_down.shape, lambda i: (0, 0))] + _part_specs(parts, tm),
        out_specs=pl.BlockSpec((tm, d), lambda i: (i, 0)),
        compiler_params=_params("parallel"),
        name="ffn_dense",
    )(x2, g, mod, w_out, w_gu, w_down, *parts)


ROUTE_E1, ROUTE_E2, ROUTE_R1, ROUTE_R2, ROUTE_G1, ROUTE_G2 = range(6)
ROUTE_ROWS = 8


def _router_kernel(x_ref, g_ref, mod_ref, wout_ref, wr_ref, *refs):
    part_refs = refs[:-6]
    x1_ref, h_ref, route_ref, route_t_ref, cnt_ref, base_ref = refs[-6:]

    @pl.when(pl.program_id(0) == 0)
    def _():
        base_ref[...] = jnp.zeros_like(base_ref)

    x = _mix_residual(x_ref[...], mod_ref, wout_ref, part_refs)
    x1_ref[...] = x
    h = _norm_mod(x, g_ref[...], mod_ref[0, 4:5, :], mod_ref[0, 3:4, :]).astype(BF16)
    h_ref[...] = h
    tm = h.shape[0]
    logits = jnp.dot(h, wr_ref[...], preferred_element_type=F32)
    lane = lax.broadcasted_iota(jnp.int32, logits.shape, 1)
    neg = jnp.float32(-jnp.inf)
    logits = jnp.where(lane < N_EXPERTS, logits, neg)
    m1 = jnp.max(logits, axis=-1, keepdims=True)
    i1 = jnp.min(jnp.where(logits == m1, lane, LANES), axis=-1, keepdims=True)
    rest = jnp.where(lane == i1, neg, logits)
    m2 = jnp.max(rest, axis=-1, keepdims=True)
    i2 = jnp.min(jnp.where(rest == m2, lane, LANES), axis=-1, keepdims=True)
    e2 = jnp.exp(m2 - m1)
    g1 = 1.0 / (1.0 + e2)
    g2 = e2 / (1.0 + e2)
    sel1, sel2 = lane == i1, lane == i2
    sel = jnp.where(sel1 | sel2, 1.0, 0.0)
    li = lax.broadcasted_iota(jnp.int32, (tm, tm), 0)
    si = lax.broadcasted_iota(jnp.int32, (tm, tm), 1)
    before = jnp.where(si < li, 1.0, 0.0).astype(BF16)
    rank = jnp.dot(before, sel.astype(BF16), preferred_element_type=F32) + base_ref[...]
    r1 = jnp.sum(jnp.where(sel1, rank, 0.0), axis=-1, keepdims=True)
    r2 = jnp.sum(jnp.where(sel2, rank, 0.0), axis=-1, keepdims=True)
    base_ref[...] += jnp.sum(sel, axis=0, keepdims=True)
    cnt_ref[...] = base_ref[...]
    fields = (i1.astype(F32), i2.astype(F32), r1, r2, g1, g2)
    route = jnp.zeros(logits.shape, F32)
    for k, v in enumerate(fields):
        route = jnp.where(lane == k, v, route)
    route_ref[...] = route
    route_t_ref[...] = route.T[:ROUTE_ROWS, :]


def moe_router(x2, parts, g, mod, w_out, w_router, tiles_per_batch, tm):
    m, d = x2.shape
    return pl.pallas_call(
        _router_kernel,
        out_shape=(jax.ShapeDtypeStruct((m, d), F32), jax.ShapeDtypeStruct((m, d), BF16),
                   jax.ShapeDtypeStruct((m, LANES), F32), jax.ShapeDtypeStruct((ROUTE_ROWS, m), F32),
                   jax.ShapeDtypeStruct((1, LANES), F32)),
        grid=(m // tm,),
        in_specs=[pl.BlockSpec((tm, d), lambda i: (i, 0)),
                  pl.BlockSpec((1, d), lambda i: (0, 0)),
                  pl.BlockSpec((1, 6, d), _mod_map(tiles_per_batch)),
                  _resident(w_out.shape, lambda i: (0, 0)),
                  pl.BlockSpec((d, LANES), lambda i: (0, 0))] + _part_specs(parts, tm),
        out_specs=(pl.BlockSpec((tm, d), lambda i: (i, 0)),
                   pl.BlockSpec((tm, d), lambda i: (i, 0)),
                   pl.BlockSpec((tm, LANES), lambda i: (i, 0)),
                   pl.BlockSpec((ROUTE_ROWS, tm), lambda i: (0, i)),
                   pl.BlockSpec((1, LANES), lambda i: (0, 0))),
        scratch_shapes=[pltpu.VMEM((1, LANES), F32)],
        compiler_params=_params("arbitrary"),
        name="moe_router",
    )(x2, g, mod, w_out, w_router, *parts)


def _moe_kernel(tile_ref, exp_ref, start_ref, end_ref, xg_ref, wgu_ref, wd_ref, y_ref, wgu_b, wd_b):
    w = pl.program_id(0)
    tm = xg_ref.shape[0]
    start, end = start_ref[w], end_ref[w]
    t0 = tile_ref[w] * tm

    @pl.when((w == 0) | (exp_ref[w] != exp_ref[jnp.maximum(w - 1, 0)]))
    def _():
        wgu_b[...] = wgu_ref[0].astype(BF16)
        wd_b[...] = wd_ref[0].astype(BF16)

    @pl.when(end > start)
    def _():
        y = _swiglu(xg_ref[...], lambda a, b: wgu_b[:, a:b], lambda a, b: wd_b[a:b, :], wd_b.shape[0])
        y = y.astype(y_ref.dtype)

        @pl.when(start == t0)
        def _():
            y_ref[...] = y

        @pl.when(start != t0)
        def _():
            row = t0 + lax.broadcasted_iota(jnp.int32, (tm, 1), 0)
            y_ref[...] = jnp.where(row >= start, y, y_ref[...])


def moe_grouped(item_tile, item_expert, item_start, item_end, xg, w_gu, w_down, tm):
    p, d = xg.shape
    grid_spec = pltpu.PrefetchScalarGridSpec(
        num_scalar_prefetch=4,
        grid=(item_tile.shape[0],),
        in_specs=[pl.BlockSpec((tm, d), lambda w, it, ie, s, e: (it[w], 0)),
                  _resident((1,) + w_gu.shape[1:], lambda w, it, ie, s, e: (ie[w], 0, 0)),
                  _resident((1,) + w_down.shape[1:], lambda w, it, ie, s, e: (ie[w], 0, 0))],
        out_specs=pl.BlockSpec((tm, d), lambda w, it, ie, s, e: (it[w], 0)),
        scratch_shapes=[pltpu.VMEM(w_gu.shape[1:], BF16), pltpu.VMEM(w_down.shape[1:], BF16)],
    )
    return pl.pallas_call(
        _moe_kernel,
        out_shape=jax.ShapeDtypeStruct((p, d), BF16),
        grid_spec=grid_spec,
        compiler_params=_params("arbitrary"),
        name="moe_grouped",
    )(item_tile, item_expert, item_start, item_end, xg, w_gu, w_down)


def _moe_combine_kernel(x_ref, ya_ref, yb_ref, route_ref, mod_ref, *refs):
    o_ref = refs[-1]
    r = route_ref[...]
    y = (r[:, ROUTE_G1:ROUTE_G1 + 1] * ya_ref[...].astype(F32) + r[:, ROUTE_G2:ROUTE_G2 + 1] * yb_ref[...].astype(F32))
    x = x_ref[...] + mod_ref[0, 5:6, :] * y
    if len(refs) == 2:
        ms = jnp.mean(x * x, axis=-1, keepdims=True)
        x = x * lax.rsqrt(ms + EPS) * refs[0][...]
    o_ref[...] = x


def moe_combine(x2, ya, yb, route, mod, tiles_per_batch, tm, g_final=None):
    m, d = x2.shape
    row = pl.BlockSpec((tm, d), lambda i: (i, 0))
    specs = [row, row, row, pl.BlockSpec((tm, LANES), lambda i: (i, 0)),
             pl.BlockSpec((1, 6, d), _mod_map(tiles_per_batch))]
    args = [x2, ya, yb, route, mod]
    if g_final is not None:
        specs.append(pl.BlockSpec((1, d), lambda i: (0, 0)))
        args.append(g_final)
    return pl.pallas_call(
        _moe_combine_kernel,
        out_shape=jax.ShapeDtypeStruct((m, d), F32),
        grid=(m // tm,),
        in_specs=specs,
        out_specs=row,
        compiler_params=_params("parallel"),
        name="moe_combine",
    )(*args)


def moe_block(x2, parts, g, mod, w_out, w_router, w_gu, w_down, tiles_per_batch, tm_tok, tm_grp, g_final=None):
    m, d = x2.shape
    x2, h, route, route_t, cnt = moe_router(x2, parts, g, mod, w_out, w_router, tiles_per_batch, tm_tok)
    cnt = cnt[0, :N_EXPERTS].astype(jnp.int32)
    p = m * TOP_K
    off = jnp.cumsum(cnt) - cnt
    field = lambda k: route_t[k].astype(jnp.int32)

    def row_of(e, r):
        for k in range(N_EXPERTS):
            r = r + jnp.where(e == k, off[k], 0)
        return r

    d1 = row_of(field(ROUTE_E1), field(ROUTE_R1))
    d2 = row_of(field(ROUTE_E2), field(ROUTE_R2))
    tok = jnp.arange(m, dtype=jnp.int32)
    _, src = lax.sort_key_val(jnp.concatenate([d1, d2]), jnp.concatenate([tok, tok]))
    n_row_tiles = p // tm_grp
    start = jnp.sort(jnp.concatenate([jnp.arange(n_row_tiles, dtype=jnp.int32) * tm_grp, off]))
    end = jnp.concatenate([start[1:], jnp.full((1,), p, jnp.int32)])
    item_tile = jnp.minimum(start // tm_grp, n_row_tiles - 1)
    item_expert = jnp.sum((off[None, :] <= start[:, None]).astype(jnp.int32), axis=1) - 1
    rows = lambda a, idx: a.at[idx].get(mode="promise_in_bounds")
    yg = moe_grouped(item_tile, item_expert, start, end, rows(h, src), w_gu, w_down, tm_grp)
    return moe_combine(x2, rows(yg, d1), rows(yg, d2), route, mod, tiles_per_batch, tm_tok, g_final)


def _final_norm_kernel(x_ref, g_ref, o_ref):
    x = x_ref[...]
    ms = jnp.mean(x * x, axis=-1, keepdims=True)
    o_ref[...] = x * lax.rsqrt(ms + EPS) * g_ref[...]


def final_norm(x2, g, tm):
    m, d = x2.shape
    return pl.pallas_call(
        _final_norm_kernel,
        out_shape=jax.ShapeDtypeStruct((m, d), F32),
        grid=(m // tm,),
        in_specs=[pl.BlockSpec((tm, d), lambda i: (i, 0)), pl.BlockSpec((1, d), lambda i: (0, 0))],
        out_specs=pl.BlockSpec((tm, d), lambda i: (i, 0)),
        compiler_params=_params("parallel"),
        name="final_norm",
    )(x2, g)


NA_QROWS = 4
NA_KROWS = NA_QROWS + WIN_R
Q_COL, K_COL, V_COL = 4, 5, 6


def _na_key_start(j, rows):
    return np.clip(j * NA_QROWS - WIN_R // 2, 0, rows - NA_KROWS)


def _na_bias_index(rows):
    nblk = rows // NA_QROWS
    pats = []
    for j in range(nblk):
        start = _na_key_start(j, rows)
        r = j * NA_QROWS + np.arange(NA_QROWS)
        sr = np.clip(r - WIN_R // 2, 0, rows - WIN_R)
        kr = start + np.arange(NA_KROWS)
        rvalid = (kr[None, :] >= sr[:, None]) & (kr[None, :] < sr[:, None] + WIN_R)
        ri = np.clip(kr[None, :] - r[:, None] + WIN_R - 1, 0, 2 * WIN_R - 2)
        pats.append((ri, rvalid))
    for j in range(2, nblk - 1):
        assert all(np.array_equal(a, b) for a, b in zip(pats[1], pats[j]))
    sel = [pats[0], pats[1], pats[nblk - 1]]
    ri = np.stack([p[0] for p in sel])
    rvalid = np.stack([p[1] for p in sel])
    c = np.arange(GRID_W)
    sc = np.clip(c - WIN_C // 2, 0, GRID_W - WIN_C)
    cvalid = (c[None, :] >= sc[:, None]) & (c[None, :] < sc[:, None] + WIN_C)
    ci = np.clip(c[None, :] - c[:, None] + WIN_C - 1, 0, 2 * WIN_C - 2)
    c_onehot = (ci[..., None] == np.arange(2 * WIN_C - 1)).astype(np.float32)
    valid = rvalid[:, :, None, :, None] & cvalid[None, None, :, None, :]
    return ri, c_onehot, valid


def na_bias_table(rpb, rows):
    ri, c_onehot, valid = _na_bias_index(rows)
    toep = jnp.einsum('hrd,qkd->hrqk', rpb, c_onehot, precision=lax.Precision.HIGHEST)
    b = jnp.take(toep, ri.reshape(-1), axis=1).reshape((NA_HEADS,) + ri.shape + (GRID_W, GRID_W))
    b = jnp.transpose(b, (1, 0, 2, 4, 3, 5))
    b = jnp.where(valid[:, None], b, -jnp.inf)
    return b.reshape(3, NA_HEADS, NA_QROWS * GRID_W, NA_KROWS * GRID_W).astype(F32)


def _attend_heads(q, key_sets, bias_fn):
    n, w = q.shape
    lane = lax.broadcasted_iota(jnp.int32, (1, w), 1)
    head_masks = [(lane >= h * NA_HEAD_DIM) & (lane < (h + 1) * NA_HEAD_DIM) for h in range(NA_HEADS)]
    all_scores = []
    for h, mh in enumerate(head_masks):
        qh = jnp.where(mh, q, 0.0).astype(BF16)
        scores = []
        for i, (k, _) in enumerate(key_sets):
            s = lax.dot_general(qh, k, (((1,), (1,)), ((), ())), preferred_element_type=F32)
            b = bias_fn(i, h)
            scores.append(s if b is None else s + b)
        all_scores.append(scores)
    all_probs = []
    for scores in all_scores:
        m = scores[0].max(axis=-1, keepdims=True)
        for s in scores[1:]:
            m = jnp.maximum(m, s.max(axis=-1, keepdims=True))
        denom = jnp.zeros((n, 1), F32)
        probs = []
        for s in scores:
            p = jnp.exp(s - m)
            denom = denom + p.sum(axis=-1, keepdims=True)
            probs.append(p.astype(BF16))
        all_probs.append((probs, denom))
    out = jnp.zeros((n, w), F32)
    for mh, (probs, denom) in zip(head_masks, all_probs):
        acc = jnp.zeros((n, w), F32)
        for p, (_, v) in zip(probs, key_sets):
            acc = acc + jnp.dot(p, v, preferred_element_type=F32)
        out = out + jnp.where(mh, acc / denom, 0.0)
    return out


def _na_kernel(q_ref, k_ref, v_ref, kc_ref, vc_ref, bias_ref, o_ref, *, rows):
    j = pl.program_id(1)
    start = jnp.clip(j * NA_QROWS - WIN_R // 2, 0, rows - NA_KROWS)
    t0 = pl.multiple_of(start * GRID_W, GRID_W)
    nk = NA_KROWS * GRID_W
    kw = k_ref[pl.ds(t0, nk), :]
    vw = v_ref[pl.ds(t0, nk), :]
    q = q_ref[...] * (1.0 / math.sqrt(NA_HEAD_DIM))
    o = _attend_heads(q, [(kw, vw), (kc_ref[...], vc_ref[...])], lambda i, h: bias_ref[0, h] if i == 0 else None)
    o_ref[...] = o.astype(o_ref.dtype)


def na_attention(pr, pr_c, bias, bn):
    S, Lc = pr.shape[0] // bn, pr_c.shape[0] // bn
    rows = S // GRID_W
    nblk = rows // NA_QROWS
    nq = NA_QROWS * GRID_W

    def pat(b, j):
        return (jnp.where(j == 0, 0, jnp.where(j == nblk - 1, 2, 1)), 0, 0, 0)

    return pl.pallas_call(
        functools.partial(_na_kernel, rows=rows),
        out_shape=jax.ShapeDtypeStruct((bn * S, W_NA), BF16),
        grid=(bn, nblk),
        in_specs=[pl.BlockSpec((nq, W_NA), lambda b, j: (b * nblk + j, Q_COL)),
                  pl.BlockSpec((S, W_NA), lambda b, j: (b, K_COL)),
                  pl.BlockSpec((S, W_NA), lambda b, j: (b, V_COL)),
                  pl.BlockSpec((Lc, W_NA), lambda b, j: (b, K_COL)),
                  pl.BlockSpec((Lc, W_NA), lambda b, j: (b, V_COL)),
                  pl.BlockSpec((1,) + bias.shape[1:], pat)],
        out_specs=pl.BlockSpec((nq, W_NA), lambda b, j: (b * nblk + j, 0)),
        compiler_params=_params("parallel", "arbitrary"),
        name="na_attention",
    )(pr, pr, pr, pr_c, pr_c, bias)


def _ctx_attn_kernel(q_ref, k_ref, v_ref, o_ref):
    q = q_ref[...] * (1.0 / math.sqrt(NA_HEAD_DIM))
    o = _attend_heads(q, [(k_ref[...], v_ref[...])], lambda i, h: None)
    o_ref[...] = o.astype(o_ref.dtype)


def ctx_attention(pr_c, bn):
    Lc = pr_c.shape[0] // bn
    return pl.pallas_call(
        _ctx_attn_kernel,
        out_shape=jax.ShapeDtypeStruct((bn * Lc, W_NA), BF16),
        grid=(bn,),
        in_specs=[pl.BlockSpec((Lc, W_NA), lambda b: (b, Q_COL)),
                  pl.BlockSpec((Lc, W_NA), lambda b: (b, K_COL)),
                  pl.BlockSpec((Lc, W_NA), lambda b: (b, V_COL))],
        out_specs=pl.BlockSpec((Lc, W_NA), lambda b: (b, 0)),
        compiler_params=_params("parallel"),
        name="ctx_attention",
    )(pr_c, pr_c, pr_c)


POOL_COL, CONV_H_COL, CONV_B_COL, CONV_C_COL = 0, 1, 2, 3


def _shift_rows(x, k, row):
    n = x.shape[0]
    y = pltpu.roll(x, k % n, 0)
    return jnp.where(row < k, 0.0, y) if k > 0 else jnp.where(row >= n + k, 0.0, y)


def _local_mix_kernel(u_ref, h_ref, bg_ref, cg_ref, pw_ref, ps_ref, cw_ref, op_ref, oc_ref, *, seq):
    row = lax.broadcasted_iota(jnp.int32, (seq, 1), 0)
    lane = lax.broadcasted_iota(jnp.int32, (1, W_POOL), 1)
    u = u_ref[...].astype(F32)
    trailing, leading = {1: u}, {1: u}
    for w in (1, 2, 4):
        trailing[2 * w] = trailing[w] + _shift_rows(trailing[w], w, row)
        leading[2 * w] = leading[w] + _shift_rows(leading[w], -w, row)
    rowf = row.astype(F32)
    win_sum = jnp.zeros_like(u)
    cnt = jnp.zeros_like(u)
    for g, win in enumerate(POOL_WINDOWS):
        half = win // 2
        mg = (lane >= g * POOL_GROUP) & (lane < (g + 1) * POOL_GROUP)
        s = _shift_rows(trailing[half], 1, row) + leading[half]
        n = jnp.minimum(rowf + half, float(seq)) - jnp.maximum(rowf - half, 0.0)
        win_sum = jnp.where(mg, s, win_sum)
        cnt = jnp.where(mg, n, cnt)
    p = win_sum / cnt - u
    pooled = jnp.dot(p.astype(BF16), pw_ref[...], preferred_element_type=F32) * ps_ref[...]
    op_ref[...] = pooled.astype(op_ref.dtype)
    v = cg_ref[...].astype(F32) * h_ref[...].astype(F32)
    conv = (_shift_rows(v, 1, row) * cw_ref[0:1, :] + v * cw_ref[1:2, :] + _shift_rows(v, -1, row) * cw_ref[2:3, :])
    oc_ref[...] = (bg_ref[...].astype(F32) * conv).astype(oc_ref.dtype)


def local_mix(pr, pool_w, pool_scale, conv_w, bn):
    seq = pr.shape[0] // bn
    pw = jax.scipy.linalg.block_diag(*[pool_w[g] for g in range(len(POOL_WINDOWS))]).astype(BF16)
    blk = lambda col: pl.BlockSpec((seq, W_POOL), lambda b: (b, col))
    full = lambda a: pl.BlockSpec(a.shape, lambda b: (0,) * a.ndim)
    args = (pw, pool_scale[None, :], conv_w)
    return pl.pallas_call(
        functools.partial(_local_mix_kernel, seq=seq),
        out_shape=(jax.ShapeDtypeStruct((bn * seq, W_POOL), BF16), jax.ShapeDtypeStruct((bn * seq, W_CONV), BF16)),
        grid=(bn,),
        in_specs=[blk(POOL_COL), blk(CONV_H_COL), blk(CONV_B_COL), blk(CONV_C_COL)] + [full(a) for a in args],
        out_specs=(pl.BlockSpec((seq, W_POOL), lambda b: (b, 0)),
                   pl.BlockSpec((seq, W_CONV), lambda b: (b, 0))),
        compiler_params=_params("parallel"),
        name="local_mix",
    )(pr, pr, pr, pr, *args)


Z_COL, XS_COL, BS_COL, CS_COL = 7, 8, 9, 10
DT_COL = 22
HEADS_PER_GROUP = SSD_HEADS // SSD_GROUPS
GROUP_W = HEADS_PER_GROUP * SSD_HEAD_DIM


def _head_mask(h):
    lane = lax.broadcasted_iota(jnp.int32, (1, W_SSD), 1)
    return (lane >= h * SSD_HEAD_DIM) & (lane < (h + 1) * SSD_HEAD_DIM)


def _expand_heads(v, d):
    if v.shape[0] == 1:
        out = jnp.zeros((1, W_SSD), F32)
        for h in range(SSD_HEADS):
            k = d * SSD_HEADS + h
            out = jnp.where(_head_mask(h), v[:, k:k + 1], out)
        return out
    src = lax.broadcasted_iota(jnp.int32, (LANES, W_SSD), 0)
    dst_head = lax.broadcasted_iota(jnp.int32, (LANES, W_SSD), 1) // SSD_HEAD_DIM
    select = (src == d * SSD_HEADS + dst_head).astype(F32)
    return jnp.dot(v, select, precision=lax.Precision.HIGHEST, preferred_element_type=F32)


def _ssd_chunk_terms(c, xs_ref, b_ref, c_ref, dt_ref, la_ref, dsk_ref, y_ref, upd_ref, dec_ref, need_y):
    T = SSD_CHUNK
    r0 = pl.multiple_of(c * T, T)
    xs = xs_ref[pl.ds(r0, T), :]
    bm = b_ref[pl.ds(r0, T), :]
    cm = c_ref[pl.ds(r0, T), :]
    dt = dt_ref[pl.ds(r0, T), :]
    la = la_ref[pl.ds(r0, T), :]
    li = lax.broadcasted_iota(jnp.int32, (T, T), 0)
    si = lax.broadcasted_iota(jnp.int32, (T, T), 1)
    causal = si <= li
    prefix = jnp.dot(causal.astype(F32), la, precision=lax.Precision.HIGHEST, preferred_element_type=F32)
    total = prefix[T - 1:T, :]
    lane = lax.broadcasted_iota(jnp.int32, (1, LANES), 1)
    acum = jnp.where(lane < SSD_HEADS, prefix, total - prefix + la)
    bt = bm.astype(F32).T.astype(BF16)
    if need_y:
        acum_t = acum.T
        cb = [lax.dot_general(cm[:, g * SSD_STATE:(g + 1) * SSD_STATE], bm[:, g * SSD_STATE:(g + 1) * SSD_STATE],
                              (((1,), (1,)), ((), ())), preferred_element_type=F32) for g in range(SSD_GROUPS)]
        y = jnp.zeros((T, W_SSD), F32)
    per_dir = []
    for d in range(2):
        acum_e = _expand_heads(acum, d)
        tot_e = _expand_heads(total, d)
        xdt = xs * _expand_heads(dt, d)
        xw = (xdt * jnp.exp(tot_e - acum_e)).astype(BF16)
        dec_ref[c, d, 0:T, :] = jnp.exp(acum_e)
        dec_ref[c, d, T:T + 1, :] = jnp.exp(tot_e)
        scores = []
        if need_y:
            mask = causal if d == 0 else (si >= li)
            for h in range(SSD_HEADS):
                k = d * SSD_HEADS + h
                decay = jnp.exp(jnp.where(mask, acum[:, k:k + 1] - acum_t[k:k + 1, :], -jnp.inf))
                scores.append((cb[h // HEADS_PER_GROUP] * decay).astype(BF16))
        per_dir.append((xw, xdt, scores))
    for d, (xw, xdt, scores) in enumerate(per_dir):
        upd = [jnp.dot(bt[g * SSD_STATE:(g + 1) * SSD_STATE, :], xw[:, g * GROUP_W:(g + 1) * GROUP_W],
                       preferred_element_type=F32) for g in range(SSD_GROUPS)]
        upd_ref[c, d] = jnp.concatenate(upd, axis=1)
        if need_y:
            x_heads = [jnp.where(_head_mask(h), xdt, 0.0).astype(BF16) for h in range(SSD_HEADS)]
            y = y + jnp.dot(jnp.concatenate(scores, axis=1), jnp.concatenate(x_heads, axis=0),
                            preferred_element_type=F32)
    if need_y:
        y_ref[pl.ds(r0, T), :] = xs * dsk_ref[...] + y


def _ssd_chunk_state(c, d, c_ref, y_ref, upd_ref, dec_ref, st_ref, need_y):
    T = SSD_CHUNK
    r0 = pl.multiple_of(c * T, T)
    st = st_ref[d]
    if need_y:
        cm = c_ref[pl.ds(r0, T), :]
        st_b = st.astype(BF16)
        ys = [jnp.dot(cm[:, g * SSD_STATE:(g + 1) * SSD_STATE], st_b[:, g * GROUP_W:(g + 1) * GROUP_W],
                      preferred_element_type=F32) for g in range(SSD_GROUPS)]
        y_ref[pl.ds(r0, T), :] += jnp.concatenate(ys, axis=1) * dec_ref[c, d, 0:T, :]
    st_ref[d] = dec_ref[c, d, T:T + 1, :] * st + upd_ref[c, d]


def _ssd_kernel(z_ref, xs_in, bs_in, cs_in, dtr_ref, cw_ref, cb_ref, dtb_ref, alog_ref, dsk_ref, ng_ref, h0_ref,
                *refs, seq, need_y):
    if need_y:
        o_ref, hT_ref, xs_ref, b_ref, c_ref, dt_ref, la_ref, upd_ref, dec_ref, st_ref, y_ref = refs
    else:
        hT_ref, xs_ref, b_ref, c_ref, dt_ref, la_ref, upd_ref, dec_ref, st_ref = refs
        y_ref = None
    nc = seq // SSD_CHUNK
    row = lax.broadcasted_iota(jnp.int32, (seq, 1), 0)
    for gi, (src, dst) in enumerate(((xs_in, xs_ref), (bs_in, b_ref), (cs_in, c_ref))):
        sl = slice(gi * W_SSD, (gi + 1) * W_SSD)
        x = src[...].astype(F32)
        cv = (_shift_rows(x, 1, row) * cw_ref[0:1, sl] + x * cw_ref[1:2, sl]
              + _shift_rows(x, -1, row) * cw_ref[2:3, sl] + cb_ref[:, sl])
        dst[...] = (cv * jax.nn.sigmoid(cv)).astype(dst.dtype)
    dtv = dtr_ref[...] + dtb_ref[...]
    dt = jnp.maximum(dtv, 0.0) + jnp.log1p(jnp.exp(-jnp.abs(dtv)))
    dt_ref[...] = dt
    la_ref[...] = dt * (-jnp.exp(alog_ref[...]))
    st_ref[...] = h0_ref[0]

    def terms(c, carry):
        _ssd_chunk_terms(c, xs_ref, b_ref, c_ref, dt_ref, la_ref, dsk_ref, y_ref, upd_ref, dec_ref, need_y)
        return carry

    lax.fori_loop(0, nc, terms, 0, unroll=2)

    def recur(i, carry):
        _ssd_chunk_state(i, 0, c_ref, y_ref, upd_ref, dec_ref, st_ref, need_y)
        _ssd_chunk_state(nc - 1 - i, 1, c_ref, y_ref, upd_ref, dec_ref, st_ref, need_y)
        return carry

    lax.fori_loop(0, nc, recur, 0, unroll=2)
    hT_ref[0] = st_ref[...]
    if need_y:
        z = z_ref[...].astype(F32)
        yz = y_ref[...] * (z * jax.nn.sigmoid(z))
        ms = jnp.mean(yz * yz, axis=-1, keepdims=True)
        o_ref[...] = (yz * lax.rsqrt(ms + EPS) * ng_ref[...]).astype(o_ref.dtype)


def ssd_scan(pr, dt, h0, conv_w, conv_b, dt_bias, a_log, d_skip, norm_g, need_y):
    bn = h0.shape[0]
    seq = pr.shape[0] // bn
    nc = seq // SSD_CHUNK
    pad = LANES - 2 * SSD_HEADS
    dtb = jnp.pad(dt_bias.reshape(1, -1), ((0, 0), (0, pad)))
    alog = jnp.pad(a_log.reshape(1, -1), ((0, 0), (0, pad)))
    dsk = jnp.repeat(d_skip, SSD_HEAD_DIM)[None, :]
    blk = lambda col: pl.BlockSpec((seq, W_SSD), lambda b: (b, col))
    full = lambda a: pl.BlockSpec(a.shape, lambda b: (0,) * a.ndim)
    st_shape = (1, 2, SSD_STATE, W_SSD)
    st_spec = pl.BlockSpec(st_shape, lambda b: (b, 0, 0, 0))
    args = (conv_w, conv_b[None, :], dtb, alog, dsk, norm_g[None, :])
    out_shape = [jax.ShapeDtypeStruct((bn,) + st_shape[1:], F32)]
    out_specs = [st_spec]
    scratch = [pltpu.VMEM((seq, W_SSD), F32),
               pltpu.VMEM((seq, W_SSD), BF16), pltpu.VMEM((seq, W_SSD), BF16),
               pltpu.VMEM((seq, LANES), F32), pltpu.VMEM((seq, LANES), F32),
               pltpu.VMEM((nc, 2, SSD_STATE, W_SSD), F32),
               pltpu.VMEM((nc, 2, SSD_CHUNK + 8, W_SSD), F32),
               pltpu.VMEM(st_shape[1:], F32)]
    if need_y:
        out_shape.insert(0, jax.ShapeDtypeStruct((bn * seq, W_SSD), BF16))
        out_specs.insert(0, pl.BlockSpec((seq, W_SSD), lambda b: (b, 0)))
        scratch.append(pltpu.VMEM((seq, W_SSD), F32))
    res = pl.pallas_call(
        functools.partial(_ssd_kernel, seq=seq, need_y=need_y),
        out_shape=out_shape,
        grid=(bn,),
        in_specs=[blk(Z_COL), blk(XS_COL), blk(BS_COL), blk(CS_COL),
                  pl.BlockSpec((seq, LANES), lambda b: (b, 0))]
                 + [full(a) for a in args] + [st_spec],
        out_specs=out_specs,
        scratch_shapes=scratch,
        compiler_params=_params("parallel"),
        name="ssd_scan",
    )(pr, pr, pr, pr, dt, *args, h0)
    return (res[0], res[1]) if need_y else (None, res[0])


def ssd_mix(pr, dt, pr_c, dt_c, conv_w, conv_b, dt_bias, a_log, d_skip, norm_g, ctx_out, bn):
    h0 = jnp.zeros((bn, 2, SSD_STATE, W_SSD), F32)
    oc, hc = ssd_scan(pr_c, dt_c, h0, conv_w, conv_b, dt_bias, a_log, d_skip, norm_g, ctx_out)
    o, _ = ssd_scan(pr, dt, hc, conv_w, conv_b, dt_bias, a_log, d_skip, norm_g, True)
    return o, oc


def mixer(pr, dt, pr_c, dt_c, pool_w, pool_scale, conv_w, na_bias, s_conv_w, s_conv_b, dt_bias, a_log, d_skip,
          s_norm_g, ctx_out, bn):
    o_ssd, oc_ssd = ssd_mix(pr, dt, pr_c, dt_c, s_conv_w, s_conv_b, dt_bias, a_log, d_skip, s_norm_g, ctx_out, bn)
    o = [*local_mix(pr, pool_w, pool_scale, conv_w, bn), na_attention(pr, pr_c, na_bias, bn), o_ssd]
    if not ctx_out:
        return o, None
    oc = [*local_mix(pr_c, pool_w, pool_scale, conv_w, bn), ctx_attention(pr_c, bn), oc_ssd]
    return o, oc


TM = 512
TM_ROUTE = 1024
TM_PROJ = 1024
TM_GRP = 512


def kernel(x, c, ctx, c_ctx, w_ada, b_ada, g_mix, g_ffn, w_in, w_out, pool_w, pool_scale, conv_w, na_rpb,
           ssd_conv_w, ssd_conv_b, ssd_dt_bias, ssd_a_log, ssd_d, ssd_norm_g, ffn_w_gu, ffn_w_down,
           moe_router, moe_w_gu, moe_w_down, g_final):
    bn, S, d = x.shape
    Lc = ctx.shape[1]
    m, mc = bn * S, bn * Lc
    tpb = S // TM
    x2 = x.reshape(m, d)
    xc2 = ctx.reshape(mc, d)
    c_rows = jnp.concatenate([c, c_ctx[None, :], jnp.zeros((7, d), F32)], axis=0)
    for l in range(DEPTH):
        ctx_out = l < DEPTH - 1
        last = l == DEPTH - 1
        ada = ada_modulation(c_rows, w_ada[l].astype(BF16), b_ada[l][None, :])
        mod = ada[:bn].reshape(bn, 6, d)
        mod_c = ada[bn:bn + 1].reshape(1, 6, d)
        w_in_l = jnp.pad(w_in[l], ((0, 0), (0, D_IN_PAD - D_IN_PROJ))).astype(BF16)
        g_m = g_mix[l][None, :]
        g_f = g_ffn[l][None, :]
        pr, dt = in_proj(x2, g_m, mod, w_in_l, S // TM_PROJ, TM_PROJ)
        pr_c, dt_c = in_proj(xc2, g_m, mod_c, w_in_l, None, min(TM_PROJ, mc))
        na_bias = na_bias_table(na_rpb[l], S // GRID_W)
        o, oc = mixer(pr, dt, pr_c, dt_c, pool_w[l], pool_scale[l], conv_w[l], na_bias, ssd_conv_w[l],
                      ssd_conv_b[l], ssd_dt_bias[l], ssd_a_log[l], ssd_d[l], ssd_norm_g[l], ctx_out, bn)
        w_out_l = w_out[l].astype(BF16)
        j = l // 2
        if l % 2 == 0:
            w_gu = ffn_w_gu[j].astype(BF16)
            w_dn = ffn_w_down[j].astype(BF16)
            x2 = ffn_dense(x2, o, g_f, mod, w_out_l, w_gu, w_dn, tpb, TM)
            if ctx_out:
                xc2 = ffn_dense(xc2, oc, g_f, mod_c, w_out_l, w_gu, w_dn, None, min(TM, mc))
        else:
            w_r = jnp.pad(moe_router[j], ((0, 0), (0, LANES - N_EXPERTS))).astype(BF16)
            w_gu, w_dn = moe_w_gu[j], moe_w_down[j]
            x2 = moe_block(x2, o, g_f, mod, w_out_l, w_r, w_gu, w_dn, S // TM_ROUTE, TM_ROUTE, TM_GRP,
                           g_final[None, :] if last else None)
            if ctx_out:
                xc2 = moe_block(xc2, oc, g_f, mod_c, w_out_l, w_r, w_gu, w_dn, None, min(TM_ROUTE, mc), TM_GRP)
            if last:
                return x2.reshape(bn, S, d)
    return final_norm(x2, g_final[None, :], TM).reshape(bn, S, d)
```

```python
import functools
import math

import numpy as np
import jax
import jax.numpy as jnp
from jax import lax
from jax.experimental import pallas as pl
from jax.experimental.pallas import tpu as pltpu

DEPTH = 2
GRID_W = 64
EPS = 1e-6
W_POOL = 256
W_CONV = 256
W_NA = 256
W_SSD = 256
POOL_WINDOWS = (2, 4, 8, 16)
POOL_GROUP = W_POOL // len(POOL_WINDOWS)
NA_HEADS = 4
NA_HEAD_DIM = W_NA // NA_HEADS
WIN_R = 8
WIN_C = 16
SSD_HEAD_DIM = 64
SSD_HEADS = W_SSD // SSD_HEAD_DIM
SSD_GROUPS = 2
SSD_STATE = 128
SSD_CHUNK = 128
SSD_XBC = W_SSD + 2 * SSD_GROUPS * SSD_STATE
D_IN_PROJ = W_POOL + 3 * W_CONV + 3 * W_NA + W_SSD + SSD_XBC + 2 * SSD_HEADS
N_EXPERTS = 8
TOP_K = 2

LANES = 128
D_IN_PAD = -(-D_IN_PROJ // LANES) * LANES
VMEM_LIMIT = 56 * 1024 * 1024
BF16 = jnp.bfloat16
F32 = jnp.float32


def _params(*sem):
    return pltpu.CompilerParams(dimension_semantics=sem, vmem_limit_bytes=VMEM_LIMIT)


def _norm_mod(x, g, scale, shift):
    ms = jnp.mean(x * x, axis=-1, keepdims=True)
    return (x * lax.rsqrt(ms + EPS) * g) * (1.0 + scale) + shift


def _mod_map(tiles_per_batch):
    if tiles_per_batch is None:
        return lambda i, *_: (0, 0, 0)
    return lambda i, *_: (i // tiles_per_batch, 0, 0)


def _resident(shape, index_map):
    return pl.BlockSpec(shape, index_map, pipeline_mode=pl.Buffered(1))


def _ada_kernel(c_ref, w_ref, b_ref, o_ref):
    c = c_ref[...]
    s = c * jax.nn.sigmoid(c)
    o_ref[...] = jnp.dot(s.astype(BF16), w_ref[...], preferred_element_type=F32) + b_ref[...]


def ada_modulation(c_rows, w, b):
    r, d = c_rows.shape
    n = w.shape[1]
    tn = 1536
    return pl.pallas_call(
        _ada_kernel,
        out_shape=jax.ShapeDtypeStruct((r, n), F32),
        grid=(n // tn,),
        in_specs=[pl.BlockSpec((r, d), lambda j: (0, 0)),
                  pl.BlockSpec((d, tn), lambda j: (0, j)),
                  pl.BlockSpec((1, tn), lambda j: (0, j))],
        out_specs=pl.BlockSpec((r, tn), lambda j: (0, j)),
        compiler_params=_params("arbitrary"),
        name="ada_modulation",
    )(c_rows, w, b)


def _in_proj_kernel(x_ref, g_ref, mod_ref, w_ref, o_ref, dt_ref):
    h = _norm_mod(x_ref[...], g_ref[...], mod_ref[0, 1:2, :], mod_ref[0, 0:1, :])
    pr = jnp.dot(h.astype(BF16), w_ref[...], preferred_element_type=F32)
    o_ref[...] = pr.astype(o_ref.dtype)
    dt_ref[...] = pr[:, DT_COL * LANES:(DT_COL + 1) * LANES]


def in_proj(x2, g, mod, w, tiles_per_batch, tm):
    m, d = x2.shape
    n = w.shape[1]
    return pl.pallas_call(
        _in_proj_kernel,
        out_shape=(jax.ShapeDtypeStruct((m, n), BF16), jax.ShapeDtypeStruct((m, LANES), F32)),
        grid=(m // tm,),
        in_specs=[pl.BlockSpec((tm, d), lambda i: (i, 0)),
                  pl.BlockSpec((1, d), lambda i: (0, 0)),
                  pl.BlockSpec((1, 6, d), _mod_map(tiles_per_batch)),
                  _resident((d, n), lambda i: (0, 0))],
        out_specs=(pl.BlockSpec((tm, n), lambda i: (i, 0)), pl.BlockSpec((tm, LANES), lambda i: (i, 0))),
        compiler_params=_params("parallel"),
        name="in_proj",
    )(x2, g, mod, w)


def _mix_residual(x, mod_ref, w_ref, part_refs):
    y, k0 = None, 0
    for p_ref in part_refs:
        k = p_ref.shape[1]
        t = jnp.dot(p_ref[...], w_ref[k0:k0 + k, :], preferred_element_type=F32)
        y = t if y is None else y + t
        k0 += k
    return x + mod_ref[0, 2:3, :] * y


def _part_specs(parts, tm):
    return [pl.BlockSpec((tm, p.shape[1]), lambda i, *_: (i, 0)) for p in parts]


FF_CHUNK = 512


def _swiglu(h, wgu, wd, ff):
    acc = None
    for c0 in range(0, ff, FF_CHUNK):
        c1 = min(c0 + FF_CHUNK, ff)
        gg = jnp.dot(h, wgu(c0, c1), preferred_element_type=F32)
        uu = jnp.dot(h, wgu(ff + c0, ff + c1), preferred_element_type=F32)
        a = (gg * jax.nn.sigmoid(gg) * uu).astype(BF16)
        part = jnp.dot(a, wd(c0, c1), preferred_element_type=F32)
        acc = part if acc is None else acc + part
    return acc


def _ffn_kernel(x_ref, g_ref, mod_ref, wout_ref, wgu_ref, wd_ref, *refs):
    x = _mix_residual(x_ref[...], mod_ref, wout_ref, refs[:-1])
    y_ref = refs[-1]
    h = _norm_mod(x, g_ref[...], mod_ref[0, 4:5, :], mod_ref[0, 3:4, :]).astype(BF16)
    y = _swiglu(h, lambda a, b: wgu_ref[:, a:b], lambda a, b: wd_ref[a:b, :], wd_ref.shape[0])
    y_ref[...] = x + mod_ref[0, 5:6, :] * y


def ffn_dense(x2, parts, g, mod, w_out, w_gu, w_down, tiles_per_batch, tm):
    m, d = x2.shape
    return pl.pallas_call(
        _ffn_kernel,
        out_shape=jax.ShapeDtypeStruct((m, d), F32),
        grid=(m // tm,),
        in_specs=[pl.BlockSpec((tm, d), lambda i: (i, 0)),
                  pl.BlockSpec((1, d), lambda i: (0, 0)),
                  pl.BlockSpec((1, 6, d), _mod_map(tiles_per_batch)),
                  _resident(w_out.shape, lambda i: (0, 0)),
                  _resident(w_gu.shape, lambda i: (0, 0)),
                  _resident(w_down.shape, lambda i: (0, 0))] + _part_specs(parts, tm),
        out_specs=pl.BlockSpec((tm, d), lambda i: (i, 0)),
        compiler_params=_params("parallel"),
        name="ffn_dense",
    )(x2, g, mod, w_out, w_gu, w_down, *parts)


ROUTE_E1, ROUTE_E2, ROUTE_R1, ROUTE_R2, ROUTE_G1, ROUTE_G2 = range(6)
ROUTE_ROWS = 8


def _router_kernel(x_ref, g_ref, mod_ref, wout_ref, wr_ref, *refs):
    part_refs = refs[:-6]
    x1_ref, h_ref, route_ref, route_t_ref, cnt_ref, base_ref = refs[-6:]

    @pl.when(pl.program_id(0) == 0)
    def _():
        base_ref[...] = jnp.zeros_like(base_ref)

    x = _mix_residual(x_ref[...], mod_ref, wout_ref, part_refs)
    x1_ref[...] = x
    h = _norm_mod(x, g_ref[...], mod_ref[0, 4:5, :], mod_ref[0, 3:4, :]).astype(BF16)
    h_ref[...] = h
    tm = h.shape[0]
    logits = jnp.dot(h, wr_ref[...], preferred_element_type=F32)
    lane = lax.broadcasted_iota(jnp.int32, logits.shape, 1)
    neg = jnp.float32(-jnp.inf)
    logits = jnp.where(lane < N_EXPERTS, logits, neg)
    m1 = jnp.max(logits, axis=-1, keepdims=True)
    i1 = jnp.min(jnp.where(logits == m1, lane, LANES), axis=-1, keepdims=True)
    rest = jnp.where(lane == i1, neg, logits)
    m2 = jnp.max(rest, axis=-1, keepdims=True)
    i2 = jnp.min(jnp.where(rest == m2, lane, LANES), axis=-1, keepdims=True)
    e2 = jnp.exp(m2 - m1)
    g1 = 1.0 / (1.0 + e2)
    g2 = e2 / (1.0 + e2)
    sel1, sel2 = lane == i1, lane == i2
    sel = jnp.where(sel1 | sel2, 1.0, 0.0)
    li = lax.broadcasted_iota(jnp.int32, (tm, tm), 0)
    si = lax.broadcasted_iota(jnp.int32, (tm, tm), 1)
    before = jnp.where(si < li, 1.0, 0.0).astype(BF16)
    rank = jnp.dot(before, sel.astype(BF16), preferred_element_type=F32) + base_ref[...]
    r1 = jnp.sum(jnp.where(sel1, rank, 0.0), axis=-1, keepdims=True)
    r2 = jnp.sum(jnp.where(sel2, rank, 0.0), axis=-1, keepdims=True)
    base_ref[...] += jnp.sum(sel, axis=0, keepdims=True)
    cnt_ref[...] = base_ref[...]
    fields = (i1.astype(F32), i2.astype(F32), r1, r2, g1, g2)
    route = jnp.zeros(logits.shape, F32)
    for k, v in enumerate(fields):
        route = jnp.where(lane == k, v, route)
    route_ref[...] = route
    route_t_ref[...] = route.T[:ROUTE_ROWS, :]


def moe_router(x2, parts, g, mod, w_out, w_router, tiles_per_batch, tm):
    m, d = x2.shape
    return pl.pallas_call(
        _router_kernel,
        out_shape=(jax.ShapeDtypeStruct((m, d), F32), jax.ShapeDtypeStruct((m, d), BF16),
                   jax.ShapeDtypeStruct((m, LANES), F32), jax.ShapeDtypeStruct((ROUTE_ROWS, m), F32),
                   jax.ShapeDtypeStruct((1, LANES), F32)),
        grid=(m // tm,),
        in_specs=[pl.BlockSpec((tm, d), lambda i: (i, 0)),
                  pl.BlockSpec((1, d), lambda i: (0, 0)),
                  pl.BlockSpec((1, 6, d), _mod_map(tiles_per_batch)),
                  _resident(w_out.shape, lambda i: (0, 0)),
                  pl.BlockSpec((d, LANES), lambda i: (0, 0))] + _part_specs(parts, tm),
        out_specs=(pl.BlockSpec((tm, d), lambda i: (i, 0)),
                   pl.BlockSpec((tm, d), lambda i: (i, 0)),
                   pl.BlockSpec((tm, LANES), lambda i: (i, 0)),
                   pl.BlockSpec((ROUTE_ROWS, tm), lambda i: (0, i)),
                   pl.BlockSpec((1, LANES), lambda i: (0, 0))),
        scratch_shapes=[pltpu.VMEM((1, LANES), F32)],
        compiler_params=_params("arbitrary"),
        name="moe_router",
    )(x2, g, mod, w_out, w_router, *parts)


def _moe_kernel(tile_ref, exp_ref, start_ref, end_ref, xg_ref, wgu_ref, wd_ref, y_ref, wgu_b, wd_b):
    w = pl.program_id(0)
    tm = xg_ref.shape[0]
    start, end = start_ref[w], end_ref[w]
    t0 = tile_ref[w] * tm

    @pl.when((w == 0) | (exp_ref[w] != exp_ref[jnp.maximum(w - 1, 0)]))
    def _():
        wgu_b[...] = wgu_ref[0].astype(BF16)
        wd_b[...] = wd_ref[0].astype(BF16)

    @pl.when(end > start)
    def _():
        y = _swiglu(xg_ref[...], lambda a, b: wgu_b[:, a:b], lambda a, b: wd_b[a:b, :], wd_b.shape[0])
        y = y.astype(y_ref.dtype)

        @pl.when(start == t0)
        def _():
            y_ref[...] = y

        @pl.when(start != t0)
        def _():
            row = t0 + lax.broadcasted_iota(jnp.int32, (tm, 1), 0)
            y_ref[...] = jnp.where(row >= start, y, y_ref[...])


def moe_grouped(item_tile, item_expert, item_start, item_end, xg, w_gu, w_down, tm):
    p, d = xg.shape
    grid_spec = pltpu.PrefetchScalarGridSpec(
        num_scalar_prefetch=4,
        grid=(item_tile.shape[0],),
        in_specs=[pl.BlockSpec((tm, d), lambda w, it, ie, s, e: (it[w], 0)),
                  _resident((1,) + w_gu.shape[1:], lambda w, it, ie, s, e: (ie[w], 0, 0)),
                  _resident((1,) + w_down.shape[1:], lambda w, it, ie, s, e: (ie[w], 0, 0))],
        out_specs=pl.BlockSpec((tm, d), lambda w, it, ie, s, e: (it[w], 0)),
        scratch_shapes=[pltpu.VMEM(w_gu.shape[1:], BF16), pltpu.VMEM(w_down.shape[1:], BF16)],
    )
    return pl.pallas_call(
        _moe_kernel,
        out_shape=jax.ShapeDtypeStruct((p, d), BF16),
        grid_spec=grid_spec,
        compiler_params=_params("arbitrary"),
        name="moe_grouped",
    )(item_tile, item_expert, item_start, item_end, xg, w_gu, w_down)


def _moe_combine_kernel(x_ref, ya_ref, yb_ref, route_ref, mod_ref, *refs):
    o_ref = refs[-1]
    r = route_ref[...]
    y = (r[:, ROUTE_G1:ROUTE_G1 + 1] * ya_ref[...].astype(F32) + r[:, ROUTE_G2:ROUTE_G2 + 1] * yb_ref[...].astype(F32))
    x = x_ref[...] + mod_ref[0, 5:6, :] * y
    if len(refs) == 2:
        ms = jnp.mean(x * x, axis=-1, keepdims=True)
        x = x * lax.rsqrt(ms + EPS) * refs[0][...]
    o_ref[...] = x


def moe_combine(x2, ya, yb, route, mod, tiles_per_batch, tm, g_final=None):
    m, d = x2.shape
    row = pl.BlockSpec((tm, d), lambda i: (i, 0))
    specs = [row, row, row, pl.BlockSpec((tm, LANES), lambda i: (i, 0)),
             pl.BlockSpec((1, 6, d), _mod_map(tiles_per_batch))]
    args = [x2, ya, yb, route, mod]
    if g_final is not None:
        specs.append(pl.BlockSpec((1, d), lambda i: (0, 0)))
        args.append(g_final)
    return pl.pallas_call(
        _moe_combine_kernel,
        out_shape=jax.ShapeDtypeStruct((m, d), F32),
        grid=(m // tm,),
        in_specs=specs,
        out_specs=row,
        compiler_params=_params("parallel"),
        name="moe_combine",
    )(*args)


def moe_block(x2, parts, g, mod, w_out, w_router, w_gu, w_down, tiles_per_batch, tm_tok, tm_grp, g_final=None):
    m, d = x2.shape
    x2, h, route, route_t, cnt = moe_router(x2, parts, g, mod, w_out, w_router, tiles_per_batch, tm_tok)
    cnt = cnt[0, :N_EXPERTS].astype(jnp.int32)
    p = m * TOP_K
    off = jnp.cumsum(cnt) - cnt
    field = lambda k: route_t[k].astype(jnp.int32)

    def row_of(e, r):
        for k in range(N_EXPERTS):
            r = r + jnp.where(e == k, off[k], 0)
        return r

    d1 = row_of(field(ROUTE_E1), field(ROUTE_R1))
    d2 = row_of(field(ROUTE_E2), field(ROUTE_R2))
    tok = jnp.arange(m, dtype=jnp.int32)
    _, src = lax.sort_key_val(jnp.concatenate([d1, d2]), jnp.concatenate([tok, tok]))
    n_row_tiles = p // tm_grp
    start = jnp.sort(jnp.concatenate([jnp.arange(n_row_tiles, dtype=jnp.int32) * tm_grp, off]))
    end = jnp.concatenate([start[1:], jnp.full((1,), p, jnp.int32)])
    item_tile = jnp.minimum(start // tm_grp, n_row_tiles - 1)
    item_expert = jnp.sum((off[None, :] <= start[:, None]).astype(jnp.int32), axis=1) - 1
    rows = lambda a, idx: a.at[idx].get(mode="promise_in_bounds")
    yg = moe_grouped(item_tile, item_expert, start, end, rows(h, src), w_gu, w_down, tm_grp)
    return moe_combine(x2, rows(yg, d1), rows(yg, d2), route, mod, tiles_per_batch, tm_tok, g_final)


def _final_norm_kernel(x_ref, g_ref, o_ref):
    x = x_ref[...]
    ms = jnp.mean(x * x, axis=-1, keepdims=True)
    o_ref[...] = x * lax.rsqrt(ms + EPS) * g_ref[...]


def final_norm(x2, g, tm):
    m, d = x2.shape
    return pl.pallas_call(
        _final_norm_kernel,
        out_shape=jax.ShapeDtypeStruct((m, d), F32),
        grid=(m // tm,),
        in_specs=[pl.BlockSpec((tm, d), lambda i: (i, 0)), pl.BlockSpec((1, d), lambda i: (0, 0))],
        out_specs=pl.BlockSpec((tm, d), lambda i: (i, 0)),
        compiler_params=_params("parallel"),
        name="final_norm",
    )(x2, g)


NA_QROWS = 4
NA_KROWS = NA_QROWS + WIN_R
Q_COL, K_COL, V_COL = 4, 5, 6


def _na_key_start(j, rows):
    return np.clip(j * NA_QROWS - WIN_R // 2, 0, rows - NA_KROWS)


def _na_bias_index(rows):
    nblk = rows // NA_QROWS
    pats = []
    for j in range(nblk):
        start = _na_key_start(j, rows)
        r = j * NA_QROWS + np.arange(NA_QROWS)
        sr = np.clip(r - WIN_R // 2, 0, rows - WIN_R)
        kr = start + np.arange(NA_KROWS)
        rvalid = (kr[None, :] >= sr[:, None]) & (kr[None, :] < sr[:, None] + WIN_R)
        ri = np.clip(kr[None, :] - r[:, None] + WIN_R - 1, 0, 2 * WIN_R - 2)
        pats.append((ri, rvalid))
    for j in range(2, nblk - 1):
        assert all(np.array_equal(a, b) for a, b in zip(pats[1], pats[j]))
    sel = [pats[0], pats[1], pats[nblk - 1]]
    ri = np.stack([p[0] for p in sel])
    rvalid = np.stack([p[1] for p in sel])
    c = np.arange(GRID_W)
    sc = np.clip(c - WIN_C // 2, 0, GRID_W - WIN_C)
    cvalid = (c[None, :] >= sc[:, None]) & (c[None, :] < sc[:, None] + WIN_C)
    ci = np.clip(c[None, :] - c[:, None] + WIN_C - 1, 0, 2 * WIN_C - 2)
    c_onehot = (ci[..., None] == np.arange(2 * WIN_C - 1)).astype(np.float32)
    valid = rvalid[:, :, None, :, None] & cvalid[None, None, :, None, :]
    return ri, c_onehot, valid


def na_bias_table(rpb, rows):
    ri, c_onehot, valid = _na_bias_index(rows)
    toep = jnp.einsum('hrd,qkd->hrqk', rpb, c_onehot, precision=lax.Precision.HIGHEST)
    b = jnp.take(toep, ri.reshape(-1), axis=1).reshape((NA_HEADS,) + ri.shape + (GRID_W, GRID_W))
    b = jnp.transpose(b, (1, 0, 2, 4, 3, 5))
    b = jnp.where(valid[:, None], b, -jnp.inf)
    return b.reshape(3, NA_HEADS, NA_QROWS * GRID_W, NA_KROWS * GRID_W).astype(F32)


def _attend_heads(q, key_sets, bias_fn):
    n, w = q.shape
    lane = lax.broadcasted_iota(jnp.int32, (1, w), 1)
    head_masks = [(lane >= h * NA_HEAD_DIM) & (lane < (h + 1) * NA_HEAD_DIM) for h in range(NA_HEADS)]
    all_scores = []
    for h, mh in enumerate(head_masks):
        qh = jnp.where(mh, q, 0.0).astype(BF16)
        scores = []
        for i, (k, _) in enumerate(key_sets):
            s = lax.dot_general(qh, k, (((1,), (1,)), ((), ())), preferred_element_type=F32)
            b = bias_fn(i, h)
            scores.append(s if b is None else s + b)
        all_scores.append(scores)
    all_probs = []
    for scores in all_scores:
        m = scores[0].max(axis=-1, keepdims=True)
        for s in scores[1:]:
            m = jnp.maximum(m, s.max(axis=-1, keepdims=True))
        denom = jnp.zeros((n, 1), F32)
        probs = []
        for s in scores:
            p = jnp.exp(s - m)
            denom = denom + p.sum(axis=-1, keepdims=True)
            probs.append(p.astype(BF16))
        all_probs.append((probs, denom))
    out = jnp.zeros((n, w), F32)
    for mh, (probs, denom) in zip(head_masks, all_probs):
        acc = jnp.zeros((n, w), F32)
        for p, (_, v) in zip(probs, key_sets):
            acc = acc + jnp.dot(p, v, preferred_element_type=F32)
        out = out + jnp.where(mh, acc / denom, 0.0)
    return out


def _na_kernel(q_ref, k_ref, v_ref, kc_ref, vc_ref, bias_ref, o_ref, *, rows):
    j = pl.program_id(1)
    start = jnp.clip(j * NA_QROWS - WIN_R // 2, 0, rows - NA_KROWS)
    t0 = pl.multiple_of(start * GRID_W, GRID_W)
    nk = NA_KROWS * GRID_W
    kw = k_ref[pl.ds(t0, nk), :]
    vw = v_ref[pl.ds(t0, nk), :]
    q = q_ref[...] * (1.0 / math.sqrt(NA_HEAD_DIM))
    o = _attend_heads(q, [(kw, vw), (kc_ref[...], vc_ref[...])], lambda i, h: bias_ref[0, h] if i == 0 else None)
    o_ref[...] = o.astype(o_ref.dtype)


def na_attention(pr, pr_c, bias, bn):
    S, Lc = pr.shape[0] // bn, pr_c.shape[0] // bn
    rows = S // GRID_W
    nblk = rows // NA_QROWS
    nq = NA_QROWS * GRID_W

    def pat(b, j):
        return (jnp.where(j == 0, 0, jnp.where(j == nblk - 1, 2, 1)), 0, 0, 0)

    return pl.pallas_call(
        functools.partial(_na_kernel, rows=rows),
        out_shape=jax.ShapeDtypeStruct((bn * S, W_NA), BF16),
        grid=(bn, nblk),
        in_specs=[pl.BlockSpec((nq, W_NA), lambda b, j: (b * nblk + j, Q_COL)),
                  pl.BlockSpec((S, W_NA), lambda b, j: (b, K_COL)),
                  pl.BlockSpec((S, W_NA), lambda b, j: (b, V_COL)),
                  pl.BlockSpec((Lc, W_NA), lambda b, j: (b, K_COL)),
                  pl.BlockSpec((Lc, W_NA), lambda b, j: (b, V_COL)),
                  pl.BlockSpec((1,) + bias.shape[1:], pat)],
        out_specs=pl.BlockSpec((nq, W_NA), lambda b, j: (b * nblk + j, 0)),
        compiler_params=_params("parallel", "arbitrary"),
        name="na_attention",
    )(pr, pr, pr, pr_c, pr_c, bias)


def _ctx_attn_kernel(q_ref, k_ref, v_ref, o_ref):
    q = q_ref[...] * (1.0 / math.sqrt(NA_HEAD_DIM))
    o = _attend_heads(q, [(k_ref[...], v_ref[...])], lambda i, h: None)
    o_ref[...] = o.astype(o_ref.dtype)


def ctx_attention(pr_c, bn):
    Lc = pr_c.shape[0] // bn
    return pl.pallas_call(
        _ctx_attn_kernel,
        out_shape=jax.ShapeDtypeStruct((bn * Lc, W_NA), BF16),
        grid=(bn,),
        in_specs=[pl.BlockSpec((Lc, W_NA), lambda b: (b, Q_COL)),
                  pl.BlockSpec((Lc, W_NA), lambda b: (b, K_COL)),
                  pl.BlockSpec((Lc, W_NA), lambda b: (b, V_COL))],
        out_specs=pl.BlockSpec((Lc, W_NA), lambda b: (b, 0)),
        compiler_params=_params("parallel"),
        name="ctx_attention",
    )(pr_c, pr_c, pr_c)


POOL_COL, CONV_H_COL, CONV_B_COL, CONV_C_COL = 0, 1, 2, 3


def _shift_rows(x, k, row):
    n = x.shape[0]
    y = pltpu.roll(x, k % n, 0)
    return jnp.where(row < k, 0.0, y) if k > 0 else jnp.where(row >= n + k, 0.0, y)


def _local_mix_kernel(u_ref, h_ref, bg_ref, cg_ref, pw_ref, ps_ref, cw_ref, op_ref, oc_ref, *, seq):
    row = lax.broadcasted_iota(jnp.int32, (seq, 1), 0)
    lane = lax.broadcasted_iota(jnp.int32, (1, W_POOL), 1)
    u = u_ref[...].astype(F32)
    trailing, leading = {1: u}, {1: u}
    for w in (1, 2, 4):
        trailing[2 * w] = trailing[w] + _shift_rows(trailing[w], w, row)
        leading[2 * w] = leading[w] + _shift_rows(leading[w], -w, row)
    rowf = row.astype(F32)
    win_sum = jnp.zeros_like(u)
    cnt = jnp.zeros_like(u)
    for g, win in enumerate(POOL_WINDOWS):
        half = win // 2
        mg = (lane >= g * POOL_GROUP) & (lane < (g + 1) * POOL_GROUP)
        s = _shift_rows(trailing[half], 1, row) + leading[half]
        n = jnp.minimum(rowf + half, float(seq)) - jnp.maximum(rowf - half, 0.0)
        win_sum = jnp.where(mg, s, win_sum)
        cnt = jnp.where(mg, n, cnt)
    p = win_sum / cnt - u
    pooled = jnp.dot(p.astype(BF16), pw_ref[...], preferred_element_type=F32) * ps_ref[...]
    op_ref[...] = pooled.astype(op_ref.dtype)
    v = cg_ref[...].astype(F32) * h_ref[...].astype(F32)
    conv = (_shift_rows(v, 1, row) * cw_ref[0:1, :] + v * cw_ref[1:2, :] + _shift_rows(v, -1, row) * cw_ref[2:3, :])
    oc_ref[...] = (bg_ref[...].astype(F32) * conv).astype(oc_ref.dtype)


def local_mix(pr, pool_w, pool_scale, conv_w, bn):
    seq = pr.shape[0] // bn
    pw = jax.scipy.linalg.block_diag(*[pool_w[g] for g in range(len(POOL_WINDOWS))]).astype(BF16)
    blk = lambda col: pl.BlockSpec((seq, W_POOL), lambda b: (b, col))
    full = lambda a: pl.BlockSpec(a.shape, lambda b: (0,) * a.ndim)
    args = (pw, pool_scale[None, :], conv_w)
    return pl.pallas_call(
        functools.partial(_local_mix_kernel, seq=seq),
        out_shape=(jax.ShapeDtypeStruct((bn * seq, W_POOL), BF16), jax.ShapeDtypeStruct((bn * seq, W_CONV), BF16)),
        grid=(bn,),
        in_specs=[blk(POOL_COL), blk(CONV_H_COL), blk(CONV_B_COL), blk(CONV_C_COL)] + [full(a) for a in args],
        out_specs=(pl.BlockSpec((seq, W_POOL), lambda b: (b, 0)),
                   pl.BlockSpec((seq, W_CONV), lambda b: (b, 0))),
        compiler_params=_params("parallel"),
        name="local_mix",
    )(pr, pr, pr, pr, *args)


Z_COL, XS_COL, BS_COL, CS_COL = 7, 8, 9, 10
DT_COL = 22
HEADS_PER_GROUP = SSD_HEADS // SSD_GROUPS
GROUP_W = HEADS_PER_GROUP * SSD_HEAD_DIM


def _head_mask(h):
    lane = lax.broadcasted_iota(jnp.int32, (1, W_SSD), 1)
    return (lane >= h * SSD_HEAD_DIM) & (lane < (h + 1) * SSD_HEAD_DIM)


def _expand_heads(v, d):
    out = jnp.zeros((v.shape[0], W_SSD), F32)
    for h in range(SSD_HEADS):
        k = d * SSD_HEADS + h
        out = jnp.where(_head_mask(h), v[:, k:k + 1], out)
    return out


def _ssd_chunk_terms(c, xs_ref, b_ref, c_ref, dt_ref, la_ref, dsk_ref, y_ref, upd_ref, dec_ref, need_y):
    T = SSD_CHUNK
    r0 = pl.multiple_of(c * T, T)
    xs = xs_ref[pl.ds(r0, T), :]
    bm = b_ref[pl.ds(r0, T), :]
    cm = c_ref[pl.ds(r0, T), :]
    dt = dt_ref[pl.ds(r0, T), :]
    la = la_ref[pl.ds(r0, T), :]
    li = lax.broadcasted_iota(jnp.int32, (T, T), 0)
    si = lax.broadcasted_iota(jnp.int32, (T, T), 1)
    causal = si <= li
    prefix = jnp.dot(causal.astype(F32), la, precision=lax.Precision.HIGHEST, preferred_element_type=F32)
    total = prefix[T - 1:T, :]
    lane = lax.broadcasted_iota(jnp.int32, (1, LANES), 1)
    acum = jnp.where(lane < SSD_HEADS, prefix, total - prefix + la)
    bt = bm.astype(F32).T.astype(BF16)
    if need_y:
        acum_t = acum.T
        cb = [lax.dot_general(cm[:, g * SSD_STATE:(g + 1) * SSD_STATE], bm[:, g * SSD_STATE:(g + 1) * SSD_STATE],
                              (((1,), (1,)), ((), ())), preferred_element_type=F32) for g in range(SSD_GROUPS)]
        y = jnp.zeros((T, W_SSD), F32)
    per_dir = []
    for d in range(2):
        acum_e = _expand_heads(acum, d)
        tot_e = _expand_heads(total, d)
        xdt = xs * _expand_heads(dt, d)
        xw = (xdt * jnp.exp(tot_e - acum_e)).astype(BF16)
        dec_ref[c, d, 0:T, :] = jnp.exp(acum_e)
        dec_ref[c, d, T:T + 1, :] = jnp.exp(tot_e)
        scores = []
        if need_y:
            mask = causal if d == 0 else (si >= li)
            for h in range(SSD_HEADS):
                k = d * SSD_HEADS + h
                decay = jnp.exp(jnp.where(mask, acum[:, k:k + 1] - acum_t[k:k + 1, :], -jnp.inf))
                scores.append((cb[h // HEADS_PER_GROUP] * decay).astype(BF16))
        per_dir.append((xw, xdt, scores))
    for d, (xw, xdt, scores) in enumerate(per_dir):
        upd = [jnp.dot(bt[g * SSD_STATE:(g + 1) * SSD_STATE, :], xw[:, g * GROUP_W:(g + 1) * GROUP_W],
                       preferred_element_type=F32) for g in range(SSD_GROUPS)]
        upd_ref[c, d] = jnp.concatenate(upd, axis=1)
        if need_y:
            x_heads = [jnp.where(_head_mask(h), xdt, 0.0).astype(BF16) for h in range(SSD_HEADS)]
            y = y + jnp.dot(jnp.concatenate(scores, axis=1), jnp.concatenate(x_heads, axis=0),
                            preferred_element_type=F32)
    if need_y:
        y_ref[pl.ds(r0, T), :] = xs * dsk_ref[...] + y


def _ssd_chunk_state(c, d, c_ref, y_ref, upd_ref, dec_ref, st_ref, need_y):
    T = SSD_CHUNK
    r0 = pl.multiple_of(c * T, T)
    st = st_ref[d]
    if need_y:
        cm = c_ref[pl.ds(r0, T), :]
        st_b = st.astype(BF16)
        ys = [jnp.dot(cm[:, g * SSD_STATE:(g + 1) * SSD_STATE], st_b[:, g * GROUP_W:(g + 1) * GROUP_W],
                      preferred_element_type=F32) for g in range(SSD_GROUPS)]
        y_ref[pl.ds(r0, T), :] += jnp.concatenate(ys, axis=1) * dec_ref[c, d, 0:T, :]
    st_ref[d] = dec_ref[c, d, T:T + 1, :] * st + upd_ref[c, d]


def _ssd_kernel(z_ref, xs_in, bs_in, cs_in, dtr_ref, cw_ref, cb_ref, dtb_ref, alog_ref, dsk_ref, ng_ref, h0_ref,
                *refs, seq, need_y):
    if need_y:
        o_ref, hT_ref, xs_ref, b_ref, c_ref, dt_ref, la_ref, upd_ref, dec_ref, st_ref, y_ref = refs
    else:
        hT_ref, xs_ref, b_ref, c_ref, dt_ref, la_ref, upd_ref, dec_ref, st_ref = refs
        y_ref = None
    nc = seq // SSD_CHUNK
    row = lax.broadcasted_iota(jnp.int32, (seq, 1), 0)
    for gi, (src, dst) in enumerate(((xs_in, xs_ref), (bs_in, b_ref), (cs_in, c_ref))):
        sl = slice(gi * W_SSD, (gi + 1) * W_SSD)
        x = src[...].astype(F32)
        cv = (_shift_rows(x, 1, row) * cw_ref[0:1, sl] + x * cw_ref[1:2, sl]
              + _shift_rows(x, -1, row) * cw_ref[2:3, sl] + cb_ref[:, sl])
        dst[...] = (cv * jax.nn.sigmoid(cv)).astype(dst.dtype)
    dtv = dtr_ref[...] + dtb_ref[...]
    dt = jnp.maximum(dtv, 0.0) + jnp.log1p(jnp.exp(-jnp.abs(dtv)))
    dt_ref[...] = dt
    la_ref[...] = dt * (-jnp.exp(alog_ref[...]))
    st_ref[...] = h0_ref[0]

    def terms(c, carry):
        _ssd_chunk_terms(c, xs_ref, b_ref, c_ref, dt_ref, la_ref, dsk_ref, y_ref, upd_ref, dec_ref, need_y)
        return carry

    lax.fori_loop(0, nc, terms, 0, unroll=2)

    def recur(i, carry):
        _ssd_chunk_state(i, 0, c_ref, y_ref, upd_ref, dec_ref, st_ref, need_y)
        _ssd_chunk_state(nc - 1 - i, 1, c_ref, y_ref, upd_ref, dec_ref, st_ref, need_y)
        return carry

    lax.fori_loop(0, nc, recur, 0, unroll=2)
    hT_ref[0] = st_ref[...]
    if need_y:
        z = z_ref[...].astype(F32)
        yz = y_ref[...] * (z * jax.nn.sigmoid(z))
        ms = jnp.mean(yz * yz, axis=-1, keepdims=True)
        o_ref[...] = (yz * lax.rsqrt(ms + EPS) * ng_ref[...]).astype(o_ref.dtype)


def ssd_scan(pr, dt, h0, conv_w, conv_b, dt_bias, a_log, d_skip, norm_g, need_y):
    bn = h0.shape[0]
    seq = pr.shape[0] // bn
    nc = seq // SSD_CHUNK
    pad = LANES - 2 * SSD_HEADS
    dtb = jnp.pad(dt_bias.reshape(1, -1), ((0, 0), (0, pad)))
    alog = jnp.pad(a_log.reshape(1, -1), ((0, 0), (0, pad)))
    dsk = jnp.repeat(d_skip, SSD_HEAD_DIM)[None, :]
    blk = lambda col: pl.BlockSpec((seq, W_SSD), lambda b: (b, col))
    full = lambda a: pl.BlockSpec(a.shape, lambda b: (0,) * a.ndim)
    st_shape = (1, 2, SSD_STATE, W_SSD)
    st_spec = pl.BlockSpec(st_shape, lambda b: (b, 0, 0, 0))
    args = (conv_w, conv_b[None, :], dtb, alog, dsk, norm_g[None, :])
    out_shape = [jax.ShapeDtypeStruct((bn,) + st_shape[1:], F32)]
    out_specs = [st_spec]
    scratch = [pltpu.VMEM((seq, W_SSD), F32),
               pltpu.VMEM((seq, W_SSD), BF16), pltpu.VMEM((seq, W_SSD), BF16),
               pltpu.VMEM((seq, LANES), F32), pltpu.VMEM((seq, LANES), F32),
               pltpu.VMEM((nc, 2, SSD_STATE, W_SSD), F32),
               pltpu.VMEM((nc, 2, SSD_CHUNK + 8, W_SSD), F32),
               pltpu.VMEM(st_shape[1:], F32)]
    if need_y:
        out_shape.insert(0, jax.ShapeDtypeStruct((bn * seq, W_SSD), BF16))
        out_specs.insert(0, pl.BlockSpec((seq, W_SSD), lambda b: (b, 0)))
        scratch.append(pltpu.VMEM((seq, W_SSD), F32))
    res = pl.pallas_call(
        functools.partial(_ssd_kernel, seq=seq, need_y=need_y),
        out_shape=out_shape,
        grid=(bn,),
        in_specs=[blk(Z_COL), blk(XS_COL), blk(BS_COL), blk(CS_COL),
                  pl.BlockSpec((seq, LANES), lambda b: (b, 0))]
                 + [full(a) for a in args] + [st_spec],
        out_specs=out_specs,
        scratch_shapes=scratch,
        compiler_params=_params("parallel"),
        name="ssd_scan",
    )(pr, pr, pr, pr, dt, *args, h0)
    return (res[0], res[1]) if need_y else (None, res[0])


def ssd_mix(pr, dt, pr_c, dt_c, conv_w, conv_b, dt_bias, a_log, d_skip, norm_g, ctx_out, bn):
    h0 = jnp.zeros((bn, 2, SSD_STATE, W_SSD), F32)
    oc, hc = ssd_scan(pr_c, dt_c, h0, conv_w, conv_b, dt_bias, a_log, d_skip, norm_g, ctx_out)
    o, _ = ssd_scan(pr, dt, hc, conv_w, conv_b, dt_bias, a_log, d_skip, norm_g, True)
    return o, oc


def mixer(pr, dt, pr_c, dt_c, pool_w, pool_scale, conv_w, na_bias, s_conv_w, s_conv_b, dt_bias, a_log, d_skip,
          s_norm_g, ctx_out, bn):
    o_ssd, oc_ssd = ssd_mix(pr, dt, pr_c, dt_c, s_conv_w, s_conv_b, dt_bias, a_log, d_skip, s_norm_g, ctx_out, bn)
    o = [*local_mix(pr, pool_w, pool_scale, conv_w, bn), na_attention(pr, pr_c, na_bias, bn), o_ssd]
    if not ctx_out:
        return o, None
    oc = [*local_mix(pr_c, pool_w, pool_scale, conv_w, bn), ctx_attention(pr_c, bn), oc_ssd]
    return o, oc


TM = 512
TM_ROUTE = 1024
TM_PROJ = 1024
TM_GRP = 512


def kernel(x, c, ctx, c_ctx, w_ada, b_ada, g_mix, g_ffn, w_in, w_out, pool_w, pool_scale, conv_w, na_rpb,
           ssd_conv_w, ssd_conv_b, ssd_dt_bias, ssd_a_log, ssd_d, ssd_norm_g, ffn_w_gu, ffn_w_down,
           moe_router, moe_w_gu, moe_w_down, g_final):
    bn, S, d = x.shape
    Lc = ctx.shape[1]
    m, mc = bn * S, bn * Lc
    tpb = S // TM
    x2 = x.reshape(m, d)
    xc2 = ctx.reshape(mc, d)
    c_rows = jnp.concatenate([c, c_ctx[None, :], jnp.zeros((7, d), F32)], axis=0)
    for l in range(DEPTH):
        ctx_out = l < DEPTH - 1
        last = l == DEPTH - 1
        ada = ada_modulation(c_rows, w_ada[l].astype(BF16), b_ada[l][None, :])
        mod = ada[:bn].reshape(bn, 6, d)
        mod_c = ada[bn:bn + 1].reshape(1, 6, d)
        w_in_l = jnp.pad(w_in[l], ((0, 0), (0, D_IN_PAD - D_IN_PROJ))).astype(BF16)
        g_m = g_mix[l][None, :]
        g_f = g_ffn[l][None, :]
        pr, dt = in_proj(x2, g_m, mod, w_in_l, S // TM_PROJ, TM_PROJ)
        pr_c, dt_c = in_proj(xc2, g_m, mod_c, w_in_l, None, min(TM_PROJ, mc))
        na_bias = na_bias_table(na_rpb[l], S // GRID_W)
        o, oc = mixer(pr, dt, pr_c, dt_c, pool_w[l], pool_scale[l], conv_w[l], na_bias, ssd_conv_w[l],
                      ssd_conv_b[l], ssd_dt_bias[l], ssd_a_log[l], ssd_d[l], ssd_norm_g[l], ctx_out, bn)
        w_out_l = w_out[l].astype(BF16)
        j = l // 2
        if l % 2 == 0:
            w_gu = ffn_w_gu[j].astype(BF16)
            w_dn = ffn_w_down[j].astype(BF16)
            x2 = ffn_dense(x2, o, g_f, mod, w_out_l, w_gu, w_dn, tpb, TM)
            if ctx_out:
                xc2 = ffn_dense(xc2, oc, g_f, mod_c, w_out_l, w_gu, w_dn, None, min(TM, mc))
        else:
            w_r = jnp.pad(moe_router[j], ((0, 0), (0, LANES - N_EXPERTS))).astype(BF16)
            w_gu, w_dn = moe_w_gu[j], moe_w_down[j]
            x2 = moe_block(x2, o, g_f, mod, w_out_l, w_r, w_gu, w_dn, S // TM_ROUTE, TM_ROUTE, TM_GRP,
                           g_final[None, :] if last else None)
            if ctx_out:
                xc2 = moe_block(xc2, oc, g_f, mod_c, w_out_l, w_r, w_gu, w_dn, None, min(TM_ROUTE, mc), TM_GRP)
            if last:
                return x2.reshape(bn, S, d)
    return final_norm(x2, g_final[None, :], TM).reshape(bn, S, d)
```

```python
import functools
import math

import numpy as np
import jax
import jax.numpy as jnp
from jax import lax
from jax.experimental import pallas as pl
from jax.experimental.pallas import tpu as pltpu

DEPTH = 2
GRID_W = 64
EPS = 1e-6
W_POOL = 256
W_CONV = 256
W_NA = 256
W_SSD = 256
POOL_WINDOWS = (2, 4, 8, 16)
POOL_GROUP = W_POOL // len(POOL_WINDOWS)
NA_HEADS = 4
NA_HEAD_DIM = W_NA // NA_HEADS
WIN_R = 8
WIN_C = 16
SSD_HEAD_DIM = 64
SSD_HEADS = W_SSD // SSD_HEAD_DIM
SSD_GROUPS = 2
SSD_STATE = 128
SSD_CHUNK = 128
SSD_XBC = W_SSD + 2 * SSD_GROUPS * SSD_STATE
D_IN_PROJ = W_POOL + 3 * W_CONV + 3 * W_NA + W_SSD + SSD_XBC + 2 * SSD_HEADS
N_EXPERTS = 8
TOP_K = 2

LANES = 128
D_IN_PAD = -(-D_IN_PROJ // LANES) * LANES
VMEM_LIMIT = 56 * 1024 * 1024
BF16 = jnp.bfloat16
F32 = jnp.float32


def _params(*sem):
    return pltpu.CompilerParams(dimension_semantics=sem, vmem_limit_bytes=VMEM_LIMIT)


def _norm_mod(x, g, scale, shift):
    ms = jnp.mean(x * x, axis=-1, keepdims=True)
    return (x * lax.rsqrt(ms + EPS) * g) * (1.0 + scale) + shift


def _mod_map(tiles_per_batch):
    if tiles_per_batch is None:
        return lambda i, *_: (0, 0, 0)
    return lambda i, *_: (i // tiles_per_batch, 0, 0)


def _resident(shape, index_map):
    return pl.BlockSpec(shape, index_map, pipeline_mode=pl.Buffered(1))


def _ada_kernel(c_ref, w_ref, b_ref, o_ref):
    c = c_ref[...]
    s = c * jax.nn.sigmoid(c)
    o_ref[...] = jnp.dot(s.astype(BF16), w_ref[...], preferred_element_type=F32) + b_ref[...]


def ada_modulation(c_rows, w, b):
    r, d = c_rows.shape
    n = w.shape[1]
    tn = 1536
    return pl.pallas_call(
        _ada_kernel,
        out_shape=jax.ShapeDtypeStruct((r, n), F32),
        grid=(n // tn,),
        in_specs=[pl.BlockSpec((r, d), lambda j: (0, 0)),
                  pl.BlockSpec((d, tn), lambda j: (0, j)),
                  pl.BlockSpec((1, tn), lambda j: (0, j))],
        out_specs=pl.BlockSpec((r, tn), lambda j: (0, j)),
        compiler_params=_params("arbitrary"),
        name="ada_modulation",
    )(c_rows, w, b)


def _in_proj_kernel(x_ref, g_ref, mod_ref, w_ref, o_ref, dt_ref):
    h = _norm_mod(x_ref[...], g_ref[...], mod_ref[0, 1:2, :], mod_ref[0, 0:1, :])
    pr = jnp.dot(h.astype(BF16), w_ref[...], preferred_element_type=F32)
    o_ref[...] = pr.astype(o_ref.dtype)
    dt_ref[...] = pr[:, DT_COL * LANES:(DT_COL + 1) * LANES]


def in_proj(x2, g, mod, w, tiles_per_batch, tm):
    m, d = x2.shape
    n = w.shape[1]
    return pl.pallas_call(
        _in_proj_kernel,
        out_shape=(jax.ShapeDtypeStruct((m, n), BF16), jax.ShapeDtypeStruct((m, LANES), F32)),
        grid=(m // tm,),
        in_specs=[pl.BlockSpec((tm, d), lambda i: (i, 0)),
                  pl.BlockSpec((1, d), lambda i: (0, 0)),
                  pl.BlockSpec((1, 6, d), _mod_map(tiles_per_batch)),
                  _resident((d, n), lambda i: (0, 0))],
        out_specs=(pl.BlockSpec((tm, n), lambda i: (i, 0)), pl.BlockSpec((tm, LANES), lambda i: (i, 0))),
        compiler_params=_params("parallel"),
        name="in_proj",
    )(x2, g, mod, w)


def _mix_residual(x, mod_ref, w_ref, part_refs):
    y, k0 = None, 0
    for p_ref in part_refs:
        k = p_ref.shape[1]
        t = jnp.dot(p_ref[...], w_ref[k0:k0 + k, :], preferred_element_type=F32)
        y = t if y is None else y + t
        k0 += k
    return x + mod_ref[0, 2:3, :] * y


def _part_specs(parts, tm):
    return [pl.BlockSpec((tm, p.shape[1]), lambda i, *_: (i, 0)) for p in parts]


FF_CHUNK = 512


def _swiglu(h, wgu, wd, ff):
    acc = None
    for c0 in range(0, ff, FF_CHUNK):
        c1 = min(c0 + FF_CHUNK, ff)
        gg = jnp.dot(h, wgu(c0, c1), preferred_element_type=F32)
        uu = jnp.dot(h, wgu(ff + c0, ff + c1), preferred_element_type=F32)
        a = (gg * jax.nn.sigmoid(gg) * uu).astype(BF16)
        part = jnp.dot(a, wd(c0, c1), preferred_element_type=F32)
        acc = part if acc is None else acc + part
    return acc


def _ffn_kernel(x_ref, g_ref, mod_ref, wout_ref, wgu_ref, wd_ref, *refs):
    x = _mix_residual(x_ref[...], mod_ref, wout_ref, refs[:-1])
    y_ref = refs[-1]
    h = _norm_mod(x, g_ref[...], mod_ref[0, 4:5, :], mod_ref[0, 3:4, :]).astype(BF16)
    y = _swiglu(h, lambda a, b: wgu_ref[:, a:b], lambda a, b: wd_ref[a:b, :], wd_ref.shape[0])
    y_ref[...] = x + mod_ref[0, 5:6, :] * y


def ffn_dense(x2, parts, g, mod, w_out, w_gu, w_down, tiles_per_batch, tm):
    m, d = x2.shape
    return pl.pallas_call(
        _ffn_kernel,
        out_shape=jax.ShapeDtypeStruct((m, d), F32),
        grid=(m // tm,),
        in_specs=[pl.BlockSpec((tm, d), lambda i: (i, 0)),
                  pl.BlockSpec((1, d), lambda i: (0, 0)),
                  pl.BlockSpec((1, 6, d), _mod_map(tiles_per_batch)),
                  _resident(w_out.shape, lambda i: (0, 0)),
                  _resident(w_gu.shape, lambda i: (0, 0)),
                  _resident(w_down.shape, lambda i: (0, 0))] + _part_specs(parts, tm),
        out_specs=pl.BlockSpec((tm, d), lambda i: (i, 0)),
        compiler_params=_params("parallel"),
        name="ffn_dense",
    )(x2, g, mod, w_out, w_gu, w_down, *parts)


ROUTE_E1, ROUTE_E2, ROUTE_R1, ROUTE_R2, ROUTE_G1, ROUTE_G2 = range(6)
ROUTE_ROWS = 8


def _router_kernel(x_ref, g_ref, mod_ref, wout_ref, wr_ref, *refs):
    part_refs = refs[:-6]
    x1_ref, h_ref, route_ref, route_t_ref, cnt_ref, base_ref = refs[-6:]

    @pl.when(pl.program_id(0) == 0)
    def _():
        base_ref[...] = jnp.zeros_like(base_ref)

    x = _mix_residual(x_ref[...], mod_ref, wout_ref, part_refs)
    x1_ref[...] = x
    h = _norm_mod(x, g_ref[...], mod_ref[0, 4:5, :], mod_ref[0, 3:4, :]).astype(BF16)
    h_ref[...] = h
    tm = h.shape[0]
    logits = jnp.dot(h, wr_ref[...], preferred_element_type=F32)
    lane = lax.broadcasted_iota(jnp.int32, logits.shape, 1)
    neg = jnp.float32(-jnp.inf)
    logits = jnp.where(lane < N_EXPERTS, logits, neg)
    m1 = jnp.max(logits, axis=-1, keepdims=True)
    i1 = jnp.min(jnp.where(logits == m1, lane, LANES), axis=-1, keepdims=True)
    rest = jnp.where(lane == i1, neg, logits)
    m2 = jnp.max(rest, axis=-1, keepdims=True)
    i2 = jnp.min(jnp.where(rest == m2, lane, LANES), axis=-1, keepdims=True)
    e2 = jnp.exp(m2 - m1)
    g1 = 1.0 / (1.0 + e2)
    g2 = e2 / (1.0 + e2)
    sel1, sel2 = lane == i1, lane == i2
    sel = jnp.where(sel1 | sel2, 1.0, 0.0)
    li = lax.broadcasted_iota(jnp.int32, (tm, tm), 0)
    si = lax.broadcasted_iota(jnp.int32, (tm, tm), 1)
    before = jnp.where(si < li, 1.0, 0.0).astype(BF16)
    rank = jnp.dot(before, sel.astype(BF16), preferred_element_type=F32) + base_ref[...]
    r1 = jnp.sum(jnp.where(sel1, rank, 0.0), axis=-1, keepdims=True)
    r2 = jnp.sum(jnp.where(sel2, rank, 0.0), axis=-1, keepdims=True)
    base_ref[...] += jnp.sum(sel, axis=0, keepdims=True)
    cnt_ref[...] = base_ref[...]
    fields = (i1.astype(F32), i2.astype(F32), r1, r2, g1, g2)
    route = jnp.zeros(logits.shape, F32)
    for k, v in enumerate(fields):
        route = jnp.where(lane == k, v, route)
    route_ref[...] = route
    route_t_ref[...] = route.T[:ROUTE_ROWS, :]


def moe_router(x2, parts, g, mod, w_out, w_router, tiles_per_batch, tm):
    m, d = x2.shape
    return pl.pallas_call(
        _router_kernel,
        out_shape=(jax.ShapeDtypeStruct((m, d), F32), jax.ShapeDtypeStruct((m, d), BF16),
                   jax.ShapeDtypeStruct((m, LANES), F32), jax.ShapeDtypeStruct((ROUTE_ROWS, m), F32),
                   jax.ShapeDtypeStruct((1, LANES), F32)),
        grid=(m // tm,),
        in_specs=[pl.BlockSpec((tm, d), lambda i: (i, 0)),
                  pl.BlockSpec((1, d), lambda i: (0, 0)),
                  pl.BlockSpec((1, 6, d), _mod_map(tiles_per_batch)),
                  _resident(w_out.shape, lambda i: (0, 0)),
                  pl.BlockSpec((d, LANES), lambda i: (0, 0))] + _part_specs(parts, tm),
        out_specs=(pl.BlockSpec((tm, d), lambda i: (i, 0)),
                   pl.BlockSpec((tm, d), lambda i: (i, 0)),
                   pl.BlockSpec((tm, LANES), lambda i: (i, 0)),
                   pl.BlockSpec((ROUTE_ROWS, tm), lambda i: (0, i)),
                   pl.BlockSpec((1, LANES), lambda i: (0, 0))),
        scratch_shapes=[pltpu.VMEM((1, LANES), F32)],
        compiler_params=_params("arbitrary"),
        name="moe_router",
    )(x2, g, mod, w_out, w_router, *parts)


def _moe_kernel(tile_ref, exp_ref, start_ref, end_ref, xg_ref, wgu_ref, wd_ref, y_ref, wgu_b, wd_b):
    w = pl.program_id(0)
    tm = xg_ref.shape[0]
    start, end = start_ref[w], end_ref[w]
    t0 = tile_ref[w] * tm

    @pl.when((w == 0) | (exp_ref[w] != exp_ref[jnp.maximum(w - 1, 0)]))
    def _():
        wgu_b[...] = wgu_ref[0].astype(BF16)
        wd_b[...] = wd_ref[0].astype(BF16)

    @pl.when(end > start)
    def _():
        y = _swiglu(xg_ref[...], lambda a, b: wgu_b[:, a:b], lambda a, b: wd_b[a:b, :], wd_b.shape[0])
        y = y.astype(y_ref.dtype)

        @pl.when(start == t0)
        def _():
            y_ref[...] = y

        @pl.when(start != t0)
        def _():
            row = t0 + lax.broadcasted_iota(jnp.int32, (tm, 1), 0)
            y_ref[...] = jnp.where(row >= start, y, y_ref[...])


def moe_grouped(item_tile, item_expert, item_start, item_end, xg, w_gu, w_down, tm):
    p, d = xg.shape
    grid_spec = pltpu.PrefetchScalarGridSpec(
        num_scalar_prefetch=4,
        grid=(item_tile.shape[0],),
        in_specs=[pl.BlockSpec((tm, d), lambda w, it, ie, s, e: (it[w], 0)),
                  _resident((1,) + w_gu.shape[1:], lambda w, it, ie, s, e: (ie[w], 0, 0)),
                  _resident((1,) + w_down.shape[1:], lambda w, it, ie, s, e: (ie[w], 0, 0))],
        out_specs=pl.BlockSpec((tm, d), lambda w, it, ie, s, e: (it[w], 0)),
        scratch_shapes=[pltpu.VMEM(w_gu.shape[1:], BF16), pltpu.VMEM(w_down.shape[1:], BF16)],
    )
    return pl.pallas_call(
        _moe_kernel,
        out_shape=jax.ShapeDtypeStruct((p, d), BF16),
        grid_spec=grid_spec,
        compiler_params=_params("arbitrary"),
        name="moe_grouped",
    )(item_tile, item_expert, item_start, item_end, xg, w_gu, w_down)


def _moe_combine_kernel(x_ref, ya_ref, yb_ref, route_ref, mod_ref, *refs):
    o_ref = refs[-1]
    r = route_ref[...]
    y = (r[:, ROUTE_G1:ROUTE_G1 + 1] * ya_ref[...].astype(F32) + r[:, ROUTE_G2:ROUTE_G2 + 1] * yb_ref[...].astype(F32))
    x = x_ref[...] + mod_ref[0, 5:6, :] * y
    if len(refs) == 2:
        ms = jnp.mean(x * x, axis=-1, keepdims=True)
        x = x * lax.rsqrt(ms + EPS) * refs[0][...]
    o_ref[...] = x


def moe_combine(x2, ya, yb, route, mod, tiles_per_batch, tm, g_final=None):
    m, d = x2.shape
    row = pl.BlockSpec((tm, d), lambda i: (i, 0))
    specs = [row, row, row, pl.BlockSpec((tm, LANES), lambda i: (i, 0)),
             pl.BlockSpec((1, 6, d), _mod_map(tiles_per_batch))]
    args = [x2, ya, yb, route, mod]
    if g_final is not None:
        specs.append(pl.BlockSpec((1, d), lambda i: (0, 0)))
        args.append(g_final)
    return pl.pallas_call(
        _moe_combine_kernel,
        out_shape=jax.ShapeDtypeStruct((m, d), F32),
        grid=(m // tm,),
        in_specs=specs,
        out_specs=row,
        compiler_params=_params("parallel"),
        name="moe_combine",
    )(*args)


def moe_block(x2, parts, g, mod, w_out, w_router, w_gu, w_down, tiles_per_batch, tm_tok, tm_grp, g_final=None):
    m, d = x2.shape
    x2, h, route, route_t, cnt = moe_router(x2, parts, g, mod, w_out, w_router, tiles_per_batch, tm_tok)
    cnt = cnt[0, :N_EXPERTS].astype(jnp.int32)
    p = m * TOP_K
    off = jnp.cumsum(cnt) - cnt
    field = lambda k: route_t[k].astype(jnp.int32)

    def row_of(e, r):
        for k in range(N_EXPERTS):
            r = r + jnp.where(e == k, off[k], 0)
        return r

    d1 = row_of(field(ROUTE_E1), field(ROUTE_R1))
    d2 = row_of(field(ROUTE_E2), field(ROUTE_R2))
    tok = jnp.arange(m, dtype=jnp.int32)
    _, src = lax.sort_key_val(jnp.concatenate([d1, d2]), jnp.concatenate([tok, tok]))
    n_row_tiles = p // tm_grp
    start = jnp.sort(jnp.concatenate([jnp.arange(n_row_tiles, dtype=jnp.int32) * tm_grp, off]))
    end = jnp.concatenate([start[1:], jnp.full((1,), p, jnp.int32)])
    item_tile = jnp.minimum(start // tm_grp, n_row_tiles - 1)
    item_expert = jnp.sum((off[None, :] <= start[:, None]).astype(jnp.int32), axis=1) - 1
    rows = lambda a, idx: a.at[idx].get(mode="promise_in_bounds")
    yg = moe_grouped(item_tile, item_expert, start, end, rows(h, src), w_gu, w_down, tm_grp)
    return moe_combine(x2, rows(yg, d1), rows(yg, d2), route, mod, tiles_per_batch, tm_tok, g_final)


def _final_norm_kernel(x_ref, g_ref, o_ref):
    x = x_ref[...]
    ms = jnp.mean(x * x, axis=-1, keepdims=True)
    o_ref[...] = x * lax.rsqrt(ms + EPS) * g_ref[...]


def final_norm(x2, g, tm):
    m, d = x2.shape
    return pl.pallas_call(
        _final_norm_kernel,
        out_shape=jax.ShapeDtypeStruct((m, d), F32),
        grid=(m // tm,),
        in_specs=[pl.BlockSpec((tm, d), lambda i: (i, 0)), pl.BlockSpec((1, d), lambda i: (0, 0))],
        out_specs=pl.BlockSpec((tm, d), lambda i: (i, 0)),
        compiler_params=_params("parallel"),
        name="final_norm",
    )(x2, g)


NA_QROWS = 4
NA_KROWS = NA_QROWS + WIN_R
Q_COL, K_COL, V_COL = 4, 5, 6


def _na_key_start(j, rows):
    return np.clip(j * NA_QROWS - WIN_R // 2, 0, rows - NA_KROWS)


def _na_bias_index(rows):
    nblk = rows // NA_QROWS
    pats = []
    for j in range(nblk):
        start = _na_key_start(j, rows)
        r = j * NA_QROWS + np.arange(NA_QROWS)
        sr = np.clip(r - WIN_R // 2, 0, rows - WIN_R)
        kr = start + np.arange(NA_KROWS)
        rvalid = (kr[None, :] >= sr[:, None]) & (kr[None, :] < sr[:, None] + WIN_R)
        ri = np.clip(kr[None, :] - r[:, None] + WIN_R - 1, 0, 2 * WIN_R - 2)
        pats.append((ri, rvalid))
    for j in range(2, nblk - 1):
        assert all(np.array_equal(a, b) for a, b in zip(pats[1], pats[j]))
    sel = [pats[0], pats[1], pats[nblk - 1]]
    ri = np.stack([p[0] for p in sel])
    rvalid = np.stack([p[1] for p in sel])
    c = np.arange(GRID_W)
    sc = np.clip(c - WIN_C // 2, 0, GRID_W - WIN_C)
    cvalid = (c[None, :] >= sc[:, None]) & (c[None, :] < sc[:, None] + WIN_C)
    ci = np.clip(c[None, :] - c[:, None] + WIN_C - 1, 0, 2 * WIN_C - 2)
    c_onehot = (ci[..., None] == np.arange(2 * WIN_C - 1)).astype(np.float32)
    valid = rvalid[:, :, None, :, None] & cvalid[None, None, :, None, :]
    return ri, c_onehot, valid


def na_bias_table(rpb, rows):
    ri, c_onehot, valid = _na_bias_index(rows)
    toep = jnp.einsum('hrd,qkd->hrqk', rpb, c_onehot, precision=lax.Precision.HIGHEST)
    b = jnp.take(toep, ri.reshape(-1), axis=1).reshape((NA_HEADS,) + ri.shape + (GRID_W, GRID_W))
    b = jnp.transpose(b, (1, 0, 2, 4, 3, 5))
    b = jnp.where(valid[:, None], b, -jnp.inf)
    return b.reshape(3, NA_HEADS, NA_QROWS * GRID_W, NA_KROWS * GRID_W).astype(F32)


def _attend_heads(q, key_sets, bias_fn):
    n, w = q.shape
    lane = lax.broadcasted_iota(jnp.int32, (1, w), 1)
    head_masks = [(lane >= h * NA_HEAD_DIM) & (lane < (h + 1) * NA_HEAD_DIM) for h in range(NA_HEADS)]
    all_scores = []
    for h, mh in enumerate(head_masks):
        qh = jnp.where(mh, q, 0.0).astype(BF16)
        scores = []
        for i, (k, _) in enumerate(key_sets):
            s = lax.dot_general(qh, k, (((1,), (1,)), ((), ())), preferred_element_type=F32)
            b = bias_fn(i, h)
            scores.append(s if b is None else s + b)
        all_scores.append(scores)
    all_probs = []
    for scores in all_scores:
        m = scores[0].max(axis=-1, keepdims=True)
        for s in scores[1:]:
            m = jnp.maximum(m, s.max(axis=-1, keepdims=True))
        denom = jnp.zeros((n, 1), F32)
        probs = []
        for s in scores:
            p = jnp.exp(s - m)
            denom = denom + p.sum(axis=-1, keepdims=True)
            probs.append(p.astype(BF16))
        all_probs.append((probs, denom))
    out = jnp.zeros((n, w), F32)
    for mh, (probs, denom) in zip(head_masks, all_probs):
        acc = jnp.zeros((n, w), F32)
        for p, (_, v) in zip(probs, key_sets):
            acc = acc + jnp.dot(p, v, preferred_element_type=F32)
        out = out + jnp.where(mh, acc / denom, 0.0)
    return out


def _na_kernel(q_ref, k_ref, v_ref, kc_ref, vc_ref, bias_ref, o_ref, *, rows):
    j = pl.program_id(1)
    start = jnp.clip(j * NA_QROWS - WIN_R // 2, 0, rows - NA_KROWS)
    t0 = pl.multiple_of(start * GRID_W, GRID_W)
    nk = NA_KROWS * GRID_W
    kw = k_ref[pl.ds(t0, nk), :]
    vw = v_ref[pl.ds(t0, nk), :]
    q = q_ref[...] * (1.0 / math.sqrt(NA_HEAD_DIM))
    o = _attend_heads(q, [(kw, vw), (kc_ref[...], vc_ref[...])], lambda i, h: bias_ref[0, h] if i == 0 else None)
    o_ref[...] = o.astype(o_ref.dtype)


def na_attention(pr, pr_c, bias, bn):
    S, Lc = pr.shape[0] // bn, pr_c.shape[0] // bn
    rows = S // GRID_W
    nblk = rows // NA_QROWS
    nq = NA_QROWS * GRID_W

    def pat(b, j):
        return (jnp.where(j == 0, 0, jnp.where(j == nblk - 1, 2, 1)), 0, 0, 0)

    return pl.pallas_call(
        functools.partial(_na_kernel, rows=rows),
        out_shape=jax.ShapeDtypeStruct((bn * S, W_NA), BF16),
        grid=(bn, nblk),
        in_specs=[pl.BlockSpec((nq, W_NA), lambda b, j: (b * nblk + j, Q_COL)),
                  pl.BlockSpec((S, W_NA), lambda b, j: (b, K_COL)),
                  pl.BlockSpec((S, W_NA), lambda b, j: (b, V_COL)),
                  pl.BlockSpec((Lc, W_NA), lambda b, j: (b, K_COL)),
                  pl.BlockSpec((Lc, W_NA), lambda b, j: (b, V_COL)),
                  pl.BlockSpec((1,) + bias.shape[1:], pat)],
        out_specs=pl.BlockSpec((nq, W_NA), lambda b, j: (b * nblk + j, 0)),
        compiler_params=_params("parallel", "arbitrary"),
        name="na_attention",
    )(pr, pr, pr, pr_c, pr_c, bias)


def _ctx_attn_kernel(q_ref, k_ref, v_ref, o_ref):
    q = q_ref[...] * (1.0 / math.sqrt(NA_HEAD_DIM))
    o = _attend_heads(q, [(k_ref[...], v_ref[...])], lambda i, h: None)
    o_ref[...] = o.astype(o_ref.dtype)


def ctx_attention(pr_c, bn):
    Lc = pr_c.shape[0] // bn
    return pl.pallas_call(
        _ctx_attn_kernel,
        out_shape=jax.ShapeDtypeStruct((bn * Lc, W_NA), BF16),
        grid=(bn,),
        in_specs=[pl.BlockSpec((Lc, W_NA), lambda b: (b, Q_COL)),
                  pl.BlockSpec((Lc, W_NA), lambda b: (b, K_COL)),
                  pl.BlockSpec((Lc, W_NA), lambda b: (b, V_COL))],
        out_specs=pl.BlockSpec((Lc, W_NA), lambda b: (b, 0)),
        compiler_params=_params("parallel"),
        name="ctx_attention",
    )(pr_c, pr_c, pr_c)


POOL_COL, CONV_H_COL, CONV_B_COL, CONV_C_COL = 0, 1, 2, 3


def _shift_rows(x, k, row):
    n = x.shape[0]
    y = pltpu.roll(x, k % n, 0)
    return jnp.where(row < k, 0.0, y) if k > 0 else jnp.where(row >= n + k, 0.0, y)


def _local_mix_kernel(u_ref, h_ref, bg_ref, cg_ref, pw_ref, ps_ref, cw_ref, op_ref, oc_ref, *, seq):
    row = lax.broadcasted_iota(jnp.int32, (seq, 1), 0)
    lane = lax.broadcasted_iota(jnp.int32, (1, W_POOL), 1)
    u = u_ref[...].astype(F32)
    trailing, leading = {1: u}, {1: u}
    for w in (1, 2, 4):
        trailing[2 * w] = trailing[w] + _shift_rows(trailing[w], w, row)
        leading[2 * w] = leading[w] + _shift_rows(leading[w], -w, row)
    rowf = row.astype(F32)
    win_sum = jnp.zeros_like(u)
    cnt = jnp.zeros_like(u)
    for g, win in enumerate(POOL_WINDOWS):
        half = win // 2
        mg = (lane >= g * POOL_GROUP) & (lane < (g + 1) * POOL_GROUP)
        s = _shift_rows(trailing[half], 1, row) + leading[half]
        n = jnp.minimum(rowf + half, float(seq)) - jnp.maximum(rowf - half, 0.0)
        win_sum = jnp.where(mg, s, win_sum)
        cnt = jnp.where(mg, n, cnt)
    p = win_sum / cnt - u
    pooled = jnp.dot(p.astype(BF16), pw_ref[...], preferred_element_type=F32) * ps_ref[...]
    op_ref[...] = pooled.astype(op_ref.dtype)
    v = cg_ref[...].astype(F32) * h_ref[...].astype(F32)
    conv = (_shift_rows(v, 1, row) * cw_ref[0:1, :] + v * cw_ref[1:2, :] + _shift_rows(v, -1, row) * cw_ref[2:3, :])
    oc_ref[...] = (bg_ref[...].astype(F32) * conv).astype(oc_ref.dtype)


def local_mix(pr, pool_w, pool_scale, conv_w, bn):
    seq = pr.shape[0] // bn
    pw = jax.scipy.linalg.block_diag(*[pool_w[g] for g in range(len(POOL_WINDOWS))]).astype(BF16)
    blk = lambda col: pl.BlockSpec((seq, W_POOL), lambda b: (b, col))
    full = lambda a: pl.BlockSpec(a.shape, lambda b: (0,) * a.ndim)
    args = (pw, pool_scale[None, :], conv_w)
    return pl.pallas_call(
        functools.partial(_local_mix_kernel, seq=seq),
        out_shape=(jax.ShapeDtypeStruct((bn * seq, W_POOL), BF16), jax.ShapeDtypeStruct((bn * seq, W_CONV), BF16)),
        grid=(bn,),
        in_specs=[blk(POOL_COL), blk(CONV_H_COL), blk(CONV_B_COL), blk(CONV_C_COL)] + [full(a) for a in args],
        out_specs=(pl.BlockSpec((seq, W_POOL), lambda b: (b, 0)),
                   pl.BlockSpec((seq, W_CONV), lambda b: (b, 0))),
        compiler_params=_params("parallel"),
        name="local_mix",
    )(pr, pr, pr, pr, *args)


Z_COL, XS_COL, BS_COL, CS_COL = 7, 8, 9, 10
DT_COL = 22
HEADS_PER_GROUP = SSD_HEADS // SSD_GROUPS
GROUP_W = HEADS_PER_GROUP * SSD_HEAD_DIM


def _head_mask(h):
    lane = lax.broadcasted_iota(jnp.int32, (1, W_SSD), 1)
    return (lane >= h * SSD_HEAD_DIM) & (lane < (h + 1) * SSD_HEAD_DIM)


def _expand_heads(v, d):
    out = jnp.zeros((v.shape[0], W_SSD), F32)
    for h in range(SSD_HEADS):
        k = d * SSD_HEADS + h
        out = jnp.where(_head_mask(h), v[:, k:k + 1], out)
    return out


def _ssd_chunk_terms(c, xs_ref, b_ref, c_ref, dt_ref, la_ref, dsk_ref, y_ref, upd_ref, dec_ref, need_y):
    T = SSD_CHUNK
    r0 = pl.multiple_of(c * T, T)
    xs = xs_ref[pl.ds(r0, T), :]
    bm = b_ref[pl.ds(r0, T), :]
    cm = c_ref[pl.ds(r0, T), :]
    dt = dt_ref[pl.ds(r0, T), :]
    la = la_ref[pl.ds(r0, T), :]
    li = lax.broadcasted_iota(jnp.int32, (T, T), 0)
    si = lax.broadcasted_iota(jnp.int32, (T, T), 1)
    causal = si <= li
    prefix = jnp.dot(causal.astype(F32), la, precision=lax.Precision.HIGHEST, preferred_element_type=F32)
    total = prefix[T - 1:T, :]
    lane = lax.broadcasted_iota(jnp.int32, (1, LANES), 1)
    acum = jnp.where(lane < SSD_HEADS, prefix, total - prefix + la)
    bt = bm.astype(F32).T.astype(BF16)
    if need_y:
        acum_t = acum.T
        cb = [lax.dot_general(cm[:, g * SSD_STATE:(g + 1) * SSD_STATE], bm[:, g * SSD_STATE:(g + 1) * SSD_STATE],
                              (((1,), (1,)), ((), ())), preferred_element_type=F32) for g in range(SSD_GROUPS)]
        y = jnp.zeros((T, W_SSD), F32)
    per_dir = []
    for d in range(2):
        acum_e = _expand_heads(acum, d)
        tot_e = _expand_heads(total, d)
        xdt = xs * _expand_heads(dt, d)
        xw = (xdt * jnp.exp(tot_e - acum_e)).astype(BF16)
        dec_ref[c, d, 0:T, :] = jnp.exp(acum_e)
        dec_ref[c, d, T:T + 1, :] = jnp.exp(tot_e)
        scores = []
        if need_y:
            mask = causal if d == 0 else (si >= li)
            for h in range(SSD_HEADS):
                k = d * SSD_HEADS + h
                decay = jnp.exp(jnp.where(mask, acum[:, k:k + 1] - acum_t[k:k + 1, :], -jnp.inf))
                scores.append((cb[h // HEADS_PER_GROUP] * decay).astype(BF16))
        per_dir.append((xw, xdt, scores))
    for d, (xw, xdt, scores) in enumerate(per_dir):
        upd = [jnp.dot(bt[g * SSD_STATE:(g + 1) * SSD_STATE, :], xw[:, g * GROUP_W:(g + 1) * GROUP_W],
                       preferred_element_type=F32) for g in range(SSD_GROUPS)]
        upd_ref[c, d] = jnp.concatenate(upd, axis=1)
        if need_y:
            x_heads = [jnp.where(_head_mask(h), xdt, 0.0).astype(BF16) for h in range(SSD_HEADS)]
            y = y + jnp.dot(jnp.concatenate(scores, axis=1), jnp.concatenate(x_heads, axis=0),
                            preferred_element_type=F32)
    if need_y:
        y_ref[pl.ds(r0, T), :] = xs * dsk_ref[...] + y


def _ssd_chunk_state(c, d, c_ref, y_ref, upd_ref, dec_ref, st_ref, need_y):
    T = SSD_CHUNK
    r0 = pl.multiple_of(c * T, T)
    st = st_ref[d]
    if need_y:
        cm = c_ref[pl.ds(r0, T), :]
        st_b = st.astype(BF16)
        ys = [jnp.dot(cm[:, g * SSD_STATE:(g + 1) * SSD_STATE], st_b[:, g * GROUP_W:(g + 1) * GROUP_W],
                      preferred_element_type=F32) for g in range(SSD_GROUPS)]
        y_ref[pl.ds(r0, T), :] += jnp.concatenate(ys, axis=1) * dec_ref[c, d, 0:T, :]
    st_ref[d] = dec_ref[c, d, T:T + 1, :] * st + upd_ref[c, d]


def _ssd_kernel(z_ref, xs_in, bs_in, cs_in, dtr_ref, cw_ref, cb_ref, dtb_ref, alog_ref, dsk_ref, ng_ref, h0_ref,
                *refs, seq, need_y):
    if need_y:
        o_ref, hT_ref, xs_ref, b_ref, c_ref, dt_ref, la_ref, upd_ref, dec_ref, st_ref, y_ref = refs
    else:
        hT_ref, xs_ref, b_ref, c_ref, dt_ref, la_ref, upd_ref, dec_ref, st_ref = refs
        y_ref = None
    nc = seq // SSD_CHUNK
    row = lax.broadcasted_iota(jnp.int32, (seq, 1), 0)
    for gi, (src, dst) in enumerate(((xs_in, xs_ref), (bs_in, b_ref), (cs_in, c_ref))):
        sl = slice(gi * W_SSD, (gi + 1) * W_SSD)
        x = src[...].astype(F32)
        cv = (_shift_rows(x, 1, row) * cw_ref[0:1, sl] + x * cw_ref[1:2, sl]
              + _shift_rows(x, -1, row) * cw_ref[2:3, sl] + cb_ref[:, sl])
        dst[...] = (cv * jax.nn.sigmoid(cv)).astype(dst.dtype)
    dtv = dtr_ref[...] + dtb_ref[...]
    dt = jnp.maximum(dtv, 0.0) + jnp.log1p(jnp.exp(-jnp.abs(dtv)))
    dt_ref[...] = dt
    la_ref[...] = dt * (-jnp.exp(alog_ref[...]))
    st_ref[...] = h0_ref[0]

    def terms(c, carry):
        _ssd_chunk_terms(c, xs_ref, b_ref, c_ref, dt_ref, la_ref, dsk_ref, y_ref, upd_ref, dec_ref, need_y)
        return carry

    lax.fori_loop(0, nc, terms, 0, unroll=min(4, nc))

    def recur(i, carry):
        _ssd_chunk_state(i, 0, c_ref, y_ref, upd_ref, dec_ref, st_ref, need_y)
        _ssd_chunk_state(nc - 1 - i, 1, c_ref, y_ref, upd_ref, dec_ref, st_ref, need_y)
        return carry

    lax.fori_loop(0, nc, recur, 0, unroll=2)
    hT_ref[0] = st_ref[...]
    if need_y:
        z = z_ref[...].astype(F32)
        yz = y_ref[...] * (z * jax.nn.sigmoid(z))
        ms = jnp.mean(yz * yz, axis=-1, keepdims=True)
        o_ref[...] = (yz * lax.rsqrt(ms + EPS) * ng_ref[...]).astype(o_ref.dtype)


def ssd_scan(pr, dt, h0, conv_w, conv_b, dt_bias, a_log, d_skip, norm_g, need_y):
    bn = h0.shape[0]
    seq = pr.shape[0] // bn
    nc = seq // SSD_CHUNK
    pad = LANES - 2 * SSD_HEADS
    dtb = jnp.pad(dt_bias.reshape(1, -1), ((0, 0), (0, pad)))
    alog = jnp.pad(a_log.reshape(1, -1), ((0, 0), (0, pad)))
    dsk = jnp.repeat(d_skip, SSD_HEAD_DIM)[None, :]
    blk = lambda col: pl.BlockSpec((seq, W_SSD), lambda b: (b, col))
    full = lambda a: pl.BlockSpec(a.shape, lambda b: (0,) * a.ndim)
    st_shape = (1, 2, SSD_STATE, W_SSD)
    st_spec = pl.BlockSpec(st_shape, lambda b: (b, 0, 0, 0))
    args = (conv_w, conv_b[None, :], dtb, alog, dsk, norm_g[None, :])
    out_shape = [jax.ShapeDtypeStruct((bn,) + st_shape[1:], F32)]
    out_specs = [st_spec]
    scratch = [pltpu.VMEM((seq, W_SSD), F32),
               pltpu.VMEM((seq, W_SSD), BF16), pltpu.VMEM((seq, W_SSD), BF16),
               pltpu.VMEM((seq, LANES), F32), pltpu.VMEM((seq, LANES), F32),
               pltpu.VMEM((nc, 2, SSD_STATE, W_SSD), F32),
               pltpu.VMEM((nc, 2, SSD_CHUNK + 8, W_SSD), F32),
               pltpu.VMEM(st_shape[1:], F32)]
    if need_y:
        out_shape.insert(0, jax.ShapeDtypeStruct((bn * seq, W_SSD), BF16))
        out_specs.insert(0, pl.BlockSpec((seq, W_SSD), lambda b: (b, 0)))
        scratch.append(pltpu.VMEM((seq, W_SSD), F32))
    res = pl.pallas_call(
        functools.partial(_ssd_kernel, seq=seq, need_y=need_y),
        out_shape=out_shape,
        grid=(bn,),
        in_specs=[blk(Z_COL), blk(XS_COL), blk(BS_COL), blk(CS_COL),
                  pl.BlockSpec((seq, LANES), lambda b: (b, 0))]
                 + [full(a) for a in args] + [st_spec],
        out_specs=out_specs,
        scratch_shapes=scratch,
        compiler_params=_params("parallel"),
        name="ssd_scan",
    )(pr, pr, pr, pr, dt, *args, h0)
    return (res[0], res[1]) if need_y else (None, res[0])


def ssd_mix(pr, dt, pr_c, dt_c, conv_w, conv_b, dt_bias, a_log, d_skip, norm_g, ctx_out, bn):
    h0 = jnp.zeros((bn, 2, SSD_STATE, W_SSD), F32)
    oc, hc = ssd_scan(pr_c, dt_c, h0, conv_w, conv_b, dt_bias, a_log, d_skip, norm_g, ctx_out)
    o, _ = ssd_scan(pr, dt, hc, conv_w, conv_b, dt_bias, a_log, d_skip, norm_g, True)
    return o, oc


def mixer(pr, dt, pr_c, dt_c, pool_w, pool_scale, conv_w, na_bias, s_conv_w, s_conv_b, dt_bias, a_log, d_skip,
          s_norm_g, ctx_out, bn):
    o_ssd, oc_ssd = ssd_mix(pr, dt, pr_c, dt_c, s_conv_w, s_conv_b, dt_bias, a_log, d_skip, s_norm_g, ctx_out, bn)
    o = [*local_mix(pr, pool_w, pool_scale, conv_w, bn), na_attention(pr, pr_c, na_bias, bn), o_ssd]
    if not ctx_out:
        return o, None
    oc = [*local_mix(pr_c, pool_w, pool_scale, conv_w, bn), ctx_attention(pr_c, bn), oc_ssd]
    return o, oc


TM = 512
TM_ROUTE = 1024
TM_PROJ = 1024
TM_GRP = 512


def kernel(x, c, ctx, c_ctx, w_ada, b_ada, g_mix, g_ffn, w_in, w_out, pool_w, pool_scale, conv_w, na_rpb,
           ssd_conv_w, ssd_conv_b, ssd_dt_bias, ssd_a_log, ssd_d, ssd_norm_g, ffn_w_gu, ffn_w_down,
           moe_router, moe_w_gu, moe_w_down, g_final):
    bn, S, d = x.shape
    Lc = ctx.shape[1]
    m, mc = bn * S, bn * Lc
    tpb = S // TM
    x2 = x.reshape(m, d)
    xc2 = ctx.reshape(mc, d)
    c_rows = jnp.concatenate([c, c_ctx[None, :], jnp.zeros((7, d), F32)], axis=0)
    for l in range(DEPTH):
        ctx_out = l < DEPTH - 1
        last = l == DEPTH - 1
        ada = ada_modulation(c_rows, w_ada[l].astype(BF16), b_ada[l][None, :])
        mod = ada[:bn].reshape(bn, 6, d)
        mod_c = ada[bn:bn + 1].reshape(1, 6, d)
        w_in_l = jnp.pad(w_in[l], ((0, 0), (0, D_IN_PAD - D_IN_PROJ))).astype(BF16)
        g_m = g_mix[l][None, :]
        g_f = g_ffn[l][None, :]
        pr, dt = in_proj(x2, g_m, mod, w_in_l, S // TM_PROJ, TM_PROJ)
        pr_c, dt_c = in_proj(xc2, g_m, mod_c, w_in_l, None, min(TM_PROJ, mc))
        na_bias = na_bias_table(na_rpb[l], S // GRID_W)
        o, oc = mixer(pr, dt, pr_c, dt_c, pool_w[l], pool_scale[l], conv_w[l], na_bias, ssd_conv_w[l],
                      ssd_conv_b[l], ssd_dt_bias[l], ssd_a_log[l], ssd_d[l], ssd_norm_g[l], ctx_out, bn)
        w_out_l = w_out[l].astype(BF16)
        j = l // 2
        if l % 2 == 0:
            w_gu = ffn_w_gu[j].astype(BF16)
            w_dn = ffn_w_down[j].astype(BF16)
            x2 = ffn_dense(x2, o, g_f, mod, w_out_l, w_gu, w_dn, tpb, TM)
            if ctx_out:
                xc2 = ffn_dense(xc2, oc, g_f, mod_c, w_out_l, w_gu, w_dn, None, min(TM, mc))
        else:
            w_r = jnp.pad(moe_router[j], ((0, 0), (0, LANES - N_EXPERTS))).astype(BF16)
            w_gu, w_dn = moe_w_gu[j], moe_w_down[j]
            x2 = moe_block(x2, o, g_f, mod, w_out_l, w_r, w_gu, w_dn, S // TM_ROUTE, TM_ROUTE, TM_GRP,
                           g_final[None, :] if last else None)
            if ctx_out:
                xc2 = moe_block(xc2, oc, g_f, mod_c, w_out_l, w_r, w_gu, w_dn, None, min(TM_ROUTE, mc), TM_GRP)
            if last:
                return x2.reshape(bn, S, d)
    return final_norm(x2, g_final[None, :], TM).reshape(bn, S, d)
```

```python
import functools
import math

import numpy as np
import jax
import jax.numpy as jnp
from jax import lax
from jax.experimental import pallas as pl
from jax.experimental.pallas import tpu as pltpu

DEPTH = 2
GRID_W = 64
EPS = 1e-6
W_POOL = 256
W_CONV = 256
W_NA = 256
W_SSD = 256
POOL_WINDOWS = (2, 4, 8, 16)
POOL_GROUP = W_POOL // len(POOL_WINDOWS)
NA_HEADS = 4
NA_HEAD_DIM = W_NA // NA_HEADS
WIN_R = 8
WIN_C = 16
SSD_HEAD_DIM = 64
SSD_HEADS = W_SSD // SSD_HEAD_DIM
SSD_GROUPS = 2
SSD_STATE = 128
SSD_CHUNK = 128
SSD_XBC = W_SSD + 2 * SSD_GROUPS * SSD_STATE
D_IN_PROJ = W_POOL + 3 * W_CONV + 3 * W_NA + W_SSD + SSD_XBC + 2 * SSD_HEADS
N_EXPERTS = 8
TOP_K = 2

LANES = 128
D_IN_PAD = -(-D_IN_PROJ // LANES) * LANES
VMEM_LIMIT = 56 * 1024 * 1024
BF16 = jnp.bfloat16
F32 = jnp.float32


def _params(*sem):
    return pltpu.CompilerParams(dimension_semantics=sem, vmem_limit_bytes=VMEM_LIMIT)


def _norm_mod(x, g, scale, shift):
    ms = jnp.mean(x * x, axis=-1, keepdims=True)
    return (x * lax.rsqrt(ms + EPS) * g) * (1.0 + scale) + shift


def _mod_map(tiles_per_batch):
    if tiles_per_batch is None:
        return lambda i, *_: (0, 0, 0)
    return lambda i, *_: (i // tiles_per_batch, 0, 0)


def _resident(shape, index_map):
    return pl.BlockSpec(shape, index_map, pipeline_mode=pl.Buffered(1))


def _ada_kernel(c_ref, w_ref, b_ref, o_ref):
    c = c_ref[...]
    s = c * jax.nn.sigmoid(c)
    o_ref[...] = jnp.dot(s.astype(BF16), w_ref[...], preferred_element_type=F32) + b_ref[...]


def ada_modulation(c_rows, w, b):
    r, d = c_rows.shape
    n = w.shape[1]
    tn = 1536
    return pl.pallas_call(
        _ada_kernel,
        out_shape=jax.ShapeDtypeStruct((r, n), F32),
        grid=(n // tn,),
        in_specs=[pl.BlockSpec((r, d), lambda j: (0, 0)),
                  pl.BlockSpec((d, tn), lambda j: (0, j)),
                  pl.BlockSpec((1, tn), lambda j: (0, j))],
        out_specs=pl.BlockSpec((r, tn), lambda j: (0, j)),
        compiler_params=_params("arbitrary"),
        name="ada_modulation",
    )(c_rows, w, b)


def _in_proj_kernel(x_ref, g_ref, mod_ref, w_ref, o_ref, dt_ref):
    h = _norm_mod(x_ref[...], g_ref[...], mod_ref[0, 1:2, :], mod_ref[0, 0:1, :])
    pr = jnp.dot(h.astype(BF16), w_ref[...], preferred_element_type=F32)
    o_ref[...] = pr.astype(o_ref.dtype)
    dt_ref[...] = pr[:, DT_COL * LANES:(DT_COL + 1) * LANES]


def in_proj(x2, g, mod, w, tiles_per_batch, tm):
    m, d = x2.shape
    n = w.shape[1]
    return pl.pallas_call(
        _in_proj_kernel,
        out_shape=(jax.ShapeDtypeStruct((m, n), BF16), jax.ShapeDtypeStruct((m, LANES), F32)),
        grid=(m // tm,),
        in_specs=[pl.BlockSpec((tm, d), lambda i: (i, 0)),
                  pl.BlockSpec((1, d), lambda i: (0, 0)),
                  pl.BlockSpec((1, 6, d), _mod_map(tiles_per_batch)),
                  _resident((d, n), lambda i: (0, 0))],
        out_specs=(pl.BlockSpec((tm, n), lambda i: (i, 0)), pl.BlockSpec((tm, LANES), lambda i: (i, 0))),
        compiler_params=_params("parallel"),
        name="in_proj",
    )(x2, g, mod, w)


def _mix_residual(x, mod_ref, w_ref, part_refs):
    y, k0 = None, 0
    for p_ref in part_refs:
        k = p_ref.shape[1]
        t = jnp.dot(p_ref[...], w_ref[k0:k0 + k, :], preferred_element_type=F32)
        y = t if y is None else y + t
        k0 += k
    return x + mod_ref[0, 2:3, :] * y


def _part_specs(parts, tm):
    return [pl.BlockSpec((tm, p.shape[1]), lambda i, *_: (i, 0)) for p in parts]


FF_CHUNK = 512


def _swiglu(h, wgu, wd, ff):
    acc = None
    for c0 in range(0, ff, FF_CHUNK):
        c1 = min(c0 + FF_CHUNK, ff)
        gg = jnp.dot(h, wgu(c0, c1), preferred_element_type=F32)
        uu = jnp.dot(h, wgu(ff + c0, ff + c1), preferred_element_type=F32)
        a = (gg * jax.nn.sigmoid(gg) * uu).astype(BF16)
        part = jnp.dot(a, wd(c0, c1), preferred_element_type=F32)
        acc = part if acc is None else acc + part
    return acc


def _ffn_kernel(x_ref, g_ref, mod_ref, wout_ref, wgu_ref, wd_ref, *refs):
    x = _mix_residual(x_ref[...], mod_ref, wout_ref, refs[:-1])
    y_ref = refs[-1]
    h = _norm_mod(x, g_ref[...], mod_ref[0, 4:5, :], mod_ref[0, 3:4, :]).astype(BF16)
    y = _swiglu(h, lambda a, b: wgu_ref[:, a:b], lambda a, b: wd_ref[a:b, :], wd_ref.shape[0])
    y_ref[...] = x + mod_ref[0, 5:6, :] * y


def ffn_dense(x2, parts, g, mod, w_out, w_gu, w_down, tiles_per_batch, tm):
    m, d = x2.shape
    return pl.pallas_call(
        _ffn_kernel,
        out_shape=jax.ShapeDtypeStruct((m, d), F32),
        grid=(m // tm,),
        in_specs=[pl.BlockSpec((tm, d), lambda i: (i, 0)),
                  pl.BlockSpec((1, d), lambda i: (0, 0)),
                  pl.BlockSpec((1, 6, d), _mod_map(tiles_per_batch)),
                  _resident(w_out.shape, lambda i: (0, 0)),
                  _resident(w_gu.shape, lambda i: (0, 0)),
                  _resident(w_down.shape, lambda i: (0, 0))] + _part_specs(parts, tm),
        out_specs=pl.BlockSpec((tm, d), lambda i: (i, 0)),
        compiler_params=_params("parallel"),
        name="ffn_dense",
    )(x2, g, mod, w_out, w_gu, w_down, *parts)


ROUTE_E1, ROUTE_E2, ROUTE_R1, ROUTE_R2, ROUTE_G1, ROUTE_G2 = range(6)
ROUTE_ROWS = 8


def _router_kernel(x_ref, g_ref, mod_ref, wout_ref, wr_ref, *refs):
    part_refs = refs[:-6]
    x1_ref, h_ref, route_ref, route_t_ref, cnt_ref, base_ref = refs[-6:]

    @pl.when(pl.program_id(0) == 0)
    def _():
        base_ref[...] = jnp.zeros_like(base_ref)

    x = _mix_residual(x_ref[...], mod_ref, wout_ref, part_refs)
    x1_ref[...] = x
    h = _norm_mod(x, g_ref[...], mod_ref[0, 4:5, :], mod_ref[0, 3:4, :]).astype(BF16)
    h_ref[...] = h
    tm = h.shape[0]
    logits = jnp.dot(h, wr_ref[...], preferred_element_type=F32)
    lane = lax.broadcasted_iota(jnp.int32, logits.shape, 1)
    neg = jnp.float32(-jnp.inf)
    logits = jnp.where(lane < N_EXPERTS, logits, neg)
    m1 = jnp.max(logits, axis=-1, keepdims=True)
    i1 = jnp.min(jnp.where(logits == m1, lane, LANES), axis=-1, keepdims=True)
    rest = jnp.where(lane == i1, neg, logits)
    m2 = jnp.max(rest, axis=-1, keepdims=True)
    i2 = jnp.min(jnp.where(rest == m2, lane, LANES), axis=-1, keepdims=True)
    e2 = jnp.exp(m2 - m1)
    g1 = 1.0 / (1.0 + e2)
    g2 = e2 / (1.0 + e2)
    sel1, sel2 = lane == i1, lane == i2
    sel = jnp.where(sel1 | sel2, 1.0, 0.0)
    li = lax.broadcasted_iota(jnp.int32, (tm, tm), 0)
    si = lax.broadcasted_iota(jnp.int32, (tm, tm), 1)
    before = jnp.where(si < li, 1.0, 0.0).astype(BF16)
    rank = jnp.dot(before, sel.astype(BF16), preferred_element_type=F32) + base_ref[...]
    r1 = jnp.sum(jnp.where(sel1, rank, 0.0), axis=-1, keepdims=True)
    r2 = jnp.sum(jnp.where(sel2, rank, 0.0), axis=-1, keepdims=True)
    base_ref[...] += jnp.sum(sel, axis=0, keepdims=True)
    cnt_ref[...] = base_ref[...]
    fields = (i1.astype(F32), i2.astype(F32), r1, r2, g1, g2)
    route = jnp.zeros(logits.shape, F32)
    for k, v in enumerate(fields):
        route = jnp.where(lane == k, v, route)
    route_ref[...] = route
    route_t_ref[...] = route.T[:ROUTE_ROWS, :]


def moe_router(x2, parts, g, mod, w_out, w_router, tiles_per_batch, tm):
    m, d = x2.shape
    return pl.pallas_call(
        _router_kernel,
        out_shape=(jax.ShapeDtypeStruct((m, d), F32), jax.ShapeDtypeStruct((m, d), BF16),
                   jax.ShapeDtypeStruct((m, LANES), F32), jax.ShapeDtypeStruct((ROUTE_ROWS, m), F32),
                   jax.ShapeDtypeStruct((1, LANES), F32)),
        grid=(m // tm,),
        in_specs=[pl.BlockSpec((tm, d), lambda i: (i, 0)),
                  pl.BlockSpec((1, d), lambda i: (0, 0)),
                  pl.BlockSpec((1, 6, d), _mod_map(tiles_per_batch)),
                  _resident(w_out.shape, lambda i: (0, 0)),
                  pl.BlockSpec((d, LANES), lambda i: (0, 0))] + _part_specs(parts, tm),
        out_specs=(pl.BlockSpec((tm, d), lambda i: (i, 0)),
                   pl.BlockSpec((tm, d), lambda i: (i, 0)),
                   pl.BlockSpec((tm, LANES), lambda i: (i, 0)),
                   pl.BlockSpec((ROUTE_ROWS, tm), lambda i: (0, i)),
                   pl.BlockSpec((1, LANES), lambda i: (0, 0))),
        scratch_shapes=[pltpu.VMEM((1, LANES), F32)],
        compiler_params=_params("arbitrary"),
        name="moe_router",
    )(x2, g, mod, w_out, w_router, *parts)


def _moe_kernel(tile_ref, exp_ref, start_ref, end_ref, xg_ref, wgu_ref, wd_ref, y_ref, wgu_b, wd_b):
    w = pl.program_id(0)
    tm = xg_ref.shape[0]
    start, end = start_ref[w], end_ref[w]
    t0 = tile_ref[w] * tm

    @pl.when((w == 0) | (exp_ref[w] != exp_ref[jnp.maximum(w - 1, 0)]))
    def _():
        wgu_b[...] = wgu_ref[0].astype(BF16)
        wd_b[...] = wd_ref[0].astype(BF16)

    @pl.when(end > start)
    def _():
        y = _swiglu(xg_ref[...], lambda a, b: wgu_b[:, a:b], lambda a, b: wd_b[a:b, :], wd_b.shape[0])
        y = y.astype(y_ref.dtype)

        @pl.when(start == t0)
        def _():
            y_ref[...] = y

        @pl.when(start != t0)
        def _():
            row = t0 + lax.broadcasted_iota(jnp.int32, (tm, 1), 0)
            y_ref[...] = jnp.where(row >= start, y, y_ref[...])


def moe_grouped(item_tile, item_expert, item_start, item_end, xg, w_gu, w_down, tm):
    p, d = xg.shape
    grid_spec = pltpu.PrefetchScalarGridSpec(
        num_scalar_prefetch=4,
        grid=(item_tile.shape[0],),
        in_specs=[pl.BlockSpec((tm, d), lambda w, it, ie, s, e: (it[w], 0)),
                  _resident((1,) + w_gu.shape[1:], lambda w, it, ie, s, e: (ie[w], 0, 0)),
                  _resident((1,) + w_down.shape[1:], lambda w, it, ie, s, e: (ie[w], 0, 0))],
        out_specs=pl.BlockSpec((tm, d), lambda w, it, ie, s, e: (it[w], 0)),
        scratch_shapes=[pltpu.VMEM(w_gu.shape[1:], BF16), pltpu.VMEM(w_down.shape[1:], BF16)],
    )
    return pl.pallas_call(
        _moe_kernel,
        out_shape=jax.ShapeDtypeStruct((p, d), BF16),
        grid_spec=grid_spec,
        compiler_params=_params("arbitrary"),
        name="moe_grouped",
    )(item_tile, item_expert, item_start, item_end, xg, w_gu, w_down)


def _moe_combine_kernel(x_ref, ya_ref, yb_ref, route_ref, mod_ref, *refs):
    o_ref = refs[-1]
    r = route_ref[...]
    y = (r[:, ROUTE_G1:ROUTE_G1 + 1] * ya_ref[...].astype(F32) + r[:, ROUTE_G2:ROUTE_G2 + 1] * yb_ref[...].astype(F32))
    x = x_ref[...] + mod_ref[0, 5:6, :] * y
    if len(refs) == 2:
        ms = jnp.mean(x * x, axis=-1, keepdims=True)
        x = x * lax.rsqrt(ms + EPS) * refs[0][...]
    o_ref[...] = x


def moe_combine(x2, ya, yb, route, mod, tiles_per_batch, tm, g_final=None):
    m, d = x2.shape
    row = pl.BlockSpec((tm, d), lambda i: (i, 0))
    specs = [row, row, row, pl.BlockSpec((tm, LANES), lambda i: (i, 0)),
             pl.BlockSpec((1, 6, d), _mod_map(tiles_per_batch))]
    args = [x2, ya, yb, route, mod]
    if g_final is not None:
        specs.append(pl.BlockSpec((1, d), lambda i: (0, 0)))
        args.append(g_final)
    return pl.pallas_call(
        _moe_combine_kernel,
        out_shape=jax.ShapeDtypeStruct((m, d), F32),
        grid=(m // tm,),
        in_specs=specs,
        out_specs=row,
        compiler_params=_params("parallel"),
        name="moe_combine",
    )(*args)


def moe_block(x2, parts, g, mod, w_out, w_router, w_gu, w_down, tiles_per_batch, tm_tok, tm_grp, g_final=None):
    m, d = x2.shape
    x2, h, route, route_t, cnt = moe_router(x2, parts, g, mod, w_out, w_router, tiles_per_batch, tm_tok)
    cnt = cnt[0, :N_EXPERTS].astype(jnp.int32)
    p = m * TOP_K
    off = jnp.cumsum(cnt) - cnt
    field = lambda k: route_t[k].astype(jnp.int32)

    def row_of(e, r):
        for k in range(N_EXPERTS):
            r = r + jnp.where(e == k, off[k], 0)
        return r

    d1 = row_of(field(ROUTE_E1), field(ROUTE_R1))
    d2 = row_of(field(ROUTE_E2), field(ROUTE_R2))
    tok = jnp.arange(m, dtype=jnp.int32)
    _, src = lax.sort_key_val(jnp.concatenate([d1, d2]), jnp.concatenate([tok, tok]))
    n_row_tiles = p // tm_grp
    start = jnp.sort(jnp.concatenate([jnp.arange(n_row_tiles, dtype=jnp.int32) * tm_grp, off]))
    end = jnp.concatenate([start[1:], jnp.full((1,), p, jnp.int32)])
    item_tile = jnp.minimum(start // tm_grp, n_row_tiles - 1)
    item_expert = jnp.sum((off[None, :] <= start[:, None]).astype(jnp.int32), axis=1) - 1
    rows = lambda a, idx: a.at[idx].get(mode="promise_in_bounds")
    yg = moe_grouped(item_tile, item_expert, start, end, rows(h, src), w_gu, w_down, tm_grp)
    return moe_combine(x2, rows(yg, d1), rows(yg, d2), route, mod, tiles_per_batch, tm_tok, g_final)


def _final_norm_kernel(x_ref, g_ref, o_ref):
    x = x_ref[...]
    ms = jnp.mean(x * x, axis=-1, keepdims=True)
    o_ref[...] = x * lax.rsqrt(ms + EPS) * g_ref[...]


def final_norm(x2, g, tm):
    m, d = x2.shape
    return pl.pallas_call(
        _final_norm_kernel,
        out_shape=jax.ShapeDtypeStruct((m, d), F32),
        grid=(m // tm,),
        in_specs=[pl.BlockSpec((tm, d), lambda i: (i, 0)), pl.BlockSpec((1, d), lambda i: (0, 0))],
        out_specs=pl.BlockSpec((tm, d), lambda i: (i, 0)),
        compiler_params=_params("parallel"),
        name="final_norm",
    )(x2, g)


NA_QROWS = 4
NA_KROWS = NA_QROWS + WIN_R
Q_COL, K_COL, V_COL = 4, 5, 6


def _na_key_start(j, rows):
    return np.clip(j * NA_QROWS - WIN_R // 2, 0, rows - NA_KROWS)


def _na_bias_index(rows):
    nblk = rows // NA_QROWS
    pats = []
    for j in range(nblk):
        start = _na_key_start(j, rows)
        r = j * NA_QROWS + np.arange(NA_QROWS)
        sr = np.clip(r - WIN_R // 2, 0, rows - WIN_R)
        kr = start + np.arange(NA_KROWS)
        rvalid = (kr[None, :] >= sr[:, None]) & (kr[None, :] < sr[:, None] + WIN_R)
        ri = np.clip(kr[None, :] - r[:, None] + WIN_R - 1, 0, 2 * WIN_R - 2)
        pats.append((ri, rvalid))
    for j in range(2, nblk - 1):
        assert all(np.array_equal(a, b) for a, b in zip(pats[1], pats[j]))
    sel = [pats[0], pats[1], pats[nblk - 1]]
    ri = np.stack([p[0] for p in sel])
    rvalid = np.stack([p[1] for p in sel])
    c = np.arange(GRID_W)
    sc = np.clip(c - WIN_C // 2, 0, GRID_W - WIN_C)
    cvalid = (c[None, :] >= sc[:, None]) & (c[None, :] < sc[:, None] + WIN_C)
    ci = np.clip(c[None, :] - c[:, None] + WIN_C - 1, 0, 2 * WIN_C - 2)
    c_onehot = (ci[..., None] == np.arange(2 * WIN_C - 1)).astype(np.float32)
    valid = rvalid[:, :, None, :, None] & cvalid[None, None, :, None, :]
    return ri, c_onehot, valid


def na_bias_table(rpb, rows):
    ri, c_onehot, valid = _na_bias_index(rows)
    toep = jnp.einsum('hrd,qkd->hrqk', rpb, c_onehot, precision=lax.Precision.HIGHEST)
    b = jnp.take(toep, ri.reshape(-1), axis=1).reshape((NA_HEADS,) + ri.shape + (GRID_W, GRID_W))
    b = jnp.transpose(b, (1, 0, 2, 4, 3, 5))
    b = jnp.where(valid[:, None], b, -jnp.inf)
    return b.reshape(3, NA_HEADS, NA_QROWS * GRID_W, NA_KROWS * GRID_W).astype(F32)


def _attend_heads(q, key_sets, bias_fn):
    n, w = q.shape
    lane = lax.broadcasted_iota(jnp.int32, (1, w), 1)
    head_masks = [(lane >= h * NA_HEAD_DIM) & (lane < (h + 1) * NA_HEAD_DIM) for h in range(NA_HEADS)]
    all_scores = []
    for h, mh in enumerate(head_masks):
        qh = jnp.where(mh, q, 0.0).astype(BF16)
        scores = []
        for i, (k, _) in enumerate(key_sets):
            s = lax.dot_general(qh, k, (((1,), (1,)), ((), ())), preferred_element_type=F32)
            b = bias_fn(i, h)
            scores.append(s if b is None else s + b)
        all_scores.append(scores)
    all_probs = []
    for scores in all_scores:
        m = scores[0].max(axis=-1, keepdims=True)
        for s in scores[1:]:
            m = jnp.maximum(m, s.max(axis=-1, keepdims=True))
        denom = jnp.zeros((n, 1), F32)
        probs = []
        for s in scores:
            p = jnp.exp(s - m)
            denom = denom + p.sum(axis=-1, keepdims=True)
            probs.append(p.astype(BF16))
        all_probs.append((probs, denom))
    out = jnp.zeros((n, w), F32)
    for mh, (probs, denom) in zip(head_masks, all_probs):
        acc = jnp.zeros((n, w), F32)
        for p, (_, v) in zip(probs, key_sets):
            acc = acc + jnp.dot(p, v, preferred_element_type=F32)
        out = out + jnp.where(mh, acc / denom, 0.0)
    return out


def _na_kernel(q_ref, k_ref, v_ref, kc_ref, vc_ref, bias_ref, o_ref, *, rows):
    j = pl.program_id(1)
    start = jnp.clip(j * NA_QROWS - WIN_R // 2, 0, rows - NA_KROWS)
    t0 = pl.multiple_of(start * GRID_W, GRID_W)
    nk = NA_KROWS * GRID_W
    kw = k_ref[pl.ds(t0, nk), :]
    vw = v_ref[pl.ds(t0, nk), :]
    q = q_ref[...] * (1.0 / math.sqrt(NA_HEAD_DIM))
    o = _attend_heads(q, [(kw, vw), (kc_ref[...], vc_ref[...])], lambda i, h: bias_ref[0, h] if i == 0 else None)
    o_ref[...] = o.astype(o_ref.dtype)


def na_attention(pr, pr_c, bias, bn):
    S, Lc = pr.shape[0] // bn, pr_c.shape[0] // bn
    rows = S // GRID_W
    nblk = rows // NA_QROWS
    nq = NA_QROWS * GRID_W

    def pat(b, j):
        return (jnp.where(j == 0, 0, jnp.where(j == nblk - 1, 2, 1)), 0, 0, 0)

    return pl.pallas_call(
        functools.partial(_na_kernel, rows=rows),
        out_shape=jax.ShapeDtypeStruct((bn * S, W_NA), BF16),
        grid=(bn, nblk),
        in_specs=[pl.BlockSpec((nq, W_NA), lambda b, j: (b * nblk + j, Q_COL)),
                  pl.BlockSpec((S, W_NA), lambda b, j: (b, K_COL)),
                  pl.BlockSpec((S, W_NA), lambda b, j: (b, V_COL)),
                  pl.BlockSpec((Lc, W_NA), lambda b, j: (b, K_COL)),
                  pl.BlockSpec((Lc, W_NA), lambda b, j: (b, V_COL)),
                  pl.BlockSpec((1,) + bias.shape[1:], pat)],
        out_specs=pl.BlockSpec((nq, W_NA), lambda b, j: (b * nblk + j, 0)),
        compiler_params=_params("parallel", "arbitrary"),
        name="na_attention",
    )(pr, pr, pr, pr_c, pr_c, bias)


def _ctx_attn_kernel(q_ref, k_ref, v_ref, o_ref):
    q = q_ref[...] * (1.0 / math.sqrt(NA_HEAD_DIM))
    o = _attend_heads(q, [(k_ref[...], v_ref[...])], lambda i, h: None)
    o_ref[...] = o.astype(o_ref.dtype)


def ctx_attention(pr_c, bn):
    Lc = pr_c.shape[0] // bn
    return pl.pallas_call(
        _ctx_attn_kernel,
        out_shape=jax.ShapeDtypeStruct((bn * Lc, W_NA), BF16),
        grid=(bn,),
        in_specs=[pl.BlockSpec((Lc, W_NA), lambda b: (b, Q_COL)),
                  pl.BlockSpec((Lc, W_NA), lambda b: (b, K_COL)),
                  pl.BlockSpec((Lc, W_NA), lambda b: (b, V_COL))],
        out_specs=pl.BlockSpec((Lc, W_NA), lambda b: (b, 0)),
        compiler_params=_params("parallel"),
        name="ctx_attention",
    )(pr_c, pr_c, pr_c)


POOL_COL, CONV_H_COL, CONV_B_COL, CONV_C_COL = 0, 1, 2, 3


def _shift_rows(x, k, row):
    n = x.shape[0]
    y = pltpu.roll(x, k % n, 0)
    return jnp.where(row < k, 0.0, y) if k > 0 else jnp.where(row >= n + k, 0.0, y)


def _local_mix_kernel(u_ref, h_ref, bg_ref, cg_ref, pw_ref, ps_ref, cw_ref, op_ref, oc_ref, *, seq):
    row = lax.broadcasted_iota(jnp.int32, (seq, 1), 0)
    lane = lax.broadcasted_iota(jnp.int32, (1, W_POOL), 1)
    u = u_ref[...].astype(F32)
    trailing, leading = {1: u}, {1: u}
    for w in (1, 2, 4):
        trailing[2 * w] = trailing[w] + _shift_rows(trailing[w], w, row)
        leading[2 * w] = leading[w] + _shift_rows(leading[w], -w, row)
    rowf = row.astype(F32)
    win_sum = jnp.zeros_like(u)
    cnt = jnp.zeros_like(u)
    for g, win in enumerate(POOL_WINDOWS):
        half = win // 2
        mg = (lane >= g * POOL_GROUP) & (lane < (g + 1) * POOL_GROUP)
        s = _shift_rows(trailing[half], 1, row) + leading[half]
        n = jnp.minimum(rowf + half, float(seq)) - jnp.maximum(rowf - half, 0.0)
        win_sum = jnp.where(mg, s, win_sum)
        cnt = jnp.where(mg, n, cnt)
    p = win_sum / cnt - u
    pooled = jnp.dot(p.astype(BF16), pw_ref[...], preferred_element_type=F32) * ps_ref[...]
    op_ref[...] = pooled.astype(op_ref.dtype)
    v = cg_ref[...].astype(F32) * h_ref[...].astype(F32)
    conv = (_shift_rows(v, 1, row) * cw_ref[0:1, :] + v * cw_ref[1:2, :] + _shift_rows(v, -1, row) * cw_ref[2:3, :])
    oc_ref[...] = (bg_ref[...].astype(F32) * conv).astype(oc_ref.dtype)


def local_mix(pr, pool_w, pool_scale, conv_w, bn):
    seq = pr.shape[0] // bn
    pw = jax.scipy.linalg.block_diag(*[pool_w[g] for g in range(len(POOL_WINDOWS))]).astype(BF16)
    blk = lambda col: pl.BlockSpec((seq, W_POOL), lambda b: (b, col))
    full = lambda a: pl.BlockSpec(a.shape, lambda b: (0,) * a.ndim)
    args = (pw, pool_scale[None, :], conv_w)
    return pl.pallas_call(
        functools.partial(_local_mix_kernel, seq=seq),
        out_shape=(jax.ShapeDtypeStruct((bn * seq, W_POOL), BF16), jax.ShapeDtypeStruct((bn * seq, W_CONV), BF16)),
        grid=(bn,),
        in_specs=[blk(POOL_COL), blk(CONV_H_COL), blk(CONV_B_COL), blk(CONV_C_COL)] + [full(a) for a in args],
        out_specs=(pl.BlockSpec((seq, W_POOL), lambda b: (b, 0)),
                   pl.BlockSpec((seq, W_CONV), lambda b: (b, 0))),
        compiler_params=_params("parallel"),
        name="local_mix",
    )(pr, pr, pr, pr, *args)


Z_COL, XS_COL, BS_COL, CS_COL = 7, 8, 9, 10
DT_COL = 22
HEADS_PER_GROUP = SSD_HEADS // SSD_GROUPS
GROUP_W = HEADS_PER_GROUP * SSD_HEAD_DIM


def _head_mask(h):
    lane = lax.broadcasted_iota(jnp.int32, (1, W_SSD), 1)
    return (lane >= h * SSD_HEAD_DIM) & (lane < (h + 1) * SSD_HEAD_DIM)


def _expand_heads(v, d):
    out = jnp.zeros((v.shape[0], W_SSD), F32)
    for h in range(SSD_HEADS):
        k = d * SSD_HEADS + h
        out = jnp.where(_head_mask(h), v[:, k:k + 1], out)
    return out


def _ssd_chunk_terms(c, xs_ref, b_ref, c_ref, dt_ref, la_ref, dsk_ref, y_ref, upd_ref, dec_ref, need_y):
    T = SSD_CHUNK
    r0 = pl.multiple_of(c * T, T)
    xs = xs_ref[pl.ds(r0, T), :]
    bm = b_ref[pl.ds(r0, T), :]
    cm = c_ref[pl.ds(r0, T), :]
    dt = dt_ref[pl.ds(r0, T), :]
    la = la_ref[pl.ds(r0, T), :]
    li = lax.broadcasted_iota(jnp.int32, (T, T), 0)
    si = lax.broadcasted_iota(jnp.int32, (T, T), 1)
    causal = si <= li
    prefix = jnp.dot(causal.astype(F32), la, precision=lax.Precision.HIGHEST, preferred_element_type=F32)
    total = prefix[T - 1:T, :]
    lane = lax.broadcasted_iota(jnp.int32, (1, LANES), 1)
    acum = jnp.where(lane < SSD_HEADS, prefix, total - prefix + la)
    bt = bm.astype(F32).T.astype(BF16)
    if need_y:
        acum_t = acum.T
        cb = [lax.dot_general(cm[:, g * SSD_STATE:(g + 1) * SSD_STATE], bm[:, g * SSD_STATE:(g + 1) * SSD_STATE],
                              (((1,), (1,)), ((), ())), preferred_element_type=F32) for g in range(SSD_GROUPS)]
        y = jnp.zeros((T, W_SSD), F32)
    per_dir = []
    for d in range(2):
        acum_e = _expand_heads(acum, d)
        tot_e = _expand_heads(total, d)
        xdt = xs * _expand_heads(dt, d)
        xw = (xdt * jnp.exp(tot_e - acum_e)).astype(BF16)
        dec_ref[c, d, 0:T, :] = jnp.exp(acum_e)
        dec_ref[c, d, T:T + 1, :] = jnp.exp(tot_e)
        scores = []
        if need_y:
            mask = causal if d == 0 else (si >= li)
            for h in range(SSD_HEADS):
                k = d * SSD_HEADS + h
                decay = jnp.exp(jnp.where(mask, acum[:, k:k + 1] - acum_t[k:k + 1, :], -jnp.inf))
                scores.append((cb[h // HEADS_PER_GROUP] * decay).astype(BF16))
        per_dir.append((xw, xdt, scores))
    for d, (xw, xdt, scores) in enumerate(per_dir):
        upd = [jnp.dot(bt[g * SSD_STATE:(g + 1) * SSD_STATE, :], xw[:, g * GROUP_W:(g + 1) * GROUP_W],
                       preferred_element_type=F32) for g in range(SSD_GROUPS)]
        upd_ref[c, d] = jnp.concatenate(upd, axis=1)
        if need_y:
            x_heads = [jnp.where(_head_mask(h), xdt, 0.0).astype(BF16) for h in range(SSD_HEADS)]
            y = y + jnp.dot(jnp.concatenate(scores, axis=1), jnp.concatenate(x_heads, axis=0),
                            preferred_element_type=F32)
    if need_y:
        y_ref[pl.ds(r0, T), :] = xs * dsk_ref[...] + y


def _ssd_chunk_state(c, d, c_ref, y_ref, upd_ref, dec_ref, st_ref, need_y):
    T = SSD_CHUNK
    r0 = pl.multiple_of(c * T, T)
    st = st_ref[d]
    if need_y:
        cm = c_ref[pl.ds(r0, T), :]
        st_b = st.astype(BF16)
        ys = [jnp.dot(cm[:, g * SSD_STATE:(g + 1) * SSD_STATE], st_b[:, g * GROUP_W:(g + 1) * GROUP_W],
                      preferred_element_type=F32) for g in range(SSD_GROUPS)]
        y_ref[pl.ds(r0, T), :] += jnp.concatenate(ys, axis=1) * dec_ref[c, d, 0:T, :]
    st_ref[d] = dec_ref[c, d, T:T + 1, :] * st + upd_ref[c, d]


def _ssd_kernel(z_ref, xs_in, bs_in, cs_in, dtr_ref, cw_ref, cb_ref, dtb_ref, alog_ref, dsk_ref, ng_ref, h0_ref,
                *refs, seq, need_y):
    if need_y:
        o_ref, hT_ref, xs_ref, b_ref, c_ref, dt_ref, la_ref, upd_ref, dec_ref, st_ref, y_ref = refs
    else:
        hT_ref, xs_ref, b_ref, c_ref, dt_ref, la_ref, upd_ref, dec_ref, st_ref = refs
        y_ref = None
    nc = seq // SSD_CHUNK
    row = lax.broadcasted_iota(jnp.int32, (seq, 1), 0)
    for gi, (src, dst) in enumerate(((xs_in, xs_ref), (bs_in, b_ref), (cs_in, c_ref))):
        sl = slice(gi * W_SSD, (gi + 1) * W_SSD)
        x = src[...].astype(F32)
        cv = (_shift_rows(x, 1, row) * cw_ref[0:1, sl] + x * cw_ref[1:2, sl]
              + _shift_rows(x, -1, row) * cw_ref[2:3, sl] + cb_ref[:, sl])
        dst[...] = (cv * jax.nn.sigmoid(cv)).astype(dst.dtype)
    dtv = dtr_ref[...] + dtb_ref[...]
    dt = jnp.maximum(dtv, 0.0) + jnp.log1p(jnp.exp(-jnp.abs(dtv)))
    dt_ref[...] = dt
    la_ref[...] = dt * (-jnp.exp(alog_ref[...]))
    st_ref[...] = h0_ref[0]

    def terms(c, carry):
        _ssd_chunk_terms(c, xs_ref, b_ref, c_ref, dt_ref, la_ref, dsk_ref, y_ref, upd_ref, dec_ref, need_y)
        return carry

    lax.fori_loop(0, nc, terms, 0, unroll=min(8, nc))

    def recur(i, carry):
        _ssd_chunk_state(i, 0, c_ref, y_ref, upd_ref, dec_ref, st_ref, need_y)
        _ssd_chunk_state(nc - 1 - i, 1, c_ref, y_ref, upd_ref, dec_ref, st_ref, need_y)
        return carry

    lax.fori_loop(0, nc, recur, 0, unroll=min(4, nc))
    hT_ref[0] = st_ref[...]
    if need_y:
        z = z_ref[...].astype(F32)
        yz = y_ref[...] * (z * jax.nn.sigmoid(z))
        ms = jnp.mean(yz * yz, axis=-1, keepdims=True)
        o_ref[...] = (yz * lax.rsqrt(ms + EPS) * ng_ref[...]).astype(o_ref.dtype)


def ssd_scan(pr, dt, h0, conv_w, conv_b, dt_bias, a_log, d_skip, norm_g, need_y):
    bn = h0.shape[0]
    seq = pr.shape[0] // bn
    nc = seq // SSD_CHUNK
    pad = LANES - 2 * SSD_HEADS
    dtb = jnp.pad(dt_bias.reshape(1, -1), ((0, 0), (0, pad)))
    alog = jnp.pad(a_log.reshape(1, -1), ((0, 0), (0, pad)))
    dsk = jnp.repeat(d_skip, SSD_HEAD_DIM)[None, :]
    blk = lambda col: pl.BlockSpec((seq, W_SSD), lambda b: (b, col))
    full = lambda a: pl.BlockSpec(a.shape, lambda b: (0,) * a.ndim)
    st_shape = (1, 2, SSD_STATE, W_SSD)
    st_spec = pl.BlockSpec(st_shape, lambda b: (b, 0, 0, 0))
    args = (conv_w, conv_b[None, :], dtb, alog, dsk, norm_g[None, :])
    out_shape = [jax.ShapeDtypeStruct((bn,) + st_shape[1:], F32)]
    out_specs = [st_spec]
    scratch = [pltpu.VMEM((seq, W_SSD), F32),
               pltpu.VMEM((seq, W_SSD), BF16), pltpu.VMEM((seq, W_SSD), BF16),
               pltpu.VMEM((seq, LANES), F32), pltpu.VMEM((seq, LANES), F32),
               pltpu.VMEM((nc, 2, SSD_STATE, W_SSD), F32),
               pltpu.VMEM((nc, 2, SSD_CHUNK + 8, W_SSD), F32),
               pltpu.VMEM(st_shape[1:], F32)]
    if need_y:
        out_shape.insert(0, jax.ShapeDtypeStruct((bn * seq, W_SSD), BF16))
        out_specs.insert(0, pl.BlockSpec((seq, W_SSD), lambda b: (b, 0)))
        scratch.append(pltpu.VMEM((seq, W_SSD), F32))
    res = pl.pallas_call(
        functools.partial(_ssd_kernel, seq=seq, need_y=need_y),
        out_shape=out_shape,
        grid=(bn,),
        in_specs=[blk(Z_COL), blk(XS_COL), blk(BS_COL), blk(CS_COL),
                  pl.BlockSpec((seq, LANES), lambda b: (b, 0))]
                 + [full(a) for a in args] + [st_spec],
        out_specs=out_specs,
        scratch_shapes=scratch,
        compiler_params=_params("parallel"),
        name="ssd_scan",
    )(pr, pr, pr, pr, dt, *args, h0)
    return (res[0], res[1]) if need_y else (None, res[0])


def ssd_mix(pr, dt, pr_c, dt_c, conv_w, conv_b, dt_bias, a_log, d_skip, norm_g, ctx_out, bn):
    h0 = jnp.zeros((bn, 2, SSD_STATE, W_SSD), F32)
    oc, hc = ssd_scan(pr_c, dt_c, h0, conv_w, conv_b, dt_bias, a_log, d_skip, norm_g, ctx_out)
    o, _ = ssd_scan(pr, dt, hc, conv_w, conv_b, dt_bias, a_log, d_skip, norm_g, True)
    return o, oc


def mixer(pr, dt, pr_c, dt_c, pool_w, pool_scale, conv_w, na_bias, s_conv_w, s_conv_b, dt_bias, a_log, d_skip,
          s_norm_g, ctx_out, bn):
    o_ssd, oc_ssd = ssd_mix(pr, dt, pr_c, dt_c, s_conv_w, s_conv_b, dt_bias, a_log, d_skip, s_norm_g, ctx_out, bn)
    o = [*local_mix(pr, pool_w, pool_scale, conv_w, bn), na_attention(pr, pr_c, na_bias, bn), o_ssd]
    if not ctx_out:
        return o, None
    oc = [*local_mix(pr_c, pool_w, pool_scale, conv_w, bn), ctx_attention(pr_c, bn), oc_ssd]
    return o, oc


TM = 512
TM_ROUTE = 1024
TM_PROJ = 1024
TM_GRP = 512


def kernel(x, c, ctx, c_ctx, w_ada, b_ada, g_mix, g_ffn, w_in, w_out, pool_w, pool_scale, conv_w, na_rpb,
           ssd_conv_w, ssd_conv_b, ssd_dt_bias, ssd_a_log, ssd_d, ssd_norm_g, ffn_w_gu, ffn_w_down,
           moe_router, moe_w_gu, moe_w_down, g_final):
    bn, S, d = x.shape
    Lc = ctx.shape[1]
    m, mc = bn * S, bn * Lc
    tpb = S // TM
    x2 = x.reshape(m, d)
    xc2 = ctx.reshape(mc, d)
    c_rows = jnp.concatenate([c, c_ctx[None, :], jnp.zeros((7, d), F32)], axis=0)
    for l in range(DEPTH):
        ctx_out = l < DEPTH - 1
        last = l == DEPTH - 1
        ada = ada_modulation(c_rows, w_ada[l].astype(BF16), b_ada[l][None, :])
        mod = ada[:bn].reshape(bn, 6, d)
        mod_c = ada[bn:bn + 1].reshape(1, 6, d)
        w_in_l = jnp.pad(w_in[l], ((0, 0), (0, D_IN_PAD - D_IN_PROJ))).astype(BF16)
        g_m = g_mix[l][None, :]
        g_f = g_ffn[l][None, :]
        pr, dt = in_proj(x2, g_m, mod, w_in_l, S // TM_PROJ, TM_PROJ)
        pr_c, dt_c = in_proj(xc2, g_m, mod_c, w_in_l, None, min(TM_PROJ, mc))
        na_bias = na_bias_table(na_rpb[l], S // GRID_W)
        o, oc = mixer(pr, dt, pr_c, dt_c, pool_w[l], pool_scale[l], conv_w[l], na_bias, ssd_conv_w[l],
                      ssd_conv_b[l], ssd_dt_bias[l], ssd_a_log[l], ssd_d[l], ssd_norm_g[l], ctx_out, bn)
        w_out_l = w_out[l].astype(BF16)
        j = l // 2
        if l % 2 == 0:
            w_gu = ffn_w_gu[j].astype(BF16)
            w_dn = ffn_w_down[j].astype(BF16)
            x2 = ffn_dense(x2, o, g_f, mod, w_out_l, w_gu, w_dn, tpb, TM)
            if ctx_out:
                xc2 = ffn_dense(xc2, oc, g_f, mod_c, w_out_l, w_gu, w_dn, None, min(TM, mc))
        else:
            w_r = jnp.pad(moe_router[j], ((0, 0), (0, LANES - N_EXPERTS))).astype(BF16)
            w_gu, w_dn = moe_w_gu[j], moe_w_down[j]
            x2 = moe_block(x2, o, g_f, mod, w_out_l, w_r, w_gu, w_dn, S // TM_ROUTE, TM_ROUTE, TM_GRP,
                           g_final[None, :] if last else None)
            if ctx_out:
                xc2 = moe_block(xc2, oc, g_f, mod_c, w_out_l, w_r, w_gu, w_dn, None, min(TM_ROUTE, mc), TM_GRP)
            if last:
                return x2.reshape(bn, S, d)
    return final_norm(x2, g_final[None, :], TM).reshape(bn, S, d)
```

```python
import functools
import math

import numpy as np
import jax
import jax.numpy as jnp
from jax import lax
from jax.experimental import pallas as pl
from jax.experimental.pallas import tpu as pltpu

DEPTH = 2
GRID_W = 64
EPS = 1e-6
W_POOL = 256
W_CONV = 256
W_NA = 256
W_SSD = 256
POOL_WINDOWS = (2, 4, 8, 16)
POOL_GROUP = W_POOL // len(POOL_WINDOWS)
NA_HEADS = 4
NA_HEAD_DIM = W_NA // NA_HEADS
WIN_R = 8
WIN_C = 16
SSD_HEAD_DIM = 64
SSD_HEADS = W_SSD // SSD_HEAD_DIM
SSD_GROUPS = 2
SSD_STATE = 128
SSD_CHUNK = 128
SSD_XBC = W_SSD + 2 * SSD_GROUPS * SSD_STATE
D_IN_PROJ = W_POOL + 3 * W_CONV + 3 * W_NA + W_SSD + SSD_XBC + 2 * SSD_HEADS
N_EXPERTS = 8
TOP_K = 2

LANES = 128
D_IN_PAD = -(-D_IN_PROJ // LANES) * LANES
VMEM_LIMIT = 56 * 1024 * 1024
BF16 = jnp.bfloat16
F32 = jnp.float32


def _params(*sem):
    return pltpu.CompilerParams(dimension_semantics=sem, vmem_limit_bytes=VMEM_LIMIT)


def _norm_mod(x, g, scale, shift):
    ms = jnp.mean(x * x, axis=-1, keepdims=True)
    return (x * lax.rsqrt(ms + EPS) * g) * (1.0 + scale) + shift


def _mod_map(tiles_per_batch):
    if tiles_per_batch is None:
        return lambda i, *_: (0, 0, 0)
    return lambda i, *_: (i // tiles_per_batch, 0, 0)


def _resident(shape, index_map):
    return pl.BlockSpec(shape, index_map, pipeline_mode=pl.Buffered(1))


def _ada_kernel(c_ref, w_ref, b_ref, o_ref):
    c = c_ref[...]
    s = c * jax.nn.sigmoid(c)
    o_ref[...] = jnp.dot(s.astype(BF16), w_ref[...], preferred_element_type=F32) + b_ref[...]


def ada_modulation(c_rows, w, b):
    r, d = c_rows.shape
    n = w.shape[1]
    tn = 1536
    return pl.pallas_call(
        _ada_kernel,
        out_shape=jax.ShapeDtypeStruct((r, n), F32),
        grid=(n // tn,),
        in_specs=[pl.BlockSpec((r, d), lambda j: (0, 0)),
                  pl.BlockSpec((d, tn), lambda j: (0, j)),
                  pl.BlockSpec((1, tn), lambda j: (0, j))],
        out_specs=pl.BlockSpec((r, tn), lambda j: (0, j)),
        compiler_params=_params("arbitrary"),
        name="ada_modulation",
    )(c_rows, w, b)


def _in_proj_kernel(x_ref, g_ref, mod_ref, w_ref, o_ref, dt_ref):
    h = _norm_mod(x_ref[...], g_ref[...], mod_ref[0, 1:2, :], mod_ref[0, 0:1, :])
    pr = jnp.dot(h.astype(BF16), w_ref[...], preferred_element_type=F32)
    o_ref[...] = pr.astype(o_ref.dtype)
    dt_ref[...] = pr[:, DT_COL * LANES:(DT_COL + 1) * LANES]


def in_proj(x2, g, mod, w, tiles_per_batch, tm):
    m, d = x2.shape
    n = w.shape[1]
    return pl.pallas_call(
        _in_proj_kernel,
        out_shape=(jax.ShapeDtypeStruct((m, n), BF16), jax.ShapeDtypeStruct((m, LANES), F32)),
        grid=(m // tm,),
        in_specs=[pl.BlockSpec((tm, d), lambda i: (i, 0)),
                  pl.BlockSpec((1, d), lambda i: (0, 0)),
                  pl.BlockSpec((1, 6, d), _mod_map(tiles_per_batch)),
                  _resident((d, n), lambda i: (0, 0))],
        out_specs=(pl.BlockSpec((tm, n), lambda i: (i, 0)), pl.BlockSpec((tm, LANES), lambda i: (i, 0))),
        compiler_params=_params("parallel"),
        name="in_proj",
    )(x2, g, mod, w)


def _mix_residual(x, mod_ref, w_ref, part_refs):
    y, k0 = None, 0
    for p_ref in part_refs:
        k = p_ref.shape[1]
        t = jnp.dot(p_ref[...], w_ref[k0:k0 + k, :], preferred_element_type=F32)
        y = t if y is None else y + t
        k0 += k
    return x + mod_ref[0, 2:3, :] * y


def _part_specs(parts, tm):
    return [pl.BlockSpec((tm, p.shape[1]), lambda i, *_: (i, 0)) for p in parts]


FF_CHUNK = 512


def _swiglu(h, wgu, wd, ff):
    acc = None
    for c0 in range(0, ff, FF_CHUNK):
        c1 = min(c0 + FF_CHUNK, ff)
        gg = jnp.dot(h, wgu(c0, c1), preferred_element_type=F32)
        uu = jnp.dot(h, wgu(ff + c0, ff + c1), preferred_element_type=F32)
        a = (gg * jax.nn.sigmoid(gg) * uu).astype(BF16)
        part = jnp.dot(a, wd(c0, c1), preferred_element_type=F32)
        acc = part if acc is None else acc + part
    return acc


def _ffn_kernel(x_ref, g_ref, mod_ref, wout_ref, wgu_ref, wd_ref, *refs):
    x = _mix_residual(x_ref[...], mod_ref, wout_ref, refs[:-1])
    y_ref = refs[-1]
    h = _norm_mod(x, g_ref[...], mod_ref[0, 4:5, :], mod_ref[0, 3:4, :]).astype(BF16)
    y = _swiglu(h, lambda a, b: wgu_ref[:, a:b], lambda a, b: wd_ref[a:b, :], wd_ref.shape[0])
    y_ref[...] = x + mod_ref[0, 5:6, :] * y


def ffn_dense(x2, parts, g, mod, w_out, w_gu, w_down, tiles_per_batch, tm):
    m, d = x2.shape
    return pl.pallas_call(
        _ffn_kernel,
        out_shape=jax.ShapeDtypeStruct((m, d), F32),
        grid=(m // tm,),
        in_specs=[pl.BlockSpec((tm, d), lambda i: (i, 0)),
                  pl.BlockSpec((1, d), lambda i: (0, 0)),
                  pl.BlockSpec((1, 6, d), _mod_map(tiles_per_batch)),
                  _resident(w_out.shape, lambda i: (0, 0)),
                  _resident(w_gu.shape, lambda i: (0, 0)),
                  _resident(w_down.shape, lambda i: (0, 0))] + _part_specs(parts, tm),
        out_specs=pl.BlockSpec((tm, d), lambda i: (i, 0)),
        compiler_params=_params("parallel"),
        name="ffn_dense",
    )(x2, g, mod, w_out, w_gu, w_down, *parts)


ROUTE_E1, ROUTE_E2, ROUTE_R1, ROUTE_R2, ROUTE_G1, ROUTE_G2 = range(6)
ROUTE_ROWS = 8


def _router_kernel(x_ref, g_ref, mod_ref, wout_ref, wr_ref, *refs):
    part_refs = refs[:-6]
    x1_ref, h_ref, route_ref, route_t_ref, cnt_ref, base_ref = refs[-6:]

    @pl.when(pl.program_id(0) == 0)
    def _():
        base_ref[...] = jnp.zeros_like(base_ref)

    x = _mix_residual(x_ref[...], mod_ref, wout_ref, part_refs)
    x1_ref[...] = x
    h = _norm_mod(x, g_ref[...], mod_ref[0, 4:5, :], mod_ref[0, 3:4, :]).astype(BF16)
    h_ref[...] = h
    tm = h.shape[0]
    logits = jnp.dot(h, wr_ref[...], preferred_element_type=F32)
    lane = lax.broadcasted_iota(jnp.int32, logits.shape, 1)
    neg = jnp.float32(-jnp.inf)
    logits = jnp.where(lane < N_EXPERTS, logits, neg)
    m1 = jnp.max(logits, axis=-1, keepdims=True)
    i1 = jnp.min(jnp.where(logits == m1, lane, LANES), axis=-1, keepdims=True)
    rest = jnp.where(lane == i1, neg, logits)
    m2 = jnp.max(rest, axis=-1, keepdims=True)
    i2 = jnp.min(jnp.where(rest == m2, lane, LANES), axis=-1, keepdims=True)
    e2 = jnp.exp(m2 - m1)
    g1 = 1.0 / (1.0 + e2)
    g2 = e2 / (1.0 + e2)
    sel1, sel2 = lane == i1, lane == i2
    sel = jnp.where(sel1 | sel2, 1.0, 0.0)
    li = lax.broadcasted_iota(jnp.int32, (tm, tm), 0)
    si = lax.broadcasted_iota(jnp.int32, (tm, tm), 1)
    before = jnp.where(si < li, 1.0, 0.0).astype(BF16)
    rank = jnp.dot(before, sel.astype(BF16), preferred_element_type=F32) + base_ref[...]
    r1 = jnp.sum(jnp.where(sel1, rank, 0.0), axis=-1, keepdims=True)
    r2 = jnp.sum(jnp.where(sel2, rank, 0.0), axis=-1, keepdims=True)
    base_ref[...] += jnp.sum(sel, axis=0, keepdims=True)
    cnt_ref[...] = base_ref[...]
    fields = (i1.astype(F32), i2.astype(F32), r1, r2, g1, g2)
    route = jnp.zeros(logits.shape, F32)
    for k, v in enumerate(fields):
        route = jnp.where(lane == k, v, route)
    route_ref[...] = route
    route_t_ref[...] = route.T[:ROUTE_ROWS, :]


def moe_router(x2, parts, g, mod, w_out, w_router, tiles_per_batch, tm):
    m, d = x2.shape
    return pl.pallas_call(
        _router_kernel,
        out_shape=(jax.ShapeDtypeStruct((m, d), F32), jax.ShapeDtypeStruct((m, d), BF16),
                   jax.ShapeDtypeStruct((m, LANES), F32), jax.ShapeDtypeStruct((ROUTE_ROWS, m), F32),
                   jax.ShapeDtypeStruct((1, LANES), F32)),
        grid=(m // tm,),
        in_specs=[pl.BlockSpec((tm, d), lambda i: (i, 0)),
                  pl.BlockSpec((1, d), lambda i: (0, 0)),
                  pl.BlockSpec((1, 6, d), _mod_map(tiles_per_batch)),
                  _resident(w_out.shape, lambda i: (0, 0)),
                  pl.BlockSpec((d, LANES), lambda i: (0, 0))] + _part_specs(parts, tm),
        out_specs=(pl.BlockSpec((tm, d), lambda i: (i, 0)),
                   pl.BlockSpec((tm, d), lambda i: (i, 0)),
                   pl.BlockSpec((tm, LANES), lambda i: (i, 0)),
                   pl.BlockSpec((ROUTE_ROWS, tm), lambda i: (0, i)),
                   pl.BlockSpec((1, LANES), lambda i: (0, 0))),
        scratch_shapes=[pltpu.VMEM((1, LANES), F32)],
        compiler_params=_params("arbitrary"),
        name="moe_router",
    )(x2, g, mod, w_out, w_router, *parts)


def _moe_kernel(tile_ref, exp_ref, start_ref, end_ref, xg_ref, wgu_ref, wd_ref, y_ref, wgu_b, wd_b):
    w = pl.program_id(0)
    tm = xg_ref.shape[0]
    start, end = start_ref[w], end_ref[w]
    t0 = tile_ref[w] * tm

    @pl.when((w == 0) | (exp_ref[w] != exp_ref[jnp.maximum(w - 1, 0)]))
    def _():
        wgu_b[...] = wgu_ref[0].astype(BF16)
        wd_b[...] = wd_ref[0].astype(BF16)

    @pl.when(end > start)
    def _():
        y = _swiglu(xg_ref[...], lambda a, b: wgu_b[:, a:b], lambda a, b: wd_b[a:b, :], wd_b.shape[0])
        y = y.astype(y_ref.dtype)

        @pl.when(start == t0)
        def _():
            y_ref[...] = y

        @pl.when(start != t0)
        def _():
            row = t0 + lax.broadcasted_iota(jnp.int32, (tm, 1), 0)
            y_ref[...] = jnp.where(row >= start, y, y_ref[...])


def moe_grouped(item_tile, item_expert, item_start, item_end, xg, w_gu, w_down, tm):
    p, d = xg.shape
    grid_spec = pltpu.PrefetchScalarGridSpec(
        num_scalar_prefetch=4,
        grid=(item_tile.shape[0],),
        in_specs=[pl.BlockSpec((tm, d), lambda w, it, ie, s, e: (it[w], 0)),
                  _resident((1,) + w_gu.shape[1:], lambda w, it, ie, s, e: (ie[w], 0, 0)),
                  _resident((1,) + w_down.shape[1:], lambda w, it, ie, s, e: (ie[w], 0, 0))],
        out_specs=pl.BlockSpec((tm, d), lambda w, it, ie, s, e: (it[w], 0)),
        scratch_shapes=[pltpu.VMEM(w_gu.shape[1:], BF16), pltpu.VMEM(w_down.shape[1:], BF16)],
    )
    return pl.pallas_call(
        _moe_kernel,
        out_shape=jax.ShapeDtypeStruct((p, d), BF16),
        grid_spec=grid_spec,
        compiler_params=_params("arbitrary"),
        name="moe_grouped",
    )(item_tile, item_expert, item_start, item_end, xg, w_gu, w_down)


def _moe_combine_kernel(x_ref, ya_ref, yb_ref, route_ref, mod_ref, *refs):
    o_ref = refs[-1]
    r = route_ref[...]
    y = (r[:, ROUTE_G1:ROUTE_G1 + 1] * ya_ref[...].astype(F32) + r[:, ROUTE_G2:ROUTE_G2 + 1] * yb_ref[...].astype(F32))
    x = x_ref[...] + mod_ref[0, 5:6, :] * y
    if len(refs) == 2:
        ms = jnp.mean(x * x, axis=-1, keepdims=True)
        x = x * lax.rsqrt(ms + EPS) * refs[0][...]
    o_ref[...] = x


def moe_combine(x2, ya, yb, route, mod, tiles_per_batch, tm, g_final=None):
    m, d = x2.shape
    row = pl.BlockSpec((tm, d), lambda i: (i, 0))
    specs = [row, row, row, pl.BlockSpec((tm, LANES), lambda i: (i, 0)),
             pl.BlockSpec((1, 6, d), _mod_map(tiles_per_batch))]
    args = [x2, ya, yb, route, mod]
    if g_final is not None:
        specs.append(pl.BlockSpec((1, d), lambda i: (0, 0)))
        args.append(g_final)
    return pl.pallas_call(
        _moe_combine_kernel,
        out_shape=jax.ShapeDtypeStruct((m, d), F32),
        grid=(m // tm,),
        in_specs=specs,
        out_specs=row,
        compiler_params=_params("parallel"),
        name="moe_combine",
    )(*args)


def moe_block(x2, parts, g, mod, w_out, w_router, w_gu, w_down, tiles_per_batch, tm_tok, tm_grp, g_final=None):
    m, d = x2.shape
    x2, h, route, route_t, cnt = moe_router(x2, parts, g, mod, w_out, w_router, tiles_per_batch, tm_tok)
    cnt = cnt[0, :N_EXPERTS].astype(jnp.int32)
    p = m * TOP_K
    off = jnp.cumsum(cnt) - cnt
    field = lambda k: route_t[k].astype(jnp.int32)

    def row_of(e, r):
        for k in range(N_EXPERTS):
            r = r + jnp.where(e == k, off[k], 0)
        return r

    d1 = row_of(field(ROUTE_E1), field(ROUTE_R1))
    d2 = row_of(field(ROUTE_E2), field(ROUTE_R2))
    tok = jnp.arange(m, dtype=jnp.int32)
    _, src = lax.sort_key_val(jnp.concatenate([d1, d2]), jnp.concatenate([tok, tok]))
    n_row_tiles = p // tm_grp
    start = jnp.sort(jnp.concatenate([jnp.arange(n_row_tiles, dtype=jnp.int32) * tm_grp, off]))
    end = jnp.concatenate([start[1:], jnp.full((1,), p, jnp.int32)])
    item_tile = jnp.minimum(start // tm_grp, n_row_tiles - 1)
    item_expert = jnp.sum((off[None, :] <= start[:, None]).astype(jnp.int32), axis=1) - 1
    rows = lambda a, idx: a.at[idx].get(mode="promise_in_bounds")
    yg = moe_grouped(item_tile, item_expert, start, end, rows(h, src), w_gu, w_down, tm_grp)
    return moe_combine(x2, rows(yg, d1), rows(yg, d2), route, mod, tiles_per_batch, tm_tok, g_final)


def _final_norm_kernel(x_ref, g_ref, o_ref):
    x = x_ref[...]
    ms = jnp.mean(x * x, axis=-1, keepdims=True)
    o_ref[...] = x * lax.rsqrt(ms + EPS) * g_ref[...]


def final_norm(x2, g, tm):
    m, d = x2.shape
    return pl.pallas_call(
        _final_norm_kernel,
        out_shape=jax.ShapeDtypeStruct((m, d), F32),
        grid=(m // tm,),
        in_specs=[pl.BlockSpec((tm, d), lambda i: (i, 0)), pl.BlockSpec((1, d), lambda i: (0, 0))],
        out_specs=pl.BlockSpec((tm, d), lambda i: (i, 0)),
        compiler_params=_params("parallel"),
        name="final_norm",
    )(x2, g)


NA_QROWS = 4
NA_KROWS = NA_QROWS + WIN_R
Q_COL, K_COL, V_COL = 4, 5, 6


def _na_key_start(j, rows):
    return np.clip(j * NA_QROWS - WIN_R // 2, 0, rows - NA_KROWS)


def _na_bias_index(rows):
    nblk = rows // NA_QROWS
    pats = []
    for j in range(nblk):
        start = _na_key_start(j, rows)
        r = j * NA_QROWS + np.arange(NA_QROWS)
        sr = np.clip(r - WIN_R // 2, 0, rows - WIN_R)
        kr = start + np.arange(NA_KROWS)
        rvalid = (kr[None, :] >= sr[:, None]) & (kr[None, :] < sr[:, None] + WIN_R)
        ri = np.clip(kr[None, :] - r[:, None] + WIN_R - 1, 0, 2 * WIN_R - 2)
        pats.append((ri, rvalid))
    for j in range(2, nblk - 1):
        assert all(np.array_equal(a, b) for a, b in zip(pats[1], pats[j]))
    sel = [pats[0], pats[1], pats[nblk - 1]]
    ri = np.stack([p[0] for p in sel])
    rvalid = np.stack([p[1] for p in sel])
    c = np.arange(GRID_W)
    sc = np.clip(c - WIN_C // 2, 0, GRID_W - WIN_C)
    cvalid = (c[None, :] >= sc[:, None]) & (c[None, :] < sc[:, None] + WIN_C)
    ci = np.clip(c[None, :] - c[:, None] + WIN_C - 1, 0, 2 * WIN_C - 2)
    c_onehot = (ci[..., None] == np.arange(2 * WIN_C - 1)).astype(np.float32)
    valid = rvalid[:, :, None, :, None] & cvalid[None, None, :, None, :]
    return ri, c_onehot, valid


def na_bias_table(rpb, rows):
    ri, c_onehot, valid = _na_bias_index(rows)
    toep = jnp.einsum('hrd,qkd->hrqk', rpb, c_onehot, precision=lax.Precision.HIGHEST)
    b = jnp.take(toep, ri.reshape(-1), axis=1).reshape((NA_HEADS,) + ri.shape + (GRID_W, GRID_W))
    b = jnp.transpose(b, (1, 0, 2, 4, 3, 5))
    b = jnp.where(valid[:, None], b, -jnp.inf)
    return b.reshape(3, NA_HEADS, NA_QROWS * GRID_W, NA_KROWS * GRID_W).astype(F32)


def _attend_heads(q, key_sets, bias_fn):
    n, w = q.shape
    lane = lax.broadcasted_iota(jnp.int32, (1, w), 1)
    head_masks = [(lane >= h * NA_HEAD_DIM) & (lane < (h + 1) * NA_HEAD_DIM) for h in range(NA_HEADS)]
    all_scores = []
    for h, mh in enumerate(head_masks):
        qh = jnp.where(mh, q, 0.0).astype(BF16)
        scores = []
        for i, (k, _) in enumerate(key_sets):
            s = lax.dot_general(qh, k, (((1,), (1,)), ((), ())), preferred_element_type=F32)
            b = bias_fn(i, h)
            scores.append(s if b is None else s + b)
        all_scores.append(scores)
    all_probs = []
    for scores in all_scores:
        m = scores[0].max(axis=-1, keepdims=True)
        for s in scores[1:]:
            m = jnp.maximum(m, s.max(axis=-1, keepdims=True))
        denom = jnp.zeros((n, 1), F32)
        probs = []
        for s in scores:
            p = jnp.exp(s - m)
            denom = denom + p.sum(axis=-1, keepdims=True)
            probs.append(p.astype(BF16))
        all_probs.append((probs, denom))
    out = jnp.zeros((n, w), F32)
    for mh, (probs, denom) in zip(head_masks, all_probs):
        acc = jnp.zeros((n, w), F32)
        for p, (_, v) in zip(probs, key_sets):
            acc = acc + jnp.dot(p, v, preferred_element_type=F32)
        out = out + jnp.where(mh, acc / denom, 0.0)
    return out


def _na_kernel(q_ref, k_ref, v_ref, kc_ref, vc_ref, bias_ref, o_ref, *, rows):
    j = pl.program_id(1)
    start = jnp.clip(j * NA_QROWS - WIN_R // 2, 0, rows - NA_KROWS)
    t0 = pl.multiple_of(start * GRID_W, GRID_W)
    nk = NA_KROWS * GRID_W
    kw = k_ref[pl.ds(t0, nk), :]
    vw = v_ref[pl.ds(t0, nk), :]
    q = q_ref[...] * (1.0 / math.sqrt(NA_HEAD_DIM))
    o = _attend_heads(q, [(kw, vw), (kc_ref[...], vc_ref[...])], lambda i, h: bias_ref[0, h] if i == 0 else None)
    o_ref[...] = o.astype(o_ref.dtype)


def na_attention(pr, pr_c, bias, bn):
    S, Lc = pr.shape[0] // bn, pr_c.shape[0] // bn
    rows = S // GRID_W
    nblk = rows // NA_QROWS
    nq = NA_QROWS * GRID_W

    def pat(b, j):
        return (jnp.where(j == 0, 0, jnp.where(j == nblk - 1, 2, 1)), 0, 0, 0)

    return pl.pallas_call(
        functools.partial(_na_kernel, rows=rows),
        out_shape=jax.ShapeDtypeStruct((bn * S, W_NA), BF16),
        grid=(bn, nblk),
        in_specs=[pl.BlockSpec((nq, W_NA), lambda b, j: (b * nblk + j, Q_COL)),
                  pl.BlockSpec((S, W_NA), lambda b, j: (b, K_COL)),
                  pl.BlockSpec((S, W_NA), lambda b, j: (b, V_COL)),
                  pl.BlockSpec((Lc, W_NA), lambda b, j: (b, K_COL)),
                  pl.BlockSpec((Lc, W_NA), lambda b, j: (b, V_COL)),
                  pl.BlockSpec((1,) + bias.shape[1:], pat)],
        out_specs=pl.BlockSpec((nq, W_NA), lambda b, j: (b * nblk + j, 0)),
        compiler_params=_params("parallel", "arbitrary"),
        name="na_attention",
    )(pr, pr, pr, pr_c, pr_c, bias)


def _ctx_attn_kernel(q_ref, k_ref, v_ref, o_ref):
    q = q_ref[...] * (1.0 / math.sqrt(NA_HEAD_DIM))
    o = _attend_heads(q, [(k_ref[...], v_ref[...])], lambda i, h: None)
    o_ref[...] = o.astype(o_ref.dtype)


def ctx_attention(pr_c, bn):
    Lc = pr_c.shape[0] // bn
    return pl.pallas_call(
        _ctx_attn_kernel,
        out_shape=jax.ShapeDtypeStruct((bn * Lc, W_NA), BF16),
        grid=(bn,),
        in_specs=[pl.BlockSpec((Lc, W_NA), lambda b: (b, Q_COL)),
                  pl.BlockSpec((Lc, W_NA), lambda b: (b, K_COL)),
                  pl.BlockSpec((Lc, W_NA), lambda b: (b, V_COL))],
        out_specs=pl.BlockSpec((Lc, W_NA), lambda b: (b, 0)),
        compiler_params=_params("parallel"),
        name="ctx_attention",
    )(pr_c, pr_c, pr_c)


POOL_COL, CONV_H_COL, CONV_B_COL, CONV_C_COL = 0, 1, 2, 3


def _shift_rows(x, k, row):
    n = x.shape[0]
    y = pltpu.roll(x, k % n, 0)
    return jnp.where(row < k, 0.0, y) if k > 0 else jnp.where(row >= n + k, 0.0, y)


def _local_mix_kernel(u_ref, h_ref, bg_ref, cg_ref, pw_ref, ps_ref, cw_ref, op_ref, oc_ref, *, seq):
    row = lax.broadcasted_iota(jnp.int32, (seq, 1), 0)
    lane = lax.broadcasted_iota(jnp.int32, (1, W_POOL), 1)
    u = u_ref[...].astype(F32)
    trailing, leading = {1: u}, {1: u}
    for w in (1, 2, 4):
        trailing[2 * w] = trailing[w] + _shift_rows(trailing[w], w, row)
        leading[2 * w] = leading[w] + _shift_rows(leading[w], -w, row)
    rowf = row.astype(F32)
    win_sum = jnp.zeros_like(u)
    cnt = jnp.zeros_like(u)
    for g, win in enumerate(POOL_WINDOWS):
        half = win // 2
        mg = (lane >= g * POOL_GROUP) & (lane < (g + 1) * POOL_GROUP)
        s = _shift_rows(trailing[half], 1, row) + leading[half]
        n = jnp.minimum(rowf + half, float(seq)) - jnp.maximum(rowf - half, 0.0)
        win_sum = jnp.where(mg, s, win_sum)
        cnt = jnp.where(mg, n, cnt)
    p = win_sum / cnt - u
    pooled = jnp.dot(p.astype(BF16), pw_ref[...], preferred_element_type=F32) * ps_ref[...]
    op_ref[...] = pooled.astype(op_ref.dtype)
    v = cg_ref[...].astype(F32) * h_ref[...].astype(F32)
    conv = (_shift_rows(v, 1, row) * cw_ref[0:1, :] + v * cw_ref[1:2, :] + _shift_rows(v, -1, row) * cw_ref[2:3, :])
    oc_ref[...] = (bg_ref[...].astype(F32) * conv).astype(oc_ref.dtype)


def local_mix(pr, pool_w, pool_scale, conv_w, bn):
    seq = pr.shape[0] // bn
    pw = jax.scipy.linalg.block_diag(*[pool_w[g] for g in range(len(POOL_WINDOWS))]).astype(BF16)
    blk = lambda col: pl.BlockSpec((seq, W_POOL), lambda b: (b, col))
    full = lambda a: pl.BlockSpec(a.shape, lambda b: (0,) * a.ndim)
    args = (pw, pool_scale[None, :], conv_w)
    return pl.pallas_call(
        functools.partial(_local_mix_kernel, seq=seq),
        out_shape=(jax.ShapeDtypeStruct((bn * seq, W_POOL), BF16), jax.ShapeDtypeStruct((bn * seq, W_CONV), BF16)),
        grid=(bn,),
        in_specs=[blk(POOL_COL), blk(CONV_H_COL), blk(CONV_B_COL), blk(CONV_C_COL)] + [full(a) for a in args],
        out_specs=(pl.BlockSpec((seq, W_POOL), lambda b: (b, 0)),
                   pl.BlockSpec((seq, W_CONV), lambda b: (b, 0))),
        compiler_params=_params("parallel"),
        name="local_mix",
    )(pr, pr, pr, pr, *args)


Z_COL, XS_COL, BS_COL, CS_COL = 7, 8, 9, 10
DT_COL = 22
HEADS_PER_GROUP = SSD_HEADS // SSD_GROUPS
GROUP_W = HEADS_PER_GROUP * SSD_HEAD_DIM


def _head_mask(h):
    lane = lax.broadcasted_iota(jnp.int32, (1, W_SSD), 1)
    return (lane >= h * SSD_HEAD_DIM) & (lane < (h + 1) * SSD_HEAD_DIM)


def _expand_heads(v, d):
    out = jnp.zeros((v.shape[0], W_SSD), F32)
    for h in range(SSD_HEADS):
        k = d * SSD_HEADS + h
        out = jnp.where(_head_mask(h), v[:, k:k + 1], out)
    return out


def _ssd_chunk_terms(c, xs_ref, b_ref, c_ref, dt_ref, la_ref, dsk_ref, y_ref, upd_ref, dec_ref, need_y):
    T = SSD_CHUNK
    r0 = pl.multiple_of(c * T, T)
    xs = xs_ref[pl.ds(r0, T), :]
    bm = b_ref[pl.ds(r0, T), :]
    cm = c_ref[pl.ds(r0, T), :]
    dt = dt_ref[pl.ds(r0, T), :]
    la = la_ref[pl.ds(r0, T), :]
    li = lax.broadcasted_iota(jnp.int32, (T, T), 0)
    si = lax.broadcasted_iota(jnp.int32, (T, T), 1)
    causal = si <= li
    prefix = jnp.dot(causal.astype(F32), la, precision=lax.Precision.HIGHEST, preferred_element_type=F32)
    total = prefix[T - 1:T, :]
    lane = lax.broadcasted_iota(jnp.int32, (1, LANES), 1)
    acum = jnp.where(lane < SSD_HEADS, prefix, total - prefix + la)
    bt = bm.astype(F32).T.astype(BF16)
    if need_y:
        acum_t = acum.T
        cb = [lax.dot_general(cm[:, g * SSD_STATE:(g + 1) * SSD_STATE], bm[:, g * SSD_STATE:(g + 1) * SSD_STATE],
                              (((1,), (1,)), ((), ())), preferred_element_type=F32) for g in range(SSD_GROUPS)]
        y = jnp.zeros((T, W_SSD), F32)
    per_dir = []
    for d in range(2):
        acum_e = _expand_heads(acum, d)
        tot_e = _expand_heads(total, d)
        xdt = xs * _expand_heads(dt, d)
        xw = (xdt * jnp.exp(tot_e - acum_e)).astype(BF16)
        dec_ref[c, d, 0:T, :] = jnp.exp(acum_e)
        dec_ref[c, d, T:T + 1, :] = jnp.exp(tot_e)
        scores = []
        if need_y:
            mask = causal if d == 0 else (si >= li)
            for h in range(SSD_HEADS):
                k = d * SSD_HEADS + h
                decay = jnp.exp(jnp.where(mask, acum[:, k:k + 1] - acum_t[k:k + 1, :], -jnp.inf))
                scores.append((cb[h // HEADS_PER_GROUP] * decay).astype(BF16))
        per_dir.append((xw, xdt, scores))
    for d, (xw, xdt, scores) in enumerate(per_dir):
        upd = [jnp.dot(bt[g * SSD_STATE:(g + 1) * SSD_STATE, :], xw[:, g * GROUP_W:(g + 1) * GROUP_W],
                       preferred_element_type=F32) for g in range(SSD_GROUPS)]
        upd_ref[c, d] = jnp.concatenate(upd, axis=1)
        if need_y:
            x_heads = [jnp.where(_head_mask(h), xdt, 0.0).astype(BF16) for h in range(SSD_HEADS)]
            y = y + jnp.dot(jnp.concatenate(scores, axis=1), jnp.concatenate(x_heads, axis=0),
                            preferred_element_type=F32)
    if need_y:
        y_ref[pl.ds(r0, T), :] = xs * dsk_ref[...] + y


def _ssd_chunk_state(c, d, c_ref, y_ref, upd_ref, dec_ref, st_ref, need_y):
    T = SSD_CHUNK
    r0 = pl.multiple_of(c * T, T)
    st = st_ref[d]
    if need_y:
        cm = c_ref[pl.ds(r0, T), :]
        st_b = st.astype(BF16)
        ys = [jnp.dot(cm[:, g * SSD_STATE:(g + 1) * SSD_STATE], st_b[:, g * GROUP_W:(g + 1) * GROUP_W],
                      preferred_element_type=F32) for g in range(SSD_GROUPS)]
        y_ref[pl.ds(r0, T), :] += jnp.concatenate(ys, axis=1) * dec_ref[c, d, 0:T, :]
    st_ref[d] = dec_ref[c, d, T:T + 1, :] * st + upd_ref[c, d]


def _ssd_kernel(z_ref, xs_in, bs_in, cs_in, dtr_ref, cw_ref, cb_ref, dtb_ref, alog_ref, dsk_ref, ng_ref, h0_ref,
                *refs, seq, need_y):
    if need_y:
        o_ref, hT_ref, xs_ref, b_ref, c_ref, dt_ref, la_ref, upd_ref, dec_ref, st_ref, y_ref = refs
    else:
        hT_ref, xs_ref, b_ref, c_ref, dt_ref, la_ref, upd_ref, dec_ref, st_ref = refs
        y_ref = None
    nc = seq // SSD_CHUNK
    row = lax.broadcasted_iota(jnp.int32, (seq, 1), 0)
    for gi, (src, dst) in enumerate(((xs_in, xs_ref), (bs_in, b_ref), (cs_in, c_ref))):
        sl = slice(gi * W_SSD, (gi + 1) * W_SSD)
        x = src[...].astype(F32)
        cv = (_shift_rows(x, 1, row) * cw_ref[0:1, sl] + x * cw_ref[1:2, sl]
              + _shift_rows(x, -1, row) * cw_ref[2:3, sl] + cb_ref[:, sl])
        dst[...] = (cv * jax.nn.sigmoid(cv)).astype(dst.dtype)
    dtv = dtr_ref[...] + dtb_ref[...]
    dt = jnp.maximum(dtv, 0.0) + jnp.log1p(jnp.exp(-jnp.abs(dtv)))
    dt_ref[...] = dt
    la_ref[...] = dt * (-jnp.exp(alog_ref[...]))
    st_ref[...] = h0_ref[0]

    def terms(c, carry):
        _ssd_chunk_terms(c, xs_ref, b_ref, c_ref, dt_ref, la_ref, dsk_ref, y_ref, upd_ref, dec_ref, need_y)
        return carry

    lax.fori_loop(0, nc, terms, 0, unroll=min(16, nc))

    def recur(i, carry):
        _ssd_chunk_state(i, 0, c_ref, y_ref, upd_ref, dec_ref, st_ref, need_y)
        _ssd_chunk_state(nc - 1 - i, 1, c_ref, y_ref, upd_ref, dec_ref, st_ref, need_y)
        return carry

    lax.fori_loop(0, nc, recur, 0, unroll=min(8, nc))
    hT_ref[0] = st_ref[...]
    if need_y:
        z = z_ref[...].astype(F32)
        yz = y_ref[...] * (z * jax.nn.sigmoid(z))
        ms = jnp.mean(yz * yz, axis=-1, keepdims=True)
        o_ref[...] = (yz * lax.rsqrt(ms + EPS) * ng_ref[...]).astype(o_ref.dtype)


def ssd_scan(pr, dt, h0, conv_w, conv_b, dt_bias, a_log, d_skip, norm_g, need_y):
    bn = h0.shape[0]
    seq = pr.shape[0] // bn
    nc = seq // SSD_CHUNK
    pad = LANES - 2 * SSD_HEADS
    dtb = jnp.pad(dt_bias.reshape(1, -1), ((0, 0), (0, pad)))
    alog = jnp.pad(a_log.reshape(1, -1), ((0, 0), (0, pad)))
    dsk = jnp.repeat(d_skip, SSD_HEAD_DIM)[None, :]
    blk = lambda col: pl.BlockSpec((seq, W_SSD), lambda b: (b, col))
    full = lambda a: pl.BlockSpec(a.shape, lambda b: (0,) * a.ndim)
    st_shape = (1, 2, SSD_STATE, W_SSD)
    st_spec = pl.BlockSpec(st_shape, lambda b: (b, 0, 0, 0))
    args = (conv_w, conv_b[None, :], dtb, alog, dsk, norm_g[None, :])
    out_shape = [jax.ShapeDtypeStruct((bn,) + st_shape[1:], F32)]
    out_specs = [st_spec]
    scratch = [pltpu.VMEM((seq, W_SSD), F32),
               pltpu.VMEM((seq, W_SSD), BF16), pltpu.VMEM((seq, W_SSD), BF16),
               pltpu.VMEM((seq, LANES), F32), pltpu.VMEM((seq, LANES), F32),
               pltpu.VMEM((nc, 2, SSD_STATE, W_SSD), F32),
               pltpu.VMEM((nc, 2, SSD_CHUNK + 8, W_SSD), F32),
               pltpu.VMEM(st_shape[1:], F32)]
    if need_y:
        out_shape.insert(0, jax.ShapeDtypeStruct((bn * seq, W_SSD), BF16))
        out_specs.insert(0, pl.BlockSpec((seq, W_SSD), lambda b: (b, 0)))
        scratch.append(pltpu.VMEM((seq, W_SSD), F32))
    res = pl.pallas_call(
        functools.partial(_ssd_kernel, seq=seq, need_y=need_y),
        out_shape=out_shape,
        grid=(bn,),
        in_specs=[blk(Z_COL), blk(XS_COL), blk(BS_COL), blk(CS_COL),
                  pl.BlockSpec((seq, LANES), lambda b: (b, 0))]
                 + [full(a) for a in args] + [st_spec],
        out_specs=out_specs,
        scratch_shapes=scratch,
        compiler_params=_params("parallel"),
        name="ssd_scan",
    )(pr, pr, pr, pr, dt, *args, h0)
    return (res[0], res[1]) if need_y else (None, res[0])


def ssd_mix(pr, dt, pr_c, dt_c, conv_w, conv_b, dt_bias, a_log, d_skip, norm_g, ctx_out, bn):
    h0 = jnp.zeros((bn, 2, SSD_STATE, W_SSD), F32)
    oc, hc = ssd_scan(pr_c, dt_c, h0, conv_w, conv_b, dt_bias, a_log, d_skip, norm_g, ctx_out)
    o, _ = ssd_scan(pr, dt, hc, conv_w, conv_b, dt_bias, a_log, d_skip, norm_g, True)
    return o, oc


def mixer(pr, dt, pr_c, dt_c, pool_w, pool_scale, conv_w, na_bias, s_conv_w, s_conv_b, dt_bias, a_log, d_skip,
          s_norm_g, ctx_out, bn):
    o_ssd, oc_ssd = ssd_mix(pr, dt, pr_c, dt_c, s_conv_w, s_conv_b, dt_bias, a_log, d_skip, s_norm_g, ctx_out, bn)
    o = [*local_mix(pr, pool_w, pool_scale, conv_w, bn), na_attention(pr, pr_c, na_bias, bn), o_ssd]
    if not ctx_out:
        return o, None
    oc = [*local_mix(pr_c, pool_w, pool_scale, conv_w, bn), ctx_attention(pr_c, bn), oc_ssd]
    return o, oc


TM = 512
TM_ROUTE = 1024
TM_PROJ = 1024
TM_GRP = 512


def kernel(x, c, ctx, c_ctx, w_ada, b_ada, g_mix, g_ffn, w_in, w_out, pool_w, pool_scale, conv_w, na_rpb,
           ssd_conv_w, ssd_conv_b, ssd_dt_bias, ssd_a_log, ssd_d, ssd_norm_g, ffn_w_gu, ffn_w_down,
           moe_router, moe_w_gu, moe_w_down, g_final):
    bn, S, d = x.shape
    Lc = ctx.shape[1]
    m, mc = bn * S, bn * Lc
    tpb = S // TM
    x2 = x.reshape(m, d)
    xc2 = ctx.reshape(mc, d)
    c_rows = jnp.concatenate([c, c_ctx[None, :], jnp.zeros((7, d), F32)], axis=0)
    for l in range(DEPTH):
        ctx_out = l < DEPTH - 1
        last = l == DEPTH - 1
        ada = ada_modulation(c_rows, w_ada[l].astype(BF16), b_ada[l][None, :])
        mod = ada[:bn].reshape(bn, 6, d)
        mod_c = ada[bn:bn + 1].reshape(1, 6, d)
        w_in_l = jnp.pad(w_in[l], ((0, 0), (0, D_IN_PAD - D_IN_PROJ))).astype(BF16)
        g_m = g_mix[l][None, :]
        g_f = g_ffn[l][None, :]
        pr, dt = in_proj(x2, g_m, mod, w_in_l, S // TM_PROJ, TM_PROJ)
        pr_c, dt_c = in_proj(xc2, g_m, mod_c, w_in_l, None, min(TM_PROJ, mc))
        na_bias = na_bias_table(na_rpb[l], S // GRID_W)
        o, oc = mixer(pr, dt, pr_c, dt_c, pool_w[l], pool_scale[l], conv_w[l], na_bias, ssd_conv_w[l],
                      ssd_conv_b[l], ssd_dt_bias[l], ssd_a_log[l], ssd_d[l], ssd_norm_g[l], ctx_out, bn)
        w_out_l = w_out[l].astype(BF16)
        j = l // 2
        if l % 2 == 0:
            w_gu = ffn_w_gu[j].astype(BF16)
            w_dn = ffn_w_down[j].astype(BF16)
            x2 = ffn_dense(x2, o, g_f, mod, w_out_l, w_gu, w_dn, tpb, TM)
            if ctx_out:
                xc2 = ffn_dense(xc2, oc, g_f, mod_c, w_out_l, w_gu, w_dn, None, min(TM, mc))
        else:
            w_r = jnp.pad(moe_router[j], ((0, 0), (0, LANES - N_EXPERTS))).astype(BF16)
            w_gu, w_dn = moe_w_gu[j], moe_w_down[j]
            x2 = moe_block(x2, o, g_f, mod, w_out_l, w_r, w_gu, w_dn, S // TM_ROUTE, TM_ROUTE, TM_GRP,
                           g_final[None, :] if last else None)
            if ctx_out:
                xc2 = moe_block(xc2, oc, g_f, mod_c, w_out_l, w_r, w_gu, w_dn, None, min(TM_ROUTE, mc), TM_GRP)
            if last:
                return x2.reshape(bn, S, d)
    return final_norm(x2, g_final[None, :], TM).reshape(bn, S, d)
```

```python
import functools
import math

import numpy as np
import jax
import jax.numpy as jnp
from jax import lax
from jax.experimental import pallas as pl
from jax.experimental.pallas import tpu as pltpu

DEPTH = 2
GRID_W = 64
EPS = 1e-6
W_POOL = 256
W_CONV = 256
W_NA = 256
W_SSD = 256
POOL_WINDOWS = (2, 4, 8, 16)
POOL_GROUP = W_POOL // len(POOL_WINDOWS)
NA_HEADS = 4
NA_HEAD_DIM = W_NA // NA_HEADS
WIN_R = 8
WIN_C = 16
SSD_HEAD_DIM = 64
SSD_HEADS = W_SSD // SSD_HEAD_DIM
SSD_GROUPS = 2
SSD_STATE = 128
SSD_CHUNK = 128
SSD_XBC = W_SSD + 2 * SSD_GROUPS * SSD_STATE
D_IN_PROJ = W_POOL + 3 * W_CONV + 3 * W_NA + W_SSD + SSD_XBC + 2 * SSD_HEADS
N_EXPERTS = 8
TOP_K = 2

LANES = 128
D_IN_PAD = -(-D_IN_PROJ // LANES) * LANES
VMEM_LIMIT = 56 * 1024 * 1024
BF16 = jnp.bfloat16
F32 = jnp.float32


def _params(*sem):
    return pltpu.CompilerParams(dimension_semantics=sem, vmem_limit_bytes=VMEM_LIMIT)


def _norm_mod(x, g, scale, shift):
    ms = jnp.mean(x * x, axis=-1, keepdims=True)
    return (x * lax.rsqrt(ms + EPS) * g) * (1.0 + scale) + shift


def _mod_map(tiles_per_batch):
    if tiles_per_batch is None:
        return lambda i, *_: (0, 0, 0)
    return lambda i, *_: (i // tiles_per_batch, 0, 0)


def _resident(shape, index_map):
    return pl.BlockSpec(shape, index_map, pipeline_mode=pl.Buffered(1))


def _ada_kernel(c_ref, w_ref, b_ref, o_ref):
    c = c_ref[...]
    s = c * jax.nn.sigmoid(c)
    o_ref[...] = jnp.dot(s.astype(BF16), w_ref[...], preferred_element_type=F32) + b_ref[...]


def ada_modulation(c_rows, w, b):
    r, d = c_rows.shape
    n = w.shape[1]
    tn = 1536
    return pl.pallas_call(
        _ada_kernel,
        out_shape=jax.ShapeDtypeStruct((r, n), F32),
        grid=(n // tn,),
        in_specs=[pl.BlockSpec((r, d), lambda j: (0, 0)),
                  pl.BlockSpec((d, tn), lambda j: (0, j)),
                  pl.BlockSpec((1, tn), lambda j: (0, j))],
        out_specs=pl.BlockSpec((r, tn), lambda j: (0, j)),
        compiler_params=_params("arbitrary"),
        name="ada_modulation",
    )(c_rows, w, b)


def _in_proj_kernel(x_ref, g_ref, mod_ref, w_ref, o_ref, dt_ref):
    h = _norm_mod(x_ref[...], g_ref[...], mod_ref[0, 1:2, :], mod_ref[0, 0:1, :])
    pr = jnp.dot(h.astype(BF16), w_ref[...], preferred_element_type=F32)
    o_ref[...] = pr.astype(o_ref.dtype)
    dt_ref[...] = pr[:, DT_COL * LANES:(DT_COL + 1) * LANES]


def in_proj(x2, g, mod, w, tiles_per_batch, tm):
    m, d = x2.shape
    n = w.shape[1]
    return pl.pallas_call(
        _in_proj_kernel,
        out_shape=(jax.ShapeDtypeStruct((m, n), BF16), jax.ShapeDtypeStruct((m, LANES), F32)),
        grid=(m // tm,),
        in_specs=[pl.BlockSpec((tm, d), lambda i: (i, 0)),
                  pl.BlockSpec((1, d), lambda i: (0, 0)),
                  pl.BlockSpec((1, 6, d), _mod_map(tiles_per_batch)),
                  _resident((d, n), lambda i: (0, 0))],
        out_specs=(pl.BlockSpec((tm, n), lambda i: (i, 0)), pl.BlockSpec((tm, LANES), lambda i: (i, 0))),
        compiler_params=_params("parallel"),
        name="in_proj",
    )(x2, g, mod, w)


def _mix_residual(x, mod_ref, w_ref, part_refs):
    y, k0 = None, 0
    for p_ref in part_refs:
        k = p_ref.shape[1]
        t = jnp.dot(p_ref[...], w_ref[k0:k0 + k, :], preferred_element_type=F32)
        y = t if y is None else y + t
        k0 += k
    return x + mod_ref[0, 2:3, :] * y


def _part_specs(parts, tm):
    return [pl.BlockSpec((tm, p.shape[1]), lambda i, *_: (i, 0)) for p in parts]


FF_CHUNK = 512


def _swiglu(h, wgu, wd, ff):
    acc = None
    for c0 in range(0, ff, FF_CHUNK):
        c1 = min(c0 + FF_CHUNK, ff)
        gg = jnp.dot(h, wgu(c0, c1), preferred_element_type=F32)
        uu = jnp.dot(h, wgu(ff + c0, ff + c1), preferred_element_type=F32)
        a = (gg * jax.nn.sigmoid(gg) * uu).astype(BF16)
        part = jnp.dot(a, wd(c0, c1), preferred_element_type=F32)
        acc = part if acc is None else acc + part
    return acc


def _ffn_kernel(x_ref, g_ref, mod_ref, wout_ref, wgu_ref, wd_ref, *refs):
    x = _mix_residual(x_ref[...], mod_ref, wout_ref, refs[:-1])
    y_ref = refs[-1]
    h = _norm_mod(x, g_ref[...], mod_ref[0, 4:5, :], mod_ref[0, 3:4, :]).astype(BF16)
    y = _swiglu(h, lambda a, b: wgu_ref[:, a:b], lambda a, b: wd_ref[a:b, :], wd_ref.shape[0])
    y_ref[...] = x + mod_ref[0, 5:6, :] * y


def ffn_dense(x2, parts, g, mod, w_out, w_gu, w_down, tiles_per_batch, tm):
    m, d = x2.shape
    return pl.pallas_call(
        _ffn_kernel,
        out_shape=jax.ShapeDtypeStruct((m, d), F32),
        grid=(m // tm,),
        in_specs=[pl.BlockSpec((tm, d), lambda i: (i, 0)),
                  pl.BlockSpec((1, d), lambda i: (0, 0)),
                  pl.BlockSpec((1, 6, d), _mod_map(tiles_per_batch)),
                  _resident(w_out.shape, lambda i: (0, 0)),
                  _resident(w_gu.shape, lambda i: (0, 0)),
                  _resident(w_down.shape, lambda i: (0, 0))] + _part_specs(parts, tm),
        out_specs=pl.BlockSpec((tm, d), lambda i: (i, 0)),
        compiler_params=_params("parallel"),
        name="ffn_dense",
    )(x2, g, mod, w_out, w_gu, w_down, *parts)


ROUTE_E1, ROUTE_E2, ROUTE_R1, ROUTE_R2, ROUTE_G1, ROUTE_G2 = range(6)
ROUTE_ROWS = 8


def _router_kernel(x_ref, g_ref, mod_ref, wout_ref, wr_ref, *refs):
    part_refs = refs[:-6]
    x1_ref, h_ref, route_ref, route_t_ref, cnt_ref, base_ref = refs[-6:]

    @pl.when(pl.program_id(0) == 0)
    def _():
        base_ref[...] = jnp.zeros_like(base_ref)

    x = _mix_residual(x_ref[...], mod_ref, wout_ref, part_refs)
    x1_ref[...] = x
    h = _norm_mod(x, g_ref[...], mod_ref[0, 4:5, :], mod_ref[0, 3:4, :]).astype(BF16)
    h_ref[...] = h
    tm = h.shape[0]
    logits = jnp.dot(h, wr_ref[...], preferred_element_type=F32)
    lane = lax.broadcasted_iota(jnp.int32, logits.shape, 1)
    neg = jnp.float32(-jnp.inf)
    logits = jnp.where(lane < N_EXPERTS, logits, neg)
    m1 = jnp.max(logits, axis=-1, keepdims=True)
    i1 = jnp.min(jnp.where(logits == m1, lane, LANES), axis=-1, keepdims=True)
    rest = jnp.where(lane == i1, neg, logits)
    m2 = jnp.max(rest, axis=-1, keepdims=True)
    i2 = jnp.min(jnp.where(rest == m2, lane, LANES), axis=-1, keepdims=True)
    e2 = jnp.exp(m2 - m1)
    g1 = 1.0 / (1.0 + e2)
    g2 = e2 / (1.0 + e2)
    sel1, sel2 = lane == i1, lane == i2
    sel = jnp.where(sel1 | sel2, 1.0, 0.0)
    li = lax.broadcasted_iota(jnp.int32, (tm, tm), 0)
    si = lax.broadcasted_iota(jnp.int32, (tm, tm), 1)
    before = jnp.where(si < li, 1.0, 0.0).astype(BF16)
    rank = jnp.dot(before, sel.astype(BF16), preferred_element_type=F32) + base_ref[...]
    r1 = jnp.sum(jnp.where(sel1, rank, 0.0), axis=-1, keepdims=True)
    r2 = jnp.sum(jnp.where(sel2, rank, 0.0), axis=-1, keepdims=True)
    base_ref[...] += jnp.sum(sel, axis=0, keepdims=True)
    cnt_ref[...] = base_ref[...]
    fields = (i1.astype(F32), i2.astype(F32), r1, r2, g1, g2)
    route = jnp.zeros(logits.shape, F32)
    for k, v in enumerate(fields):
        route = jnp.where(lane == k, v, route)
    route_ref[...] = route
    route_t_ref[...] = route.T[:ROUTE_ROWS, :]


def moe_router(x2, parts, g, mod, w_out, w_router, tiles_per_batch, tm):
    m, d = x2.shape
    return pl.pallas_call(
        _router_kernel,
        out_shape=(jax.ShapeDtypeStruct((m, d), F32), jax.ShapeDtypeStruct((m, d), BF16),
                   jax.ShapeDtypeStruct((m, LANES), F32), jax.ShapeDtypeStruct((ROUTE_ROWS, m), F32),
                   jax.ShapeDtypeStruct((1, LANES), F32)),
        grid=(m // tm,),
        in_specs=[pl.BlockSpec((tm, d), lambda i: (i, 0)),
                  pl.BlockSpec((1, d), lambda i: (0, 0)),
                  pl.BlockSpec((1, 6, d), _mod_map(tiles_per_batch)),
                  _resident(w_out.shape, lambda i: (0, 0)),
                  pl.BlockSpec((d, LANES), lambda i: (0, 0))] + _part_specs(parts, tm),
        out_specs=(pl.BlockSpec((tm, d), lambda i: (i, 0)),
                   pl.BlockSpec((tm, d), lambda i: (i, 0)),
                   pl.BlockSpec((tm, LANES), lambda i: (i, 0)),
                   pl.BlockSpec((ROUTE_ROWS, tm), lambda i: (0, i)),
                   pl.BlockSpec((1, LANES), lambda i: (0, 0))),
        scratch_shapes=[pltpu.VMEM((1, LANES), F32)],
        compiler_params=_params("arbitrary"),
        name="moe_router",
    )(x2, g, mod, w_out, w_router, *parts)


def _moe_kernel(tile_ref, exp_ref, start_ref, end_ref, xg_ref, wgu_ref, wd_ref, y_ref, wgu_b, wd_b):
    w = pl.program_id(0)
    tm = xg_ref.shape[0]
    start, end = start_ref[w], end_ref[w]
    t0 = tile_ref[w] * tm

    @pl.when((w == 0) | (exp_ref[w] != exp_ref[jnp.maximum(w - 1, 0)]))
    def _():
        wgu_b[...] = wgu_ref[0].astype(BF16)
        wd_b[...] = wd_ref[0].astype(BF16)

    @pl.when(end > start)
    def _():
        y = _swiglu(xg_ref[...], lambda a, b: wgu_b[:, a:b], lambda a, b: wd_b[a:b, :], wd_b.shape[0])
        y = y.astype(y_ref.dtype)

        @pl.when(start == t0)
        def _():
            y_ref[...] = y

        @pl.when(start != t0)
        def _():
            row = t0 + lax.broadcasted_iota(jnp.int32, (tm, 1), 0)
            y_ref[...] = jnp.where(row >= start, y, y_ref[...])


def moe_grouped(item_tile, item_expert, item_start, item_end, xg, w_gu, w_down, tm):
    p, d = xg.shape
    grid_spec = pltpu.PrefetchScalarGridSpec(
        num_scalar_prefetch=4,
        grid=(item_tile.shape[0],),
        in_specs=[pl.BlockSpec((tm, d), lambda w, it, ie, s, e: (it[w], 0)),
                  _resident((1,) + w_gu.shape[1:], lambda w, it, ie, s, e: (ie[w], 0, 0)),
                  _resident((1,) + w_down.shape[1:], lambda w, it, ie, s, e: (ie[w], 0, 0))],
        out_specs=pl.BlockSpec((tm, d), lambda w, it, ie, s, e: (it[w], 0)),
        scratch_shapes=[pltpu.VMEM(w_gu.shape[1:], BF16), pltpu.VMEM(w_down.shape[1:], BF16)],
    )
    return pl.pallas_call(
        _moe_kernel,
        out_shape=jax.ShapeDtypeStruct((p, d), BF16),
        grid_spec=grid_spec,
        compiler_params=_params("arbitrary"),
        name="moe_grouped",
    )(item_tile, item_expert, item_start, item_end, xg, w_gu, w_down)


def _moe_combine_kernel(x_ref, ya_ref, yb_ref, route_ref, mod_ref, *refs):
    o_ref = refs[-1]
    r = route_ref[...]
    y = (r[:, ROUTE_G1:ROUTE_G1 + 1] * ya_ref[...].astype(F32) + r[:, ROUTE_G2:ROUTE_G2 + 1] * yb_ref[...].astype(F32))
    x = x_ref[...] + mod_ref[0, 5:6, :] * y
    if len(refs) == 2:
        ms = jnp.mean(x * x, axis=-1, keepdims=True)
        x = x * lax.rsqrt(ms + EPS) * refs[0][...]
    o_ref[...] = x


def moe_combine(x2, ya, yb, route, mod, tiles_per_batch, tm, g_final=None):
    m, d = x2.shape
    row = pl.BlockSpec((tm, d), lambda i: (i, 0))
    specs = [row, row, row, pl.BlockSpec((tm, LANES), lambda i: (i, 0)),
             pl.BlockSpec((1, 6, d), _mod_map(tiles_per_batch))]
    args = [x2, ya, yb, route, mod]
    if g_final is not None:
        specs.append(pl.BlockSpec((1, d), lambda i: (0, 0)))
        args.append(g_final)
    return pl.pallas_call(
        _moe_combine_kernel,
        out_shape=jax.ShapeDtypeStruct((m, d), F32),
        grid=(m // tm,),
        in_specs=specs,
        out_specs=row,
        compiler_params=_params("parallel"),
        name="moe_combine",
    )(*args)


def moe_block(x2, parts, g, mod, w_out, w_router, w_gu, w_down, tiles_per_batch, tm_tok, tm_grp, g_final=None):
    m, d = x2.shape
    x2, h, route, route_t, cnt = moe_router(x2, parts, g, mod, w_out, w_router, tiles_per_batch, tm_tok)
    cnt = cnt[0, :N_EXPERTS].astype(jnp.int32)
    p = m * TOP_K
    off = jnp.cumsum(cnt) - cnt
    field = lambda k: route_t[k].astype(jnp.int32)

    def row_of(e, r):
        for k in range(N_EXPERTS):
            r = r + jnp.where(e == k, off[k], 0)
        return r

    d1 = row_of(field(ROUTE_E1), field(ROUTE_R1))
    d2 = row_of(field(ROUTE_E2), field(ROUTE_R2))
    tok = jnp.arange(m, dtype=jnp.int32)
    _, src = lax.sort_key_val(jnp.concatenate([d1, d2]), jnp.concatenate([tok, tok]))
    n_row_tiles = p // tm_grp
    start = jnp.sort(jnp.concatenate([jnp.arange(n_row_tiles, dtype=jnp.int32) * tm_grp, off]))
    end = jnp.concatenate([start[1:], jnp.full((1,), p, jnp.int32)])
    item_tile = jnp.minimum(start // tm_grp, n_row_tiles - 1)
    item_expert = jnp.sum((off[None, :] <= start[:, None]).astype(jnp.int32), axis=1) - 1
    rows = lambda a, idx: a.at[idx].get(mode="promise_in_bounds")
    yg = moe_grouped(item_tile, item_expert, start, end, rows(h, src), w_gu, w_down, tm_grp)
    return moe_combine(x2, rows(yg, d1), rows(yg, d2), route, mod, tiles_per_batch, tm_tok, g_final)


def _final_norm_kernel(x_ref, g_ref, o_ref):
    x = x_ref[...]
    ms = jnp.mean(x * x, axis=-1, keepdims=True)
    o_ref[...] = x * lax.rsqrt(ms + EPS) * g_ref[...]


def final_norm(x2, g, tm):
    m, d = x2.shape
    return pl.pallas_call(
        _final_norm_kernel,
        out_shape=jax.ShapeDtypeStruct((m, d), F32),
        grid=(m // tm,),
        in_specs=[pl.BlockSpec((tm, d), lambda i: (i, 0)), pl.BlockSpec((1, d), lambda i: (0, 0))],
        out_specs=pl.BlockSpec((tm, d), lambda i: (i, 0)),
        compiler_params=_params("parallel"),
        name="final_norm",
    )(x2, g)


NA_QROWS = 4
NA_KROWS = NA_QROWS + WIN_R
Q_COL, K_COL, V_COL = 4, 5, 6


def _na_key_start(j, rows):
    return np.clip(j * NA_QROWS - WIN_R // 2, 0, rows - NA_KROWS)


def _na_bias_index(rows):
    nblk = rows // NA_QROWS
    pats = []
    for j in range(nblk):
        start = _na_key_start(j, rows)
        r = j * NA_QROWS + np.arange(NA_QROWS)
        sr = np.clip(r - WIN_R // 2, 0, rows - WIN_R)
        kr = start + np.arange(NA_KROWS)
        rvalid = (kr[None, :] >= sr[:, None]) & (kr[None, :] < sr[:, None] + WIN_R)
        ri = np.clip(kr[None, :] - r[:, None] + WIN_R - 1, 0, 2 * WIN_R - 2)
        pats.append((ri, rvalid))
    for j in range(2, nblk - 1):
        assert all(np.array_equal(a, b) for a, b in zip(pats[1], pats[j]))
    sel = [pats[0], pats[1], pats[nblk - 1]]
    ri = np.stack([p[0] for p in sel])
    rvalid = np.stack([p[1] for p in sel])
    c = np.arange(GRID_W)
    sc = np.clip(c - WIN_C // 2, 0, GRID_W - WIN_C)
    cvalid = (c[None, :] >= sc[:, None]) & (c[None, :] < sc[:, None] + WIN_C)
    ci = np.clip(c[None, :] - c[:, None] + WIN_C - 1, 0, 2 * WIN_C - 2)
    c_onehot = (ci[..., None] == np.arange(2 * WIN_C - 1)).astype(np.float32)
    valid = rvalid[:, :, None, :, None] & cvalid[None, None, :, None, :]
    return ri, c_onehot, valid


def na_bias_table(rpb, rows):
    ri, c_onehot, valid = _na_bias_index(rows)
    toep = jnp.einsum('hrd,qkd->hrqk', rpb, c_onehot, precision=lax.Precision.HIGHEST)
    b = jnp.take(toep, ri.reshape(-1), axis=1).reshape((NA_HEADS,) + ri.shape + (GRID_W, GRID_W))
    b = jnp.transpose(b, (1, 0, 2, 4, 3, 5))
    b = jnp.where(valid[:, None], b, -jnp.inf)
    return b.reshape(3, NA_HEADS, NA_QROWS * GRID_W, NA_KROWS * GRID_W).astype(F32)


def _attend_heads(q, key_sets, bias_fn):
    n, w = q.shape
    lane = lax.broadcasted_iota(jnp.int32, (1, w), 1)
    head_masks = [(lane >= h * NA_HEAD_DIM) & (lane < (h + 1) * NA_HEAD_DIM) for h in range(NA_HEADS)]
    all_scores = []
    for h, mh in enumerate(head_masks):
        qh = jnp.where(mh, q, 0.0).astype(BF16)
        scores = []
        for i, (k, _) in enumerate(key_sets):
            s = lax.dot_general(qh, k, (((1,), (1,)), ((), ())), preferred_element_type=F32)
            b = bias_fn(i, h)
            scores.append(s if b is None else s + b)
        all_scores.append(scores)
    all_probs = []
    for scores in all_scores:
        m = scores[0].max(axis=-1, keepdims=True)
        for s in scores[1:]:
            m = jnp.maximum(m, s.max(axis=-1, keepdims=True))
        denom = jnp.zeros((n, 1), F32)
        probs = []
        for s in scores:
            p = jnp.exp(s - m)
            denom = denom + p.sum(axis=-1, keepdims=True)
            probs.append(p.astype(BF16))
        all_probs.append((probs, denom))
    out = jnp.zeros((n, w), F32)
    for mh, (probs, denom) in zip(head_masks, all_probs):
        acc = jnp.zeros((n, w), F32)
        for p, (_, v) in zip(probs, key_sets):
            acc = acc + jnp.dot(p, v, preferred_element_type=F32)
        out = out + jnp.where(mh, acc / denom, 0.0)
    return out


def _na_kernel(q_ref, k_ref, v_ref, kc_ref, vc_ref, bias_ref, o_ref, *, rows):
    j = pl.program_id(1)
    start = jnp.clip(j * NA_QROWS - WIN_R // 2, 0, rows - NA_KROWS)
    t0 = pl.multiple_of(start * GRID_W, GRID_W)
    nk = NA_KROWS * GRID_W
    kw = k_ref[pl.ds(t0, nk), :]
    vw = v_ref[pl.ds(t0, nk), :]
    q = q_ref[...] * (1.0 / math.sqrt(NA_HEAD_DIM))
    o = _attend_heads(q, [(kw, vw), (kc_ref[...], vc_ref[...])], lambda i, h: bias_ref[0, h] if i == 0 else None)
    o_ref[...] = o.astype(o_ref.dtype)


def na_attention(pr, pr_c, bias, bn):
    S, Lc = pr.shape[0] // bn, pr_c.shape[0] // bn
    rows = S // GRID_W
    nblk = rows // NA_QROWS
    nq = NA_QROWS * GRID_W

    def pat(b, j):
        return (jnp.where(j == 0, 0, jnp.where(j == nblk - 1, 2, 1)), 0, 0, 0)

    return pl.pallas_call(
        functools.partial(_na_kernel, rows=rows),
        out_shape=jax.ShapeDtypeStruct((bn * S, W_NA), BF16),
        grid=(bn, nblk),
        in_specs=[pl.BlockSpec((nq, W_NA), lambda b, j: (b * nblk + j, Q_COL)),
                  pl.BlockSpec((S, W_NA), lambda b, j: (b, K_COL)),
                  pl.BlockSpec((S, W_NA), lambda b, j: (b, V_COL)),
                  pl.BlockSpec((Lc, W_NA), lambda b, j: (b, K_COL)),
                  pl.BlockSpec((Lc, W_NA), lambda b, j: (b, V_COL)),
                  pl.BlockSpec((1,) + bias.shape[1:], pat)],
        out_specs=pl.BlockSpec((nq, W_NA), lambda b, j: (b * nblk + j, 0)),
        compiler_params=_params("parallel", "arbitrary"),
        name="na_attention",
    )(pr, pr, pr, pr_c, pr_c, bias)


def _ctx_attn_kernel(q_ref, k_ref, v_ref, o_ref):
    q = q_ref[...] * (1.0 / math.sqrt(NA_HEAD_DIM))
    o = _attend_heads(q, [(k_ref[...], v_ref[...])], lambda i, h: None)
    o_ref[...] = o.astype(o_ref.dtype)


def ctx_attention(pr_c, bn):
    Lc = pr_c.shape[0] // bn
    return pl.pallas_call(
        _ctx_attn_kernel,
        out_shape=jax.ShapeDtypeStruct((bn * Lc, W_NA), BF16),
        grid=(bn,),
        in_specs=[pl.BlockSpec((Lc, W_NA), lambda b: (b, Q_COL)),
                  pl.BlockSpec((Lc, W_NA), lambda b: (b, K_COL)),
                  pl.BlockSpec((Lc, W_NA), lambda b: (b, V_COL))],
        out_specs=pl.BlockSpec((Lc, W_NA), lambda b: (b, 0)),
        compiler_params=_params("parallel"),
        name="ctx_attention",
    )(pr_c, pr_c, pr_c)


POOL_COL, CONV_H_COL, CONV_B_COL, CONV_C_COL = 0, 1, 2, 3


def _shift_rows(x, k, row):
    n = x.shape[0]
    y = pltpu.roll(x, k % n, 0)
    return jnp.where(row < k, 0.0, y) if k > 0 else jnp.where(row >= n + k, 0.0, y)


def _local_mix_kernel(u_ref, h_ref, bg_ref, cg_ref, pw_ref, ps_ref, cw_ref, op_ref, oc_ref, *, seq):
    row = lax.broadcasted_iota(jnp.int32, (seq, 1), 0)
    lane = lax.broadcasted_iota(jnp.int32, (1, W_POOL), 1)
    u = u_ref[...].astype(F32)
    trailing, leading = {1: u}, {1: u}
    for w in (1, 2, 4):
        trailing[2 * w] = trailing[w] + _shift_rows(trailing[w], w, row)
        leading[2 * w] = leading[w] + _shift_rows(leading[w], -w, row)
    rowf = row.astype(F32)
    win_sum = jnp.zeros_like(u)
    cnt = jnp.zeros_like(u)
    for g, win in enumerate(POOL_WINDOWS):
        half = win // 2
        mg = (lane >= g * POOL_GROUP) & (lane < (g + 1) * POOL_GROUP)
        s = _shift_rows(trailing[half], 1, row) + leading[half]
        n = jnp.minimum(rowf + half, float(seq)) - jnp.maximum(rowf - half, 0.0)
        win_sum = jnp.where(mg, s, win_sum)
        cnt = jnp.where(mg, n, cnt)
    p = win_sum / cnt - u
    pooled = jnp.dot(p.astype(BF16), pw_ref[...], preferred_element_type=F32) * ps_ref[...]
    op_ref[...] = pooled.astype(op_ref.dtype)
    v = cg_ref[...].astype(F32) * h_ref[...].astype(F32)
    conv = (_shift_rows(v, 1, row) * cw_ref[0:1, :] + v * cw_ref[1:2, :] + _shift_rows(v, -1, row) * cw_ref[2:3, :])
    oc_ref[...] = (bg_ref[...].astype(F32) * conv).astype(oc_ref.dtype)


def local_mix(pr, pool_w, pool_scale, conv_w, bn):
    seq = pr.shape[0] // bn
    pw = jax.scipy.linalg.block_diag(*[pool_w[g] for g in range(len(POOL_WINDOWS))]).astype(BF16)
    blk = lambda col: pl.BlockSpec((seq, W_POOL), lambda b: (b, col))
    full = lambda a: pl.BlockSpec(a.shape, lambda b: (0,) * a.ndim)
    args = (pw, pool_scale[None, :], conv_w)
    return pl.pallas_call(
        functools.partial(_local_mix_kernel, seq=seq),
        out_shape=(jax.ShapeDtypeStruct((bn * seq, W_POOL), BF16), jax.ShapeDtypeStruct((bn * seq, W_CONV), BF16)),
        grid=(bn,),
        in_specs=[blk(POOL_COL), blk(CONV_H_COL), blk(CONV_B_COL), blk(CONV_C_COL)] + [full(a) for a in args],
        out_specs=(pl.BlockSpec((seq, W_POOL), lambda b: (b, 0)),
                   pl.BlockSpec((seq, W_CONV), lambda b: (b, 0))),
        compiler_params=_params("parallel"),
        name="local_mix",
    )(pr, pr, pr, pr, *args)


Z_COL, XS_COL, BS_COL, CS_COL = 7, 8, 9, 10
DT_COL = 22
HEADS_PER_GROUP = SSD_HEADS // SSD_GROUPS
GROUP_W = HEADS_PER_GROUP * SSD_HEAD_DIM


def _head_mask(h):
    lane = lax.broadcasted_iota(jnp.int32, (1, W_SSD), 1)
    return (lane >= h * SSD_HEAD_DIM) & (lane < (h + 1) * SSD_HEAD_DIM)


def _expand_heads(v, d):
    out = jnp.zeros((v.shape[0], W_SSD), F32)
    for h in range(SSD_HEADS):
        k = d * SSD_HEADS + h
        out = jnp.where(_head_mask(h), v[:, k:k + 1], out)
    return out


def _ssd_chunk_terms(c, xs_ref, b_ref, c_ref, dt_ref, la_ref, dsk_ref, y_ref, upd_ref, dec_ref, need_y):
    T = SSD_CHUNK
    r0 = pl.multiple_of(c * T, T)
    xs = xs_ref[pl.ds(r0, T), :]
    bm = b_ref[pl.ds(r0, T), :]
    cm = c_ref[pl.ds(r0, T), :]
    dt = dt_ref[pl.ds(r0, T), :]
    la = la_ref[pl.ds(r0, T), :]
    li = lax.broadcasted_iota(jnp.int32, (T, T), 0)
    si = lax.broadcasted_iota(jnp.int32, (T, T), 1)
    causal = si <= li
    prefix = jnp.dot(causal.astype(F32), la, precision=lax.Precision.HIGHEST, preferred_element_type=F32)
    total = prefix[T - 1:T, :]
    lane = lax.broadcasted_iota(jnp.int32, (1, LANES), 1)
    acum = jnp.where(lane < SSD_HEADS, prefix, total - prefix + la)
    bt = bm.astype(F32).T.astype(BF16)
    if need_y:
        acum_t = acum.T
        cb = [lax.dot_general(cm[:, g * SSD_STATE:(g + 1) * SSD_STATE], bm[:, g * SSD_STATE:(g + 1) * SSD_STATE],
                              (((1,), (1,)), ((), ())), preferred_element_type=F32) for g in range(SSD_GROUPS)]
        y = jnp.zeros((T, W_SSD), F32)
    per_dir = []
    for d in range(2):
        acum_e = _expand_heads(acum, d)
        tot_e = _expand_heads(total, d)
        xdt = xs * _expand_heads(dt, d)
        xw = (xdt * jnp.exp(tot_e - acum_e)).astype(BF16)
        dec_ref[c, d, 0:T, :] = jnp.exp(acum_e)
        dec_ref[c, d, T:T + 1, :] = jnp.exp(tot_e)
        scores = []
        if need_y:
            mask = causal if d == 0 else (si >= li)
            for h in range(SSD_HEADS):
                k = d * SSD_HEADS + h
                decay = jnp.exp(jnp.where(mask, acum[:, k:k + 1] - acum_t[k:k + 1, :], -jnp.inf))
                scores.append((cb[h // HEADS_PER_GROUP] * decay).astype(BF16))
        per_dir.append((xw, xdt, scores))
    for d, (xw, xdt, scores) in enumerate(per_dir):
        upd = [jnp.dot(bt[g * SSD_STATE:(g + 1) * SSD_STATE, :], xw[:, g * GROUP_W:(g + 1) * GROUP_W],
                       preferred_element_type=F32) for g in range(SSD_GROUPS)]
        upd_ref[c, d] = jnp.concatenate(upd, axis=1)
        if need_y:
            x_heads = [jnp.where(_head_mask(h), xdt, 0.0).astype(BF16) for h in range(SSD_HEADS)]
            y = y + jnp.dot(jnp.concatenate(scores, axis=1), jnp.concatenate(x_heads, axis=0),
                            preferred_element_type=F32)
    if need_y:
        y_ref[pl.ds(r0, T), :] = xs * dsk_ref[...] + y


def _ssd_chunk_state(c, d, c_ref, y_ref, upd_ref, dec_ref, st_ref, need_y):
    T = SSD_CHUNK
    r0 = pl.multiple_of(c * T, T)
    st = st_ref[d]
    if need_y:
        cm = c_ref[pl.ds(r0, T), :]
        st_b = st.astype(BF16)
        ys = [jnp.dot(cm[:, g * SSD_STATE:(g + 1) * SSD_STATE], st_b[:, g * GROUP_W:(g + 1) * GROUP_W],
                      preferred_element_type=F32) for g in range(SSD_GROUPS)]
        y_ref[pl.ds(r0, T), :] += jnp.concatenate(ys, axis=1) * dec_ref[c, d, 0:T, :]
    st_ref[d] = dec_ref[c, d, T:T + 1, :] * st + upd_ref[c, d]


def _ssd_kernel(z_ref, xs_in, bs_in, cs_in, dtr_ref, cw_ref, cb_ref, dtb_ref, alog_ref, dsk_ref, ng_ref, h0_ref,
                *refs, seq, need_y):
    if need_y:
        o_ref, hT_ref, xs_ref, b_ref, c_ref, dt_ref, la_ref, upd_ref, dec_ref, st_ref, y_ref = refs
    else:
        hT_ref, xs_ref, b_ref, c_ref, dt_ref, la_ref, upd_ref, dec_ref, st_ref = refs
        y_ref = None
    nc = seq // SSD_CHUNK
    row = lax.broadcasted_iota(jnp.int32, (seq, 1), 0)
    for gi, (src, dst) in enumerate(((xs_in, xs_ref), (bs_in, b_ref), (cs_in, c_ref))):
        sl = slice(gi * W_SSD, (gi + 1) * W_SSD)
        x = src[...].astype(F32)
        cv = (_shift_rows(x, 1, row) * cw_ref[0:1, sl] + x * cw_ref[1:2, sl]
              + _shift_rows(x, -1, row) * cw_ref[2:3, sl] + cb_ref[:, sl])
        dst[...] = (cv * jax.nn.sigmoid(cv)).astype(dst.dtype)
    dtv = dtr_ref[...] + dtb_ref[...]
    dt = jnp.maximum(dtv, 0.0) + jnp.log1p(jnp.exp(-jnp.abs(dtv)))
    dt_ref[...] = dt
    la_ref[...] = dt * (-jnp.exp(alog_ref[...]))
    st_ref[...] = h0_ref[0]

    def terms(c, carry):
        _ssd_chunk_terms(c, xs_ref, b_ref, c_ref, dt_ref, la_ref, dsk_ref, y_ref, upd_ref, dec_ref, need_y)
        return carry

    lax.fori_loop(0, nc, terms, 0, unroll=min(16, nc))

    def recur(i, carry):
        _ssd_chunk_state(i, 0, c_ref, y_ref, upd_ref, dec_ref, st_ref, need_y)
        _ssd_chunk_state(nc - 1 - i, 1, c_ref, y_ref, upd_ref, dec_ref, st_ref, need_y)
        return carry

    lax.fori_loop(0, nc, recur, 0, unroll=min(16, nc))
    hT_ref[0] = st_ref[...]
    if need_y:
        z = z_ref[...].astype(F32)
        yz = y_ref[...] * (z * jax.nn.sigmoid(z))
        ms = jnp.mean(yz * yz, axis=-1, keepdims=True)
        o_ref[...] = (yz * lax.rsqrt(ms + EPS) * ng_ref[...]).astype(o_ref.dtype)


def ssd_scan(pr, dt, h0, conv_w, conv_b, dt_bias, a_log, d_skip, norm_g, need_y):
    bn = h0.shape[0]
    seq = pr.shape[0] // bn
    nc = seq // SSD_CHUNK
    pad = LANES - 2 * SSD_HEADS
    dtb = jnp.pad(dt_bias.reshape(1, -1), ((0, 0), (0, pad)))
    alog = jnp.pad(a_log.reshape(1, -1), ((0, 0), (0, pad)))
    dsk = jnp.repeat(d_skip, SSD_HEAD_DIM)[None, :]
    blk = lambda col: pl.BlockSpec((seq, W_SSD), lambda b: (b, col))
    full = lambda a: pl.BlockSpec(a.shape, lambda b: (0,) * a.ndim)
    st_shape = (1, 2, SSD_STATE, W_SSD)
    st_spec = pl.BlockSpec(st_shape, lambda b: (b, 0, 0, 0))
    args = (conv_w, conv_b[None, :], dtb, alog, dsk, norm_g[None, :])
    out_shape = [jax.ShapeDtypeStruct((bn,) + st_shape[1:], F32)]
    out_specs = [st_spec]
    scratch = [pltpu.VMEM((seq, W_SSD), F32),
               pltpu.VMEM((seq, W_SSD), BF16), pltpu.VMEM((seq, W_SSD), BF16),
               pltpu.VMEM((seq, LANES), F32), pltpu.VMEM((seq, LANES), F32),
               pltpu.VMEM((nc, 2, SSD_STATE, W_SSD), F32),
               pltpu.VMEM((nc, 2, SSD_CHUNK + 8, W_SSD), F32),
               pltpu.VMEM(st_shape[1:], F32)]
    if need_y:
        out_shape.insert(0, jax.ShapeDtypeStruct((bn * seq, W_SSD), BF16))
        out_specs.insert(0, pl.BlockSpec((seq, W_SSD), lambda b: (b, 0)))
        scratch.append(pltpu.VMEM((seq, W_SSD), F32))
    res = pl.pallas_call(
        functools.partial(_ssd_kernel, seq=seq, need_y=need_y),
        out_shape=out_shape,
        grid=(bn,),
        in_specs=[blk(Z_COL), blk(XS_COL), blk(BS_COL), blk(CS_COL),
                  pl.BlockSpec((seq, LANES), lambda b: (b, 0))]
                 + [full(a) for a in args] + [st_spec],
        out_specs=out_specs,
        scratch_shapes=scratch,
        compiler_params=_params("parallel"),
        name="ssd_scan",
    )(pr, pr, pr, pr, dt, *args, h0)
    return (res[0], res[1]) if need_y else (None, res[0])


def ssd_mix(pr, dt, pr_c, dt_c, conv_w, conv_b, dt_bias, a_log, d_skip, norm_g, ctx_out, bn):
    h0 = jnp.zeros((bn, 2, SSD_STATE, W_SSD), F32)
    oc, hc = ssd_scan(pr_c, dt_c, h0, conv_w, conv_b, dt_bias, a_log, d_skip, norm_g, ctx_out)
    o, _ = ssd_scan(pr, dt, hc, conv_w, conv_b, dt_bias, a_log, d_skip, norm_g, True)
    return o, oc


def mixer(pr, dt, pr_c, dt_c, pool_w, pool_scale, conv_w, na_bias, s_conv_w, s_conv_b, dt_bias, a_log, d_skip,
          s_norm_g, ctx_out, bn):
    o_ssd, oc_ssd = ssd_mix(pr, dt, pr_c, dt_c, s_conv_w, s_conv_b, dt_bias, a_log, d_skip, s_norm_g, ctx_out, bn)
    o = [*local_mix(pr, pool_w, pool_scale, conv_w, bn), na_attention(pr, pr_c, na_bias, bn), o_ssd]
    if not ctx_out:
        return o, None
    oc = [*local_mix(pr_c, pool_w, pool_scale, conv_w, bn), ctx_attention(pr_c, bn), oc_ssd]
    return o, oc


TM = 512
TM_ROUTE = 1024
TM_PROJ = 1024
TM_GRP = 512


def kernel(x, c, ctx, c_ctx, w_ada, b_ada, g_mix, g_ffn, w_in, w_out, pool_w, pool_scale, conv_w, na_rpb,
           ssd_conv_w, ssd_conv_b, ssd_dt_bias, ssd_a_log, ssd_d, ssd_norm_g, ffn_w_gu, ffn_w_down,
           moe_router, moe_w_gu, moe_w_down, g_final):
    bn, S, d = x.shape
    Lc = ctx.shape[1]
    m, mc = bn * S, bn * Lc
    tpb = S // TM
    x2 = x.reshape(m, d)
    xc2 = ctx.reshape(mc, d)
    c_rows = jnp.concatenate([c, c_ctx[None, :], jnp.zeros((7, d), F32)], axis=0)
    for l in range(DEPTH):
        ctx_out = l < DEPTH - 1
        last = l == DEPTH - 1
        ada = ada_modulation(c_rows, w_ada[l].astype(BF16), b_ada[l][None, :])
        mod = ada[:bn].reshape(bn, 6, d)
        mod_c = ada[bn:bn + 1].reshape(1, 6, d)
        w_in_l = jnp.pad(w_in[l], ((0, 0), (0, D_IN_PAD - D_IN_PROJ))).astype(BF16)
        g_m = g_mix[l][None, :]
        g_f = g_ffn[l][None, :]
        pr, dt = in_proj(x2, g_m, mod, w_in_l, S // TM_PROJ, TM_PROJ)
        pr_c, dt_c = in_proj(xc2, g_m, mod_c, w_in_l, None, min(TM_PROJ, mc))
        na_bias = na_bias_table(na_rpb[l], S // GRID_W)
        o, oc = mixer(pr, dt, pr_c, dt_c, pool_w[l], pool_scale[l], conv_w[l], na_bias, ssd_conv_w[l],
                      ssd_conv_b[l], ssd_dt_bias[l], ssd_a_log[l], ssd_d[l], ssd_norm_g[l], ctx_out, bn)
        w_out_l = w_out[l].astype(BF16)
        j = l // 2
        if l % 2 == 0:
            w_gu = ffn_w_gu[j].astype(BF16)
            w_dn = ffn_w_down[j].astype(BF16)
            x2 = ffn_dense(x2, o, g_f, mod, w_out_l, w_gu, w_dn, tpb, TM)
            if ctx_out:
                xc2 = ffn_dense(xc2, oc, g_f, mod_c, w_out_l, w_gu, w_dn, None, min(TM, mc))
        else:
            w_r = jnp.pad(moe_router[j], ((0, 0), (0, LANES - N_EXPERTS))).astype(BF16)
            w_gu, w_dn = moe_w_gu[j], moe_w_down[j]
            x2 = moe_block(x2, o, g_f, mod, w_out_l, w_r, w_gu, w_dn, S // TM_ROUTE, TM_ROUTE, TM_GRP,
                           g_final[None, :] if last else None)
            if ctx_out:
                xc2 = moe_block(xc2, oc, g_f, mod_c, w_out_l, w_r, w_gu, w_dn, None, min(TM_ROUTE, mc), TM_GRP)
            if last:
                return x2.reshape(bn, S, d)
    return final_norm(x2, g_final[None, :], TM).reshape(bn, S, d)
```

```python
import functools
import math

import numpy as np
import jax
import jax.numpy as jnp
from jax import lax
from jax.experimental import pallas as pl
from jax.experimental.pallas import tpu as pltpu

DEPTH = 2
GRID_W = 64
EPS = 1e-6
W_POOL = 256
W_CONV = 256
W_NA = 256
W_SSD = 256
POOL_WINDOWS = (2, 4, 8, 16)
POOL_GROUP = W_POOL // len(POOL_WINDOWS)
NA_HEADS = 4
NA_HEAD_DIM = W_NA // NA_HEADS
WIN_R = 8
WIN_C = 16
SSD_HEAD_DIM = 64
SSD_HEADS = W_SSD // SSD_HEAD_DIM
SSD_GROUPS = 2
SSD_STATE = 128
SSD_CHUNK = 128
SSD_XBC = W_SSD + 2 * SSD_GROUPS * SSD_STATE
D_IN_PROJ = W_POOL + 3 * W_CONV + 3 * W_NA + W_SSD + SSD_XBC + 2 * SSD_HEADS
N_EXPERTS = 8
TOP_K = 2

LANES = 128
D_IN_PAD = -(-D_IN_PROJ // LANES) * LANES
VMEM_LIMIT = 56 * 1024 * 1024
BF16 = jnp.bfloat16
F32 = jnp.float32


def _params(*sem):
    return pltpu.CompilerParams(dimension_semantics=sem, vmem_limit_bytes=VMEM_LIMIT)


def _norm_mod(x, g, scale, shift):
    ms = jnp.mean(x * x, axis=-1, keepdims=True)
    return (x * lax.rsqrt(ms + EPS) * g) * (1.0 + scale) + shift


def _mod_map(tiles_per_batch):
    if tiles_per_batch is None:
        return lambda i, *_: (0, 0, 0)
    return lambda i, *_: (i // tiles_per_batch, 0, 0)


def _resident(shape, index_map):
    return pl.BlockSpec(shape, index_map, pipeline_mode=pl.Buffered(1))


def _ada_kernel(c_ref, w_ref, b_ref, o_ref):
    c = c_ref[...]
    s = c * jax.nn.sigmoid(c)
    o_ref[...] = jnp.dot(s.astype(BF16), w_ref[...], preferred_element_type=F32) + b_ref[...]


def ada_modulation(c_rows, w, b):
    r, d = c_rows.shape
    n = w.shape[1]
    tn = 1536
    return pl.pallas_call(
        _ada_kernel,
        out_shape=jax.ShapeDtypeStruct((r, n), F32),
        grid=(n // tn,),
        in_specs=[pl.BlockSpec((r, d), lambda j: (0, 0)),
                  pl.BlockSpec((d, tn), lambda j: (0, j)),
                  pl.BlockSpec((1, tn), lambda j: (0, j))],
        out_specs=pl.BlockSpec((r, tn), lambda j: (0, j)),
        compiler_params=_params("arbitrary"),
        name="ada_modulation",
    )(c_rows, w, b)


def _in_proj_kernel(x_ref, g_ref, mod_ref, w_ref, o_ref, dt_ref):
    h = _norm_mod(x_ref[...], g_ref[...], mod_ref[0, 1:2, :], mod_ref[0, 0:1, :])
    pr = jnp.dot(h.astype(BF16), w_ref[...], preferred_element_type=F32)
    o_ref[...] = pr.astype(o_ref.dtype)
    dt_ref[...] = pr[:, DT_COL * LANES:(DT_COL + 1) * LANES]


def in_proj(x2, g, mod, w, tiles_per_batch, tm):
    m, d = x2.shape
    n = w.shape[1]
    return pl.pallas_call(
        _in_proj_kernel,
        out_shape=(jax.ShapeDtypeStruct((m, n), BF16), jax.ShapeDtypeStruct((m, LANES), F32)),
        grid=(m // tm,),
        in_specs=[pl.BlockSpec((tm, d), lambda i: (i, 0)),
                  pl.BlockSpec((1, d), lambda i: (0, 0)),
                  pl.BlockSpec((1, 6, d), _mod_map(tiles_per_batch)),
                  _resident((d, n), lambda i: (0, 0))],
        out_specs=(pl.BlockSpec((tm, n), lambda i: (i, 0)), pl.BlockSpec((tm, LANES), lambda i: (i, 0))),
        compiler_params=_params("parallel"),
        name="in_proj",
    )(x2, g, mod, w)


def _mix_residual(x, mod_ref, w_ref, part_refs):
    y, k0 = None, 0
    for p_ref in part_refs:
        k = p_ref.shape[1]
        t = jnp.dot(p_ref[...], w_ref[k0:k0 + k, :], preferred_element_type=F32)
        y = t if y is None else y + t
        k0 += k
    return x + mod_ref[0, 2:3, :] * y


def _part_specs(parts, tm):
    return [pl.BlockSpec((tm, p.shape[1]), lambda i, *_: (i, 0)) for p in parts]


FF_CHUNK = 512


def _swiglu(h, wgu, wd, ff):
    acc = None
    for c0 in range(0, ff, FF_CHUNK):
        c1 = min(c0 + FF_CHUNK, ff)
        gg = jnp.dot(h, wgu(c0, c1), preferred_element_type=F32)
        uu = jnp.dot(h, wgu(ff + c0, ff + c1), preferred_element_type=F32)
        a = (gg * jax.nn.sigmoid(gg) * uu).astype(BF16)
        part = jnp.dot(a, wd(c0, c1), preferred_element_type=F32)
        acc = part if acc is None else acc + part
    return acc


def _ffn_kernel(x_ref, g_ref, mod_ref, wout_ref, wgu_ref, wd_ref, *refs):
    x = _mix_residual(x_ref[...], mod_ref, wout_ref, refs[:-1])
    y_ref = refs[-1]
    h = _norm_mod(x, g_ref[...], mod_ref[0, 4:5, :], mod_ref[0, 3:4, :]).astype(BF16)
    y = _swiglu(h, lambda a, b: wgu_ref[:, a:b], lambda a, b: wd_ref[a:b, :], wd_ref.shape[0])
    y_ref[...] = x + mod_ref[0, 5:6, :] * y


def ffn_dense(x2, parts, g, mod, w_out, w_gu, w_down, tiles_per_batch, tm):
    m, d = x2.shape
    return pl.pallas_call(
        _ffn_kernel,
        out_shape=jax.ShapeDtypeStruct((m, d), F32),
        grid=(m // tm,),
        in_specs=[pl.BlockSpec((tm, d), lambda i: (i, 0)),
                  pl.BlockSpec((1, d), lambda i: (0, 0)),
                  pl.BlockSpec((1, 6, d), _mod_map(tiles_per_batch)),
                  _resident(w_out.shape, lambda i: (0, 0)),
                  _resident(w_gu.shape, lambda i: (0, 0)),
                  _resident(w_down.shape, lambda i: (0, 0))] + _part_specs(parts, tm),
        out_specs=pl.BlockSpec((tm, d), lambda i: (i, 0)),
        compiler_params=_params("parallel"),
        name="ffn_dense",
    )(x2, g, mod, w_out, w_gu, w_down, *parts)


ROUTE_E1, ROUTE_E2, ROUTE_R1, ROUTE_R2, ROUTE_G1, ROUTE_G2 = range(6)
ROUTE_ROWS = 8


def _router_kernel(x_ref, g_ref, mod_ref, wout_ref, wr_ref, *refs):
    part_refs = refs[:-6]
    x1_ref, h_ref, route_ref, route_t_ref, cnt_ref, base_ref = refs[-6:]

    @pl.when(pl.program_id(0) == 0)
    def _():
        base_ref[...] = jnp.zeros_like(base_ref)

    x = _mix_residual(x_ref[...], mod_ref, wout_ref, part_refs)
    x1_ref[...] = x
    h = _norm_mod(x, g_ref[...], mod_ref[0, 4:5, :], mod_ref[0, 3:4, :]).astype(BF16)
    h_ref[...] = h
    tm = h.shape[0]
    logits = jnp.dot(h, wr_ref[...], preferred_element_type=F32)
    lane = lax.broadcasted_iota(jnp.int32, logits.shape, 1)
    neg = jnp.float32(-jnp.inf)
    logits = jnp.where(lane < N_EXPERTS, logits, neg)
    m1 = jnp.max(logits, axis=-1, keepdims=True)
    i1 = jnp.min(jnp.where(logits == m1, lane, LANES), axis=-1, keepdims=True)
    rest = jnp.where(lane == i1, neg, logits)
    m2 = jnp.max(rest, axis=-1, keepdims=True)
    i2 = jnp.min(jnp.where(rest == m2, lane, LANES), axis=-1, keepdims=True)
    e2 = jnp.exp(m2 - m1)
    g1 = 1.0 / (1.0 + e2)
    g2 = e2 / (1.0 + e2)
    sel1, sel2 = lane == i1, lane == i2
    sel = jnp.where(sel1 | sel2, 1.0, 0.0)
    li = lax.broadcasted_iota(jnp.int32, (tm, tm), 0)
    si = lax.broadcasted_iota(jnp.int32, (tm, tm), 1)
    before = jnp.where(si < li, 1.0, 0.0).astype(BF16)
    rank = jnp.dot(before, sel.astype(BF16), preferred_element_type=F32) + base_ref[...]
    r1 = jnp.sum(jnp.where(sel1, rank, 0.0), axis=-1, keepdims=True)
    r2 = jnp.sum(jnp.where(sel2, rank, 0.0), axis=-1, keepdims=True)
    base_ref[...] += jnp.sum(sel, axis=0, keepdims=True)
    cnt_ref[...] = base_ref[...]
    fields = (i1.astype(F32), i2.astype(F32), r1, r2, g1, g2)
    route = jnp.zeros(logits.shape, F32)
    for k, v in enumerate(fields):
        route = jnp.where(lane == k, v, route)
    route_ref[...] = route
    route_t_ref[...] = route.T[:ROUTE_ROWS, :]


def moe_router(x2, parts, g, mod, w_out, w_router, tiles_per_batch, tm):
    m, d = x2.shape
    return pl.pallas_call(
        _router_kernel,
        out_shape=(jax.ShapeDtypeStruct((m, d), F32), jax.ShapeDtypeStruct((m, d), BF16),
                   jax.ShapeDtypeStruct((m, LANES), F32), jax.ShapeDtypeStruct((ROUTE_ROWS, m), F32),
                   jax.ShapeDtypeStruct((1, LANES), F32)),
        grid=(m // tm,),
        in_specs=[pl.BlockSpec((tm, d), lambda i: (i, 0)),
                  pl.BlockSpec((1, d), lambda i: (0, 0)),
                  pl.BlockSpec((1, 6, d), _mod_map(tiles_per_batch)),
                  _resident(w_out.shape, lambda i: (0, 0)),
                  pl.BlockSpec((d, LANES), lambda i: (0, 0))] + _part_specs(parts, tm),
        out_specs=(pl.BlockSpec((tm, d), lambda i: (i, 0)),
                   pl.BlockSpec((tm, d), lambda i: (i, 0)),
                   pl.BlockSpec((tm, LANES), lambda i: (i, 0)),
                   pl.BlockSpec((ROUTE_ROWS, tm), lambda i: (0, i)),
                   pl.BlockSpec((1, LANES), lambda i: (0, 0))),
        scratch_shapes=[pltpu.VMEM((1, LANES), F32)],
        compiler_params=_params("arbitrary"),
        name="moe_router",
    )(x2, g, mod, w_out, w_router, *parts)


def _moe_kernel(tile_ref, exp_ref, start_ref, end_ref, xg_ref, wgu_ref, wd_ref, y_ref, wgu_b, wd_b):
    w = pl.program_id(0)
    tm = xg_ref.shape[0]
    start, end = start_ref[w], end_ref[w]
    t0 = tile_ref[w] * tm

    @pl.when((w == 0) | (exp_ref[w] != exp_ref[jnp.maximum(w - 1, 0)]))
    def _():
        wgu_b[...] = wgu_ref[0].astype(BF16)
        wd_b[...] = wd_ref[0].astype(BF16)

    @pl.when(end > start)
    def _():
        y = _swiglu(xg_ref[...], lambda a, b: wgu_b[:, a:b], lambda a, b: wd_b[a:b, :], wd_b.shape[0])
        y = y.astype(y_ref.dtype)

        @pl.when(start == t0)
        def _():
            y_ref[...] = y

        @pl.when(start != t0)
        def _():
            row = t0 + lax.broadcasted_iota(jnp.int32, (tm, 1), 0)
            y_ref[...] = jnp.where(row >= start, y, y_ref[...])


def moe_grouped(item_tile, item_expert, item_start, item_end, xg, w_gu, w_down, tm):
    p, d = xg.shape
    grid_spec = pltpu.PrefetchScalarGridSpec(
        num_scalar_prefetch=4,
        grid=(item_tile.shape[0],),
        in_specs=[pl.BlockSpec((tm, d), lambda w, it, ie, s, e: (it[w], 0)),
                  _resident((1,) + w_gu.shape[1:], lambda w, it, ie, s, e: (ie[w], 0, 0)),
                  _resident((1,) + w_down.shape[1:], lambda w, it, ie, s, e: (ie[w], 0, 0))],
        out_specs=pl.BlockSpec((tm, d), lambda w, it, ie, s, e: (it[w], 0)),
        scratch_shapes=[pltpu.VMEM(w_gu.shape[1:], BF16), pltpu.VMEM(w_down.shape[1:], BF16)],
    )
    return pl.pallas_call(
        _moe_kernel,
        out_shape=jax.ShapeDtypeStruct((p, d), BF16),
        grid_spec=grid_spec,
        compiler_params=_params("arbitrary"),
        name="moe_grouped",
    )(item_tile, item_expert, item_start, item_end, xg, w_gu, w_down)


def _moe_combine_kernel(x_ref, ya_ref, yb_ref, route_ref, mod_ref, *refs):
    o_ref = refs[-1]
    r = route_ref[...]
    y = (r[:, ROUTE_G1:ROUTE_G1 + 1] * ya_ref[...].astype(F32) + r[:, ROUTE_G2:ROUTE_G2 + 1] * yb_ref[...].astype(F32))
    x = x_ref[...] + mod_ref[0, 5:6, :] * y
    if len(refs) == 2:
        ms = jnp.mean(x * x, axis=-1, keepdims=True)
        x = x * lax.rsqrt(ms + EPS) * refs[0][...]
    o_ref[...] = x


def moe_combine(x2, ya, yb, route, mod, tiles_per_batch, tm, g_final=None):
    m, d = x2.shape
    row = pl.BlockSpec((tm, d), lambda i: (i, 0))
    specs = [row, row, row, pl.BlockSpec((tm, LANES), lambda i: (i, 0)),
             pl.BlockSpec((1, 6, d), _mod_map(tiles_per_batch))]
    args = [x2, ya, yb, route, mod]
    if g_final is not None:
        specs.append(pl.BlockSpec((1, d), lambda i: (0, 0)))
        args.append(g_final)
    return pl.pallas_call(
        _moe_combine_kernel,
        out_shape=jax.ShapeDtypeStruct((m, d), F32),
        grid=(m // tm,),
        in_specs=specs,
        out_specs=row,
        compiler_params=_params("parallel"),
        name="moe_combine",
    )(*args)


def moe_block(x2, parts, g, mod, w_out, w_router, w_gu, w_down, tiles_per_batch, tm_tok, tm_grp, g_final=None):
    m, d = x2.shape
    x2, h, route, route_t, cnt = moe_router(x2, parts, g, mod, w_out, w_router, tiles_per_batch, tm_tok)
    cnt = cnt[0, :N_EXPERTS].astype(jnp.int32)
    p = m * TOP_K
    off = jnp.cumsum(cnt) - cnt
    field = lambda k: route_t[k].astype(jnp.int32)

    def row_of(e, r):
        for k in range(N_EXPERTS):
            r = r + jnp.where(e == k, off[k], 0)
        return r

    d1 = row_of(field(ROUTE_E1), field(ROUTE_R1))
    d2 = row_of(field(ROUTE_E2), field(ROUTE_R2))
    tok = jnp.arange(m, dtype=jnp.int32)
    _, src = lax.sort_key_val(jnp.concatenate([d1, d2]), jnp.concatenate([tok, tok]))
    n_row_tiles = p // tm_grp
    start = jnp.sort(jnp.concatenate([jnp.arange(n_row_tiles, dtype=jnp.int32) * tm_grp, off]))
    end = jnp.concatenate([start[1:], jnp.full((1,), p, jnp.int32)])
    item_tile = jnp.minimum(start // tm_grp, n_row_tiles - 1)
    item_expert = jnp.sum((off[None, :] <= start[:, None]).astype(jnp.int32), axis=1) - 1
    rows = lambda a, idx: a.at[idx].get(mode="promise_in_bounds")
    yg = moe_grouped(item_tile, item_expert, start, end, rows(h, src), w_gu, w_down, tm_grp)
    return moe_combine(x2, rows(yg, d1), rows(yg, d2), route, mod, tiles_per_batch, tm_tok, g_final)


def _final_norm_kernel(x_ref, g_ref, o_ref):
    x = x_ref[...]
    ms = jnp.mean(x * x, axis=-1, keepdims=True)
    o_ref[...] = x * lax.rsqrt(ms + EPS) * g_ref[...]


def final_norm(x2, g, tm):
    m, d = x2.shape
    return pl.pallas_call(
        _final_norm_kernel,
        out_shape=jax.ShapeDtypeStruct((m, d), F32),
        grid=(m // tm,),
        in_specs=[pl.BlockSpec((tm, d), lambda i: (i, 0)), pl.BlockSpec((1, d), lambda i: (0, 0))],
        out_specs=pl.BlockSpec((tm, d), lambda i: (i, 0)),
        compiler_params=_params("parallel"),
        name="final_norm",
    )(x2, g)


NA_QROWS = 4
NA_KROWS = NA_QROWS + WIN_R
NA_STEP_BLOCKS = 2
Q_COL, K_COL, V_COL = 4, 5, 6


def _na_key_start(j, rows):
    return np.clip(j * NA_QROWS - WIN_R // 2, 0, rows - NA_KROWS)


def _na_bias_index(rows):
    nblk = rows // NA_QROWS
    pats = []
    for j in range(nblk):
        start = _na_key_start(j, rows)
        r = j * NA_QROWS + np.arange(NA_QROWS)
        sr = np.clip(r - WIN_R // 2, 0, rows - WIN_R)
        kr = start + np.arange(NA_KROWS)
        rvalid = (kr[None, :] >= sr[:, None]) & (kr[None, :] < sr[:, None] + WIN_R)
        ri = np.clip(kr[None, :] - r[:, None] + WIN_R - 1, 0, 2 * WIN_R - 2)
        pats.append((ri, rvalid))
    for j in range(2, nblk - 1):
        assert all(np.array_equal(a, b) for a, b in zip(pats[1], pats[j]))
    sel = [pats[0], pats[1], pats[nblk - 1]]
    ri = np.stack([p[0] for p in sel])
    rvalid = np.stack([p[1] for p in sel])
    c = np.arange(GRID_W)
    sc = np.clip(c - WIN_C // 2, 0, GRID_W - WIN_C)
    cvalid = (c[None, :] >= sc[:, None]) & (c[None, :] < sc[:, None] + WIN_C)
    ci = np.clip(c[None, :] - c[:, None] + WIN_C - 1, 0, 2 * WIN_C - 2)
    c_onehot = (ci[..., None] == np.arange(2 * WIN_C - 1)).astype(np.float32)
    valid = rvalid[:, :, None, :, None] & cvalid[None, None, :, None, :]
    return ri, c_onehot, valid


def na_bias_table(rpb, rows):
    ri, c_onehot, valid = _na_bias_index(rows)
    toep = jnp.einsum('hrd,qkd->hrqk', rpb, c_onehot, precision=lax.Precision.HIGHEST)
    b = jnp.take(toep, ri.reshape(-1), axis=1).reshape((NA_HEADS,) + ri.shape + (GRID_W, GRID_W))
    b = jnp.transpose(b, (1, 0, 2, 4, 3, 5))
    b = jnp.where(valid[:, None], b, -jnp.inf)
    return b.reshape(3, NA_HEADS, NA_QROWS * GRID_W, NA_KROWS * GRID_W).astype(F32)


def _attend_heads(q, key_sets, bias_fn):
    n, w = q.shape
    lane = lax.broadcasted_iota(jnp.int32, (1, w), 1)
    head_masks = [(lane >= h * NA_HEAD_DIM) & (lane < (h + 1) * NA_HEAD_DIM) for h in range(NA_HEADS)]
    all_scores = []
    for h, mh in enumerate(head_masks):
        qh = jnp.where(mh, q, 0.0).astype(BF16)
        scores = []
        for i, (k, _) in enumerate(key_sets):
            s = lax.dot_general(qh, k, (((1,), (1,)), ((), ())), preferred_element_type=F32)
            b = bias_fn(i, h)
            scores.append(s if b is None else s + b)
        all_scores.append(scores)
    all_probs = []
    for scores in all_scores:
        m = scores[0].max(axis=-1, keepdims=True)
        for s in scores[1:]:
            m = jnp.maximum(m, s.max(axis=-1, keepdims=True))
        denom = jnp.zeros((n, 1), F32)
        probs = []
        for s in scores:
            p = jnp.exp(s - m)
            denom = denom + p.sum(axis=-1, keepdims=True)
            probs.append(p.astype(BF16))
        all_probs.append((probs, denom))
    out = jnp.zeros((n, w), F32)
    for mh, (probs, denom) in zip(head_masks, all_probs):
        acc = jnp.zeros((n, w), F32)
        for p, (_, v) in zip(probs, key_sets):
            acc = acc + jnp.dot(p, v, preferred_element_type=F32)
        out = out + jnp.where(mh, acc / denom, 0.0)
    return out


def _na_kernel(q_ref, k_ref, v_ref, kc_ref, vc_ref, bias_ref, o_ref, *, rows):
    nblk = rows // NA_QROWS
    nq, nk = NA_QROWS * GRID_W, NA_KROWS * GRID_W
    for s in range(NA_STEP_BLOCKS):
        j = pl.program_id(1) * NA_STEP_BLOCKS + s
        start = jnp.clip(j * NA_QROWS - WIN_R // 2, 0, rows - NA_KROWS)
        t0 = pl.multiple_of(start * GRID_W, GRID_W)
        pattern = jnp.where(j == 0, 0, jnp.where(j == nblk - 1, 2, 1))
        kw = k_ref[pl.ds(t0, nk), :]
        vw = v_ref[pl.ds(t0, nk), :]
        q = q_ref[s * nq:(s + 1) * nq, :] * (1.0 / math.sqrt(NA_HEAD_DIM))
        o = _attend_heads(q, [(kw, vw), (kc_ref[...], vc_ref[...])],
                          lambda i, h: bias_ref[pattern, h] if i == 0 else None)
        o_ref[s * nq:(s + 1) * nq, :] = o.astype(o_ref.dtype)


def na_attention(pr, pr_c, bias, bn):
    S, Lc = pr.shape[0] // bn, pr_c.shape[0] // bn
    rows = S // GRID_W
    nsteps = rows // NA_QROWS // NA_STEP_BLOCKS
    nq = NA_STEP_BLOCKS * NA_QROWS * GRID_W
    return pl.pallas_call(
        functools.partial(_na_kernel, rows=rows),
        out_shape=jax.ShapeDtypeStruct((bn * S, W_NA), BF16),
        grid=(bn, nsteps),
        in_specs=[pl.BlockSpec((nq, W_NA), lambda b, j: (b * nsteps + j, Q_COL)),
                  pl.BlockSpec((S, W_NA), lambda b, j: (b, K_COL)),
                  pl.BlockSpec((S, W_NA), lambda b, j: (b, V_COL)),
                  pl.BlockSpec((Lc, W_NA), lambda b, j: (b, K_COL)),
                  pl.BlockSpec((Lc, W_NA), lambda b, j: (b, V_COL)),
                  _resident(bias.shape, lambda b, j: (0, 0, 0, 0))],
        out_specs=pl.BlockSpec((nq, W_NA), lambda b, j: (b * nsteps + j, 0)),
        compiler_params=_params("parallel", "arbitrary"),
        name="na_attention",
    )(pr, pr, pr, pr_c, pr_c, bias)


def _ctx_attn_kernel(q_ref, k_ref, v_ref, o_ref):
    q = q_ref[...] * (1.0 / math.sqrt(NA_HEAD_DIM))
    o = _attend_heads(q, [(k_ref[...], v_ref[...])], lambda i, h: None)
    o_ref[...] = o.astype(o_ref.dtype)


def ctx_attention(pr_c, bn):
    Lc = pr_c.shape[0] // bn
    return pl.pallas_call(
        _ctx_attn_kernel,
        out_shape=jax.ShapeDtypeStruct((bn * Lc, W_NA), BF16),
        grid=(bn,),
        in_specs=[pl.BlockSpec((Lc, W_NA), lambda b: (b, Q_COL)),
                  pl.BlockSpec((Lc, W_NA), lambda b: (b, K_COL)),
                  pl.BlockSpec((Lc, W_NA), lambda b: (b, V_COL))],
        out_specs=pl.BlockSpec((Lc, W_NA), lambda b: (b, 0)),
        compiler_params=_params("parallel"),
        name="ctx_attention",
    )(pr_c, pr_c, pr_c)


POOL_COL, CONV_H_COL, CONV_B_COL, CONV_C_COL = 0, 1, 2, 3


def _shift_rows(x, k, row):
    n = x.shape[0]
    y = pltpu.roll(x, k % n, 0)
    return jnp.where(row < k, 0.0, y) if k > 0 else jnp.where(row >= n + k, 0.0, y)


def _local_mix_kernel(u_ref, h_ref, bg_ref, cg_ref, pw_ref, ps_ref, cw_ref, op_ref, oc_ref, *, seq):
    row = lax.broadcasted_iota(jnp.int32, (seq, 1), 0)
    lane = lax.broadcasted_iota(jnp.int32, (1, W_POOL), 1)
    u = u_ref[...].astype(F32)
    trailing, leading = {1: u}, {1: u}
    for w in (1, 2, 4):
        trailing[2 * w] = trailing[w] + _shift_rows(trailing[w], w, row)
        leading[2 * w] = leading[w] + _shift_rows(leading[w], -w, row)
    rowf = row.astype(F32)
    win_sum = jnp.zeros_like(u)
    cnt = jnp.zeros_like(u)
    for g, win in enumerate(POOL_WINDOWS):
        half = win // 2
        mg = (lane >= g * POOL_GROUP) & (lane < (g + 1) * POOL_GROUP)
        s = _shift_rows(trailing[half], 1, row) + leading[half]
        n = jnp.minimum(rowf + half, float(seq)) - jnp.maximum(rowf - half, 0.0)
        win_sum = jnp.where(mg, s, win_sum)
        cnt = jnp.where(mg, n, cnt)
    p = win_sum / cnt - u
    pooled = jnp.dot(p.astype(BF16), pw_ref[...], preferred_element_type=F32) * ps_ref[...]
    op_ref[...] = pooled.astype(op_ref.dtype)
    v = cg_ref[...].astype(F32) * h_ref[...].astype(F32)
    conv = (_shift_rows(v, 1, row) * cw_ref[0:1, :] + v * cw_ref[1:2, :] + _shift_rows(v, -1, row) * cw_ref[2:3, :])
    oc_ref[...] = (bg_ref[...].astype(F32) * conv).astype(oc_ref.dtype)


def local_mix(pr, pool_w, pool_scale, conv_w, bn):
    seq = pr.shape[0] // bn
    pw = jax.scipy.linalg.block_diag(*[pool_w[g] for g in range(len(POOL_WINDOWS))]).astype(BF16)
    blk = lambda col: pl.BlockSpec((seq, W_POOL), lambda b: (b, col))
    full = lambda a: pl.BlockSpec(a.shape, lambda b: (0,) * a.ndim)
    args = (pw, pool_scale[None, :], conv_w)
    return pl.pallas_call(
        functools.partial(_local_mix_kernel, seq=seq),
        out_shape=(jax.ShapeDtypeStruct((bn * seq, W_POOL), BF16), jax.ShapeDtypeStruct((bn * seq, W_CONV), BF16)),
        grid=(bn,),
        in_specs=[blk(POOL_COL), blk(CONV_H_COL), blk(CONV_B_COL), blk(CONV_C_COL)] + [full(a) for a in args],
        out_specs=(pl.BlockSpec((seq, W_POOL), lambda b: (b, 0)),
                   pl.BlockSpec((seq, W_CONV), lambda b: (b, 0))),
        compiler_params=_params("parallel"),
        name="local_mix",
    )(pr, pr, pr, pr, *args)


Z_COL, XS_COL, BS_COL, CS_COL = 7, 8, 9, 10
DT_COL = 22
HEADS_PER_GROUP = SSD_HEADS // SSD_GROUPS
GROUP_W = HEADS_PER_GROUP * SSD_HEAD_DIM


def _head_mask(h):
    lane = lax.broadcasted_iota(jnp.int32, (1, W_SSD), 1)
    return (lane >= h * SSD_HEAD_DIM) & (lane < (h + 1) * SSD_HEAD_DIM)


def _expand_heads(v, d):
    out = jnp.zeros((v.shape[0], W_SSD), F32)
    for h in range(SSD_HEADS):
        k = d * SSD_HEADS + h
        out = jnp.where(_head_mask(h), v[:, k:k + 1], out)
    return out


def _ssd_chunk_terms(c, xs_ref, b_ref, c_ref, dt_ref, la_ref, dsk_ref, y_ref, upd_ref, dec_ref, need_y):
    T = SSD_CHUNK
    r0 = pl.multiple_of(c * T, T)
    xs = xs_ref[pl.ds(r0, T), :]
    bm = b_ref[pl.ds(r0, T), :]
    cm = c_ref[pl.ds(r0, T), :]
    dt = dt_ref[pl.ds(r0, T), :]
    la = la_ref[pl.ds(r0, T), :]
    li = lax.broadcasted_iota(jnp.int32, (T, T), 0)
    si = lax.broadcasted_iota(jnp.int32, (T, T), 1)
    causal = si <= li
    prefix = jnp.dot(causal.astype(F32), la, precision=lax.Precision.HIGHEST, preferred_element_type=F32)
    total = prefix[T - 1:T, :]
    lane = lax.broadcasted_iota(jnp.int32, (1, LANES), 1)
    acum = jnp.where(lane < SSD_HEADS, prefix, total - prefix + la)
    bt = bm.astype(F32).T.astype(BF16)
    if need_y:
        acum_t = acum.T
        cb = [lax.dot_general(cm[:, g * SSD_STATE:(g + 1) * SSD_STATE], bm[:, g * SSD_STATE:(g + 1) * SSD_STATE],
                              (((1,), (1,)), ((), ())), preferred_element_type=F32) for g in range(SSD_GROUPS)]
        y = jnp.zeros((T, W_SSD), F32)
    per_dir = []
    for d in range(2):
        acum_e = _expand_heads(acum, d)
        tot_e = _expand_heads(total, d)
        xdt = xs * _expand_heads(dt, d)
        xw = (xdt * jnp.exp(tot_e - acum_e)).astype(BF16)
        dec_ref[c, d, 0:T, :] = jnp.exp(acum_e)
        dec_ref[c, d, T:T + 1, :] = jnp.exp(tot_e)
        scores = []
        if need_y:
            mask = causal if d == 0 else (si >= li)
            for h in range(SSD_HEADS):
                k = d * SSD_HEADS + h
                decay = jnp.exp(jnp.where(mask, acum[:, k:k + 1] - acum_t[k:k + 1, :], -jnp.inf))
                scores.append((cb[h // HEADS_PER_GROUP] * decay).astype(BF16))
        per_dir.append((xw, xdt, scores))
    for d, (xw, xdt, scores) in enumerate(per_dir):
        upd = [jnp.dot(bt[g * SSD_STATE:(g + 1) * SSD_STATE, :], xw[:, g * GROUP_W:(g + 1) * GROUP_W],
                       preferred_element_type=F32) for g in range(SSD_GROUPS)]
        upd_ref[c, d] = jnp.concatenate(upd, axis=1)
        if need_y:
            x_heads = [jnp.where(_head_mask(h), xdt, 0.0).astype(BF16) for h in range(SSD_HEADS)]
            y = y + jnp.dot(jnp.concatenate(scores, axis=1), jnp.concatenate(x_heads, axis=0),
                            preferred_element_type=F32)
    if need_y:
        y_ref[pl.ds(r0, T), :] = xs * dsk_ref[...] + y


def _ssd_chunk_state(c, d, c_ref, y_ref, upd_ref, dec_ref, st_ref, need_y):
    T = SSD_CHUNK
    r0 = pl.multiple_of(c * T, T)
    st = st_ref[d]
    if need_y:
        cm = c_ref[pl.ds(r0, T), :]
        st_b = st.astype(BF16)
        ys = [jnp.dot(cm[:, g * SSD_STATE:(g + 1) * SSD_STATE], st_b[:, g * GROUP_W:(g + 1) * GROUP_W],
                      preferred_element_type=F32) for g in range(SSD_GROUPS)]
        y_ref[pl.ds(r0, T), :] += jnp.concatenate(ys, axis=1) * dec_ref[c, d, 0:T, :]
    st_ref[d] = dec_ref[c, d, T:T + 1, :] * st + upd_ref[c, d]


def _ssd_kernel(z_ref, xs_in, bs_in, cs_in, dtr_ref, cw_ref, cb_ref, dtb_ref, alog_ref, dsk_ref, ng_ref, h0_ref,
                *refs, seq, need_y):
    if need_y:
        o_ref, hT_ref, xs_ref, b_ref, c_ref, dt_ref, la_ref, upd_ref, dec_ref, st_ref, y_ref = refs
    else:
        hT_ref, xs_ref, b_ref, c_ref, dt_ref, la_ref, upd_ref, dec_ref, st_ref = refs
        y_ref = None
    nc = seq // SSD_CHUNK
    row = lax.broadcasted_iota(jnp.int32, (seq, 1), 0)
    for gi, (src, dst) in enumerate(((xs_in, xs_ref), (bs_in, b_ref), (cs_in, c_ref))):
        sl = slice(gi * W_SSD, (gi + 1) * W_SSD)
        x = src[...].astype(F32)
        cv = (_shift_rows(x, 1, row) * cw_ref[0:1, sl] + x * cw_ref[1:2, sl]
              + _shift_rows(x, -1, row) * cw_ref[2:3, sl] + cb_ref[:, sl])
        dst[...] = (cv * jax.nn.sigmoid(cv)).astype(dst.dtype)
    dtv = dtr_ref[...] + dtb_ref[...]
    dt = jnp.maximum(dtv, 0.0) + jnp.log1p(jnp.exp(-jnp.abs(dtv)))
    dt_ref[...] = dt
    la_ref[...] = dt * (-jnp.exp(alog_ref[...]))
    st_ref[...] = h0_ref[0]

    def terms(c, carry):
        _ssd_chunk_terms(c, xs_ref, b_ref, c_ref, dt_ref, la_ref, dsk_ref, y_ref, upd_ref, dec_ref, need_y)
        return carry

    lax.fori_loop(0, nc, terms, 0, unroll=min(16, nc))

    def recur(i, carry):
        _ssd_chunk_state(i, 0, c_ref, y_ref, upd_ref, dec_ref, st_ref, need_y)
        _ssd_chunk_state(nc - 1 - i, 1, c_ref, y_ref, upd_ref, dec_ref, st_ref, need_y)
        return carry

    lax.fori_loop(0, nc, recur, 0, unroll=min(16, nc))
    hT_ref[0] = st_ref[...]
    if need_y:
        z = z_ref[...].astype(F32)
        yz = y_ref[...] * (z * jax.nn.sigmoid(z))
        ms = jnp.mean(yz * yz, axis=-1, keepdims=True)
        o_ref[...] = (yz * lax.rsqrt(ms + EPS) * ng_ref[...]).astype(o_ref.dtype)


def ssd_scan(pr, dt, h0, conv_w, conv_b, dt_bias, a_log, d_skip, norm_g, need_y):
    bn = h0.shape[0]
    seq = pr.shape[0] // bn
    nc = seq // SSD_CHUNK
    pad = LANES - 2 * SSD_HEADS
    dtb = jnp.pad(dt_bias.reshape(1, -1), ((0, 0), (0, pad)))
    alog = jnp.pad(a_log.reshape(1, -1), ((0, 0), (0, pad)))
    dsk = jnp.repeat(d_skip, SSD_HEAD_DIM)[None, :]
    blk = lambda col: pl.BlockSpec((seq, W_SSD), lambda b: (b, col))
    full = lambda a: pl.BlockSpec(a.shape, lambda b: (0,) * a.ndim)
    st_shape = (1, 2, SSD_STATE, W_SSD)
    st_spec = pl.BlockSpec(st_shape, lambda b: (b, 0, 0, 0))
    args = (conv_w, conv_b[None, :], dtb, alog, dsk, norm_g[None, :])
    out_shape = [jax.ShapeDtypeStruct((bn,) + st_shape[1:], F32)]
    out_specs = [st_spec]
    scratch = [pltpu.VMEM((seq, W_SSD), F32),
               pltpu.VMEM((seq, W_SSD), BF16), pltpu.VMEM((seq, W_SSD), BF16),
               pltpu.VMEM((seq, LANES), F32), pltpu.VMEM((seq, LANES), F32),
               pltpu.VMEM((nc, 2, SSD_STATE, W_SSD), F32),
               pltpu.VMEM((nc, 2, SSD_CHUNK + 8, W_SSD), F32),
               pltpu.VMEM(st_shape[1:], F32)]
    if need_y:
        out_shape.insert(0, jax.ShapeDtypeStruct((bn * seq, W_SSD), BF16))
        out_specs.insert(0, pl.BlockSpec((seq, W_SSD), lambda b: (b, 0)))
        scratch.append(pltpu.VMEM((seq, W_SSD), F32))
    res = pl.pallas_call(
        functools.partial(_ssd_kernel, seq=seq, need_y=need_y),
        out_shape=out_shape,
        grid=(bn,),
        in_specs=[blk(Z_COL), blk(XS_COL), blk(BS_COL), blk(CS_COL),
                  pl.BlockSpec((seq, LANES), lambda b: (b, 0))]
                 + [full(a) for a in args] + [st_spec],
        out_specs=out_specs,
        scratch_shapes=scratch,
        compiler_params=_params("parallel"),
        name="ssd_scan",
    )(pr, pr, pr, pr, dt, *args, h0)
    return (res[0], res[1]) if need_y else (None, res[0])


def ssd_mix(pr, dt, pr_c, dt_c, conv_w, conv_b, dt_bias, a_log, d_skip, norm_g, ctx_out, bn):
    h0 = jnp.zeros((bn, 2, SSD_STATE, W_SSD), F32)
    oc, hc = ssd_scan(pr_c, dt_c, h0, conv_w, conv_b, dt_bias, a_log, d_skip, norm_g, ctx_out)
    o, _ = ssd_scan(pr, dt, hc, conv_w, conv_b, dt_bias, a_log, d_skip, norm_g, True)
    return o, oc


def mixer(pr, dt, pr_c, dt_c, pool_w, pool_scale, conv_w, na_bias, s_conv_w, s_conv_b, dt_bias, a_log, d_skip,
          s_norm_g, ctx_out, bn):
    o_ssd, oc_ssd = ssd_mix(pr, dt, pr_c, dt_c, s_conv_w, s_conv_b, dt_bias, a_log, d_skip, s_norm_g, ctx_out, bn)
    o = [*local_mix(pr, pool_w, pool_scale, conv_w, bn), na_attention(pr, pr_c, na_bias, bn), o_ssd]
    if not ctx_out:
        return o, None
    oc = [*local_mix(pr_c, pool_w, pool_scale, conv_w, bn), ctx_attention(pr_c, bn), oc_ssd]
    return o, oc


TM = 512
TM_ROUTE = 1024
TM_PROJ = 1024
TM_GRP = 512


def kernel(x, c, ctx, c_ctx, w_ada, b_ada, g_mix, g_ffn, w_in, w_out, pool_w, pool_scale, conv_w, na_rpb,
           ssd_conv_w, ssd_conv_b, ssd_dt_bias, ssd_a_log, ssd_d, ssd_norm_g, ffn_w_gu, ffn_w_down,
           moe_router, moe_w_gu, moe_w_down, g_final):
    bn, S, d = x.shape
    Lc = ctx.shape[1]
    m, mc = bn * S, bn * Lc
    tpb = S // TM
    x2 = x.reshape(m, d)
    xc2 = ctx.reshape(mc, d)
    c_rows = jnp.concatenate([c, c_ctx[None, :], jnp.zeros((7, d), F32)], axis=0)
    for l in range(DEPTH):
        ctx_out = l < DEPTH - 1
        last = l == DEPTH - 1
        ada = ada_modulation(c_rows, w_ada[l].astype(BF16), b_ada[l][None, :])
        mod = ada[:bn].reshape(bn, 6, d)
        mod_c = ada[bn:bn + 1].reshape(1, 6, d)
        w_in_l = jnp.pad(w_in[l], ((0, 0), (0, D_IN_PAD - D_IN_PROJ))).astype(BF16)
        g_m = g_mix[l][None, :]
        g_f = g_ffn[l][None, :]
        pr, dt = in_proj(x2, g_m, mod, w_in_l, S // TM_PROJ, TM_PROJ)
        pr_c, dt_c = in_proj(xc2, g_m, mod_c, w_in_l, None, min(TM_PROJ, mc))
        na_bias = na_bias_table(na_rpb[l], S // GRID_W)
        o, oc = mixer(pr, dt, pr_c, dt_c, pool_w[l], pool_scale[l], conv_w[l], na_bias, ssd_conv_w[l],
                      ssd_conv_b[l], ssd_dt_bias[l], ssd_a_log[l], ssd_d[l], ssd_norm_g[l], ctx_out, bn)
        w_out_l = w_out[l].astype(BF16)
        j = l // 2
        if l % 2 == 0:
            w_gu = ffn_w_gu[j].astype(BF16)
            w_dn = ffn_w_down[j].astype(BF16)
            x2 = ffn_dense(x2, o, g_f, mod, w_out_l, w_gu, w_dn, tpb, TM)
            if ctx_out:
                xc2 = ffn_dense(xc2, oc, g_f, mod_c, w_out_l, w_gu, w_dn, None, min(TM, mc))
        else:
            w_r = jnp.pad(moe_router[j], ((0, 0), (0, LANES - N_EXPERTS))).astype(BF16)
            w_gu, w_dn = moe_w_gu[j], moe_w_down[j]
            x2 = moe_block(x2, o, g_f, mod, w_out_l, w_r, w_gu, w_dn, S // TM_ROUTE, TM_ROUTE, TM_GRP,
                           g_final[None, :] if last else None)
            if ctx_out:
                xc2 = moe_block(xc2, oc, g_f, mod_c, w_out_l, w_r, w_gu, w_dn, None, min(TM_ROUTE, mc), TM_GRP)
            if last:
                return x2.reshape(bn, S, d)
    return final_norm(x2, g_final[None, :], TM).reshape(bn, S, d)
```

```python
import functools
import math

import numpy as np
import jax
import jax.numpy as jnp
from jax import lax
from jax.experimental import pallas as pl
from jax.experimental.pallas import tpu as pltpu

DEPTH = 2
GRID_W = 64
EPS = 1e-6
W_POOL = 256
W_CONV = 256
W_NA = 256
W_SSD = 256
POOL_WINDOWS = (2, 4, 8, 16)
POOL_GROUP = W_POOL // len(POOL_WINDOWS)
NA_HEADS = 4
NA_HEAD_DIM = W_NA // NA_HEADS
WIN_R = 8
WIN_C = 16
SSD_HEAD_DIM = 64
SSD_HEADS = W_SSD // SSD_HEAD_DIM
SSD_GROUPS = 2
SSD_STATE = 128
SSD_CHUNK = 128
SSD_XBC = W_SSD + 2 * SSD_GROUPS * SSD_STATE
D_IN_PROJ = W_POOL + 3 * W_CONV + 3 * W_NA + W_SSD + SSD_XBC + 2 * SSD_HEADS
N_EXPERTS = 8
TOP_K = 2

LANES = 128
D_IN_PAD = -(-D_IN_PROJ // LANES) * LANES
VMEM_LIMIT = 56 * 1024 * 1024
BF16 = jnp.bfloat16
F32 = jnp.float32


def _params(*sem):
    return pltpu.CompilerParams(dimension_semantics=sem, vmem_limit_bytes=VMEM_LIMIT)


def _norm_mod(x, g, scale, shift):
    ms = jnp.mean(x * x, axis=-1, keepdims=True)
    return (x * lax.rsqrt(ms + EPS) * g) * (1.0 + scale) + shift


def _mod_map(tiles_per_batch):
    if tiles_per_batch is None:
        return lambda i, *_: (0, 0, 0)
    return lambda i, *_: (i // tiles_per_batch, 0, 0)


def _resident(shape, index_map):
    return pl.BlockSpec(shape, index_map, pipeline_mode=pl.Buffered(1))


def _ada_kernel(c_ref, w_ref, b_ref, o_ref):
    c = c_ref[...]
    s = c * jax.nn.sigmoid(c)
    o_ref[...] = jnp.dot(s.astype(BF16), w_ref[...], preferred_element_type=F32) + b_ref[...]


def ada_modulation(c_rows, w, b):
    r, d = c_rows.shape
    n = w.shape[1]
    tn = 1536
    return pl.pallas_call(
        _ada_kernel,
        out_shape=jax.ShapeDtypeStruct((r, n), F32),
        grid=(n // tn,),
        in_specs=[pl.BlockSpec((r, d), lambda j: (0, 0)),
                  pl.BlockSpec((d, tn), lambda j: (0, j)),
                  pl.BlockSpec((1, tn), lambda j: (0, j))],
        out_specs=pl.BlockSpec((r, tn), lambda j: (0, j)),
        compiler_params=_params("arbitrary"),
        name="ada_modulation",
    )(c_rows, w, b)


def _in_proj_kernel(x_ref, g_ref, mod_ref, w_ref, o_ref, dt_ref):
    h = _norm_mod(x_ref[...], g_ref[...], mod_ref[0, 1:2, :], mod_ref[0, 0:1, :])
    pr = jnp.dot(h.astype(BF16), w_ref[...], preferred_element_type=F32)
    o_ref[...] = pr.astype(o_ref.dtype)
    dt_ref[...] = pr[:, DT_COL * LANES:(DT_COL + 1) * LANES]


def in_proj(x2, g, mod, w, tiles_per_batch, tm):
    m, d = x2.shape
    n = w.shape[1]
    return pl.pallas_call(
        _in_proj_kernel,
        out_shape=(jax.ShapeDtypeStruct((m, n), BF16), jax.ShapeDtypeStruct((m, LANES), F32)),
        grid=(m // tm,),
        in_specs=[pl.BlockSpec((tm, d), lambda i: (i, 0)),
                  pl.BlockSpec((1, d), lambda i: (0, 0)),
                  pl.BlockSpec((1, 6, d), _mod_map(tiles_per_batch)),
                  _resident((d, n), lambda i: (0, 0))],
        out_specs=(pl.BlockSpec((tm, n), lambda i: (i, 0)), pl.BlockSpec((tm, LANES), lambda i: (i, 0))),
        compiler_params=_params("parallel"),
        name="in_proj",
    )(x2, g, mod, w)


def _mix_residual(x, mod_ref, w_ref, part_refs):
    y, k0 = None, 0
    for p_ref in part_refs:
        k = p_ref.shape[1]
        t = jnp.dot(p_ref[...], w_ref[k0:k0 + k, :], preferred_element_type=F32)
        y = t if y is None else y + t
        k0 += k
    return x + mod_ref[0, 2:3, :] * y


def _part_specs(parts, tm):
    return [pl.BlockSpec((tm, p.shape[1]), lambda i, *_: (i, 0)) for p in parts]


FF_CHUNK = 512


def _swiglu(h, wgu, wd, ff):
    acc = None
    for c0 in range(0, ff, FF_CHUNK):
        c1 = min(c0 + FF_CHUNK, ff)
        gg = jnp.dot(h, wgu(c0, c1), preferred_element_type=F32)
        uu = jnp.dot(h, wgu(ff + c0, ff + c1), preferred_element_type=F32)
        a = (gg * jax.nn.sigmoid(gg) * uu).astype(BF16)
        part = jnp.dot(a, wd(c0, c1), preferred_element_type=F32)
        acc = part if acc is None else acc + part
    return acc


def _ffn_kernel(x_ref, g_ref, mod_ref, wout_ref, wgu_ref, wd_ref, *refs):
    x = _mix_residual(x_ref[...], mod_ref, wout_ref, refs[:-1])
    y_ref = refs[-1]
    h = _norm_mod(x, g_ref[...], mod_ref[0, 4:5, :], mod_ref[0, 3:4, :]).astype(BF16)
    y = _swiglu(h, lambda a, b: wgu_ref[:, a:b], lambda a, b: wd_ref[a:b, :], wd_ref.shape[0])
    y_ref[...] = x + mod_ref[0, 5:6, :] * y


def ffn_dense(x2, parts, g, mod, w_out, w_gu, w_down, tiles_per_batch, tm):
    m, d = x2.shape
    return pl.pallas_call(
        _ffn_kernel,
        out_shape=jax.ShapeDtypeStruct((m, d), F32),
        grid=(m // tm,),
        in_specs=[pl.BlockSpec((tm, d), lambda i: (i, 0)),
                  pl.BlockSpec((1, d), lambda i: (0, 0)),
                  pl.BlockSpec((1, 6, d), _mod_map(tiles_per_batch)),
                  _resident(w_out.shape, lambda i: (0, 0)),
                  _resident(w_gu.shape, lambda i: (0, 0)),
                  _resident(w_down.shape, lambda i: (0, 0))] + _part_specs(parts, tm),
        out_specs=pl.BlockSpec((tm, d), lambda i: (i, 0)),
        compiler_params=_params("parallel"),
        name="ffn_dense",
    )(x2, g, mod, w_out, w_gu, w_down, *parts)


ROUTE_E1, ROUTE_E2, ROUTE_R1, ROUTE_R2, ROUTE_G1, ROUTE_G2 = range(6)
ROUTE_ROWS = 8


def _router_kernel(x_ref, g_ref, mod_ref, wout_ref, wr_ref, *refs):
    part_refs = refs[:-6]
    x1_ref, h_ref, route_ref, route_t_ref, cnt_ref, base_ref = refs[-6:]

    @pl.when(pl.program_id(0) == 0)
    def _():
        base_ref[...] = jnp.zeros_like(base_ref)

    x = _mix_residual(x_ref[...], mod_ref, wout_ref, part_refs)
    x1_ref[...] = x
    h = _norm_mod(x, g_ref[...], mod_ref[0, 4:5, :], mod_ref[0, 3:4, :]).astype(BF16)
    h_ref[...] = h
    tm = h.shape[0]
    logits = jnp.dot(h, wr_ref[...], preferred_element_type=F32)
    lane = lax.broadcasted_iota(jnp.int32, logits.shape, 1)
    neg = jnp.float32(-jnp.inf)
    logits = jnp.where(lane < N_EXPERTS, logits, neg)
    m1 = jnp.max(logits, axis=-1, keepdims=True)
    i1 = jnp.min(jnp.where(logits == m1, lane, LANES), axis=-1, keepdims=True)
    rest = jnp.where(lane == i1, neg, logits)
    m2 = jnp.max(rest, axis=-1, keepdims=True)
    i2 = jnp.min(jnp.where(rest == m2, lane, LANES), axis=-1, keepdims=True)
    e2 = jnp.exp(m2 - m1)
    g1 = 1.0 / (1.0 + e2)
    g2 = e2 / (1.0 + e2)
    sel1, sel2 = lane == i1, lane == i2
    sel = jnp.where(sel1 | sel2, 1.0, 0.0)
    li = lax.broadcasted_iota(jnp.int32, (tm, tm), 0)
    si = lax.broadcasted_iota(jnp.int32, (tm, tm), 1)
    before = jnp.where(si < li, 1.0, 0.0).astype(BF16)
    rank = jnp.dot(before, sel.astype(BF16), preferred_element_type=F32) + base_ref[...]
    r1 = jnp.sum(jnp.where(sel1, rank, 0.0), axis=-1, keepdims=True)
    r2 = jnp.sum(jnp.where(sel2, rank, 0.0), axis=-1, keepdims=True)
    base_ref[...] += jnp.sum(sel, axis=0, keepdims=True)
    cnt_ref[...] = base_ref[...]
    fields = (i1.astype(F32), i2.astype(F32), r1, r2, g1, g2)
    route = jnp.zeros(logits.shape, F32)
    for k, v in enumerate(fields):
        route = jnp.where(lane == k, v, route)
    route_ref[...] = route
    route_t_ref[...] = route.T[:ROUTE_ROWS, :]


def moe_router(x2, parts, g, mod, w_out, w_router, tiles_per_batch, tm):
    m, d = x2.shape
    return pl.pallas_call(
        _router_kernel,
        out_shape=(jax.ShapeDtypeStruct((m, d), F32), jax.ShapeDtypeStruct((m, d), BF16),
                   jax.ShapeDtypeStruct((m, LANES), F32), jax.ShapeDtypeStruct((ROUTE_ROWS, m), F32),
                   jax.ShapeDtypeStruct((1, LANES), F32)),
        grid=(m // tm,),
        in_specs=[pl.BlockSpec((tm, d), lambda i: (i, 0)),
                  pl.BlockSpec((1, d), lambda i: (0, 0)),
                  pl.BlockSpec((1, 6, d), _mod_map(tiles_per_batch)),
                  _resident(w_out.shape, lambda i: (0, 0)),
                  pl.BlockSpec((d, LANES), lambda i: (0, 0))] + _part_specs(parts, tm),
        out_specs=(pl.BlockSpec((tm, d), lambda i: (i, 0)),
                   pl.BlockSpec((tm, d), lambda i: (i, 0)),
                   pl.BlockSpec((tm, LANES), lambda i: (i, 0)),
                   pl.BlockSpec((ROUTE_ROWS, tm), lambda i: (0, i)),
                   pl.BlockSpec((1, LANES), lambda i: (0, 0))),
        scratch_shapes=[pltpu.VMEM((1, LANES), F32)],
        compiler_params=_params("arbitrary"),
        name="moe_router",
    )(x2, g, mod, w_out, w_router, *parts)


def _moe_kernel(tile_ref, exp_ref, start_ref, end_ref, xg_ref, wgu_ref, wd_ref, y_ref, wgu_b, wd_b):
    w = pl.program_id(0)
    tm = xg_ref.shape[0]
    start, end = start_ref[w], end_ref[w]
    t0 = tile_ref[w] * tm

    @pl.when((w == 0) | (exp_ref[w] != exp_ref[jnp.maximum(w - 1, 0)]))
    def _():
        wgu_b[...] = wgu_ref[0].astype(BF16)
        wd_b[...] = wd_ref[0].astype(BF16)

    @pl.when(end > start)
    def _():
        y = _swiglu(xg_ref[...], lambda a, b: wgu_b[:, a:b], lambda a, b: wd_b[a:b, :], wd_b.shape[0])
        y = y.astype(y_ref.dtype)

        @pl.when(start == t0)
        def _():
            y_ref[...] = y

        @pl.when(start != t0)
        def _():
            row = t0 + lax.broadcasted_iota(jnp.int32, (tm, 1), 0)
            y_ref[...] = jnp.where(row >= start, y, y_ref[...])


def moe_grouped(item_tile, item_expert, item_start, item_end, xg, w_gu, w_down, tm):
    p, d = xg.shape
    grid_spec = pltpu.PrefetchScalarGridSpec(
        num_scalar_prefetch=4,
        grid=(item_tile.shape[0],),
        in_specs=[pl.BlockSpec((tm, d), lambda w, it, ie, s, e: (it[w], 0)),
                  _resident((1,) + w_gu.shape[1:], lambda w, it, ie, s, e: (ie[w], 0, 0)),
                  _resident((1,) + w_down.shape[1:], lambda w, it, ie, s, e: (ie[w], 0, 0))],
        out_specs=pl.BlockSpec((tm, d), lambda w, it, ie, s, e: (it[w], 0)),
        scratch_shapes=[pltpu.VMEM(w_gu.shape[1:], BF16), pltpu.VMEM(w_down.shape[1:], BF16)],
    )
    return pl.pallas_call(
        _moe_kernel,
        out_shape=jax.ShapeDtypeStruct((p, d), BF16),
        grid_spec=grid_spec,
        compiler_params=_params("arbitrary"),
        name="moe_grouped",
    )(item_tile, item_expert, item_start, item_end, xg, w_gu, w_down)


def _moe_combine_kernel(x_ref, ya_ref, yb_ref, route_ref, mod_ref, *refs):
    o_ref = refs[-1]
    r = route_ref[...]
    y = (r[:, ROUTE_G1:ROUTE_G1 + 1] * ya_ref[...].astype(F32) + r[:, ROUTE_G2:ROUTE_G2 + 1] * yb_ref[...].astype(F32))
    x = x_ref[...] + mod_ref[0, 5:6, :] * y
    if len(refs) == 2:
        ms = jnp.mean(x * x, axis=-1, keepdims=True)
        x = x * lax.rsqrt(ms + EPS) * refs[0][...]
    o_ref[...] = x


def moe_combine(x2, ya, yb, route, mod, tiles_per_batch, tm, g_final=None):
    m, d = x2.shape
    row = pl.BlockSpec((tm, d), lambda i: (i, 0))
    specs = [row, row, row, pl.BlockSpec((tm, LANES), lambda i: (i, 0)),
             pl.BlockSpec((1, 6, d), _mod_map(tiles_per_batch))]
    args = [x2, ya, yb, route, mod]
    if g_final is not None:
        specs.append(pl.BlockSpec((1, d), lambda i: (0, 0)))
        args.append(g_final)
    return pl.pallas_call(
        _moe_combine_kernel,
        out_shape=jax.ShapeDtypeStruct((m, d), F32),
        grid=(m // tm,),
        in_specs=specs,
        out_specs=row,
        compiler_params=_params("parallel"),
        name="moe_combine",
    )(*args)


def moe_block(x2, parts, g, mod, w_out, w_router, w_gu, w_down, tiles_per_batch, tm_tok, tm_grp, g_final=None):
    m, d = x2.shape
    x2, h, route, route_t, cnt = moe_router(x2, parts, g, mod, w_out, w_router, tiles_per_batch, tm_tok)
    cnt = cnt[0, :N_EXPERTS].astype(jnp.int32)
    p = m * TOP_K
    off = jnp.cumsum(cnt) - cnt
    field = lambda k: route_t[k].astype(jnp.int32)

    def row_of(e, r):
        for k in range(N_EXPERTS):
            r = r + jnp.where(e == k, off[k], 0)
        return r

    d1 = row_of(field(ROUTE_E1), field(ROUTE_R1))
    d2 = row_of(field(ROUTE_E2), field(ROUTE_R2))
    tok = jnp.arange(m, dtype=jnp.int32)
    _, src = lax.sort_key_val(jnp.concatenate([d1, d2]), jnp.concatenate([tok, tok]))
    n_row_tiles = p // tm_grp
    start = jnp.sort(jnp.concatenate([jnp.arange(n_row_tiles, dtype=jnp.int32) * tm_grp, off]))
    end = jnp.concatenate([start[1:], jnp.full((1,), p, jnp.int32)])
    item_tile = jnp.minimum(start // tm_grp, n_row_tiles - 1)
    item_expert = jnp.sum((off[None, :] <= start[:, None]).astype(jnp.int32), axis=1) - 1
    rows = lambda a, idx: a.at[idx].get(mode="promise_in_bounds")
    yg = moe_grouped(item_tile, item_expert, start, end, rows(h, src), w_gu, w_down, tm_grp)
    return moe_combine(x2, rows(yg, d1), rows(yg, d2), route, mod, tiles_per_batch, tm_tok, g_final)


def _final_norm_kernel(x_ref, g_ref, o_ref):
    x = x_ref[...]
    ms = jnp.mean(x * x, axis=-1, keepdims=True)
    o_ref[...] = x * lax.rsqrt(ms + EPS) * g_ref[...]


def final_norm(x2, g, tm):
    m, d = x2.shape
    return pl.pallas_call(
        _final_norm_kernel,
        out_shape=jax.ShapeDtypeStruct((m, d), F32),
        grid=(m // tm,),
        in_specs=[pl.BlockSpec((tm, d), lambda i: (i, 0)), pl.BlockSpec((1, d), lambda i: (0, 0))],
        out_specs=pl.BlockSpec((tm, d), lambda i: (i, 0)),
        compiler_params=_params("parallel"),
        name="final_norm",
    )(x2, g)


NA_QROWS = 4
NA_KROWS = NA_QROWS + WIN_R
NA_STEP_BLOCKS = 4
Q_COL, K_COL, V_COL = 4, 5, 6


def _na_key_start(j, rows):
    return np.clip(j * NA_QROWS - WIN_R // 2, 0, rows - NA_KROWS)


def _na_bias_index(rows):
    nblk = rows // NA_QROWS
    pats = []
    for j in range(nblk):
        start = _na_key_start(j, rows)
        r = j * NA_QROWS + np.arange(NA_QROWS)
        sr = np.clip(r - WIN_R // 2, 0, rows - WIN_R)
        kr = start + np.arange(NA_KROWS)
        rvalid = (kr[None, :] >= sr[:, None]) & (kr[None, :] < sr[:, None] + WIN_R)
        ri = np.clip(kr[None, :] - r[:, None] + WIN_R - 1, 0, 2 * WIN_R - 2)
        pats.append((ri, rvalid))
    for j in range(2, nblk - 1):
        assert all(np.array_equal(a, b) for a, b in zip(pats[1], pats[j]))
    sel = [pats[0], pats[1], pats[nblk - 1]]
    ri = np.stack([p[0] for p in sel])
    rvalid = np.stack([p[1] for p in sel])
    c = np.arange(GRID_W)
    sc = np.clip(c - WIN_C // 2, 0, GRID_W - WIN_C)
    cvalid = (c[None, :] >= sc[:, None]) & (c[None, :] < sc[:, None] + WIN_C)
    ci = np.clip(c[None, :] - c[:, None] + WIN_C - 1, 0, 2 * WIN_C - 2)
    c_onehot = (ci[..., None] == np.arange(2 * WIN_C - 1)).astype(np.float32)
    valid = rvalid[:, :, None, :, None] & cvalid[None, None, :, None, :]
    return ri, c_onehot, valid


def na_bias_table(rpb, rows):
    ri, c_onehot, valid = _na_bias_index(rows)
    toep = jnp.einsum('hrd,qkd->hrqk', rpb, c_onehot, precision=lax.Precision.HIGHEST)
    b = jnp.take(toep, ri.reshape(-1), axis=1).reshape((NA_HEADS,) + ri.shape + (GRID_W, GRID_W))
    b = jnp.transpose(b, (1, 0, 2, 4, 3, 5))
    b = jnp.where(valid[:, None], b, -jnp.inf)
    return b.reshape(3, NA_HEADS, NA_QROWS * GRID_W, NA_KROWS * GRID_W).astype(F32)


def _attend_heads(q, key_sets, bias_fn):
    n, w = q.shape
    lane = lax.broadcasted_iota(jnp.int32, (1, w), 1)
    head_masks = [(lane >= h * NA_HEAD_DIM) & (lane < (h + 1) * NA_HEAD_DIM) for h in range(NA_HEADS)]
    all_scores = []
    for h, mh in enumerate(head_masks):
        qh = jnp.where(mh, q, 0.0).astype(BF16)
        scores = []
        for i, (k, _) in enumerate(key_sets):
            s = lax.dot_general(qh, k, (((1,), (1,)), ((), ())), preferred_element_type=F32)
            b = bias_fn(i, h)
            scores.append(s if b is None else s + b)
        all_scores.append(scores)
    all_probs = []
    for scores in all_scores:
        m = scores[0].max(axis=-1, keepdims=True)
        for s in scores[1:]:
            m = jnp.maximum(m, s.max(axis=-1, keepdims=True))
        denom = jnp.zeros((n, 1), F32)
        probs = []
        for s in scores:
            p = jnp.exp(s - m)
            denom = denom + p.sum(axis=-1, keepdims=True)
            probs.append(p.astype(BF16))
        all_probs.append((probs, denom))
    out = jnp.zeros((n, w), F32)
    for mh, (probs, denom) in zip(head_masks, all_probs):
        acc = jnp.zeros((n, w), F32)
        for p, (_, v) in zip(probs, key_sets):
            acc = acc + jnp.dot(p, v, preferred_element_type=F32)
        out = out + jnp.where(mh, acc / denom, 0.0)
    return out


def _na_kernel(q_ref, k_ref, v_ref, kc_ref, vc_ref, bias_ref, o_ref, *, rows):
    nblk = rows // NA_QROWS
    nq, nk = NA_QROWS * GRID_W, NA_KROWS * GRID_W
    for s in range(NA_STEP_BLOCKS):
        j = pl.program_id(1) * NA_STEP_BLOCKS + s
        start = jnp.clip(j * NA_QROWS - WIN_R // 2, 0, rows - NA_KROWS)
        t0 = pl.multiple_of(start * GRID_W, GRID_W)
        pattern = jnp.where(j == 0, 0, jnp.where(j == nblk - 1, 2, 1))
        kw = k_ref[pl.ds(t0, nk), :]
        vw = v_ref[pl.ds(t0, nk), :]
        q = q_ref[s * nq:(s + 1) * nq, :] * (1.0 / math.sqrt(NA_HEAD_DIM))
        o = _attend_heads(q, [(kw, vw), (kc_ref[...], vc_ref[...])],
                          lambda i, h: bias_ref[pattern, h] if i == 0 else None)
        o_ref[s * nq:(s + 1) * nq, :] = o.astype(o_ref.dtype)


def na_attention(pr, pr_c, bias, bn):
    S, Lc = pr.shape[0] // bn, pr_c.shape[0] // bn
    rows = S // GRID_W
    nsteps = rows // NA_QROWS // NA_STEP_BLOCKS
    nq = NA_STEP_BLOCKS * NA_QROWS * GRID_W
    return pl.pallas_call(
        functools.partial(_na_kernel, rows=rows),
        out_shape=jax.ShapeDtypeStruct((bn * S, W_NA), BF16),
        grid=(bn, nsteps),
        in_specs=[pl.BlockSpec((nq, W_NA), lambda b, j: (b * nsteps + j, Q_COL)),
                  pl.BlockSpec((S, W_NA), lambda b, j: (b, K_COL)),
                  pl.BlockSpec((S, W_NA), lambda b, j: (b, V_COL)),
                  pl.BlockSpec((Lc, W_NA), lambda b, j: (b, K_COL)),
                  pl.BlockSpec((Lc, W_NA), lambda b, j: (b, V_COL)),
                  _resident(bias.shape, lambda b, j: (0, 0, 0, 0))],
        out_specs=pl.BlockSpec((nq, W_NA), lambda b, j: (b * nsteps + j, 0)),
        compiler_params=_params("parallel", "arbitrary"),
        name="na_attention",
    )(pr, pr, pr, pr_c, pr_c, bias)


def _ctx_attn_kernel(q_ref, k_ref, v_ref, o_ref):
    q = q_ref[...] * (1.0 / math.sqrt(NA_HEAD_DIM))
    o = _attend_heads(q, [(k_ref[...], v_ref[...])], lambda i, h: None)
    o_ref[...] = o.astype(o_ref.dtype)


def ctx_attention(pr_c, bn):
    Lc = pr_c.shape[0] // bn
    return pl.pallas_call(
        _ctx_attn_kernel,
        out_shape=jax.ShapeDtypeStruct((bn * Lc, W_NA), BF16),
        grid=(bn,),
        in_specs=[pl.BlockSpec((Lc, W_NA), lambda b: (b, Q_COL)),
                  pl.BlockSpec((Lc, W_NA), lambda b: (b, K_COL)),
                  pl.BlockSpec((Lc, W_NA), lambda b: (b, V_COL))],
        out_specs=pl.BlockSpec((Lc, W_NA), lambda b: (b, 0)),
        compiler_params=_params("parallel"),
        name="ctx_attention",
    )(pr_c, pr_c, pr_c)


POOL_COL, CONV_H_COL, CONV_B_COL, CONV_C_COL = 0, 1, 2, 3


def _shift_rows(x, k, row):
    n = x.shape[0]
    y = pltpu.roll(x, k % n, 0)
    return jnp.where(row < k, 0.0, y) if k > 0 else jnp.where(row >= n + k, 0.0, y)


def _local_mix_kernel(u_ref, h_ref, bg_ref, cg_ref, pw_ref, ps_ref, cw_ref, op_ref, oc_ref, *, seq):
    row = lax.broadcasted_iota(jnp.int32, (seq, 1), 0)
    lane = lax.broadcasted_iota(jnp.int32, (1, W_POOL), 1)
    u = u_ref[...].astype(F32)
    trailing, leading = {1: u}, {1: u}
    for w in (1, 2, 4):
        trailing[2 * w] = trailing[w] + _shift_rows(trailing[w], w, row)
        leading[2 * w] = leading[w] + _shift_rows(leading[w], -w, row)
    rowf = row.astype(F32)
    win_sum = jnp.zeros_like(u)
    cnt = jnp.zeros_like(u)
    for g, win in enumerate(POOL_WINDOWS):
        half = win // 2
        mg = (lane >= g * POOL_GROUP) & (lane < (g + 1) * POOL_GROUP)
        s = _shift_rows(trailing[half], 1, row) + leading[half]
        n = jnp.minimum(rowf + half, float(seq)) - jnp.maximum(rowf - half, 0.0)
        win_sum = jnp.where(mg, s, win_sum)
        cnt = jnp.where(mg, n, cnt)
    p = win_sum / cnt - u
    pooled = jnp.dot(p.astype(BF16), pw_ref[...], preferred_element_type=F32) * ps_ref[...]
    op_ref[...] = pooled.astype(op_ref.dtype)
    v = cg_ref[...].astype(F32) * h_ref[...].astype(F32)
    conv = (_shift_rows(v, 1, row) * cw_ref[0:1, :] + v * cw_ref[1:2, :] + _shift_rows(v, -1, row) * cw_ref[2:3, :])
    oc_ref[...] = (bg_ref[...].astype(F32) * conv).astype(oc_ref.dtype)


def local_mix(pr, pool_w, pool_scale, conv_w, bn):
    seq = pr.shape[0] // bn
    pw = jax.scipy.linalg.block_diag(*[pool_w[g] for g in range(len(POOL_WINDOWS))]).astype(BF16)
    blk = lambda col: pl.BlockSpec((seq, W_POOL), lambda b: (b, col))
    full = lambda a: pl.BlockSpec(a.shape, lambda b: (0,) * a.ndim)
    args = (pw, pool_scale[None, :], conv_w)
    return pl.pallas_call(
        functools.partial(_local_mix_kernel, seq=seq),
        out_shape=(jax.ShapeDtypeStruct((bn * seq, W_POOL), BF16), jax.ShapeDtypeStruct((bn * seq, W_CONV), BF16)),
        grid=(bn,),
        in_specs=[blk(POOL_COL), blk(CONV_H_COL), blk(CONV_B_COL), blk(CONV_C_COL)] + [full(a) for a in args],
        out_specs=(pl.BlockSpec((seq, W_POOL), lambda b: (b, 0)),
                   pl.BlockSpec((seq, W_CONV), lambda b: (b, 0))),
        compiler_params=_params("parallel"),
        name="local_mix",
    )(pr, pr, pr, pr, *args)


Z_COL, XS_COL, BS_COL, CS_COL = 7, 8, 9, 10
DT_COL = 22
HEADS_PER_GROUP = SSD_HEADS // SSD_GROUPS
GROUP_W = HEADS_PER_GROUP * SSD_HEAD_DIM


def _head_mask(h):
    lane = lax.broadcasted_iota(jnp.int32, (1, W_SSD), 1)
    return (lane >= h * SSD_HEAD_DIM) & (lane < (h + 1) * SSD_HEAD_DIM)


def _expand_heads(v, d):
    out = jnp.zeros((v.shape[0], W_SSD), F32)
    for h in range(SSD_HEADS):
        k = d * SSD_HEADS + h
        out = jnp.where(_head_mask(h), v[:, k:k + 1], out)
    return out


def _ssd_chunk_terms(c, xs_ref, b_ref, c_ref, dt_ref, la_ref, dsk_ref, y_ref, upd_ref, dec_ref, need_y):
    T = SSD_CHUNK
    r0 = pl.multiple_of(c * T, T)
    xs = xs_ref[pl.ds(r0, T), :]
    bm = b_ref[pl.ds(r0, T), :]
    cm = c_ref[pl.ds(r0, T), :]
    dt = dt_ref[pl.ds(r0, T), :]
    la = la_ref[pl.ds(r0, T), :]
    li = lax.broadcasted_iota(jnp.int32, (T, T), 0)
    si = lax.broadcasted_iota(jnp.int32, (T, T), 1)
    causal = si <= li
    prefix = jnp.dot(causal.astype(F32), la, precision=lax.Precision.HIGHEST, preferred_element_type=F32)
    total = prefix[T - 1:T, :]
    lane = lax.broadcasted_iota(jnp.int32, (1, LANES), 1)
    acum = jnp.where(lane < SSD_HEADS, prefix, total - prefix + la)
    bt = bm.astype(F32).T.astype(BF16)
    if need_y:
        acum_t = acum.T
        cb = [lax.dot_general(cm[:, g * SSD_STATE:(g + 1) * SSD_STATE], bm[:, g * SSD_STATE:(g + 1) * SSD_STATE],
                              (((1,), (1,)), ((), ())), preferred_element_type=F32) for g in range(SSD_GROUPS)]
        y = jnp.zeros((T, W_SSD), F32)
    per_dir = []
    for d in range(2):
        acum_e = _expand_heads(acum, d)
        tot_e = _expand_heads(total, d)
        xdt = xs * _expand_heads(dt, d)
        xw = (xdt * jnp.exp(tot_e - acum_e)).astype(BF16)
        dec_ref[c, d, 0:T, :] = jnp.exp(acum_e)
        dec_ref[c, d, T:T + 1, :] = jnp.exp(tot_e)
        scores = []
        if need_y:
            mask = causal if d == 0 else (si >= li)
            for h in range(SSD_HEADS):
                k = d * SSD_HEADS + h
                decay = jnp.exp(jnp.where(mask, acum[:, k:k + 1] - acum_t[k:k + 1, :], -jnp.inf))
                scores.append((cb[h // HEADS_PER_GROUP] * decay).astype(BF16))
        per_dir.append((xw, xdt, scores))
    for d, (xw, xdt, scores) in enumerate(per_dir):
        upd = [jnp.dot(bt[g * SSD_STATE:(g + 1) * SSD_STATE, :], xw[:, g * GROUP_W:(g + 1) * GROUP_W],
                       preferred_element_type=F32) for g in range(SSD_GROUPS)]
        upd_ref[c, d] = jnp.concatenate(upd, axis=1)
        if need_y:
            x_heads = [jnp.where(_head_mask(h), xdt, 0.0).astype(BF16) for h in range(SSD_HEADS)]
            y = y + jnp.dot(jnp.concatenate(scores, axis=1), jnp.concatenate(x_heads, axis=0),
                            preferred_element_type=F32)
    if need_y:
        y_ref[pl.ds(r0, T), :] = xs * dsk_ref[...] + y


def _ssd_chunk_state(c, d, c_ref, y_ref, upd_ref, dec_ref, st_ref, need_y):
    T = SSD_CHUNK
    r0 = pl.multiple_of(c * T, T)
    st = st_ref[d]
    if need_y:
        cm = c_ref[pl.ds(r0, T), :]
        st_b = st.astype(BF16)
        ys = [jnp.dot(cm[:, g * SSD_STATE:(g + 1) * SSD_STATE], st_b[:, g * GROUP_W:(g + 1) * GROUP_W],
                      preferred_element_type=F32) for g in range(SSD_GROUPS)]
        y_ref[pl.ds(r0, T), :] += jnp.concatenate(ys, axis=1) * dec_ref[c, d, 0:T, :]
    st_ref[d] = dec_ref[c, d, T:T + 1, :] * st + upd_ref[c, d]


def _ssd_kernel(z_ref, xs_in, bs_in, cs_in, dtr_ref, cw_ref, cb_ref, dtb_ref, alog_ref, dsk_ref, ng_ref, h0_ref,
                *refs, seq, need_y):
    if need_y:
        o_ref, hT_ref, xs_ref, b_ref, c_ref, dt_ref, la_ref, upd_ref, dec_ref, st_ref, y_ref = refs
    else:
        hT_ref, xs_ref, b_ref, c_ref, dt_ref, la_ref, upd_ref, dec_ref, st_ref = refs
        y_ref = None
    nc = seq // SSD_CHUNK
    row = lax.broadcasted_iota(jnp.int32, (seq, 1), 0)
    for gi, (src, dst) in enumerate(((xs_in, xs_ref), (bs_in, b_ref), (cs_in, c_ref))):
        sl = slice(gi * W_SSD, (gi + 1) * W_SSD)
        x = src[...].astype(F32)
        cv = (_shift_rows(x, 1, row) * cw_ref[0:1, sl] + x * cw_ref[1:2, sl]
              + _shift_rows(x, -1, row) * cw_ref[2:3, sl] + cb_ref[:, sl])
        dst[...] = (cv * jax.nn.sigmoid(cv)).astype(dst.dtype)
    dtv = dtr_ref[...] + dtb_ref[...]
    dt = jnp.maximum(dtv, 0.0) + jnp.log1p(jnp.exp(-jnp.abs(dtv)))
    dt_ref[...] = dt
    la_ref[...] = dt * (-jnp.exp(alog_ref[...]))
    st_ref[...] = h0_ref[0]

    def terms(c, carry):
        _ssd_chunk_terms(c, xs_ref, b_ref, c_ref, dt_ref, la_ref, dsk_ref, y_ref, upd_ref, dec_ref, need_y)
        return carry

    lax.fori_loop(0, nc, terms, 0, unroll=min(16, nc))

    def recur(i, carry):
        _ssd_chunk_state(i, 0, c_ref, y_ref, upd_ref, dec_ref, st_ref, need_y)
        _ssd_chunk_state(nc - 1 - i, 1, c_ref, y_ref, upd_ref, dec_ref, st_ref, need_y)
        return carry

    lax.fori_loop(0, nc, recur, 0, unroll=min(16, nc))
    hT_ref[0] = st_ref[...]
    if need_y:
        z = z_ref[...].astype(F32)
        yz = y_ref[...] * (z * jax.nn.sigmoid(z))
        ms = jnp.mean(yz * yz, axis=-1, keepdims=True)
        o_ref[...] = (yz * lax.rsqrt(ms + EPS) * ng_ref[...]).astype(o_ref.dtype)


def ssd_scan(pr, dt, h0, conv_w, conv_b, dt_bias, a_log, d_skip, norm_g, need_y):
    bn = h0.shape[0]
    seq = pr.shape[0] // bn
    nc = seq // SSD_CHUNK
    pad = LANES - 2 * SSD_HEADS
    dtb = jnp.pad(dt_bias.reshape(1, -1), ((0, 0), (0, pad)))
    alog = jnp.pad(a_log.reshape(1, -1), ((0, 0), (0, pad)))
    dsk = jnp.repeat(d_skip, SSD_HEAD_DIM)[None, :]
    blk = lambda col: pl.BlockSpec((seq, W_SSD), lambda b: (b, col))
    full = lambda a: pl.BlockSpec(a.shape, lambda b: (0,) * a.ndim)
    st_shape = (1, 2, SSD_STATE, W_SSD)
    st_spec = pl.BlockSpec(st_shape, lambda b: (b, 0, 0, 0))
    args = (conv_w, conv_b[None, :], dtb, alog, dsk, norm_g[None, :])
    out_shape = [jax.ShapeDtypeStruct((bn,) + st_shape[1:], F32)]
    out_specs = [st_spec]
    scratch = [pltpu.VMEM((seq, W_SSD), F32),
               pltpu.VMEM((seq, W_SSD), BF16), pltpu.VMEM((seq, W_SSD), BF16),
               pltpu.VMEM((seq, LANES), F32), pltpu.VMEM((seq, LANES), F32),
               pltpu.VMEM((nc, 2, SSD_STATE, W_SSD), F32),
               pltpu.VMEM((nc, 2, SSD_CHUNK + 8, W_SSD), F32),
               pltpu.VMEM(st_shape[1:], F32)]
    if need_y:
        out_shape.insert(0, jax.ShapeDtypeStruct((bn * seq, W_SSD), BF16))
        out_specs.insert(0, pl.BlockSpec((seq, W_SSD), lambda b: (b, 0)))
        scratch.append(pltpu.VMEM((seq, W_SSD), F32))
    res = pl.pallas_call(
        functools.partial(_ssd_kernel, seq=seq, need_y=need_y),
        out_shape=out_shape,
        grid=(bn,),
        in_specs=[blk(Z_COL), blk(XS_COL), blk(BS_COL), blk(CS_COL),
                  pl.BlockSpec((seq, LANES), lambda b: (b, 0))]
                 + [full(a) for a in args] + [st_spec],
        out_specs=out_specs,
        scratch_shapes=scratch,
        compiler_params=_params("parallel"),
        name="ssd_scan",
    )(pr, pr, pr, pr, dt, *args, h0)
    return (res[0], res[1]) if need_y else (None, res[0])


def ssd_mix(pr, dt, pr_c, dt_c, conv_w, conv_b, dt_bias, a_log, d_skip, norm_g, ctx_out, bn):
    h0 = jnp.zeros((bn, 2, SSD_STATE, W_SSD), F32)
    oc, hc = ssd_scan(pr_c, dt_c, h0, conv_w, conv_b, dt_bias, a_log, d_skip, norm_g, ctx_out)
    o, _ = ssd_scan(pr, dt, hc, conv_w, conv_b, dt_bias, a_log, d_skip, norm_g, True)
    return o, oc


def mixer(pr, dt, pr_c, dt_c, pool_w, pool_scale, conv_w, na_bias, s_conv_w, s_conv_b, dt_bias, a_log, d_skip,
          s_norm_g, ctx_out, bn):
    o_ssd, oc_ssd = ssd_mix(pr, dt, pr_c, dt_c, s_conv_w, s_conv_b, dt_bias, a_log, d_skip, s_norm_g, ctx_out, bn)
    o = [*local_mix(pr, pool_w, pool_scale, conv_w, bn), na_attention(pr, pr_c, na_bias, bn), o_ssd]
    if not ctx_out:
        return o, None
    oc = [*local_mix(pr_c, pool_w, pool_scale, conv_w, bn), ctx_attention(pr_c, bn), oc_ssd]
    return o, oc


TM = 512
TM_ROUTE = 1024
TM_PROJ = 1024
TM_GRP = 512


def kernel(x, c, ctx, c_ctx, w_ada, b_ada, g_mix, g_ffn, w_in, w_out, pool_w, pool_scale, conv_w, na_rpb,
           ssd_conv_w, ssd_conv_b, ssd_dt_bias, ssd_a_log, ssd_d, ssd_norm_g, ffn_w_gu, ffn_w_down,
           moe_router, moe_w_gu, moe_w_down, g_final):
    bn, S, d = x.shape
    Lc = ctx.shape[1]
    m, mc = bn * S, bn * Lc
    tpb = S // TM
    x2 = x.reshape(m, d)
    xc2 = ctx.reshape(mc, d)
    c_rows = jnp.concatenate([c, c_ctx[None, :], jnp.zeros((7, d), F32)], axis=0)
    for l in range(DEPTH):
        ctx_out = l < DEPTH - 1
        last = l == DEPTH - 1
        ada = ada_modulation(c_rows, w_ada[l].astype(BF16), b_ada[l][None, :])
        mod = ada[:bn].reshape(bn, 6, d)
        mod_c = ada[bn:bn + 1].reshape(1, 6, d)
        w_in_l = jnp.pad(w_in[l], ((0, 0), (0, D_IN_PAD - D_IN_PROJ))).astype(BF16)
        g_m = g_mix[l][None, :]
        g_f = g_ffn[l][None, :]
        pr, dt = in_proj(x2, g_m, mod, w_in_l, S // TM_PROJ, TM_PROJ)
        pr_c, dt_c = in_proj(xc2, g_m, mod_c, w_in_l, None, min(TM_PROJ, mc))
        na_bias = na_bias_table(na_rpb[l], S // GRID_W)
        o, oc = mixer(pr, dt, pr_c, dt_c, pool_w[l], pool_scale[l], conv_w[l], na_bias, ssd_conv_w[l],
                      ssd_conv_b[l], ssd_dt_bias[l], ssd_a_log[l], ssd_d[l], ssd_norm_g[l], ctx_out, bn)
        w_out_l = w_out[l].astype(BF16)
        j = l // 2
        if l % 2 == 0:
            w_gu = ffn_w_gu[j].astype(BF16)
            w_dn = ffn_w_down[j].astype(BF16)
            x2 = ffn_dense(x2, o, g_f, mod, w_out_l, w_gu, w_dn, tpb, TM)
            if ctx_out:
                xc2 = ffn_dense(xc2, oc, g_f, mod_c, w_out_l, w_gu, w_dn, None, min(TM, mc))
        else:
            w_r = jnp.pad(moe_router[j], ((0, 0), (0, LANES - N_EXPERTS))).astype(BF16)
            w_gu, w_dn = moe_w_gu[j], moe_w_down[j]
            x2 = moe_block(x2, o, g_f, mod, w_out_l, w_r, w_gu, w_dn, S // TM_ROUTE, TM_ROUTE, TM_GRP,
                           g_final[None, :] if last else None)
            if ctx_out:
                xc2 = moe_block(xc2, oc, g_f, mod_c, w_out_l, w_r, w_gu, w_dn, None, min(TM_ROUTE, mc), TM_GRP)
            if last:
                return x2.reshape(bn, S, d)
    return final_norm(x2, g_final[None, :], TM).reshape(bn, S, d)
```

```python
import functools
import math

import numpy as np
import jax
import jax.numpy as jnp
from jax import lax
from jax.experimental import pallas as pl
from jax.experimental.pallas import tpu as pltpu

DEPTH = 2
GRID_W = 64
EPS = 1e-6
W_POOL = 256
W_CONV = 256
W_NA = 256
W_SSD = 256
POOL_WINDOWS = (2, 4, 8, 16)
POOL_GROUP = W_POOL // len(POOL_WINDOWS)
NA_HEADS = 4
NA_HEAD_DIM = W_NA // NA_HEADS
WIN_R = 8
WIN_C = 16
SSD_HEAD_DIM = 64
SSD_HEADS = W_SSD // SSD_HEAD_DIM
SSD_GROUPS = 2
SSD_STATE = 128
SSD_CHUNK = 128
SSD_XBC = W_SSD + 2 * SSD_GROUPS * SSD_STATE
D_IN_PROJ = W_POOL + 3 * W_CONV + 3 * W_NA + W_SSD + SSD_XBC + 2 * SSD_HEADS
N_EXPERTS = 8
TOP_K = 2

LANES = 128
D_IN_PAD = -(-D_IN_PROJ // LANES) * LANES
VMEM_LIMIT = 56 * 1024 * 1024
BF16 = jnp.bfloat16
F32 = jnp.float32


def _params(*sem):
    return pltpu.CompilerParams(dimension_semantics=sem, vmem_limit_bytes=VMEM_LIMIT)


def _norm_mod(x, g, scale, shift):
    ms = jnp.mean(x * x, axis=-1, keepdims=True)
    return (x * lax.rsqrt(ms + EPS) * g) * (1.0 + scale) + shift


def _mod_map(tiles_per_batch):
    if tiles_per_batch is None:
        return lambda i, *_: (0, 0, 0)
    return lambda i, *_: (i // tiles_per_batch, 0, 0)


def _resident(shape, index_map):
    return pl.BlockSpec(shape, index_map, pipeline_mode=pl.Buffered(1))


def _ada_kernel(c_ref, w_ref, b_ref, o_ref):
    c = c_ref[...]
    s = c * jax.nn.sigmoid(c)
    o_ref[...] = jnp.dot(s.astype(BF16), w_ref[...], preferred_element_type=F32) + b_ref[...]


def ada_modulation(c_rows, w, b):
    r, d = c_rows.shape
    n = w.shape[1]
    tn = 1536
    return pl.pallas_call(
        _ada_kernel,
        out_shape=jax.ShapeDtypeStruct((r, n), F32),
        grid=(n // tn,),
        in_specs=[pl.BlockSpec((r, d), lambda j: (0, 0)),
                  pl.BlockSpec((d, tn), lambda j: (0, j)),
                  pl.BlockSpec((1, tn), lambda j: (0, j))],
        out_specs=pl.BlockSpec((r, tn), lambda j: (0, j)),
        compiler_params=_params("arbitrary"),
        name="ada_modulation",
    )(c_rows, w, b)


PROJ_ROWS = 512


def _in_proj_kernel(x_ref, g_ref, mod_ref, w_ref, o_ref, dt_ref):
    tm = x_ref.shape[0]
    parts = [slice(r0, min(r0 + PROJ_ROWS, tm)) for r0 in range(0, tm, PROJ_ROWS)]
    hs = [_norm_mod(x_ref[rs, :], g_ref[...], mod_ref[0, 1:2, :], mod_ref[0, 0:1, :]).astype(BF16) for rs in parts]
    for rs, h in zip(parts, hs):
        pr = jnp.dot(h, w_ref[...], preferred_element_type=F32)
        o_ref[rs, :] = pr.astype(o_ref.dtype)
        dt_ref[rs, :] = pr[:, DT_COL * LANES:(DT_COL + 1) * LANES]


def in_proj(x2, g, mod, w, tiles_per_batch, tm):
    m, d = x2.shape
    n = w.shape[1]
    return pl.pallas_call(
        _in_proj_kernel,
        out_shape=(jax.ShapeDtypeStruct((m, n), BF16), jax.ShapeDtypeStruct((m, LANES), F32)),
        grid=(m // tm,),
        in_specs=[pl.BlockSpec((tm, d), lambda i: (i, 0)),
                  pl.BlockSpec((1, d), lambda i: (0, 0)),
                  pl.BlockSpec((1, 6, d), _mod_map(tiles_per_batch)),
                  _resident((d, n), lambda i: (0, 0))],
        out_specs=(pl.BlockSpec((tm, n), lambda i: (i, 0)), pl.BlockSpec((tm, LANES), lambda i: (i, 0))),
        compiler_params=_params("parallel"),
        name="in_proj",
    )(x2, g, mod, w)


def _mix_residual(x, mod_ref, w_ref, part_refs):
    y, k0 = None, 0
    for p_ref in part_refs:
        k = p_ref.shape[1]
        t = jnp.dot(p_ref[...], w_ref[k0:k0 + k, :], preferred_element_type=F32)
        y = t if y is None else y + t
        k0 += k
    return x + mod_ref[0, 2:3, :] * y


def _part_specs(parts, tm):
    return [pl.BlockSpec((tm, p.shape[1]), lambda i, *_: (i, 0)) for p in parts]


FF_CHUNK = 512


def _swiglu(h, wgu, wd, ff):
    acc = None
    for c0 in range(0, ff, FF_CHUNK):
        c1 = min(c0 + FF_CHUNK, ff)
        gg = jnp.dot(h, wgu(c0, c1), preferred_element_type=F32)
        uu = jnp.dot(h, wgu(ff + c0, ff + c1), preferred_element_type=F32)
        a = (gg * jax.nn.sigmoid(gg) * uu).astype(BF16)
        part = jnp.dot(a, wd(c0, c1), preferred_element_type=F32)
        acc = part if acc is None else acc + part
    return acc


def _ffn_kernel(x_ref, g_ref, mod_ref, wout_ref, wgu_ref, wd_ref, *refs):
    x = _mix_residual(x_ref[...], mod_ref, wout_ref, refs[:-1])
    y_ref = refs[-1]
    h = _norm_mod(x, g_ref[...], mod_ref[0, 4:5, :], mod_ref[0, 3:4, :]).astype(BF16)
    y = _swiglu(h, lambda a, b: wgu_ref[:, a:b], lambda a, b: wd_ref[a:b, :], wd_ref.shape[0])
    y_ref[...] = x + mod_ref[0, 5:6, :] * y


def ffn_dense(x2, parts, g, mod, w_out, w_gu, w_down, tiles_per_batch, tm):
    m, d = x2.shape
    return pl.pallas_call(
        _ffn_kernel,
        out_shape=jax.ShapeDtypeStruct((m, d), F32),
        grid=(m // tm,),
        in_specs=[pl.BlockSpec((tm, d), lambda i: (i, 0)),
                  pl.BlockSpec((1, d), lambda i: (0, 0)),
                  pl.BlockSpec((1, 6, d), _mod_map(tiles_per_batch)),
                  _resident(w_out.shape, lambda i: (0, 0)),
                  _resident(w_gu.shape, lambda i: (0, 0)),
                  _resident(w_down.shape, lambda i: (0, 0))] + _part_specs(parts, tm),
        out_specs=pl.BlockSpec((tm, d), lambda i: (i, 0)),
        compiler_params=_params("parallel"),
        name="ffn_dense",
    )(x2, g, mod, w_out, w_gu, w_down, *parts)


ROUTE_E1, ROUTE_E2, ROUTE_R1, ROUTE_R2, ROUTE_G1, ROUTE_G2 = range(6)
ROUTE_ROWS = 8


def _router_kernel(x_ref, g_ref, mod_ref, wout_ref, wr_ref, *refs):
    part_refs = refs[:-6]
    x1_ref, h_ref, route_ref, route_t_ref, cnt_ref, base_ref = refs[-6:]

    @pl.when(pl.program_id(0) == 0)
    def _():
        base_ref[...] = jnp.zeros_like(base_ref)

    x = _mix_residual(x_ref[...], mod_ref, wout_ref, part_refs)
    x1_ref[...] = x
    h = _norm_mod(x, g_ref[...], mod_ref[0, 4:5, :], mod_ref[0, 3:4, :]).astype(BF16)
    h_ref[...] = h
    tm = h.shape[0]
    logits = jnp.dot(h, wr_ref[...], preferred_element_type=F32)
    lane = lax.broadcasted_iota(jnp.int32, logits.shape, 1)
    neg = jnp.float32(-jnp.inf)
    logits = jnp.where(lane < N_EXPERTS, logits, neg)
    m1 = jnp.max(logits, axis=-1, keepdims=True)
    i1 = jnp.min(jnp.where(logits == m1, lane, LANES), axis=-1, keepdims=True)
    rest = jnp.where(lane == i1, neg, logits)
    m2 = jnp.max(rest, axis=-1, keepdims=True)
    i2 = jnp.min(jnp.where(rest == m2, lane, LANES), axis=-1, keepdims=True)
    e2 = jnp.exp(m2 - m1)
    g1 = 1.0 / (1.0 + e2)
    g2 = e2 / (1.0 + e2)
    sel1, sel2 = lane == i1, lane == i2
    sel = jnp.where(sel1 | sel2, 1.0, 0.0)
    li = lax.broadcasted_iota(jnp.int32, (tm, tm), 0)
    si = lax.broadcasted_iota(jnp.int32, (tm, tm), 1)
    before = jnp.where(si < li, 1.0, 0.0).astype(BF16)
    rank = jnp.dot(before, sel.astype(BF16), preferred_element_type=F32) + base_ref[...]
    r1 = jnp.sum(jnp.where(sel1, rank, 0.0), axis=-1, keepdims=True)
    r2 = jnp.sum(jnp.where(sel2, rank, 0.0), axis=-1, keepdims=True)
    base_ref[...] += jnp.sum(sel, axis=0, keepdims=True)
    cnt_ref[...] = base_ref[...]
    fields = (i1.astype(F32), i2.astype(F32), r1, r2, g1, g2)
    route = jnp.zeros(logits.shape, F32)
    for k, v in enumerate(fields):
        route = jnp.where(lane == k, v, route)
    route_ref[...] = route
    route_t_ref[...] = route.T[:ROUTE_ROWS, :]


def moe_router(x2, parts, g, mod, w_out, w_router, tiles_per_batch, tm):
    m, d = x2.shape
    return pl.pallas_call(
        _router_kernel,
        out_shape=(jax.ShapeDtypeStruct((m, d), F32), jax.ShapeDtypeStruct((m, d), BF16),
                   jax.ShapeDtypeStruct((m, LANES), F32), jax.ShapeDtypeStruct((ROUTE_ROWS, m), F32),
                   jax.ShapeDtypeStruct((1, LANES), F32)),
        grid=(m // tm,),
        in_specs=[pl.BlockSpec((tm, d), lambda i: (i, 0)),
                  pl.BlockSpec((1, d), lambda i: (0, 0)),
                  pl.BlockSpec((1, 6, d), _mod_map(tiles_per_batch)),
                  _resident(w_out.shape, lambda i: (0, 0)),
                  pl.BlockSpec((d, LANES), lambda i: (0, 0))] + _part_specs(parts, tm),
        out_specs=(pl.BlockSpec((tm, d), lambda i: (i, 0)),
                   pl.BlockSpec((tm, d), lambda i: (i, 0)),
                   pl.BlockSpec((tm, LANES), lambda i: (i, 0)),
                   pl.BlockSpec((ROUTE_ROWS, tm), lambda i: (0, i)),
                   pl.BlockSpec((1, LANES), lambda i: (0, 0))),
        scratch_shapes=[pltpu.VMEM((1, LANES), F32)],
        compiler_params=_params("arbitrary"),
        name="moe_router",
    )(x2, g, mod, w_out, w_router, *parts)


def _moe_kernel(tile_ref, exp_ref, start_ref, end_ref, xg_ref, wgu_ref, wd_ref, y_ref, wgu_b, wd_b):
    w = pl.program_id(0)
    tm = xg_ref.shape[0]
    start, end = start_ref[w], end_ref[w]
    t0 = tile_ref[w] * tm

    @pl.when((w == 0) | (exp_ref[w] != exp_ref[jnp.maximum(w - 1, 0)]))
    def _():
        wgu_b[...] = wgu_ref[0].astype(BF16)
        wd_b[...] = wd_ref[0].astype(BF16)

    @pl.when(end > start)
    def _():
        y = _swiglu(xg_ref[...], lambda a, b: wgu_b[:, a:b], lambda a, b: wd_b[a:b, :], wd_b.shape[0])
        y = y.astype(y_ref.dtype)

        @pl.when(start == t0)
        def _():
            y_ref[...] = y

        @pl.when(start != t0)
        def _():
            row = t0 + lax.broadcasted_iota(jnp.int32, (tm, 1), 0)
            y_ref[...] = jnp.where(row >= start, y, y_ref[...])


def moe_grouped(item_tile, item_expert, item_start, item_end, xg, w_gu, w_down, tm):
    p, d = xg.shape
    grid_spec = pltpu.PrefetchScalarGridSpec(
        num_scalar_prefetch=4,
        grid=(item_tile.shape[0],),
        in_specs=[pl.BlockSpec((tm, d), lambda w, it, ie, s, e: (it[w], 0)),
                  _resident((1,) + w_gu.shape[1:], lambda w, it, ie, s, e: (ie[w], 0, 0)),
                  _resident((1,) + w_down.shape[1:], lambda w, it, ie, s, e: (ie[w], 0, 0))],
        out_specs=pl.BlockSpec((tm, d), lambda w, it, ie, s, e: (it[w], 0)),
        scratch_shapes=[pltpu.VMEM(w_gu.shape[1:], BF16), pltpu.VMEM(w_down.shape[1:], BF16)],
    )
    return pl.pallas_call(
        _moe_kernel,
        out_shape=jax.ShapeDtypeStruct((p, d), BF16),
        grid_spec=grid_spec,
        compiler_params=_params("arbitrary"),
        name="moe_grouped",
    )(item_tile, item_expert, item_start, item_end, xg, w_gu, w_down)


def _moe_combine_kernel(x_ref, ya_ref, yb_ref, route_ref, mod_ref, *refs):
    o_ref = refs[-1]
    r = route_ref[...]
    y = (r[:, ROUTE_G1:ROUTE_G1 + 1] * ya_ref[...].astype(F32) + r[:, ROUTE_G2:ROUTE_G2 + 1] * yb_ref[...].astype(F32))
    x = x_ref[...] + mod_ref[0, 5:6, :] * y
    if len(refs) == 2:
        ms = jnp.mean(x * x, axis=-1, keepdims=True)
        x = x * lax.rsqrt(ms + EPS) * refs[0][...]
    o_ref[...] = x


def moe_combine(x2, ya, yb, route, mod, tiles_per_batch, tm, g_final=None):
    m, d = x2.shape
    row = pl.BlockSpec((tm, d), lambda i: (i, 0))
    specs = [row, row, row, pl.BlockSpec((tm, LANES), lambda i: (i, 0)),
             pl.BlockSpec((1, 6, d), _mod_map(tiles_per_batch))]
    args = [x2, ya, yb, route, mod]
    if g_final is not None:
        specs.append(pl.BlockSpec((1, d), lambda i: (0, 0)))
        args.append(g_final)
    return pl.pallas_call(
        _moe_combine_kernel,
        out_shape=jax.ShapeDtypeStruct((m, d), F32),
        grid=(m // tm,),
        in_specs=specs,
        out_specs=row,
        compiler_params=_params("parallel"),
        name="moe_combine",
    )(*args)


def moe_block(x2, parts, g, mod, w_out, w_router, w_gu, w_down, tiles_per_batch, tm_tok, tm_grp, g_final=None):
    m, d = x2.shape
    x2, h, route, route_t, cnt = moe_router(x2, parts, g, mod, w_out, w_router, tiles_per_batch, tm_tok)
    cnt = cnt[0, :N_EXPERTS].astype(jnp.int32)
    p = m * TOP_K
    off = jnp.cumsum(cnt) - cnt
    field = lambda k: route_t[k].astype(jnp.int32)

    def row_of(e, r):
        for k in range(N_EXPERTS):
            r = r + jnp.where(e == k, off[k], 0)
        return r

    d1 = row_of(field(ROUTE_E1), field(ROUTE_R1))
    d2 = row_of(field(ROUTE_E2), field(ROUTE_R2))
    tok = jnp.arange(m, dtype=jnp.int32)
    _, src = lax.sort_key_val(jnp.concatenate([d1, d2]), jnp.concatenate([tok, tok]))
    n_row_tiles = p // tm_grp
    start = jnp.sort(jnp.concatenate([jnp.arange(n_row_tiles, dtype=jnp.int32) * tm_grp, off]))
    end = jnp.concatenate([start[1:], jnp.full((1,), p, jnp.int32)])
    item_tile = jnp.minimum(start // tm_grp, n_row_tiles - 1)
    item_expert = jnp.sum((off[None, :] <= start[:, None]).astype(jnp.int32), axis=1) - 1
    rows = lambda a, idx: a.at[idx].get(mode="promise_in_bounds")
    yg = moe_grouped(item_tile, item_expert, start, end, rows(h, src), w_gu, w_down, tm_grp)
    return moe_combine(x2, rows(yg, d1), rows(yg, d2), route, mod, tiles_per_batch, tm_tok, g_final)


def _final_norm_kernel(x_ref, g_ref, o_ref):
    x = x_ref[...]
    ms = jnp.mean(x * x, axis=-1, keepdims=True)
    o_ref[...] = x * lax.rsqrt(ms + EPS) * g_ref[...]


def final_norm(x2, g, tm):
    m, d = x2.shape
    return pl.pallas_call(
        _final_norm_kernel,
        out_shape=jax.ShapeDtypeStruct((m, d), F32),
        grid=(m // tm,),
        in_specs=[pl.BlockSpec((tm, d), lambda i: (i, 0)), pl.BlockSpec((1, d), lambda i: (0, 0))],
        out_specs=pl.BlockSpec((tm, d), lambda i: (i, 0)),
        compiler_params=_params("parallel"),
        name="final_norm",
    )(x2, g)


NA_QROWS = 4
NA_KROWS = NA_QROWS + WIN_R
NA_STEP_BLOCKS = 4
Q_COL, K_COL, V_COL = 4, 5, 6


def _na_key_start(j, rows):
    return np.clip(j * NA_QROWS - WIN_R // 2, 0, rows - NA_KROWS)


def _na_bias_index(rows):
    nblk = rows // NA_QROWS
    pats = []
    for j in range(nblk):
        start = _na_key_start(j, rows)
        r = j * NA_QROWS + np.arange(NA_QROWS)
        sr = np.clip(r - WIN_R // 2, 0, rows - WIN_R)
        kr = start + np.arange(NA_KROWS)
        rvalid = (kr[None, :] >= sr[:, None]) & (kr[None, :] < sr[:, None] + WIN_R)
        ri = np.clip(kr[None, :] - r[:, None] + WIN_R - 1, 0, 2 * WIN_R - 2)
        pats.append((ri, rvalid))
    for j in range(2, nblk - 1):
        assert all(np.array_equal(a, b) for a, b in zip(pats[1], pats[j]))
    sel = [pats[0], pats[1], pats[nblk - 1]]
    ri = np.stack([p[0] for p in sel])
    rvalid = np.stack([p[1] for p in sel])
    c = np.arange(GRID_W)
    sc = np.clip(c - WIN_C // 2, 0, GRID_W - WIN_C)
    cvalid = (c[None, :] >= sc[:, None]) & (c[None, :] < sc[:, None] + WIN_C)
    ci = np.clip(c[None, :] - c[:, None] + WIN_C - 1, 0, 2 * WIN_C - 2)
    c_onehot = (ci[..., None] == np.arange(2 * WIN_C - 1)).astype(np.float32)
    valid = rvalid[:, :, None, :, None] & cvalid[None, None, :, None, :]
    return ri, c_onehot, valid


def na_bias_table(rpb, rows):
    ri, c_onehot, valid = _na_bias_index(rows)
    toep = jnp.einsum('hrd,qkd->hrqk', rpb, c_onehot, precision=lax.Precision.HIGHEST)
    b = jnp.take(toep, ri.reshape(-1), axis=1).reshape((NA_HEADS,) + ri.shape + (GRID_W, GRID_W))
    b = jnp.transpose(b, (1, 0, 2, 4, 3, 5))
    b = jnp.where(valid[:, None], b, -jnp.inf)
    return b.reshape(3, NA_HEADS, NA_QROWS * GRID_W, NA_KROWS * GRID_W).astype(F32)


def _attend_heads(q, key_sets, bias_fn):
    n, w = q.shape
    lane = lax.broadcasted_iota(jnp.int32, (1, w), 1)
    head_masks = [(lane >= h * NA_HEAD_DIM) & (lane < (h + 1) * NA_HEAD_DIM) for h in range(NA_HEADS)]
    all_scores = []
    for h, mh in enumerate(head_masks):
        qh = jnp.where(mh, q, 0.0).astype(BF16)
        scores = []
        for i, (k, _) in enumerate(key_sets):
            s = lax.dot_general(qh, k, (((1,), (1,)), ((), ())), preferred_element_type=F32)
            b = bias_fn(i, h)
            scores.append(s if b is None else s + b)
        all_scores.append(scores)
    all_probs = []
    for scores in all_scores:
        m = scores[0].max(axis=-1, keepdims=True)
        for s in scores[1:]:
            m = jnp.maximum(m, s.max(axis=-1, keepdims=True))
        denom = jnp.zeros((n, 1), F32)
        probs = []
        for s in scores:
            p = jnp.exp(s - m)
            denom = denom + p.sum(axis=-1, keepdims=True)
            probs.append(p.astype(BF16))
        all_probs.append((probs, denom))
    out = jnp.zeros((n, w), F32)
    for mh, (probs, denom) in zip(head_masks, all_probs):
        acc = jnp.zeros((n, w), F32)
        for p, (_, v) in zip(probs, key_sets):
            acc = acc + jnp.dot(p, v, preferred_element_type=F32)
        out = out + jnp.where(mh, acc / denom, 0.0)
    return out


def _na_kernel(q_ref, k_ref, v_ref, kc_ref, vc_ref, bias_ref, o_ref, *, rows):
    nblk = rows // NA_QROWS
    nq, nk = NA_QROWS * GRID_W, NA_KROWS * GRID_W
    for s in range(NA_STEP_BLOCKS):
        j = pl.program_id(1) * NA_STEP_BLOCKS + s
        start = jnp.clip(j * NA_QROWS - WIN_R // 2, 0, rows - NA_KROWS)
        t0 = pl.multiple_of(start * GRID_W, GRID_W)
        pattern = jnp.where(j == 0, 0, jnp.where(j == nblk - 1, 2, 1))
        kw = k_ref[pl.ds(t0, nk), :]
        vw = v_ref[pl.ds(t0, nk), :]
        q = q_ref[s * nq:(s + 1) * nq, :] * (1.0 / math.sqrt(NA_HEAD_DIM))
        o = _attend_heads(q, [(kw, vw), (kc_ref[...], vc_ref[...])],
                          lambda i, h: bias_ref[pattern, h] if i == 0 else None)
        o_ref[s * nq:(s + 1) * nq, :] = o.astype(o_ref.dtype)


def na_attention(pr, pr_c, bias, bn):
    S, Lc = pr.shape[0] // bn, pr_c.shape[0] // bn
    rows = S // GRID_W
    nsteps = rows // NA_QROWS // NA_STEP_BLOCKS
    nq = NA_STEP_BLOCKS * NA_QROWS * GRID_W
    return pl.pallas_call(
        functools.partial(_na_kernel, rows=rows),
        out_shape=jax.ShapeDtypeStruct((bn * S, W_NA), BF16),
        grid=(bn, nsteps),
        in_specs=[pl.BlockSpec((nq, W_NA), lambda b, j: (b * nsteps + j, Q_COL)),
                  pl.BlockSpec((S, W_NA), lambda b, j: (b, K_COL)),
                  pl.BlockSpec((S, W_NA), lambda b, j: (b, V_COL)),
                  pl.BlockSpec((Lc, W_NA), lambda b, j: (b, K_COL)),
                  pl.BlockSpec((Lc, W_NA), lambda b, j: (b, V_COL)),
                  _resident(bias.shape, lambda b, j: (0, 0, 0, 0))],
        out_specs=pl.BlockSpec((nq, W_NA), lambda b, j: (b * nsteps + j, 0)),
        compiler_params=_params("parallel", "arbitrary"),
        name="na_attention",
    )(pr, pr, pr, pr_c, pr_c, bias)


def _ctx_attn_kernel(q_ref, k_ref, v_ref, o_ref):
    q = q_ref[...] * (1.0 / math.sqrt(NA_HEAD_DIM))
    o = _attend_heads(q, [(k_ref[...], v_ref[...])], lambda i, h: None)
    o_ref[...] = o.astype(o_ref.dtype)


def ctx_attention(pr_c, bn):
    Lc = pr_c.shape[0] // bn
    return pl.pallas_call(
        _ctx_attn_kernel,
        out_shape=jax.ShapeDtypeStruct((bn * Lc, W_NA), BF16),
        grid=(bn,),
        in_specs=[pl.BlockSpec((Lc, W_NA), lambda b: (b, Q_COL)),
                  pl.BlockSpec((Lc, W_NA), lambda b: (b, K_COL)),
                  pl.BlockSpec((Lc, W_NA), lambda b: (b, V_COL))],
        out_specs=pl.BlockSpec((Lc, W_NA), lambda b: (b, 0)),
        compiler_params=_params("parallel"),
        name="ctx_attention",
    )(pr_c, pr_c, pr_c)


POOL_COL, CONV_H_COL, CONV_B_COL, CONV_C_COL = 0, 1, 2, 3


def _shift_rows(x, k, row):
    n = x.shape[0]
    y = pltpu.roll(x, k % n, 0)
    return jnp.where(row < k, 0.0, y) if k > 0 else jnp.where(row >= n + k, 0.0, y)


def _local_mix_kernel(u_ref, h_ref, bg_ref, cg_ref, pw_ref, ps_ref, cw_ref, op_ref, oc_ref, *, seq):
    row = lax.broadcasted_iota(jnp.int32, (seq, 1), 0)
    lane = lax.broadcasted_iota(jnp.int32, (1, W_POOL), 1)
    u = u_ref[...].astype(F32)
    trailing, leading = {1: u}, {1: u}
    for w in (1, 2, 4):
        trailing[2 * w] = trailing[w] + _shift_rows(trailing[w], w, row)
        leading[2 * w] = leading[w] + _shift_rows(leading[w], -w, row)
    rowf = row.astype(F32)
    win_sum = jnp.zeros_like(u)
    cnt = jnp.zeros_like(u)
    for g, win in enumerate(POOL_WINDOWS):
        half = win // 2
        mg = (lane >= g * POOL_GROUP) & (lane < (g + 1) * POOL_GROUP)
        s = _shift_rows(trailing[half], 1, row) + leading[half]
        n = jnp.minimum(rowf + half, float(seq)) - jnp.maximum(rowf - half, 0.0)
        win_sum = jnp.where(mg, s, win_sum)
        cnt = jnp.where(mg, n, cnt)
    p = win_sum / cnt - u
    pooled = jnp.dot(p.astype(BF16), pw_ref[...], preferred_element_type=F32) * ps_ref[...]
    op_ref[...] = pooled.astype(op_ref.dtype)
    v = cg_ref[...].astype(F32) * h_ref[...].astype(F32)
    conv = (_shift_rows(v, 1, row) * cw_ref[0:1, :] + v * cw_ref[1:2, :] + _shift_rows(v, -1, row) * cw_ref[2:3, :])
    oc_ref[...] = (bg_ref[...].astype(F32) * conv).astype(oc_ref.dtype)


def local_mix(pr, pool_w, pool_scale, conv_w, bn):
    seq = pr.shape[0] // bn
    pw = jax.scipy.linalg.block_diag(*[pool_w[g] for g in range(len(POOL_WINDOWS))]).astype(BF16)
    blk = lambda col: pl.BlockSpec((seq, W_POOL), lambda b: (b, col))
    full = lambda a: pl.BlockSpec(a.shape, lambda b: (0,) * a.ndim)
    args = (pw, pool_scale[None, :], conv_w)
    return pl.pallas_call(
        functools.partial(_local_mix_kernel, seq=seq),
        out_shape=(jax.ShapeDtypeStruct((bn * seq, W_POOL), BF16), jax.ShapeDtypeStruct((bn * seq, W_CONV), BF16)),
        grid=(bn,),
        in_specs=[blk(POOL_COL), blk(CONV_H_COL), blk(CONV_B_COL), blk(CONV_C_COL)] + [full(a) for a in args],
        out_specs=(pl.BlockSpec((seq, W_POOL), lambda b: (b, 0)),
                   pl.BlockSpec((seq, W_CONV), lambda b: (b, 0))),
        compiler_params=_params("parallel"),
        name="local_mix",
    )(pr, pr, pr, pr, *args)


Z_COL, XS_COL, BS_COL, CS_COL = 7, 8, 9, 10
DT_COL = 22
HEADS_PER_GROUP = SSD_HEADS // SSD_GROUPS
GROUP_W = HEADS_PER_GROUP * SSD_HEAD_DIM


def _head_mask(h):
    lane = lax.broadcasted_iota(jnp.int32, (1, W_SSD), 1)
    return (lane >= h * SSD_HEAD_DIM) & (lane < (h + 1) * SSD_HEAD_DIM)


def _expand_heads(v, d):
    out = jnp.zeros((v.shape[0], W_SSD), F32)
    for h in range(SSD_HEADS):
        k = d * SSD_HEADS + h
        out = jnp.where(_head_mask(h), v[:, k:k + 1], out)
    return out


def _ssd_chunk_terms(c, xs_ref, b_ref, c_ref, dt_ref, la_ref, dsk_ref, y_ref, upd_ref, dec_ref, need_y):
    T = SSD_CHUNK
    r0 = pl.multiple_of(c * T, T)
    xs = xs_ref[pl.ds(r0, T), :]
    bm = b_ref[pl.ds(r0, T), :]
    cm = c_ref[pl.ds(r0, T), :]
    dt = dt_ref[pl.ds(r0, T), :]
    la = la_ref[pl.ds(r0, T), :]
    li = lax.broadcasted_iota(jnp.int32, (T, T), 0)
    si = lax.broadcasted_iota(jnp.int32, (T, T), 1)
    causal = si <= li
    prefix = jnp.dot(causal.astype(F32), la, precision=lax.Precision.HIGHEST, preferred_element_type=F32)
    total = prefix[T - 1:T, :]
    lane = lax.broadcasted_iota(jnp.int32, (1, LANES), 1)
    acum = jnp.where(lane < SSD_HEADS, prefix, total - prefix + la)
    bt = bm.astype(F32).T.astype(BF16)
    if need_y:
        acum_t = acum.T
        cb = [lax.dot_general(cm[:, g * SSD_STATE:(g + 1) * SSD_STATE], bm[:, g * SSD_STATE:(g + 1) * SSD_STATE],
                              (((1,), (1,)), ((), ())), preferred_element_type=F32) for g in range(SSD_GROUPS)]
        y = jnp.zeros((T, W_SSD), F32)
    per_dir = []
    for d in range(2):
        acum_e = _expand_heads(acum, d)
        tot_e = _expand_heads(total, d)
        xdt = xs * _expand_heads(dt, d)
        xw = (xdt * jnp.exp(tot_e - acum_e)).astype(BF16)
        dec_ref[c, d, 0:T, :] = jnp.exp(acum_e)
        dec_ref[c, d, T:T + 1, :] = jnp.exp(tot_e)
        scores = []
        if need_y:
            mask = causal if d == 0 else (si >= li)
            for h in range(SSD_HEADS):
                k = d * SSD_HEADS + h
                decay = jnp.exp(jnp.where(mask, acum[:, k:k + 1] - acum_t[k:k + 1, :], -jnp.inf))
                scores.append((cb[h // HEADS_PER_GROUP] * decay).astype(BF16))
        per_dir.append((xw, xdt, scores))
    for d, (xw, xdt, scores) in enumerate(per_dir):
        upd = [jnp.dot(bt[g * SSD_STATE:(g + 1) * SSD_STATE, :], xw[:, g * GROUP_W:(g + 1) * GROUP_W],
                       preferred_element_type=F32) for g in range(SSD_GROUPS)]
        upd_ref[c, d] = jnp.concatenate(upd, axis=1)
        if need_y:
            x_heads = [jnp.where(_head_mask(h), xdt, 0.0).astype(BF16) for h in range(SSD_HEADS)]
            y = y + jnp.dot(jnp.concatenate(scores, axis=1), jnp.concatenate(x_heads, axis=0),
                            preferred_element_type=F32)
    if need_y:
        y_ref[pl.ds(r0, T), :] = xs * dsk_ref[...] + y


def _ssd_chunk_state(c, d, c_ref, y_ref, upd_ref, dec_ref, st_ref, need_y):
    T = SSD_CHUNK
    r0 = pl.multiple_of(c * T, T)
    st = st_ref[d]
    if need_y:
        cm = c_ref[pl.ds(r0, T), :]
        st_b = st.astype(BF16)
        ys = [jnp.dot(cm[:, g * SSD_STATE:(g + 1) * SSD_STATE], st_b[:, g * GROUP_W:(g + 1) * GROUP_W],
                      preferred_element_type=F32) for g in range(SSD_GROUPS)]
        y_ref[pl.ds(r0, T), :] += jnp.concatenate(ys, axis=1) * dec_ref[c, d, 0:T, :]
    st_ref[d] = dec_ref[c, d, T:T + 1, :] * st + upd_ref[c, d]


def _ssd_kernel(z_ref, xs_in, bs_in, cs_in, dtr_ref, cw_ref, cb_ref, dtb_ref, alog_ref, dsk_ref, ng_ref, h0_ref,
                *refs, seq, need_y):
    if need_y:
        o_ref, hT_ref, xs_ref, b_ref, c_ref, dt_ref, la_ref, upd_ref, dec_ref, st_ref, y_ref = refs
    else:
        hT_ref, xs_ref, b_ref, c_ref, dt_ref, la_ref, upd_ref, dec_ref, st_ref = refs
        y_ref = None
    nc = seq // SSD_CHUNK
    row = lax.broadcasted_iota(jnp.int32, (seq, 1), 0)
    for gi, (src, dst) in enumerate(((xs_in, xs_ref), (bs_in, b_ref), (cs_in, c_ref))):
        sl = slice(gi * W_SSD, (gi + 1) * W_SSD)
        x = src[...].astype(F32)
        cv = (_shift_rows(x, 1, row) * cw_ref[0:1, sl] + x * cw_ref[1:2, sl]
              + _shift_rows(x, -1, row) * cw_ref[2:3, sl] + cb_ref[:, sl])
        dst[...] = (cv * jax.nn.sigmoid(cv)).astype(dst.dtype)
    dtv = dtr_ref[...] + dtb_ref[...]
    dt = jnp.maximum(dtv, 0.0) + jnp.log1p(jnp.exp(-jnp.abs(dtv)))
    dt_ref[...] = dt
    la_ref[...] = dt * (-jnp.exp(alog_ref[...]))
    st_ref[...] = h0_ref[0]

    def terms(c, carry):
        _ssd_chunk_terms(c, xs_ref, b_ref, c_ref, dt_ref, la_ref, dsk_ref, y_ref, upd_ref, dec_ref, need_y)
        return carry

    lax.fori_loop(0, nc, terms, 0, unroll=min(16, nc))

    def recur(i, carry):
        _ssd_chunk_state(i, 0, c_ref, y_ref, upd_ref, dec_ref, st_ref, need_y)
        _ssd_chunk_state(nc - 1 - i, 1, c_ref, y_ref, upd_ref, dec_ref, st_ref, need_y)
        return carry

    lax.fori_loop(0, nc, recur, 0, unroll=min(16, nc))
    hT_ref[0] = st_ref[...]
    if need_y:
        z = z_ref[...].astype(F32)
        yz = y_ref[...] * (z * jax.nn.sigmoid(z))
        ms = jnp.mean(yz * yz, axis=-1, keepdims=True)
        o_ref[...] = (yz * lax.rsqrt(ms + EPS) * ng_ref[...]).astype(o_ref.dtype)


def ssd_scan(pr, dt, h0, conv_w, conv_b, dt_bias, a_log, d_skip, norm_g, need_y):
    bn = h0.shape[0]
    seq = pr.shape[0] // bn
    nc = seq // SSD_CHUNK
    pad = LANES - 2 * SSD_HEADS
    dtb = jnp.pad(dt_bias.reshape(1, -1), ((0, 0), (0, pad)))
    alog = jnp.pad(a_log.reshape(1, -1), ((0, 0), (0, pad)))
    dsk = jnp.repeat(d_skip, SSD_HEAD_DIM)[None, :]
    blk = lambda col: pl.BlockSpec((seq, W_SSD), lambda b: (b, col))
    full = lambda a: pl.BlockSpec(a.shape, lambda b: (0,) * a.ndim)
    st_shape = (1, 2, SSD_STATE, W_SSD)
    st_spec = pl.BlockSpec(st_shape, lambda b: (b, 0, 0, 0))
    args = (conv_w, conv_b[None, :], dtb, alog, dsk, norm_g[None, :])
    out_shape = [jax.ShapeDtypeStruct((bn,) + st_shape[1:], F32)]
    out_specs = [st_spec]
    scratch = [pltpu.VMEM((seq, W_SSD), F32),
               pltpu.VMEM((seq, W_SSD), BF16), pltpu.VMEM((seq, W_SSD), BF16),
               pltpu.VMEM((seq, LANES), F32), pltpu.VMEM((seq, LANES), F32),
               pltpu.VMEM((nc, 2, SSD_STATE, W_SSD), F32),
               pltpu.VMEM((nc, 2, SSD_CHUNK + 8, W_SSD), F32),
               pltpu.VMEM(st_shape[1:], F32)]
    if need_y:
        out_shape.insert(0, jax.ShapeDtypeStruct((bn * seq, W_SSD), BF16))
        out_specs.insert(0, pl.BlockSpec((seq, W_SSD), lambda b: (b, 0)))
        scratch.append(pltpu.VMEM((seq, W_SSD), F32))
    res = pl.pallas_call(
        functools.partial(_ssd_kernel, seq=seq, need_y=need_y),
        out_shape=out_shape,
        grid=(bn,),
        in_specs=[blk(Z_COL), blk(XS_COL), blk(BS_COL), blk(CS_COL),
                  pl.BlockSpec((seq, LANES), lambda b: (b, 0))]
                 + [full(a) for a in args] + [st_spec],
        out_specs=out_specs,
        scratch_shapes=scratch,
        compiler_params=_params("parallel"),
        name="ssd_scan",
    )(pr, pr, pr, pr, dt, *args, h0)
    return (res[0], res[1]) if need_y else (None, res[0])


def ssd_mix(pr, dt, pr_c, dt_c, conv_w, conv_b, dt_bias, a_log, d_skip, norm_g, ctx_out, bn):
    h0 = jnp.zeros((bn, 2, SSD_STATE, W_SSD), F32)
    oc, hc = ssd_scan(pr_c, dt_c, h0, conv_w, conv_b, dt_bias, a_log, d_skip, norm_g, ctx_out)
    o, _ = ssd_scan(pr, dt, hc, conv_w, conv_b, dt_bias, a_log, d_skip, norm_g, True)
    return o, oc


def mixer(pr, dt, pr_c, dt_c, pool_w, pool_scale, conv_w, na_bias, s_conv_w, s_conv_b, dt_bias, a_log, d_skip,
          s_norm_g, ctx_out, bn):
    o_ssd, oc_ssd = ssd_mix(pr, dt, pr_c, dt_c, s_conv_w, s_conv_b, dt_bias, a_log, d_skip, s_norm_g, ctx_out, bn)
    o = [*local_mix(pr, pool_w, pool_scale, conv_w, bn), na_attention(pr, pr_c, na_bias, bn), o_ssd]
    if not ctx_out:
        return o, None
    oc = [*local_mix(pr_c, pool_w, pool_scale, conv_w, bn), ctx_attention(pr_c, bn), oc_ssd]
    return o, oc


TM = 512
TM_ROUTE = 1024
TM_PROJ = 1024
TM_GRP = 512


def kernel(x, c, ctx, c_ctx, w_ada, b_ada, g_mix, g_ffn, w_in, w_out, pool_w, pool_scale, conv_w, na_rpb,
           ssd_conv_w, ssd_conv_b, ssd_dt_bias, ssd_a_log, ssd_d, ssd_norm_g, ffn_w_gu, ffn_w_down,
           moe_router, moe_w_gu, moe_w_down, g_final):
    bn, S, d = x.shape
    Lc = ctx.shape[1]
    m, mc = bn * S, bn * Lc
    tpb = S // TM
    x2 = x.reshape(m, d)
    xc2 = ctx.reshape(mc, d)
    c_rows = jnp.concatenate([c, c_ctx[None, :], jnp.zeros((7, d), F32)], axis=0)
    for l in range(DEPTH):
        ctx_out = l < DEPTH - 1
        last = l == DEPTH - 1
        ada = ada_modulation(c_rows, w_ada[l].astype(BF16), b_ada[l][None, :])
        mod = ada[:bn].reshape(bn, 6, d)
        mod_c = ada[bn:bn + 1].reshape(1, 6, d)
        w_in_l = jnp.pad(w_in[l], ((0, 0), (0, D_IN_PAD - D_IN_PROJ))).astype(BF16)
        g_m = g_mix[l][None, :]
        g_f = g_ffn[l][None, :]
        pr, dt = in_proj(x2, g_m, mod, w_in_l, S // TM_PROJ, TM_PROJ)
        pr_c, dt_c = in_proj(xc2, g_m, mod_c, w_in_l, None, min(TM_PROJ, mc))
        na_bias = na_bias_table(na_rpb[l], S // GRID_W)
        o, oc = mixer(pr, dt, pr_c, dt_c, pool_w[l], pool_scale[l], conv_w[l], na_bias, ssd_conv_w[l],
                      ssd_conv_b[l], ssd_dt_bias[l], ssd_a_log[l], ssd_d[l], ssd_norm_g[l], ctx_out, bn)
        w_out_l = w_out[l].astype(BF16)
        j = l // 2
        if l % 2 == 0:
            w_gu = ffn_w_gu[j].astype(BF16)
            w_dn = ffn_w_down[j].astype(BF16)
            x2 = ffn_dense(x2, o, g_f, mod, w_out_l, w_gu, w_dn, tpb, TM)
            if ctx_out:
                xc2 = ffn_dense(xc2, oc, g_f, mod_c, w_out_l, w_gu, w_dn, None, min(TM, mc))
        else:
            w_r = jnp.pad(moe_router[j], ((0, 0), (0, LANES - N_EXPERTS))).astype(BF16)
            w_gu, w_dn = moe_w_gu[j], moe_w_down[j]
            x2 = moe_block(x2, o, g_f, mod, w_out_l, w_r, w_gu, w_dn, S // TM_ROUTE, TM_ROUTE, TM_GRP,
                           g_final[None, :] if last else None)
            if ctx_out:
                xc2 = moe_block(xc2, oc, g_f, mod_c, w_out_l, w_r, w_gu, w_dn, None, min(TM_ROUTE, mc), TM_GRP)
            if last:
                return x2.reshape(bn, S, d)
    return final_norm(x2, g_final[None, :], TM).reshape(bn, S, d)
```

```python
import functools
import math

import numpy as np
import jax
import jax.numpy as jnp
from jax import lax
from jax.experimental import pallas as pl
from jax.experimental.pallas import tpu as pltpu

DEPTH = 2
GRID_W = 64
EPS = 1e-6
W_POOL = 256
W_CONV = 256
W_NA = 256
W_SSD = 256
POOL_WINDOWS = (2, 4, 8, 16)
POOL_GROUP = W_POOL // len(POOL_WINDOWS)
NA_HEADS = 4
NA_HEAD_DIM = W_NA // NA_HEADS
WIN_R = 8
WIN_C = 16
SSD_HEAD_DIM = 64
SSD_HEADS = W_SSD // SSD_HEAD_DIM
SSD_GROUPS = 2
SSD_STATE = 128
SSD_CHUNK = 128
SSD_XBC = W_SSD + 2 * SSD_GROUPS * SSD_STATE
D_IN_PROJ = W_POOL + 3 * W_CONV + 3 * W_NA + W_SSD + SSD_XBC + 2 * SSD_HEADS
N_EXPERTS = 8
TOP_K = 2

LANES = 128
D_IN_PAD = -(-D_IN_PROJ // LANES) * LANES
VMEM_LIMIT = 56 * 1024 * 1024
BF16 = jnp.bfloat16
F32 = jnp.float32


def _params(*sem):
    return pltpu.CompilerParams(dimension_semantics=sem, vmem_limit_bytes=VMEM_LIMIT)


def _norm_mod(x, g, scale, shift):
    ms = jnp.mean(x * x, axis=-1, keepdims=True)
    return (x * lax.rsqrt(ms + EPS) * g) * (1.0 + scale) + shift


def _mod_map(tiles_per_batch):
    if tiles_per_batch is None:
        return lambda i, *_: (0, 0, 0)
    return lambda i, *_: (i // tiles_per_batch, 0, 0)


def _resident(shape, index_map):
    return pl.BlockSpec(shape, index_map, pipeline_mode=pl.Buffered(1))


def _ada_kernel(c_ref, w_ref, b_ref, o_ref):
    c = c_ref[...]
    s = c * jax.nn.sigmoid(c)
    o_ref[...] = jnp.dot(s.astype(BF16), w_ref[...], preferred_element_type=F32) + b_ref[...]


def ada_modulation(c_rows, w, b):
    r, d = c_rows.shape
    n = w.shape[1]
    tn = 1536
    return pl.pallas_call(
        _ada_kernel,
        out_shape=jax.ShapeDtypeStruct((r, n), F32),
        grid=(n // tn,),
        in_specs=[pl.BlockSpec((r, d), lambda j: (0, 0)),
                  pl.BlockSpec((d, tn), lambda j: (0, j)),
                  pl.BlockSpec((1, tn), lambda j: (0, j))],
        out_specs=pl.BlockSpec((r, tn), lambda j: (0, j)),
        compiler_params=_params("arbitrary"),
        name="ada_modulation",
    )(c_rows, w, b)


PROJ_ROWS = 512


def _in_proj_kernel(x_ref, g_ref, mod_ref, w_ref, o_ref, dt_ref):
    tm = x_ref.shape[0]
    parts = [slice(r0, min(r0 + PROJ_ROWS, tm)) for r0 in range(0, tm, PROJ_ROWS)]
    hs = [_norm_mod(x_ref[rs, :], g_ref[...], mod_ref[0, 1:2, :], mod_ref[0, 0:1, :]).astype(BF16) for rs in parts]
    for rs, h in zip(parts, hs):
        pr = jnp.dot(h, w_ref[...], preferred_element_type=F32)
        o_ref[rs, :] = pr.astype(o_ref.dtype)
        dt_ref[rs, :] = pr[:, DT_COL * LANES:(DT_COL + 1) * LANES]


def in_proj(x2, g, mod, w, tiles_per_batch, tm):
    m, d = x2.shape
    n = w.shape[1]
    return pl.pallas_call(
        _in_proj_kernel,
        out_shape=(jax.ShapeDtypeStruct((m, n), BF16), jax.ShapeDtypeStruct((m, LANES), F32)),
        grid=(m // tm,),
        in_specs=[pl.BlockSpec((tm, d), lambda i: (i, 0)),
                  pl.BlockSpec((1, d), lambda i: (0, 0)),
                  pl.BlockSpec((1, 6, d), _mod_map(tiles_per_batch)),
                  _resident((d, n), lambda i: (0, 0))],
        out_specs=(pl.BlockSpec((tm, n), lambda i: (i, 0)), pl.BlockSpec((tm, LANES), lambda i: (i, 0))),
        compiler_params=_params("parallel"),
        name="in_proj",
    )(x2, g, mod, w)


def _mix_residual(x, mod_ref, w_ref, part_refs):
    y, k0 = None, 0
    for p_ref in part_refs:
        k = p_ref.shape[1]
        t = jnp.dot(p_ref[...], w_ref[k0:k0 + k, :], preferred_element_type=F32)
        y = t if y is None else y + t
        k0 += k
    return x + mod_ref[0, 2:3, :] * y


def _part_specs(parts, tm):
    return [pl.BlockSpec((tm, p.shape[1]), lambda i, *_: (i, 0)) for p in parts]


FF_CHUNK = 512


def _swiglu(h, wgu, wd, ff):
    acc = None
    for c0 in range(0, ff, FF_CHUNK):
        c1 = min(c0 + FF_CHUNK, ff)
        gg = jnp.dot(h, wgu(c0, c1), preferred_element_type=F32)
        uu = jnp.dot(h, wgu(ff + c0, ff + c1), preferred_element_type=F32)
        a = (gg * jax.nn.sigmoid(gg) * uu).astype(BF16)
        part = jnp.dot(a, wd(c0, c1), preferred_element_type=F32)
        acc = part if acc is None else acc + part
    return acc


def _ffn_kernel(x_ref, g_ref, mod_ref, wout_ref, wgu_ref, wd_ref, *refs):
    x = _mix_residual(x_ref[...], mod_ref, wout_ref, refs[:-1])
    y_ref = refs[-1]
    h = _norm_mod(x, g_ref[...], mod_ref[0, 4:5, :], mod_ref[0, 3:4, :]).astype(BF16)
    y = _swiglu(h, lambda a, b: wgu_ref[:, a:b], lambda a, b: wd_ref[a:b, :], wd_ref.shape[0])
    y_ref[...] = x + mod_ref[0, 5:6, :] * y


def ffn_dense(x2, parts, g, mod, w_out, w_gu, w_down, tiles_per_batch, tm):
    m, d = x2.shape
    return pl.pallas_call(
        _ffn_kernel,
        out_shape=jax.ShapeDtypeStruct((m, d), F32),
        grid=(m // tm,),
        in_specs=[pl.BlockSpec((tm, d), lambda i: (i, 0)),
                  pl.BlockSpec((1, d), lambda i: (0, 0)),
                  pl.BlockSpec((1, 6, d), _mod_map(tiles_per_batch)),
                  _resident(w_out.shape, lambda i: (0, 0)),
                  _resident(w_gu.shape, lambda i: (0, 0)),
                  _resident(w_down.shape, lambda i: (0, 0))] + _part_specs(parts, tm),
        out_specs=pl.BlockSpec((tm, d), lambda i: (i, 0)),
        compiler_params=_params("parallel"),
        name="ffn_dense",
    )(x2, g, mod, w_out, w_gu, w_down, *parts)


ROUTE_E1, ROUTE_E2, ROUTE_R1, ROUTE_R2, ROUTE_G1, ROUTE_G2 = range(6)
ROUTE_ROWS = 8


def _router_kernel(x_ref, g_ref, mod_ref, wout_ref, wr_ref, *refs):
    part_refs = refs[:-6]
    x1_ref, h_ref, route_ref, route_t_ref, cnt_ref, base_ref = refs[-6:]

    @pl.when(pl.program_id(0) == 0)
    def _():
        base_ref[...] = jnp.zeros_like(base_ref)

    x = _mix_residual(x_ref[...], mod_ref, wout_ref, part_refs)
    x1_ref[...] = x
    h = _norm_mod(x, g_ref[...], mod_ref[0, 4:5, :], mod_ref[0, 3:4, :]).astype(BF16)
    h_ref[...] = h
    tm = h.shape[0]
    logits = jnp.dot(h, wr_ref[...], preferred_element_type=F32)
    lane = lax.broadcasted_iota(jnp.int32, logits.shape, 1)
    neg = jnp.float32(-jnp.inf)
    logits = jnp.where(lane < N_EXPERTS, logits, neg)
    m1 = jnp.max(logits, axis=-1, keepdims=True)
    i1 = jnp.min(jnp.where(logits == m1, lane, LANES), axis=-1, keepdims=True)
    rest = jnp.where(lane == i1, neg, logits)
    m2 = jnp.max(rest, axis=-1, keepdims=True)
    i2 = jnp.min(jnp.where(rest == m2, lane, LANES), axis=-1, keepdims=True)
    e2 = jnp.exp(m2 - m1)
    g1 = 1.0 / (1.0 + e2)
    g2 = e2 / (1.0 + e2)
    sel1, sel2 = lane == i1, lane == i2
    sel = jnp.where(sel1 | sel2, 1.0, 0.0)
    li = lax.broadcasted_iota(jnp.int32, (tm, tm), 0)
    si = lax.broadcasted_iota(jnp.int32, (tm, tm), 1)
    before = jnp.where(si < li, 1.0, 0.0).astype(BF16)
    rank = jnp.dot(before, sel.astype(BF16), preferred_element_type=F32) + base_ref[...]
    r1 = jnp.sum(jnp.where(sel1, rank, 0.0), axis=-1, keepdims=True)
    r2 = jnp.sum(jnp.where(sel2, rank, 0.0), axis=-1, keepdims=True)
    base_ref[...] += jnp.sum(sel, axis=0, keepdims=True)
    cnt_ref[...] = base_ref[...]
    fields = (i1.astype(F32), i2.astype(F32), r1, r2, g1, g2)
    route = jnp.zeros(logits.shape, F32)
    for k, v in enumerate(fields):
        route = jnp.where(lane == k, v, route)
    route_ref[...] = route
    route_t_ref[...] = route.T[:ROUTE_ROWS, :]


def moe_router(x2, parts, g, mod, w_out, w_router, tiles_per_batch, tm):
    m, d = x2.shape
    return pl.pallas_call(
        _router_kernel,
        out_shape=(jax.ShapeDtypeStruct((m, d), F32), jax.ShapeDtypeStruct((m, d), BF16),
                   jax.ShapeDtypeStruct((m, LANES), F32), jax.ShapeDtypeStruct((ROUTE_ROWS, m), F32),
                   jax.ShapeDtypeStruct((1, LANES), F32)),
        grid=(m // tm,),
        in_specs=[pl.BlockSpec((tm, d), lambda i: (i, 0)),
                  pl.BlockSpec((1, d), lambda i: (0, 0)),
                  pl.BlockSpec((1, 6, d), _mod_map(tiles_per_batch)),
                  _resident(w_out.shape, lambda i: (0, 0)),
                  pl.BlockSpec((d, LANES), lambda i: (0, 0))] + _part_specs(parts, tm),
        out_specs=(pl.BlockSpec((tm, d), lambda i: (i, 0)),
                   pl.BlockSpec((tm, d), lambda i: (i, 0)),
                   pl.BlockSpec((tm, LANES), lambda i: (i, 0)),
                   pl.BlockSpec((ROUTE_ROWS, tm), lambda i: (0, i)),
                   pl.BlockSpec((1, LANES), lambda i: (0, 0))),
        scratch_shapes=[pltpu.VMEM((1, LANES), F32)],
        compiler_params=_params("arbitrary"),
        name="moe_router",
    )(x2, g, mod, w_out, w_router, *parts)


def _moe_kernel(tile_ref, exp_ref, start_ref, end_ref, xg_ref, wgu_ref, wd_ref, y_ref, wgu_b, wd_b):
    w = pl.program_id(0)
    tm = xg_ref.shape[0]
    start, end = start_ref[w], end_ref[w]
    t0 = tile_ref[w] * tm

    @pl.when((w == 0) | (exp_ref[w] != exp_ref[jnp.maximum(w - 1, 0)]))
    def _():
        wgu_b[...] = wgu_ref[0].astype(BF16)
        wd_b[...] = wd_ref[0].astype(BF16)

    @pl.when(end > start)
    def _():
        y = _swiglu(xg_ref[...], lambda a, b: wgu_b[:, a:b], lambda a, b: wd_b[a:b, :], wd_b.shape[0])
        y = y.astype(y_ref.dtype)

        @pl.when(start == t0)
        def _():
            y_ref[...] = y

        @pl.when(start != t0)
        def _():
            row = t0 + lax.broadcasted_iota(jnp.int32, (tm, 1), 0)
            y_ref[...] = jnp.where(row >= start, y, y_ref[...])


def moe_grouped(item_tile, item_expert, item_start, item_end, xg, w_gu, w_down, tm):
    p, d = xg.shape
    grid_spec = pltpu.PrefetchScalarGridSpec(
        num_scalar_prefetch=4,
        grid=(item_tile.shape[0],),
        in_specs=[pl.BlockSpec((tm, d), lambda w, it, ie, s, e: (it[w], 0)),
                  _resident((1,) + w_gu.shape[1:], lambda w, it, ie, s, e: (ie[w], 0, 0)),
                  _resident((1,) + w_down.shape[1:], lambda w, it, ie, s, e: (ie[w], 0, 0))],
        out_specs=pl.BlockSpec((tm, d), lambda w, it, ie, s, e: (it[w], 0)),
        scratch_shapes=[pltpu.VMEM(w_gu.shape[1:], BF16), pltpu.VMEM(w_down.shape[1:], BF16)],
    )
    return pl.pallas_call(
        _moe_kernel,
        out_shape=jax.ShapeDtypeStruct((p, d), BF16),
        grid_spec=grid_spec,
        compiler_params=_params("arbitrary"),
        name="moe_grouped",
    )(item_tile, item_expert, item_start, item_end, xg, w_gu, w_down)


def _moe_combine_kernel(x_ref, ya_ref, yb_ref, route_ref, mod_ref, *refs):
    o_ref = refs[-1]
    r = route_ref[...]
    y = (r[:, ROUTE_G1:ROUTE_G1 + 1] * ya_ref[...].astype(F32) + r[:, ROUTE_G2:ROUTE_G2 + 1] * yb_ref[...].astype(F32))
    x = x_ref[...] + mod_ref[0, 5:6, :] * y
    if len(refs) == 2:
        ms = jnp.mean(x * x, axis=-1, keepdims=True)
        x = x * lax.rsqrt(ms + EPS) * refs[0][...]
    o_ref[...] = x


def moe_combine(x2, ya, yb, route, mod, tiles_per_batch, tm, g_final=None):
    m, d = x2.shape
    row = pl.BlockSpec((tm, d), lambda i: (i, 0))
    specs = [row, row, row, pl.BlockSpec((tm, LANES), lambda i: (i, 0)),
             pl.BlockSpec((1, 6, d), _mod_map(tiles_per_batch))]
    args = [x2, ya, yb, route, mod]
    if g_final is not None:
        specs.append(pl.BlockSpec((1, d), lambda i: (0, 0)))
        args.append(g_final)
    return pl.pallas_call(
        _moe_combine_kernel,
        out_shape=jax.ShapeDtypeStruct((m, d), F32),
        grid=(m // tm,),
        in_specs=specs,
        out_specs=row,
        compiler_params=_params("parallel"),
        name="moe_combine",
    )(*args)


def moe_block(x2, parts, g, mod, w_out, w_router, w_gu, w_down, tiles_per_batch, tm_tok, tm_grp, g_final=None):
    m, d = x2.shape
    x2, h, route, route_t, cnt = moe_router(x2, parts, g, mod, w_out, w_router, tiles_per_batch, tm_tok)
    cnt = cnt[0, :N_EXPERTS].astype(jnp.int32)
    p = m * TOP_K
    off = jnp.cumsum(cnt) - cnt
    field = lambda k: route_t[k].astype(jnp.int32)

    def row_of(e, r):
        for k in range(N_EXPERTS):
            r = r + jnp.where(e == k, off[k], 0)
        return r

    d1 = row_of(field(ROUTE_E1), field(ROUTE_R1))
    d2 = row_of(field(ROUTE_E2), field(ROUTE_R2))
    tok = jnp.arange(m, dtype=jnp.int32)
    _, src = lax.sort_key_val(jnp.concatenate([d1, d2]), jnp.concatenate([tok, tok]))
    n_row_tiles = p // tm_grp
    start = jnp.sort(jnp.concatenate([jnp.arange(n_row_tiles, dtype=jnp.int32) * tm_grp, off]))
    end = jnp.concatenate([start[1:], jnp.full((1,), p, jnp.int32)])
    item_tile = jnp.minimum(start // tm_grp, n_row_tiles - 1)
    item_expert = jnp.sum((off[None, :] <= start[:, None]).astype(jnp.int32), axis=1) - 1
    rows = lambda a, idx: a.at[idx].get(mode="promise_in_bounds")
    yg = moe_grouped(item_tile, item_expert, start, end, rows(h, src), w_gu, w_down, tm_grp)
    return moe_combine(x2, rows(yg, d1), rows(yg, d2), route, mod, tiles_per_batch, tm_tok, g_final)


def _final_norm_kernel(x_ref, g_ref, o_ref):
    x = x_ref[...]
    ms = jnp.mean(x * x, axis=-1, keepdims=True)
    o_ref[...] = x * lax.rsqrt(ms + EPS) * g_ref[...]


def final_norm(x2, g, tm):
    m, d = x2.shape
    return pl.pallas_call(
        _final_norm_kernel,
        out_shape=jax.ShapeDtypeStruct((m, d), F32),
        grid=(m // tm,),
        in_specs=[pl.BlockSpec((tm, d), lambda i: (i, 0)), pl.BlockSpec((1, d), lambda i: (0, 0))],
        out_specs=pl.BlockSpec((tm, d), lambda i: (i, 0)),
        compiler_params=_params("parallel"),
        name="final_norm",
    )(x2, g)


NA_QROWS = 4
NA_KROWS = NA_QROWS + WIN_R
NA_STEP_BLOCKS = 8
Q_COL, K_COL, V_COL = 4, 5, 6


def _na_key_start(j, rows):
    return np.clip(j * NA_QROWS - WIN_R // 2, 0, rows - NA_KROWS)


def _na_bias_index(rows):
    nblk = rows // NA_QROWS
    pats = []
    for j in range(nblk):
        start = _na_key_start(j, rows)
        r = j * NA_QROWS + np.arange(NA_QROWS)
        sr = np.clip(r - WIN_R // 2, 0, rows - WIN_R)
        kr = start + np.arange(NA_KROWS)
        rvalid = (kr[None, :] >= sr[:, None]) & (kr[None, :] < sr[:, None] + WIN_R)
        ri = np.clip(kr[None, :] - r[:, None] + WIN_R - 1, 0, 2 * WIN_R - 2)
        pats.append((ri, rvalid))
    for j in range(2, nblk - 1):
        assert all(np.array_equal(a, b) for a, b in zip(pats[1], pats[j]))
    sel = [pats[0], pats[1], pats[nblk - 1]]
    ri = np.stack([p[0] for p in sel])
    rvalid = np.stack([p[1] for p in sel])
    c = np.arange(GRID_W)
    sc = np.clip(c - WIN_C // 2, 0, GRID_W - WIN_C)
    cvalid = (c[None, :] >= sc[:, None]) & (c[None, :] < sc[:, None] + WIN_C)
    ci = np.clip(c[None, :] - c[:, None] + WIN_C - 1, 0, 2 * WIN_C - 2)
    c_onehot = (ci[..., None] == np.arange(2 * WIN_C - 1)).astype(np.float32)
    valid = rvalid[:, :, None, :, None] & cvalid[None, None, :, None, :]
    return ri, c_onehot, valid


def na_bias_table(rpb, rows):
    ri, c_onehot, valid = _na_bias_index(rows)
    toep = jnp.einsum('hrd,qkd->hrqk', rpb, c_onehot, precision=lax.Precision.HIGHEST)
    b = jnp.take(toep, ri.reshape(-1), axis=1).reshape((NA_HEADS,) + ri.shape + (GRID_W, GRID_W))
    b = jnp.transpose(b, (1, 0, 2, 4, 3, 5))
    b = jnp.where(valid[:, None], b, -jnp.inf)
    return b.reshape(3, NA_HEADS, NA_QROWS * GRID_W, NA_KROWS * GRID_W).astype(F32)


def _attend_heads(q, key_sets, bias_fn):
    n, w = q.shape
    lane = lax.broadcasted_iota(jnp.int32, (1, w), 1)
    head_masks = [(lane >= h * NA_HEAD_DIM) & (lane < (h + 1) * NA_HEAD_DIM) for h in range(NA_HEADS)]
    all_scores = []
    for h, mh in enumerate(head_masks):
        qh = jnp.where(mh, q, 0.0).astype(BF16)
        scores = []
        for i, (k, _) in enumerate(key_sets):
            s = lax.dot_general(qh, k, (((1,), (1,)), ((), ())), preferred_element_type=F32)
            b = bias_fn(i, h)
            scores.append(s if b is None else s + b)
        all_scores.append(scores)
    all_probs = []
    for scores in all_scores:
        m = scores[0].max(axis=-1, keepdims=True)
        for s in scores[1:]:
            m = jnp.maximum(m, s.max(axis=-1, keepdims=True))
        denom = jnp.zeros((n, 1), F32)
        probs = []
        for s in scores:
            p = jnp.exp(s - m)
            denom = denom + p.sum(axis=-1, keepdims=True)
            probs.append(p.astype(BF16))
        all_probs.append((probs, denom))
    out = jnp.zeros((n, w), F32)
    for mh, (probs, denom) in zip(head_masks, all_probs):
        acc = jnp.zeros((n, w), F32)
        for p, (_, v) in zip(probs, key_sets):
            acc = acc + jnp.dot(p, v, preferred_element_type=F32)
        out = out + jnp.where(mh, acc / denom, 0.0)
    return out


def _na_kernel(q_ref, k_ref, v_ref, kc_ref, vc_ref, bias_ref, o_ref, *, rows):
    nblk = rows // NA_QROWS
    nq, nk = NA_QROWS * GRID_W, NA_KROWS * GRID_W
    for s in range(NA_STEP_BLOCKS):
        j = pl.program_id(1) * NA_STEP_BLOCKS + s
        start = jnp.clip(j * NA_QROWS - WIN_R // 2, 0, rows - NA_KROWS)
        t0 = pl.multiple_of(start * GRID_W, GRID_W)
        pattern = jnp.where(j == 0, 0, jnp.where(j == nblk - 1, 2, 1))
        kw = k_ref[pl.ds(t0, nk), :]
        vw = v_ref[pl.ds(t0, nk), :]
        q = q_ref[s * nq:(s + 1) * nq, :] * (1.0 / math.sqrt(NA_HEAD_DIM))
        o = _attend_heads(q, [(kw, vw), (kc_ref[...], vc_ref[...])],
                          lambda i, h: bias_ref[pattern, h] if i == 0 else None)
        o_ref[s * nq:(s + 1) * nq, :] = o.astype(o_ref.dtype)


def na_attention(pr, pr_c, bias, bn):
    S, Lc = pr.shape[0] // bn, pr_c.shape[0] // bn
    rows = S // GRID_W
    nsteps = rows // NA_QROWS // NA_STEP_BLOCKS
    nq = NA_STEP_BLOCKS * NA_QROWS * GRID_W
    return pl.pallas_call(
        functools.partial(_na_kernel, rows=rows),
        out_shape=jax.ShapeDtypeStruct((bn * S, W_NA), BF16),
        grid=(bn, nsteps),
        in_specs=[pl.BlockSpec((nq, W_NA), lambda b, j: (b * nsteps + j, Q_COL)),
                  pl.BlockSpec((S, W_NA), lambda b, j: (b, K_COL)),
                  pl.BlockSpec((S, W_NA), lambda b, j: (b, V_COL)),
                  pl.BlockSpec((Lc, W_NA), lambda b, j: (b, K_COL)),
                  pl.BlockSpec((Lc, W_NA), lambda b, j: (b, V_COL)),
                  _resident(bias.shape, lambda b, j: (0, 0, 0, 0))],
        out_specs=pl.BlockSpec((nq, W_NA), lambda b, j: (b * nsteps + j, 0)),
        compiler_params=_params("parallel", "arbitrary"),
        name="na_attention",
    )(pr, pr, pr, pr_c, pr_c, bias)


def _ctx_attn_kernel(q_ref, k_ref, v_ref, o_ref):
    q = q_ref[...] * (1.0 / math.sqrt(NA_HEAD_DIM))
    o = _attend_heads(q, [(k_ref[...], v_ref[...])], lambda i, h: None)
    o_ref[...] = o.astype(o_ref.dtype)


def ctx_attention(pr_c, bn):
    Lc = pr_c.shape[0] // bn
    return pl.pallas_call(
        _ctx_attn_kernel,
        out_shape=jax.ShapeDtypeStruct((bn * Lc, W_NA), BF16),
        grid=(bn,),
        in_specs=[pl.BlockSpec((Lc, W_NA), lambda b: (b, Q_COL)),
                  pl.BlockSpec((Lc, W_NA), lambda b: (b, K_COL)),
                  pl.BlockSpec((Lc, W_NA), lambda b: (b, V_COL))],
        out_specs=pl.BlockSpec((Lc, W_NA), lambda b: (b, 0)),
        compiler_params=_params("parallel"),
        name="ctx_attention",
    )(pr_c, pr_c, pr_c)


POOL_COL, CONV_H_COL, CONV_B_COL, CONV_C_COL = 0, 1, 2, 3


def _shift_rows(x, k, row):
    n = x.shape[0]
    y = pltpu.roll(x, k % n, 0)
    return jnp.where(row < k, 0.0, y) if k > 0 else jnp.where(row >= n + k, 0.0, y)


def _local_mix_kernel(u_ref, h_ref, bg_ref, cg_ref, pw_ref, ps_ref, cw_ref, op_ref, oc_ref, *, seq):
    row = lax.broadcasted_iota(jnp.int32, (seq, 1), 0)
    lane = lax.broadcasted_iota(jnp.int32, (1, W_POOL), 1)
    u = u_ref[...].astype(F32)
    trailing, leading = {1: u}, {1: u}
    for w in (1, 2, 4):
        trailing[2 * w] = trailing[w] + _shift_rows(trailing[w], w, row)
        leading[2 * w] = leading[w] + _shift_rows(leading[w], -w, row)
    rowf = row.astype(F32)
    win_sum = jnp.zeros_like(u)
    cnt = jnp.zeros_like(u)
    for g, win in enumerate(POOL_WINDOWS):
        half = win // 2
        mg = (lane >= g * POOL_GROUP) & (lane < (g + 1) * POOL_GROUP)
        s = _shift_rows(trailing[half], 1, row) + leading[half]
        n = jnp.minimum(rowf + half, float(seq)) - jnp.maximum(rowf - half, 0.0)
        win_sum = jnp.where(mg, s, win_sum)
        cnt = jnp.where(mg, n, cnt)
    p = win_sum / cnt - u
    pooled = jnp.dot(p.astype(BF16), pw_ref[...], preferred_element_type=F32) * ps_ref[...]
    op_ref[...] = pooled.astype(op_ref.dtype)
    v = cg_ref[...].astype(F32) * h_ref[...].astype(F32)
    conv = (_shift_rows(v, 1, row) * cw_ref[0:1, :] + v * cw_ref[1:2, :] + _shift_rows(v, -1, row) * cw_ref[2:3, :])
    oc_ref[...] = (bg_ref[...].astype(F32) * conv).astype(oc_ref.dtype)


def local_mix(pr, pool_w, pool_scale, conv_w, bn):
    seq = pr.shape[0] // bn
    pw = jax.scipy.linalg.block_diag(*[pool_w[g] for g in range(len(POOL_WINDOWS))]).astype(BF16)
    blk = lambda col: pl.BlockSpec((seq, W_POOL), lambda b: (b, col))
    full = lambda a: pl.BlockSpec(a.shape, lambda b: (0,) * a.ndim)
    args = (pw, pool_scale[None, :], conv_w)
    return pl.pallas_call(
        functools.partial(_local_mix_kernel, seq=seq),
        out_shape=(jax.ShapeDtypeStruct((bn * seq, W_POOL), BF16), jax.ShapeDtypeStruct((bn * seq, W_CONV), BF16)),
        grid=(bn,),
        in_specs=[blk(POOL_COL), blk(CONV_H_COL), blk(CONV_B_COL), blk(CONV_C_COL)] + [full(a) for a in args],
        out_specs=(pl.BlockSpec((seq, W_POOL), lambda b: (b, 0)),
                   pl.BlockSpec((seq, W_CONV), lambda b: (b, 0))),
        compiler_params=_params("parallel"),
        name="local_mix",
    )(pr, pr, pr, pr, *args)


Z_COL, XS_COL, BS_COL, CS_COL = 7, 8, 9, 10
DT_COL = 22
HEADS_PER_GROUP = SSD_HEADS // SSD_GROUPS
GROUP_W = HEADS_PER_GROUP * SSD_HEAD_DIM


def _head_mask(h):
    lane = lax.broadcasted_iota(jnp.int32, (1, W_SSD), 1)
    return (lane >= h * SSD_HEAD_DIM) & (lane < (h + 1) * SSD_HEAD_DIM)


def _expand_heads(v, d):
    out = jnp.zeros((v.shape[0], W_SSD), F32)
    for h in range(SSD_HEADS):
        k = d * SSD_HEADS + h
        out = jnp.where(_head_mask(h), v[:, k:k + 1], out)
    return out


def _ssd_chunk_terms(c, xs_ref, b_ref, c_ref, dt_ref, la_ref, dsk_ref, y_ref, upd_ref, dec_ref, need_y):
    T = SSD_CHUNK
    r0 = pl.multiple_of(c * T, T)
    xs = xs_ref[pl.ds(r0, T), :]
    bm = b_ref[pl.ds(r0, T), :]
    cm = c_ref[pl.ds(r0, T), :]
    dt = dt_ref[pl.ds(r0, T), :]
    la = la_ref[pl.ds(r0, T), :]
    li = lax.broadcasted_iota(jnp.int32, (T, T), 0)
    si = lax.broadcasted_iota(jnp.int32, (T, T), 1)
    causal = si <= li
    prefix = jnp.dot(causal.astype(F32), la, precision=lax.Precision.HIGHEST, preferred_element_type=F32)
    total = prefix[T - 1:T, :]
    lane = lax.broadcasted_iota(jnp.int32, (1, LANES), 1)
    acum = jnp.where(lane < SSD_HEADS, prefix, total - prefix + la)
    bt = bm.astype(F32).T.astype(BF16)
    if need_y:
        acum_t = acum.T
        cb = [lax.dot_general(cm[:, g * SSD_STATE:(g + 1) * SSD_STATE], bm[:, g * SSD_STATE:(g + 1) * SSD_STATE],
                              (((1,), (1,)), ((), ())), preferred_element_type=F32) for g in range(SSD_GROUPS)]
        y = jnp.zeros((T, W_SSD), F32)
    per_dir = []
    for d in range(2):
        acum_e = _expand_heads(acum, d)
        tot_e = _expand_heads(total, d)
        xdt = xs * _expand_heads(dt, d)
        xw = (xdt * jnp.exp(tot_e - acum_e)).astype(BF16)
        dec_ref[c, d, 0:T, :] = jnp.exp(acum_e)
        dec_ref[c, d, T:T + 1, :] = jnp.exp(tot_e)
        scores = []
        if need_y:
            mask = causal if d == 0 else (si >= li)
            for h in range(SSD_HEADS):
                k = d * SSD_HEADS + h
                decay = jnp.exp(jnp.where(mask, acum[:, k:k + 1] - acum_t[k:k + 1, :], -jnp.inf))
                scores.append((cb[h // HEADS_PER_GROUP] * decay).astype(BF16))
        per_dir.append((xw, xdt, scores))
    for d, (xw, xdt, scores) in enumerate(per_dir):
        upd = [jnp.dot(bt[g * SSD_STATE:(g + 1) * SSD_STATE, :], xw[:, g * GROUP_W:(g + 1) * GROUP_W],
                       preferred_element_type=F32) for g in range(SSD_GROUPS)]
        upd_ref[c, d] = jnp.concatenate(upd, axis=1)
        if need_y:
            x_heads = [jnp.where(_head_mask(h), xdt, 0.0).astype(BF16) for h in range(SSD_HEADS)]
            y = y + jnp.dot(jnp.concatenate(scores, axis=1), jnp.concatenate(x_heads, axis=0),
                            preferred_element_type=F32)
    if need_y:
        y_ref[pl.ds(r0, T), :] = xs * dsk_ref[...] + y


def _ssd_chunk_state(c, d, c_ref, y_ref, upd_ref, dec_ref, st_ref, need_y):
    T = SSD_CHUNK
    r0 = pl.multiple_of(c * T, T)
    st = st_ref[d]
    if need_y:
        cm = c_ref[pl.ds(r0, T), :]
        st_b = st.astype(BF16)
        ys = [jnp.dot(cm[:, g * SSD_STATE:(g + 1) * SSD_STATE], st_b[:, g * GROUP_W:(g + 1) * GROUP_W],
                      preferred_element_type=F32) for g in range(SSD_GROUPS)]
        y_ref[pl.ds(r0, T), :] += jnp.concatenate(ys, axis=1) * dec_ref[c, d, 0:T, :]
    st_ref[d] = dec_ref[c, d, T:T + 1, :] * st + upd_ref[c, d]


def _ssd_kernel(z_ref, xs_in, bs_in, cs_in, dtr_ref, cw_ref, cb_ref, dtb_ref, alog_ref, dsk_ref, ng_ref, h0_ref,
                *refs, seq, need_y):
    if need_y:
        o_ref, hT_ref, xs_ref, b_ref, c_ref, dt_ref, la_ref, upd_ref, dec_ref, st_ref, y_ref = refs
    else:
        hT_ref, xs_ref, b_ref, c_ref, dt_ref, la_ref, upd_ref, dec_ref, st_ref = refs
        y_ref = None
    nc = seq // SSD_CHUNK
    row = lax.broadcasted_iota(jnp.int32, (seq, 1), 0)
    for gi, (src, dst) in enumerate(((xs_in, xs_ref), (bs_in, b_ref), (cs_in, c_ref))):
        sl = slice(gi * W_SSD, (gi + 1) * W_SSD)
        x = src[...].astype(F32)
        cv = (_shift_rows(x, 1, row) * cw_ref[0:1, sl] + x * cw_ref[1:2, sl]
              + _shift_rows(x, -1, row) * cw_ref[2:3, sl] + cb_ref[:, sl])
        dst[...] = (cv * jax.nn.sigmoid(cv)).astype(dst.dtype)
    dtv = dtr_ref[...] + dtb_ref[...]
    dt = jnp.maximum(dtv, 0.0) + jnp.log1p(jnp.exp(-jnp.abs(dtv)))
    dt_ref[...] = dt
    la_ref[...] = dt * (-jnp.exp(alog_ref[...]))
    st_ref[...] = h0_ref[0]

    def terms(c, carry):
        _ssd_chunk_terms(c, xs_ref, b_ref, c_ref, dt_ref, la_ref, dsk_ref, y_ref, upd_ref, dec_ref, need_y)
        return carry

    lax.fori_loop(0, nc, terms, 0, unroll=min(16, nc))

    def recur(i, carry):
        _ssd_chunk_state(i, 0, c_ref, y_ref, upd_ref, dec_ref, st_ref, need_y)
        _ssd_chunk_state(nc - 1 - i, 1, c_ref, y_ref, upd_ref, dec_ref, st_ref, need_y)
        return carry

    lax.fori_loop(0, nc, recur, 0, unroll=min(16, nc))
    hT_ref[0] = st_ref[...]
    if need_y:
        z = z_ref[...].astype(F32)
        yz = y_ref[...] * (z * jax.nn.sigmoid(z))
        ms = jnp.mean(yz * yz, axis=-1, keepdims=True)
        o_ref[...] = (yz * lax.rsqrt(ms + EPS) * ng_ref[...]).astype(o_ref.dtype)


def ssd_scan(pr, dt, h0, conv_w, conv_b, dt_bias, a_log, d_skip, norm_g, need_y):
    bn = h0.shape[0]
    seq = pr.shape[0] // bn
    nc = seq // SSD_CHUNK
    pad = LANES - 2 * SSD_HEADS
    dtb = jnp.pad(dt_bias.reshape(1, -1), ((0, 0), (0, pad)))
    alog = jnp.pad(a_log.reshape(1, -1), ((0, 0), (0, pad)))
    dsk = jnp.repeat(d_skip, SSD_HEAD_DIM)[None, :]
    blk = lambda col: pl.BlockSpec((seq, W_SSD), lambda b: (b, col))
    full = lambda a: pl.BlockSpec(a.shape, lambda b: (0,) * a.ndim)
    st_shape = (1, 2, SSD_STATE, W_SSD)
    st_spec = pl.BlockSpec(st_shape, lambda b: (b, 0, 0, 0))
    args = (conv_w, conv_b[None, :], dtb, alog, dsk, norm_g[None, :])
    out_shape = [jax.ShapeDtypeStruct((bn,) + st_shape[1:], F32)]
    out_specs = [st_spec]
    scratch = [pltpu.VMEM((seq, W_SSD), F32),
               pltpu.VMEM((seq, W_SSD), BF16), pltpu.VMEM((seq, W_SSD), BF16),
               pltpu.VMEM((seq, LANES), F32), pltpu.VMEM((seq, LANES), F32),
               pltpu.VMEM((nc, 2, SSD_STATE, W_SSD), F32),
               pltpu.VMEM((nc, 2, SSD_CHUNK + 8, W_SSD), F32),
               pltpu.VMEM(st_shape[1:], F32)]
    if need_y:
        out_shape.insert(0, jax.ShapeDtypeStruct((bn * seq, W_SSD), BF16))
        out_specs.insert(0, pl.BlockSpec((seq, W_SSD), lambda b: (b, 0)))
        scratch.append(pltpu.VMEM((seq, W_SSD), F32))
    res = pl.pallas_call(
        functools.partial(_ssd_kernel, seq=seq, need_y=need_y),
        out_shape=out_shape,
        grid=(bn,),
        in_specs=[blk(Z_COL), blk(XS_COL), blk(BS_COL), blk(CS_COL),
                  pl.BlockSpec((seq, LANES), lambda b: (b, 0))]
                 + [full(a) for a in args] + [st_spec],
        out_specs=out_specs,
        scratch_shapes=scratch,
        compiler_params=_params("parallel"),
        name="ssd_scan",
    )(pr, pr, pr, pr, dt, *args, h0)
    return (res[0], res[1]) if need_y else (None, res[0])


def ssd_mix(pr, dt, pr_c, dt_c, conv_w, conv_b, dt_bias, a_log, d_skip, norm_g, ctx_out, bn):
    h0 = jnp.zeros((bn, 2, SSD_STATE, W_SSD), F32)
    oc, hc = ssd_scan(pr_c, dt_c, h0, conv_w, conv_b, dt_bias, a_log, d_skip, norm_g, ctx_out)
    o, _ = ssd_scan(pr, dt, hc, conv_w, conv_b, dt_bias, a_log, d_skip, norm_g, True)
    return o, oc


def mixer(pr, dt, pr_c, dt_c, pool_w, pool_scale, conv_w, na_bias, s_conv_w, s_conv_b, dt_bias, a_log, d_skip,
          s_norm_g, ctx_out, bn):
    o_ssd, oc_ssd = ssd_mix(pr, dt, pr_c, dt_c, s_conv_w, s_conv_b, dt_bias, a_log, d_skip, s_norm_g, ctx_out, bn)
    o = [*local_mix(pr, pool_w, pool_scale, conv_w, bn), na_attention(pr, pr_c, na_bias, bn), o_ssd]
    if not ctx_out:
        return o, None
    oc = [*local_mix(pr_c, pool_w, pool_scale, conv_w, bn), ctx_attention(pr_c, bn), oc_ssd]
    return o, oc


TM = 512
TM_ROUTE = 1024
TM_PROJ = 1024
TM_GRP = 512


def kernel(x, c, ctx, c_ctx, w_ada, b_ada, g_mix, g_ffn, w_in, w_out, pool_w, pool_scale, conv_w, na_rpb,
           ssd_conv_w, ssd_conv_b, ssd_dt_bias, ssd_a_log, ssd_d, ssd_norm_g, ffn_w_gu, ffn_w_down,
           moe_router, moe_w_gu, moe_w_down, g_final):
    bn, S, d = x.shape
    Lc = ctx.shape[1]
    m, mc = bn * S, bn * Lc
    tpb = S // TM
    x2 = x.reshape(m, d)
    xc2 = ctx.reshape(mc, d)
    c_rows = jnp.concatenate([c, c_ctx[None, :], jnp.zeros((7, d), F32)], axis=0)
    for l in range(DEPTH):
        ctx_out = l < DEPTH - 1
        last = l == DEPTH - 1
        ada = ada_modulation(c_rows, w_ada[l].astype(BF16), b_ada[l][None, :])
        mod = ada[:bn].reshape(bn, 6, d)
        mod_c = ada[bn:bn + 1].reshape(1, 6, d)
        w_in_l = jnp.pad(w_in[l], ((0, 0), (0, D_IN_PAD - D_IN_PROJ))).astype(BF16)
        g_m = g_mix[l][None, :]
        g_f = g_ffn[l][None, :]
        pr, dt = in_proj(x2, g_m, mod, w_in_l, S // TM_PROJ, TM_PROJ)
        pr_c, dt_c = in_proj(xc2, g_m, mod_c, w_in_l, None, min(TM_PROJ, mc))
        na_bias = na_bias_table(na_rpb[l], S // GRID_W)
        o, oc = mixer(pr, dt, pr_c, dt_c, pool_w[l], pool_scale[l], conv_w[l], na_bias, ssd_conv_w[l],
                      ssd_conv_b[l], ssd_dt_bias[l], ssd_a_log[l], ssd_d[l], ssd_norm_g[l], ctx_out, bn)
        w_out_l = w_out[l].astype(BF16)
        j = l // 2
        if l % 2 == 0:
            w_gu = ffn_w_gu[j].astype(BF16)
            w_dn = ffn_w_down[j].astype(BF16)
            x2 = ffn_dense(x2, o, g_f, mod, w_out_l, w_gu, w_dn, tpb, TM)
            if ctx_out:
                xc2 = ffn_dense(xc2, oc, g_f, mod_c, w_out_l, w_gu, w_dn, None, min(TM, mc))
        else:
            w_r = jnp.pad(moe_router[j], ((0, 0), (0, LANES - N_EXPERTS))).astype(BF16)
            w_gu, w_dn = moe_w_gu[j], moe_w_down[j]
            x2 = moe_block(x2, o, g_f, mod, w_out_l, w_r, w_gu, w_dn, S // TM_ROUTE, TM_ROUTE, TM_GRP,
                           g_final[None, :] if last else None)
            if ctx_out:
                xc2 = moe_block(xc2, oc, g_f, mod_c, w_out_l, w_r, w_gu, w_dn, None, min(TM_ROUTE, mc), TM_GRP)
            if last:
                return x2.reshape(bn, S, d)
    return final_norm(x2, g_final[None, :], TM).reshape(bn, S, d)
```

```python
import functools
import math

import numpy as np
import jax
import jax.numpy as jnp
from jax import lax
from jax.experimental import pallas as pl
from jax.experimental.pallas import tpu as pltpu

DEPTH = 2
GRID_W = 64
EPS = 1e-6
W_POOL = 256
W_CONV = 256
W_NA = 256
W_SSD = 256
POOL_WINDOWS = (2, 4, 8, 16)
POOL_GROUP = W_POOL // len(POOL_WINDOWS)
NA_HEADS = 4
NA_HEAD_DIM = W_NA // NA_HEADS
WIN_R = 8
WIN_C = 16
SSD_HEAD_DIM = 64
SSD_HEADS = W_SSD // SSD_HEAD_DIM
SSD_GROUPS = 2
SSD_STATE = 128
SSD_CHUNK = 128
SSD_XBC = W_SSD + 2 * SSD_GROUPS * SSD_STATE
D_IN_PROJ = W_POOL + 3 * W_CONV + 3 * W_NA + W_SSD + SSD_XBC + 2 * SSD_HEADS
N_EXPERTS = 8
TOP_K = 2

LANES = 128
D_IN_PAD = -(-D_IN_PROJ // LANES) * LANES
VMEM_LIMIT = 56 * 1024 * 1024
BF16 = jnp.bfloat16
F32 = jnp.float32


def _params(*sem):
    return pltpu.CompilerParams(dimension_semantics=sem, vmem_limit_bytes=VMEM_LIMIT)


def _norm_mod(x, g, scale, shift):
    ms = jnp.mean(x * x, axis=-1, keepdims=True)
    return (x * lax.rsqrt(ms + EPS) * g) * (1.0 + scale) + shift


def _mod_map(tiles_per_batch):
    if tiles_per_batch is None:
        return lambda i, *_: (0, 0, 0)
    return lambda i, *_: (i // tiles_per_batch, 0, 0)


def _resident(shape, index_map):
    return pl.BlockSpec(shape, index_map, pipeline_mode=pl.Buffered(1))


def _ada_kernel(c_ref, w_ref, b_ref, o_ref):
    c = c_ref[...]
    s = c * jax.nn.sigmoid(c)
    o_ref[...] = jnp.dot(s.astype(BF16), w_ref[...], preferred_element_type=F32) + b_ref[...]


def ada_modulation(c_rows, w, b):
    r, d = c_rows.shape
    n = w.shape[1]
    tn = 1536
    return pl.pallas_call(
        _ada_kernel,
        out_shape=jax.ShapeDtypeStruct((r, n), F32),
        grid=(n // tn,),
        in_specs=[pl.BlockSpec((r, d), lambda j: (0, 0)),
                  pl.BlockSpec((d, tn), lambda j: (0, j)),
                  pl.BlockSpec((1, tn), lambda j: (0, j))],
        out_specs=pl.BlockSpec((r, tn), lambda j: (0, j)),
        compiler_params=_params("arbitrary"),
        name="ada_modulation",
    )(c_rows, w, b)


PROJ_ROWS = 512


def _in_proj_kernel(x_ref, g_ref, mod_ref, w_ref, o_ref, dt_ref):
    tm = x_ref.shape[0]
    parts = [slice(r0, min(r0 + PROJ_ROWS, tm)) for r0 in range(0, tm, PROJ_ROWS)]
    hs = [_norm_mod(x_ref[rs, :], g_ref[...], mod_ref[0, 1:2, :], mod_ref[0, 0:1, :]).astype(BF16) for rs in parts]
    for rs, h in zip(parts, hs):
        pr = jnp.dot(h, w_ref[...], preferred_element_type=F32)
        o_ref[rs, :] = pr.astype(o_ref.dtype)
        dt_ref[rs, :] = pr[:, DT_COL * LANES:(DT_COL + 1) * LANES]


def in_proj(x2, g, mod, w, tiles_per_batch, tm):
    m, d = x2.shape
    n = w.shape[1]
    return pl.pallas_call(
        _in_proj_kernel,
        out_shape=(jax.ShapeDtypeStruct((m, n), BF16), jax.ShapeDtypeStruct((m, LANES), F32)),
        grid=(m // tm,),
        in_specs=[pl.BlockSpec((tm, d), lambda i: (i, 0)),
                  pl.BlockSpec((1, d), lambda i: (0, 0)),
                  pl.BlockSpec((1, 6, d), _mod_map(tiles_per_batch)),
                  _resident((d, n), lambda i: (0, 0))],
        out_specs=(pl.BlockSpec((tm, n), lambda i: (i, 0)), pl.BlockSpec((tm, LANES), lambda i: (i, 0))),
        compiler_params=_params("parallel"),
        name="in_proj",
    )(x2, g, mod, w)


def _mix_residual(x, mod_ref, w_ref, part_refs):
    y, k0 = None, 0
    for p_ref in part_refs:
        k = p_ref.shape[1]
        t = jnp.dot(p_ref[...], w_ref[k0:k0 + k, :], preferred_element_type=F32)
        y = t if y is None else y + t
        k0 += k
    return x + mod_ref[0, 2:3, :] * y


def _part_specs(parts, tm):
    return [pl.BlockSpec((tm, p.shape[1]), lambda i, *_: (i, 0)) for p in parts]


FF_CHUNK = 1024


def _swiglu(h, wgu, wd, ff):
    acc = None
    for c0 in range(0, ff, FF_CHUNK):
        c1 = min(c0 + FF_CHUNK, ff)
        gg = jnp.dot(h, wgu(c0, c1), preferred_element_type=F32)
        uu = jnp.dot(h, wgu(ff + c0, ff + c1), preferred_element_type=F32)
        a = (gg * jax.nn.sigmoid(gg) * uu).astype(BF16)
        part = jnp.dot(a, wd(c0, c1), preferred_element_type=F32)
        acc = part if acc is None else acc + part
    return acc


def _ffn_kernel(x_ref, g_ref, mod_ref, wout_ref, wgu_ref, wd_ref, *refs):
    x = _mix_residual(x_ref[...], mod_ref, wout_ref, refs[:-1])
    y_ref = refs[-1]
    h = _norm_mod(x, g_ref[...], mod_ref[0, 4:5, :], mod_ref[0, 3:4, :]).astype(BF16)
    y = _swiglu(h, lambda a, b: wgu_ref[:, a:b], lambda a, b: wd_ref[a:b, :], wd_ref.shape[0])
    y_ref[...] = x + mod_ref[0, 5:6, :] * y


def ffn_dense(x2, parts, g, mod, w_out, w_gu, w_down, tiles_per_batch, tm):
    m, d = x2.shape
    return pl.pallas_call(
        _ffn_kernel,
        out_shape=jax.ShapeDtypeStruct((m, d), F32),
        grid=(m // tm,),
        in_specs=[pl.BlockSpec((tm, d), lambda i: (i, 0)),
                  pl.BlockSpec((1, d), lambda i: (0, 0)),
                  pl.BlockSpec((1, 6, d), _mod_map(tiles_per_batch)),
                  _resident(w_out.shape, lambda i: (0, 0)),
                  _resident(w_gu.shape, lambda i: (0, 0)),
                  _resident(w_down.shape, lambda i: (0, 0))] + _part_specs(parts, tm),
        out_specs=pl.BlockSpec((tm, d), lambda i: (i, 0)),
        compiler_params=_params("parallel"),
        name="ffn_dense",
    )(x2, g, mod, w_out, w_gu, w_down, *parts)


ROUTE_E1, ROUTE_E2, ROUTE_R1, ROUTE_R2, ROUTE_G1, ROUTE_G2 = range(6)
ROUTE_ROWS = 8


def _router_kernel(x_ref, g_ref, mod_ref, wout_ref, wr_ref, *refs):
    part_refs = refs[:-6]
    x1_ref, h_ref, route_ref, route_t_ref, cnt_ref, base_ref = refs[-6:]

    @pl.when(pl.program_id(0) == 0)
    def _():
        base_ref[...] = jnp.zeros_like(base_ref)

    x = _mix_residual(x_ref[...], mod_ref, wout_ref, part_refs)
    x1_ref[...] = x
    h = _norm_mod(x, g_ref[...], mod_ref[0, 4:5, :], mod_ref[0, 3:4, :]).astype(BF16)
    h_ref[...] = h
    tm = h.shape[0]
    logits = jnp.dot(h, wr_ref[...], preferred_element_type=F32)
    lane = lax.broadcasted_iota(jnp.int32, logits.shape, 1)
    neg = jnp.float32(-jnp.inf)
    logits = jnp.where(lane < N_EXPERTS, logits, neg)
    m1 = jnp.max(logits, axis=-1, keepdims=True)
    i1 = jnp.min(jnp.where(logits == m1, lane, LANES), axis=-1, keepdims=True)
    rest = jnp.where(lane == i1, neg, logits)
    m2 = jnp.max(rest, axis=-1, keepdims=True)
    i2 = jnp.min(jnp.where(rest == m2, lane, LANES), axis=-1, keepdims=True)
    e2 = jnp.exp(m2 - m1)
    g1 = 1.0 / (1.0 + e2)
    g2 = e2 / (1.0 + e2)
    sel1, sel2 = lane == i1, lane == i2
    sel = jnp.where(sel1 | sel2, 1.0, 0.0)
    li = lax.broadcasted_iota(jnp.int32, (tm, tm), 0)
    si = lax.broadcasted_iota(jnp.int32, (tm, tm), 1)
    before = jnp.where(si < li, 1.0, 0.0).astype(BF16)
    rank = jnp.dot(before, sel.astype(BF16), preferred_element_type=F32) + base_ref[...]
    r1 = jnp.sum(jnp.where(sel1, rank, 0.0), axis=-1, keepdims=True)
    r2 = jnp.sum(jnp.where(sel2, rank, 0.0), axis=-1, keepdims=True)
    base_ref[...] += jnp.sum(sel, axis=0, keepdims=True)
    cnt_ref[...] = base_ref[...]
    fields = (i1.astype(F32), i2.astype(F32), r1, r2, g1, g2)
    route = jnp.zeros(logits.shape, F32)
    for k, v in enumerate(fields):
        route = jnp.where(lane == k, v, route)
    route_ref[...] = route
    route_t_ref[...] = route.T[:ROUTE_ROWS, :]


def moe_router(x2, parts, g, mod, w_out, w_router, tiles_per_batch, tm):
    m, d = x2.shape
    return pl.pallas_call(
        _router_kernel,
        out_shape=(jax.ShapeDtypeStruct((m, d), F32), jax.ShapeDtypeStruct((m, d), BF16),
                   jax.ShapeDtypeStruct((m, LANES), F32), jax.ShapeDtypeStruct((ROUTE_ROWS, m), F32),
                   jax.ShapeDtypeStruct((1, LANES), F32)),
        grid=(m // tm,),
        in_specs=[pl.BlockSpec((tm, d), lambda i: (i, 0)),
                  pl.BlockSpec((1, d), lambda i: (0, 0)),
                  pl.BlockSpec((1, 6, d), _mod_map(tiles_per_batch)),
                  _resident(w_out.shape, lambda i: (0, 0)),
                  pl.BlockSpec((d, LANES), lambda i: (0, 0))] + _part_specs(parts, tm),
        out_specs=(pl.BlockSpec((tm, d), lambda i: (i, 0)),
                   pl.BlockSpec((tm, d), lambda i: (i, 0)),
                   pl.BlockSpec((tm, LANES), lambda i: (i, 0)),
                   pl.BlockSpec((ROUTE_ROWS, tm), lambda i: (0, i)),
                   pl.BlockSpec((1, LANES), lambda i: (0, 0))),
        scratch_shapes=[pltpu.VMEM((1, LANES), F32)],
        compiler_params=_params("arbitrary"),
        name="moe_router",
    )(x2, g, mod, w_out, w_router, *parts)


def _moe_kernel(tile_ref, exp_ref, start_ref, end_ref, xg_ref, wgu_ref, wd_ref, y_ref, wgu_b, wd_b):
    w = pl.program_id(0)
    tm = xg_ref.shape[0]
    start, end = start_ref[w], end_ref[w]
    t0 = tile_ref[w] * tm

    @pl.when((w == 0) | (exp_ref[w] != exp_ref[jnp.maximum(w - 1, 0)]))
    def _():
        wgu_b[...] = wgu_ref[0].astype(BF16)
        wd_b[...] = wd_ref[0].astype(BF16)

    @pl.when(end > start)
    def _():
        y = _swiglu(xg_ref[...], lambda a, b: wgu_b[:, a:b], lambda a, b: wd_b[a:b, :], wd_b.shape[0])
        y = y.astype(y_ref.dtype)

        @pl.when(start == t0)
        def _():
            y_ref[...] = y

        @pl.when(start != t0)
        def _():
            row = t0 + lax.broadcasted_iota(jnp.int32, (tm, 1), 0)
            y_ref[...] = jnp.where(row >= start, y, y_ref[...])


def moe_grouped(item_tile, item_expert, item_start, item_end, xg, w_gu, w_down, tm):
    p, d = xg.shape
    grid_spec = pltpu.PrefetchScalarGridSpec(
        num_scalar_prefetch=4,
        grid=(item_tile.shape[0],),
        in_specs=[pl.BlockSpec((tm, d), lambda w, it, ie, s, e: (it[w], 0)),
                  _resident((1,) + w_gu.shape[1:], lambda w, it, ie, s, e: (ie[w], 0, 0)),
                  _resident((1,) + w_down.shape[1:], lambda w, it, ie, s, e: (ie[w], 0, 0))],
        out_specs=pl.BlockSpec((tm, d), lambda w, it, ie, s, e: (it[w], 0)),
        scratch_shapes=[pltpu.VMEM(w_gu.shape[1:], BF16), pltpu.VMEM(w_down.shape[1:], BF16)],
    )
    return pl.pallas_call(
        _moe_kernel,
        out_shape=jax.ShapeDtypeStruct((p, d), BF16),
        grid_spec=grid_spec,
        compiler_params=_params("arbitrary"),
        name="moe_grouped",
    )(item_tile, item_expert, item_start, item_end, xg, w_gu, w_down)


def _moe_combine_kernel(x_ref, ya_ref, yb_ref, route_ref, mod_ref, *refs):
    o_ref = refs[-1]
    r = route_ref[...]
    y = (r[:, ROUTE_G1:ROUTE_G1 + 1] * ya_ref[...].astype(F32) + r[:, ROUTE_G2:ROUTE_G2 + 1] * yb_ref[...].astype(F32))
    x = x_ref[...] + mod_ref[0, 5:6, :] * y
    if len(refs) == 2:
        ms = jnp.mean(x * x, axis=-1, keepdims=True)
        x = x * lax.rsqrt(ms + EPS) * refs[0][...]
    o_ref[...] = x


def moe_combine(x2, ya, yb, route, mod, tiles_per_batch, tm, g_final=None):
    m, d = x2.shape
    row = pl.BlockSpec((tm, d), lambda i: (i, 0))
    specs = [row, row, row, pl.BlockSpec((tm, LANES), lambda i: (i, 0)),
             pl.BlockSpec((1, 6, d), _mod_map(tiles_per_batch))]
    args = [x2, ya, yb, route, mod]
    if g_final is not None:
        specs.append(pl.BlockSpec((1, d), lambda i: (0, 0)))
        args.append(g_final)
    return pl.pallas_call(
        _moe_combine_kernel,
        out_shape=jax.ShapeDtypeStruct((m, d), F32),
        grid=(m // tm,),
        in_specs=specs,
        out_specs=row,
        compiler_params=_params("parallel"),
        name="moe_combine",
    )(*args)


def moe_block(x2, parts, g, mod, w_out, w_router, w_gu, w_down, tiles_per_batch, tm_tok, tm_grp, g_final=None):
    m, d = x2.shape
    x2, h, route, route_t, cnt = moe_router(x2, parts, g, mod, w_out, w_router, tiles_per_batch, tm_tok)
    cnt = cnt[0, :N_EXPERTS].astype(jnp.int32)
    p = m * TOP_K
    off = jnp.cumsum(cnt) - cnt
    field = lambda k: route_t[k].astype(jnp.int32)

    def row_of(e, r):
        for k in range(N_EXPERTS):
            r = r + jnp.where(e == k, off[k], 0)
        return r

    d1 = row_of(field(ROUTE_E1), field(ROUTE_R1))
    d2 = row_of(field(ROUTE_E2), field(ROUTE_R2))
    tok = jnp.arange(m, dtype=jnp.int32)
    _, src = lax.sort_key_val(jnp.concatenate([d1, d2]), jnp.concatenate([tok, tok]))
    n_row_tiles = p // tm_grp
    start = jnp.sort(jnp.concatenate([jnp.arange(n_row_tiles, dtype=jnp.int32) * tm_grp, off]))
    end = jnp.concatenate([start[1:], jnp.full((1,), p, jnp.int32)])
    item_tile = jnp.minimum(start // tm_grp, n_row_tiles - 1)
    item_expert = jnp.sum((off[None, :] <= start[:, None]).astype(jnp.int32), axis=1) - 1
    rows = lambda a, idx: a.at[idx].get(mode="promise_in_bounds")
    yg = moe_grouped(item_tile, item_expert, start, end, rows(h, src), w_gu, w_down, tm_grp)
    return moe_combine(x2, rows(yg, d1), rows(yg, d2), route, mod, tiles_per_batch, tm_tok, g_final)


def _final_norm_kernel(x_ref, g_ref, o_ref):
    x = x_ref[...]
    ms = jnp.mean(x * x, axis=-1, keepdims=True)
    o_ref[...] = x * lax.rsqrt(ms + EPS) * g_ref[...]


def final_norm(x2, g, tm):
    m, d = x2.shape
    return pl.pallas_call(
        _final_norm_kernel,
        out_shape=jax.ShapeDtypeStruct((m, d), F32),
        grid=(m // tm,),
        in_specs=[pl.BlockSpec((tm, d), lambda i: (i, 0)), pl.BlockSpec((1, d), lambda i: (0, 0))],
        out_specs=pl.BlockSpec((tm, d), lambda i: (i, 0)),
        compiler_params=_params("parallel"),
        name="final_norm",
    )(x2, g)


NA_QROWS = 4
NA_KROWS = NA_QROWS + WIN_R
NA_STEP_BLOCKS = 8
Q_COL, K_COL, V_COL = 4, 5, 6


def _na_key_start(j, rows):
    return np.clip(j * NA_QROWS - WIN_R // 2, 0, rows - NA_KROWS)


def _na_bias_index(rows):
    nblk = rows // NA_QROWS
    pats = []
    for j in range(nblk):
        start = _na_key_start(j, rows)
        r = j * NA_QROWS + np.arange(NA_QROWS)
        sr = np.clip(r - WIN_R // 2, 0, rows - WIN_R)
        kr = start + np.arange(NA_KROWS)
        rvalid = (kr[None, :] >= sr[:, None]) & (kr[None, :] < sr[:, None] + WIN_R)
        ri = np.clip(kr[None, :] - r[:, None] + WIN_R - 1, 0, 2 * WIN_R - 2)
        pats.append((ri, rvalid))
    for j in range(2, nblk - 1):
        assert all(np.array_equal(a, b) for a, b in zip(pats[1], pats[j]))
    sel = [pats[0], pats[1], pats[nblk - 1]]
    ri = np.stack([p[0] for p in sel])
    rvalid = np.stack([p[1] for p in sel])
    c = np.arange(GRID_W)
    sc = np.clip(c - WIN_C // 2, 0, GRID_W - WIN_C)
    cvalid = (c[None, :] >= sc[:, None]) & (c[None, :] < sc[:, None] + WIN_C)
    ci = np.clip(c[None, :] - c[:, None] + WIN_C - 1, 0, 2 * WIN_C - 2)
    c_onehot = (ci[..., None] == np.arange(2 * WIN_C - 1)).astype(np.float32)
    valid = rvalid[:, :, None, :, None] & cvalid[None, None, :, None, :]
    return ri, c_onehot, valid


def na_bias_table(rpb, rows):
    ri, c_onehot, valid = _na_bias_index(rows)
    toep = jnp.einsum('hrd,qkd->hrqk', rpb, c_onehot, precision=lax.Precision.HIGHEST)
    b = jnp.take(toep, ri.reshape(-1), axis=1).reshape((NA_HEADS,) + ri.shape + (GRID_W, GRID_W))
    b = jnp.transpose(b, (1, 0, 2, 4, 3, 5))
    b = jnp.where(valid[:, None], b, -jnp.inf)
    return b.reshape(3, NA_HEADS, NA_QROWS * GRID_W, NA_KROWS * GRID_W).astype(F32)


def _attend_heads(q, key_sets, bias_fn):
    n, w = q.shape
    lane = lax.broadcasted_iota(jnp.int32, (1, w), 1)
    head_masks = [(lane >= h * NA_HEAD_DIM) & (lane < (h + 1) * NA_HEAD_DIM) for h in range(NA_HEADS)]
    all_scores = []
    for h, mh in enumerate(head_masks):
        qh = jnp.where(mh, q, 0.0).astype(BF16)
        scores = []
        for i, (k, _) in enumerate(key_sets):
            s = lax.dot_general(qh, k, (((1,), (1,)), ((), ())), preferred_element_type=F32)
            b = bias_fn(i, h)
            scores.append(s if b is None else s + b)
        all_scores.append(scores)
    all_probs = []
    for scores in all_scores:
        m = scores[0].max(axis=-1, keepdims=True)
        for s in scores[1:]:
            m = jnp.maximum(m, s.max(axis=-1, keepdims=True))
        denom = jnp.zeros((n, 1), F32)
        probs = []
        for s in scores:
            p = jnp.exp(s - m)
            denom = denom + p.sum(axis=-1, keepdims=True)
            probs.append(p.astype(BF16))
        all_probs.append((probs, denom))
    out = jnp.zeros((n, w), F32)
    for mh, (probs, denom) in zip(head_masks, all_probs):
        acc = jnp.zeros((n, w), F32)
        for p, (_, v) in zip(probs, key_sets):
            acc = acc + jnp.dot(p, v, preferred_element_type=F32)
        out = out + jnp.where(mh, acc / denom, 0.0)
    return out


def _na_kernel(q_ref, k_ref, v_ref, kc_ref, vc_ref, bias_ref, o_ref, *, rows):
    nblk = rows // NA_QROWS
    nq, nk = NA_QROWS * GRID_W, NA_KROWS * GRID_W
    for s in range(NA_STEP_BLOCKS):
        j = pl.program_id(1) * NA_STEP_BLOCKS + s
        start = jnp.clip(j * NA_QROWS - WIN_R // 2, 0, rows - NA_KROWS)
        t0 = pl.multiple_of(start * GRID_W, GRID_W)
        pattern = jnp.where(j == 0, 0, jnp.where(j == nblk - 1, 2, 1))
        kw = k_ref[pl.ds(t0, nk), :]
        vw = v_ref[pl.ds(t0, nk), :]
        q = q_ref[s * nq:(s + 1) * nq, :] * (1.0 / math.sqrt(NA_HEAD_DIM))
        o = _attend_heads(q, [(kw, vw), (kc_ref[...], vc_ref[...])],
                          lambda i, h: bias_ref[pattern, h] if i == 0 else None)
        o_ref[s * nq:(s + 1) * nq, :] = o.astype(o_ref.dtype)


def na_attention(pr, pr_c, bias, bn):
    S, Lc = pr.shape[0] // bn, pr_c.shape[0] // bn
    rows = S // GRID_W
    nsteps = rows // NA_QROWS // NA_STEP_BLOCKS
    nq = NA_STEP_BLOCKS * NA_QROWS * GRID_W
    return pl.pallas_call(
        functools.partial(_na_kernel, rows=rows),
        out_shape=jax.ShapeDtypeStruct((bn * S, W_NA), BF16),
        grid=(bn, nsteps),
        in_specs=[pl.BlockSpec((nq, W_NA), lambda b, j: (b * nsteps + j, Q_COL)),
                  pl.BlockSpec((S, W_NA), lambda b, j: (b, K_COL)),
                  pl.BlockSpec((S, W_NA), lambda b, j: (b, V_COL)),
                  pl.BlockSpec((Lc, W_NA), lambda b, j: (b, K_COL)),
                  pl.BlockSpec((Lc, W_NA), lambda b, j: (b, V_COL)),
                  _resident(bias.shape, lambda b, j: (0, 0, 0, 0))],
        out_specs=pl.BlockSpec((nq, W_NA), lambda b, j: (b * nsteps + j, 0)),
        compiler_params=_params("parallel", "arbitrary"),
        name="na_attention",
    )(pr, pr, pr, pr_c, pr_c, bias)


def _ctx_attn_kernel(q_ref, k_ref, v_ref, o_ref):
    q = q_ref[...] * (1.0 / math.sqrt(NA_HEAD_DIM))
    o = _attend_heads(q, [(k_ref[...], v_ref[...])], lambda i, h: None)
    o_ref[...] = o.astype(o_ref.dtype)


def ctx_attention(pr_c, bn):
    Lc = pr_c.shape[0] // bn
    return pl.pallas_call(
        _ctx_attn_kernel,
        out_shape=jax.ShapeDtypeStruct((bn * Lc, W_NA), BF16),
        grid=(bn,),
        in_specs=[pl.BlockSpec((Lc, W_NA), lambda b: (b, Q_COL)),
                  pl.BlockSpec((Lc, W_NA), lambda b: (b, K_COL)),
                  pl.BlockSpec((Lc, W_NA), lambda b: (b, V_COL))],
        out_specs=pl.BlockSpec((Lc, W_NA), lambda b: (b, 0)),
        compiler_params=_params("parallel"),
        name="ctx_attention",
    )(pr_c, pr_c, pr_c)


POOL_COL, CONV_H_COL, CONV_B_COL, CONV_C_COL = 0, 1, 2, 3


def _shift_rows(x, k, row):
    n = x.shape[0]
    y = pltpu.roll(x, k % n, 0)
    return jnp.where(row < k, 0.0, y) if k > 0 else jnp.where(row >= n + k, 0.0, y)


def _local_mix_kernel(u_ref, h_ref, bg_ref, cg_ref, pw_ref, ps_ref, cw_ref, op_ref, oc_ref, *, seq):
    row = lax.broadcasted_iota(jnp.int32, (seq, 1), 0)
    lane = lax.broadcasted_iota(jnp.int32, (1, W_POOL), 1)
    u = u_ref[...].astype(F32)
    trailing, leading = {1: u}, {1: u}
    for w in (1, 2, 4):
        trailing[2 * w] = trailing[w] + _shift_rows(trailing[w], w, row)
        leading[2 * w] = leading[w] + _shift_rows(leading[w], -w, row)
    rowf = row.astype(F32)
    win_sum = jnp.zeros_like(u)
    cnt = jnp.zeros_like(u)
    for g, win in enumerate(POOL_WINDOWS):
        half = win // 2
        mg = (lane >= g * POOL_GROUP) & (lane < (g + 1) * POOL_GROUP)
        s = _shift_rows(trailing[half], 1, row) + leading[half]
        n = jnp.minimum(rowf + half, float(seq)) - jnp.maximum(rowf - half, 0.0)
        win_sum = jnp.where(mg, s, win_sum)
        cnt = jnp.where(mg, n, cnt)
    p = win_sum / cnt - u
    pooled = jnp.dot(p.astype(BF16), pw_ref[...], preferred_element_type=F32) * ps_ref[...]
    op_ref[...] = pooled.astype(op_ref.dtype)
    v = cg_ref[...].astype(F32) * h_ref[...].astype(F32)
    conv = (_shift_rows(v, 1, row) * cw_ref[0:1, :] + v * cw_ref[1:2, :] + _shift_rows(v, -1, row) * cw_ref[2:3, :])
    oc_ref[...] = (bg_ref[...].astype(F32) * conv).astype(oc_ref.dtype)


def local_mix(pr, pool_w, pool_scale, conv_w, bn):
    seq = pr.shape[0] // bn
    pw = jax.scipy.linalg.block_diag(*[pool_w[g] for g in range(len(POOL_WINDOWS))]).astype(BF16)
    blk = lambda col: pl.BlockSpec((seq, W_POOL), lambda b: (b, col))
    full = lambda a: pl.BlockSpec(a.shape, lambda b: (0,) * a.ndim)
    args = (pw, pool_scale[None, :], conv_w)
    return pl.pallas_call(
        functools.partial(_local_mix_kernel, seq=seq),
        out_shape=(jax.ShapeDtypeStruct((bn * seq, W_POOL), BF16), jax.ShapeDtypeStruct((bn * seq, W_CONV), BF16)),
        grid=(bn,),
        in_specs=[blk(POOL_COL), blk(CONV_H_COL), blk(CONV_B_COL), blk(CONV_C_COL)] + [full(a) for a in args],
        out_specs=(pl.BlockSpec((seq, W_POOL), lambda b: (b, 0)),
                   pl.BlockSpec((seq, W_CONV), lambda b: (b, 0))),
        compiler_params=_params("parallel"),
        name="local_mix",
    )(pr, pr, pr, pr, *args)


Z_COL, XS_COL, BS_COL, CS_COL = 7, 8, 9, 10
DT_COL = 22
HEADS_PER_GROUP = SSD_HEADS // SSD_GROUPS
GROUP_W = HEADS_PER_GROUP * SSD_HEAD_DIM


def _head_mask(h):
    lane = lax.broadcasted_iota(jnp.int32, (1, W_SSD), 1)
    return (lane >= h * SSD_HEAD_DIM) & (lane < (h + 1) * SSD_HEAD_DIM)


def _expand_heads(v, d):
    out = jnp.zeros((v.shape[0], W_SSD), F32)
    for h in range(SSD_HEADS):
        k = d * SSD_HEADS + h
        out = jnp.where(_head_mask(h), v[:, k:k + 1], out)
    return out


def _ssd_chunk_terms(c, xs_ref, b_ref, c_ref, dt_ref, la_ref, dsk_ref, y_ref, upd_ref, dec_ref, need_y):
    T = SSD_CHUNK
    r0 = pl.multiple_of(c * T, T)
    xs = xs_ref[pl.ds(r0, T), :]
    bm = b_ref[pl.ds(r0, T), :]
    cm = c_ref[pl.ds(r0, T), :]
    dt = dt_ref[pl.ds(r0, T), :]
    la = la_ref[pl.ds(r0, T), :]
    li = lax.broadcasted_iota(jnp.int32, (T, T), 0)
    si = lax.broadcasted_iota(jnp.int32, (T, T), 1)
    causal = si <= li
    prefix = jnp.dot(causal.astype(F32), la, precision=lax.Precision.HIGHEST, preferred_element_type=F32)
    total = prefix[T - 1:T, :]
    lane = lax.broadcasted_iota(jnp.int32, (1, LANES), 1)
    acum = jnp.where(lane < SSD_HEADS, prefix, total - prefix + la)
    bt = bm.astype(F32).T.astype(BF16)
    if need_y:
        acum_t = acum.T
        cb = [lax.dot_general(cm[:, g * SSD_STATE:(g + 1) * SSD_STATE], bm[:, g * SSD_STATE:(g + 1) * SSD_STATE],
                              (((1,), (1,)), ((), ())), preferred_element_type=F32) for g in range(SSD_GROUPS)]
        y = jnp.zeros((T, W_SSD), F32)
    per_dir = []
    for d in range(2):
        acum_e = _expand_heads(acum, d)
        tot_e = _expand_heads(total, d)
        xdt = xs * _expand_heads(dt, d)
        xw = (xdt * jnp.exp(tot_e - acum_e)).astype(BF16)
        dec_ref[c, d, 0:T, :] = jnp.exp(acum_e)
        dec_ref[c, d, T:T + 1, :] = jnp.exp(tot_e)
        scores = []
        if need_y:
            mask = causal if d == 0 else (si >= li)
            for h in range(SSD_HEADS):
                k = d * SSD_HEADS + h
                decay = jnp.exp(jnp.where(mask, acum[:, k:k + 1] - acum_t[k:k + 1, :], -jnp.inf))
                scores.append((cb[h // HEADS_PER_GROUP] * decay).astype(BF16))
        per_dir.append((xw, xdt, scores))
    for d, (xw, xdt, scores) in enumerate(per_dir):
        upd = [jnp.dot(bt[g * SSD_STATE:(g + 1) * SSD_STATE, :], xw[:, g * GROUP_W:(g + 1) * GROUP_W],
                       preferred_element_type=F32) for g in range(SSD_GROUPS)]
        upd_ref[c, d] = jnp.concatenate(upd, axis=1)
        if need_y:
            x_heads = [jnp.where(_head_mask(h), xdt, 0.0).astype(BF16) for h in range(SSD_HEADS)]
            y = y + jnp.dot(jnp.concatenate(scores, axis=1), jnp.concatenate(x_heads, axis=0),
                            preferred_element_type=F32)
    if need_y:
        y_ref[pl.ds(r0, T), :] = xs * dsk_ref[...] + y


def _ssd_chunk_state(c, d, c_ref, y_ref, upd_ref, dec_ref, st_ref, need_y):
    T = SSD_CHUNK
    r0 = pl.multiple_of(c * T, T)
    st = st_ref[d]
    if need_y:
        cm = c_ref[pl.ds(r0, T), :]
        st_b = st.astype(BF16)
        ys = [jnp.dot(cm[:, g * SSD_STATE:(g + 1) * SSD_STATE], st_b[:, g * GROUP_W:(g + 1) * GROUP_W],
                      preferred_element_type=F32) for g in range(SSD_GROUPS)]
        y_ref[pl.ds(r0, T), :] += jnp.concatenate(ys, axis=1) * dec_ref[c, d, 0:T, :]
    st_ref[d] = dec_ref[c, d, T:T + 1, :] * st + upd_ref[c, d]


def _ssd_kernel(z_ref, xs_in, bs_in, cs_in, dtr_ref, cw_ref, cb_ref, dtb_ref, alog_ref, dsk_ref, ng_ref, h0_ref,
                *refs, seq, need_y):
    if need_y:
        o_ref, hT_ref, xs_ref, b_ref, c_ref, dt_ref, la_ref, upd_ref, dec_ref, st_ref, y_ref = refs
    else:
        hT_ref, xs_ref, b_ref, c_ref, dt_ref, la_ref, upd_ref, dec_ref, st_ref = refs
        y_ref = None
    nc = seq // SSD_CHUNK
    row = lax.broadcasted_iota(jnp.int32, (seq, 1), 0)
    for gi, (src, dst) in enumerate(((xs_in, xs_ref), (bs_in, b_ref), (cs_in, c_ref))):
        sl = slice(gi * W_SSD, (gi + 1) * W_SSD)
        x = src[...].astype(F32)
        cv = (_shift_rows(x, 1, row) * cw_ref[0:1, sl] + x * cw_ref[1:2, sl]
              + _shift_rows(x, -1, row) * cw_ref[2:3, sl] + cb_ref[:, sl])
        dst[...] = (cv * jax.nn.sigmoid(cv)).astype(dst.dtype)
    dtv = dtr_ref[...] + dtb_ref[...]
    dt = jnp.maximum(dtv, 0.0) + jnp.log1p(jnp.exp(-jnp.abs(dtv)))
    dt_ref[...] = dt
    la_ref[...] = dt * (-jnp.exp(alog_ref[...]))
    st_ref[...] = h0_ref[0]

    def terms(c, carry):
        _ssd_chunk_terms(c, xs_ref, b_ref, c_ref, dt_ref, la_ref, dsk_ref, y_ref, upd_ref, dec_ref, need_y)
        return carry

    lax.fori_loop(0, nc, terms, 0, unroll=min(16, nc))

    def recur(i, carry):
        _ssd_chunk_state(i, 0, c_ref, y_ref, upd_ref, dec_ref, st_ref, need_y)
        _ssd_chunk_state(nc - 1 - i, 1, c_ref, y_ref, upd_ref, dec_ref, st_ref, need_y)
        return carry

    lax.fori_loop(0, nc, recur, 0, unroll=min(16, nc))
    hT_ref[0] = st_ref[...]
    if need_y:
        z = z_ref[...].astype(F32)
        yz = y_ref[...] * (z * jax.nn.sigmoid(z))
        ms = jnp.mean(yz * yz, axis=-1, keepdims=True)
        o_ref[...] = (yz * lax.rsqrt(ms + EPS) * ng_ref[...]).astype(o_ref.dtype)


def ssd_scan(pr, dt, h0, conv_w, conv_b, dt_bias, a_log, d_skip, norm_g, need_y):
    bn = h0.shape[0]
    seq = pr.shape[0] // bn
    nc = seq // SSD_CHUNK
    pad = LANES - 2 * SSD_HEADS
    dtb = jnp.pad(dt_bias.reshape(1, -1), ((0, 0), (0, pad)))
    alog = jnp.pad(a_log.reshape(1, -1), ((0, 0), (0, pad)))
    dsk = jnp.repeat(d_skip, SSD_HEAD_DIM)[None, :]
    blk = lambda col: pl.BlockSpec((seq, W_SSD), lambda b: (b, col))
    full = lambda a: pl.BlockSpec(a.shape, lambda b: (0,) * a.ndim)
    st_shape = (1, 2, SSD_STATE, W_SSD)
    st_spec = pl.BlockSpec(st_shape, lambda b: (b, 0, 0, 0))
    args = (conv_w, conv_b[None, :], dtb, alog, dsk, norm_g[None, :])
    out_shape = [jax.ShapeDtypeStruct((bn,) + st_shape[1:], F32)]
    out_specs = [st_spec]
    scratch = [pltpu.VMEM((seq, W_SSD), F32),
               pltpu.VMEM((seq, W_SSD), BF16), pltpu.VMEM((seq, W_SSD), BF16),
               pltpu.VMEM((seq, LANES), F32), pltpu.VMEM((seq, LANES), F32),
               pltpu.VMEM((nc, 2, SSD_STATE, W_SSD), F32),
               pltpu.VMEM((nc, 2, SSD_CHUNK + 8, W_SSD), F32),
               pltpu.VMEM(st_shape[1:], F32)]
    if need_y:
        out_shape.insert(0, jax.ShapeDtypeStruct((bn * seq, W_SSD), BF16))
        out_specs.insert(0, pl.BlockSpec((seq, W_SSD), lambda b: (b, 0)))
        scratch.append(pltpu.VMEM((seq, W_SSD), F32))
    res = pl.pallas_call(
        functools.partial(_ssd_kernel, seq=seq, need_y=need_y),
        out_shape=out_shape,
        grid=(bn,),
        in_specs=[blk(Z_COL), blk(XS_COL), blk(BS_COL), blk(CS_COL),
                  pl.BlockSpec((seq, LANES), lambda b: (b, 0))]
                 + [full(a) for a in args] + [st_spec],
        out_specs=out_specs,
        scratch_shapes=scratch,
        compiler_params=_params("parallel"),
        name="ssd_scan",
    )(pr, pr, pr, pr, dt, *args, h0)
    return (res[0], res[1]) if need_y else (None, res[0])


def ssd_mix(pr, dt, pr_c, dt_c, conv_w, conv_b, dt_bias, a_log, d_skip, norm_g, ctx_out, bn):
    h0 = jnp.zeros((bn, 2, SSD_STATE, W_SSD), F32)
    oc, hc = ssd_scan(pr_c, dt_c, h0, conv_w, conv_b, dt_bias, a_log, d_skip, norm_g, ctx_out)
    o, _ = ssd_scan(pr, dt, hc, conv_w, conv_b, dt_bias, a_log, d_skip, norm_g, True)
    return o, oc


def mixer(pr, dt, pr_c, dt_c, pool_w, pool_scale, conv_w, na_bias, s_conv_w, s_conv_b, dt_bias, a_log, d_skip,
          s_norm_g, ctx_out, bn):
    o_ssd, oc_ssd = ssd_mix(pr, dt, pr_c, dt_c, s_conv_w, s_conv_b, dt_bias, a_log, d_skip, s_norm_g, ctx_out, bn)
    o = [*local_mix(pr, pool_w, pool_scale, conv_w, bn), na_attention(pr, pr_c, na_bias, bn), o_ssd]
    if not ctx_out:
        return o, None
    oc = [*local_mix(pr_c, pool_w, pool_scale, conv_w, bn), ctx_attention(pr_c, bn), oc_ssd]
    return o, oc


TM = 512
TM_ROUTE = 1024
TM_PROJ = 1024
TM_GRP = 512


def kernel(x, c, ctx, c_ctx, w_ada, b_ada, g_mix, g_ffn, w_in, w_out, pool_w, pool_scale, conv_w, na_rpb,
           ssd_conv_w, ssd_conv_b, ssd_dt_bias, ssd_a_log, ssd_d, ssd_norm_g, ffn_w_gu, ffn_w_down,
           moe_router, moe_w_gu, moe_w_down, g_final):
    bn, S, d = x.shape
    Lc = ctx.shape[1]
    m, mc = bn * S, bn * Lc
    tpb = S // TM
    x2 = x.reshape(m, d)
    xc2 = ctx.reshape(mc, d)
    c_rows = jnp.concatenate([c, c_ctx[None, :], jnp.zeros((7, d), F32)], axis=0)
    for l in range(DEPTH):
        ctx_out = l < DEPTH - 1
        last = l == DEPTH - 1
        ada = ada_modulation(c_rows, w_ada[l].astype(BF16), b_ada[l][None, :])
        mod = ada[:bn].reshape(bn, 6, d)
        mod_c = ada[bn:bn + 1].reshape(1, 6, d)
        w_in_l = jnp.pad(w_in[l], ((0, 0), (0, D_IN_PAD - D_IN_PROJ))).astype(BF16)
        g_m = g_mix[l][None, :]
        g_f = g_ffn[l][None, :]
        pr, dt = in_proj(x2, g_m, mod, w_in_l, S // TM_PROJ, TM_PROJ)
        pr_c, dt_c = in_proj(xc2, g_m, mod_c, w_in_l, None, min(TM_PROJ, mc))
        na_bias = na_bias_table(na_rpb[l], S // GRID_W)
        o, oc = mixer(pr, dt, pr_c, dt_c, pool_w[l], pool_scale[l], conv_w[l], na_bias, ssd_conv_w[l],
                      ssd_conv_b[l], ssd_dt_bias[l], ssd_a_log[l], ssd_d[l], ssd_norm_g[l], ctx_out, bn)
        w_out_l = w_out[l].astype(BF16)
        j = l // 2
        if l % 2 == 0:
            w_gu = ffn_w_gu[j].astype(BF16)
            w_dn = ffn_w_down[j].astype(BF16)
            x2 = ffn_dense(x2, o, g_f, mod, w_out_l, w_gu, w_dn, tpb, TM)
            if ctx_out:
                xc2 = ffn_dense(xc2, oc, g_f, mod_c, w_out_l, w_gu, w_dn, None, min(TM, mc))
        else:
            w_r = jnp.pad(moe_router[j], ((0, 0), (0, LANES - N_EXPERTS))).astype(BF16)
            w_gu, w_dn = moe_w_gu[j], moe_w_down[j]
            x2 = moe_block(x2, o, g_f, mod, w_out_l, w_r, w_gu, w_dn, S // TM_ROUTE, TM_ROUTE, TM_GRP,
                           g_final[None, :] if last else None)
            if ctx_out:
                xc2 = moe_block(xc2, oc, g_f, mod_c, w_out_l, w_r, w_gu, w_dn, None, min(TM_ROUTE, mc), TM_GRP)
            if last:
                return x2.reshape(bn, S, d)
    return final_norm(x2, g_final[None, :], TM).reshape(bn, S, d)
```

```python
import functools
import math

import numpy as np
import jax
import jax.numpy as jnp
from jax import lax
from jax.experimental import pallas as pl
from jax.experimental.pallas import tpu as pltpu

DEPTH = 2
GRID_W = 64
EPS = 1e-6
W_POOL = 256
W_CONV = 256
W_NA = 256
W_SSD = 256
POOL_WINDOWS = (2, 4, 8, 16)
POOL_GROUP = W_POOL // len(POOL_WINDOWS)
NA_HEADS = 4
NA_HEAD_DIM = W_NA // NA_HEADS
WIN_R = 8
WIN_C = 16
SSD_HEAD_DIM = 64
SSD_HEADS = W_SSD // SSD_HEAD_DIM
SSD_GROUPS = 2
SSD_STATE = 128
SSD_CHUNK = 128
SSD_XBC = W_SSD + 2 * SSD_GROUPS * SSD_STATE
D_IN_PROJ = W_POOL + 3 * W_CONV + 3 * W_NA + W_SSD + SSD_XBC + 2 * SSD_HEADS
N_EXPERTS = 8
TOP_K = 2

LANES = 128
D_IN_PAD = -(-D_IN_PROJ // LANES) * LANES
VMEM_LIMIT = 56 * 1024 * 1024
BF16 = jnp.bfloat16
F32 = jnp.float32


def _params(*sem):
    return pltpu.CompilerParams(dimension_semantics=sem, vmem_limit_bytes=VMEM_LIMIT)


def _norm_mod(x, g, scale, shift):
    ms = jnp.mean(x * x, axis=-1, keepdims=True)
    return (x * lax.rsqrt(ms + EPS) * g) * (1.0 + scale) + shift


def _mod_map(tiles_per_batch):
    if tiles_per_batch is None:
        return lambda i, *_: (0, 0, 0)
    return lambda i, *_: (i // tiles_per_batch, 0, 0)


def _resident(shape, index_map):
    return pl.BlockSpec(shape, index_map, pipeline_mode=pl.Buffered(1))


def _ada_kernel(c_ref, w_ref, b_ref, o_ref):
    c = c_ref[...]
    s = c * jax.nn.sigmoid(c)
    o_ref[...] = jnp.dot(s.astype(BF16), w_ref[...], preferred_element_type=F32) + b_ref[...]


def ada_modulation(c_rows, w, b):
    r, d = c_rows.shape
    n = w.shape[1]
    tn = 1536
    return pl.pallas_call(
        _ada_kernel,
        out_shape=jax.ShapeDtypeStruct((r, n), F32),
        grid=(n // tn,),
        in_specs=[pl.BlockSpec((r, d), lambda j: (0, 0)),
                  pl.BlockSpec((d, tn), lambda j: (0, j)),
                  pl.BlockSpec((1, tn), lambda j: (0, j))],
        out_specs=pl.BlockSpec((r, tn), lambda j: (0, j)),
        compiler_params=_params("arbitrary"),
        name="ada_modulation",
    )(c_rows, w, b)


PROJ_ROWS = 512


def _in_proj_kernel(x_ref, g_ref, mod_ref, w_ref, o_ref, dt_ref):
    tm = x_ref.shape[0]
    parts = [slice(r0, min(r0 + PROJ_ROWS, tm)) for r0 in range(0, tm, PROJ_ROWS)]
    hs = [_norm_mod(x_ref[rs, :], g_ref[...], mod_ref[0, 1:2, :], mod_ref[0, 0:1, :]).astype(BF16) for rs in parts]
    for rs, h in zip(parts, hs):
        pr = jnp.dot(h, w_ref[...], preferred_element_type=F32)
        o_ref[rs, :] = pr.astype(o_ref.dtype)
        dt_ref[rs, :] = pr[:, DT_COL * LANES:(DT_COL + 1) * LANES]


def in_proj(x2, g, mod, w, tiles_per_batch, tm):
    m, d = x2.shape
    n = w.shape[1]
    return pl.pallas_call(
        _in_proj_kernel,
        out_shape=(jax.ShapeDtypeStruct((m, n), BF16), jax.ShapeDtypeStruct((m, LANES), F32)),
        grid=(m // tm,),
        in_specs=[pl.BlockSpec((tm, d), lambda i: (i, 0)),
                  pl.BlockSpec((1, d), lambda i: (0, 0)),
                  pl.BlockSpec((1, 6, d), _mod_map(tiles_per_batch)),
                  _resident((d, n), lambda i: (0, 0))],
        out_specs=(pl.BlockSpec((tm, n), lambda i: (i, 0)), pl.BlockSpec((tm, LANES), lambda i: (i, 0))),
        compiler_params=_params("parallel"),
        name="in_proj",
    )(x2, g, mod, w)


def _mix_residual(x, mod_ref, w_ref, part_refs):
    y, k0 = None, 0
    for p_ref in part_refs:
        k = p_ref.shape[1]
        t = jnp.dot(p_ref[...], w_ref[k0:k0 + k, :], preferred_element_type=F32)
        y = t if y is None else y + t
        k0 += k
    return x + mod_ref[0, 2:3, :] * y


def _part_specs(parts, tm):
    return [pl.BlockSpec((tm, p.shape[1]), lambda i, *_: (i, 0)) for p in parts]


FF_CHUNK = 512


def _swiglu(h, wgu, wd, ff):
    acc = None
    for c0 in range(0, ff, FF_CHUNK):
        c1 = min(c0 + FF_CHUNK, ff)
        gg = jnp.dot(h, wgu(c0, c1), preferred_element_type=F32)
        uu = jnp.dot(h, wgu(ff + c0, ff + c1), preferred_element_type=F32)
        a = (gg * jax.nn.sigmoid(gg) * uu).astype(BF16)
        part = jnp.dot(a, wd(c0, c1), preferred_element_type=F32)
        acc = part if acc is None else acc + part
    return acc


def _ffn_kernel(x_ref, g_ref, mod_ref, wout_ref, wgu_ref, wd_ref, *refs):
    x = _mix_residual(x_ref[...], mod_ref, wout_ref, refs[:-1])
    y_ref = refs[-1]
    h = _norm_mod(x, g_ref[...], mod_ref[0, 4:5, :], mod_ref[0, 3:4, :]).astype(BF16)
    y = _swiglu(h, lambda a, b: wgu_ref[:, a:b], lambda a, b: wd_ref[a:b, :], wd_ref.shape[0])
    y_ref[...] = x + mod_ref[0, 5:6, :] * y


def ffn_dense(x2, parts, g, mod, w_out, w_gu, w_down, tiles_per_batch, tm):
    m, d = x2.shape
    return pl.pallas_call(
        _ffn_kernel,
        out_shape=jax.ShapeDtypeStruct((m, d), F32),
        grid=(m // tm,),
        in_specs=[pl.BlockSpec((tm, d), lambda i: (i, 0)),
                  pl.BlockSpec((1, d), lambda i: (0, 0)),
                  pl.BlockSpec((1, 6, d), _mod_map(tiles_per_batch)),
                  _resident(w_out.shape, lambda i: (0, 0)),
                  _resident(w_gu.shape, lambda i: (0, 0)),
                  _resident(w_down.shape, lambda i: (0, 0))] + _part_specs(parts, tm),
        out_specs=pl.BlockSpec((tm, d), lambda i: (i, 0)),
        compiler_params=_params("parallel"),
        name="ffn_dense",
    )(x2, g, mod, w_out, w_gu, w_down, *parts)


ROUTE_E1, ROUTE_E2, ROUTE_R1, ROUTE_R2, ROUTE_G1, ROUTE_G2 = range(6)
ROUTE_ROWS = 8


def _router_kernel(x_ref, g_ref, mod_ref, wout_ref, wr_ref, *refs):
    part_refs = refs[:-6]
    x1_ref, h_ref, route_ref, route_t_ref, cnt_ref, base_ref = refs[-6:]

    @pl.when(pl.program_id(0) == 0)
    def _():
        base_ref[...] = jnp.zeros_like(base_ref)

    x = _mix_residual(x_ref[...], mod_ref, wout_ref, part_refs)
    x1_ref[...] = x
    h = _norm_mod(x, g_ref[...], mod_ref[0, 4:5, :], mod_ref[0, 3:4, :]).astype(BF16)
    h_ref[...] = h
    tm = h.shape[0]
    logits = jnp.dot(h, wr_ref[...], preferred_element_type=F32)
    lane = lax.broadcasted_iota(jnp.int32, logits.shape, 1)
    neg = jnp.float32(-jnp.inf)
    logits = jnp.where(lane < N_EXPERTS, logits, neg)
    m1 = jnp.max(logits, axis=-1, keepdims=True)
    i1 = jnp.min(jnp.where(logits == m1, lane, LANES), axis=-1, keepdims=True)
    rest = jnp.where(lane == i1, neg, logits)
    m2 = jnp.max(rest, axis=-1, keepdims=True)
    i2 = jnp.min(jnp.where(rest == m2, lane, LANES), axis=-1, keepdims=True)
    e2 = jnp.exp(m2 - m1)
    g1 = 1.0 / (1.0 + e2)
    g2 = e2 / (1.0 + e2)
    sel1, sel2 = lane == i1, lane == i2
    sel = jnp.where(sel1 | sel2, 1.0, 0.0)
    li = lax.broadcasted_iota(jnp.int32, (tm, tm), 0)
    si = lax.broadcasted_iota(jnp.int32, (tm, tm), 1)
    before = jnp.where(si < li, 1.0, 0.0).astype(BF16)
    rank = jnp.dot(before, sel.astype(BF16), preferred_element_type=F32) + base_ref[...]
    r1 = jnp.sum(jnp.where(sel1, rank, 0.0), axis=-1, keepdims=True)
    r2 = jnp.sum(jnp.where(sel2, rank, 0.0), axis=-1, keepdims=True)
    base_ref[...] += jnp.sum(sel, axis=0, keepdims=True)
    cnt_ref[...] = base_ref[...]
    fields = (i1.astype(F32), i2.astype(F32), r1, r2, g1, g2)
    route = jnp.zeros(logits.shape, F32)
    for k, v in enumerate(fields):
        route = jnp.where(lane == k, v, route)
    route_ref[...] = route
    route_t_ref[...] = route.T[:ROUTE_ROWS, :]


def moe_router(x2, parts, g, mod, w_out, w_router, tiles_per_batch, tm):
    m, d = x2.shape
    return pl.pallas_call(
        _router_kernel,
        out_shape=(jax.ShapeDtypeStruct((m, d), F32), jax.ShapeDtypeStruct((m, d), BF16),
                   jax.ShapeDtypeStruct((m, LANES), F32), jax.ShapeDtypeStruct((ROUTE_ROWS, m), F32),
                   jax.ShapeDtypeStruct((1, LANES), F32)),
        grid=(m // tm,),
        in_specs=[pl.BlockSpec((tm, d), lambda i: (i, 0)),
                  pl.BlockSpec((1, d), lambda i: (0, 0)),
                  pl.BlockSpec((1, 6, d), _mod_map(tiles_per_batch)),
                  _resident(w_out.shape, lambda i: (0, 0)),
                  pl.BlockSpec((d, LANES), lambda i: (0, 0))] + _part_specs(parts, tm),
        out_specs=(pl.BlockSpec((tm, d), lambda i: (i, 0)),
                   pl.BlockSpec((tm, d), lambda i: (i, 0)),
                   pl.BlockSpec((tm, LANES), lambda i: (i, 0)),
                   pl.BlockSpec((ROUTE_ROWS, tm), lambda i: (0, i)),
                   pl.BlockSpec((1, LANES), lambda i: (0, 0))),
        scratch_shapes=[pltpu.VMEM((1, LANES), F32)],
        compiler_params=_params("arbitrary"),
        name="moe_router",
    )(x2, g, mod, w_out, w_router, *parts)


def _moe_kernel(tile_ref, exp_ref, start_ref, end_ref, xg_ref, wgu_ref, wd_ref, y_ref, wgu_b, wd_b):
    w = pl.program_id(0)
    tm = xg_ref.shape[0]
    start, end = start_ref[w], end_ref[w]
    t0 = tile_ref[w] * tm

    @pl.when((w == 0) | (exp_ref[w] != exp_ref[jnp.maximum(w - 1, 0)]))
    def _():
        wgu_b[...] = wgu_ref[0].astype(BF16)
        wd_b[...] = wd_ref[0].astype(BF16)

    @pl.when(end > start)
    def _():
        y = _swiglu(xg_ref[...], lambda a, b: wgu_b[:, a:b], lambda a, b: wd_b[a:b, :], wd_b.shape[0])
        y = y.astype(y_ref.dtype)

        @pl.when(start == t0)
        def _():
            y_ref[...] = y

        @pl.when(start != t0)
        def _():
            row = t0 + lax.broadcasted_iota(jnp.int32, (tm, 1), 0)
            y_ref[...] = jnp.where(row >= start, y, y_ref[...])


def moe_grouped(item_tile, item_expert, item_start, item_end, xg, w_gu, w_down, tm):
    p, d = xg.shape
    grid_spec = pltpu.PrefetchScalarGridSpec(
        num_scalar_prefetch=4,
        grid=(item_tile.shape[0],),
        in_specs=[pl.BlockSpec((tm, d), lambda w, it, ie, s, e: (it[w], 0)),
                  pl.BlockSpec((1,) + w_gu.shape[1:], lambda w, it, ie, s, e: (ie[w], 0, 0)),
                  pl.BlockSpec((1,) + w_down.shape[1:], lambda w, it, ie, s, e: (ie[w], 0, 0))],
        out_specs=pl.BlockSpec((tm, d), lambda w, it, ie, s, e: (it[w], 0)),
        scratch_shapes=[pltpu.VMEM(w_gu.shape[1:], BF16), pltpu.VMEM(w_down.shape[1:], BF16)],
    )
    return pl.pallas_call(
        _moe_kernel,
        out_shape=jax.ShapeDtypeStruct((p, d), BF16),
        grid_spec=grid_spec,
        compiler_params=_params("arbitrary"),
        name="moe_grouped",
    )(item_tile, item_expert, item_start, item_end, xg, w_gu, w_down)


def _moe_combine_kernel(x_ref, ya_ref, yb_ref, route_ref, mod_ref, *refs):
    o_ref = refs[-1]
    r = route_ref[...]
    y = (r[:, ROUTE_G1:ROUTE_G1 + 1] * ya_ref[...].astype(F32) + r[:, ROUTE_G2:ROUTE_G2 + 1] * yb_ref[...].astype(F32))
    x = x_ref[...] + mod_ref[0, 5:6, :] * y
    if len(refs) == 2:
        ms = jnp.mean(x * x, axis=-1, keepdims=True)
        x = x * lax.rsqrt(ms + EPS) * refs[0][...]
    o_ref[...] = x


def moe_combine(x2, ya, yb, route, mod, tiles_per_batch, tm, g_final=None):
    m, d = x2.shape
    row = pl.BlockSpec((tm, d), lambda i: (i, 0))
    specs = [row, row, row, pl.BlockSpec((tm, LANES), lambda i: (i, 0)),
             pl.BlockSpec((1, 6, d), _mod_map(tiles_per_batch))]
    args = [x2, ya, yb, route, mod]
    if g_final is not None:
        specs.append(pl.BlockSpec((1, d), lambda i: (0, 0)))
        args.append(g_final)
    return pl.pallas_call(
        _moe_combine_kernel,
        out_shape=jax.ShapeDtypeStruct((m, d), F32),
        grid=(m // tm,),
        in_specs=specs,
        out_specs=row,
        compiler_params=_params("parallel"),
        name="moe_combine",
    )(*args)


def moe_block(x2, parts, g, mod, w_out, w_router, w_gu, w_down, tiles_per_batch, tm_tok, tm_grp, g_final=None):
    m, d = x2.shape
    x2, h, route, route_t, cnt = moe_router(x2, parts, g, mod, w_out, w_router, tiles_per_batch, tm_tok)
    cnt = cnt[0, :N_EXPERTS].astype(jnp.int32)
    p = m * TOP_K
    off = jnp.cumsum(cnt) - cnt
    field = lambda k: route_t[k].astype(jnp.int32)

    def row_of(e, r):
        for k in range(N_EXPERTS):
            r = r + jnp.where(e == k, off[k], 0)
        return r

    d1 = row_of(field(ROUTE_E1), field(ROUTE_R1))
    d2 = row_of(field(ROUTE_E2), field(ROUTE_R2))
    tok = jnp.arange(m, dtype=jnp.int32)
    _, src = lax.sort_key_val(jnp.concatenate([d1, d2]), jnp.concatenate([tok, tok]))
    n_row_tiles = p // tm_grp
    start = jnp.sort(jnp.concatenate([jnp.arange(n_row_tiles, dtype=jnp.int32) * tm_grp, off]))
    end = jnp.concatenate([start[1:], jnp.full((1,), p, jnp.int32)])
    item_tile = jnp.minimum(start // tm_grp, n_row_tiles - 1)
    item_expert = jnp.sum((off[None, :] <= start[:, None]).astype(jnp.int32), axis=1) - 1
    rows = lambda a, idx: a.at[idx].get(mode="promise_in_bounds")
    yg = moe_grouped(item_tile, item_expert, start, end, rows(h, src), w_gu, w_down, tm_grp)
    return moe_combine(x2, rows(yg, d1), rows(yg, d2), route, mod, tiles_per_batch, tm_tok, g_final)


def _final_norm_kernel(x_ref, g_ref, o_ref):
    x = x_ref[...]
    ms = jnp.mean(x * x, axis=-1, keepdims=True)
    o_ref[...] = x * lax.rsqrt(ms + EPS) * g_ref[...]


def final_norm(x2, g, tm):
    m, d = x2.shape
    return pl.pallas_call(
        _final_norm_kernel,
        out_shape=jax.ShapeDtypeStruct((m, d), F32),
        grid=(m // tm,),
        in_specs=[pl.BlockSpec((tm, d), lambda i: (i, 0)), pl.BlockSpec((1, d), lambda i: (0, 0))],
        out_specs=pl.BlockSpec((tm, d), lambda i: (i, 0)),
        compiler_params=_params("parallel"),
        name="final_norm",
    )(x2, g)


NA_QROWS = 4
NA_KROWS = NA_QROWS + WIN_R
NA_STEP_BLOCKS = 8
Q_COL, K_COL, V_COL = 4, 5, 6


def _na_key_start(j, rows):
    return np.clip(j * NA_QROWS - WIN_R // 2, 0, rows - NA_KROWS)


def _na_bias_index(rows):
    nblk = rows // NA_QROWS
    pats = []
    for j in range(nblk):
        start = _na_key_start(j, rows)
        r = j * NA_QROWS + np.arange(NA_QROWS)
        sr = np.clip(r - WIN_R // 2, 0, rows - WIN_R)
        kr = start + np.arange(NA_KROWS)
        rvalid = (kr[None, :] >= sr[:, None]) & (kr[None, :] < sr[:, None] + WIN_R)
        ri = np.clip(kr[None, :] - r[:, None] + WIN_R - 1, 0, 2 * WIN_R - 2)
        pats.append((ri, rvalid))
    for j in range(2, nblk - 1):
        assert all(np.array_equal(a, b) for a, b in zip(pats[1], pats[j]))
    sel = [pats[0], pats[1], pats[nblk - 1]]
    ri = np.stack([p[0] for p in sel])
    rvalid = np.stack([p[1] for p in sel])
    c = np.arange(GRID_W)
    sc = np.clip(c - WIN_C // 2, 0, GRID_W - WIN_C)
    cvalid = (c[None, :] >= sc[:, None]) & (c[None, :] < sc[:, None] + WIN_C)
    ci = np.clip(c[None, :] - c[:, None] + WIN_C - 1, 0, 2 * WIN_C - 2)
    c_onehot = (ci[..., None] == np.arange(2 * WIN_C - 1)).astype(np.float32)
    valid = rvalid[:, :, None, :, None] & cvalid[None, None, :, None, :]
    return ri, c_onehot, valid


def na_bias_table(rpb, rows):
    ri, c_onehot, valid = _na_bias_index(rows)
    toep = jnp.einsum('hrd,qkd->hrqk', rpb, c_onehot, precision=lax.Precision.HIGHEST)
    b = jnp.take(toep, ri.reshape(-1), axis=1).reshape((NA_HEADS,) + ri.shape + (GRID_W, GRID_W))
    b = jnp.transpose(b, (1, 0, 2, 4, 3, 5))
    b = jnp.where(valid[:, None], b, -jnp.inf)
    return b.reshape(3, NA_HEADS, NA_QROWS * GRID_W, NA_KROWS * GRID_W).astype(F32)


def _attend_heads(q, key_sets, bias_fn):
    n, w = q.shape
    lane = lax.broadcasted_iota(jnp.int32, (1, w), 1)
    head_masks = [(lane >= h * NA_HEAD_DIM) & (lane < (h + 1) * NA_HEAD_DIM) for h in range(NA_HEADS)]
    all_scores = []
    for h, mh in enumerate(head_masks):
        qh = jnp.where(mh, q, 0.0).astype(BF16)
        scores = []
        for i, (k, _) in enumerate(key_sets):
            s = lax.dot_general(qh, k, (((1,), (1,)), ((), ())), preferred_element_type=F32)
            b = bias_fn(i, h)
            scores.append(s if b is None else s + b)
        all_scores.append(scores)
    all_probs = []
    for scores in all_scores:
        m = scores[0].max(axis=-1, keepdims=True)
        for s in scores[1:]:
            m = jnp.maximum(m, s.max(axis=-1, keepdims=True))
        denom = jnp.zeros((n, 1), F32)
        probs = []
        for s in scores:
            p = jnp.exp(s - m)
            denom = denom + p.sum(axis=-1, keepdims=True)
            probs.append(p.astype(BF16))
        all_probs.append((probs, denom))
    out = jnp.zeros((n, w), F32)
    for mh, (probs, denom) in zip(head_masks, all_probs):
        acc = jnp.zeros((n, w), F32)
        for p, (_, v) in zip(probs, key_sets):
            acc = acc + jnp.dot(p, v, preferred_element_type=F32)
        out = out + jnp.where(mh, acc / denom, 0.0)
    return out


def _na_kernel(q_ref, k_ref, v_ref, kc_ref, vc_ref, bias_ref, o_ref, *, rows):
    nblk = rows // NA_QROWS
    nq, nk = NA_QROWS * GRID_W, NA_KROWS * GRID_W
    for s in range(NA_STEP_BLOCKS):
        j = pl.program_id(1) * NA_STEP_BLOCKS + s
        start = jnp.clip(j * NA_QROWS - WIN_R // 2, 0, rows - NA_KROWS)
        t0 = pl.multiple_of(start * GRID_W, GRID_W)
        pattern = jnp.where(j == 0, 0, jnp.where(j == nblk - 1, 2, 1))
        kw = k_ref[pl.ds(t0, nk), :]
        vw = v_ref[pl.ds(t0, nk), :]
        q = q_ref[s * nq:(s + 1) * nq, :] * (1.0 / math.sqrt(NA_HEAD_DIM))
        o = _attend_heads(q, [(kw, vw), (kc_ref[...], vc_ref[...])],
                          lambda i, h: bias_ref[pattern, h] if i == 0 else None)
        o_ref[s * nq:(s + 1) * nq, :] = o.astype(o_ref.dtype)


def na_attention(pr, pr_c, bias, bn):
    S, Lc = pr.shape[0] // bn, pr_c.shape[0] // bn
    rows = S // GRID_W
    nsteps = rows // NA_QROWS // NA_STEP_BLOCKS
    nq = NA_STEP_BLOCKS * NA_QROWS * GRID_W
    return pl.pallas_call(
        functools.partial(_na_kernel, rows=rows),
        out_shape=jax.ShapeDtypeStruct((bn * S, W_NA), BF16),
        grid=(bn, nsteps),
        in_specs=[pl.BlockSpec((nq, W_NA), lambda b, j: (b * nsteps + j, Q_COL)),
                  pl.BlockSpec((S, W_NA), lambda b, j: (b, K_COL)),
                  pl.BlockSpec((S, W_NA), lambda b, j: (b, V_COL)),
                  pl.BlockSpec((Lc, W_NA), lambda b, j: (b, K_COL)),
                  pl.BlockSpec((Lc, W_NA), lambda b, j: (b, V_COL)),
                  _resident(bias.shape, lambda b, j: (0, 0, 0, 0))],
        out_specs=pl.BlockSpec((nq, W_NA), lambda b, j: (b * nsteps + j, 0)),
        compiler_params=_params("parallel", "arbitrary"),
        name="na_attention",
    )(pr, pr, pr, pr_c, pr_c, bias)


def _ctx_attn_kernel(q_ref, k_ref, v_ref, o_ref):
    q = q_ref[...] * (1.0 / math.sqrt(NA_HEAD_DIM))
    o = _attend_heads(q, [(k_ref[...], v_ref[...])], lambda i, h: None)
    o_ref[...] = o.astype(o_ref.dtype)


def ctx_attention(pr_c, bn):
    Lc = pr_c.shape[0] // bn
    return pl.pallas_call(
        _ctx_attn_kernel,
        out_shape=jax.ShapeDtypeStruct((bn * Lc, W_NA), BF16),
        grid=(bn,),
        in_specs=[pl.BlockSpec((Lc, W_NA), lambda b: (b, Q_COL)),
                  pl.BlockSpec((Lc, W_NA), lambda b: (b, K_COL)),
                  pl.BlockSpec((Lc, W_NA), lambda b: (b, V_COL))],
        out_specs=pl.BlockSpec((Lc, W_NA), lambda b: (b, 0)),
        compiler_params=_params("parallel"),
        name="ctx_attention",
    )(pr_c, pr_c, pr_c)


POOL_COL, CONV_H_COL, CONV_B_COL, CONV_C_COL = 0, 1, 2, 3


def _shift_rows(x, k, row):
    n = x.shape[0]
    y = pltpu.roll(x, k % n, 0)
    return jnp.where(row < k, 0.0, y) if k > 0 else jnp.where(row >= n + k, 0.0, y)


def _local_mix_kernel(u_ref, h_ref, bg_ref, cg_ref, pw_ref, ps_ref, cw_ref, op_ref, oc_ref, *, seq):
    row = lax.broadcasted_iota(jnp.int32, (seq, 1), 0)
    lane = lax.broadcasted_iota(jnp.int32, (1, W_POOL), 1)
    u = u_ref[...].astype(F32)
    trailing, leading = {1: u}, {1: u}
    for w in (1, 2, 4):
        trailing[2 * w] = trailing[w] + _shift_rows(trailing[w], w, row)
        leading[2 * w] = leading[w] + _shift_rows(leading[w], -w, row)
    rowf = row.astype(F32)
    win_sum = jnp.zeros_like(u)
    cnt = jnp.zeros_like(u)
    for g, win in enumerate(POOL_WINDOWS):
        half = win // 2
        mg = (lane >= g * POOL_GROUP) & (lane < (g + 1) * POOL_GROUP)
        s = _shift_rows(trailing[half], 1, row) + leading[half]
        n = jnp.minimum(rowf + half, float(seq)) - jnp.maximum(rowf - half, 0.0)
        win_sum = jnp.where(mg, s, win_sum)
        cnt = jnp.where(mg, n, cnt)
    p = win_sum / cnt - u
    pooled = jnp.dot(p.astype(BF16), pw_ref[...], preferred_element_type=F32) * ps_ref[...]
    op_ref[...] = pooled.astype(op_ref.dtype)
    v = cg_ref[...].astype(F32) * h_ref[...].astype(F32)
    conv = (_shift_rows(v, 1, row) * cw_ref[0:1, :] + v * cw_ref[1:2, :] + _shift_rows(v, -1, row) * cw_ref[2:3, :])
    oc_ref[...] = (bg_ref[...].astype(F32) * conv).astype(oc_ref.dtype)


def local_mix(pr, pool_w, pool_scale, conv_w, bn):
    seq = pr.shape[0] // bn
    pw = jax.scipy.linalg.block_diag(*[pool_w[g] for g in range(len(POOL_WINDOWS))]).astype(BF16)
    blk = lambda col: pl.BlockSpec((seq, W_POOL), lambda b: (b, col))
    full = lambda a: pl.BlockSpec(a.shape, lambda b: (0,) * a.ndim)
    args = (pw, pool_scale[None, :], conv_w)
    return pl.pallas_call(
        functools.partial(_local_mix_kernel, seq=seq),
        out_shape=(jax.ShapeDtypeStruct((bn * seq, W_POOL), BF16), jax.ShapeDtypeStruct((bn * seq, W_CONV), BF16)),
        grid=(bn,),
        in_specs=[blk(POOL_COL), blk(CONV_H_COL), blk(CONV_B_COL), blk(CONV_C_COL)] + [full(a) for a in args],
        out_specs=(pl.BlockSpec((seq, W_POOL), lambda b: (b, 0)),
                   pl.BlockSpec((seq, W_CONV), lambda b: (b, 0))),
        compiler_params=_params("parallel"),
        name="local_mix",
    )(pr, pr, pr, pr, *args)


Z_COL, XS_COL, BS_COL, CS_COL = 7, 8, 9, 10
DT_COL = 22
HEADS_PER_GROUP = SSD_HEADS // SSD_GROUPS
GROUP_W = HEADS_PER_GROUP * SSD_HEAD_DIM


def _head_mask(h):
    lane = lax.broadcasted_iota(jnp.int32, (1, W_SSD), 1)
    return (lane >= h * SSD_HEAD_DIM) & (lane < (h + 1) * SSD_HEAD_DIM)


def _expand_heads(v, d):
    out = jnp.zeros((v.shape[0], W_SSD), F32)
    for h in range(SSD_HEADS):
        k = d * SSD_HEADS + h
        out = jnp.where(_head_mask(h), v[:, k:k + 1], out)
    return out


def _ssd_chunk_terms(c, xs_ref, b_ref, c_ref, dt_ref, la_ref, dsk_ref, y_ref, upd_ref, dec_ref, need_y):
    T = SSD_CHUNK
    r0 = pl.multiple_of(c * T, T)
    xs = xs_ref[pl.ds(r0, T), :]
    bm = b_ref[pl.ds(r0, T), :]
    cm = c_ref[pl.ds(r0, T), :]
    dt = dt_ref[pl.ds(r0, T), :]
    la = la_ref[pl.ds(r0, T), :]
    li = lax.broadcasted_iota(jnp.int32, (T, T), 0)
    si = lax.broadcasted_iota(jnp.int32, (T, T), 1)
    causal = si <= li
    prefix = jnp.dot(causal.astype(F32), la, precision=lax.Precision.HIGHEST, preferred_element_type=F32)
    total = prefix[T - 1:T, :]
    lane = lax.broadcasted_iota(jnp.int32, (1, LANES), 1)
    acum = jnp.where(lane < SSD_HEADS, prefix, total - prefix + la)
    bt = bm.astype(F32).T.astype(BF16)
    if need_y:
        acum_t = acum.T
        cb = [lax.dot_general(cm[:, g * SSD_STATE:(g + 1) * SSD_STATE], bm[:, g * SSD_STATE:(g + 1) * SSD_STATE],
                              (((1,), (1,)), ((), ())), preferred_element_type=F32) for g in range(SSD_GROUPS)]
        y = jnp.zeros((T, W_SSD), F32)
    per_dir = []
    for d in range(2):
        acum_e = _expand_heads(acum, d)
        tot_e = _expand_heads(total, d)
        xdt = xs * _expand_heads(dt, d)
        xw = (xdt * jnp.exp(tot_e - acum_e)).astype(BF16)
        dec_ref[c, d, 0:T, :] = jnp.exp(acum_e)
        dec_ref[c, d, T:T + 1, :] = jnp.exp(tot_e)
        scores = []
        if need_y:
            mask = causal if d == 0 else (si >= li)
            for h in range(SSD_HEADS):
                k = d * SSD_HEADS + h
                decay = jnp.exp(jnp.where(mask, acum[:, k:k + 1] - acum_t[k:k + 1, :], -jnp.inf))
                scores.append((cb[h // HEADS_PER_GROUP] * decay).astype(BF16))
        per_dir.append((xw, xdt, scores))
    for d, (xw, xdt, scores) in enumerate(per_dir):
        upd = [jnp.dot(bt[g * SSD_STATE:(g + 1) * SSD_STATE, :], xw[:, g * GROUP_W:(g + 1) * GROUP_W],
                       preferred_element_type=F32) for g in range(SSD_GROUPS)]
        upd_ref[c, d] = jnp.concatenate(upd, axis=1)
        if need_y:
            x_heads = [jnp.where(_head_mask(h), xdt, 0.0).astype(BF16) for h in range(SSD_HEADS)]
            y = y + jnp.dot(jnp.concatenate(scores, axis=1), jnp.concatenate(x_heads, axis=0),
                            preferred_element_type=F32)
    if need_y:
        y_ref[pl.ds(r0, T), :] = xs * dsk_ref[...] + y


def _ssd_chunk_state(c, d, c_ref, y_ref, upd_ref, dec_ref, st_ref, need_y):
    T = SSD_CHUNK
    r0 = pl.multiple_of(c * T, T)
    st = st_ref[d]
    if need_y:
        cm = c_ref[pl.ds(r0, T), :]
        st_b = st.astype(BF16)
        ys = [jnp.dot(cm[:, g * SSD_STATE:(g + 1) * SSD_STATE], st_b[:, g * GROUP_W:(g + 1) * GROUP_W],
                      preferred_element_type=F32) for g in range(SSD_GROUPS)]
        y_ref[pl.ds(r0, T), :] += jnp.concatenate(ys, axis=1) * dec_ref[c, d, 0:T, :]
    st_ref[d] = dec_ref[c, d, T:T + 1, :] * st + upd_ref[c, d]


def _ssd_kernel(z_ref, xs_in, bs_in, cs_in, dtr_ref, cw_ref, cb_ref, dtb_ref, alog_ref, dsk_ref, ng_ref, h0_ref,
                *refs, seq, need_y):
    if need_y:
        o_ref, hT_ref, xs_ref, b_ref, c_ref, dt_ref, la_ref, upd_ref, dec_ref, st_ref, y_ref = refs
    else:
        hT_ref, xs_ref, b_ref, c_ref, dt_ref, la_ref, upd_ref, dec_ref, st_ref = refs
        y_ref = None
    nc = seq // SSD_CHUNK
    row = lax.broadcasted_iota(jnp.int32, (seq, 1), 0)
    for gi, (src, dst) in enumerate(((xs_in, xs_ref), (bs_in, b_ref), (cs_in, c_ref))):
        sl = slice(gi * W_SSD, (gi + 1) * W_SSD)
        x = src[...].astype(F32)
        cv = (_shift_rows(x, 1, row) * cw_ref[0:1, sl] + x * cw_ref[1:2, sl]
              + _shift_rows(x, -1, row) * cw_ref[2:3, sl] + cb_ref[:, sl])
        dst[...] = (cv * jax.nn.sigmoid(cv)).astype(dst.dtype)
    dtv = dtr_ref[...] + dtb_ref[...]
    dt = jnp.maximum(dtv, 0.0) + jnp.log1p(jnp.exp(-jnp.abs(dtv)))
    dt_ref[...] = dt
    la_ref[...] = dt * (-jnp.exp(alog_ref[...]))
    st_ref[...] = h0_ref[0]

    def terms(c, carry):
        _ssd_chunk_terms(c, xs_ref, b_ref, c_ref, dt_ref, la_ref, dsk_ref, y_ref, upd_ref, dec_ref, need_y)
        return carry

    lax.fori_loop(0, nc, terms, 0, unroll=min(16, nc))

    def recur(i, carry):
        _ssd_chunk_state(i, 0, c_ref, y_ref, upd_ref, dec_ref, st_ref, need_y)
        _ssd_chunk_state(nc - 1 - i, 1, c_ref, y_ref, upd_ref, dec_ref, st_ref, need_y)
        return carry

    lax.fori_loop(0, nc, recur, 0, unroll=min(16, nc))
    hT_ref[0] = st_ref[...]
    if need_y:
        z = z_ref[...].astype(F32)
        yz = y_ref[...] * (z * jax.nn.sigmoid(z))
        ms = jnp.mean(yz * yz, axis=-1, keepdims=True)
        o_ref[...] = (yz * lax.rsqrt(ms + EPS) * ng_ref[...]).astype(o_ref.dtype)


def ssd_scan(pr, dt, h0, conv_w, conv_b, dt_bias, a_log, d_skip, norm_g, need_y):
    bn = h0.shape[0]
    seq = pr.shape[0] // bn
    nc = seq // SSD_CHUNK
    pad = LANES - 2 * SSD_HEADS
    dtb = jnp.pad(dt_bias.reshape(1, -1), ((0, 0), (0, pad)))
    alog = jnp.pad(a_log.reshape(1, -1), ((0, 0), (0, pad)))
    dsk = jnp.repeat(d_skip, SSD_HEAD_DIM)[None, :]
    blk = lambda col: pl.BlockSpec((seq, W_SSD), lambda b: (b, col))
    full = lambda a: pl.BlockSpec(a.shape, lambda b: (0,) * a.ndim)
    st_shape = (1, 2, SSD_STATE, W_SSD)
    st_spec = pl.BlockSpec(st_shape, lambda b: (b, 0, 0, 0))
    args = (conv_w, conv_b[None, :], dtb, alog, dsk, norm_g[None, :])
    out_shape = [jax.ShapeDtypeStruct((bn,) + st_shape[1:], F32)]
    out_specs = [st_spec]
    scratch = [pltpu.VMEM((seq, W_SSD), F32),
               pltpu.VMEM((seq, W_SSD), BF16), pltpu.VMEM((seq, W_SSD), BF16),
               pltpu.VMEM((seq, LANES), F32), pltpu.VMEM((seq, LANES), F32),
               pltpu.VMEM((nc, 2, SSD_STATE, W_SSD), F32),
               pltpu.VMEM((nc, 2, SSD_CHUNK + 8, W_SSD), F32),
               pltpu.VMEM(st_shape[1:], F32)]
    if need_y:
        out_shape.insert(0, jax.ShapeDtypeStruct((bn * seq, W_SSD), BF16))
        out_specs.insert(0, pl.BlockSpec((seq, W_SSD), lambda b: (b, 0)))
        scratch.append(pltpu.VMEM((seq, W_SSD), F32))
    res = pl.pallas_call(
        functools.partial(_ssd_kernel, seq=seq, need_y=need_y),
        out_shape=out_shape,
        grid=(bn,),
        in_specs=[blk(Z_COL), blk(XS_COL), blk(BS_COL), blk(CS_COL),
                  pl.BlockSpec((seq, LANES), lambda b: (b, 0))]
                 + [full(a) for a in args] + [st_spec],
        out_specs=out_specs,
        scratch_shapes=scratch,
        compiler_params=_params("parallel"),
        name="ssd_scan",
    )(pr, pr, pr, pr, dt, *args, h0)
    return (res[0], res[1]) if need_y else (None, res[0])


def ssd_mix(pr, dt, pr_c, dt_c, conv_w, conv_b, dt_bias, a_log, d_skip, norm_g, ctx_out, bn):
    h0 = jnp.zeros((bn, 2, SSD_STATE, W_SSD), F32)
    oc, hc = ssd_scan(pr_c, dt_c, h0, conv_w, conv_b, dt_bias, a_log, d_skip, norm_g, ctx_out)
    o, _ = ssd_scan(pr, dt, hc, conv_w, conv_b, dt_bias, a_log, d_skip, norm_g, True)
    return o, oc


def mixer(pr, dt, pr_c, dt_c, pool_w, pool_scale, conv_w, na_bias, s_conv_w, s_conv_b, dt_bias, a_log, d_skip,
          s_norm_g, ctx_out, bn):
    o_ssd, oc_ssd = ssd_mix(pr, dt, pr_c, dt_c, s_conv_w, s_conv_b, dt_bias, a_log, d_skip, s_norm_g, ctx_out, bn)
    o = [*local_mix(pr, pool_w, pool_scale, conv_w, bn), na_attention(pr, pr_c, na_bias, bn), o_ssd]
    if not ctx_out:
        return o, None
    oc = [*local_mix(pr_c, pool_w, pool_scale, conv_w, bn), ctx_attention(pr_c, bn), oc_ssd]
    return o, oc


TM = 512
TM_ROUTE = 1024
TM_PROJ = 1024
TM_GRP = 512


def kernel(x, c, ctx, c_ctx, w_ada, b_ada, g_mix, g_ffn, w_in, w_out, pool_w, pool_scale, conv_w, na_rpb,
           ssd_conv_w, ssd_conv_b, ssd_dt_bias, ssd_a_log, ssd_d, ssd_norm_g, ffn_w_gu, ffn_w_down,
           moe_router, moe_w_gu, moe_w_down, g_final):
    bn, S, d = x.shape
    Lc = ctx.shape[1]
    m, mc = bn * S, bn * Lc
    tpb = S // TM
    x2 = x.reshape(m, d)
    xc2 = ctx.reshape(mc, d)
    c_rows = jnp.concatenate([c, c_ctx[None, :], jnp.zeros((7, d), F32)], axis=0)
    for l in range(DEPTH):
        ctx_out = l < DEPTH - 1
        last = l == DEPTH - 1
        ada = ada_modulation(c_rows, w_ada[l].astype(BF16), b_ada[l][None, :])
        mod = ada[:bn].reshape(bn, 6, d)
        mod_c = ada[bn:bn + 1].reshape(1, 6, d)
        w_in_l = jnp.pad(w_in[l], ((0, 0), (0, D_IN_PAD - D_IN_PROJ))).astype(BF16)
        g_m = g_mix[l][None, :]
        g_f = g_ffn[l][None, :]
        pr, dt = in_proj(x2, g_m, mod, w_in_l, S // TM_PROJ, TM_PROJ)
        pr_c, dt_c = in_proj(xc2, g_m, mod_c, w_in_l, None, min(TM_PROJ, mc))
        na_bias = na_bias_table(na_rpb[l], S // GRID_W)
        o, oc = mixer(pr, dt, pr_c, dt_c, pool_w[l], pool_scale[l], conv_w[l], na_bias, ssd_conv_w[l],
                      ssd_conv_b[l], ssd_dt_bias[l], ssd_a_log[l], ssd_d[l], ssd_norm_g[l], ctx_out, bn)
        w_out_l = w_out[l].astype(BF16)
        j = l // 2
        if l % 2 == 0:
            w_gu = ffn_w_gu[j].astype(BF16)
            w_dn = ffn_w_down[j].astype(BF16)
            x2 = ffn_dense(x2, o, g_f, mod, w_out_l, w_gu, w_dn, tpb, TM)
            if ctx_out:
                xc2 = ffn_dense(xc2, oc, g_f, mod_c, w_out_l, w_gu, w_dn, None, min(TM, mc))
        else:
            w_r = jnp.pad(moe_router[j], ((0, 0), (0, LANES - N_EXPERTS))).astype(BF16)
            w_gu, w_dn = moe_w_gu[j], moe_w_down[j]
            x2 = moe_block(x2, o, g_f, mod, w_out_l, w_r, w_gu, w_dn, S // TM_ROUTE, TM_ROUTE, TM_GRP,
                           g_final[None, :] if last else None)
            if ctx_out:
                xc2 = moe_block(xc2, oc, g_f, mod_c, w_out_l, w_r, w_gu, w_dn, None, min(TM_ROUTE, mc), TM_GRP)
            if last:
                return x2.reshape(bn, S, d)
    return final_norm(x2, g_final[None, :], TM).reshape(bn, S, d)
```

```python
import functools
import math

import numpy as np
import jax
import jax.numpy as jnp
from jax import lax
from jax.experimental import pallas as pl
from jax.experimental.pallas import tpu as pltpu

DEPTH = 2
GRID_W = 64
EPS = 1e-6
W_POOL = 256
W_CONV = 256
W_NA = 256
W_SSD = 256
POOL_WINDOWS = (2, 4, 8, 16)
POOL_GROUP = W_POOL // len(POOL_WINDOWS)
NA_HEADS = 4
NA_HEAD_DIM = W_NA // NA_HEADS
WIN_R = 8
WIN_C = 16
SSD_HEAD_DIM = 64
SSD_HEADS = W_SSD // SSD_HEAD_DIM
SSD_GROUPS = 2
SSD_STATE = 128
SSD_CHUNK = 128
SSD_XBC = W_SSD + 2 * SSD_GROUPS * SSD_STATE
D_IN_PROJ = W_POOL + 3 * W_CONV + 3 * W_NA + W_SSD + SSD_XBC + 2 * SSD_HEADS
N_EXPERTS = 8
TOP_K = 2

LANES = 128
D_IN_PAD = -(-D_IN_PROJ // LANES) * LANES
VMEM_LIMIT = 56 * 1024 * 1024
BF16 = jnp.bfloat16
F32 = jnp.float32


def _params(*sem):
    return pltpu.CompilerParams(dimension_semantics=sem, vmem_limit_bytes=VMEM_LIMIT)


def _norm_mod(x, g, scale, shift):
    ms = jnp.mean(x * x, axis=-1, keepdims=True)
    return (x * lax.rsqrt(ms + EPS) * g) * (1.0 + scale) + shift


def _mod_map(tiles_per_batch):
    if tiles_per_batch is None:
        return lambda i, *_: (0, 0, 0)
    return lambda i, *_: (i // tiles_per_batch, 0, 0)


def _resident(shape, index_map):
    return pl.BlockSpec(shape, index_map, pipeline_mode=pl.Buffered(1))


def _ada_kernel(c_ref, w_ref, b_ref, o_ref):
    c = c_ref[...]
    s = c * jax.nn.sigmoid(c)
    o_ref[...] = jnp.dot(s.astype(BF16), w_ref[...], preferred_element_type=F32) + b_ref[...]


def ada_modulation(c_rows, w, b):
    r, d = c_rows.shape
    n = w.shape[1]
    tn = 1536
    return pl.pallas_call(
        _ada_kernel,
        out_shape=jax.ShapeDtypeStruct((r, n), F32),
        grid=(n // tn,),
        in_specs=[pl.BlockSpec((r, d), lambda j: (0, 0)),
                  pl.BlockSpec((d, tn), lambda j: (0, j)),
                  pl.BlockSpec((1, tn), lambda j: (0, j))],
        out_specs=pl.BlockSpec((r, tn), lambda j: (0, j)),
        compiler_params=_params("arbitrary"),
        name="ada_modulation",
    )(c_rows, w, b)


PROJ_ROWS = 512


def _in_proj_kernel(x_ref, g_ref, mod_ref, w_ref, o_ref, dt_ref):
    tm = x_ref.shape[0]
    parts = [slice(r0, min(r0 + PROJ_ROWS, tm)) for r0 in range(0, tm, PROJ_ROWS)]
    hs = [_norm_mod(x_ref[rs, :], g_ref[...], mod_ref[0, 1:2, :], mod_ref[0, 0:1, :]).astype(BF16) for rs in parts]
    for rs, h in zip(parts, hs):
        pr = jnp.dot(h, w_ref[...], preferred_element_type=F32)
        o_ref[rs, :] = pr.astype(o_ref.dtype)
        dt_ref[rs, :] = pr[:, DT_COL * LANES:(DT_COL + 1) * LANES]


def in_proj(x2, g, mod, w, tiles_per_batch, tm):
    m, d = x2.shape
    n = w.shape[1]
    return pl.pallas_call(
        _in_proj_kernel,
        out_shape=(jax.ShapeDtypeStruct((m, n), BF16), jax.ShapeDtypeStruct((m, LANES), F32)),
        grid=(m // tm,),
        in_specs=[pl.BlockSpec((tm, d), lambda i: (i, 0)),
                  pl.BlockSpec((1, d), lambda i: (0, 0)),
                  pl.BlockSpec((1, 6, d), _mod_map(tiles_per_batch)),
                  _resident((d, n), lambda i: (0, 0))],
        out_specs=(pl.BlockSpec((tm, n), lambda i: (i, 0)), pl.BlockSpec((tm, LANES), lambda i: (i, 0))),
        compiler_params=_params("parallel"),
        name="in_proj",
    )(x2, g, mod, w)


def _mix_residual(x, mod_ref, w_ref, part_refs):
    y, k0 = None, 0
    for p_ref in part_refs:
        k = p_ref.shape[1]
        t = jnp.dot(p_ref[...], w_ref[k0:k0 + k, :], preferred_element_type=F32)
        y = t if y is None else y + t
        k0 += k
    return x + mod_ref[0, 2:3, :] * y


def _part_specs(parts, tm):
    return [pl.BlockSpec((tm, p.shape[1]), lambda i, *_: (i, 0)) for p in parts]


FF_CHUNK = 512


def _swiglu(h, wgu, wd, ff):
    acc = None
    for c0 in range(0, ff, FF_CHUNK):
        c1 = min(c0 + FF_CHUNK, ff)
        gg = jnp.dot(h, wgu(c0, c1), preferred_element_type=F32)
        uu = jnp.dot(h, wgu(ff + c0, ff + c1), preferred_element_type=F32)
        a = (gg * jax.nn.sigmoid(gg) * uu).astype(BF16)
        part = jnp.dot(a, wd(c0, c1), preferred_element_type=F32)
        acc = part if acc is None else acc + part
    return acc


def _ffn_kernel(x_ref, g_ref, mod_ref, wout_ref, wgu_ref, wd_ref, *refs):
    x = _mix_residual(x_ref[...], mod_ref, wout_ref, refs[:-1])
    y_ref = refs[-1]
    h = _norm_mod(x, g_ref[...], mod_ref[0, 4:5, :], mod_ref[0, 3:4, :]).astype(BF16)
    y = _swiglu(h, lambda a, b: wgu_ref[:, a:b], lambda a, b: wd_ref[a:b, :], wd_ref.shape[0])
    y_ref[...] = x + mod_ref[0, 5:6, :] * y


def ffn_dense(x2, parts, g, mod, w_out, w_gu, w_down, tiles_per_batch, tm):
    m, d = x2.shape
    return pl.pallas_call(
        _ffn_kernel,
        out_shape=jax.ShapeDtypeStruct((m, d), F32),
        grid=(m // tm,),
        in_specs=[pl.BlockSpec((tm, d), lambda i: (i, 0)),
                  pl.BlockSpec((1, d), lambda i: (0, 0)),
                  pl.BlockSpec((1, 6, d), _mod_map(tiles_per_batch)),
                  _resident(w_out.shape, lambda i: (0, 0)),
                  _resident(w_gu.shape, lambda i: (0, 0)),
                  _resident(w_down.shape, lambda i: (0, 0))] + _part_specs(parts, tm),
        out_specs=pl.BlockSpec((tm, d), lambda i: (i, 0)),
        compiler_params=_params("parallel"),
        name="ffn_dense",
    )(x2, g, mod, w_out, w_gu, w_down, *parts)


ROUTE_E1, ROUTE_E2, ROUTE_R1, ROUTE_R2, ROUTE_G1, ROUTE_G2 = range(6)
ROUTE_ROWS = 8


def _router_kernel(x_ref, g_ref, mod_ref, wout_ref, wr_ref, *refs):
    part_refs = refs[:-6]
    x1_ref, h_ref, route_ref, route_t_ref, cnt_ref, base_ref = refs[-6:]

    @pl.when(pl.program_id(0) == 0)
    def _():
        base_ref[...] = jnp.zeros_like(base_ref)

    x = _mix_residual(x_ref[...], mod_ref, wout_ref, part_refs)
    x1_ref[...] = x
    h = _norm_mod(x, g_ref[...], mod_ref[0, 4:5, :], mod_ref[0, 3:4, :]).astype(BF16)
    h_ref[...] = h
    tm = h.shape[0]
    logits = jnp.dot(h, wr_ref[...], preferred_element_type=F32)
    lane = lax.broadcasted_iota(jnp.int32, logits.shape, 1)
    neg = jnp.float32(-jnp.inf)
    logits = jnp.where(lane < N_EXPERTS, logits, neg)
    m1 = jnp.max(logits, axis=-1, keepdims=True)
    i1 = jnp.min(jnp.where(logits == m1, lane, LANES), axis=-1, keepdims=True)
    rest = jnp.where(lane == i1, neg, logits)
    m2 = jnp.max(rest, axis=-1, keepdims=True)
    i2 = jnp.min(jnp.where(rest == m2, lane, LANES), axis=-1, keepdims=True)
    e2 = jnp.exp(m2 - m1)
    g1 = 1.0 / (1.0 + e2)
    g2 = e2 / (1.0 + e2)
    sel1, sel2 = lane == i1, lane == i2
    sel = jnp.where(sel1 | sel2, 1.0, 0.0)
    li = lax.broadcasted_iota(jnp.int32, (tm, tm), 0)
    si = lax.broadcasted_iota(jnp.int32, (tm, tm), 1)
    before = jnp.where(si < li, 1.0, 0.0).astype(BF16)
    rank = jnp.dot(before, sel.astype(BF16), preferred_element_type=F32) + base_ref[...]
    r1 = jnp.sum(jnp.where(sel1, rank, 0.0), axis=-1, keepdims=True)
    r2 = jnp.sum(jnp.where(sel2, rank, 0.0), axis=-1, keepdims=True)
    base_ref[...] += jnp.sum(sel, axis=0, keepdims=True)
    cnt_ref[...] = base_ref[...]
    fields = (i1.astype(F32), i2.astype(F32), r1, r2, g1, g2)
    route = jnp.zeros(logits.shape, F32)
    for k, v in enumerate(fields):
        route = jnp.where(lane == k, v, route)
    route_ref[...] = route
    route_t_ref[...] = route.T[:ROUTE_ROWS, :]


def moe_router(x2, parts, g, mod, w_out, w_router, tiles_per_batch, tm):
    m, d = x2.shape
    return pl.pallas_call(
        _router_kernel,
        out_shape=(jax.ShapeDtypeStruct((m, d), F32), jax.ShapeDtypeStruct((m, d), BF16),
                   jax.ShapeDtypeStruct((m, LANES), F32), jax.ShapeDtypeStruct((ROUTE_ROWS, m), F32),
                   jax.ShapeDtypeStruct((1, LANES), F32)),
        grid=(m // tm,),
        in_specs=[pl.BlockSpec((tm, d), lambda i: (i, 0)),
                  pl.BlockSpec((1, d), lambda i: (0, 0)),
                  pl.BlockSpec((1, 6, d), _mod_map(tiles_per_batch)),
                  _resident(w_out.shape, lambda i: (0, 0)),
                  pl.BlockSpec((d, LANES), lambda i: (0, 0))] + _part_specs(parts, tm),
        out_specs=(pl.BlockSpec((tm, d), lambda i: (i, 0)),
                   pl.BlockSpec((tm, d), lambda i: (i, 0)),
                   pl.BlockSpec((tm, LANES), lambda i: (i, 0)),
                   pl.BlockSpec((ROUTE_ROWS, tm), lambda i: (0, i)),
                   pl.BlockSpec((1, LANES), lambda i: (0, 0))),
        scratch_shapes=[pltpu.VMEM((1, LANES), F32)],
        compiler_params=_params("arbitrary"),
        name="moe_router",
    )(x2, g, mod, w_out, w_router, *parts)


def _moe_kernel(tile_ref, exp_ref, start_ref, end_ref, xg_ref, wgu_ref, wd_ref, y_ref, wgu_b, wd_b):
    w = pl.program_id(0)
    tm = xg_ref.shape[0]
    start, end = start_ref[w], end_ref[w]
    t0 = tile_ref[w] * tm

    @pl.when((w == 0) | (exp_ref[w] != exp_ref[jnp.maximum(w - 1, 0)]))
    def _():
        wgu_b[...] = wgu_ref[0].astype(BF16)
        wd_b[...] = wd_ref[0].astype(BF16)

    @pl.when(end > start)
    def _():
        y = _swiglu(xg_ref[...], lambda a, b: wgu_b[:, a:b], lambda a, b: wd_b[a:b, :], wd_b.shape[0])
        y = y.astype(y_ref.dtype)

        @pl.when(start == t0)
        def _():
            y_ref[...] = y

        @pl.when(start != t0)
        def _():
            row = t0 + lax.broadcasted_iota(jnp.int32, (tm, 1), 0)
            y_ref[...] = jnp.where(row >= start, y, y_ref[...])


def moe_grouped(item_tile, item_expert, item_start, item_end, xg, w_gu, w_down, tm):
    p, d = xg.shape
    grid_spec = pltpu.PrefetchScalarGridSpec(
        num_scalar_prefetch=4,
        grid=(item_tile.shape[0],),
        in_specs=[pl.BlockSpec((tm, d), lambda w, it, ie, s, e: (it[w], 0)),
                  pl.BlockSpec((1,) + w_gu.shape[1:], lambda w, it, ie, s, e: (ie[w], 0, 0)),
                  pl.BlockSpec((1,) + w_down.shape[1:], lambda w, it, ie, s, e: (ie[w], 0, 0))],
        out_specs=pl.BlockSpec((tm, d), lambda w, it, ie, s, e: (it[w], 0)),
        scratch_shapes=[pltpu.VMEM(w_gu.shape[1:], BF16), pltpu.VMEM(w_down.shape[1:], BF16)],
    )
    return pl.pallas_call(
        _moe_kernel,
        out_shape=jax.ShapeDtypeStruct((p, d), BF16),
        grid_spec=grid_spec,
        compiler_params=_params("arbitrary"),
        name="moe_grouped",
    )(item_tile, item_expert, item_start, item_end, xg, w_gu, w_down)


def _moe_combine_kernel(x_ref, ya_ref, yb_ref, route_ref, mod_ref, *refs):
    o_ref = refs[-1]
    r = route_ref[...]
    y = (r[:, ROUTE_G1:ROUTE_G1 + 1] * ya_ref[...].astype(F32) + r[:, ROUTE_G2:ROUTE_G2 + 1] * yb_ref[...].astype(F32))
    x = x_ref[...] + mod_ref[0, 5:6, :] * y
    if len(refs) == 2:
        ms = jnp.mean(x * x, axis=-1, keepdims=True)
        x = x * lax.rsqrt(ms + EPS) * refs[0][...]
    o_ref[...] = x


def moe_combine(x2, ya, yb, route, mod, tiles_per_batch, tm, g_final=None):
    m, d = x2.shape
    row = pl.BlockSpec((tm, d), lambda i: (i, 0))
    specs = [row, row, row, pl.BlockSpec((tm, LANES), lambda i: (i, 0)),
             pl.BlockSpec((1, 6, d), _mod_map(tiles_per_batch))]
    args = [x2, ya, yb, route, mod]
    if g_final is not None:
        specs.append(pl.BlockSpec((1, d), lambda i: (0, 0)))
        args.append(g_final)
    return pl.pallas_call(
        _moe_combine_kernel,
        out_shape=jax.ShapeDtypeStruct((m, d), F32),
        grid=(m // tm,),
        in_specs=specs,
        out_specs=row,
        compiler_params=_params("parallel"),
        name="moe_combine",
    )(*args)


def moe_block(x2, parts, g, mod, w_out, w_router, w_gu, w_down, tiles_per_batch, tm_tok, tm_grp, g_final=None):
    m, d = x2.shape
    x2, h, route, route_t, cnt = moe_router(x2, parts, g, mod, w_out, w_router, tiles_per_batch, tm_tok)
    cnt = cnt[0, :N_EXPERTS].astype(jnp.int32)
    p = m * TOP_K
    off = jnp.cumsum(cnt) - cnt
    field = lambda k: route_t[k].astype(jnp.int32)

    def row_of(e, r):
        for k in range(N_EXPERTS):
            r = r + jnp.where(e == k, off[k], 0)
        return r

    d1 = row_of(field(ROUTE_E1), field(ROUTE_R1))
    d2 = row_of(field(ROUTE_E2), field(ROUTE_R2))
    tok = jnp.arange(m, dtype=jnp.int32)
    _, src = lax.sort_key_val(jnp.concatenate([d1, d2]), jnp.concatenate([tok, tok]))
    n_row_tiles = p // tm_grp
    start = jnp.sort(jnp.concatenate([jnp.arange(n_row_tiles, dtype=jnp.int32) * tm_grp, off]))
    end = jnp.concatenate([start[1:], jnp.full((1,), p, jnp.int32)])
    item_tile = jnp.minimum(start // tm_grp, n_row_tiles - 1)
    item_expert = jnp.sum((off[None, :] <= start[:, None]).astype(jnp.int32), axis=1) - 1
    rows = lambda a, idx: a.at[idx].get(mode="promise_in_bounds")
    yg = moe_grouped(item_tile, item_expert, start, end, rows(h, src), w_gu, w_down, tm_grp)
    return moe_combine(x2, rows(yg, d1), rows(yg, d2), route, mod, tiles_per_batch, tm_tok, g_final)


def _final_norm_kernel(x_ref, g_ref, o_ref):
    x = x_ref[...]
    ms = jnp.mean(x * x, axis=-1, keepdims=True)
    o_ref[...] = x * lax.rsqrt(ms + EPS) * g_ref[...]


def final_norm(x2, g, tm):
    m, d = x2.shape
    return pl.pallas_call(
        _final_norm_kernel,
        out_shape=jax.ShapeDtypeStruct((m, d), F32),
        grid=(m // tm,),
        in_specs=[pl.BlockSpec((tm, d), lambda i: (i, 0)), pl.BlockSpec((1, d), lambda i: (0, 0))],
        out_specs=pl.BlockSpec((tm, d), lambda i: (i, 0)),
        compiler_params=_params("parallel"),
        name="final_norm",
    )(x2, g)


NA_QROWS = 4
NA_KROWS = NA_QROWS + WIN_R
NA_STEP_BLOCKS = 8
Q_COL, K_COL, V_COL = 4, 5, 6


def _na_key_start(j, rows):
    return np.clip(j * NA_QROWS - WIN_R // 2, 0, rows - NA_KROWS)


def _na_bias_index(rows):
    nblk = rows // NA_QROWS
    pats = []
    for j in range(nblk):
        start = _na_key_start(j, rows)
        r = j * NA_QROWS + np.arange(NA_QROWS)
        sr = np.clip(r - WIN_R // 2, 0, rows - WIN_R)
        kr = start + np.arange(NA_KROWS)
        rvalid = (kr[None, :] >= sr[:, None]) & (kr[None, :] < sr[:, None] + WIN_R)
        ri = np.clip(kr[None, :] - r[:, None] + WIN_R - 1, 0, 2 * WIN_R - 2)
        pats.append((ri, rvalid))
    for j in range(2, nblk - 1):
        assert all(np.array_equal(a, b) for a, b in zip(pats[1], pats[j]))
    sel = [pats[0], pats[1], pats[nblk - 1]]
    ri = np.stack([p[0] for p in sel])
    rvalid = np.stack([p[1] for p in sel])
    c = np.arange(GRID_W)
    sc = np.clip(c - WIN_C // 2, 0, GRID_W - WIN_C)
    cvalid = (c[None, :] >= sc[:, None]) & (c[None, :] < sc[:, None] + WIN_C)
    ci = np.clip(c[None, :] - c[:, None] + WIN_C - 1, 0, 2 * WIN_C - 2)
    c_onehot = (ci[..., None] == np.arange(2 * WIN_C - 1)).astype(np.float32)
    valid = rvalid[:, :, None, :, None] & cvalid[None, None, :, None, :]
    return ri, c_onehot, valid


def na_bias_table(rpb, rows):
    ri, c_onehot, valid = _na_bias_index(rows)
    toep = jnp.einsum('hrd,qkd->hrqk', rpb, c_onehot, precision=lax.Precision.HIGHEST)
    b = jnp.take(toep, ri.reshape(-1), axis=1).reshape((NA_HEADS,) + ri.shape + (GRID_W, GRID_W))
    b = jnp.transpose(b, (1, 0, 2, 4, 3, 5))
    b = jnp.where(valid[:, None], b, -jnp.inf)
    return b.reshape(3, NA_HEADS, NA_QROWS * GRID_W, NA_KROWS * GRID_W).astype(F32)


def _attend_heads(q, key_sets, bias_fn):
    n, w = q.shape
    lane = lax.broadcasted_iota(jnp.int32, (1, w), 1)
    head_masks = [(lane >= h * NA_HEAD_DIM) & (lane < (h + 1) * NA_HEAD_DIM) for h in range(NA_HEADS)]
    all_scores = []
    for h, mh in enumerate(head_masks):
        qh = jnp.where(mh, q, 0.0).astype(BF16)
        scores = []
        for i, (k, _) in enumerate(key_sets):
            s = lax.dot_general(qh, k, (((1,), (1,)), ((), ())), preferred_element_type=F32)
            b = bias_fn(i, h)
            scores.append(s if b is None else s + b)
        all_scores.append(scores)
    all_probs = []
    for scores in all_scores:
        m = scores[0].max(axis=-1, keepdims=True)
        for s in scores[1:]:
            m = jnp.maximum(m, s.max(axis=-1, keepdims=True))
        denom = jnp.zeros((n, 1), F32)
        probs = []
        for s in scores:
            p = jnp.exp(s - m)
            denom = denom + p.sum(axis=-1, keepdims=True)
            probs.append(p.astype(BF16))
        all_probs.append((probs, denom))
    out = jnp.zeros((n, w), F32)
    for mh, (probs, denom) in zip(head_masks, all_probs):
        acc = jnp.zeros((n, w), F32)
        for p, (_, v) in zip(probs, key_sets):
            acc = acc + jnp.dot(p, v, preferred_element_type=F32)
        out = out + jnp.where(mh, acc / denom, 0.0)
    return out


def _na_kernel(q_ref, k_ref, v_ref, kc_ref, vc_ref, bias_ref, o_ref, *, rows):
    nblk = rows // NA_QROWS
    nq, nk = NA_QROWS * GRID_W, NA_KROWS * GRID_W
    for s in range(NA_STEP_BLOCKS):
        j = pl.program_id(1) * NA_STEP_BLOCKS + s
        start = jnp.clip(j * NA_QROWS - WIN_R // 2, 0, rows - NA_KROWS)
        t0 = pl.multiple_of(start * GRID_W, GRID_W)
        pattern = jnp.where(j == 0, 0, jnp.where(j == nblk - 1, 2, 1))
        kw = k_ref[pl.ds(t0, nk), :]
        vw = v_ref[pl.ds(t0, nk), :]
        q = q_ref[s * nq:(s + 1) * nq, :] * (1.0 / math.sqrt(NA_HEAD_DIM))
        o = _attend_heads(q, [(kw, vw), (kc_ref[...], vc_ref[...])],
                          lambda i, h: bias_ref[pattern, h] if i == 0 else None)
        o_ref[s * nq:(s + 1) * nq, :] = o.astype(o_ref.dtype)


def na_attention(pr, pr_c, bias, bn):
    S, Lc = pr.shape[0] // bn, pr_c.shape[0] // bn
    rows = S // GRID_W
    nsteps = rows // NA_QROWS // NA_STEP_BLOCKS
    nq = NA_STEP_BLOCKS * NA_QROWS * GRID_W
    return pl.pallas_call(
        functools.partial(_na_kernel, rows=rows),
        out_shape=jax.ShapeDtypeStruct((bn * S, W_NA), BF16),
        grid=(bn, nsteps),
        in_specs=[pl.BlockSpec((nq, W_NA), lambda b, j: (b * nsteps + j, Q_COL)),
                  pl.BlockSpec((S, W_NA), lambda b, j: (b, K_COL)),
                  pl.BlockSpec((S, W_NA), lambda b, j: (b, V_COL)),
                  pl.BlockSpec((Lc, W_NA), lambda b, j: (b, K_COL)),
                  pl.BlockSpec((Lc, W_NA), lambda b, j: (b, V_COL)),
                  _resident(bias.shape, lambda b, j: (0, 0, 0, 0))],
        out_specs=pl.BlockSpec((nq, W_NA), lambda b, j: (b * nsteps + j, 0)),
        compiler_params=_params("parallel", "arbitrary"),
        name="na_attention",
    )(pr, pr, pr, pr_c, pr_c, bias)


def _ctx_attn_kernel(q_ref, k_ref, v_ref, o_ref):
    q = q_ref[...] * (1.0 / math.sqrt(NA_HEAD_DIM))
    o = _attend_heads(q, [(k_ref[...], v_ref[...])], lambda i, h: None)
    o_ref[...] = o.astype(o_ref.dtype)


def ctx_attention(pr_c, bn):
    Lc = pr_c.shape[0] // bn
    return pl.pallas_call(
        _ctx_attn_kernel,
        out_shape=jax.ShapeDtypeStruct((bn * Lc, W_NA), BF16),
        grid=(bn,),
        in_specs=[pl.BlockSpec((Lc, W_NA), lambda b: (b, Q_COL)),
                  pl.BlockSpec((Lc, W_NA), lambda b: (b, K_COL)),
                  pl.BlockSpec((Lc, W_NA), lambda b: (b, V_COL))],
        out_specs=pl.BlockSpec((Lc, W_NA), lambda b: (b, 0)),
        compiler_params=_params("parallel"),
        name="ctx_attention",
    )(pr_c, pr_c, pr_c)


POOL_COL, CONV_H_COL, CONV_B_COL, CONV_C_COL = 0, 1, 2, 3


def _shift_rows(x, k, row):
    n = x.shape[0]
    y = pltpu.roll(x, k % n, 0)
    return jnp.where(row < k, 0.0, y) if k > 0 else jnp.where(row >= n + k, 0.0, y)


def _local_mix_kernel(u_ref, h_ref, bg_ref, cg_ref, pw_ref, ps_ref, cw_ref, op_ref, oc_ref, *, seq):
    row = lax.broadcasted_iota(jnp.int32, (seq, 1), 0)
    lane = lax.broadcasted_iota(jnp.int32, (1, W_POOL), 1)
    u = u_ref[...].astype(F32)
    trailing, leading = {1: u}, {1: u}
    for w in (1, 2, 4):
        trailing[2 * w] = trailing[w] + _shift_rows(trailing[w], w, row)
        leading[2 * w] = leading[w] + _shift_rows(leading[w], -w, row)
    rowf = row.astype(F32)
    win_sum = jnp.zeros_like(u)
    cnt = jnp.zeros_like(u)
    for g, win in enumerate(POOL_WINDOWS):
        half = win // 2
        mg = (lane >= g * POOL_GROUP) & (lane < (g + 1) * POOL_GROUP)
        s = _shift_rows(trailing[half], 1, row) + leading[half]
        n = jnp.minimum(rowf + half, float(seq)) - jnp.maximum(rowf - half, 0.0)
        win_sum = jnp.where(mg, s, win_sum)
        cnt = jnp.where(mg, n, cnt)
    p = win_sum / cnt - u
    pooled = jnp.dot(p.astype(BF16), pw_ref[...], preferred_element_type=F32) * ps_ref[...]
    op_ref[...] = pooled.astype(op_ref.dtype)
    v = cg_ref[...].astype(F32) * h_ref[...].astype(F32)
    conv = (_shift_rows(v, 1, row) * cw_ref[0:1, :] + v * cw_ref[1:2, :] + _shift_rows(v, -1, row) * cw_ref[2:3, :])
    oc_ref[...] = (bg_ref[...].astype(F32) * conv).astype(oc_ref.dtype)


def local_mix(pr, pool_w, pool_scale, conv_w, bn):
    seq = pr.shape[0] // bn
    pw = jax.scipy.linalg.block_diag(*[pool_w[g] for g in range(len(POOL_WINDOWS))]).astype(BF16)
    blk = lambda col: pl.BlockSpec((seq, W_POOL), lambda b: (b, col))
    full = lambda a: pl.BlockSpec(a.shape, lambda b: (0,) * a.ndim)
    args = (pw, pool_scale[None, :], conv_w)
    return pl.pallas_call(
        functools.partial(_local_mix_kernel, seq=seq),
        out_shape=(jax.ShapeDtypeStruct((bn * seq, W_POOL), BF16), jax.ShapeDtypeStruct((bn * seq, W_CONV), BF16)),
        grid=(bn,),
        in_specs=[blk(POOL_COL), blk(CONV_H_COL), blk(CONV_B_COL), blk(CONV_C_COL)] + [full(a) for a in args],
        out_specs=(pl.BlockSpec((seq, W_POOL), lambda b: (b, 0)),
                   pl.BlockSpec((seq, W_CONV), lambda b: (b, 0))),
        compiler_params=_params("parallel"),
        name="local_mix",
    )(pr, pr, pr, pr, *args)


Z_COL, XS_COL, BS_COL, CS_COL = 7, 8, 9, 10
DT_COL = 22
HEADS_PER_GROUP = SSD_HEADS // SSD_GROUPS
GROUP_W = HEADS_PER_GROUP * SSD_HEAD_DIM


def _head_mask(h):
    lane = lax.broadcasted_iota(jnp.int32, (1, W_SSD), 1)
    return (lane >= h * SSD_HEAD_DIM) & (lane < (h + 1) * SSD_HEAD_DIM)


def _expand_heads(v, d):
    out = jnp.zeros((v.shape[0], W_SSD), F32)
    for h in range(SSD_HEADS):
        k = d * SSD_HEADS + h
        out = jnp.where(_head_mask(h), v[:, k:k + 1], out)
    return out


def _ssd_chunk_terms(c, xs_ref, b_ref, c_ref, dt_ref, la_ref, dsk_ref, y_ref, upd_ref, dec_ref, need_y):
    T = SSD_CHUNK
    r0 = pl.multiple_of(c * T, T)
    xs = xs_ref[pl.ds(r0, T), :]
    bm = b_ref[pl.ds(r0, T), :]
    cm = c_ref[pl.ds(r0, T), :]
    dt = dt_ref[pl.ds(r0, T), :]
    la = la_ref[pl.ds(r0, T), :]
    li = lax.broadcasted_iota(jnp.int32, (T, T), 0)
    si = lax.broadcasted_iota(jnp.int32, (T, T), 1)
    causal = si <= li
    prefix = jnp.dot(causal.astype(F32), la, precision=lax.Precision.HIGHEST, preferred_element_type=F32)
    total = prefix[T - 1:T, :]
    lane = lax.broadcasted_iota(jnp.int32, (1, LANES), 1)
    acum = jnp.where(lane < SSD_HEADS, prefix, total - prefix + la)
    bt = bm.astype(F32).T.astype(BF16)
    if need_y:
        acum_t = acum.T
        cb = [lax.dot_general(cm[:, g * SSD_STATE:(g + 1) * SSD_STATE], bm[:, g * SSD_STATE:(g + 1) * SSD_STATE],
                              (((1,), (1,)), ((), ())), preferred_element_type=F32) for g in range(SSD_GROUPS)]
        y = jnp.zeros((T, W_SSD), F32)
    per_dir = []
    for d in range(2):
        acum_e = _expand_heads(acum, d)
        tot_e = _expand_heads(total, d)
        xdt = xs * _expand_heads(dt, d)
        xw = (xdt * jnp.exp(tot_e - acum_e)).astype(BF16)
        dec_ref[c, d, 0:T, :] = jnp.exp(acum_e)
        dec_ref[c, d, T:T + 1, :] = jnp.exp(tot_e)
        scores = []
        if need_y:
            mask = causal if d == 0 else (si >= li)
            for h in range(SSD_HEADS):
                k = d * SSD_HEADS + h
                decay = jnp.exp(jnp.where(mask, acum[:, k:k + 1] - acum_t[k:k + 1, :], -jnp.inf))
                scores.append((cb[h // HEADS_PER_GROUP] * decay).astype(BF16))
        per_dir.append((xw, xdt, scores))
    for d, (xw, xdt, scores) in enumerate(per_dir):
        upd = [jnp.dot(bt[g * SSD_STATE:(g + 1) * SSD_STATE, :], xw[:, g * GROUP_W:(g + 1) * GROUP_W],
                       preferred_element_type=F32) for g in range(SSD_GROUPS)]
        upd_ref[c, d] = jnp.concatenate(upd, axis=1)
        if need_y:
            x_heads = [jnp.where(_head_mask(h), xdt, 0.0).astype(BF16) for h in range(SSD_HEADS)]
            y = y + jnp.dot(jnp.concatenate(scores, axis=1), jnp.concatenate(x_heads, axis=0),
                            preferred_element_type=F32)
    if need_y:
        y_ref[pl.ds(r0, T), :] = xs * dsk_ref[...] + y


def _ssd_chunk_state(c, d, c_ref, y_ref, upd_ref, dec_ref, st_ref, need_y):
    T = SSD_CHUNK
    r0 = pl.multiple_of(c * T, T)
    st = st_ref[d]
    if need_y:
        cm = c_ref[pl.ds(r0, T), :]
        st_b = st.astype(BF16)
        ys = [jnp.dot(cm[:, g * SSD_STATE:(g + 1) * SSD_STATE], st_b[:, g * GROUP_W:(g + 1) * GROUP_W],
                      preferred_element_type=F32) for g in range(SSD_GROUPS)]
        y_ref[pl.ds(r0, T), :] += jnp.concatenate(ys, axis=1) * dec_ref[c, d, 0:T, :]
    st_ref[d] = dec_ref[c, d, T:T + 1, :] * st + upd_ref[c, d]


def _ssd_kernel(z_ref, xs_in, bs_in, cs_in, dtr_ref, cw_ref, cb_ref, dtb_ref, alog_ref, dsk_ref, ng_ref, h0_ref,
                *refs, seq, need_y):
    if need_y:
        o_ref, hT_ref, xs_ref, b_ref, c_ref, dt_ref, la_ref, upd_ref, dec_ref, st_ref, y_ref = refs
    else:
        hT_ref, xs_ref, b_ref, c_ref, dt_ref, la_ref, upd_ref, dec_ref, st_ref = refs
        y_ref = None
    nc = seq // SSD_CHUNK
    row = lax.broadcasted_iota(jnp.int32, (seq, 1), 0)
    for gi, (src, dst) in enumerate(((xs_in, xs_ref), (bs_in, b_ref), (cs_in, c_ref))):
        sl = slice(gi * W_SSD, (gi + 1) * W_SSD)
        x = src[...].astype(F32)
        cv = (_shift_rows(x, 1, row) * cw_ref[0:1, sl] + x * cw_ref[1:2, sl]
              + _shift_rows(x, -1, row) * cw_ref[2:3, sl] + cb_ref[:, sl])
        dst[...] = (cv * jax.nn.sigmoid(cv)).astype(dst.dtype)
    dtv = dtr_ref[...] + dtb_ref[...]
    dt = jnp.maximum(dtv, 0.0) + jnp.log1p(jnp.exp(-jnp.abs(dtv)))
    dt_ref[...] = dt
    la_ref[...] = dt * (-jnp.exp(alog_ref[...]))
    st_ref[...] = h0_ref[0]

    def terms(c, carry):
        _ssd_chunk_terms(c, xs_ref, b_ref, c_ref, dt_ref, la_ref, dsk_ref, y_ref, upd_ref, dec_ref, need_y)
        return carry

    lax.fori_loop(0, nc, terms, 0, unroll=min(16, nc))

    def recur(i, carry):
        _ssd_chunk_state(i, 0, c_ref, y_ref, upd_ref, dec_ref, st_ref, need_y)
        _ssd_chunk_state(nc - 1 - i, 1, c_ref, y_ref, upd_ref, dec_ref, st_ref, need_y)
        return carry

    lax.fori_loop(0, nc, recur, 0, unroll=min(16, nc))
    hT_ref[0] = st_ref[...]
    if need_y:
        z = z_ref[...].astype(F32)
        yz = y_ref[...] * (z * jax.nn.sigmoid(z))
        ms = jnp.mean(yz * yz, axis=-1, keepdims=True)
        o_ref[...] = (yz * lax.rsqrt(ms + EPS) * ng_ref[...]).astype(o_ref.dtype)


def ssd_scan(pr, dt, h0, conv_w, conv_b, dt_bias, a_log, d_skip, norm_g, need_y):
    bn = h0.shape[0]
    seq = pr.shape[0] // bn
    nc = seq // SSD_CHUNK
    pad = LANES - 2 * SSD_HEADS
    dtb = jnp.pad(dt_bias.reshape(1, -1), ((0, 0), (0, pad)))
    alog = jnp.pad(a_log.reshape(1, -1), ((0, 0), (0, pad)))
    dsk = jnp.repeat(d_skip, SSD_HEAD_DIM)[None, :]
    blk = lambda col: pl.BlockSpec((seq, W_SSD), lambda b: (b, col))
    full = lambda a: pl.BlockSpec(a.shape, lambda b: (0,) * a.ndim)
    st_shape = (1, 2, SSD_STATE, W_SSD)
    st_spec = pl.BlockSpec(st_shape, lambda b: (b, 0, 0, 0))
    args = (conv_w, conv_b[None, :], dtb, alog, dsk, norm_g[None, :])
    out_shape = [jax.ShapeDtypeStruct((bn,) + st_shape[1:], F32)]
    out_specs = [st_spec]
    scratch = [pltpu.VMEM((seq, W_SSD), F32),
               pltpu.VMEM((seq, W_SSD), BF16), pltpu.VMEM((seq, W_SSD), BF16),
               pltpu.VMEM((seq, LANES), F32), pltpu.VMEM((seq, LANES), F32),
               pltpu.VMEM((nc, 2, SSD_STATE, W_SSD), F32),
               pltpu.VMEM((nc, 2, SSD_CHUNK + 8, W_SSD), F32),
               pltpu.VMEM(st_shape[1:], F32)]
    if need_y:
        out_shape.insert(0, jax.ShapeDtypeStruct((bn * seq, W_SSD), BF16))
        out_specs.insert(0, pl.BlockSpec((seq, W_SSD), lambda b: (b, 0)))
        scratch.append(pltpu.VMEM((seq, W_SSD), F32))
    res = pl.pallas_call(
        functools.partial(_ssd_kernel, seq=seq, need_y=need_y),
        out_shape=out_shape,
        grid=(bn,),
        in_specs=[blk(Z_COL), blk(XS_COL), blk(BS_COL), blk(CS_COL),
                  pl.BlockSpec((seq, LANES), lambda b: (b, 0))]
                 + [full(a) for a in args] + [st_spec],
        out_specs=out_specs,
        scratch_shapes=scratch,
        compiler_params=_params("parallel"),
        name="ssd_scan",
    )(pr, pr, pr, pr, dt, *args, h0)
    return (res[0], res[1]) if need_y else (None, res[0])


def ssd_mix(pr, dt, pr_c, dt_c, conv_w, conv_b, dt_bias, a_log, d_skip, norm_g, ctx_out, bn):
    h0 = jnp.zeros((bn, 2, SSD_STATE, W_SSD), F32)
    oc, hc = ssd_scan(pr_c, dt_c, h0, conv_w, conv_b, dt_bias, a_log, d_skip, norm_g, ctx_out)
    o, _ = ssd_scan(pr, dt, hc, conv_w, conv_b, dt_bias, a_log, d_skip, norm_g, True)
    return o, oc


def mixer(pr, dt, pr_c, dt_c, pool_w, pool_scale, conv_w, na_bias, s_conv_w, s_conv_b, dt_bias, a_log, d_skip,
          s_norm_g, ctx_out, bn):
    o_ssd, oc_ssd = ssd_mix(pr, dt, pr_c, dt_c, s_conv_w, s_conv_b, dt_bias, a_log, d_skip, s_norm_g, ctx_out, bn)
    o = [*local_mix(pr, pool_w, pool_scale, conv_w, bn), na_attention(pr, pr_c, na_bias, bn), o_ssd]
    if not ctx_out:
        return o, None
    oc = [*local_mix(pr_c, pool_w, pool_scale, conv_w, bn), ctx_attention(pr_c, bn), oc_ssd]
    return o, oc


TM = 512
TM_ROUTE = 1024
TM_PROJ = 1024
TM_GRP = 512


def kernel(x, c, ctx, c_ctx, w_ada, b_ada, g_mix, g_ffn, w_in, w_out, pool_w, pool_scale, conv_w, na_rpb,
           ssd_conv_w, ssd_conv_b, ssd_dt_bias, ssd_a_log, ssd_d, ssd_norm_g, ffn_w_gu, ffn_w_down,
           moe_router, moe_w_gu, moe_w_down, g_final):
    bn, S, d = x.shape
    Lc = ctx.shape[1]
    m, mc = bn * S, bn * Lc
    tpb = S // TM
    x2 = x.reshape(m, d)
    xc2 = ctx.reshape(mc, d)
    c_rows = jnp.concatenate([c, c_ctx[None, :], jnp.zeros((7, d), F32)], axis=0)
    for l in range(DEPTH):
        ctx_out = l < DEPTH - 1
        last = l == DEPTH - 1
        ada = ada_modulation(c_rows, w_ada[l].astype(BF16), b_ada[l][None, :])
        mod = ada[:bn].reshape(bn, 6, d)
        mod_c = ada[bn:bn + 1].reshape(1, 6, d)
        w_in_l = jnp.pad(w_in[l], ((0, 0), (0, D_IN_PAD - D_IN_PROJ))).astype(BF16)
        g_m = g_mix[l][None, :]
        g_f = g_ffn[l][None, :]
        pr, dt = in_proj(x2, g_m, mod, w_in_l, S // TM_PROJ, TM_PROJ)
        pr_c, dt_c = in_proj(xc2, g_m, mod_c, w_in_l, None, min(TM_PROJ, mc))
        na_bias = na_bias_table(na_rpb[l], S // GRID_W)
        o, oc = mixer(pr, dt, pr_c, dt_c, pool_w[l], pool_scale[l], conv_w[l], na_bias, ssd_conv_w[l],
                      ssd_conv_b[l], ssd_dt_bias[l], ssd_a_log[l], ssd_d[l], ssd_norm_g[l], ctx_out, bn)
        w_out_l = w_out[l].astype(BF16)
        j = l // 2
        if l % 2 == 0:
            w_gu = ffn_w_gu[j].astype(BF16)
            w_dn = ffn_w_down[j].astype(BF16)
            x2 = ffn_dense(x2, o, g_f, mod, w_out_l, w_gu, w_dn, S // TM_ROUTE, TM_ROUTE)
            if ctx_out:
                xc2 = ffn_dense(xc2, oc, g_f, mod_c, w_out_l, w_gu, w_dn, None, min(TM, mc))
        else:
            w_r = jnp.pad(moe_router[j], ((0, 0), (0, LANES - N_EXPERTS))).astype(BF16)
            w_gu, w_dn = moe_w_gu[j], moe_w_down[j]
            x2 = moe_block(x2, o, g_f, mod, w_out_l, w_r, w_gu, w_dn, S // TM_ROUTE, TM_ROUTE, TM_GRP,
                           g_final[None, :] if last else None)
            if ctx_out:
                xc2 = moe_block(xc2, oc, g_f, mod_c, w_out_l, w_r, w_gu, w_dn, None, min(TM_ROUTE, mc), TM_GRP)
            if last:
                return x2.reshape(bn, S, d)
    return final_norm(x2, g_final[None, :], TM).reshape(bn, S, d)
```

```python
import functools
import math

import numpy as np
import jax
import jax.numpy as jnp
from jax import lax
from jax.experimental import pallas as pl
from jax.experimental.pallas import tpu as pltpu

DEPTH = 2
GRID_W = 64
EPS = 1e-6
W_POOL = 256
W_CONV = 256
W_NA = 256
W_SSD = 256
POOL_WINDOWS = (2, 4, 8, 16)
POOL_GROUP = W_POOL // len(POOL_WINDOWS)
NA_HEADS = 4
NA_HEAD_DIM = W_NA // NA_HEADS
WIN_R = 8
WIN_C = 16
SSD_HEAD_DIM = 64
SSD_HEADS = W_SSD // SSD_HEAD_DIM
SSD_GROUPS = 2
SSD_STATE = 128
SSD_CHUNK = 128
SSD_XBC = W_SSD + 2 * SSD_GROUPS * SSD_STATE
D_IN_PROJ = W_POOL + 3 * W_CONV + 3 * W_NA + W_SSD + SSD_XBC + 2 * SSD_HEADS
N_EXPERTS = 8
TOP_K = 2

LANES = 128
D_IN_PAD = -(-D_IN_PROJ // LANES) * LANES
VMEM_LIMIT = 56 * 1024 * 1024
BF16 = jnp.bfloat16
F32 = jnp.float32


def _params(*sem):
    return pltpu.CompilerParams(dimension_semantics=sem, vmem_limit_bytes=VMEM_LIMIT)


def _norm_mod(x, g, scale, shift):
    ms = jnp.mean(x * x, axis=-1, keepdims=True)
    return (x * lax.rsqrt(ms + EPS) * g) * (1.0 + scale) + shift


def _mod_map(tiles_per_batch):
    if tiles_per_batch is None:
        return lambda i, *_: (0, 0, 0)
    return lambda i, *_: (i // tiles_per_batch, 0, 0)


def _resident(shape, index_map):
    return pl.BlockSpec(shape, index_map, pipeline_mode=pl.Buffered(1))


def _ada_kernel(c_ref, w_ref, b_ref, o_ref):
    c = c_ref[...]
    s = c * jax.nn.sigmoid(c)
    o_ref[...] = jnp.dot(s.astype(BF16), w_ref[...], preferred_element_type=F32) + b_ref[...]


def ada_modulation(c_rows, w, b):
    r, d = c_rows.shape
    n = w.shape[1]
    tn = 1536
    return pl.pallas_call(
        _ada_kernel,
        out_shape=jax.ShapeDtypeStruct((r, n), F32),
        grid=(n // tn,),
        in_specs=[pl.BlockSpec((r, d), lambda j: (0, 0)),
                  pl.BlockSpec((d, tn), lambda j: (0, j)),
                  pl.BlockSpec((1, tn), lambda j: (0, j))],
        out_specs=pl.BlockSpec((r, tn), lambda j: (0, j)),
        compiler_params=_params("arbitrary"),
        name="ada_modulation",
    )(c_rows, w, b)


PROJ_ROWS = 512


def _in_proj_kernel(x_ref, g_ref, mod_ref, w_ref, o_ref, dt_ref):
    tm = x_ref.shape[0]
    parts = [slice(r0, min(r0 + PROJ_ROWS, tm)) for r0 in range(0, tm, PROJ_ROWS)]
    hs = [_norm_mod(x_ref[rs, :], g_ref[...], mod_ref[0, 1:2, :], mod_ref[0, 0:1, :]).astype(BF16) for rs in parts]
    for rs, h in zip(parts, hs):
        pr = jnp.dot(h, w_ref[...], preferred_element_type=F32)
        o_ref[rs, :] = pr.astype(o_ref.dtype)
        dt_ref[rs, :] = pr[:, DT_COL * LANES:(DT_COL + 1) * LANES]


def in_proj(x2, g, mod, w, tiles_per_batch, tm):
    m, d = x2.shape
    n = w.shape[1]
    return pl.pallas_call(
        _in_proj_kernel,
        out_shape=(jax.ShapeDtypeStruct((m, n), BF16), jax.ShapeDtypeStruct((m, LANES), F32)),
        grid=(m // tm,),
        in_specs=[pl.BlockSpec((tm, d), lambda i: (i, 0)),
                  pl.BlockSpec((1, d), lambda i: (0, 0)),
                  pl.BlockSpec((1, 6, d), _mod_map(tiles_per_batch)),
                  _resident((d, n), lambda i: (0, 0))],
        out_specs=(pl.BlockSpec((tm, n), lambda i: (i, 0)), pl.BlockSpec((tm, LANES), lambda i: (i, 0))),
        compiler_params=_params("parallel"),
        name="in_proj",
    )(x2, g, mod, w)


def _mix_residual(x, mod_ref, w_ref, part_refs):
    y, k0 = None, 0
    for p_ref in part_refs:
        k = p_ref.shape[1]
        t = jnp.dot(p_ref[...], w_ref[k0:k0 + k, :], preferred_element_type=F32)
        y = t if y is None else y + t
        k0 += k
    return x + mod_ref[0, 2:3, :] * y


def _part_specs(parts, tm):
    return [pl.BlockSpec((tm, p.shape[1]), lambda i, *_: (i, 0)) for p in parts]


FF_CHUNK = 512


def _swiglu(h, wgu, wd, ff):
    acc = None
    for c0 in range(0, ff, FF_CHUNK):
        c1 = min(c0 + FF_CHUNK, ff)
        gg = jnp.dot(h, wgu(c0, c1), preferred_element_type=F32)
        uu = jnp.dot(h, wgu(ff + c0, ff + c1), preferred_element_type=F32)
        a = (gg * jax.nn.sigmoid(gg) * uu).astype(BF16)
        part = jnp.dot(a, wd(c0, c1), preferred_element_type=F32)
        acc = part if acc is None else acc + part
    return acc


def _ffn_kernel(x_ref, g_ref, mod_ref, wout_ref, wgu_ref, wd_ref, *refs):
    x = _mix_residual(x_ref[...], mod_ref, wout_ref, refs[:-1])
    y_ref = refs[-1]
    h = _norm_mod(x, g_ref[...], mod_ref[0, 4:5, :], mod_ref[0, 3:4, :]).astype(BF16)
    y = _swiglu(h, lambda a, b: wgu_ref[:, a:b], lambda a, b: wd_ref[a:b, :], wd_ref.shape[0])
    y_ref[...] = x + mod_ref[0, 5:6, :] * y


def ffn_dense(x2, parts, g, mod, w_out, w_gu, w_down, tiles_per_batch, tm):
    m, d = x2.shape
    return pl.pallas_call(
        _ffn_kernel,
        out_shape=jax.ShapeDtypeStruct((m, d), F32),
        grid=(m // tm,),
        in_specs=[pl.BlockSpec((tm, d), lambda i: (i, 0)),
                  pl.BlockSpec((1, d), lambda i: (0, 0)),
                  pl.BlockSpec((1, 6, d), _mod_map(tiles_per_batch)),
                  _resident(w_out.shape, lambda i: (0, 0)),
                  _resident(w_gu.shape, lambda i: (0, 0)),
                  _resident(w_down.shape, lambda i: (0, 0))] + _part_specs(parts, tm),
        out_specs=pl.BlockSpec((tm, d), lambda i: (i, 0)),
        compiler_params=_params("parallel"),
        name="ffn_dense",
    )(x2, g, mod, w_out, w_gu, w_down, *parts)


ROUTE_E1, ROUTE_E2, ROUTE_R1, ROUTE_R2, ROUTE_G1, ROUTE_G2 = range(6)
ROUTE_ROWS = 8


def _router_kernel(x_ref, g_ref, mod_ref, wout_ref, wr_ref, *refs):
    part_refs = refs[:-6]
    x1_ref, h_ref, route_ref, route_t_ref, cnt_ref, base_ref = refs[-6:]

    @pl.when(pl.program_id(0) == 0)
    def _():
        base_ref[...] = jnp.zeros_like(base_ref)

    x = _mix_residual(x_ref[...], mod_ref, wout_ref, part_refs)
    x1_ref[...] = x
    h = _norm_mod(x, g_ref[...], mod_ref[0, 4:5, :], mod_ref[0, 3:4, :]).astype(BF16)
    h_ref[...] = h
    tm = h.shape[0]
    logits = jnp.dot(h, wr_ref[...], preferred_element_type=F32)
    lane = lax.broadcasted_iota(jnp.int32, logits.shape, 1)
    neg = jnp.float32(-jnp.inf)
    logits = jnp.where(lane < N_EXPERTS, logits, neg)
    m1 = jnp.max(logits, axis=-1, keepdims=True)
    i1 = jnp.min(jnp.where(logits == m1, lane, LANES), axis=-1, keepdims=True)
    rest = jnp.where(lane == i1, neg, logits)
    m2 = jnp.max(rest, axis=-1, keepdims=True)
    i2 = jnp.min(jnp.where(rest == m2, lane, LANES), axis=-1, keepdims=True)
    e2 = jnp.exp(m2 - m1)
    g1 = 1.0 / (1.0 + e2)
    g2 = e2 / (1.0 + e2)
    sel1, sel2 = lane == i1, lane == i2
    sel = jnp.where(sel1 | sel2, 1.0, 0.0)
    li = lax.broadcasted_iota(jnp.int32, (tm, tm), 0)
    si = lax.broadcasted_iota(jnp.int32, (tm, tm), 1)
    before = jnp.where(si < li, 1.0, 0.0).astype(BF16)
    rank = jnp.dot(before, sel.astype(BF16), preferred_element_type=F32) + base_ref[...]
    r1 = jnp.sum(jnp.where(sel1, rank, 0.0), axis=-1, keepdims=True)
    r2 = jnp.sum(jnp.where(sel2, rank, 0.0), axis=-1, keepdims=True)
    base_ref[...] += jnp.sum(sel, axis=0, keepdims=True)
    cnt_ref[...] = base_ref[...]
    fields = (i1.astype(F32), i2.astype(F32), r1, r2, g1, g2)
    route = jnp.zeros(logits.shape, F32)
    for k, v in enumerate(fields):
        route = jnp.where(lane == k, v, route)
    route_ref[...] = route
    route_t_ref[...] = route.T[:ROUTE_ROWS, :]


def moe_router(x2, parts, g, mod, w_out, w_router, tiles_per_batch, tm):
    m, d = x2.shape
    return pl.pallas_call(
        _router_kernel,
        out_shape=(jax.ShapeDtypeStruct((m, d), F32), jax.ShapeDtypeStruct((m, d), BF16),
                   jax.ShapeDtypeStruct((m, LANES), F32), jax.ShapeDtypeStruct((ROUTE_ROWS, m), F32),
                   jax.ShapeDtypeStruct((1, LANES), F32)),
        grid=(m // tm,),
        in_specs=[pl.BlockSpec((tm, d), lambda i: (i, 0)),
                  pl.BlockSpec((1, d), lambda i: (0, 0)),
                  pl.BlockSpec((1, 6, d), _mod_map(tiles_per_batch)),
                  _resident(w_out.shape, lambda i: (0, 0)),
                  pl.BlockSpec((d, LANES), lambda i: (0, 0))] + _part_specs(parts, tm),
        out_specs=(pl.BlockSpec((tm, d), lambda i: (i, 0)),
                   pl.BlockSpec((tm, d), lambda i: (i, 0)),
                   pl.BlockSpec((tm, LANES), lambda i: (i, 0)),
                   pl.BlockSpec((ROUTE_ROWS, tm), lambda i: (0, i)),
                   pl.BlockSpec((1, LANES), lambda i: (0, 0))),
        scratch_shapes=[pltpu.VMEM((1, LANES), F32)],
        compiler_params=_params("arbitrary"),
        name="moe_router",
    )(x2, g, mod, w_out, w_router, *parts)


def _moe_kernel(tile_ref, exp_ref, start_ref, end_ref, xg_ref, wgu_ref, wd_ref, y_ref, wgu_b, wd_b):
    w = pl.program_id(0)
    tm = xg_ref.shape[0]
    start, end = start_ref[w], end_ref[w]
    t0 = tile_ref[w] * tm

    @pl.when((w == 0) | (exp_ref[w] != exp_ref[jnp.maximum(w - 1, 0)]))
    def _():
        wgu_b[...] = wgu_ref[0].astype(BF16)
        wd_b[...] = wd_ref[0].astype(BF16)

    @pl.when(end > start)
    def _():
        y = _swiglu(xg_ref[...], lambda a, b: wgu_b[:, a:b], lambda a, b: wd_b[a:b, :], wd_b.shape[0])
        y = y.astype(y_ref.dtype)

        @pl.when(start == t0)
        def _():
            y_ref[...] = y

        @pl.when(start != t0)
        def _():
            row = t0 + lax.broadcasted_iota(jnp.int32, (tm, 1), 0)
            y_ref[...] = jnp.where(row >= start, y, y_ref[...])


def moe_grouped(item_tile, item_expert, item_start, item_end, xg, w_gu, w_down, tm):
    p, d = xg.shape
    grid_spec = pltpu.PrefetchScalarGridSpec(
        num_scalar_prefetch=4,
        grid=(item_tile.shape[0],),
        in_specs=[pl.BlockSpec((tm, d), lambda w, it, ie, s, e: (it[w], 0)),
                  pl.BlockSpec((1,) + w_gu.shape[1:], lambda w, it, ie, s, e: (ie[w], 0, 0)),
                  pl.BlockSpec((1,) + w_down.shape[1:], lambda w, it, ie, s, e: (ie[w], 0, 0))],
        out_specs=pl.BlockSpec((tm, d), lambda w, it, ie, s, e: (it[w], 0)),
        scratch_shapes=[pltpu.VMEM(w_gu.shape[1:], BF16), pltpu.VMEM(w_down.shape[1:], BF16)],
    )
    return pl.pallas_call(
        _moe_kernel,
        out_shape=jax.ShapeDtypeStruct((p, d), BF16),
        grid_spec=grid_spec,
        compiler_params=_params("arbitrary"),
        name="moe_grouped",
    )(item_tile, item_expert, item_start, item_end, xg, w_gu, w_down)


def _moe_combine_kernel(x_ref, ya_ref, yb_ref, route_ref, mod_ref, *refs):
    o_ref = refs[-1]
    r = route_ref[...]
    y = (r[:, ROUTE_G1:ROUTE_G1 + 1] * ya_ref[...].astype(F32) + r[:, ROUTE_G2:ROUTE_G2 + 1] * yb_ref[...].astype(F32))
    x = x_ref[...] + mod_ref[0, 5:6, :] * y
    if len(refs) == 2:
        ms = jnp.mean(x * x, axis=-1, keepdims=True)
        x = x * lax.rsqrt(ms + EPS) * refs[0][...]
    o_ref[...] = x


def moe_combine(x2, ya, yb, route, mod, tiles_per_batch, tm, g_final=None):
    m, d = x2.shape
    row = pl.BlockSpec((tm, d), lambda i: (i, 0))
    specs = [row, row, row, pl.BlockSpec((tm, LANES), lambda i: (i, 0)),
             pl.BlockSpec((1, 6, d), _mod_map(tiles_per_batch))]
    args = [x2, ya, yb, route, mod]
    if g_final is not None:
        specs.append(pl.BlockSpec((1, d), lambda i: (0, 0)))
        args.append(g_final)
    return pl.pallas_call(
        _moe_combine_kernel,
        out_shape=jax.ShapeDtypeStruct((m, d), F32),
        grid=(m // tm,),
        in_specs=specs,
        out_specs=row,
        compiler_params=_params("parallel"),
        name="moe_combine",
    )(*args)


def moe_block(x2, parts, g, mod, w_out, w_router, w_gu, w_down, tiles_per_batch, tm_tok, tm_grp, g_final=None):
    m, d = x2.shape
    x2, h, route, route_t, cnt = moe_router(x2, parts, g, mod, w_out, w_router, tiles_per_batch, tm_tok)
    cnt = cnt[0, :N_EXPERTS].astype(jnp.int32)
    p = m * TOP_K
    off = jnp.cumsum(cnt) - cnt
    field = lambda k: route_t[k].astype(jnp.int32)

    def row_of(e, r):
        for k in range(N_EXPERTS):
            r = r + jnp.where(e == k, off[k], 0)
        return r

    d1 = row_of(field(ROUTE_E1), field(ROUTE_R1))
    d2 = row_of(field(ROUTE_E2), field(ROUTE_R2))
    tok = jnp.arange(m, dtype=jnp.int32)
    _, src = lax.sort_key_val(jnp.concatenate([d1, d2]), jnp.concatenate([tok, tok]))
    n_row_tiles = p // tm_grp
    start = jnp.sort(jnp.concatenate([jnp.arange(n_row_tiles, dtype=jnp.int32) * tm_grp, off]))
    end = jnp.concatenate([start[1:], jnp.full((1,), p, jnp.int32)])
    item_tile = jnp.minimum(start // tm_grp, n_row_tiles - 1)
    item_expert = jnp.sum((off[None, :] <= start[:, None]).astype(jnp.int32), axis=1) - 1
    rows = lambda a, idx: a.at[idx].get(mode="promise_in_bounds")
    yg = moe_grouped(item_tile, item_expert, start, end, rows(h, src), w_gu, w_down, tm_grp)
    return moe_combine(x2, rows(yg, d1), rows(yg, d2), route, mod, tiles_per_batch, tm_tok, g_final)


def _final_norm_kernel(x_ref, g_ref, o_ref):
    x = x_ref[...]
    ms = jnp.mean(x * x, axis=-1, keepdims=True)
    o_ref[...] = x * lax.rsqrt(ms + EPS) * g_ref[...]


def final_norm(x2, g, tm):
    m, d = x2.shape
    return pl.pallas_call(
        _final_norm_kernel,
        out_shape=jax.ShapeDtypeStruct((m, d), F32),
        grid=(m // tm,),
        in_specs=[pl.BlockSpec((tm, d), lambda i: (i, 0)), pl.BlockSpec((1, d), lambda i: (0, 0))],
        out_specs=pl.BlockSpec((tm, d), lambda i: (i, 0)),
        compiler_params=_params("parallel"),
        name="final_norm",
    )(x2, g)


NA_QROWS = 4
NA_KROWS = NA_QROWS + WIN_R
NA_STEP_BLOCKS = 8
Q_COL, K_COL, V_COL = 4, 5, 6


def _na_key_start(j, rows):
    return np.clip(j * NA_QROWS - WIN_R // 2, 0, rows - NA_KROWS)


def _na_bias_index(rows):
    nblk = rows // NA_QROWS
    pats = []
    for j in range(nblk):
        start = _na_key_start(j, rows)
        r = j * NA_QROWS + np.arange(NA_QROWS)
        sr = np.clip(r - WIN_R // 2, 0, rows - WIN_R)
        kr = start + np.arange(NA_KROWS)
        rvalid = (kr[None, :] >= sr[:, None]) & (kr[None, :] < sr[:, None] + WIN_R)
        ri = np.clip(kr[None, :] - r[:, None] + WIN_R - 1, 0, 2 * WIN_R - 2)
        pats.append((ri, rvalid))
    for j in range(2, nblk - 1):
        assert all(np.array_equal(a, b) for a, b in zip(pats[1], pats[j]))
    sel = [pats[0], pats[1], pats[nblk - 1]]
    ri = np.stack([p[0] for p in sel])
    rvalid = np.stack([p[1] for p in sel])
    c = np.arange(GRID_W)
    sc = np.clip(c - WIN_C // 2, 0, GRID_W - WIN_C)
    cvalid = (c[None, :] >= sc[:, None]) & (c[None, :] < sc[:, None] + WIN_C)
    ci = np.clip(c[None, :] - c[:, None] + WIN_C - 1, 0, 2 * WIN_C - 2)
    c_onehot = (ci[..., None] == np.arange(2 * WIN_C - 1)).astype(np.float32)
    valid = rvalid[:, :, None, :, None] & cvalid[None, None, :, None, :]
    return ri, c_onehot, valid


def na_bias_table(rpb, rows):
    ri, c_onehot, valid = _na_bias_index(rows)
    toep = jnp.einsum('hrd,qkd->hrqk', rpb, c_onehot, precision=lax.Precision.HIGHEST)
    b = jnp.take(toep, ri.reshape(-1), axis=1).reshape((NA_HEADS,) + ri.shape + (GRID_W, GRID_W))
    b = jnp.transpose(b, (1, 0, 2, 4, 3, 5))
    b = jnp.where(valid[:, None], b, -jnp.inf)
    return b.reshape(3, NA_HEADS, NA_QROWS * GRID_W, NA_KROWS * GRID_W).astype(F32)


def _attend_heads(q, key_sets, bias_fn):
    n, w = q.shape
    lane = lax.broadcasted_iota(jnp.int32, (1, w), 1)
    head_masks = [(lane >= h * NA_HEAD_DIM) & (lane < (h + 1) * NA_HEAD_DIM) for h in range(NA_HEADS)]
    all_scores = []
    for h, mh in enumerate(head_masks):
        qh = jnp.where(mh, q, 0.0).astype(BF16)
        scores = []
        for i, (k, _) in enumerate(key_sets):
            s = lax.dot_general(qh, k, (((1,), (1,)), ((), ())), preferred_element_type=F32)
            b = bias_fn(i, h)
            scores.append(s if b is None else s + b)
        all_scores.append(scores)
    all_probs = []
    for scores in all_scores:
        m = scores[0].max(axis=-1, keepdims=True)
        for s in scores[1:]:
            m = jnp.maximum(m, s.max(axis=-1, keepdims=True))
        denom = jnp.zeros((n, 1), F32)
        probs = []
        for s in scores:
            p = jnp.exp(s - m)
            denom = denom + p.sum(axis=-1, keepdims=True)
            probs.append(p.astype(BF16))
        all_probs.append((probs, denom))
    out = jnp.zeros((n, w), F32)
    for mh, (probs, denom) in zip(head_masks, all_probs):
        acc = jnp.zeros((n, w), F32)
        for p, (_, v) in zip(probs, key_sets):
            acc = acc + jnp.dot(p, v, preferred_element_type=F32)
        out = out + jnp.where(mh, acc * (1.0 / denom), 0.0)
    return out


def _na_kernel(q_ref, k_ref, v_ref, kc_ref, vc_ref, bias_ref, o_ref, *, rows):
    nblk = rows // NA_QROWS
    nq, nk = NA_QROWS * GRID_W, NA_KROWS * GRID_W
    for s in range(NA_STEP_BLOCKS):
        j = pl.program_id(1) * NA_STEP_BLOCKS + s
        start = jnp.clip(j * NA_QROWS - WIN_R // 2, 0, rows - NA_KROWS)
        t0 = pl.multiple_of(start * GRID_W, GRID_W)
        pattern = jnp.where(j == 0, 0, jnp.where(j == nblk - 1, 2, 1))
        kw = k_ref[pl.ds(t0, nk), :]
        vw = v_ref[pl.ds(t0, nk), :]
        q = q_ref[s * nq:(s + 1) * nq, :] * (1.0 / math.sqrt(NA_HEAD_DIM))
        o = _attend_heads(q, [(kw, vw), (kc_ref[...], vc_ref[...])],
                          lambda i, h: bias_ref[pattern, h] if i == 0 else None)
        o_ref[s * nq:(s + 1) * nq, :] = o.astype(o_ref.dtype)


def na_attention(pr, pr_c, bias, bn):
    S, Lc = pr.shape[0] // bn, pr_c.shape[0] // bn
    rows = S // GRID_W
    nsteps = rows // NA_QROWS // NA_STEP_BLOCKS
    nq = NA_STEP_BLOCKS * NA_QROWS * GRID_W
    return pl.pallas_call(
        functools.partial(_na_kernel, rows=rows),
        out_shape=jax.ShapeDtypeStruct((bn * S, W_NA), BF16),
        grid=(bn, nsteps),
        in_specs=[pl.BlockSpec((nq, W_NA), lambda b, j: (b * nsteps + j, Q_COL)),
                  pl.BlockSpec((S, W_NA), lambda b, j: (b, K_COL)),
                  pl.BlockSpec((S, W_NA), lambda b, j: (b, V_COL)),
                  pl.BlockSpec((Lc, W_NA), lambda b, j: (b, K_COL)),
                  pl.BlockSpec((Lc, W_NA), lambda b, j: (b, V_COL)),
                  _resident(bias.shape, lambda b, j: (0, 0, 0, 0))],
        out_specs=pl.BlockSpec((nq, W_NA), lambda b, j: (b * nsteps + j, 0)),
        compiler_params=_params("parallel", "arbitrary"),
        name="na_attention",
    )(pr, pr, pr, pr_c, pr_c, bias)


def _ctx_attn_kernel(q_ref, k_ref, v_ref, o_ref):
    q = q_ref[...] * (1.0 / math.sqrt(NA_HEAD_DIM))
    o = _attend_heads(q, [(k_ref[...], v_ref[...])], lambda i, h: None)
    o_ref[...] = o.astype(o_ref.dtype)


def ctx_attention(pr_c, bn):
    Lc = pr_c.shape[0] // bn
    return pl.pallas_call(
        _ctx_attn_kernel,
        out_shape=jax.ShapeDtypeStruct((bn * Lc, W_NA), BF16),
        grid=(bn,),
        in_specs=[pl.BlockSpec((Lc, W_NA), lambda b: (b, Q_COL)),
                  pl.BlockSpec((Lc, W_NA), lambda b: (b, K_COL)),
                  pl.BlockSpec((Lc, W_NA), lambda b: (b, V_COL))],
        out_specs=pl.BlockSpec((Lc, W_NA), lambda b: (b, 0)),
        compiler_params=_params("parallel"),
        name="ctx_attention",
    )(pr_c, pr_c, pr_c)


POOL_COL, CONV_H_COL, CONV_B_COL, CONV_C_COL = 0, 1, 2, 3


def _shift_rows(x, k, row):
    n = x.shape[0]
    y = pltpu.roll(x, k % n, 0)
    return jnp.where(row < k, 0.0, y) if k > 0 else jnp.where(row >= n + k, 0.0, y)


def _local_mix_kernel(u_ref, h_ref, bg_ref, cg_ref, pw_ref, ps_ref, cw_ref, op_ref, oc_ref, *, seq):
    row = lax.broadcasted_iota(jnp.int32, (seq, 1), 0)
    lane = lax.broadcasted_iota(jnp.int32, (1, W_POOL), 1)
    u = u_ref[...].astype(F32)
    trailing, leading = {1: u}, {1: u}
    for w in (1, 2, 4):
        trailing[2 * w] = trailing[w] + _shift_rows(trailing[w], w, row)
        leading[2 * w] = leading[w] + _shift_rows(leading[w], -w, row)
    rowf = row.astype(F32)
    win_sum = jnp.zeros_like(u)
    cnt = jnp.zeros_like(u)
    for g, win in enumerate(POOL_WINDOWS):
        half = win // 2
        mg = (lane >= g * POOL_GROUP) & (lane < (g + 1) * POOL_GROUP)
        s = _shift_rows(trailing[half], 1, row) + leading[half]
        n = jnp.minimum(rowf + half, float(seq)) - jnp.maximum(rowf - half, 0.0)
        win_sum = jnp.where(mg, s, win_sum)
        cnt = jnp.where(mg, n, cnt)
    p = win_sum / cnt - u
    pooled = jnp.dot(p.astype(BF16), pw_ref[...], preferred_element_type=F32) * ps_ref[...]
    op_ref[...] = pooled.astype(op_ref.dtype)
    v = cg_ref[...].astype(F32) * h_ref[...].astype(F32)
    conv = (_shift_rows(v, 1, row) * cw_ref[0:1, :] + v * cw_ref[1:2, :] + _shift_rows(v, -1, row) * cw_ref[2:3, :])
    oc_ref[...] = (bg_ref[...].astype(F32) * conv).astype(oc_ref.dtype)


def local_mix(pr, pool_w, pool_scale, conv_w, bn):
    seq = pr.shape[0] // bn
    pw = jax.scipy.linalg.block_diag(*[pool_w[g] for g in range(len(POOL_WINDOWS))]).astype(BF16)
    blk = lambda col: pl.BlockSpec((seq, W_POOL), lambda b: (b, col))
    full = lambda a: pl.BlockSpec(a.shape, lambda b: (0,) * a.ndim)
    args = (pw, pool_scale[None, :], conv_w)
    return pl.pallas_call(
        functools.partial(_local_mix_kernel, seq=seq),
        out_shape=(jax.ShapeDtypeStruct((bn * seq, W_POOL), BF16), jax.ShapeDtypeStruct((bn * seq, W_CONV), BF16)),
        grid=(bn,),
        in_specs=[blk(POOL_COL), blk(CONV_H_COL), blk(CONV_B_COL), blk(CONV_C_COL)] + [full(a) for a in args],
        out_specs=(pl.BlockSpec((seq, W_POOL), lambda b: (b, 0)),
                   pl.BlockSpec((seq, W_CONV), lambda b: (b, 0))),
        compiler_params=_params("parallel"),
        name="local_mix",
    )(pr, pr, pr, pr, *args)


Z_COL, XS_COL, BS_COL, CS_COL = 7, 8, 9, 10
DT_COL = 22
HEADS_PER_GROUP = SSD_HEADS // SSD_GROUPS
GROUP_W = HEADS_PER_GROUP * SSD_HEAD_DIM


def _head_mask(h):
    lane = lax.broadcasted_iota(jnp.int32, (1, W_SSD), 1)
    return (lane >= h * SSD_HEAD_DIM) & (lane < (h + 1) * SSD_HEAD_DIM)


def _expand_heads(v, d):
    out = jnp.zeros((v.shape[0], W_SSD), F32)
    for h in range(SSD_HEADS):
        k = d * SSD_HEADS + h
        out = jnp.where(_head_mask(h), v[:, k:k + 1], out)
    return out


def _ssd_chunk_terms(c, xs_ref, b_ref, c_ref, dt_ref, la_ref, dsk_ref, y_ref, upd_ref, dec_ref, need_y):
    T = SSD_CHUNK
    r0 = pl.multiple_of(c * T, T)
    xs = xs_ref[pl.ds(r0, T), :]
    bm = b_ref[pl.ds(r0, T), :]
    cm = c_ref[pl.ds(r0, T), :]
    dt = dt_ref[pl.ds(r0, T), :]
    la = la_ref[pl.ds(r0, T), :]
    li = lax.broadcasted_iota(jnp.int32, (T, T), 0)
    si = lax.broadcasted_iota(jnp.int32, (T, T), 1)
    causal = si <= li
    prefix = jnp.dot(causal.astype(F32), la, precision=lax.Precision.HIGHEST, preferred_element_type=F32)
    total = prefix[T - 1:T, :]
    lane = lax.broadcasted_iota(jnp.int32, (1, LANES), 1)
    acum = jnp.where(lane < SSD_HEADS, prefix, total - prefix + la)
    bt = bm.astype(F32).T.astype(BF16)
    if need_y:
        acum_t = acum.T
        cb = [lax.dot_general(cm[:, g * SSD_STATE:(g + 1) * SSD_STATE], bm[:, g * SSD_STATE:(g + 1) * SSD_STATE],
                              (((1,), (1,)), ((), ())), preferred_element_type=F32) for g in range(SSD_GROUPS)]
        y = jnp.zeros((T, W_SSD), F32)
    per_dir = []
    for d in range(2):
        acum_e = _expand_heads(acum, d)
        tot_e = _expand_heads(total, d)
        xdt = xs * _expand_heads(dt, d)
        xw = (xdt * jnp.exp(tot_e - acum_e)).astype(BF16)
        dec_ref[c, d, 0:T, :] = jnp.exp(acum_e)
        dec_ref[c, d, T:T + 1, :] = jnp.exp(tot_e)
        scores = []
        if need_y:
            mask = causal if d == 0 else (si >= li)
            for h in range(SSD_HEADS):
                k = d * SSD_HEADS + h
                decay = jnp.exp(jnp.where(mask, acum[:, k:k + 1] - acum_t[k:k + 1, :], -jnp.inf))
                scores.append((cb[h // HEADS_PER_GROUP] * decay).astype(BF16))
        per_dir.append((xw, xdt, scores))
    for d, (xw, xdt, scores) in enumerate(per_dir):
        upd = [jnp.dot(bt[g * SSD_STATE:(g + 1) * SSD_STATE, :], xw[:, g * GROUP_W:(g + 1) * GROUP_W],
                       preferred_element_type=F32) for g in range(SSD_GROUPS)]
        upd_ref[c, d] = jnp.concatenate(upd, axis=1)
        if need_y:
            x_heads = [jnp.where(_head_mask(h), xdt, 0.0).astype(BF16) for h in range(SSD_HEADS)]
            y = y + jnp.dot(jnp.concatenate(scores, axis=1), jnp.concatenate(x_heads, axis=0),
                            preferred_element_type=F32)
    if need_y:
        y_ref[pl.ds(r0, T), :] = xs * dsk_ref[...] + y


def _ssd_chunk_state(c, d, c_ref, y_ref, upd_ref, dec_ref, st_ref, need_y):
    T = SSD_CHUNK
    r0 = pl.multiple_of(c * T, T)
    st = st_ref[d]
    if need_y:
        cm = c_ref[pl.ds(r0, T), :]
        st_b = st.astype(BF16)
        ys = [jnp.dot(cm[:, g * SSD_STATE:(g + 1) * SSD_STATE], st_b[:, g * GROUP_W:(g + 1) * GROUP_W],
                      preferred_element_type=F32) for g in range(SSD_GROUPS)]
        y_ref[pl.ds(r0, T), :] += jnp.concatenate(ys, axis=1) * dec_ref[c, d, 0:T, :]
    st_ref[d] = dec_ref[c, d, T:T + 1, :] * st + upd_ref[c, d]


def _ssd_kernel(z_ref, xs_in, bs_in, cs_in, dtr_ref, cw_ref, cb_ref, dtb_ref, alog_ref, dsk_ref, ng_ref, h0_ref,
                *refs, seq, need_y):
    if need_y:
        o_ref, hT_ref, xs_ref, b_ref, c_ref, dt_ref, la_ref, upd_ref, dec_ref, st_ref, y_ref = refs
    else:
        hT_ref, xs_ref, b_ref, c_ref, dt_ref, la_ref, upd_ref, dec_ref, st_ref = refs
        y_ref = None
    nc = seq // SSD_CHUNK
    row = lax.broadcasted_iota(jnp.int32, (seq, 1), 0)
    for gi, (src, dst) in enumerate(((xs_in, xs_ref), (bs_in, b_ref), (cs_in, c_ref))):
        sl = slice(gi * W_SSD, (gi + 1) * W_SSD)
        x = src[...].astype(F32)
        cv = (_shift_rows(x, 1, row) * cw_ref[0:1, sl] + x * cw_ref[1:2, sl]
              + _shift_rows(x, -1, row) * cw_ref[2:3, sl] + cb_ref[:, sl])
        dst[...] = (cv * jax.nn.sigmoid(cv)).astype(dst.dtype)
    dtv = dtr_ref[...] + dtb_ref[...]
    dt = jnp.maximum(dtv, 0.0) + jnp.log1p(jnp.exp(-jnp.abs(dtv)))
    dt_ref[...] = dt
    la_ref[...] = dt * (-jnp.exp(alog_ref[...]))
    st_ref[...] = h0_ref[0]

    def terms(c, carry):
        _ssd_chunk_terms(c, xs_ref, b_ref, c_ref, dt_ref, la_ref, dsk_ref, y_ref, upd_ref, dec_ref, need_y)
        return carry

    lax.fori_loop(0, nc, terms, 0, unroll=min(16, nc))

    def recur(i, carry):
        _ssd_chunk_state(i, 0, c_ref, y_ref, upd_ref, dec_ref, st_ref, need_y)
        _ssd_chunk_state(nc - 1 - i, 1, c_ref, y_ref, upd_ref, dec_ref, st_ref, need_y)
        return carry

    lax.fori_loop(0, nc, recur, 0, unroll=min(16, nc))
    hT_ref[0] = st_ref[...]
    if need_y:
        z = z_ref[...].astype(F32)
        yz = y_ref[...] * (z * jax.nn.sigmoid(z))
        ms = jnp.mean(yz * yz, axis=-1, keepdims=True)
        o_ref[...] = (yz * lax.rsqrt(ms + EPS) * ng_ref[...]).astype(o_ref.dtype)


def ssd_scan(pr, dt, h0, conv_w, conv_b, dt_bias, a_log, d_skip, norm_g, need_y):
    bn = h0.shape[0]
    seq = pr.shape[0] // bn
    nc = seq // SSD_CHUNK
    pad = LANES - 2 * SSD_HEADS
    dtb = jnp.pad(dt_bias.reshape(1, -1), ((0, 0), (0, pad)))
    alog = jnp.pad(a_log.reshape(1, -1), ((0, 0), (0, pad)))
    dsk = jnp.repeat(d_skip, SSD_HEAD_DIM)[None, :]
    blk = lambda col: pl.BlockSpec((seq, W_SSD), lambda b: (b, col))
    full = lambda a: pl.BlockSpec(a.shape, lambda b: (0,) * a.ndim)
    st_shape = (1, 2, SSD_STATE, W_SSD)
    st_spec = pl.BlockSpec(st_shape, lambda b: (b, 0, 0, 0))
    args = (conv_w, conv_b[None, :], dtb, alog, dsk, norm_g[None, :])
    out_shape = [jax.ShapeDtypeStruct((bn,) + st_shape[1:], F32)]
    out_specs = [st_spec]
    scratch = [pltpu.VMEM((seq, W_SSD), F32),
               pltpu.VMEM((seq, W_SSD), BF16), pltpu.VMEM((seq, W_SSD), BF16),
               pltpu.VMEM((seq, LANES), F32), pltpu.VMEM((seq, LANES), F32),
               pltpu.VMEM((nc, 2, SSD_STATE, W_SSD), F32),
               pltpu.VMEM((nc, 2, SSD_CHUNK + 8, W_SSD), F32),
               pltpu.VMEM(st_shape[1:], F32)]
    if need_y:
        out_shape.insert(0, jax.ShapeDtypeStruct((bn * seq, W_SSD), BF16))
        out_specs.insert(0, pl.BlockSpec((seq, W_SSD), lambda b: (b, 0)))
        scratch.append(pltpu.VMEM((seq, W_SSD), F32))
    res = pl.pallas_call(
        functools.partial(_ssd_kernel, seq=seq, need_y=need_y),
        out_shape=out_shape,
        grid=(bn,),
        in_specs=[blk(Z_COL), blk(XS_COL), blk(BS_COL), blk(CS_COL),
                  pl.BlockSpec((seq, LANES), lambda b: (b, 0))]
                 + [full(a) for a in args] + [st_spec],
        out_specs=out_specs,
        scratch_shapes=scratch,
        compiler_params=_params("parallel"),
        name="ssd_scan",
    )(pr, pr, pr, pr, dt, *args, h0)
    return (res[0], res[1]) if need_y else (None, res[0])


def ssd_mix(pr, dt, pr_c, dt_c, conv_w, conv_b, dt_bias, a_log, d_skip, norm_g, ctx_out, bn):
    h0 = jnp.zeros((bn, 2, SSD_STATE, W_SSD), F32)
    oc, hc = ssd_scan(pr_c, dt_c, h0, conv_w, conv_b, dt_bias, a_log, d_skip, norm_g, ctx_out)
    o, _ = ssd_scan(pr, dt, hc, conv_w, conv_b, dt_bias, a_log, d_skip, norm_g, True)
    return o, oc


def mixer(pr, dt, pr_c, dt_c, pool_w, pool_scale, conv_w, na_bias, s_conv_w, s_conv_b, dt_bias, a_log, d_skip,
          s_norm_g, ctx_out, bn):
    o_ssd, oc_ssd = ssd_mix(pr, dt, pr_c, dt_c, s_conv_w, s_conv_b, dt_bias, a_log, d_skip, s_norm_g, ctx_out, bn)
    o = [*local_mix(pr, pool_w, pool_scale, conv_w, bn), na_attention(pr, pr_c, na_bias, bn), o_ssd]
    if not ctx_out:
        return o, None
    oc = [*local_mix(pr_c, pool_w, pool_scale, conv_w, bn), ctx_attention(pr_c, bn), oc_ssd]
    return o, oc


TM = 512
TM_ROUTE = 1024
TM_PROJ = 1024
TM_GRP = 512


def kernel(x, c, ctx, c_ctx, w_ada, b_ada, g_mix, g_ffn, w_in, w_out, pool_w, pool_scale, conv_w, na_rpb,
           ssd_conv_w, ssd_conv_b, ssd_dt_bias, ssd_a_log, ssd_d, ssd_norm_g, ffn_w_gu, ffn_w_down,
           moe_router, moe_w_gu, moe_w_down, g_final):
    bn, S, d = x.shape
    Lc = ctx.shape[1]
    m, mc = bn * S, bn * Lc
    tpb = S // TM
    x2 = x.reshape(m, d)
    xc2 = ctx.reshape(mc, d)
    c_rows = jnp.concatenate([c, c_ctx[None, :], jnp.zeros((7, d), F32)], axis=0)
    for l in range(DEPTH):
        ctx_out = l < DEPTH - 1
        last = l == DEPTH - 1
        ada = ada_modulation(c_rows, w_ada[l].astype(BF16), b_ada[l][None, :])
        mod = ada[:bn].reshape(bn, 6, d)
        mod_c = ada[bn:bn + 1].reshape(1, 6, d)
        w_in_l = jnp.pad(w_in[l], ((0, 0), (0, D_IN_PAD - D_IN_PROJ))).astype(BF16)
        g_m = g_mix[l][None, :]
        g_f = g_ffn[l][None, :]
        pr, dt = in_proj(x2, g_m, mod, w_in_l, S // TM_PROJ, TM_PROJ)
        pr_c, dt_c = in_proj(xc2, g_m, mod_c, w_in_l, None, min(TM_PROJ, mc))
        na_bias = na_bias_table(na_rpb[l], S // GRID_W)
        o, oc = mixer(pr, dt, pr_c, dt_c, pool_w[l], pool_scale[l], conv_w[l], na_bias, ssd_conv_w[l],
                      ssd_conv_b[l], ssd_dt_bias[l], ssd_a_log[l], ssd_d[l], ssd_norm_g[l], ctx_out, bn)
        w_out_l = w_out[l].astype(BF16)
        j = l // 2
        if l % 2 == 0:
            w_gu = ffn_w_gu[j].astype(BF16)
            w_dn = ffn_w_down[j].astype(BF16)
            x2 = ffn_dense(x2, o, g_f, mod, w_out_l, w_gu, w_dn, tpb, TM)
            if ctx_out:
                xc2 = ffn_dense(xc2, oc, g_f, mod_c, w_out_l, w_gu, w_dn, None, min(TM, mc))
        else:
            w_r = jnp.pad(moe_router[j], ((0, 0), (0, LANES - N_EXPERTS))).astype(BF16)
            w_gu, w_dn = moe_w_gu[j], moe_w_down[j]
            x2 = moe_block(x2, o, g_f, mod, w_out_l, w_r, w_gu, w_dn, S // TM_ROUTE, TM_ROUTE, TM_GRP,
                           g_final[None, :] if last else None)
            if ctx_out:
                xc2 = moe_block(xc2, oc, g_f, mod_c, w_out_l, w_r, w_gu, w_dn, None, min(TM_ROUTE, mc), TM_GRP)
            if last:
                return x2.reshape(bn, S, d)
    return final_norm(x2, g_final[None, :], TM).reshape(bn, S, d)
```

```python
import functools
import math

import numpy as np
import jax
import jax.numpy as jnp
from jax import lax
from jax.experimental import pallas as pl
from jax.experimental.pallas import tpu as pltpu

DEPTH = 2
GRID_W = 64
EPS = 1e-6
W_POOL = 256
W_CONV = 256
W_NA = 256
W_SSD = 256
POOL_WINDOWS = (2, 4, 8, 16)
POOL_GROUP = W_POOL // len(POOL_WINDOWS)
NA_HEADS = 4
NA_HEAD_DIM = W_NA // NA_HEADS
WIN_R = 8
WIN_C = 16
SSD_HEAD_DIM = 64
SSD_HEADS = W_SSD // SSD_HEAD_DIM
SSD_GROUPS = 2
SSD_STATE = 128
SSD_CHUNK = 128
SSD_XBC = W_SSD + 2 * SSD_GROUPS * SSD_STATE
D_IN_PROJ = W_POOL + 3 * W_CONV + 3 * W_NA + W_SSD + SSD_XBC + 2 * SSD_HEADS
N_EXPERTS = 8
TOP_K = 2

LANES = 128
D_IN_PAD = -(-D_IN_PROJ // LANES) * LANES
VMEM_LIMIT = 56 * 1024 * 1024
BF16 = jnp.bfloat16
F32 = jnp.float32


def _params(*sem):
    return pltpu.CompilerParams(dimension_semantics=sem, vmem_limit_bytes=VMEM_LIMIT)


def _norm_mod(x, g, scale, shift):
    ms = jnp.mean(x * x, axis=-1, keepdims=True)
    return (x * lax.rsqrt(ms + EPS) * g) * (1.0 + scale) + shift


def _mod_map(tiles_per_batch):
    if tiles_per_batch is None:
        return lambda i, *_: (0, 0, 0)
    return lambda i, *_: (i // tiles_per_batch, 0, 0)


def _resident(shape, index_map):
    return pl.BlockSpec(shape, index_map, pipeline_mode=pl.Buffered(1))


def _ada_kernel(c_ref, w_ref, b_ref, o_ref):
    c = c_ref[...]
    s = c * jax.nn.sigmoid(c)
    o_ref[...] = jnp.dot(s.astype(BF16), w_ref[...], preferred_element_type=F32) + b_ref[...]


def ada_modulation(c_rows, w, b):
    r, d = c_rows.shape
    n = w.shape[1]
    tn = 1536
    return pl.pallas_call(
        _ada_kernel,
        out_shape=jax.ShapeDtypeStruct((r, n), F32),
        grid=(n // tn,),
        in_specs=[pl.BlockSpec((r, d), lambda j: (0, 0)),
                  pl.BlockSpec((d, tn), lambda j: (0, j)),
                  pl.BlockSpec((1, tn), lambda j: (0, j))],
        out_specs=pl.BlockSpec((r, tn), lambda j: (0, j)),
        compiler_params=_params("arbitrary"),
        name="ada_modulation",
    )(c_rows, w, b)


PROJ_ROWS = 512


def _in_proj_kernel(x_ref, g_ref, mod_ref, w_ref, o_ref, dt_ref):
    tm = x_ref.shape[0]
    parts = [slice(r0, min(r0 + PROJ_ROWS, tm)) for r0 in range(0, tm, PROJ_ROWS)]
    hs = [_norm_mod(x_ref[rs, :], g_ref[...], mod_ref[0, 1:2, :], mod_ref[0, 0:1, :]).astype(BF16) for rs in parts]
    for rs, h in zip(parts, hs):
        pr = jnp.dot(h, w_ref[...], preferred_element_type=F32)
        o_ref[rs, :] = pr.astype(o_ref.dtype)
        dt_ref[rs, :] = pr[:, DT_COL * LANES:(DT_COL + 1) * LANES]


def in_proj(x2, g, mod, w, tiles_per_batch, tm):
    m, d = x2.shape
    n = w.shape[1]
    return pl.pallas_call(
        _in_proj_kernel,
        out_shape=(jax.ShapeDtypeStruct((m, n), BF16), jax.ShapeDtypeStruct((m, LANES), F32)),
        grid=(m // tm,),
        in_specs=[pl.BlockSpec((tm, d), lambda i: (i, 0)),
                  pl.BlockSpec((1, d), lambda i: (0, 0)),
                  pl.BlockSpec((1, 6, d), _mod_map(tiles_per_batch)),
                  _resident((d, n), lambda i: (0, 0))],
        out_specs=(pl.BlockSpec((tm, n), lambda i: (i, 0)), pl.BlockSpec((tm, LANES), lambda i: (i, 0))),
        compiler_params=_params("parallel"),
        name="in_proj",
    )(x2, g, mod, w)


def _mix_residual(x, mod_ref, w_ref, part_refs):
    y, k0 = None, 0
    for p_ref in part_refs:
        k = p_ref.shape[1]
        t = jnp.dot(p_ref[...], w_ref[k0:k0 + k, :], preferred_element_type=F32)
        y = t if y is None else y + t
        k0 += k
    return x + mod_ref[0, 2:3, :] * y


def _part_specs(parts, tm):
    return [pl.BlockSpec((tm, p.shape[1]), lambda i, *_: (i, 0)) for p in parts]


FF_CHUNK = 512


def _swiglu(h, wgu, wd, ff):
    acc = None
    for c0 in range(0, ff, FF_CHUNK):
        c1 = min(c0 + FF_CHUNK, ff)
        gg = jnp.dot(h, wgu(c0, c1), preferred_element_type=F32)
        uu = jnp.dot(h, wgu(ff + c0, ff + c1), preferred_element_type=F32)
        a = (gg * jax.nn.sigmoid(gg) * uu).astype(BF16)
        part = jnp.dot(a, wd(c0, c1), preferred_element_type=F32)
        acc = part if acc is None else acc + part
    return acc


def _ffn_kernel(x_ref, g_ref, mod_ref, wout_ref, wgu_ref, wd_ref, *refs):
    x = _mix_residual(x_ref[...], mod_ref, wout_ref, refs[:-1])
    y_ref = refs[-1]
    h = _norm_mod(x, g_ref[...], mod_ref[0, 4:5, :], mod_ref[0, 3:4, :]).astype(BF16)
    y = _swiglu(h, lambda a, b: wgu_ref[:, a:b], lambda a, b: wd_ref[a:b, :], wd_ref.shape[0])
    y_ref[...] = x + mod_ref[0, 5:6, :] * y


def ffn_dense(x2, parts, g, mod, w_out, w_gu, w_down, tiles_per_batch, tm):
    m, d = x2.shape
    return pl.pallas_call(
        _ffn_kernel,
        out_shape=jax.ShapeDtypeStruct((m, d), F32),
        grid=(m // tm,),
        in_specs=[pl.BlockSpec((tm, d), lambda i: (i, 0)),
                  pl.BlockSpec((1, d), lambda i: (0, 0)),
                  pl.BlockSpec((1, 6, d), _mod_map(tiles_per_batch)),
                  _resident(w_out.shape, lambda i: (0, 0)),
                  _resident(w_gu.shape, lambda i: (0, 0)),
                  _resident(w_down.shape, lambda i: (0, 0))] + _part_specs(parts, tm),
        out_specs=pl.BlockSpec((tm, d), lambda i: (i, 0)),
        compiler_params=_params("parallel"),
        name="ffn_dense",
    )(x2, g, mod, w_out, w_gu, w_down, *parts)


ROUTE_E1, ROUTE_E2, ROUTE_R1, ROUTE_R2, ROUTE_G1, ROUTE_G2 = range(6)
ROUTE_ROWS = 8


def _router_kernel(x_ref, g_ref, mod_ref, wout_ref, wr_ref, *refs):
    part_refs = refs[:-6]
    x1_ref, h_ref, route_ref, route_t_ref, cnt_ref, base_ref = refs[-6:]

    @pl.when(pl.program_id(0) == 0)
    def _():
        base_ref[...] = jnp.zeros_like(base_ref)

    x = _mix_residual(x_ref[...], mod_ref, wout_ref, part_refs)
    x1_ref[...] = x
    h = _norm_mod(x, g_ref[...], mod_ref[0, 4:5, :], mod_ref[0, 3:4, :]).astype(BF16)
    h_ref[...] = h
    tm = h.shape[0]
    logits = jnp.dot(h, wr_ref[...], preferred_element_type=F32)
    lane = lax.broadcasted_iota(jnp.int32, logits.shape, 1)
    neg = jnp.float32(-jnp.inf)
    logits = jnp.where(lane < N_EXPERTS, logits, neg)
    m1 = jnp.max(logits, axis=-1, keepdims=True)
    i1 = jnp.min(jnp.where(logits == m1, lane, LANES), axis=-1, keepdims=True)
    rest = jnp.where(lane == i1, neg, logits)
    m2 = jnp.max(rest, axis=-1, keepdims=True)
    i2 = jnp.min(jnp.where(rest == m2, lane, LANES), axis=-1, keepdims=True)
    e2 = jnp.exp(m2 - m1)
    g1 = 1.0 / (1.0 + e2)
    g2 = e2 / (1.0 + e2)
    sel1, sel2 = lane == i1, lane == i2
    sel = jnp.where(sel1 | sel2, 1.0, 0.0)
    li = lax.broadcasted_iota(jnp.int32, (tm, tm), 0)
    si = lax.broadcasted_iota(jnp.int32, (tm, tm), 1)
    before = jnp.where(si < li, 1.0, 0.0).astype(BF16)
    rank = jnp.dot(before, sel.astype(BF16), preferred_element_type=F32) + base_ref[...]
    r1 = jnp.sum(jnp.where(sel1, rank, 0.0), axis=-1, keepdims=True)
    r2 = jnp.sum(jnp.where(sel2, rank, 0.0), axis=-1, keepdims=True)
    base_ref[...] += jnp.sum(sel, axis=0, keepdims=True)
    cnt_ref[...] = base_ref[...]
    fields = (i1.astype(F32), i2.astype(F32), r1, r2, g1, g2)
    route = jnp.zeros(logits.shape, F32)
    for k, v in enumerate(fields):
        route = jnp.where(lane == k, v, route)
    route_ref[...] = route
    route_t_ref[...] = route.T[:ROUTE_ROWS, :]


def moe_router(x2, parts, g, mod, w_out, w_router, tiles_per_batch, tm):
    m, d = x2.shape
    return pl.pallas_call(
        _router_kernel,
        out_shape=(jax.ShapeDtypeStruct((m, d), F32), jax.ShapeDtypeStruct((m, d), BF16),
                   jax.ShapeDtypeStruct((m, LANES), F32), jax.ShapeDtypeStruct((ROUTE_ROWS, m), F32),
                   jax.ShapeDtypeStruct((1, LANES), F32)),
        grid=(m // tm,),
        in_specs=[pl.BlockSpec((tm, d), lambda i: (i, 0)),
                  pl.BlockSpec((1, d), lambda i: (0, 0)),
                  pl.BlockSpec((1, 6, d), _mod_map(tiles_per_batch)),
                  _resident(w_out.shape, lambda i: (0, 0)),
                  pl.BlockSpec((d, LANES), lambda i: (0, 0))] + _part_specs(parts, tm),
        out_specs=(pl.BlockSpec((tm, d), lambda i: (i, 0)),
                   pl.BlockSpec((tm, d), lambda i: (i, 0)),
                   pl.BlockSpec((tm, LANES), lambda i: (i, 0)),
                   pl.BlockSpec((ROUTE_ROWS, tm), lambda i: (0, i)),
                   pl.BlockSpec((1, LANES), lambda i: (0, 0))),
        scratch_shapes=[pltpu.VMEM((1, LANES), F32)],
        compiler_params=_params("arbitrary"),
        name="moe_router",
    )(x2, g, mod, w_out, w_router, *parts)


def _moe_kernel(tile_ref, exp_ref, start_ref, end_ref, xg_ref, wgu_ref, wd_ref, y_ref, wgu_b, wd_b):
    w = pl.program_id(0)
    tm = xg_ref.shape[0]
    start, end = start_ref[w], end_ref[w]
    t0 = tile_ref[w] * tm

    @pl.when((w == 0) | (exp_ref[w] != exp_ref[jnp.maximum(w - 1, 0)]))
    def _():
        wgu_b[...] = wgu_ref[0].astype(BF16)
        wd_b[...] = wd_ref[0].astype(BF16)

    @pl.when(end > start)
    def _():
        y = _swiglu(xg_ref[...], lambda a, b: wgu_b[:, a:b], lambda a, b: wd_b[a:b, :], wd_b.shape[0])
        y = y.astype(y_ref.dtype)

        @pl.when(start == t0)
        def _():
            y_ref[...] = y

        @pl.when(start != t0)
        def _():
            row = t0 + lax.broadcasted_iota(jnp.int32, (tm, 1), 0)
            y_ref[...] = jnp.where(row >= start, y, y_ref[...])


def moe_grouped(item_tile, item_expert, item_start, item_end, xg, w_gu, w_down, tm):
    p, d = xg.shape
    grid_spec = pltpu.PrefetchScalarGridSpec(
        num_scalar_prefetch=4,
        grid=(item_tile.shape[0],),
        in_specs=[pl.BlockSpec((tm, d), lambda w, it, ie, s, e: (it[w], 0)),
                  pl.BlockSpec((1,) + w_gu.shape[1:], lambda w, it, ie, s, e: (ie[w], 0, 0)),
                  pl.BlockSpec((1,) + w_down.shape[1:], lambda w, it, ie, s, e: (ie[w], 0, 0))],
        out_specs=pl.BlockSpec((tm, d), lambda w, it, ie, s, e: (it[w], 0)),
        scratch_shapes=[pltpu.VMEM(w_gu.shape[1:], BF16), pltpu.VMEM(w_down.shape[1:], BF16)],
    )
    return pl.pallas_call(
        _moe_kernel,
        out_shape=jax.ShapeDtypeStruct((p, d), BF16),
        grid_spec=grid_spec,
        compiler_params=_params("arbitrary"),
        name="moe_grouped",
    )(item_tile, item_expert, item_start, item_end, xg, w_gu, w_down)


def _moe_combine_kernel(x_ref, ya_ref, yb_ref, route_ref, mod_ref, *refs):
    o_ref = refs[-1]
    r = route_ref[...]
    y = (r[:, ROUTE_G1:ROUTE_G1 + 1] * ya_ref[...].astype(F32) + r[:, ROUTE_G2:ROUTE_G2 + 1] * yb_ref[...].astype(F32))
    x = x_ref[...] + mod_ref[0, 5:6, :] * y
    if len(refs) == 2:
        ms = jnp.mean(x * x, axis=-1, keepdims=True)
        x = x * lax.rsqrt(ms + EPS) * refs[0][...]
    o_ref[...] = x


def moe_combine(x2, ya, yb, route, mod, tiles_per_batch, tm, g_final=None):
    m, d = x2.shape
    row = pl.BlockSpec((tm, d), lambda i: (i, 0))
    specs = [row, row, row, pl.BlockSpec((tm, LANES), lambda i: (i, 0)),
             pl.BlockSpec((1, 6, d), _mod_map(tiles_per_batch))]
    args = [x2, ya, yb, route, mod]
    if g_final is not None:
        specs.append(pl.BlockSpec((1, d), lambda i: (0, 0)))
        args.append(g_final)
    return pl.pallas_call(
        _moe_combine_kernel,
        out_shape=jax.ShapeDtypeStruct((m, d), F32),
        grid=(m // tm,),
        in_specs=specs,
        out_specs=row,
        compiler_params=_params("parallel"),
        name="moe_combine",
    )(*args)


def moe_block(x2, parts, g, mod, w_out, w_router, w_gu, w_down, tiles_per_batch, tm_tok, tm_grp, g_final=None):
    m, d = x2.shape
    x2, h, route, route_t, cnt = moe_router(x2, parts, g, mod, w_out, w_router, tiles_per_batch, tm_tok)
    cnt = cnt[0, :N_EXPERTS].astype(jnp.int32)
    p = m * TOP_K
    off = jnp.cumsum(cnt) - cnt
    field = lambda k: route_t[k].astype(jnp.int32)

    def row_of(e, r):
        for k in range(N_EXPERTS):
            r = r + jnp.where(e == k, off[k], 0)
        return r

    d1 = row_of(field(ROUTE_E1), field(ROUTE_R1))
    d2 = row_of(field(ROUTE_E2), field(ROUTE_R2))
    tok = jnp.arange(m, dtype=jnp.int32)
    _, src = lax.sort_key_val(jnp.concatenate([d1, d2]), jnp.concatenate([tok, tok]))
    n_row_tiles = p // tm_grp
    start = jnp.sort(jnp.concatenate([jnp.arange(n_row_tiles, dtype=jnp.int32) * tm_grp, off]))
    end = jnp.concatenate([start[1:], jnp.full((1,), p, jnp.int32)])
    item_tile = jnp.minimum(start // tm_grp, n_row_tiles - 1)
    item_expert = jnp.sum((off[None, :] <= start[:, None]).astype(jnp.int32), axis=1) - 1
    rows = lambda a, idx: a.at[idx].get(mode="promise_in_bounds")
    yg = moe_grouped(item_tile, item_expert, start, end, rows(h, src), w_gu, w_down, tm_grp)
    return moe_combine(x2, rows(yg, d1), rows(yg, d2), route, mod, tiles_per_batch, tm_tok, g_final)


def _final_norm_kernel(x_ref, g_ref, o_ref):
    x = x_ref[...]
    ms = jnp.mean(x * x, axis=-1, keepdims=True)
    o_ref[...] = x * lax.rsqrt(ms + EPS) * g_ref[...]


def final_norm(x2, g, tm):
    m, d = x2.shape
    return pl.pallas_call(
        _final_norm_kernel,
        out_shape=jax.ShapeDtypeStruct((m, d), F32),
        grid=(m // tm,),
        in_specs=[pl.BlockSpec((tm, d), lambda i: (i, 0)), pl.BlockSpec((1, d), lambda i: (0, 0))],
        out_specs=pl.BlockSpec((tm, d), lambda i: (i, 0)),
        compiler_params=_params("parallel"),
        name="final_norm",
    )(x2, g)


NA_QROWS = 4
NA_KROWS = NA_QROWS + WIN_R
NA_STEP_BLOCKS = 8
Q_COL, K_COL, V_COL = 4, 5, 6


def _na_key_start(j, rows):
    return np.clip(j * NA_QROWS - WIN_R // 2, 0, rows - NA_KROWS)


def _na_bias_index(rows):
    nblk = rows // NA_QROWS
    pats = []
    for j in range(nblk):
        start = _na_key_start(j, rows)
        r = j * NA_QROWS + np.arange(NA_QROWS)
        sr = np.clip(r - WIN_R // 2, 0, rows - WIN_R)
        kr = start + np.arange(NA_KROWS)
        rvalid = (kr[None, :] >= sr[:, None]) & (kr[None, :] < sr[:, None] + WIN_R)
        ri = np.clip(kr[None, :] - r[:, None] + WIN_R - 1, 0, 2 * WIN_R - 2)
        pats.append((ri, rvalid))
    for j in range(2, nblk - 1):
        assert all(np.array_equal(a, b) for a, b in zip(pats[1], pats[j]))
    sel = [pats[0], pats[1], pats[nblk - 1]]
    ri = np.stack([p[0] for p in sel])
    rvalid = np.stack([p[1] for p in sel])
    c = np.arange(GRID_W)
    sc = np.clip(c - WIN_C // 2, 0, GRID_W - WIN_C)
    cvalid = (c[None, :] >= sc[:, None]) & (c[None, :] < sc[:, None] + WIN_C)
    ci = np.clip(c[None, :] - c[:, None] + WIN_C - 1, 0, 2 * WIN_C - 2)
    c_onehot = (ci[..., None] == np.arange(2 * WIN_C - 1)).astype(np.float32)
    valid = rvalid[:, :, None, :, None] & cvalid[None, None, :, None, :]
    return ri, c_onehot, valid


def na_bias_table(rpb, rows):
    ri, c_onehot, valid = _na_bias_index(rows)
    toep = jnp.einsum('hrd,qkd->hrqk', rpb, c_onehot, precision=lax.Precision.HIGHEST)
    b = jnp.take(toep, ri.reshape(-1), axis=1).reshape((NA_HEADS,) + ri.shape + (GRID_W, GRID_W))
    b = jnp.transpose(b, (1, 0, 2, 4, 3, 5))
    b = jnp.where(valid[:, None], b, -jnp.inf)
    return b.reshape(3, NA_HEADS, NA_QROWS * GRID_W, NA_KROWS * GRID_W).astype(F32)


def _attend_heads(q, key_sets, bias_fn):
    n, w = q.shape
    lane = lax.broadcasted_iota(jnp.int32, (1, w), 1)
    head_masks = [(lane >= h * NA_HEAD_DIM) & (lane < (h + 1) * NA_HEAD_DIM) for h in range(NA_HEADS)]
    all_scores = []
    for h, mh in enumerate(head_masks):
        qh = jnp.where(mh, q, 0.0).astype(BF16)
        scores = []
        for i, (k, _) in enumerate(key_sets):
            s = lax.dot_general(qh, k, (((1,), (1,)), ((), ())), preferred_element_type=F32)
            b = bias_fn(i, h)
            scores.append(s if b is None else s + b)
        all_scores.append(scores)
    all_probs = []
    for scores in all_scores:
        m = scores[0].max(axis=-1, keepdims=True)
        for s in scores[1:]:
            m = jnp.maximum(m, s.max(axis=-1, keepdims=True))
        denom = jnp.zeros((n, 1), F32)
        probs = []
        for s in scores:
            p = jnp.exp(s - m)
            denom = denom + p.sum(axis=-1, keepdims=True)
            probs.append(p.astype(BF16))
        all_probs.append((probs, denom))
    out = jnp.zeros((n, w), F32)
    for mh, (probs, denom) in zip(head_masks, all_probs):
        acc = jnp.zeros((n, w), F32)
        for p, (_, v) in zip(probs, key_sets):
            acc = acc + jnp.dot(p, v, preferred_element_type=F32)
        out = out + jnp.where(mh, acc / denom, 0.0)
    return out


def _na_kernel(q_ref, k_ref, v_ref, kc_ref, vc_ref, bias_ref, o_ref, *, rows):
    nblk = rows // NA_QROWS
    nq, nk = NA_QROWS * GRID_W, NA_KROWS * GRID_W
    for s in range(NA_STEP_BLOCKS):
        j = pl.program_id(1) * NA_STEP_BLOCKS + s
        start = jnp.clip(j * NA_QROWS - WIN_R // 2, 0, rows - NA_KROWS)
        t0 = pl.multiple_of(start * GRID_W, GRID_W)
        pattern = jnp.where(j == 0, 0, jnp.where(j == nblk - 1, 2, 1))
        kw = k_ref[pl.ds(t0, nk), :]
        vw = v_ref[pl.ds(t0, nk), :]
        q = q_ref[s * nq:(s + 1) * nq, :] * (1.0 / math.sqrt(NA_HEAD_DIM))
        o = _attend_heads(q, [(kw, vw), (kc_ref[...], vc_ref[...])],
                          lambda i, h: bias_ref[pattern, h] if i == 0 else None)
        o_ref[s * nq:(s + 1) * nq, :] = o.astype(o_ref.dtype)


def na_attention(pr, pr_c, bias, bn):
    S, Lc = pr.shape[0] // bn, pr_c.shape[0] // bn
    rows = S // GRID_W
    nsteps = rows // NA_QROWS // NA_STEP_BLOCKS
    nq = NA_STEP_BLOCKS * NA_QROWS * GRID_W
    return pl.pallas_call(
        functools.partial(_na_kernel, rows=rows),
        out_shape=jax.ShapeDtypeStruct((bn * S, W_NA), BF16),
        grid=(bn, nsteps),
        in_specs=[pl.BlockSpec((nq, W_NA), lambda b, j: (b * nsteps + j, Q_COL)),
                  pl.BlockSpec((S, W_NA), lambda b, j: (b, K_COL)),
                  pl.BlockSpec((S, W_NA), lambda b, j: (b, V_COL)),
                  pl.BlockSpec((Lc, W_NA), lambda b, j: (b, K_COL)),
                  pl.BlockSpec((Lc, W_NA), lambda b, j: (b, V_COL)),
                  _resident(bias.shape, lambda b, j: (0, 0, 0, 0))],
        out_specs=pl.BlockSpec((nq, W_NA), lambda b, j: (b * nsteps + j, 0)),
        compiler_params=_params("parallel", "arbitrary"),
        name="na_attention",
    )(pr, pr, pr, pr_c, pr_c, bias)


def _ctx_attn_kernel(q_ref, k_ref, v_ref, o_ref):
    q = q_ref[...] * (1.0 / math.sqrt(NA_HEAD_DIM))
    o = _attend_heads(q, [(k_ref[...], v_ref[...])], lambda i, h: None)
    o_ref[...] = o.astype(o_ref.dtype)


def ctx_attention(pr_c, bn):
    Lc = pr_c.shape[0] // bn
    return pl.pallas_call(
        _ctx_attn_kernel,
        out_shape=jax.ShapeDtypeStruct((bn * Lc, W_NA), BF16),
        grid=(bn,),
        in_specs=[pl.BlockSpec((Lc, W_NA), lambda b: (b, Q_COL)),
                  pl.BlockSpec((Lc, W_NA), lambda b: (b, K_COL)),
                  pl.BlockSpec((Lc, W_NA), lambda b: (b, V_COL))],
        out_specs=pl.BlockSpec((Lc, W_NA), lambda b: (b, 0)),
        compiler_params=_params("parallel"),
        name="ctx_attention",
    )(pr_c, pr_c, pr_c)


POOL_COL, CONV_H_COL, CONV_B_COL, CONV_C_COL = 0, 1, 2, 3


def _shift_rows(x, k, row):
    n = x.shape[0]
    y = pltpu.roll(x, k % n, 0)
    return jnp.where(row < k, 0.0, y) if k > 0 else jnp.where(row >= n + k, 0.0, y)


def _local_mix_kernel(u_ref, h_ref, bg_ref, cg_ref, pw_ref, ps_ref, cw_ref, op_ref, oc_ref, *, seq):
    row = lax.broadcasted_iota(jnp.int32, (seq, 1), 0)
    lane = lax.broadcasted_iota(jnp.int32, (1, W_POOL), 1)
    u = u_ref[...].astype(F32)
    trailing, leading = {1: u}, {1: u}
    for w in (1, 2, 4):
        trailing[2 * w] = trailing[w] + _shift_rows(trailing[w], w, row)
        leading[2 * w] = leading[w] + _shift_rows(leading[w], -w, row)
    rowf = row.astype(F32)
    win_sum = jnp.zeros_like(u)
    cnt = jnp.zeros_like(u)
    for g, win in enumerate(POOL_WINDOWS):
        half = win // 2
        mg = (lane >= g * POOL_GROUP) & (lane < (g + 1) * POOL_GROUP)
        s = _shift_rows(trailing[half], 1, row) + leading[half]
        n = jnp.minimum(rowf + half, float(seq)) - jnp.maximum(rowf - half, 0.0)
        win_sum = jnp.where(mg, s, win_sum)
        cnt = jnp.where(mg, n, cnt)
    p = win_sum / cnt - u
    pooled = jnp.dot(p.astype(BF16), pw_ref[...], preferred_element_type=F32) * ps_ref[...]
    op_ref[...] = pooled.astype(op_ref.dtype)
    v = cg_ref[...].astype(F32) * h_ref[...].astype(F32)
    conv = (_shift_rows(v, 1, row) * cw_ref[0:1, :] + v * cw_ref[1:2, :] + _shift_rows(v, -1, row) * cw_ref[2:3, :])
    oc_ref[...] = (bg_ref[...].astype(F32) * conv).astype(oc_ref.dtype)


def local_mix(pr, pool_w, pool_scale, conv_w, bn):
    seq = pr.shape[0] // bn
    pw = jax.scipy.linalg.block_diag(*[pool_w[g] for g in range(len(POOL_WINDOWS))]).astype(BF16)
    blk = lambda col: pl.BlockSpec((seq, W_POOL), lambda b: (b, col))
    full = lambda a: pl.BlockSpec(a.shape, lambda b: (0,) * a.ndim)
    args = (pw, pool_scale[None, :], conv_w)
    return pl.pallas_call(
        functools.partial(_local_mix_kernel, seq=seq),
        out_shape=(jax.ShapeDtypeStruct((bn * seq, W_POOL), BF16), jax.ShapeDtypeStruct((bn * seq, W_CONV), BF16)),
        grid=(bn,),
        in_specs=[blk(POOL_COL), blk(CONV_H_COL), blk(CONV_B_COL), blk(CONV_C_COL)] + [full(a) for a in args],
        out_specs=(pl.BlockSpec((seq, W_POOL), lambda b: (b, 0)),
                   pl.BlockSpec((seq, W_CONV), lambda b: (b, 0))),
        compiler_params=_params("parallel"),
        name="local_mix",
    )(pr, pr, pr, pr, *args)


Z_COL, XS_COL, BS_COL, CS_COL = 7, 8, 9, 10
DT_COL = 22
HEADS_PER_GROUP = SSD_HEADS // SSD_GROUPS
GROUP_W = HEADS_PER_GROUP * SSD_HEAD_DIM


def _head_mask(h):
    lane = lax.broadcasted_iota(jnp.int32, (1, W_SSD), 1)
    return (lane >= h * SSD_HEAD_DIM) & (lane < (h + 1) * SSD_HEAD_DIM)


def _expand_heads(v, d):
    out = jnp.zeros((v.shape[0], W_SSD), F32)
    for h in range(SSD_HEADS):
        k = d * SSD_HEADS + h
        out = jnp.where(_head_mask(h), v[:, k:k + 1], out)
    return out


def _ssd_chunk_terms(c, xs_ref, b_ref, c_ref, dt_ref, la_ref, dsk_ref, y_ref, upd_ref, dec_ref, need_y):
    T = SSD_CHUNK
    r0 = pl.multiple_of(c * T, T)
    xs = xs_ref[pl.ds(r0, T), :]
    bm = b_ref[pl.ds(r0, T), :]
    cm = c_ref[pl.ds(r0, T), :]
    dt = dt_ref[pl.ds(r0, T), :]
    la = la_ref[pl.ds(r0, T), :]
    li = lax.broadcasted_iota(jnp.int32, (T, T), 0)
    si = lax.broadcasted_iota(jnp.int32, (T, T), 1)
    causal = si <= li
    prefix = jnp.dot(causal.astype(F32), la, precision=lax.Precision.HIGHEST, preferred_element_type=F32)
    total = prefix[T - 1:T, :]
    lane = lax.broadcasted_iota(jnp.int32, (1, LANES), 1)
    acum = jnp.where(lane < SSD_HEADS, prefix, total - prefix + la)
    bt = bm.astype(F32).T.astype(BF16)
    if need_y:
        acum_t = acum.T
        cb = [lax.dot_general(cm[:, g * SSD_STATE:(g + 1) * SSD_STATE], bm[:, g * SSD_STATE:(g + 1) * SSD_STATE],
                              (((1,), (1,)), ((), ())), preferred_element_type=F32) for g in range(SSD_GROUPS)]
        y = jnp.zeros((T, W_SSD), F32)
    per_dir = []
    for d in range(2):
        acum_e = _expand_heads(acum, d)
        tot_e = _expand_heads(total, d)
        xdt = xs * _expand_heads(dt, d)
        xw = (xdt * jnp.exp(tot_e - acum_e)).astype(BF16)
        dec_ref[c, d, 0:T, :] = jnp.exp(acum_e)
        dec_ref[c, d, T:T + 1, :] = jnp.exp(tot_e)
        scores = []
        if need_y:
            mask = causal if d == 0 else (si >= li)
            for h in range(SSD_HEADS):
                k = d * SSD_HEADS + h
                decay = jnp.exp(jnp.where(mask, acum[:, k:k + 1] - acum_t[k:k + 1, :], -jnp.inf))
                scores.append((cb[h // HEADS_PER_GROUP] * decay).astype(BF16))
        per_dir.append((xw, xdt, scores))
    for d, (xw, xdt, scores) in enumerate(per_dir):
        upd = [jnp.dot(bt[g * SSD_STATE:(g + 1) * SSD_STATE, :], xw[:, g * GROUP_W:(g + 1) * GROUP_W],
                       preferred_element_type=F32) for g in range(SSD_GROUPS)]
        upd_ref[c, d] = jnp.concatenate(upd, axis=1)
        if need_y:
            x_heads = [jnp.where(_head_mask(h), xdt, 0.0).astype(BF16) for h in range(SSD_HEADS)]
            y = y + jnp.dot(jnp.concatenate(scores, axis=1), jnp.concatenate(x_heads, axis=0),
                            preferred_element_type=F32)
    if need_y:
        y_ref[pl.ds(r0, T), :] = xs * dsk_ref[...] + y


def _ssd_chunk_state(c, d, c_ref, y_ref, upd_ref, dec_ref, st_ref, need_y):
    T = SSD_CHUNK
    r0 = pl.multiple_of(c * T, T)
    st = st_ref[d]
    if need_y:
        cm = c_ref[pl.ds(r0, T), :]
        st_b = st.astype(BF16)
        ys = [jnp.dot(cm[:, g * SSD_STATE:(g + 1) * SSD_STATE], st_b[:, g * GROUP_W:(g + 1) * GROUP_W],
                      preferred_element_type=F32) for g in range(SSD_GROUPS)]
        y_ref[pl.ds(r0, T), :] += jnp.concatenate(ys, axis=1) * dec_ref[c, d, 0:T, :]
    st_ref[d] = dec_ref[c, d, T:T + 1, :] * st + upd_ref[c, d]


def _ssd_kernel(z_ref, xs_in, bs_in, cs_in, dtr_ref, cw_ref, cb_ref, dtb_ref, alog_ref, dsk_ref, ng_ref, h0_ref,
                *refs, seq, need_y):
    if need_y:
        o_ref, hT_ref, xs_ref, b_ref, c_ref, dt_ref, la_ref, upd_ref, dec_ref, st_ref, y_ref = refs
    else:
        hT_ref, xs_ref, b_ref, c_ref, dt_ref, la_ref, upd_ref, dec_ref, st_ref = refs
        y_ref = None
    nc = seq // SSD_CHUNK
    ne = 16
    edge = lax.broadcasted_iota(jnp.int32, (ne, 1), 0)
    for gi, (src, dst) in enumerate(((xs_in, xs_ref), (bs_in, b_ref), (cs_in, c_ref))):
        sl = slice(gi * W_SSD, (gi + 1) * W_SSD)
        x = src[...].astype(F32)
        prev, nxt = pltpu.roll(x, 1, 0), pltpu.roll(x, seq - 1, 0)

        def conv_silu(p, c, n):
            cv = p * cw_ref[0:1, sl] + c * cw_ref[1:2, sl] + n * cw_ref[2:3, sl] + cb_ref[:, sl]
            return (cv * jax.nn.sigmoid(cv)).astype(dst.dtype)

        dst[...] = conv_silu(prev, x, nxt)
        dst[0:ne, :] = conv_silu(jnp.where(edge == 0, 0.0, prev[0:ne]), x[0:ne], nxt[0:ne])
        dst[seq - ne:seq, :] = conv_silu(prev[seq - ne:seq], x[seq - ne:seq],
                                         jnp.where(edge == ne - 1, 0.0, nxt[seq - ne:seq]))
    dtv = dtr_ref[...] + dtb_ref[...]
    dt = jnp.maximum(dtv, 0.0) + jnp.log1p(jnp.exp(-jnp.abs(dtv)))
    dt_ref[...] = dt
    la_ref[...] = dt * (-jnp.exp(alog_ref[...]))
    st_ref[...] = h0_ref[0]

    def terms(c, carry):
        _ssd_chunk_terms(c, xs_ref, b_ref, c_ref, dt_ref, la_ref, dsk_ref, y_ref, upd_ref, dec_ref, need_y)
        return carry

    lax.fori_loop(0, nc, terms, 0, unroll=min(16, nc))

    def recur(i, carry):
        _ssd_chunk_state(i, 0, c_ref, y_ref, upd_ref, dec_ref, st_ref, need_y)
        _ssd_chunk_state(nc - 1 - i, 1, c_ref, y_ref, upd_ref, dec_ref, st_ref, need_y)
        return carry

    lax.fori_loop(0, nc, recur, 0, unroll=min(16, nc))
    hT_ref[0] = st_ref[...]
    if need_y:
        z = z_ref[...].astype(F32)
        yz = y_ref[...] * (z * jax.nn.sigmoid(z))
        ms = jnp.mean(yz * yz, axis=-1, keepdims=True)
        o_ref[...] = (yz * lax.rsqrt(ms + EPS) * ng_ref[...]).astype(o_ref.dtype)


def ssd_scan(pr, dt, h0, conv_w, conv_b, dt_bias, a_log, d_skip, norm_g, need_y):
    bn = h0.shape[0]
    seq = pr.shape[0] // bn
    nc = seq // SSD_CHUNK
    pad = LANES - 2 * SSD_HEADS
    dtb = jnp.pad(dt_bias.reshape(1, -1), ((0, 0), (0, pad)))
    alog = jnp.pad(a_log.reshape(1, -1), ((0, 0), (0, pad)))
    dsk = jnp.repeat(d_skip, SSD_HEAD_DIM)[None, :]
    blk = lambda col: pl.BlockSpec((seq, W_SSD), lambda b: (b, col))
    full = lambda a: pl.BlockSpec(a.shape, lambda b: (0,) * a.ndim)
    st_shape = (1, 2, SSD_STATE, W_SSD)
    st_spec = pl.BlockSpec(st_shape, lambda b: (b, 0, 0, 0))
    args = (conv_w, conv_b[None, :], dtb, alog, dsk, norm_g[None, :])
    out_shape = [jax.ShapeDtypeStruct((bn,) + st_shape[1:], F32)]
    out_specs = [st_spec]
    scratch = [pltpu.VMEM((seq, W_SSD), F32),
               pltpu.VMEM((seq, W_SSD), BF16), pltpu.VMEM((seq, W_SSD), BF16),
               pltpu.VMEM((seq, LANES), F32), pltpu.VMEM((seq, LANES), F32),
               pltpu.VMEM((nc, 2, SSD_STATE, W_SSD), F32),
               pltpu.VMEM((nc, 2, SSD_CHUNK + 8, W_SSD), F32),
               pltpu.VMEM(st_shape[1:], F32)]
    if need_y:
        out_shape.insert(0, jax.ShapeDtypeStruct((bn * seq, W_SSD), BF16))
        out_specs.insert(0, pl.BlockSpec((seq, W_SSD), lambda b: (b, 0)))
        scratch.append(pltpu.VMEM((seq, W_SSD), F32))
    res = pl.pallas_call(
        functools.partial(_ssd_kernel, seq=seq, need_y=need_y),
        out_shape=out_shape,
        grid=(bn,),
        in_specs=[blk(Z_COL), blk(XS_COL), blk(BS_COL), blk(CS_COL),
                  pl.BlockSpec((seq, LANES), lambda b: (b, 0))]
                 + [full(a) for a in args] + [st_spec],
        out_specs=out_specs,
        scratch_shapes=scratch,
        compiler_params=_params("parallel"),
        name="ssd_scan",
    )(pr, pr, pr, pr, dt, *args, h0)
    return (res[0], res[1]) if need_y else (None, res[0])


def ssd_mix(pr, dt, pr_c, dt_c, conv_w, conv_b, dt_bias, a_log, d_skip, norm_g, ctx_out, bn):
    h0 = jnp.zeros((bn, 2, SSD_STATE, W_SSD), F32)
    oc, hc = ssd_scan(pr_c, dt_c, h0, conv_w, conv_b, dt_bias, a_log, d_skip, norm_g, ctx_out)
    o, _ = ssd_scan(pr, dt, hc, conv_w, conv_b, dt_bias, a_log, d_skip, norm_g, True)
    return o, oc


def mixer(pr, dt, pr_c, dt_c, pool_w, pool_scale, conv_w, na_bias, s_conv_w, s_conv_b, dt_bias, a_log, d_skip,
          s_norm_g, ctx_out, bn):
    o_ssd, oc_ssd = ssd_mix(pr, dt, pr_c, dt_c, s_conv_w, s_conv_b, dt_bias, a_log, d_skip, s_norm_g, ctx_out, bn)
    o = [*local_mix(pr, pool_w, pool_scale, conv_w, bn), na_attention(pr, pr_c, na_bias, bn), o_ssd]
    if not ctx_out:
        return o, None
    oc = [*local_mix(pr_c, pool_w, pool_scale, conv_w, bn), ctx_attention(pr_c, bn), oc_ssd]
    return o, oc


TM = 512
TM_ROUTE = 1024
TM_PROJ = 1024
TM_GRP = 512


def kernel(x, c, ctx, c_ctx, w_ada, b_ada, g_mix, g_ffn, w_in, w_out, pool_w, pool_scale, conv_w, na_rpb,
           ssd_conv_w, ssd_conv_b, ssd_dt_bias, ssd_a_log, ssd_d, ssd_norm_g, ffn_w_gu, ffn_w_down,
           moe_router, moe_w_gu, moe_w_down, g_final):
    bn, S, d = x.shape
    Lc = ctx.shape[1]
    m, mc = bn * S, bn * Lc
    tpb = S // TM
    x2 = x.reshape(m, d)
    xc2 = ctx.reshape(mc, d)
    c_rows = jnp.concatenate([c, c_ctx[None, :], jnp.zeros((7, d), F32)], axis=0)
    for l in range(DEPTH):
        ctx_out = l < DEPTH - 1
        last = l == DEPTH - 1
        ada = ada_modulation(c_rows, w_ada[l].astype(BF16), b_ada[l][None, :])
        mod = ada[:bn].reshape(bn, 6, d)
        mod_c = ada[bn:bn + 1].reshape(1, 6, d)
        w_in_l = jnp.pad(w_in[l], ((0, 0), (0, D_IN_PAD - D_IN_PROJ))).astype(BF16)
        g_m = g_mix[l][None, :]
        g_f = g_ffn[l][None, :]
        pr, dt = in_proj(x2, g_m, mod, w_in_l, S // TM_PROJ, TM_PROJ)
        pr_c, dt_c = in_proj(xc2, g_m, mod_c, w_in_l, None, min(TM_PROJ, mc))
        na_bias = na_bias_table(na_rpb[l], S // GRID_W)
        o, oc = mixer(pr, dt, pr_c, dt_c, pool_w[l], pool_scale[l], conv_w[l], na_bias, ssd_conv_w[l],
                      ssd_conv_b[l], ssd_dt_bias[l], ssd_a_log[l], ssd_d[l], ssd_norm_g[l], ctx_out, bn)
        w_out_l = w_out[l].astype(BF16)
        j = l // 2
        if l % 2 == 0:
            w_gu = ffn_w_gu[j].astype(BF16)
            w_dn = ffn_w_down[j].astype(BF16)
            x2 = ffn_dense(x2, o, g_f, mod, w_out_l, w_gu, w_dn, tpb, TM)
            if ctx_out:
                xc2 = ffn_dense(xc2, oc, g_f, mod_c, w_out_l, w_gu, w_dn, None, min(TM, mc))
        else:
            w_r = jnp.pad(moe_router[j], ((0, 0), (0, LANES - N_EXPERTS))).astype(BF16)
            w_gu, w_dn = moe_w_gu[j], moe_w_down[j]
            x2 = moe_block(x2, o, g_f, mod, w_out_l, w_r, w_gu, w_dn, S // TM_ROUTE, TM_ROUTE, TM_GRP,
                           g_final[None, :] if last else None)
            if ctx_out:
                xc2 = moe_block(xc2, oc, g_f, mod_c, w_out_l, w_r, w_gu, w_dn, None, min(TM_ROUTE, mc), TM_GRP)
            if last:
                return x2.reshape(bn, S, d)
    return final_norm(x2, g_final[None, :], TM).reshape(bn, S, d)
```

```python
import functools
import math

import numpy as np
import jax
import jax.numpy as jnp
from jax import lax
from jax.experimental import pallas as pl
from jax.experimental.pallas import tpu as pltpu

DEPTH = 2
GRID_W = 64
EPS = 1e-6
W_POOL = 256
W_CONV = 256
W_NA = 256
W_SSD = 256
POOL_WINDOWS = (2, 4, 8, 16)
POOL_GROUP = W_POOL // len(POOL_WINDOWS)
NA_HEADS = 4
NA_HEAD_DIM = W_NA // NA_HEADS
WIN_R = 8
WIN_C = 16
SSD_HEAD_DIM = 64
SSD_HEADS = W_SSD // SSD_HEAD_DIM
SSD_GROUPS = 2
SSD_STATE = 128
SSD_CHUNK = 128
SSD_XBC = W_SSD + 2 * SSD_GROUPS * SSD_STATE
D_IN_PROJ = W_POOL + 3 * W_CONV + 3 * W_NA + W_SSD + SSD_XBC + 2 * SSD_HEADS
N_EXPERTS = 8
TOP_K = 2

LANES = 128
D_IN_PAD = -(-D_IN_PROJ // LANES) * LANES
VMEM_LIMIT = 56 * 1024 * 1024
BF16 = jnp.bfloat16
F32 = jnp.float32


def _params(*sem):
    return pltpu.CompilerParams(dimension_semantics=sem, vmem_limit_bytes=VMEM_LIMIT)


def _norm_mod(x, g, scale, shift):
    ms = jnp.mean(x * x, axis=-1, keepdims=True)
    return (x * lax.rsqrt(ms + EPS) * g) * (1.0 + scale) + shift


def _mod_map(tiles_per_batch):
    if tiles_per_batch is None:
        return lambda i, *_: (0, 0, 0)
    return lambda i, *_: (i // tiles_per_batch, 0, 0)


def _resident(shape, index_map):
    return pl.BlockSpec(shape, index_map, pipeline_mode=pl.Buffered(1))


def _ada_kernel(c_ref, w_ref, b_ref, o_ref):
    c = c_ref[...]
    s = c * jax.nn.sigmoid(c)
    o_ref[...] = jnp.dot(s.astype(BF16), w_ref[...], preferred_element_type=F32) + b_ref[...]


def ada_modulation(c_rows, w, b):
    r, d = c_rows.shape
    n = w.shape[1]
    tn = 1536
    return pl.pallas_call(
        _ada_kernel,
        out_shape=jax.ShapeDtypeStruct((r, n), F32),
        grid=(n // tn,),
        in_specs=[pl.BlockSpec((r, d), lambda j: (0, 0)),
                  pl.BlockSpec((d, tn), lambda j: (0, j)),
                  pl.BlockSpec((1, tn), lambda j: (0, j))],
        out_specs=pl.BlockSpec((r, tn), lambda j: (0, j)),
        compiler_params=_params("arbitrary"),
        name="ada_modulation",
    )(c_rows, w, b)


PROJ_ROWS = 512


def _in_proj_kernel(x_ref, g_ref, mod_ref, w_ref, o_ref, dt_ref):
    tm = x_ref.shape[0]
    parts = [slice(r0, min(r0 + PROJ_ROWS, tm)) for r0 in range(0, tm, PROJ_ROWS)]
    hs = [_norm_mod(x_ref[rs, :], g_ref[...], mod_ref[0, 1:2, :], mod_ref[0, 0:1, :]).astype(BF16) for rs in parts]
    for rs, h in zip(parts, hs):
        pr = jnp.dot(h, w_ref[...], preferred_element_type=F32)
        o_ref[rs, :] = pr.astype(o_ref.dtype)
        dt_ref[rs, :] = pr[:, DT_COL * LANES:(DT_COL + 1) * LANES]


def in_proj(x2, g, mod, w, tiles_per_batch, tm):
    m, d = x2.shape
    n = w.shape[1]
    return pl.pallas_call(
        _in_proj_kernel,
        out_shape=(jax.ShapeDtypeStruct((m, n), BF16), jax.ShapeDtypeStruct((m, LANES), F32)),
        grid=(m // tm,),
        in_specs=[pl.BlockSpec((tm, d), lambda i: (i, 0)),
                  pl.BlockSpec((1, d), lambda i: (0, 0)),
                  pl.BlockSpec((1, 6, d), _mod_map(tiles_per_batch)),
                  _resident((d, n), lambda i: (0, 0))],
        out_specs=(pl.BlockSpec((tm, n), lambda i: (i, 0)), pl.BlockSpec((tm, LANES), lambda i: (i, 0))),
        compiler_params=_params("parallel"),
        name="in_proj",
    )(x2, g, mod, w)


def _mix_residual(x, mod_ref, w_ref, part_refs, rows=slice(None)):
    y, k0 = None, 0
    for p_ref in part_refs:
        k = p_ref.shape[1]
        t = jnp.dot(p_ref[rows, :], w_ref[k0:k0 + k, :], preferred_element_type=F32)
        y = t if y is None else y + t
        k0 += k
    return x + mod_ref[0, 2:3, :] * y


def _part_specs(parts, tm):
    return [pl.BlockSpec((tm, p.shape[1]), lambda i, *_: (i, 0)) for p in parts]


FF_CHUNK = 512


def _swiglu(h, wgu, wd, ff):
    acc = None
    for c0 in range(0, ff, FF_CHUNK):
        c1 = min(c0 + FF_CHUNK, ff)
        gg = jnp.dot(h, wgu(c0, c1), preferred_element_type=F32)
        uu = jnp.dot(h, wgu(ff + c0, ff + c1), preferred_element_type=F32)
        a = (gg * jax.nn.sigmoid(gg) * uu).astype(BF16)
        part = jnp.dot(a, wd(c0, c1), preferred_element_type=F32)
        acc = part if acc is None else acc + part
    return acc


def _ffn_kernel(x_ref, g_ref, mod_ref, wout_ref, wgu_ref, wd_ref, *refs):
    x = _mix_residual(x_ref[...], mod_ref, wout_ref, refs[:-1])
    y_ref = refs[-1]
    h = _norm_mod(x, g_ref[...], mod_ref[0, 4:5, :], mod_ref[0, 3:4, :]).astype(BF16)
    y = _swiglu(h, lambda a, b: wgu_ref[:, a:b], lambda a, b: wd_ref[a:b, :], wd_ref.shape[0])
    y_ref[...] = x + mod_ref[0, 5:6, :] * y


def ffn_dense(x2, parts, g, mod, w_out, w_gu, w_down, tiles_per_batch, tm):
    m, d = x2.shape
    return pl.pallas_call(
        _ffn_kernel,
        out_shape=jax.ShapeDtypeStruct((m, d), F32),
        grid=(m // tm,),
        in_specs=[pl.BlockSpec((tm, d), lambda i: (i, 0)),
                  pl.BlockSpec((1, d), lambda i: (0, 0)),
                  pl.BlockSpec((1, 6, d), _mod_map(tiles_per_batch)),
                  _resident(w_out.shape, lambda i: (0, 0)),
                  _resident(w_gu.shape, lambda i: (0, 0)),
                  _resident(w_down.shape, lambda i: (0, 0))] + _part_specs(parts, tm),
        out_specs=pl.BlockSpec((tm, d), lambda i: (i, 0)),
        compiler_params=_params("parallel"),
        name="ffn_dense",
    )(x2, g, mod, w_out, w_gu, w_down, *parts)


ROUTE_E1, ROUTE_E2, ROUTE_R1, ROUTE_R2, ROUTE_G1, ROUTE_G2 = range(6)
ROUTE_ROWS = 8
ROUTE_PASS_ROWS = 512


def _router_kernel(x_ref, g_ref, mod_ref, wout_ref, wr_ref, *refs):
    part_refs = refs[:-6]
    x1_ref, h_ref, route_ref, route_t_ref, cnt_ref, base_ref = refs[-6:]

    @pl.when(pl.program_id(0) == 0)
    def _():
        base_ref[...] = jnp.zeros_like(base_ref)

    base = base_ref[...]
    for r0 in range(0, x_ref.shape[0], ROUTE_PASS_ROWS):
        rs = slice(r0, min(r0 + ROUTE_PASS_ROWS, x_ref.shape[0]))
        x = _mix_residual(x_ref[rs, :], mod_ref, wout_ref, part_refs, rs)
        x1_ref[rs, :] = x
        h = _norm_mod(x, g_ref[...], mod_ref[0, 4:5, :], mod_ref[0, 3:4, :]).astype(BF16)
        h_ref[rs, :] = h
        tm = h.shape[0]
        logits = jnp.dot(h, wr_ref[...], preferred_element_type=F32)
        lane = lax.broadcasted_iota(jnp.int32, logits.shape, 1)
        neg = jnp.float32(-jnp.inf)
        logits = jnp.where(lane < N_EXPERTS, logits, neg)
        m1 = jnp.max(logits, axis=-1, keepdims=True)
        i1 = jnp.min(jnp.where(logits == m1, lane, LANES), axis=-1, keepdims=True)
        rest = jnp.where(lane == i1, neg, logits)
        m2 = jnp.max(rest, axis=-1, keepdims=True)
        i2 = jnp.min(jnp.where(rest == m2, lane, LANES), axis=-1, keepdims=True)
        e2 = jnp.exp(m2 - m1)
        g1 = 1.0 / (1.0 + e2)
        g2 = e2 / (1.0 + e2)
        sel1, sel2 = lane == i1, lane == i2
        sel = jnp.where(sel1 | sel2, 1.0, 0.0)
        li = lax.broadcasted_iota(jnp.int32, (tm, tm), 0)
        si = lax.broadcasted_iota(jnp.int32, (tm, tm), 1)
        before = jnp.where(si < li, 1.0, 0.0).astype(BF16)
        rank = jnp.dot(before, sel.astype(BF16), preferred_element_type=F32) + base
        r1 = jnp.sum(jnp.where(sel1, rank, 0.0), axis=-1, keepdims=True)
        r2 = jnp.sum(jnp.where(sel2, rank, 0.0), axis=-1, keepdims=True)
        base = base + jnp.sum(sel, axis=0, keepdims=True)
        fields = (i1.astype(F32), i2.astype(F32), r1, r2, g1, g2)
        route = jnp.zeros(logits.shape, F32)
        for k, v in enumerate(fields):
            route = jnp.where(lane == k, v, route)
        route_ref[rs, :] = route
        route_t_ref[:, rs] = route.T[:ROUTE_ROWS, :]
    base_ref[...] = base
    cnt_ref[...] = base


def moe_router(x2, parts, g, mod, w_out, w_router, tiles_per_batch, tm):
    m, d = x2.shape
    return pl.pallas_call(
        _router_kernel,
        out_shape=(jax.ShapeDtypeStruct((m, d), F32), jax.ShapeDtypeStruct((m, d), BF16),
                   jax.ShapeDtypeStruct((m, LANES), F32), jax.ShapeDtypeStruct((ROUTE_ROWS, m), F32),
                   jax.ShapeDtypeStruct((1, LANES), F32)),
        grid=(m // tm,),
        in_specs=[pl.BlockSpec((tm, d), lambda i: (i, 0)),
                  pl.BlockSpec((1, d), lambda i: (0, 0)),
                  pl.BlockSpec((1, 6, d), _mod_map(tiles_per_batch)),
                  _resident(w_out.shape, lambda i: (0, 0)),
                  pl.BlockSpec((d, LANES), lambda i: (0, 0))] + _part_specs(parts, tm),
        out_specs=(pl.BlockSpec((tm, d), lambda i: (i, 0)),
                   pl.BlockSpec((tm, d), lambda i: (i, 0)),
                   pl.BlockSpec((tm, LANES), lambda i: (i, 0)),
                   pl.BlockSpec((ROUTE_ROWS, tm), lambda i: (0, i)),
                   pl.BlockSpec((1, LANES), lambda i: (0, 0))),
        scratch_shapes=[pltpu.VMEM((1, LANES), F32)],
        compiler_params=_params("arbitrary"),
        name="moe_router",
    )(x2, g, mod, w_out, w_router, *parts)


def _moe_kernel(tile_ref, exp_ref, start_ref, end_ref, xg_ref, wgu_ref, wd_ref, y_ref, wgu_b, wd_b):
    w = pl.program_id(0)
    tm = xg_ref.shape[0]
    start, end = start_ref[w], end_ref[w]
    t0 = tile_ref[w] * tm

    @pl.when((w == 0) | (exp_ref[w] != exp_ref[jnp.maximum(w - 1, 0)]))
    def _():
        wgu_b[...] = wgu_ref[0].astype(BF16)
        wd_b[...] = wd_ref[0].astype(BF16)

    @pl.when(end > start)
    def _():
        y = _swiglu(xg_ref[...], lambda a, b: wgu_b[:, a:b], lambda a, b: wd_b[a:b, :], wd_b.shape[0])
        y = y.astype(y_ref.dtype)

        @pl.when(start == t0)
        def _():
            y_ref[...] = y

        @pl.when(start != t0)
        def _():
            row = t0 + lax.broadcasted_iota(jnp.int32, (tm, 1), 0)
            y_ref[...] = jnp.where(row >= start, y, y_ref[...])


def moe_grouped(item_tile, item_expert, item_start, item_end, xg, w_gu, w_down, tm):
    p, d = xg.shape
    grid_spec = pltpu.PrefetchScalarGridSpec(
        num_scalar_prefetch=4,
        grid=(item_tile.shape[0],),
        in_specs=[pl.BlockSpec((tm, d), lambda w, it, ie, s, e: (it[w], 0)),
                  pl.BlockSpec((1,) + w_gu.shape[1:], lambda w, it, ie, s, e: (ie[w], 0, 0)),
                  pl.BlockSpec((1,) + w_down.shape[1:], lambda w, it, ie, s, e: (ie[w], 0, 0))],
        out_specs=pl.BlockSpec((tm, d), lambda w, it, ie, s, e: (it[w], 0)),
        scratch_shapes=[pltpu.VMEM(w_gu.shape[1:], BF16), pltpu.VMEM(w_down.shape[1:], BF16)],
    )
    return pl.pallas_call(
        _moe_kernel,
        out_shape=jax.ShapeDtypeStruct((p, d), BF16),
        grid_spec=grid_spec,
        compiler_params=_params("arbitrary"),
        name="moe_grouped",
    )(item_tile, item_expert, item_start, item_end, xg, w_gu, w_down)


def _moe_combine_kernel(x_ref, ya_ref, yb_ref, route_ref, mod_ref, *refs):
    o_ref = refs[-1]
    r = route_ref[...]
    y = (r[:, ROUTE_G1:ROUTE_G1 + 1] * ya_ref[...].astype(F32) + r[:, ROUTE_G2:ROUTE_G2 + 1] * yb_ref[...].astype(F32))
    x = x_ref[...] + mod_ref[0, 5:6, :] * y
    if len(refs) == 2:
        ms = jnp.mean(x * x, axis=-1, keepdims=True)
        x = x * lax.rsqrt(ms + EPS) * refs[0][...]
    o_ref[...] = x


def moe_combine(x2, ya, yb, route, mod, tiles_per_batch, tm, g_final=None):
    m, d = x2.shape
    row = pl.BlockSpec((tm, d), lambda i: (i, 0))
    specs = [row, row, row, pl.BlockSpec((tm, LANES), lambda i: (i, 0)),
             pl.BlockSpec((1, 6, d), _mod_map(tiles_per_batch))]
    args = [x2, ya, yb, route, mod]
    if g_final is not None:
        specs.append(pl.BlockSpec((1, d), lambda i: (0, 0)))
        args.append(g_final)
    return pl.pallas_call(
        _moe_combine_kernel,
        out_shape=jax.ShapeDtypeStruct((m, d), F32),
        grid=(m // tm,),
        in_specs=specs,
        out_specs=row,
        compiler_params=_params("parallel"),
        name="moe_combine",
    )(*args)


def moe_block(x2, parts, g, mod, w_out, w_router, w_gu, w_down, tiles_per_batch, tm_tok, tm_grp, g_final=None):
    m, d = x2.shape
    x2, h, route, route_t, cnt = moe_router(x2, parts, g, mod, w_out, w_router, tiles_per_batch, tm_tok)
    cnt = cnt[0, :N_EXPERTS].astype(jnp.int32)
    p = m * TOP_K
    off = jnp.cumsum(cnt) - cnt
    field = lambda k: route_t[k].astype(jnp.int32)

    def row_of(e, r):
        for k in range(N_EXPERTS):
            r = r + jnp.where(e == k, off[k], 0)
        return r

    d1 = row_of(field(ROUTE_E1), field(ROUTE_R1))
    d2 = row_of(field(ROUTE_E2), field(ROUTE_R2))
    tok = jnp.arange(m, dtype=jnp.int32)
    _, src = lax.sort_key_val(jnp.concatenate([d1, d2]), jnp.concatenate([tok, tok]))
    n_row_tiles = p // tm_grp
    start = jnp.sort(jnp.concatenate([jnp.arange(n_row_tiles, dtype=jnp.int32) * tm_grp, off]))
    end = jnp.concatenate([start[1:], jnp.full((1,), p, jnp.int32)])
    item_tile = jnp.minimum(start // tm_grp, n_row_tiles - 1)
    item_expert = jnp.sum((off[None, :] <= start[:, None]).astype(jnp.int32), axis=1) - 1
    rows = lambda a, idx: a.at[idx].get(mode="promise_in_bounds")
    yg = moe_grouped(item_tile, item_expert, start, end, rows(h, src), w_gu, w_down, tm_grp)
    return moe_combine(x2, rows(yg, d1), rows(yg, d2), route, mod, tiles_per_batch, tm_tok, g_final)


def _final_norm_kernel(x_ref, g_ref, o_ref):
    x = x_ref[...]
    ms = jnp.mean(x * x, axis=-1, keepdims=True)
    o_ref[...] = x * lax.rsqrt(ms + EPS) * g_ref[...]


def final_norm(x2, g, tm):
    m, d = x2.shape
    return pl.pallas_call(
        _final_norm_kernel,
        out_shape=jax.ShapeDtypeStruct((m, d), F32),
        grid=(m // tm,),
        in_specs=[pl.BlockSpec((tm, d), lambda i: (i, 0)), pl.BlockSpec((1, d), lambda i: (0, 0))],
        out_specs=pl.BlockSpec((tm, d), lambda i: (i, 0)),
        compiler_params=_params("parallel"),
        name="final_norm",
    )(x2, g)


NA_QROWS = 4
NA_KROWS = NA_QROWS + WIN_R
NA_STEP_BLOCKS = 8
Q_COL, K_COL, V_COL = 4, 5, 6


def _na_key_start(j, rows):
    return np.clip(j * NA_QROWS - WIN_R // 2, 0, rows - NA_KROWS)


def _na_bias_index(rows):
    nblk = rows // NA_QROWS
    pats = []
    for j in range(nblk):
        start = _na_key_start(j, rows)
        r = j * NA_QROWS + np.arange(NA_QROWS)
        sr = np.clip(r - WIN_R // 2, 0, rows - WIN_R)
        kr = start + np.arange(NA_KROWS)
        rvalid = (kr[None, :] >= sr[:, None]) & (kr[None, :] < sr[:, None] + WIN_R)
        ri = np.clip(kr[None, :] - r[:, None] + WIN_R - 1, 0, 2 * WIN_R - 2)
        pats.append((ri, rvalid))
    for j in range(2, nblk - 1):
        assert all(np.array_equal(a, b) for a, b in zip(pats[1], pats[j]))
    sel = [pats[0], pats[1], pats[nblk - 1]]
    ri = np.stack([p[0] for p in sel])
    rvalid = np.stack([p[1] for p in sel])
    c = np.arange(GRID_W)
    sc = np.clip(c - WIN_C // 2, 0, GRID_W - WIN_C)
    cvalid = (c[None, :] >= sc[:, None]) & (c[None, :] < sc[:, None] + WIN_C)
    ci = np.clip(c[None, :] - c[:, None] + WIN_C - 1, 0, 2 * WIN_C - 2)
    c_onehot = (ci[..., None] == np.arange(2 * WIN_C - 1)).astype(np.float32)
    valid = rvalid[:, :, None, :, None] & cvalid[None, None, :, None, :]
    return ri, c_onehot, valid


def na_bias_table(rpb, rows):
    ri, c_onehot, valid = _na_bias_index(rows)
    toep = jnp.einsum('hrd,qkd->hrqk', rpb, c_onehot, precision=lax.Precision.HIGHEST)
    b = jnp.take(toep, ri.reshape(-1), axis=1).reshape((NA_HEADS,) + ri.shape + (GRID_W, GRID_W))
    b = jnp.transpose(b, (1, 0, 2, 4, 3, 5))
    b = jnp.where(valid[:, None], b, -jnp.inf)
    return b.reshape(3, NA_HEADS, NA_QROWS * GRID_W, NA_KROWS * GRID_W).astype(F32)


def _attend_heads(q, key_sets, bias_fn):
    n, w = q.shape
    lane = lax.broadcasted_iota(jnp.int32, (1, w), 1)
    head_masks = [(lane >= h * NA_HEAD_DIM) & (lane < (h + 1) * NA_HEAD_DIM) for h in range(NA_HEADS)]
    all_scores = []
    for h, mh in enumerate(head_masks):
        qh = jnp.where(mh, q, 0.0).astype(BF16)
        scores = []
        for i, (k, _) in enumerate(key_sets):
            s = lax.dot_general(qh, k, (((1,), (1,)), ((), ())), preferred_element_type=F32)
            b = bias_fn(i, h)
            scores.append(s if b is None else s + b)
        all_scores.append(scores)
    all_probs = []
    for scores in all_scores:
        m = scores[0].max(axis=-1, keepdims=True)
        for s in scores[1:]:
            m = jnp.maximum(m, s.max(axis=-1, keepdims=True))
        denom = jnp.zeros((n, 1), F32)
        probs = []
        for s in scores:
            p = jnp.exp(s - m)
            denom = denom + p.sum(axis=-1, keepdims=True)
            probs.append(p.astype(BF16))
        all_probs.append((probs, denom))
    out = jnp.zeros((n, w), F32)
    for mh, (probs, denom) in zip(head_masks, all_probs):
        acc = jnp.zeros((n, w), F32)
        for p, (_, v) in zip(probs, key_sets):
            acc = acc + jnp.dot(p, v, preferred_element_type=F32)
        out = out + jnp.where(mh, acc / denom, 0.0)
    return out


def _na_kernel(q_ref, k_ref, v_ref, kc_ref, vc_ref, bias_ref, o_ref, *, rows):
    nblk = rows // NA_QROWS
    nq, nk = NA_QROWS * GRID_W, NA_KROWS * GRID_W
    for s in range(NA_STEP_BLOCKS):
        j = pl.program_id(1) * NA_STEP_BLOCKS + s
        start = jnp.clip(j * NA_QROWS - WIN_R // 2, 0, rows - NA_KROWS)
        t0 = pl.multiple_of(start * GRID_W, GRID_W)
        pattern = jnp.where(j == 0, 0, jnp.where(j == nblk - 1, 2, 1))
        kw = k_ref[pl.ds(t0, nk), :]
        vw = v_ref[pl.ds(t0, nk), :]
        q = q_ref[s * nq:(s + 1) * nq, :] * (1.0 / math.sqrt(NA_HEAD_DIM))
        o = _attend_heads(q, [(kw, vw), (kc_ref[...], vc_ref[...])],
                          lambda i, h: bias_ref[pattern, h] if i == 0 else None)
        o_ref[s * nq:(s + 1) * nq, :] = o.astype(o_ref.dtype)


def na_attention(pr, pr_c, bias, bn):
    S, Lc = pr.shape[0] // bn, pr_c.shape[0] // bn
    rows = S // GRID_W
    nsteps = rows // NA_QROWS // NA_STEP_BLOCKS
    nq = NA_STEP_BLOCKS * NA_QROWS * GRID_W
    return pl.pallas_call(
        functools.partial(_na_kernel, rows=rows),
        out_shape=jax.ShapeDtypeStruct((bn * S, W_NA), BF16),
        grid=(bn, nsteps),
        in_specs=[pl.BlockSpec((nq, W_NA), lambda b, j: (b * nsteps + j, Q_COL)),
                  pl.BlockSpec((S, W_NA), lambda b, j: (b, K_COL)),
                  pl.BlockSpec((S, W_NA), lambda b, j: (b, V_COL)),
                  pl.BlockSpec((Lc, W_NA), lambda b, j: (b, K_COL)),
                  pl.BlockSpec((Lc, W_NA), lambda b, j: (b, V_COL)),
                  _resident(bias.shape, lambda b, j: (0, 0, 0, 0))],
        out_specs=pl.BlockSpec((nq, W_NA), lambda b, j: (b * nsteps + j, 0)),
        compiler_params=_params("parallel", "arbitrary"),
        name="na_attention",
    )(pr, pr, pr, pr_c, pr_c, bias)


def _ctx_attn_kernel(q_ref, k_ref, v_ref, o_ref):
    q = q_ref[...] * (1.0 / math.sqrt(NA_HEAD_DIM))
    o = _attend_heads(q, [(k_ref[...], v_ref[...])], lambda i, h: None)
    o_ref[...] = o.astype(o_ref.dtype)


def ctx_attention(pr_c, bn):
    Lc = pr_c.shape[0] // bn
    return pl.pallas_call(
        _ctx_attn_kernel,
        out_shape=jax.ShapeDtypeStruct((bn * Lc, W_NA), BF16),
        grid=(bn,),
        in_specs=[pl.BlockSpec((Lc, W_NA), lambda b: (b, Q_COL)),
                  pl.BlockSpec((Lc, W_NA), lambda b: (b, K_COL)),
                  pl.BlockSpec((Lc, W_NA), lambda b: (b, V_COL))],
        out_specs=pl.BlockSpec((Lc, W_NA), lambda b: (b, 0)),
        compiler_params=_params("parallel"),
        name="ctx_attention",
    )(pr_c, pr_c, pr_c)


POOL_COL, CONV_H_COL, CONV_B_COL, CONV_C_COL = 0, 1, 2, 3


def _shift_rows(x, k, row):
    n = x.shape[0]
    y = pltpu.roll(x, k % n, 0)
    return jnp.where(row < k, 0.0, y) if k > 0 else jnp.where(row >= n + k, 0.0, y)


def _local_mix_kernel(u_ref, h_ref, bg_ref, cg_ref, pw_ref, ps_ref, cw_ref, op_ref, oc_ref, *, seq):
    row = lax.broadcasted_iota(jnp.int32, (seq, 1), 0)
    lane = lax.broadcasted_iota(jnp.int32, (1, W_POOL), 1)
    u = u_ref[...].astype(F32)
    trailing, leading = {1: u}, {1: u}
    for w in (1, 2, 4):
        trailing[2 * w] = trailing[w] + _shift_rows(trailing[w], w, row)
        leading[2 * w] = leading[w] + _shift_rows(leading[w], -w, row)
    rowf = row.astype(F32)
    win_sum = jnp.zeros_like(u)
    cnt = jnp.zeros_like(u)
    for g, win in enumerate(POOL_WINDOWS):
        half = win // 2
        mg = (lane >= g * POOL_GROUP) & (lane < (g + 1) * POOL_GROUP)
        s = _shift_rows(trailing[half], 1, row) + leading[half]
        n = jnp.minimum(rowf + half, float(seq)) - jnp.maximum(rowf - half, 0.0)
        win_sum = jnp.where(mg, s, win_sum)
        cnt = jnp.where(mg, n, cnt)
    p = win_sum / cnt - u
    pooled = jnp.dot(p.astype(BF16), pw_ref[...], preferred_element_type=F32) * ps_ref[...]
    op_ref[...] = pooled.astype(op_ref.dtype)
    v = cg_ref[...].astype(F32) * h_ref[...].astype(F32)
    conv = (_shift_rows(v, 1, row) * cw_ref[0:1, :] + v * cw_ref[1:2, :] + _shift_rows(v, -1, row) * cw_ref[2:3, :])
    oc_ref[...] = (bg_ref[...].astype(F32) * conv).astype(oc_ref.dtype)


def local_mix(pr, pool_w, pool_scale, conv_w, bn):
    seq = pr.shape[0] // bn
    pw = jax.scipy.linalg.block_diag(*[pool_w[g] for g in range(len(POOL_WINDOWS))]).astype(BF16)
    blk = lambda col: pl.BlockSpec((seq, W_POOL), lambda b: (b, col))
    full = lambda a: pl.BlockSpec(a.shape, lambda b: (0,) * a.ndim)
    args = (pw, pool_scale[None, :], conv_w)
    return pl.pallas_call(
        functools.partial(_local_mix_kernel, seq=seq),
        out_shape=(jax.ShapeDtypeStruct((bn * seq, W_POOL), BF16), jax.ShapeDtypeStruct((bn * seq, W_CONV), BF16)),
        grid=(bn,),
        in_specs=[blk(POOL_COL), blk(CONV_H_COL), blk(CONV_B_COL), blk(CONV_C_COL)] + [full(a) for a in args],
        out_specs=(pl.BlockSpec((seq, W_POOL), lambda b: (b, 0)),
                   pl.BlockSpec((seq, W_CONV), lambda b: (b, 0))),
        compiler_params=_params("parallel"),
        name="local_mix",
    )(pr, pr, pr, pr, *args)


Z_COL, XS_COL, BS_COL, CS_COL = 7, 8, 9, 10
DT_COL = 22
HEADS_PER_GROUP = SSD_HEADS // SSD_GROUPS
GROUP_W = HEADS_PER_GROUP * SSD_HEAD_DIM


def _head_mask(h):
    lane = lax.broadcasted_iota(jnp.int32, (1, W_SSD), 1)
    return (lane >= h * SSD_HEAD_DIM) & (lane < (h + 1) * SSD_HEAD_DIM)


def _expand_heads(v, d):
    out = jnp.zeros((v.shape[0], W_SSD), F32)
    for h in range(SSD_HEADS):
        k = d * SSD_HEADS + h
        out = jnp.where(_head_mask(h), v[:, k:k + 1], out)
    return out


def _ssd_chunk_terms(c, xs_ref, b_ref, c_ref, dt_ref, la_ref, dsk_ref, y_ref, upd_ref, dec_ref, need_y):
    T = SSD_CHUNK
    r0 = pl.multiple_of(c * T, T)
    xs = xs_ref[pl.ds(r0, T), :]
    bm = b_ref[pl.ds(r0, T), :]
    cm = c_ref[pl.ds(r0, T), :]
    dt = dt_ref[pl.ds(r0, T), :]
    la = la_ref[pl.ds(r0, T), :]
    li = lax.broadcasted_iota(jnp.int32, (T, T), 0)
    si = lax.broadcasted_iota(jnp.int32, (T, T), 1)
    causal = si <= li
    prefix = jnp.dot(causal.astype(F32), la, precision=lax.Precision.HIGHEST, preferred_element_type=F32)
    total = prefix[T - 1:T, :]
    lane = lax.broadcasted_iota(jnp.int32, (1, LANES), 1)
    acum = jnp.where(lane < SSD_HEADS, prefix, total - prefix + la)
    bt = bm.astype(F32).T.astype(BF16)
    if need_y:
        acum_t = acum.T
        cb = [lax.dot_general(cm[:, g * SSD_STATE:(g + 1) * SSD_STATE], bm[:, g * SSD_STATE:(g + 1) * SSD_STATE],
                              (((1,), (1,)), ((), ())), preferred_element_type=F32) for g in range(SSD_GROUPS)]
        y = jnp.zeros((T, W_SSD), F32)
    per_dir = []
    for d in range(2):
        acum_e = _expand_heads(acum, d)
        tot_e = _expand_heads(total, d)
        xdt = xs * _expand_heads(dt, d)
        xw = (xdt * jnp.exp(tot_e - acum_e)).astype(BF16)
        dec_ref[c, d, 0:T, :] = jnp.exp(acum_e)
        dec_ref[c, d, T:T + 1, :] = jnp.exp(tot_e)
        scores = []
        if need_y:
            mask = causal if d == 0 else (si >= li)
            for h in range(SSD_HEADS):
                k = d * SSD_HEADS + h
                decay = jnp.exp(jnp.where(mask, acum[:, k:k + 1] - acum_t[k:k + 1, :], -jnp.inf))
                scores.append((cb[h // HEADS_PER_GROUP] * decay).astype(BF16))
        per_dir.append((xw, xdt, scores))
    for d, (xw, xdt, scores) in enumerate(per_dir):
        upd = [jnp.dot(bt[g * SSD_STATE:(g + 1) * SSD_STATE, :], xw[:, g * GROUP_W:(g + 1) * GROUP_W],
                       preferred_element_type=F32) for g in range(SSD_GROUPS)]
        upd_ref[c, d] = jnp.concatenate(upd, axis=1)
        if need_y:
            x_heads = [jnp.where(_head_mask(h), xdt, 0.0).astype(BF16) for h in range(SSD_HEADS)]
            y = y + jnp.dot(jnp.concatenate(scores, axis=1), jnp.concatenate(x_heads, axis=0),
                            preferred_element_type=F32)
    if need_y:
        y_ref[pl.ds(r0, T), :] = xs * dsk_ref[...] + y


def _ssd_chunk_state(c, d, c_ref, y_ref, upd_ref, dec_ref, st_ref, need_y):
    T = SSD_CHUNK
    r0 = pl.multiple_of(c * T, T)
    st = st_ref[d]
    if need_y:
        cm = c_ref[pl.ds(r0, T), :]
        st_b = st.astype(BF16)
        ys = [jnp.dot(cm[:, g * SSD_STATE:(g + 1) * SSD_STATE], st_b[:, g * GROUP_W:(g + 1) * GROUP_W],
                      preferred_element_type=F32) for g in range(SSD_GROUPS)]
        y_ref[pl.ds(r0, T), :] += jnp.concatenate(ys, axis=1) * dec_ref[c, d, 0:T, :]
    st_ref[d] = dec_ref[c, d, T:T + 1, :] * st + upd_ref[c, d]


def _ssd_kernel(z_ref, xs_in, bs_in, cs_in, dtr_ref, cw_ref, cb_ref, dtb_ref, alog_ref, dsk_ref, ng_ref, h0_ref,
                *refs, seq, need_y):
    if need_y:
        o_ref, hT_ref, xs_ref, b_ref, c_ref, dt_ref, la_ref, upd_ref, dec_ref, st_ref, y_ref = refs
    else:
        hT_ref, xs_ref, b_ref, c_ref, dt_ref, la_ref, upd_ref, dec_ref, st_ref = refs
        y_ref = None
    nc = seq // SSD_CHUNK
    row = lax.broadcasted_iota(jnp.int32, (seq, 1), 0)
    for gi, (src, dst) in enumerate(((xs_in, xs_ref), (bs_in, b_ref), (cs_in, c_ref))):
        sl = slice(gi * W_SSD, (gi + 1) * W_SSD)
        x = src[...].astype(F32)
        cv = (_shift_rows(x, 1, row) * cw_ref[0:1, sl] + x * cw_ref[1:2, sl]
              + _shift_rows(x, -1, row) * cw_ref[2:3, sl] + cb_ref[:, sl])
        dst[...] = (cv * jax.nn.sigmoid(cv)).astype(dst.dtype)
    dtv = dtr_ref[...] + dtb_ref[...]
    dt = jnp.maximum(dtv, 0.0) + jnp.log1p(jnp.exp(-jnp.abs(dtv)))
    dt_ref[...] = dt
    la_ref[...] = dt * (-jnp.exp(alog_ref[...]))
    st_ref[...] = h0_ref[0]

    def terms(c, carry):
        _ssd_chunk_terms(c, xs_ref, b_ref, c_ref, dt_ref, la_ref, dsk_ref, y_ref, upd_ref, dec_ref, need_y)
        return carry

    lax.fori_loop(0, nc, terms, 0, unroll=min(16, nc))

    def recur(i, carry):
        _ssd_chunk_state(i, 0, c_ref, y_ref, upd_ref, dec_ref, st_ref, need_y)
        _ssd_chunk_state(nc - 1 - i, 1, c_ref, y_ref, upd_ref, dec_ref, st_ref, need_y)
        return carry

    lax.fori_loop(0, nc, recur, 0, unroll=min(16, nc))
    hT_ref[0] = st_ref[...]
    if need_y:
        z = z_ref[...].astype(F32)
        yz = y_ref[...] * (z * jax.nn.sigmoid(z))
        ms = jnp.mean(yz * yz, axis=-1, keepdims=True)
        o_ref[...] = (yz * lax.rsqrt(ms + EPS) * ng_ref[...]).astype(o_ref.dtype)


def ssd_scan(pr, dt, h0, conv_w, conv_b, dt_bias, a_log, d_skip, norm_g, need_y):
    bn = h0.shape[0]
    seq = pr.shape[0] // bn
    nc = seq // SSD_CHUNK
    pad = LANES - 2 * SSD_HEADS
    dtb = jnp.pad(dt_bias.reshape(1, -1), ((0, 0), (0, pad)))
    alog = jnp.pad(a_log.reshape(1, -1), ((0, 0), (0, pad)))
    dsk = jnp.repeat(d_skip, SSD_HEAD_DIM)[None, :]
    blk = lambda col: pl.BlockSpec((seq, W_SSD), lambda b: (b, col))
    full = lambda a: pl.BlockSpec(a.shape, lambda b: (0,) * a.ndim)
    st_shape = (1, 2, SSD_STATE, W_SSD)
    st_spec = pl.BlockSpec(st_shape, lambda b: (b, 0, 0, 0))
    args = (conv_w, conv_b[None, :], dtb, alog, dsk, norm_g[None, :])
    out_shape = [jax.ShapeDtypeStruct((bn,) + st_shape[1:], F32)]
    out_specs = [st_spec]
    scratch = [pltpu.VMEM((seq, W_SSD), F32),
               pltpu.VMEM((seq, W_SSD), BF16), pltpu.VMEM((seq, W_SSD), BF16),
               pltpu.VMEM((seq, LANES), F32), pltpu.VMEM((seq, LANES), F32),
               pltpu.VMEM((nc, 2, SSD_STATE, W_SSD), F32),
               pltpu.VMEM((nc, 2, SSD_CHUNK + 8, W_SSD), F32),
               pltpu.VMEM(st_shape[1:], F32)]
    if need_y:
        out_shape.insert(0, jax.ShapeDtypeStruct((bn * seq, W_SSD), BF16))
        out_specs.insert(0, pl.BlockSpec((seq, W_SSD), lambda b: (b, 0)))
        scratch.append(pltpu.VMEM((seq, W_SSD), F32))
    res = pl.pallas_call(
        functools.partial(_ssd_kernel, seq=seq, need_y=need_y),
        out_shape=out_shape,
        grid=(bn,),
        in_specs=[blk(Z_COL), blk(XS_COL), blk(BS_COL), blk(CS_COL),
                  pl.BlockSpec((seq, LANES), lambda b: (b, 0))]
                 + [full(a) for a in args] + [st_spec],
        out_specs=out_specs,
        scratch_shapes=scratch,
        compiler_params=_params("parallel"),
        name="ssd_scan",
    )(pr, pr, pr, pr, dt, *args, h0)
    return (res[0], res[1]) if need_y else (None, res[0])


def ssd_mix(pr, dt, pr_c, dt_c, conv_w, conv_b, dt_bias, a_log, d_skip, norm_g, ctx_out, bn):
    h0 = jnp.zeros((bn, 2, SSD_STATE, W_SSD), F32)
    oc, hc = ssd_scan(pr_c, dt_c, h0, conv_w, conv_b, dt_bias, a_log, d_skip, norm_g, ctx_out)
    o, _ = ssd_scan(pr, dt, hc, conv_w, conv_b, dt_bias, a_log, d_skip, norm_g, True)
    return o, oc


def mixer(pr, dt, pr_c, dt_c, pool_w, pool_scale, conv_w, na_bias, s_conv_w, s_conv_b, dt_bias, a_log, d_skip,
          s_norm_g, ctx_out, bn):
    o_ssd, oc_ssd = ssd_mix(pr, dt, pr_c, dt_c, s_conv_w, s_conv_b, dt_bias, a_log, d_skip, s_norm_g, ctx_out, bn)
    o = [*local_mix(pr, pool_w, pool_scale, conv_w, bn), na_attention(pr, pr_c, na_bias, bn), o_ssd]
    if not ctx_out:
        return o, None
    oc = [*local_mix(pr_c, pool_w, pool_scale, conv_w, bn), ctx_attention(pr_c, bn), oc_ssd]
    return o, oc


TM = 512
TM_ROUTE = 1024
TM_PROJ = 1024
TM_GRP = 512


def kernel(x, c, ctx, c_ctx, w_ada, b_ada, g_mix, g_ffn, w_in, w_out, pool_w, pool_scale, conv_w, na_rpb,
           ssd_conv_w, ssd_conv_b, ssd_dt_bias, ssd_a_log, ssd_d, ssd_norm_g, ffn_w_gu, ffn_w_down,
           moe_router, moe_w_gu, moe_w_down, g_final):
    bn, S, d = x.shape
    Lc = ctx.shape[1]
    m, mc = bn * S, bn * Lc
    tpb = S // TM
    x2 = x.reshape(m, d)
    xc2 = ctx.reshape(mc, d)
    c_rows = jnp.concatenate([c, c_ctx[None, :], jnp.zeros((7, d), F32)], axis=0)
    for l in range(DEPTH):
        ctx_out = l < DEPTH - 1
        last = l == DEPTH - 1
        ada = ada_modulation(c_rows, w_ada[l].astype(BF16), b_ada[l][None, :])
        mod = ada[:bn].reshape(bn, 6, d)
        mod_c = ada[bn:bn + 1].reshape(1, 6, d)
        w_in_l = jnp.pad(w_in[l], ((0, 0), (0, D_IN_PAD - D_IN_PROJ))).astype(BF16)
        g_m = g_mix[l][None, :]
        g_f = g_ffn[l][None, :]
        pr, dt = in_proj(x2, g_m, mod, w_in_l, S // TM_PROJ, TM_PROJ)
        pr_c, dt_c = in_proj(xc2, g_m, mod_c, w_in_l, None, min(TM_PROJ, mc))
        na_bias = na_bias_table(na_rpb[l], S // GRID_W)
        o, oc = mixer(pr, dt, pr_c, dt_c, pool_w[l], pool_scale[l], conv_w[l], na_bias, ssd_conv_w[l],
                      ssd_conv_b[l], ssd_dt_bias[l], ssd_a_log[l], ssd_d[l], ssd_norm_g[l], ctx_out, bn)
        w_out_l = w_out[l].astype(BF16)
        j = l // 2
        if l % 2 == 0:
            w_gu = ffn_w_gu[j].astype(BF16)
            w_dn = ffn_w_down[j].astype(BF16)
            x2 = ffn_dense(x2, o, g_f, mod, w_out_l, w_gu, w_dn, tpb, TM)
            if ctx_out:
                xc2 = ffn_dense(xc2, oc, g_f, mod_c, w_out_l, w_gu, w_dn, None, min(TM, mc))
        else:
            w_r = jnp.pad(moe_router[j], ((0, 0), (0, LANES - N_EXPERTS))).astype(BF16)
            w_gu, w_dn = moe_w_gu[j], moe_w_down[j]
            x2 = moe_block(x2, o, g_f, mod, w_out_l, w_r, w_gu, w_dn, S // TM_ROUTE, TM_ROUTE, TM_GRP,
                           g_final[None, :] if last else None)
            if ctx_out:
                xc2 = moe_block(xc2, oc, g_f, mod_c, w_out_l, w_r, w_gu, w_dn, None, min(TM_ROUTE, mc), TM_GRP)
            if last:
                return x2.reshape(bn, S, d)
    return final_norm(x2, g_final[None, :], TM).reshape(bn, S, d)
```
